```python
import jax, jax.numpy as jnp
from jax import lax
import numpy as np

D_MODEL = 1024
BATCH = 16
SEQ = 256
DEPTH = 4
DEC_BATCH = 2
DEC_SEQ = 2048
PAST_LEN = 256

GRID_W = 64
N_MIXERS = 4
N_A = (DEPTH + 3) // 4
N_B = (DEPTH + 2) // 4
N_C = (DEPTH + 1) // 4
N_D = DEPTH // 4
H_A = 8
KV_A = 2
HD_A = 128
ROPE_THETA = 10000.0
Q_BLOCK = 128
H_B = 8
DK_B = 128
DV_B = 128
GLA_CHUNK = 32
H_C = 8
DQK_C = 64
DV_C = 128
MLSTM_CHUNK = 64
GATE_SOFTCAP = 15.0
H_D = 16
HD_D = 64
NA_ROWS = 8
NA_COLS = 16
N_EXPERTS = 32
TOP_K = 4
D_FF = 1024
SWIGLU_ALPHA = 1.702
SWIGLU_LIMIT = 7.0
MOE_BLOCK = 128
NORM_EPS = 1e-6
W_IN_A = (H_A + 2 * KV_A) * HD_A
W_IN_B = 5 * H_B * DK_B
W_IN_C = 2 * H_C * DQK_C + 2 * H_C * DV_C + 4 * H_C
W_IN_D = 3 * H_D * HD_D
F32 = jnp.float32

kernel_name = 'hybrid_diffusion_prefix_ctx_step'


def rms_norm(x, g):
    xf = x.astype(F32)
    y = xf * lax.rsqrt(jnp.mean(xf * xf, axis=-1, keepdims=True) + NORM_EPS)
    return y.astype(x.dtype) * g


def flip(a):
    return jnp.flip(a, axis=1)


def ada_mod(cond, w_mod, b_mod):
    m = jax.nn.silu(cond) @ w_mod + b_mod
    return jnp.split(m, 6, axis=-1)


def block_attention(q, k, v):
    b, t, h, hd = q.shape
    g = k.shape[2]
    rep = h // g
    nb = t // Q_BLOCK
    qb = jnp.moveaxis(q.reshape(b, nb, Q_BLOCK, g, rep, hd), 1, 0)
    scale = hd ** -0.5

    def one_block(qblk):
        s = jnp.einsum('bqgrd,bkgd->bgrqk', qblk, k).astype(F32) * scale
        p = jax.nn.softmax(s, axis=-1).astype(v.dtype)
        return jnp.einsum('bgrqk,bkgd->bqgrd', p, v)

    ob = lax.map(one_block, qb)
    return jnp.moveaxis(ob, 0, 1).reshape(b, t, h, hd)


def axial_rope_angles(n_tok):
    pos = jnp.arange(n_tok)
    row = (pos // GRID_W).astype(F32)
    col = (pos % GRID_W).astype(F32)
    n_freq = HD_A // 4
    inv_freq = ROPE_THETA ** (-jnp.arange(n_freq, dtype=F32) / n_freq)
    return jnp.concatenate([row[:, None] * inv_freq, col[:, None] * inv_freq], axis=-1)


def apply_rope(x, ang):
    half = x.shape[-1] // 2
    cos = jnp.cos(ang)[None, :, None, :]
    sin = jnp.sin(ang)[None, :, None, :]
    xf = x.astype(F32)
    x1, x2 = xf[..., :half], xf[..., half:]
    return jnp.concatenate([x1 * cos - x2 * sin, x2 * cos + x1 * sin], axis=-1).astype(x.dtype)


def mixer_gqa(h, ctx, w_in, qn_g, kn_g, w_out):
    b, t, _ = h.shape
    q, k, v = jnp.split(h @ w_in, [H_A * HD_A, (H_A + KV_A) * HD_A], axis=-1)
    q = rms_norm(q.reshape(b, t, H_A, HD_A), qn_g)
    k = rms_norm(k.reshape(b, t, KV_A, HD_A), kn_g)
    v = v.reshape(b, t, KV_A, HD_A)
    if ctx is None:
        o = block_attention(q, k, v)
        new = (k, v)
    else:
        k_ctx, v_ctx = ctx
        ang = axial_rope_angles(t)
        q = apply_rope(q, ang)
        k = apply_rope(k, ang)
        o = block_attention(q, jnp.concatenate([k_ctx.astype(k.dtype), k], axis=1),
                            jnp.concatenate([v_ctx.astype(v.dtype), v], axis=1))
        new = None
    return o.reshape(b, t, H_A * HD_A) @ w_out, new


def chunk_gla(q, k, v, log_f, s0):
    b, t, h, dk = q.shape
    dv = v.shape[-1]
    L = GLA_CHUNK
    nc = t // L

    def chunks(a):
        return a.reshape(b, nc, L, h, a.shape[-1]).transpose(1, 0, 3, 2, 4)

    causal = jnp.tril(jnp.ones((L, L), bool))[..., None]

    def step(S, inp):
        qi, ki, vi, gi = inp
        cum = jnp.cumsum(gi, axis=2)
        o_inter = jnp.einsum('bhtk,bhkv->bhtv', qi * jnp.exp(cum), S)
        decay = jnp.exp(jnp.where(causal, cum[:, :, :, None, :] - cum[:, :, None, :, :], -jnp.inf))
        attn = jnp.einsum('bhtk,bhsk,bhtsk->bhts', qi, ki, decay)
        o = o_inter + jnp.einsum('bhts,bhsv->bhtv', attn, vi)
        end = cum[:, :, -1:, :]
        S_new = jnp.exp(end[:, :, 0, :, None]) * S + jnp.einsum('bhsk,bhsv->bhkv', ki * jnp.exp(end - cum), vi)
        return S_new, o

    S, oc = lax.scan(step, s0, (chunks(q), chunks(k), chunks(v), chunks(log_f)))
    return oc.transpose(1, 0, 3, 2, 4).reshape(b, t, h, dv), S


def mixer_hgrn2(h, ctx, lb, w_in, b_f, onorm_g, w_out):
    b, t, _ = h.shape
    q, i_val, f_fw, f_bw, g = jnp.split(h @ w_in, 5, axis=-1)
    q = (jax.nn.silu(q.astype(F32)) * DK_B ** -0.5).reshape(b, t, H_B, DK_B)
    v = i_val.astype(F32).reshape(b, t, H_B, DV_B)
    f_pre = jnp.stack([f_fw, f_bw]).astype(F32) + b_f.astype(F32)[:, None, None, :]
    lbv = lb[:, None, None, :]
    f = lbv + (1.0 - lbv) * jax.nn.sigmoid(f_pre)
    log_f = jnp.log(f).reshape(2, b, t, H_B, DK_B)
    k = (1.0 - f).reshape(2, b, t, H_B, DK_B)
    if ctx is None:
        s0 = jnp.zeros((b, 2, H_B, DK_B, DV_B), F32)
    else:
        s0 = ctx[0].astype(F32)
    o_fw, s_fw = chunk_gla(q, k[0], v, log_f[0], s0[:, 0])
    o_bw, s_bw = chunk_gla(flip(q), flip(k[1]), flip(v), flip(log_f[1]), s0[:, 1])
    o = rms_norm(o_fw + flip(o_bw), onorm_g).reshape(b, t, H_B * DV_B).astype(h.dtype)
    out = (o * jax.nn.silu(g)) @ w_out
    new = (jnp.stack([s_fw, s_bw], axis=1),) if ctx is None else None
    return out, new


def chunk_mlstm(q, k, v, i_pre, log_f, C0, n0, m0):
    b, t, h, dqk = q.shape
    dv = v.shape[-1]
    L = MLSTM_CHUNK
    nc = t // L

    def heads_chunks(a):
        return a.reshape(b, nc, L, h, a.shape[-1]).transpose(1, 0, 3, 2, 4)

    def gate_chunks(a):
        return a.reshape(b, nc, L, h).transpose(1, 0, 3, 2)

    causal = jnp.tril(jnp.ones((L, L), bool))

    def step(carry, inp):
        C, n, m = carry
        qi, ki, vi, ii, fi = inp
        cum = jnp.cumsum(fi, axis=-1)
        d = jnp.where(causal, cum[..., :, None] - cum[..., None, :] + ii[..., None, :], -jnp.inf)
        m_inter = cum + m[..., None]
        m_t = jnp.maximum(m_inter, jnp.max(d, axis=-1))
        w_inter = jnp.exp(m_inter - m_t)
        w_intra = jnp.exp(d - m_t[..., None])
        qk = jnp.einsum('bhtd,bhsd->bhts', qi, ki) * w_intra
        num = w_inter[..., None] * jnp.einsum('bhtd,bhdv->bhtv', qi, C) + jnp.einsum('bhts,bhsv->bhtv', qk, vi)
        den = w_inter * jnp.einsum('bhtd,bhd->bht', qi, n) + jnp.sum(qk, axis=-1)
        h_out = num / jnp.maximum(jnp.abs(den), jnp.exp(-m_t))[..., None]
        g_end = cum[..., -1:] - cum + ii
        m_new = jnp.maximum(cum[..., -1] + m, jnp.max(g_end, axis=-1))
        w_old = jnp.exp(cum[..., -1] + m - m_new)
        w_s = jnp.exp(g_end - m_new[..., None])
        C_new = w_old[..., None, None] * C + jnp.einsum('bhs,bhsd,bhsv->bhdv', w_s, ki, vi)
        n_new = w_old[..., None] * n + jnp.einsum('bhs,bhsd->bhd', w_s, ki)
        return (C_new, n_new, m_new), h_out

    (C, n, m), hc = lax.scan(step, (C0, n0, m0), (heads_chunks(q), heads_chunks(k), heads_chunks(v),
                                                   gate_chunks(i_pre), gate_chunks(log_f)))
    return hc.transpose(1, 0, 3, 2, 4).reshape(b, t, h, dv), (C, n, m)


def mixer_mlstm(h, ctx, w_in, b_gates, onorm_g, w_out):
    b, t, _ = h.shape
    nq = H_C * DQK_C
    nv = H_C * DV_C
    q, k, v, og, gates = jnp.split(h @ w_in, [nq, 2 * nq, 2 * nq + nv, 2 * nq + 2 * nv], axis=-1)
    q = q.astype(F32).reshape(b, t, H_C, DQK_C) * DQK_C ** -0.5
    k = k.astype(F32).reshape(b, t, H_C, DQK_C)
    v = v.astype(F32).reshape(b, t, H_C, DV_C)
    gates = gates.astype(F32).reshape(b, t, 4, H_C) + b_gates.astype(F32)
    gates = GATE_SOFTCAP * jnp.tanh(gates / GATE_SOFTCAP)
    i_fw, f_fw, i_bw, f_bw = gates[:, :, 0], gates[:, :, 1], gates[:, :, 2], gates[:, :, 3]
    if ctx is None:
        C0 = jnp.zeros((b, 2, H_C, DQK_C, DV_C), F32)
        n0 = jnp.zeros((b, 2, H_C, DQK_C), F32)
        m0 = jnp.zeros((b, 2, H_C), F32)
    else:
        C0, n0, m0 = ctx[0].astype(F32), ctx[1].astype(F32), ctx[2].astype(F32)
    h_fw, (C_fw, n_fw, m_fw) = chunk_mlstm(q, k, v, i_fw, jax.nn.log_sigmoid(f_fw), C0[:, 0], n0[:, 0], m0[:, 0])
    h_bw, (C_bw, n_bw, m_bw) = chunk_mlstm(flip(q), flip(k), flip(v), flip(i_bw), flip(jax.nn.log_sigmoid(f_bw)),
                                           C0[:, 1], n0[:, 1], m0[:, 1])
    hs = rms_norm(h_fw + flip(h_bw), onorm_g).reshape(b, t, nv)
    out = (jax.nn.sigmoid(og.astype(F32)) * hs).astype(h.dtype) @ w_out
    if ctx is None:
        new = (jnp.stack([C_fw, C_bw], axis=1), jnp.stack([n_fw, n_bw], axis=1), jnp.stack([m_fw, m_bw], axis=1))
    else:
        new = None
    return out, new


def neighborhood_attention(q, k, v, k_ctx, v_ctx, rpb):
    b, t, h, hd = q.shape
    rows = t // GRID_W
    kr = min(NA_ROWS, rows)
    qg = q.reshape(b, rows, GRID_W, h, hd)
    kg = k.reshape(b, rows, GRID_W, h, hd)
    vg = v.reshape(b, rows, GRID_W, h, hd)
    col = jnp.arange(GRID_W)
    col_start = jnp.clip(col - NA_COLS // 2, 0, GRID_W - NA_COLS)
    col_idx = col_start[:, None] + jnp.arange(NA_COLS)[None, :]
    col_bias_idx = col_idx - col[:, None] + (NA_COLS - 1)
    scale = hd ** -0.5
    n_loc = kr * NA_COLS

    def one_row(r):
        r_start = jnp.clip(r - NA_ROWS // 2, 0, rows - kr)
        k_win = lax.dynamic_slice_in_dim(kg, r_start, kr, axis=1)[:, :, col_idx]
        v_win = lax.dynamic_slice_in_dim(vg, r_start, kr, axis=1)[:, :, col_idx]
        q_row = lax.dynamic_index_in_dim(qg, r, axis=1, keepdims=False)
        row_bias_idx = r_start + jnp.arange(kr) - r + (NA_ROWS - 1)
        bias = rpb[:, row_bias_idx][:, :, col_bias_idx].transpose(0, 2, 1, 3)
        s_loc = jnp.einsum('bwhd,bawchd->bhwac', q_row, k_win).astype(F32) * scale + bias[None].astype(F32)
        s_ctx = jnp.einsum('bwhd,bkhd->bhwk', q_row, k_ctx).astype(F32) * scale
        s = jnp.concatenate([s_loc.reshape(b, h, GRID_W, n_loc), s_ctx], axis=-1)
        p = jax.nn.softmax(s, axis=-1).astype(v.dtype)
        p_loc = p[..., :n_loc].reshape(b, h, GRID_W, kr, NA_COLS)
        p_ctx = p[..., n_loc:]
        return jnp.einsum('bhwac,bawchd->bwhd', p_loc, v_win) + jnp.einsum('bhwk,bkhd->bwhd', p_ctx, v_ctx)

    out_rows = lax.map(one_row, jnp.arange(rows))
    return jnp.moveaxis(out_rows, 0, 1).reshape(b, t, h, hd)


def mixer_na(h, ctx, w_in, rpb, w_out):
    b, t, _ = h.shape
    q, k, v = jnp.split(h @ w_in, 3, axis=-1)
    q = q.reshape(b, t, H_D, HD_D)
    k = k.reshape(b, t, H_D, HD_D)
    v = v.reshape(b, t, H_D, HD_D)
    if ctx is None:
        o = block_attention(q, k, v)
        new = (k, v)
    else:
        o = neighborhood_attention(q, k, v, ctx[0].astype(k.dtype), ctx[1].astype(v.dtype), rpb)
        new = None
    return o.reshape(b, t, H_D * HD_D) @ w_out, new


def moe(xn, w_router, b_router, w_gu, b_gu, w_dn, b_dn):
    n_tok, d = xn.shape
    logits = (xn @ w_router + b_router).astype(F32)
    top_val, top_idx = lax.top_k(logits, TOP_K)
    gates = jax.nn.softmax(top_val, axis=-1)
    n_assign = n_tok * TOP_K
    flat_e = top_idx.reshape(-1)
    order = jnp.argsort(flat_e)
    e_sorted = flat_e[order]
    tok_sorted = (order // TOP_K).astype(jnp.int32)
    gate_sorted = gates.reshape(-1)[order]
    counts = jnp.zeros((N_EXPERTS,), jnp.int32).at[flat_e].add(1)
    padded = (counts + MOE_BLOCK - 1) // MOE_BLOCK * MOE_BLOCK
    pad_end = jnp.cumsum(padded)
    pad_start = pad_end - padded
    grp_start = jnp.cumsum(counts) - counts
    dest = pad_start[e_sorted] + jnp.arange(n_assign) - grp_start[e_sorted]
    n_blocks = n_assign // MOE_BLOCK + N_EXPERTS
    n_slots = n_blocks * MOE_BLOCK
    slot_tok = jnp.full((n_slots,), n_tok, jnp.int32).at[dest].set(tok_sorted)
    slot_gate = jnp.zeros((n_slots,), F32).at[dest].set(gate_sorted)
    block_e = jnp.minimum(jnp.searchsorted(pad_end, jnp.arange(n_blocks) * MOE_BLOCK, side='right'), N_EXPERTS - 1)
    x_pad = jnp.concatenate([xn, jnp.zeros((1, d), xn.dtype)], axis=0)
    xb = x_pad[slot_tok].reshape(n_blocks, MOE_BLOCK, d)

    def expert_block(args):
        xblk, e = args
        gu = xblk @ w_gu[e] + b_gu[e]
        x_glu, x_lin = jnp.split(gu, 2, axis=-1)
        x_glu = jnp.minimum(x_glu, SWIGLU_LIMIT)
        x_lin = jnp.clip(x_lin, -SWIGLU_LIMIT, SWIGLU_LIMIT)
        hid = x_glu * jax.nn.sigmoid(SWIGLU_ALPHA * x_glu) * (x_lin + 1.0)
        return hid @ w_dn[e] + b_dn[e]

    yb = lax.map(expert_block, (xb, block_e)).reshape(n_slots, d)
    y = jnp.zeros((n_tok + 1, d), F32).at[slot_tok].add(yb.astype(F32) * slot_gate[:, None])
    return y[:n_tok].astype(xn.dtype)


def _layer(i, x, cond, ctx, W):
    kind = i % N_MIXERS
    j = i // N_MIXERS
    sh1, sc1, g1, sh2, sc2, g2 = ada_mod(cond, W['w_mod'][i], W['b_mod'][i])
    h = rms_norm(x, W['norm1_g'][i]) * (1.0 + sc1) + sh1
    if kind == 0:
        out, new = mixer_gqa(h, ctx, W['w_in_a'][j], W['qnorm_a'][j], W['knorm_a'][j], W['w_out_a'][j])
    elif kind == 1:
        out, new = mixer_hgrn2(h, ctx, W['lb_b'][i], W['w_in_b'][j], W['b_f_b'][j], W['onorm_b'][j], W['w_out_b'][j])
    elif kind == 2:
        out, new = mixer_mlstm(h, ctx, W['w_in_c'][j], W['b_gates_c'][j], W['onorm_c'][j], W['w_out_c'][j])
    else:
        out, new = mixer_na(h, ctx, W['w_in_d'][j], W['rpb_d'][j], W['w_out_d'][j])
    x = x + g1 * out
    h = rms_norm(x, W['norm2_g'][i]) * (1.0 + sc2) + sh2
    b, t, d = h.shape
    y = moe(h.reshape(b * t, d), W['w_router'][i], W['b_router'][i], W['w_gu'][i], W['b_gu'][i],
            W['w_dn'][i], W['b_dn'][i]).reshape(b, t, d)
    return x + g2 * y, new


def setup_inputs(seed: int = 0) -> dict:
    key = jax.random.key(seed)
    ks = jax.random.split(key, 40)

    def nrm(idx, shape, scale):
        return jax.random.normal(ks[idx], shape, jnp.float32) * scale

    D = D_MODEL
    sd = D ** -0.5
    f_bias = jnp.array([0.0, 1.0, 0.0, 1.0], jnp.float32)[:, None] * jnp.linspace(3.0, 6.0, H_C, dtype=jnp.float32)[None, :]
    return {
        'x_prompt': nrm(0, (BATCH, SEQ, D), 1.0),
        'x_sample': nrm(1, (DEC_BATCH, DEC_SEQ, D), 1.0),
        'cache_k_a': nrm(2, (DEC_BATCH, N_A, PAST_LEN, KV_A, HD_A), 1.0),
        'cache_v_a': nrm(3, (DEC_BATCH, N_A, PAST_LEN, KV_A, HD_A), 1.0),
        'state_b': nrm(4, (DEC_BATCH, N_B, 2, H_B, DK_B, DV_B), 0.5),
        'state_c_C': nrm(5, (DEC_BATCH, N_C, 2, H_C, DQK_C, DV_C), 0.5),
        'state_c_n': nrm(6, (DEC_BATCH, N_C, 2, H_C, DQK_C), 0.5),
        'state_c_m': nrm(7, (DEC_BATCH, N_C, 2, H_C), 1.0),
        'cache_k_d': nrm(8, (DEC_BATCH, N_D, PAST_LEN, H_D, HD_D), 1.0),
        'cache_v_d': nrm(9, (DEC_BATCH, N_D, PAST_LEN, H_D, HD_D), 1.0),
        'c': nrm(10, (DEC_BATCH, D), 1.0),
        'c_ctx': nrm(11, (D,), 1.0),
        'norm1_g': 1.0 + nrm(12, (DEPTH, D), 0.02),
        'norm2_g': 1.0 + nrm(13, (DEPTH, D), 0.02),
        'w_mod': nrm(14, (DEPTH, D, 6 * D), 0.5 * sd),
        'b_mod': nrm(15, (DEPTH, 6 * D), 0.01),
        'w_in_a': nrm(16, (N_A, D, W_IN_A), sd),
        'qnorm_a': 1.0 + nrm(17, (N_A, HD_A), 0.02),
        'knorm_a': 1.0 + nrm(18, (N_A, HD_A), 0.02),
        'w_out_a': nrm(19, (N_A, H_A * HD_A, D), (H_A * HD_A) ** -0.5),
        'w_in_b': nrm(20, (N_B, D, W_IN_B), sd),
        'b_f_b': nrm(21, (N_B, 2, H_B * DK_B), 0.1),
        'lower_bounds_b': nrm(22, (DEPTH, 2, H_B * DK_B), 1.0),
        'onorm_b': 1.0 + nrm(23, (N_B, DV_B), 0.02),
        'w_out_b': nrm(24, (N_B, H_B * DV_B, D), (H_B * DV_B) ** -0.5),
        'w_in_c': nrm(25, (N_C, D, W_IN_C), sd),
        'b_gates_c': nrm(26, (N_C, 4, H_C), 0.1) + f_bias,
        'onorm_c': 1.0 + nrm(27, (N_C, DV_C), 0.02),
        'w_out_c': nrm(28, (N_C, H_C * DV_C, D), (H_C * DV_C) ** -0.5),
        'w_in_d': nrm(29, (N_D, D, W_IN_D), sd),
        'rpb_d': nrm(30, (N_D, H_D, 2 * NA_ROWS - 1, 2 * NA_COLS - 1), 0.1),
        'w_out_d': nrm(31, (N_D, H_D * HD_D, D), (H_D * HD_D) ** -0.5),
        'w_router': nrm(32, (DEPTH, D, N_EXPERTS), sd),
        'b_router': nrm(33, (DEPTH, N_EXPERTS), 0.01),
        'w_gu': nrm(34, (DEPTH, N_EXPERTS, D, 2 * D_FF), sd),
        'b_gu': nrm(35, (DEPTH, N_EXPERTS, 2 * D_FF), 0.01),
        'w_dn': nrm(36, (DEPTH, N_EXPERTS, D_FF, D), D_FF ** -0.5),
        'b_dn': nrm(37, (DEPTH, N_EXPERTS, D), 0.01),
        'final_g': 1.0 + nrm(38, (D,), 0.02),
    }


def reference(x_prompt, x_sample, cache_k_a, cache_v_a, state_b, state_c_C, state_c_n, state_c_m,
              cache_k_d, cache_v_d, c, c_ctx, norm1_g, norm2_g, w_mod, b_mod,
              w_in_a, qnorm_a, knorm_a, w_out_a, w_in_b, b_f_b, lower_bounds_b, onorm_b, w_out_b,
              w_in_c, b_gates_c, onorm_c, w_out_c, w_in_d, rpb_d, w_out_d,
              w_router, b_router, w_gu, b_gu, w_dn, b_dn, final_g):
    lb_cum = jnp.cumsum(jax.nn.softmax(lower_bounds_b.astype(F32), axis=0), axis=0)
    lb_b = lb_cum - lb_cum[0]
    W = {'w_mod': w_mod, 'b_mod': b_mod, 'norm1_g': norm1_g, 'norm2_g': norm2_g,
         'w_in_a': w_in_a, 'qnorm_a': qnorm_a, 'knorm_a': knorm_a, 'w_out_a': w_out_a,
         'lb_b': lb_b, 'w_in_b': w_in_b, 'b_f_b': b_f_b, 'onorm_b': onorm_b, 'w_out_b': w_out_b,
         'w_in_c': w_in_c, 'b_gates_c': b_gates_c, 'onorm_c': onorm_c, 'w_out_c': w_out_c,
         'w_in_d': w_in_d, 'rpb_d': rpb_d, 'w_out_d': w_out_d,
         'w_router': w_router, 'b_router': b_router, 'w_gu': w_gu, 'b_gu': b_gu, 'w_dn': w_dn, 'b_dn': b_dn}

    x = x_prompt
    ctx_out = ([], [], [], [])
    for i in range(DEPTH):
        x, new = _layer(i, x, c_ctx, None, W)
        ctx_out[i % N_MIXERS].append(new)
    y_prompt = rms_norm(x, final_g)
    new_k_a = jnp.stack([s[0] for s in ctx_out[0]], axis=1)
    new_v_a = jnp.stack([s[1] for s in ctx_out[0]], axis=1)
    new_state_b = jnp.stack([s[0] for s in ctx_out[1]], axis=1)
    new_state_c_C = jnp.stack([s[0] for s in ctx_out[2]], axis=1)
    new_state_c_n = jnp.stack([s[1] for s in ctx_out[2]], axis=1)
    new_state_c_m = jnp.stack([s[2] for s in ctx_out[2]], axis=1)
    new_k_d = jnp.stack([s[0] for s in ctx_out[3]], axis=1)
    new_v_d = jnp.stack([s[1] for s in ctx_out[3]], axis=1)

    x = x_sample
    cond = c[:, None, :]
    for i in range(DEPTH):
        j = i // N_MIXERS
        kind = i % N_MIXERS
        if kind == 0:
            ctx = (cache_k_a[:, j], cache_v_a[:, j])
        elif kind == 1:
            ctx = (state_b[:, j],)
        elif kind == 2:
            ctx = (state_c_C[:, j], state_c_n[:, j], state_c_m[:, j])
        else:
            ctx = (cache_k_d[:, j], cache_v_d[:, j])
        x, _ = _layer(i, x, cond, ctx, W)
    y_sample = rms_norm(x, final_g)
    return (y_prompt, y_sample, new_k_a, new_v_a, new_state_b, new_state_c_C, new_state_c_n, new_state_c_m, new_k_d, new_v_d)
```

```python
import functools

import numpy as np
import jax
import jax.numpy as jnp
from jax import lax
from jax.experimental import pallas as pl
from jax.experimental.pallas import tpu as pltpu

F32 = jnp.float32
BF16 = jnp.bfloat16
I32 = jnp.int32

NORM_EPS = 1e-6
GRID_W = 64
ROPE_THETA = 10000.0
TOP_K = 4
GLA_CHUNK = 32
GATE_SOFTCAP = 15.0
NA_ROWS = 8
NA_COLS = 16
SWIGLU_ALPHA = 1.702
SWIGLU_LIMIT = 7.0
NEG_BIG = -1e30

TOKEN_TILE = 256
MOE_BLOCK = 256
V7X_VMEM_LIMIT = 52 * 1024 * 1024


def _params(sem, vmem=V7X_VMEM_LIMIT):
    return pltpu.CompilerParams(dimension_semantics=sem, vmem_limit_bytes=vmem)


def _bdot(a, b):
    return jnp.dot(a.astype(BF16), b.astype(BF16), preferred_element_type=F32)


def _bdot_nt(a, b):
    return lax.dot_general(a.astype(BF16), b.astype(BF16), (((1,), (1,)), ((), ())),
                           preferred_element_type=F32)


def _bdot_tn(a, b):
    return lax.dot_general(a.astype(BF16), b.astype(BF16), (((0,), (0,)), ((), ())),
                           preferred_element_type=F32)


def _fdot(a, b):
    return jnp.dot(a, b, preferred_element_type=F32, precision=lax.Precision.HIGHEST)


def _rms(x, g):
    return x * lax.rsqrt(jnp.mean(x * x, axis=-1, keepdims=True) + NORM_EPS) * g


def _rms_heads(x, g, n_heads, hd):
    return jnp.concatenate([_rms(x[:, h * hd:(h + 1) * hd], g) for h in range(n_heads)], axis=-1)


def _sigmoid(x):
    return 1.0 / (1.0 + jnp.exp(-x))


def _silu(x):
    return x * _sigmoid(x)


def _softmax_rows(parts):
    m = parts[0].max(axis=-1, keepdims=True)
    for p in parts[1:]:
        m = jnp.maximum(m, p.max(axis=-1, keepdims=True))
    es = [jnp.exp(p - m) for p in parts]
    den = es[0].sum(axis=-1, keepdims=True)
    for e in es[1:]:
        den = den + e.sum(axis=-1, keepdims=True)
    return es, den


def _mod_kernel(c_ref, w_ref, b_ref, o_ref):
    o_ref[0] = _bdot(_silu(c_ref[...]), w_ref[0]) + b_ref[0]


def _mod_call(cond8, w_mod, b_mod):
    depth, d, d6 = w_mod.shape
    tn = 1024
    return pl.pallas_call(
        _mod_kernel,
        grid=(depth, d6 // tn),
        in_specs=[pl.BlockSpec((8, d), lambda i, j: (0, 0)),
                  pl.BlockSpec((1, d, tn), lambda i, j: (i, 0, j)),
                  pl.BlockSpec((1, 1, tn), lambda i, j: (i, 0, j))],
        out_specs=pl.BlockSpec((1, 8, tn), lambda i, j: (i, 0, j)),
        out_shape=jax.ShapeDtypeStruct((depth, 8, d6), F32),
        compiler_params=_params(("arbitrary", "arbitrary")),
        name="adaln_mod",
    )(cond8, w_mod, b_mod.reshape(depth, 1, d6))


def _inproj_kernel(x_ref, m_ref, g_ref, w_ref, o_ref):
    h = _rms(x_ref[...], g_ref[...]) * (1.0 + m_ref[0, 1:2, :]) + m_ref[0, 0:1, :]
    o_ref[...] = _bdot(h, w_ref[...])


def _inproj_call(x, modsel, g, w):
    n, d = x.shape
    wout = w.shape[1]
    tm = TOKEN_TILE
    return pl.pallas_call(
        _inproj_kernel,
        grid=(n // tm,),
        in_specs=[pl.BlockSpec((tm, d), lambda i: (i, 0)),
                  pl.BlockSpec((1, 6, d), lambda i: (i, 0, 0)),
                  pl.BlockSpec((1, d), lambda i: (0, 0)),
                  pl.BlockSpec((d, wout), lambda i: (0, 0))],
        out_specs=pl.BlockSpec((tm, wout), lambda i: (i, 0)),
        out_shape=jax.ShapeDtypeStruct((n, wout), F32),
        compiler_params=_params(("arbitrary",)),
        name="inproj",
    )(x, modsel, g.reshape(1, d), w)


def _outproj_plain_kernel(o_ref, w_ref, x_ref, m_ref, y_ref):
    y_ref[...] = x_ref[...] + m_ref[0, 2:3, :] * _bdot(o_ref[...], w_ref[...])


def _outproj_hgrn_kernel(of_ref, ob_ref, gate_ref, on_ref, w_ref, x_ref, m_ref, y_ref, *, n_heads, hd):
    o = _rms_heads(of_ref[...] + ob_ref[...], on_ref[...], n_heads, hd)
    o = o * _silu(gate_ref[...])
    y_ref[...] = x_ref[...] + m_ref[0, 2:3, :] * _bdot(o, w_ref[...])


def _outproj_mlstm_kernel(of_ref, ob_ref, gate_ref, on_ref, w_ref, x_ref, m_ref, y_ref, *, n_heads, hd):
    hs = _rms_heads(of_ref[...] + ob_ref[...], on_ref[...], n_heads, hd)
    o = _sigmoid(gate_ref[...]) * hs
    y_ref[...] = x_ref[...] + m_ref[0, 2:3, :] * _bdot(o, w_ref[...])


def _outproj_call(kind, mix_ins, w, x, modsel):
    n, d = x.shape
    tm = TOKEN_TILE
    specs, args = [], []
    for arr, width, cb in mix_ins:
        if arr.shape[0] == n:
            specs.append(pl.BlockSpec((tm, width), functools.partial(lambda i, cb: (i, cb), cb=cb)))
        else:
            specs.append(pl.BlockSpec(arr.shape, lambda i: (0, 0)))
        args.append(arr)
    specs += [pl.BlockSpec(w.shape, lambda i: (0, 0)),
              pl.BlockSpec((tm, d), lambda i: (i, 0)),
              pl.BlockSpec((1, 6, d), lambda i: (i, 0, 0))]
    args += [w, x, modsel]
    if kind == "plain":
        body = _outproj_plain_kernel
    elif kind == "hgrn":
        body = functools.partial(_outproj_hgrn_kernel, n_heads=8, hd=128)
    else:
        body = functools.partial(_outproj_mlstm_kernel, n_heads=8, hd=128)
    return pl.pallas_call(
        body,
        grid=(n // tm,),
        in_specs=specs,
        out_specs=pl.BlockSpec((tm, d), lambda i: (i, 0)),
        out_shape=jax.ShapeDtypeStruct((n, d), F32),
        compiler_params=_params(("arbitrary",)),
        name="outproj_" + kind,
    )(*args)


def _gqa_ctx_kernel(p_ref, qn_ref, kn_ref, o_ref, k_ref, *, n_heads, n_kv, hd):
    rep = n_heads // n_kv
    scale = hd ** -0.5
    koff = n_heads * hd
    voff = koff + n_kv * hd
    ks = [_rms(p_ref[:, koff + g * hd: koff + (g + 1) * hd], kn_ref[...]) for g in range(n_kv)]
    k_ref[...] = jnp.concatenate(ks, axis=-1)
    for h in range(n_heads):
        g = h // rep
        q = _rms(p_ref[:, h * hd:(h + 1) * hd], qn_ref[...])
        s = _bdot_nt(q, ks[g]) * scale
        (e,), den = _softmax_rows([s])
        o_ref[:, h * hd:(h + 1) * hd] = _bdot(e / den, p_ref[:, voff + g * hd: voff + (g + 1) * hd])


def _gqa_ctx_call(proj, qn, kn, n_seq, t):
    n_heads, n_kv, hd = 8, 2, 128
    win = proj.shape[1]
    body = functools.partial(_gqa_ctx_kernel, n_heads=n_heads, n_kv=n_kv, hd=hd)
    return pl.pallas_call(
        body,
        grid=(n_seq,),
        in_specs=[pl.BlockSpec((t, win), lambda b: (b, 0)),
                  pl.BlockSpec((1, hd), lambda b: (0, 0)),
                  pl.BlockSpec((1, hd), lambda b: (0, 0))],
        out_specs=[pl.BlockSpec((t, n_heads * hd), lambda b: (b, 0)),
                   pl.BlockSpec((t, n_kv * hd), lambda b: (b, 0))],
        out_shape=[jax.ShapeDtypeStruct((n_seq * t, n_heads * hd), F32),
                   jax.ShapeDtypeStruct((n_seq * t, n_kv * hd), F32)],
        compiler_params=_params(("arbitrary",)),
        name="gqa_ctx",
    )(proj, qn.reshape(1, hd), kn.reshape(1, hd))


def _rope(x, cosd, sind):
    return x * cosd + pltpu.roll(x, x.shape[-1] // 2, 1) * sind


def _gqa_lat_kernel(pq_ref, pkv_ref, kc_ref, vc_ref, cq_ref, sq_ref, ck_ref, sk_ref, qn_ref, kn_ref,
                    o_ref, k_scr, v_scr, *, n_heads, n_kv, hd, t_ctx):
    rep = n_heads // n_kv
    scale = hd ** -0.5

    @pl.when(pl.program_id(1) == 0)
    def _():
        k_scr[0:t_ctx, :] = kc_ref[0].astype(BF16)
        v_scr[0:t_ctx, :] = vc_ref[0].astype(BF16)
        for g in range(n_kv):
            k = _rms(pkv_ref[:, g * hd:(g + 1) * hd], kn_ref[...])
            k_scr[t_ctx:, g * hd:(g + 1) * hd] = _rope(k, ck_ref[...], sk_ref[...]).astype(BF16)
        v_scr[t_ctx:, :] = pkv_ref[:, n_kv * hd:].astype(BF16)

    for h in range(n_heads):
        g = h // rep
        q = _rope(_rms(pq_ref[:, h * hd:(h + 1) * hd], qn_ref[...]), cq_ref[...], sq_ref[...])
        s = _bdot_nt(q, k_scr[:, g * hd:(g + 1) * hd]) * scale
        (e,), den = _softmax_rows([s])
        o_ref[:, h * hd:(h + 1) * hd] = _bdot(e / den, v_scr[:, g * hd:(g + 1) * hd])


def _gqa_lat_call(proj, row0, n_seq, t, cache_k, cache_v, cosd, sind, qn, kn):
    n_heads, n_kv, hd = 8, 2, 128
    tq = 256
    t_ctx = cache_k.shape[1]
    nq = t // tq
    kvw = 2 * n_kv * hd
    qblk0 = row0 // tq
    sblk0 = row0 // t
    body = functools.partial(_gqa_lat_kernel, n_heads=n_heads, n_kv=n_kv, hd=hd, t_ctx=t_ctx)
    return pl.pallas_call(
        body,
        grid=(n_seq, nq),
        in_specs=[pl.BlockSpec((tq, n_heads * hd), lambda b, i: (qblk0 + b * nq + i, 0)),
                  pl.BlockSpec((t, kvw), lambda b, i: (sblk0 + b, (n_heads * hd) // kvw)),
                  pl.BlockSpec((1, t_ctx, n_kv * hd), lambda b, i: (b, 0, 0)),
                  pl.BlockSpec((1, t_ctx, n_kv * hd), lambda b, i: (b, 0, 0)),
                  pl.BlockSpec((tq, hd), lambda b, i: (i, 0)),
                  pl.BlockSpec((tq, hd), lambda b, i: (i, 0)),
                  pl.BlockSpec((t, hd), lambda b, i: (0, 0)),
                  pl.BlockSpec((t, hd), lambda b, i: (0, 0)),
                  pl.BlockSpec((1, hd), lambda b, i: (0, 0)),
                  pl.BlockSpec((1, hd), lambda b, i: (0, 0))],
        out_specs=pl.BlockSpec((tq, n_heads * hd), lambda b, i: (b * nq + i, 0)),
        out_shape=jax.ShapeDtypeStruct((n_seq * t, n_heads * hd), F32),
        scratch_shapes=[pltpu.VMEM((t_ctx + t, n_kv * hd), BF16),
                        pltpu.VMEM((t_ctx + t, n_kv * hd), BF16)],
        compiler_params=_params(("arbitrary", "arbitrary")),
        name="gqa_latent",
    )(proj, proj, cache_k, cache_v, cosd, sind, cosd, sind, qn.reshape(1, hd), kn.reshape(1, hd))


def _mha_ctx_kernel(q_ref, k_ref, v_ref, o_ref, *, hd):
    scale = hd ** -0.5
    for h in range(q_ref.shape[1] // hd):
        sl = slice(h * hd, (h + 1) * hd)
        s = _bdot_nt(q_ref[:, sl], k_ref[:, sl]) * scale
        (e,), den = _softmax_rows([s])
        o_ref[:, sl] = _bdot(e / den, v_ref[:, sl])


def _mha_ctx_call(proj, n_seq, t):
    hd, cw = 64, 128
    d = proj.shape[1] // 3
    ncb = d // cw
    body = functools.partial(_mha_ctx_kernel, hd=hd)
    return pl.pallas_call(
        body,
        grid=(n_seq, ncb),
        in_specs=[pl.BlockSpec((t, cw), lambda b, c: (b, c)),
                  pl.BlockSpec((t, cw), lambda b, c: (b, ncb + c)),
                  pl.BlockSpec((t, cw), lambda b, c: (b, 2 * ncb + c))],
        out_specs=pl.BlockSpec((t, cw), lambda b, c: (b, c)),
        out_shape=jax.ShapeDtypeStruct((n_seq * t, d), F32),
        compiler_params=_params(("arbitrary", "arbitrary")),
        name="mha_ctx",
    )(proj, proj, proj)


def _na_bias_kernel(rpb_ref, o_ref, *, n_rel_rows, n_rel_cols):
    h = pl.program_id(0)
    w_io = lax.broadcasted_iota(I32, (GRID_W, 2 * GRID_W), 0)
    lane = lax.broadcasted_iota(I32, (GRID_W, 2 * GRID_W), 1)
    ck = jnp.where(lane < GRID_W, lane, lane - GRID_W)
    c_start = jnp.clip(w_io - NA_COLS // 2, 0, GRID_W - NA_COLS)
    in_win = (ck >= c_start) & (ck < c_start + NA_COLS)
    rel = ck - w_io + (NA_COLS - 1)
    base = h * (n_rel_rows * n_rel_cols)
    tiles = []
    for j in range(n_rel_rows):
        acc = jnp.zeros((GRID_W, 2 * GRID_W), F32)
        for jj in range(n_rel_cols):
            acc = jnp.where(rel == jj, rpb_ref[base + j * n_rel_cols + jj], acc)
        tiles.append(jnp.where(in_win, acc, NEG_BIG))
    for j in range(n_rel_rows):
        hi = tiles[j + 1] if j + 1 < n_rel_rows else jnp.full((GRID_W, 2 * GRID_W), NEG_BIG, F32)
        o_ref[0, j] = jnp.where(lane < GRID_W, tiles[j], hi)


def _na_bias_call(rpb):
    n_heads, nrr, nrc = rpb.shape
    body = functools.partial(_na_bias_kernel, n_rel_rows=nrr, n_rel_cols=nrc)
    return pl.pallas_call(
        body,
        grid=(n_heads,),
        in_specs=[pl.BlockSpec(memory_space=pltpu.SMEM)],
        out_specs=pl.BlockSpec((1, nrr, GRID_W, 2 * GRID_W), lambda h: (h, 0, 0, 0)),
        out_shape=jax.ShapeDtypeStruct((n_heads, nrr, GRID_W, 2 * GRID_W), F32),
        compiler_params=_params(("arbitrary",)),
        name="na_bias",
    )(rpb.reshape(-1))


NA_QROWS = 4
NA_KROWS = 12


def _na_kernel(q_ref, k0_ref, k1_ref, k2_ref, v0_ref, v1_ref, v2_ref, kc_ref, vc_ref, tz_ref, o_ref,
               *, hd, n_grid_rows):
    scale = hd ** -0.5
    blk = pl.program_id(1)
    kstart = jnp.clip(blk * NA_QROWS - NA_ROWS // 2, 0, n_grid_rows - NA_KROWS)
    lane = lax.broadcasted_iota(I32, (GRID_W, 2 * GRID_W), 1)
    n_rel = tz_ref.shape[1]
    for hh in range(q_ref.shape[1] // hd):
        sl = slice(hh * hd, (hh + 1) * hd)
        rows = []
        for rq_l in range(NA_QROWS):
            rq = blk * NA_QROWS + rq_l
            r_start = jnp.clip(rq - NA_ROWS // 2, 0, n_grid_rows - NA_ROWS)
            tiles = []
            for m in range(NA_KROWS // 2):
                rk = kstart + 2 * m
                j = jnp.clip(rk - rq + (NA_ROWS - 1), 0, n_rel - 1)
                ok0 = (rk >= r_start) & (rk < r_start + NA_ROWS)
                ok1 = (rk + 1 >= r_start) & (rk + 1 < r_start + NA_ROWS)
                pen = jnp.where(lane < GRID_W, jnp.where(ok0, 0.0, NEG_BIG), jnp.where(ok1, 0.0, NEG_BIG))
                tiles.append(tz_ref[hh, pl.ds(j, 1)][0] + pen)
            rows.append(jnp.concatenate(tiles, axis=-1))
        bias = jnp.concatenate(rows, axis=0)
        q = q_ref[:, sl]
        k_loc = jnp.concatenate([k0_ref[:, sl], k1_ref[:, sl], k2_ref[:, sl]], axis=0)
        v_loc = jnp.concatenate([v0_ref[:, sl], v1_ref[:, sl], v2_ref[:, sl]], axis=0)
        s_loc = _bdot_nt(q, k_loc) * scale + bias
        s_ctx = _bdot_nt(q, kc_ref[0, :, sl]) * scale
        (e_loc, e_ctx), den = _softmax_rows([s_loc, s_ctx])
        o_ref[:, sl] = (_bdot(e_loc, v_loc) + _bdot(e_ctx, vc_ref[0, :, sl])) / den


def _na_call(proj, row0, n_seq, t, cache_k, cache_v, tz):
    hd, cw = 64, 128
    d = proj.shape[1] // 3
    ncb = d // cw
    tq = NA_QROWS * GRID_W
    nq = t // tq
    n_grid_rows = t // GRID_W
    t_ctx = cache_k.shape[1]
    qblk0 = row0 // tq

    def kv_map(which, j):
        def index_map(b, i, c):
            ks = jnp.clip(i * NA_QROWS - NA_ROWS // 2, 0, n_grid_rows - NA_KROWS) // NA_QROWS
            return (qblk0 + b * nq + ks + j, which * ncb + c)
        return index_map

    body = functools.partial(_na_kernel, hd=hd, n_grid_rows=n_grid_rows)
    return pl.pallas_call(
        body,
        grid=(n_seq, nq, ncb),
        in_specs=[pl.BlockSpec((tq, cw), lambda b, i, c: (qblk0 + b * nq + i, c))]
                 + [pl.BlockSpec((tq, cw), kv_map(1, j)) for j in range(3)]
                 + [pl.BlockSpec((tq, cw), kv_map(2, j)) for j in range(3)]
                 + [pl.BlockSpec((1, t_ctx, cw), lambda b, i, c: (b, 0, c)),
                    pl.BlockSpec((1, t_ctx, cw), lambda b, i, c: (b, 0, c)),
                    pl.BlockSpec((cw // hd,) + tz.shape[1:], lambda b, i, c: (c, 0, 0, 0))],
        out_specs=pl.BlockSpec((tq, cw), lambda b, i, c: (b * nq + i, c)),
        out_shape=jax.ShapeDtypeStruct((n_seq * t, d), F32),
        compiler_params=_params(("arbitrary", "arbitrary", "arbitrary")),
        name="nbr_attn",
    )(proj, proj, proj, proj, proj, proj, proj, cache_k, cache_v, tz)


def _hgrn_kernel(q_ref, v_ref, f_ref, bf_ref, lb_ref, s0_ref, o_ref, s_ref, st_scr,
                 *, reverse, n_heads, dk, has_s0, n_blk):
    c = pl.program_id(1)
    L = GLA_CHUNK
    tb = q_ref.shape[0]

    @pl.when(c == 0)
    def _():
        for h in range(n_heads):
            if has_s0:
                st_scr[h] = s0_ref[0, 0, h].T
            else:
                st_scr[h] = jnp.zeros_like(st_scr[h])

    row = lax.broadcasted_iota(I32, (L, L), 0)
    col = lax.broadcasted_iota(I32, (L, L), 1)
    tri = jnp.where((col >= row) if reverse else (col <= row), 1.0, 0.0).astype(F32)
    lb = lb_ref[0]
    bf = bf_ref[0]
    n_chunks = tb // L

    def chunk_step(jj, carry):
        jc = (n_chunks - 1 - jj) if reverse else jj
        rs = pl.ds(pl.multiple_of(jc * L, L), L)
        q = _silu(q_ref[rs, :]) * (dk ** -0.5)
        v = v_ref[rs, :]
        f = lb + (1.0 - lb) * _sigmoid(f_ref[rs, :] + bf)
        logf = jnp.log(f)
        kk = 1.0 - f
        cum = _fdot(tri, logf)
        end = cum[0:1, :] if reverse else cum[L - 1:L, :]
        qd = q * jnp.exp(cum)
        kd = kk * jnp.exp(end - cum)
        e_end = jnp.exp(end)
        acc = []
        for h in range(n_heads):
            hs = slice(h * dk, (h + 1) * dk)
            acc.append(_bdot_nt(qd[:, hs], st_scr[h]))
        tio = lax.broadcasted_iota(I32, (L, 1), 0)
        for s in range(L):
            g8 = (s // 8) * 8
            r0, r1 = (0, g8 + 8) if reverse else (g8, L)
            keep = (tio[r0:r1] <= s) if reverse else (tio[r0:r1] >= s)
            arg = jnp.where(keep, cum[r0:r1, :] - cum[s:s + 1, :], -jnp.inf)
            w = jnp.exp(arg) * q[r0:r1, :] * kk[s:s + 1, :]
            for h in range(n_heads):
                hs = slice(h * dk, (h + 1) * dk)
                a = jnp.sum(w[:, hs], axis=-1, keepdims=True)
                upd = a * v[s:s + 1, hs]
                if r0 == 0 and r1 == L:
                    acc[h] = acc[h] + upd
                elif r0 == 0:
                    acc[h] = jnp.concatenate([acc[h][:r1] + upd, acc[h][r1:]], axis=0)
                else:
                    acc[h] = jnp.concatenate([acc[h][:r0], acc[h][r0:] + upd], axis=0)
        o_ref[rs, :] = jnp.concatenate(acc, axis=-1)
        for h in range(n_heads):
            hs = slice(h * dk, (h + 1) * dk)
            st_scr[h] = st_scr[h] * e_end[:, hs] + _bdot_tn(v[:, hs], kd[:, hs])
        return carry

    lax.fori_loop(0, n_chunks, chunk_step, 0)

    @pl.when(c == n_blk - 1)
    def _():
        for h in range(n_heads):
            s_ref[0, 0, h] = st_scr[h].T


def _hgrn_call(proj, row0, n_seq, t, b_f, lb, s0, reverse):
    n_heads, dk = 8, 128
    d = n_heads * dk
    tb = 256
    n_blk = t // tb
    blk0 = row0 // tb
    di = 1 if reverse else 0
    has_s0 = s0 is not None
    if not has_s0:
        s0 = jnp.zeros((1, 2, n_heads, dk, dk), F32)

    def tok(b, c):
        return blk0 + b * n_blk + ((n_blk - 1 - c) if reverse else c)

    body = functools.partial(_hgrn_kernel, reverse=reverse, n_heads=n_heads, dk=dk,
                             has_s0=has_s0, n_blk=n_blk)
    return pl.pallas_call(
        body,
        grid=(n_seq, n_blk),
        in_specs=[pl.BlockSpec((tb, d), lambda b, c: (tok(b, c), 0)),
                  pl.BlockSpec((tb, d), lambda b, c: (tok(b, c), 1)),
                  pl.BlockSpec((tb, d), lambda b, c: (tok(b, c), 2 + di)),
                  pl.BlockSpec((1, 1, d), lambda b, c: (di, 0, 0)),
                  pl.BlockSpec((1, 1, d), lambda b, c: (di, 0, 0)),
                  pl.BlockSpec((1, 1, n_heads, dk, dk),
                               (lambda b, c: (b, di, 0, 0, 0)) if has_s0 else (lambda b, c: (0, 0, 0, 0, 0)))],
        out_specs=[pl.BlockSpec((tb, d), lambda b, c: (tok(b, c) - blk0, 0)),
                   pl.BlockSpec((1, 1, n_heads, dk, dk), lambda b, c: (b, 0, 0, 0, 0))],
        out_shape=[jax.ShapeDtypeStruct((n_seq * t, d), F32),
                   jax.ShapeDtypeStruct((n_seq, 1, n_heads, dk, dk), F32)],
        scratch_shapes=[pltpu.VMEM((n_heads, dk, dk), F32)],
        compiler_params=_params(("arbitrary", "arbitrary")),
        name="hgrn2_bw" if reverse else "hgrn2_fw",
    )(proj, proj, proj, b_f.reshape(2, 1, d), lb.reshape(2, 1, d), s0)


def _log_sigmoid(x):
    return jnp.minimum(x, 0.0) - jnp.log(1.0 + jnp.exp(-jnp.abs(x)))


def _mlstm_kernel(q_ref, k_ref, v_ref, g_ref, bg_ref, c0_ref, n0_ref, m0_ref,
                  o_ref, c_out, n_out, m_out, c_scr, n_scr, m_scr,
                  *, reverse, n_heads, dqk, dv, has_state, n_blk):
    c = pl.program_id(1)
    L = q_ref.shape[0]
    i_off = 2 * n_heads if reverse else 0
    f_off = i_off + n_heads

    @pl.when(c == 0)
    def _():
        if has_state:
            c_scr[...] = c0_ref[0, 0]
            n_scr[...] = n0_ref[0, 0]
            m_scr[...] = m0_ref[0, 0]
        else:
            c_scr[...] = jnp.zeros_like(c_scr)
            n_scr[...] = jnp.zeros_like(n_scr)
            m_scr[...] = jnp.zeros_like(m_scr)

    gates = GATE_SOFTCAP * jnp.tanh((g_ref[...] + bg_ref[...]) / GATE_SOFTCAP)
    logf = _log_sigmoid(gates)
    row = lax.broadcasted_iota(I32, (L, L), 0)
    col = lax.broadcasted_iota(I32, (L, L), 1)
    causal = (col >= row) if reverse else (col <= row)
    tri = jnp.where(causal, 1.0, 0.0).astype(F32)
    cum = _fdot(tri, logf)
    cum_t = cum.T
    gates_t = gates.T
    e_row = 0 if reverse else L - 1
    outs = []
    for h in range(n_heads):
        qh = q_ref[:, h * dqk:(h + 1) * dqk] * (dqk ** -0.5)
        kh = k_ref[:, h * dqk:(h + 1) * dqk]
        vh = v_ref[:, h * dv:(h + 1) * dv]
        cum_c = cum[:, f_off + h:f_off + h + 1]
        cum_r = cum_t[f_off + h:f_off + h + 1, :]
        i_c = gates[:, i_off + h:i_off + h + 1]
        i_r = gates_t[i_off + h:i_off + h + 1, :]
        m_prev = m_scr[0:1, h:h + 1]
        d = jnp.where(causal, cum_c - cum_r + i_r, -jnp.inf)
        m_inter = cum_c + m_prev
        m_t = jnp.maximum(m_inter, d.max(axis=-1, keepdims=True))
        w_inter = jnp.exp(m_inter - m_t)
        w_intra = jnp.exp(d - m_t)
        qk = _bdot_nt(qh, kh) * w_intra
        num = w_inter * _bdot(qh, c_scr[h]) + _bdot(qk, vh)
        den = w_inter * jnp.sum(qh * n_scr[h], axis=-1, keepdims=True) + qk.sum(axis=-1, keepdims=True)
        outs.append(num / jnp.maximum(jnp.abs(den), jnp.exp(-m_t)))
        end = cum_c[e_row:e_row + 1, :]
        g_end_r = end - cum_r + i_r
        g_end_c = end - cum_c + i_c
        m_new = jnp.maximum(end + m_prev, g_end_r.max(axis=-1, keepdims=True))
        w_old = jnp.exp(end + m_prev - m_new)
        ks = kh * jnp.exp(g_end_c - m_new)
        c_scr[h] = w_old * c_scr[h] + _bdot_tn(ks, vh)
        n_scr[h] = w_old * n_scr[h] + ks.sum(axis=0, keepdims=True)
        m_scr[0:1, h:h + 1] = m_new
    o_ref[...] = jnp.concatenate(outs, axis=-1)

    @pl.when(c == n_blk - 1)
    def _():
        c_out[0, 0] = c_scr[...]
        n_out[0, 0] = n_scr[...]
        m_out[0, 0] = m_scr[...]


def _mlstm_call(proj, row0, n_seq, t, b_gates_pad, state, reverse):
    n_heads, dqk, dv = 8, 64, 128
    wq, wv = n_heads * dqk, n_heads * dv
    L = 256
    n_blk = t // L
    blk0 = row0 // L
    di = 1 if reverse else 0
    has_state = state is not None
    if has_state:
        c0, n0, m0 = state
        smap = lambda b, c: (b, di, 0, 0, 0)
        mmap = lambda b, c: (b, di, 0, 0)
    else:
        c0 = jnp.zeros((1, 1, n_heads, dqk, dv), F32)
        n0 = jnp.zeros((1, 1, n_heads, 1, dqk), F32)
        m0 = jnp.zeros((1, 1, 1, 128), F32)
        smap = lambda b, c: (0, 0, 0, 0, 0)
        mmap = lambda b, c: (0, 0, 0, 0)

    def tok(b, c):
        return blk0 + b * n_blk + ((n_blk - 1 - c) if reverse else c)

    body = functools.partial(_mlstm_kernel, reverse=reverse, n_heads=n_heads, dqk=dqk, dv=dv,
                             has_state=has_state, n_blk=n_blk)
    gate_cb = (2 * wq + 2 * wv) // 128
    return pl.pallas_call(
        body,
        grid=(n_seq, n_blk),
        in_specs=[pl.BlockSpec((L, wq), lambda b, c: (tok(b, c), 0)),
                  pl.BlockSpec((L, wq), lambda b, c: (tok(b, c), 1)),
                  pl.BlockSpec((L, wv), lambda b, c: (tok(b, c), (2 * wq) // wv)),
                  pl.BlockSpec((L, 128), lambda b, c: (tok(b, c), gate_cb)),
                  pl.BlockSpec((1, 128), lambda b, c: (0, 0)),
                  pl.BlockSpec((1, 1, n_heads, dqk, dv), smap),
                  pl.BlockSpec((1, 1, n_heads, 1, dqk), smap),
                  pl.BlockSpec((1, 1, 1, 128), mmap)],
        out_specs=[pl.BlockSpec((L, wv), lambda b, c: (tok(b, c) - blk0, 0)),
                   pl.BlockSpec((1, 1, n_heads, dqk, dv), lambda b, c: (b, 0, 0, 0, 0)),
                   pl.BlockSpec((1, 1, n_heads, 1, dqk), lambda b, c: (b, 0, 0, 0, 0)),
                   pl.BlockSpec((1, 1, 1, 128), lambda b, c: (b, 0, 0, 0))],
        out_shape=[jax.ShapeDtypeStruct((n_seq * t, wv), F32),
                   jax.ShapeDtypeStruct((n_seq, 1, n_heads, dqk, dv), F32),
                   jax.ShapeDtypeStruct((n_seq, 1, n_heads, 1, dqk), F32),
                   jax.ShapeDtypeStruct((n_seq, 1, 1, 128), F32)],
        scratch_shapes=[pltpu.VMEM((n_heads, dqk, dv), F32),
                        pltpu.VMEM((n_heads, 1, dqk), F32),
                        pltpu.VMEM((1, 128), F32)],
        compiler_params=_params(("arbitrary", "arbitrary")),
        name="mlstm_bw" if reverse else "mlstm_fw",
    )(proj, proj, proj, proj, b_gates_pad, c0, n0, m0)


def _moe_input(x_ref, m_ref, g_ref):
    return _rms(x_ref[...], g_ref[...]) * (1.0 + m_ref[0, 4:5, :]) + m_ref[0, 3:4, :]


def _route_kernel(x_ref, m_ref, g_ref, wr_ref, br_ref, idx_ref, gate_ref, rank_ref, cnt_ref, carry_scr,
                  *, n_experts):
    i = pl.program_id(0)
    tm = x_ref.shape[0]

    @pl.when(i == 0)
    def _():
        carry_scr[...] = jnp.zeros_like(carry_scr)

    h = _moe_input(x_ref, m_ref, g_ref)
    logits = lax.dot_general(wr_ref[...], h, (((1,), (1,)), ((), ())), preferred_element_type=F32,
                             precision=lax.Precision.HIGHEST) + br_ref[...]
    e_io = lax.broadcasted_iota(I32, (n_experts, tm), 0).astype(F32)
    work = logits
    vals, idxs = [], []
    chosen = jnp.zeros((n_experts, tm), F32)
    for _ in range(TOP_K):
        mx = work.max(axis=0, keepdims=True)
        ix = jnp.min(jnp.where(work == mx, e_io, float(n_experts)), axis=0, keepdims=True)
        hit = e_io == ix
        vals.append(mx)
        idxs.append(ix)
        chosen = jnp.where(hit, 1.0, chosen)
        work = jnp.where(hit, -jnp.inf, work)
    es = [jnp.exp(v - vals[0]) for v in vals]
    den = es[0] + es[1] + es[2] + es[3]
    srow = lax.broadcasted_iota(I32, (tm, tm), 0)
    scol = lax.broadcasted_iota(I32, (tm, tm), 1)
    before = jnp.where(srow < scol, 1.0, 0.0).astype(BF16)
    pos = jnp.dot(chosen.astype(BF16), before, preferred_element_type=F32) + carry_scr[...]
    ranks = [jnp.sum(jnp.where(e_io == ix, pos, 0.0), axis=0, keepdims=True) for ix in idxs]
    carry_scr[...] = carry_scr[...] + chosen.sum(axis=1, keepdims=True)
    idx_ref[...] = jnp.concatenate(idxs, axis=0).astype(I32)
    gate_ref[...] = jnp.concatenate([e / den for e in es], axis=0)
    rank_ref[...] = jnp.concatenate(ranks, axis=0).astype(I32)
    cnt_ref[...] = jnp.broadcast_to(carry_scr[...], cnt_ref.shape).astype(I32)


def _route_call(x, modsel, g, w_router, b_router):
    n, d = x.shape
    n_experts = w_router.shape[1]
    tm = TOKEN_TILE
    body = functools.partial(_route_kernel, n_experts=n_experts)
    return pl.pallas_call(
        body,
        grid=(n // tm,),
        in_specs=[pl.BlockSpec((tm, d), lambda i: (i, 0)),
                  pl.BlockSpec((1, 6, d), lambda i: (i, 0, 0)),
                  pl.BlockSpec((1, d), lambda i: (0, 0)),
                  pl.BlockSpec((n_experts, d), lambda i: (0, 0)),
                  pl.BlockSpec((n_experts, 1), lambda i: (0, 0))],
        out_specs=[pl.BlockSpec((TOP_K, tm), lambda i: (0, i)),
                   pl.BlockSpec((TOP_K, tm), lambda i: (0, i)),
                   pl.BlockSpec((TOP_K, tm), lambda i: (0, i)),
                   pl.BlockSpec((n_experts, 128), lambda i: (0, 0))],
        out_shape=[jax.ShapeDtypeStruct((TOP_K, n), I32),
                   jax.ShapeDtypeStruct((TOP_K, n), F32),
                   jax.ShapeDtypeStruct((TOP_K, n), I32),
                   jax.ShapeDtypeStruct((n_experts, 128), I32)],
        scratch_shapes=[pltpu.VMEM((n_experts, 1), F32)],
        compiler_params=_params(("arbitrary",)),
        name="moe_route",
    )(x, modsel, g.reshape(1, d), w_router.T, b_router.reshape(n_experts, 1))


def _slot_kernel(cnt_ref, idx_ref, rank_ref, dest_ref, bexp_ref, nused_ref, *, n_experts, n_blocks):
    cnt = cnt_ref[:, 0:1].astype(F32)
    padded = jnp.ceil(cnt * (1.0 / MOE_BLOCK)) * MOE_BLOCK
    er = lax.broadcasted_iota(I32, (n_experts, n_experts), 0)
    ec = lax.broadcasted_iota(I32, (n_experts, n_experts), 1)
    start_row = jnp.sum(jnp.where(er < ec, padded, 0.0), axis=0, keepdims=True)
    start_col = jnp.sum(jnp.where(er == ec, start_row, 0.0), axis=1, keepdims=True)
    end_col = start_col + padded
    idx = idx_ref[...]
    e_io = lax.broadcasted_iota(I32, (n_experts,) + idx.shape[1:], 0)
    rows = []
    for k in range(TOP_K):
        hit = e_io == idx[k:k + 1, :]
        rows.append(jnp.sum(jnp.where(hit, start_col, 0.0), axis=0, keepdims=True))
    dest_ref[...] = jnp.concatenate(rows, axis=0).astype(I32) + rank_ref[...]
    blk_start = (lax.broadcasted_iota(I32, (n_experts, n_blocks), 1) * MOE_BLOCK).astype(F32)
    n_done = jnp.sum(jnp.where(end_col <= blk_start, 1.0, 0.0), axis=0, keepdims=True)
    bexp_ref[...] = jnp.minimum(n_done, n_experts - 1.0).astype(I32)
    nused_ref[...] = (jnp.sum(padded, axis=0, keepdims=True) * (1.0 / MOE_BLOCK)).astype(I32)


def _slot_call(counts, idx_t, rank_t, n_blocks):
    n_experts = counts.shape[0]
    n = idx_t.shape[1]
    tn = min(2048, n)
    body = functools.partial(_slot_kernel, n_experts=n_experts, n_blocks=n_blocks)
    return pl.pallas_call(
        body,
        grid=(n // tn,),
        in_specs=[pl.BlockSpec((n_experts, 128), lambda i: (0, 0)),
                  pl.BlockSpec((TOP_K, tn), lambda i: (0, i)),
                  pl.BlockSpec((TOP_K, tn), lambda i: (0, i))],
        out_specs=[pl.BlockSpec((TOP_K, tn), lambda i: (0, i)),
                   pl.BlockSpec((1, n_blocks), lambda i: (0, 0)),
                   pl.BlockSpec((1, 1), lambda i: (0, 0))],
        out_shape=[jax.ShapeDtypeStruct((TOP_K, n), I32),
                   jax.ShapeDtypeStruct((1, n_blocks), I32),
                   jax.ShapeDtypeStruct((1, 1), I32)],
        compiler_params=_params(("arbitrary",)),
        name="moe_slots",
    )(counts, idx_t, rank_t)


def _dispatch_kernel(dest_ref, x_ref, m_ref, g_ref, xs_in_ref, xs_ref, h_scr, sem):
    del xs_in_ref
    tm = x_ref.shape[0]
    h_scr[...] = _moe_input(x_ref, m_ref, g_ref)

    def row_copy(j):
        r = lax.rem(j, tm)
        return pltpu.make_async_copy(h_scr.at[pl.ds(r, 1)], xs_ref.at[pl.ds(dest_ref[0, 0, j], 1)], sem.at[0])

    def start(j, carry):
        row_copy(j).start()
        return carry

    def wait(j, carry):
        row_copy(j).wait()
        return carry

    lax.fori_loop(0, TOP_K * tm, start, 0)
    lax.fori_loop(0, TOP_K * tm, wait, 0)


def _dispatch_call(dest_tiles, x, modsel, g, xs_zero):
    n, d = x.shape
    tm = TOKEN_TILE
    return pl.pallas_call(
        _dispatch_kernel,
        grid=(n // tm,),
        in_specs=[pl.BlockSpec((1, 1, TOP_K * tm), lambda i: (i, 0, 0), memory_space=pltpu.SMEM),
                  pl.BlockSpec((tm, d), lambda i: (i, 0)),
                  pl.BlockSpec((1, 6, d), lambda i: (i, 0, 0)),
                  pl.BlockSpec((1, d), lambda i: (0, 0)),
                  pl.BlockSpec(memory_space=pl.ANY)],
        out_specs=pl.BlockSpec(memory_space=pl.ANY),
        out_shape=jax.ShapeDtypeStruct(xs_zero.shape, F32),
        scratch_shapes=[pltpu.VMEM((tm, d), F32), pltpu.SemaphoreType.DMA((1,))],
        input_output_aliases={4: 0},
        compiler_params=_params(("arbitrary",)),
        name="moe_dispatch",
    )(dest_tiles, x, modsel, g.reshape(1, d), xs_zero)


def _ffn_kernel(bexp_ref, nused_ref, xs_ref, wgu_ref, bgu_ref, wdn_ref, bdn_ref, ys_ref, wgu_scr, wdn_scr):
    b = pl.program_id(0)
    d_ff = wdn_ref.shape[1]

    @pl.when(b < nused_ref[0])
    def _():
        prev = bexp_ref[jnp.maximum(b - 1, 0)]

        @pl.when((b == 0) | (bexp_ref[b] != prev))
        def _():
            wgu_scr[...] = wgu_ref[0].astype(BF16)
            wdn_scr[...] = wdn_ref[0].astype(BF16)

        gu = jnp.dot(xs_ref[...].astype(BF16), wgu_scr[...], preferred_element_type=F32) + bgu_ref[0]
        x_glu = jnp.minimum(gu[:, :d_ff], SWIGLU_LIMIT)
        x_lin = jnp.clip(gu[:, d_ff:], -SWIGLU_LIMIT, SWIGLU_LIMIT)
        hid = x_glu * _sigmoid(SWIGLU_ALPHA * x_glu) * (x_lin + 1.0)
        ys_ref[...] = jnp.dot(hid.astype(BF16), wdn_scr[...], preferred_element_type=F32) + bdn_ref[0]

    @pl.when(b >= nused_ref[0])
    def _():
        ys_ref[...] = jnp.zeros_like(ys_ref)


def _ffn_call(block_expert, n_used, xs, w_gu, b_gu, w_dn, b_dn):
    n_slots, d = xs.shape
    n_experts, _, d_ff2 = w_gu.shape
    d_ff = d_ff2 // 2
    n_blocks = n_slots // MOE_BLOCK

    def blk(b, be, nu):
        return jnp.minimum(b, nu[0] - 1)

    grid_spec = pltpu.PrefetchScalarGridSpec(
        num_scalar_prefetch=2,
        grid=(n_blocks,),
        in_specs=[pl.BlockSpec((MOE_BLOCK, d), lambda b, be, nu: (blk(b, be, nu), 0)),
                  pl.BlockSpec((1, d, d_ff2), lambda b, be, nu: (be[blk(b, be, nu)], 0, 0)),
                  pl.BlockSpec((1, 1, d_ff2), lambda b, be, nu: (be[blk(b, be, nu)], 0, 0)),
                  pl.BlockSpec((1, d_ff, d), lambda b, be, nu: (be[blk(b, be, nu)], 0, 0)),
                  pl.BlockSpec((1, 1, d), lambda b, be, nu: (be[blk(b, be, nu)], 0, 0))],
        out_specs=pl.BlockSpec((MOE_BLOCK, d), lambda b, be, nu: (b, 0)),
        scratch_shapes=[pltpu.VMEM((d, d_ff2), BF16), pltpu.VMEM((d_ff, d), BF16)],
    )
    return pl.pallas_call(
        _ffn_kernel,
        grid_spec=grid_spec,
        out_shape=jax.ShapeDtypeStruct((n_slots, d), F32),
        compiler_params=_params(("arbitrary",)),
        name="moe_ffn",
    )(block_expert, n_used, xs, w_gu, b_gu.reshape(n_experts, 1, d_ff2), w_dn, b_dn.reshape(n_experts, 1, d))


def _combine_kernel(dest_ref, ys_ref, gate_ref, x_ref, m_ref, fg_ref, y_ref, buf, sem, *, final_norm):
    tm = x_ref.shape[0]

    def row_copy(j):
        k = j // tm
        r = j - k * tm
        return pltpu.make_async_copy(ys_ref.at[pl.ds(dest_ref[0, 0, j], 1)], buf.at[k, pl.ds(r, 1)], sem.at[0])

    def start(j, carry):
        row_copy(j).start()
        return carry

    def wait(j, carry):
        row_copy(j).wait()
        return carry

    lax.fori_loop(0, TOP_K * tm, start, 0)
    lax.fori_loop(0, TOP_K * tm, wait, 0)
    y = gate_ref[:, 0:1] * buf[0]
    for k in range(1, TOP_K):
        y = y + gate_ref[:, k:k + 1] * buf[k]
    out = x_ref[...] + m_ref[0, 5:6, :] * y
    if final_norm:
        out = _rms(out, fg_ref[...])
    y_ref[...] = out


def _combine_call(dest_tiles, ys, gates_nk, x, modsel, final_g, final_norm):
    n, d = x.shape
    tm = TOKEN_TILE
    body = functools.partial(_combine_kernel, final_norm=final_norm)
    return pl.pallas_call(
        body,
        grid=(n // tm,),
        in_specs=[pl.BlockSpec((1, 1, TOP_K * tm), lambda i: (i, 0, 0), memory_space=pltpu.SMEM),
                  pl.BlockSpec(memory_space=pl.ANY),
                  pl.BlockSpec((tm, TOP_K), lambda i: (i, 0)),
                  pl.BlockSpec((tm, d), lambda i: (i, 0)),
                  pl.BlockSpec((1, 6, d), lambda i: (i, 0, 0)),
                  pl.BlockSpec((1, d), lambda i: (0, 0))],
        out_specs=pl.BlockSpec((tm, d), lambda i: (i, 0)),
        out_shape=jax.ShapeDtypeStruct((n, d), F32),
        scratch_shapes=[pltpu.VMEM((TOP_K, tm, d), F32), pltpu.SemaphoreType.DMA((1,))],
        compiler_params=_params(("arbitrary",)),
        name="moe_combine",
    )(dest_tiles, ys, gates_nk, x, modsel, final_g.reshape(1, d))


def _moe_layer(x, modsel, g2, w_router, b_router, w_gu, b_gu, w_dn, b_dn, final_g, final_norm):
    n, d = x.shape
    n_experts = w_router.shape[1]
    tm = TOKEN_TILE
    n_blocks = (n * TOP_K) // MOE_BLOCK + n_experts
    idx_t, gate_t, rank_t, counts = _route_call(x, modsel, g2, w_router, b_router)
    dest_t, block_expert, n_used = _slot_call(counts, idx_t, rank_t, n_blocks)
    dest_tiles = dest_t.reshape(TOP_K, n // tm, tm).transpose(1, 0, 2).reshape(n // tm, 1, TOP_K * tm)
    xs = _dispatch_call(dest_tiles, x, modsel, g2, jnp.zeros((n_blocks * MOE_BLOCK, d), F32))
    ys = _ffn_call(block_expert.reshape(n_blocks), n_used.reshape(1), xs, w_gu, b_gu, w_dn, b_dn)
    return _combine_call(dest_tiles, ys, gate_t.T, x, modsel, final_g, final_norm)


def _rope_tables(t, hd):
    pos = np.arange(t)
    n_freq = hd // 4
    inv_freq = ROPE_THETA ** (-np.arange(n_freq, dtype=np.float32) / n_freq)
    ang = np.concatenate([(pos // GRID_W).astype(np.float32)[:, None] * inv_freq,
                          (pos % GRID_W).astype(np.float32)[:, None] * inv_freq], axis=-1)
    ang = jnp.asarray(ang, F32)
    cos, sin = jnp.cos(ang), jnp.sin(ang)
    return jnp.concatenate([cos, cos], axis=-1), jnp.concatenate([-sin, sin], axis=-1)


def kernel(x_prompt, x_sample, cache_k_a, cache_v_a, state_b, state_c_C, state_c_n, state_c_m, cache_k_d, cache_v_d, c, c_ctx, norm1_g, norm2_g, w_mod, b_mod, w_in_a, qnorm_a, knorm_a, w_out_a, w_in_b, b_f_b, lower_bounds_b, onorm_b, w_out_b, w_in_c, b_gates_c, onorm_c, w_out_c, w_in_d, rpb_d, w_out_d, w_router, b_router, w_gu, b_gu, w_dn, b_dn, final_g):
    n_ctx_seq, t_ctx, d = x_prompt.shape
    n_lat_seq, t_lat, _ = x_sample.shape
    depth = w_mod.shape[0]
    n_ctx = n_ctx_seq * t_ctx
    n_lat = n_lat_seq * t_lat
    n = n_ctx + n_lat
    tm = TOKEN_TILE
    assert t_ctx % tm == 0 and t_lat % tm == 0 and n_lat_seq + 1 <= 8

    lb_cum = jnp.cumsum(jax.nn.softmax(lower_bounds_b.astype(F32), axis=0), axis=0)
    lb_all = lb_cum - lb_cum[0]

    cond8 = jnp.zeros((8, d), F32).at[0].set(c_ctx).at[1:1 + n_lat_seq].set(c)
    mod = _mod_call(cond8, w_mod, b_mod)
    tile_row = np.concatenate([np.zeros(n_ctx // tm, np.int32),
                               1 + np.repeat(np.arange(n_lat_seq, dtype=np.int32), t_lat // tm)])

    x = jnp.concatenate([x_prompt.reshape(n_ctx, d), x_sample.reshape(n_lat, d)], axis=0)
    outs = {}
    for i in range(depth):
        kind = i % 4
        j = i // 4
        modsel = mod[i].reshape(8, 6, d)[tile_row]
        if kind == 0:
            proj = _inproj_call(x, modsel, norm1_g[i], w_in_a[j].astype(BF16))
            o_ctx, k_new = _gqa_ctx_call(proj, qnorm_a[j], knorm_a[j], n_ctx_seq, t_ctx)
            cosd, sind = _rope_tables(t_lat, 128)
            o_lat = _gqa_lat_call(proj, n_ctx, n_lat_seq, t_lat,
                                  cache_k_a[:, j].reshape(n_lat_seq, -1, 256),
                                  cache_v_a[:, j].reshape(n_lat_seq, -1, 256), cosd, sind,
                                  qnorm_a[j], knorm_a[j])
            outs["k_a"] = k_new.reshape(n_ctx_seq, 1, t_ctx, 2, 128)
            outs["v_a"] = proj[:n_ctx, 1280:1536].reshape(n_ctx_seq, 1, t_ctx, 2, 128)
            o = jnp.concatenate([o_ctx, o_lat], axis=0)
            x = _outproj_call("plain", [(o, d, 0)], w_out_a[j].astype(BF16), x, modsel)
        elif kind == 1:
            proj = _inproj_call(x, modsel, norm1_g[i], w_in_b[j].astype(BF16))
            o_dirs, s_dirs = [], []
            for reverse in (False, True):
                oc, sc = _hgrn_call(proj, 0, n_ctx_seq, t_ctx, b_f_b[j], lb_all[i], None, reverse)
                ol, _ = _hgrn_call(proj, n_ctx, n_lat_seq, t_lat, b_f_b[j], lb_all[i], state_b[:, j], reverse)
                o_dirs.append(jnp.concatenate([oc, ol], axis=0))
                s_dirs.append(sc)
            outs["s_b"] = jnp.concatenate(s_dirs, axis=1)[:, None]
            x = _outproj_call("hgrn", [(o_dirs[0], d, 0), (o_dirs[1], d, 0), (proj, d, 4),
                                       (onorm_b[j].reshape(1, 128), None, None)],
                              w_out_b[j].astype(BF16), x, modsel)
        elif kind == 2:
            w_c = jnp.pad(w_in_c[j], ((0, 0), (0, 128 - 32))).astype(BF16)
            bg = jnp.pad(b_gates_c[j].reshape(1, 32), ((0, 0), (0, 128 - 32)))
            proj = _inproj_call(x, modsel, norm1_g[i], w_c)
            state = (state_c_C[:, j], state_c_n[:, j][:, :, :, None, :],
                     jnp.pad(state_c_m[:, j], ((0, 0), (0, 0), (0, 120)))[:, :, None, :])
            o_dirs, st = [], []
            for reverse in (False, True):
                oc, cc, nc, mc = _mlstm_call(proj, 0, n_ctx_seq, t_ctx, bg, None, reverse)
                ol, _, _, _ = _mlstm_call(proj, n_ctx, n_lat_seq, t_lat, bg, state, reverse)
                o_dirs.append(jnp.concatenate([oc, ol], axis=0))
                st.append((cc, nc, mc))
            outs["c_C"] = jnp.concatenate([st[0][0], st[1][0]], axis=1)[:, None]
            outs["c_n"] = jnp.concatenate([st[0][1], st[1][1]], axis=1)[:, None, :, :, 0, :]
            outs["c_m"] = jnp.concatenate([st[0][2], st[1][2]], axis=1)[:, None, :, 0, :8]
            x = _outproj_call("mlstm", [(o_dirs[0], d, 0), (o_dirs[1], d, 0), (proj, d, 2),
                                        (onorm_c[j].reshape(1, 128), None, None)],
                              w_out_c[j].astype(BF16), x, modsel)
        else:
            proj = _inproj_call(x, modsel, norm1_g[i], w_in_d[j].astype(BF16))
            o_ctx = _mha_ctx_call(proj, n_ctx_seq, t_ctx)
            tz = _na_bias_call(rpb_d[j])
            o_lat = _na_call(proj, n_ctx, n_lat_seq, t_lat,
                             cache_k_d[:, j].reshape(n_lat_seq, -1, d),
                             cache_v_d[:, j].reshape(n_lat_seq, -1, d), tz)
            outs["k_d"] = proj[:n_ctx, d:2 * d].reshape(n_ctx_seq, 1, t_ctx, 16, 64)
            outs["v_d"] = proj[:n_ctx, 2 * d:3 * d].reshape(n_ctx_seq, 1, t_ctx, 16, 64)
            o = jnp.concatenate([o_ctx, o_lat], axis=0)
            x = _outproj_call("plain", [(o, d, 0)], w_out_d[j].astype(BF16), x, modsel)
        x = _moe_layer(x, modsel, norm2_g[i], w_router[i], b_router[i], w_gu[i], b_gu[i], w_dn[i], b_dn[i],
                       final_g, final_norm=(i == depth - 1))

    y_prompt = x[:n_ctx].reshape(n_ctx_seq, t_ctx, d)
    y_sample = x[n_ctx:].reshape(n_lat_seq, t_lat, d)
    return (y_prompt, y_sample, outs["k_a"], outs["v_a"], outs["s_b"], outs["c_C"], outs["c_n"], outs["c_m"],
            outs["k_d"], outs["v_d"])
```

```python
import functools

import numpy as np
import jax
import jax.numpy as jnp
from jax import lax
from jax.experimental import pallas as pl
from jax.experimental.pallas import tpu as pltpu

F32 = jnp.float32
BF16 = jnp.bfloat16
I32 = jnp.int32

NORM_EPS = 1e-6
GRID_W = 64
ROPE_THETA = 10000.0
TOP_K = 4
GLA_CHUNK = 32
GATE_SOFTCAP = 15.0
NA_ROWS = 8
NA_COLS = 16
SWIGLU_ALPHA = 1.702
SWIGLU_LIMIT = 7.0
NEG_BIG = -1e30

TOKEN_TILE = 256
MOE_BLOCK = 256
V7X_VMEM_LIMIT = 52 * 1024 * 1024


def _params(sem, vmem=V7X_VMEM_LIMIT):
    return pltpu.CompilerParams(dimension_semantics=sem, vmem_limit_bytes=vmem)


def _bdot(a, b):
    return jnp.dot(a.astype(BF16), b.astype(BF16), preferred_element_type=F32)


def _bdot_nt(a, b):
    return lax.dot_general(a.astype(BF16), b.astype(BF16), (((1,), (1,)), ((), ())),
                           preferred_element_type=F32)


def _bdot_tn(a, b):
    return lax.dot_general(a.astype(BF16), b.astype(BF16), (((0,), (0,)), ((), ())),
                           preferred_element_type=F32)


def _fdot(a, b):
    return jnp.dot(a, b, preferred_element_type=F32, precision=lax.Precision.HIGHEST)


def _rms(x, g):
    return x * lax.rsqrt(jnp.mean(x * x, axis=-1, keepdims=True) + NORM_EPS) * g


def _rms_heads(x, g, n_heads, hd):
    return jnp.concatenate([_rms(x[:, h * hd:(h + 1) * hd], g) for h in range(n_heads)], axis=-1)


def _sigmoid(x):
    return 1.0 / (1.0 + jnp.exp(-x))


def _silu(x):
    return x * _sigmoid(x)


def _softmax_rows(parts):
    m = parts[0].max(axis=-1, keepdims=True)
    for p in parts[1:]:
        m = jnp.maximum(m, p.max(axis=-1, keepdims=True))
    es = [jnp.exp(p - m) for p in parts]
    den = es[0].sum(axis=-1, keepdims=True)
    for e in es[1:]:
        den = den + e.sum(axis=-1, keepdims=True)
    return es, den


def _mod_kernel(c_ref, w_ref, b_ref, o_ref):
    o_ref[0] = _bdot(_silu(c_ref[...]), w_ref[0]) + b_ref[0]


def _mod_call(cond8, w_mod, b_mod):
    depth, d, d6 = w_mod.shape
    tn = 1024
    return pl.pallas_call(
        _mod_kernel,
        grid=(depth, d6 // tn),
        in_specs=[pl.BlockSpec((8, d), lambda i, j: (0, 0)),
                  pl.BlockSpec((1, d, tn), lambda i, j: (i, 0, j)),
                  pl.BlockSpec((1, 1, tn), lambda i, j: (i, 0, j))],
        out_specs=pl.BlockSpec((1, 8, tn), lambda i, j: (i, 0, j)),
        out_shape=jax.ShapeDtypeStruct((depth, 8, d6), F32),
        compiler_params=_params(("arbitrary", "arbitrary")),
        name="adaln_mod",
    )(cond8, w_mod, b_mod.reshape(depth, 1, d6))


def _inproj_kernel(x_ref, m_ref, g_ref, w_ref, o_ref):
    h = _rms(x_ref[...], g_ref[...]) * (1.0 + m_ref[0, 1:2, :]) + m_ref[0, 0:1, :]
    o_ref[...] = _bdot(h, w_ref[...])


def _inproj_call(x, modsel, g, w):
    n, d = x.shape
    wout = w.shape[1]
    tm = TOKEN_TILE
    return pl.pallas_call(
        _inproj_kernel,
        grid=(n // tm,),
        in_specs=[pl.BlockSpec((tm, d), lambda i: (i, 0)),
                  pl.BlockSpec((1, 6, d), lambda i: (i, 0, 0)),
                  pl.BlockSpec((1, d), lambda i: (0, 0)),
                  pl.BlockSpec((d, wout), lambda i: (0, 0))],
        out_specs=pl.BlockSpec((tm, wout), lambda i: (i, 0)),
        out_shape=jax.ShapeDtypeStruct((n, wout), F32),
        compiler_params=_params(("arbitrary",)),
        name="inproj",
    )(x, modsel, g.reshape(1, d), w)


def _outproj_plain_kernel(o_ref, w_ref, x_ref, m_ref, y_ref):
    y_ref[...] = x_ref[...] + m_ref[0, 2:3, :] * _bdot(o_ref[...], w_ref[...])


def _outproj_hgrn_kernel(of_ref, ob_ref, gate_ref, on_ref, w_ref, x_ref, m_ref, y_ref, *, n_heads, hd):
    o = _rms_heads(of_ref[...] + ob_ref[...], on_ref[...], n_heads, hd)
    o = o * _silu(gate_ref[...])
    y_ref[...] = x_ref[...] + m_ref[0, 2:3, :] * _bdot(o, w_ref[...])


def _outproj_mlstm_kernel(of_ref, ob_ref, gate_ref, on_ref, w_ref, x_ref, m_ref, y_ref, *, n_heads, hd):
    hs = _rms_heads(of_ref[...] + ob_ref[...], on_ref[...], n_heads, hd)
    o = _sigmoid(gate_ref[...]) * hs
    y_ref[...] = x_ref[...] + m_ref[0, 2:3, :] * _bdot(o, w_ref[...])


def _outproj_call(kind, mix_ins, w, x, modsel):
    n, d = x.shape
    tm = TOKEN_TILE
    specs, args = [], []
    for arr, width, cb in mix_ins:
        if arr.shape[0] == n:
            specs.append(pl.BlockSpec((tm, width), functools.partial(lambda i, cb: (i, cb), cb=cb)))
        else:
            specs.append(pl.BlockSpec(arr.shape, lambda i: (0, 0)))
        args.append(arr)
    specs += [pl.BlockSpec(w.shape, lambda i: (0, 0)),
              pl.BlockSpec((tm, d), lambda i: (i, 0)),
              pl.BlockSpec((1, 6, d), lambda i: (i, 0, 0))]
    args += [w, x, modsel]
    if kind == "plain":
        body = _outproj_plain_kernel
    elif kind == "hgrn":
        body = functools.partial(_outproj_hgrn_kernel, n_heads=8, hd=128)
    else:
        body = functools.partial(_outproj_mlstm_kernel, n_heads=8, hd=128)
    return pl.pallas_call(
        body,
        grid=(n // tm,),
        in_specs=specs,
        out_specs=pl.BlockSpec((tm, d), lambda i: (i, 0)),
        out_shape=jax.ShapeDtypeStruct((n, d), F32),
        compiler_params=_params(("arbitrary",)),
        name="outproj_" + kind,
    )(*args)


def _gqa_ctx_kernel(p_ref, qn_ref, kn_ref, o_ref, k_ref, *, n_heads, n_kv, hd):
    rep = n_heads // n_kv
    scale = hd ** -0.5
    koff = n_heads * hd
    voff = koff + n_kv * hd
    ks = [_rms(p_ref[:, koff + g * hd: koff + (g + 1) * hd], kn_ref[...]) for g in range(n_kv)]
    k_ref[...] = jnp.concatenate(ks, axis=-1)
    for h in range(n_heads):
        g = h // rep
        q = _rms(p_ref[:, h * hd:(h + 1) * hd], qn_ref[...])
        s = _bdot_nt(q, ks[g]) * scale
        (e,), den = _softmax_rows([s])
        o_ref[:, h * hd:(h + 1) * hd] = _bdot(e / den, p_ref[:, voff + g * hd: voff + (g + 1) * hd])


def _gqa_ctx_call(proj, qn, kn, n_seq, t):
    n_heads, n_kv, hd = 8, 2, 128
    win = proj.shape[1]
    body = functools.partial(_gqa_ctx_kernel, n_heads=n_heads, n_kv=n_kv, hd=hd)
    return pl.pallas_call(
        body,
        grid=(n_seq,),
        in_specs=[pl.BlockSpec((t, win), lambda b: (b, 0)),
                  pl.BlockSpec((1, hd), lambda b: (0, 0)),
                  pl.BlockSpec((1, hd), lambda b: (0, 0))],
        out_specs=[pl.BlockSpec((t, n_heads * hd), lambda b: (b, 0)),
                   pl.BlockSpec((t, n_kv * hd), lambda b: (b, 0))],
        out_shape=[jax.ShapeDtypeStruct((n_seq * t, n_heads * hd), F32),
                   jax.ShapeDtypeStruct((n_seq * t, n_kv * hd), F32)],
        compiler_params=_params(("arbitrary",)),
        name="gqa_ctx",
    )(proj, qn.reshape(1, hd), kn.reshape(1, hd))


def _rope(x, cosd, sind):
    return x * cosd + pltpu.roll(x, x.shape[-1] // 2, 1) * sind


def _gqa_lat_kernel(pq_ref, pkv_ref, kc_ref, vc_ref, cq_ref, sq_ref, ck_ref, sk_ref, qn_ref, kn_ref,
                    o_ref, k_scr, v_scr, *, n_heads, n_kv, hd, t_ctx):
    rep = n_heads // n_kv
    scale = hd ** -0.5

    @pl.when(pl.program_id(1) == 0)
    def _():
        k_scr[0:t_ctx, :] = kc_ref[0].astype(BF16)
        v_scr[0:t_ctx, :] = vc_ref[0].astype(BF16)
        for g in range(n_kv):
            k = _rms(pkv_ref[:, g * hd:(g + 1) * hd], kn_ref[...])
            k_scr[t_ctx:, g * hd:(g + 1) * hd] = _rope(k, ck_ref[...], sk_ref[...]).astype(BF16)
        v_scr[t_ctx:, :] = pkv_ref[:, n_kv * hd:].astype(BF16)

    for h in range(n_heads):
        g = h // rep
        q = _rope(_rms(pq_ref[:, h * hd:(h + 1) * hd], qn_ref[...]), cq_ref[...], sq_ref[...])
        s = _bdot_nt(q, k_scr[:, g * hd:(g + 1) * hd]) * scale
        (e,), den = _softmax_rows([s])
        o_ref[:, h * hd:(h + 1) * hd] = _bdot(e / den, v_scr[:, g * hd:(g + 1) * hd])


def _gqa_lat_call(proj, row0, n_seq, t, cache_k, cache_v, cosd, sind, qn, kn):
    n_heads, n_kv, hd = 8, 2, 128
    tq = 256
    t_ctx = cache_k.shape[1]
    nq = t // tq
    kvw = 2 * n_kv * hd
    qblk0 = row0 // tq
    sblk0 = row0 // t
    body = functools.partial(_gqa_lat_kernel, n_heads=n_heads, n_kv=n_kv, hd=hd, t_ctx=t_ctx)
    return pl.pallas_call(
        body,
        grid=(n_seq, nq),
        in_specs=[pl.BlockSpec((tq, n_heads * hd), lambda b, i: (qblk0 + b * nq + i, 0)),
                  pl.BlockSpec((t, kvw), lambda b, i: (sblk0 + b, (n_heads * hd) // kvw)),
                  pl.BlockSpec((1, t_ctx, n_kv * hd), lambda b, i: (b, 0, 0)),
                  pl.BlockSpec((1, t_ctx, n_kv * hd), lambda b, i: (b, 0, 0)),
                  pl.BlockSpec((tq, hd), lambda b, i: (i, 0)),
                  pl.BlockSpec((tq, hd), lambda b, i: (i, 0)),
                  pl.BlockSpec((t, hd), lambda b, i: (0, 0)),
                  pl.BlockSpec((t, hd), lambda b, i: (0, 0)),
                  pl.BlockSpec((1, hd), lambda b, i: (0, 0)),
                  pl.BlockSpec((1, hd), lambda b, i: (0, 0))],
        out_specs=pl.BlockSpec((tq, n_heads * hd), lambda b, i: (b * nq + i, 0)),
        out_shape=jax.ShapeDtypeStruct((n_seq * t, n_heads * hd), F32),
        scratch_shapes=[pltpu.VMEM((t_ctx + t, n_kv * hd), BF16),
                        pltpu.VMEM((t_ctx + t, n_kv * hd), BF16)],
        compiler_params=_params(("arbitrary", "arbitrary")),
        name="gqa_latent",
    )(proj, proj, cache_k, cache_v, cosd, sind, cosd, sind, qn.reshape(1, hd), kn.reshape(1, hd))


def _mha_ctx_kernel(q_ref, k_ref, v_ref, o_ref, *, hd):
    scale = hd ** -0.5
    for h in range(q_ref.shape[1] // hd):
        sl = slice(h * hd, (h + 1) * hd)
        s = _bdot_nt(q_ref[:, sl], k_ref[:, sl]) * scale
        (e,), den = _softmax_rows([s])
        o_ref[:, sl] = _bdot(e / den, v_ref[:, sl])


def _mha_ctx_call(proj, n_seq, t):
    hd, cw = 64, 128
    d = proj.shape[1] // 3
    ncb = d // cw
    body = functools.partial(_mha_ctx_kernel, hd=hd)
    return pl.pallas_call(
        body,
        grid=(n_seq, ncb),
        in_specs=[pl.BlockSpec((t, cw), lambda b, c: (b, c)),
                  pl.BlockSpec((t, cw), lambda b, c: (b, ncb + c)),
                  pl.BlockSpec((t, cw), lambda b, c: (b, 2 * ncb + c))],
        out_specs=pl.BlockSpec((t, cw), lambda b, c: (b, c)),
        out_shape=jax.ShapeDtypeStruct((n_seq * t, d), F32),
        compiler_params=_params(("arbitrary", "arbitrary")),
        name="mha_ctx",
    )(proj, proj, proj)


def _na_bias_kernel(rpb_ref, o_ref, *, n_rel_rows, n_rel_cols):
    h = pl.program_id(0)
    w_io = lax.broadcasted_iota(I32, (GRID_W, 2 * GRID_W), 0)
    lane = lax.broadcasted_iota(I32, (GRID_W, 2 * GRID_W), 1)
    ck = jnp.where(lane < GRID_W, lane, lane - GRID_W)
    c_start = jnp.clip(w_io - NA_COLS // 2, 0, GRID_W - NA_COLS)
    in_win = (ck >= c_start) & (ck < c_start + NA_COLS)
    rel = ck - w_io + (NA_COLS - 1)
    base = h * (n_rel_rows * n_rel_cols)
    tiles = []
    for j in range(n_rel_rows):
        acc = jnp.zeros((GRID_W, 2 * GRID_W), F32)
        for jj in range(n_rel_cols):
            acc = jnp.where(rel == jj, rpb_ref[base + j * n_rel_cols + jj], acc)
        tiles.append(jnp.where(in_win, acc, NEG_BIG))
    for j in range(n_rel_rows):
        hi = tiles[j + 1] if j + 1 < n_rel_rows else jnp.full((GRID_W, 2 * GRID_W), NEG_BIG, F32)
        o_ref[0, j] = jnp.where(lane < GRID_W, tiles[j], hi)


def _na_bias_call(rpb):
    n_heads, nrr, nrc = rpb.shape
    body = functools.partial(_na_bias_kernel, n_rel_rows=nrr, n_rel_cols=nrc)
    return pl.pallas_call(
        body,
        grid=(n_heads,),
        in_specs=[pl.BlockSpec(memory_space=pltpu.SMEM)],
        out_specs=pl.BlockSpec((1, nrr, GRID_W, 2 * GRID_W), lambda h: (h, 0, 0, 0)),
        out_shape=jax.ShapeDtypeStruct((n_heads, nrr, GRID_W, 2 * GRID_W), F32),
        compiler_params=_params(("arbitrary",)),
        name="na_bias",
    )(rpb.reshape(-1))


NA_QROWS = 4
NA_KROWS = 12


def _na_kernel(q_ref, k0_ref, k1_ref, k2_ref, v0_ref, v1_ref, v2_ref, kc_ref, vc_ref, tz_ref, o_ref,
               *, hd, n_grid_rows):
    scale = hd ** -0.5
    blk = pl.program_id(1)
    kstart = jnp.clip(blk * NA_QROWS - NA_ROWS // 2, 0, n_grid_rows - NA_KROWS)
    lane = lax.broadcasted_iota(I32, (GRID_W, 2 * GRID_W), 1)
    n_rel = tz_ref.shape[1]
    for hh in range(q_ref.shape[1] // hd):
        sl = slice(hh * hd, (hh + 1) * hd)
        rows = []
        for rq_l in range(NA_QROWS):
            rq = blk * NA_QROWS + rq_l
            r_start = jnp.clip(rq - NA_ROWS // 2, 0, n_grid_rows - NA_ROWS)
            tiles = []
            for m in range(NA_KROWS // 2):
                rk = kstart + 2 * m
                j = jnp.clip(rk - rq + (NA_ROWS - 1), 0, n_rel - 1)
                ok0 = (rk >= r_start) & (rk < r_start + NA_ROWS)
                ok1 = (rk + 1 >= r_start) & (rk + 1 < r_start + NA_ROWS)
                pen = jnp.where(lane < GRID_W, jnp.where(ok0, 0.0, NEG_BIG), jnp.where(ok1, 0.0, NEG_BIG))
                tiles.append(tz_ref[hh, pl.ds(j, 1)][0] + pen)
            rows.append(jnp.concatenate(tiles, axis=-1))
        bias = jnp.concatenate(rows, axis=0)
        q = q_ref[:, sl]
        k_loc = jnp.concatenate([k0_ref[:, sl], k1_ref[:, sl], k2_ref[:, sl]], axis=0)
        v_loc = jnp.concatenate([v0_ref[:, sl], v1_ref[:, sl], v2_ref[:, sl]], axis=0)
        s_loc = _bdot_nt(q, k_loc) * scale + bias
        s_ctx = _bdot_nt(q, kc_ref[0, :, sl]) * scale
        (e_loc, e_ctx), den = _softmax_rows([s_loc, s_ctx])
        o_ref[:, sl] = (_bdot(e_loc, v_loc) + _bdot(e_ctx, vc_ref[0, :, sl])) / den


def _na_call(proj, row0, n_seq, t, cache_k, cache_v, tz):
    hd, cw = 64, 128
    d = proj.shape[1] // 3
    ncb = d // cw
    tq = NA_QROWS * GRID_W
    nq = t // tq
    n_grid_rows = t // GRID_W
    t_ctx = cache_k.shape[1]
    qblk0 = row0 // tq

    def kv_map(which, j):
        def index_map(b, i, c):
            ks = jnp.clip(i * NA_QROWS - NA_ROWS // 2, 0, n_grid_rows - NA_KROWS) // NA_QROWS
            return (qblk0 + b * nq + ks + j, which * ncb + c)
        return index_map

    body = functools.partial(_na_kernel, hd=hd, n_grid_rows=n_grid_rows)
    return pl.pallas_call(
        body,
        grid=(n_seq, nq, ncb),
        in_specs=[pl.BlockSpec((tq, cw), lambda b, i, c: (qblk0 + b * nq + i, c))]
                 + [pl.BlockSpec((tq, cw), kv_map(1, j)) for j in range(3)]
                 + [pl.BlockSpec((tq, cw), kv_map(2, j)) for j in range(3)]
                 + [pl.BlockSpec((1, t_ctx, cw), lambda b, i, c: (b, 0, c)),
                    pl.BlockSpec((1, t_ctx, cw), lambda b, i, c: (b, 0, c)),
                    pl.BlockSpec((cw // hd,) + tz.shape[1:], lambda b, i, c: (c, 0, 0, 0))],
        out_specs=pl.BlockSpec((tq, cw), lambda b, i, c: (b * nq + i, c)),
        out_shape=jax.ShapeDtypeStruct((n_seq * t, d), F32),
        compiler_params=_params(("arbitrary", "arbitrary", "arbitrary")),
        name="nbr_attn",
    )(proj, proj, proj, proj, proj, proj, proj, cache_k, cache_v, tz)


def _hgrn_kernel(q_ref, v_ref, f_ref, bf_ref, lb_ref, s0_ref, o_ref, s_ref, st_scr,
                 *, reverse, n_heads, dk, has_s0, n_blk):
    c = pl.program_id(1)
    L = GLA_CHUNK
    tb = q_ref.shape[0]

    @pl.when(c == 0)
    def _():
        for h in range(n_heads):
            if has_s0:
                st_scr[h] = s0_ref[0, 0, h].T
            else:
                st_scr[h] = jnp.zeros_like(st_scr[h])

    row = lax.broadcasted_iota(I32, (L, L), 0)
    col = lax.broadcasted_iota(I32, (L, L), 1)
    tri = jnp.where((col >= row) if reverse else (col <= row), 1.0, 0.0).astype(F32)
    lb = lb_ref[0]
    bf = bf_ref[0]
    n_chunks = tb // L

    def chunk_step(jj, carry):
        jc = (n_chunks - 1 - jj) if reverse else jj
        rs = pl.ds(pl.multiple_of(jc * L, L), L)
        q = _silu(q_ref[rs, :]) * (dk ** -0.5)
        v = v_ref[rs, :]
        f = lb + (1.0 - lb) * _sigmoid(f_ref[rs, :] + bf)
        logf = jnp.log(f)
        kk = 1.0 - f
        cum = _fdot(tri, logf)
        end = cum[0:1, :] if reverse else cum[L - 1:L, :]
        qd = q * jnp.exp(cum)
        kd = kk * jnp.exp(end - cum)
        e_end = jnp.exp(end)
        acc = []
        for h in range(n_heads):
            hs = slice(h * dk, (h + 1) * dk)
            acc.append(_bdot_nt(qd[:, hs], st_scr[h]))
        tio = lax.broadcasted_iota(I32, (L, 1), 0)
        for s in range(L):
            g8 = (s // 8) * 8
            r0, r1 = (0, g8 + 8) if reverse else (g8, L)
            keep = (tio[r0:r1] <= s) if reverse else (tio[r0:r1] >= s)
            arg = jnp.where(keep, cum[r0:r1, :] - cum[s:s + 1, :], -jnp.inf)
            w = jnp.exp(arg) * q[r0:r1, :] * kk[s:s + 1, :]
            for h in range(n_heads):
                hs = slice(h * dk, (h + 1) * dk)
                a = jnp.sum(w[:, hs], axis=-1, keepdims=True)
                upd = a * v[s:s + 1, hs]
                if r0 == 0 and r1 == L:
                    acc[h] = acc[h] + upd
                elif r0 == 0:
                    acc[h] = jnp.concatenate([acc[h][:r1] + upd, acc[h][r1:]], axis=0)
                else:
                    acc[h] = jnp.concatenate([acc[h][:r0], acc[h][r0:] + upd], axis=0)
        o_ref[rs, :] = jnp.concatenate(acc, axis=-1)
        for h in range(n_heads):
            hs = slice(h * dk, (h + 1) * dk)
            st_scr[h] = st_scr[h] * e_end[:, hs] + _bdot_tn(v[:, hs], kd[:, hs])
        return carry

    lax.fori_loop(0, n_chunks, chunk_step, 0)

    @pl.when(c == n_blk - 1)
    def _():
        for h in range(n_heads):
            s_ref[0, 0, h] = st_scr[h].T


def _hgrn_call(proj, row0, n_seq, t, b_f, lb, s0, reverse):
    n_heads, dk = 8, 128
    d = n_heads * dk
    tb = 256
    n_blk = t // tb
    blk0 = row0 // tb
    di = 1 if reverse else 0
    has_s0 = s0 is not None
    if not has_s0:
        s0 = jnp.zeros((1, 2, n_heads, dk, dk), F32)

    def tok(b, c):
        return blk0 + b * n_blk + ((n_blk - 1 - c) if reverse else c)

    body = functools.partial(_hgrn_kernel, reverse=reverse, n_heads=n_heads, dk=dk,
                             has_s0=has_s0, n_blk=n_blk)
    return pl.pallas_call(
        body,
        grid=(n_seq, n_blk),
        in_specs=[pl.BlockSpec((tb, d), lambda b, c: (tok(b, c), 0)),
                  pl.BlockSpec((tb, d), lambda b, c: (tok(b, c), 1)),
                  pl.BlockSpec((tb, d), lambda b, c: (tok(b, c), 2 + di)),
                  pl.BlockSpec((1, 1, d), lambda b, c: (di, 0, 0)),
                  pl.BlockSpec((1, 1, d), lambda b, c: (di, 0, 0)),
                  pl.BlockSpec((1, 1, n_heads, dk, dk),
                               (lambda b, c: (b, di, 0, 0, 0)) if has_s0 else (lambda b, c: (0, 0, 0, 0, 0)))],
        out_specs=[pl.BlockSpec((tb, d), lambda b, c: (tok(b, c) - blk0, 0)),
                   pl.BlockSpec((1, 1, n_heads, dk, dk), lambda b, c: (b, 0, 0, 0, 0))],
        out_shape=[jax.ShapeDtypeStruct((n_seq * t, d), F32),
                   jax.ShapeDtypeStruct((n_seq, 1, n_heads, dk, dk), F32)],
        scratch_shapes=[pltpu.VMEM((n_heads, dk, dk), F32)],
        compiler_params=_params(("arbitrary", "arbitrary")),
        name="hgrn2_bw" if reverse else "hgrn2_fw",
    )(proj, proj, proj, b_f.reshape(2, 1, d), lb.reshape(2, 1, d), s0)


def _log_sigmoid(x):
    return jnp.minimum(x, 0.0) - jnp.log(1.0 + jnp.exp(-jnp.abs(x)))


def _mlstm_kernel(q_ref, k_ref, v_ref, g_ref, bg_ref, c0_ref, n0_ref, m0_ref,
                  o_ref, c_out, n_out, m_out, c_scr, n_scr, m_scr,
                  *, reverse, n_heads, dqk, dv, has_state, n_blk):
    c = pl.program_id(1)
    L = q_ref.shape[0]
    i_off = 2 * n_heads if reverse else 0
    f_off = i_off + n_heads

    @pl.when(c == 0)
    def _():
        if has_state:
            c_scr[...] = c0_ref[0, 0]
            n_scr[...] = n0_ref[0, 0]
            m_scr[...] = m0_ref[0, 0]
        else:
            c_scr[...] = jnp.zeros_like(c_scr)
            n_scr[...] = jnp.zeros_like(n_scr)
            m_scr[...] = jnp.zeros_like(m_scr)

    gates = GATE_SOFTCAP * jnp.tanh((g_ref[...] + bg_ref[...]) / GATE_SOFTCAP)
    logf = _log_sigmoid(gates)
    row = lax.broadcasted_iota(I32, (L, L), 0)
    col = lax.broadcasted_iota(I32, (L, L), 1)
    causal = (col >= row) if reverse else (col <= row)
    tri = jnp.where(causal, 1.0, 0.0).astype(F32)
    cum = _fdot(tri, logf)
    cum_t = cum.T
    gates_t = gates.T
    e_row = 0 if reverse else L - 1
    outs = []
    for h in range(n_heads):
        qh = q_ref[:, h * dqk:(h + 1) * dqk] * (dqk ** -0.5)
        kh = k_ref[:, h * dqk:(h + 1) * dqk]
        vh = v_ref[:, h * dv:(h + 1) * dv]
        cum_c = cum[:, f_off + h:f_off + h + 1]
        cum_r = cum_t[f_off + h:f_off + h + 1, :]
        i_c = gates[:, i_off + h:i_off + h + 1]
        i_r = gates_t[i_off + h:i_off + h + 1, :]
        m_prev = m_scr[0:1, h:h + 1]
        d = jnp.where(causal, cum_c - cum_r + i_r, -jnp.inf)
        m_inter = cum_c + m_prev
        m_t = jnp.maximum(m_inter, d.max(axis=-1, keepdims=True))
        w_inter = jnp.exp(m_inter - m_t)
        w_intra = jnp.exp(d - m_t)
        qk = _bdot_nt(qh, kh) * w_intra
        num = w_inter * _bdot(qh, c_scr[h]) + _bdot(qk, vh)
        den = w_inter * jnp.sum(qh * n_scr[h], axis=-1, keepdims=True) + qk.sum(axis=-1, keepdims=True)
        outs.append(num / jnp.maximum(jnp.abs(den), jnp.exp(-m_t)))
        end = cum_c[e_row:e_row + 1, :]
        g_end_r = end - cum_r + i_r
        g_end_c = end - cum_c + i_c
        m_new = jnp.maximum(end + m_prev, g_end_r.max(axis=-1, keepdims=True))
        w_old = jnp.exp(end + m_prev - m_new)
        ks = kh * jnp.exp(g_end_c - m_new)
        c_scr[h] = w_old * c_scr[h] + _bdot_tn(ks, vh)
        n_scr[h] = w_old * n_scr[h] + ks.sum(axis=0, keepdims=True)
        m_scr[0:1, h:h + 1] = m_new
    o_ref[...] = jnp.concatenate(outs, axis=-1)

    @pl.when(c == n_blk - 1)
    def _():
        c_out[0, 0] = c_scr[...]
        n_out[0, 0] = n_scr[...]
        m_out[0, 0] = m_scr[...]


def _mlstm_call(proj, row0, n_seq, t, b_gates_pad, state, reverse):
    n_heads, dqk, dv = 8, 64, 128
    wq, wv = n_heads * dqk, n_heads * dv
    L = 256
    n_blk = t // L
    blk0 = row0 // L
    di = 1 if reverse else 0
    has_state = state is not None
    if has_state:
        c0, n0, m0 = state
        smap = lambda b, c: (b, di, 0, 0, 0)
        mmap = lambda b, c: (b, di, 0, 0)
    else:
        c0 = jnp.zeros((1, 1, n_heads, dqk, dv), F32)
        n0 = jnp.zeros((1, 1, n_heads, 1, dqk), F32)
        m0 = jnp.zeros((1, 1, 1, 128), F32)
        smap = lambda b, c: (0, 0, 0, 0, 0)
        mmap = lambda b, c: (0, 0, 0, 0)

    def tok(b, c):
        return blk0 + b * n_blk + ((n_blk - 1 - c) if reverse else c)

    body = functools.partial(_mlstm_kernel, reverse=reverse, n_heads=n_heads, dqk=dqk, dv=dv,
                             has_state=has_state, n_blk=n_blk)
    gate_cb = (2 * wq + 2 * wv) // 128
    return pl.pallas_call(
        body,
        grid=(n_seq, n_blk),
        in_specs=[pl.BlockSpec((L, wq), lambda b, c: (tok(b, c), 0)),
                  pl.BlockSpec((L, wq), lambda b, c: (tok(b, c), 1)),
                  pl.BlockSpec((L, wv), lambda b, c: (tok(b, c), (2 * wq) // wv)),
                  pl.BlockSpec((L, 128), lambda b, c: (tok(b, c), gate_cb)),
                  pl.BlockSpec((1, 128), lambda b, c: (0, 0)),
                  pl.BlockSpec((1, 1, n_heads, dqk, dv), smap),
                  pl.BlockSpec((1, 1, n_heads, 1, dqk), smap),
                  pl.BlockSpec((1, 1, 1, 128), mmap)],
        out_specs=[pl.BlockSpec((L, wv), lambda b, c: (tok(b, c) - blk0, 0)),
                   pl.BlockSpec((1, 1, n_heads, dqk, dv), lambda b, c: (b, 0, 0, 0, 0)),
                   pl.BlockSpec((1, 1, n_heads, 1, dqk), lambda b, c: (b, 0, 0, 0, 0)),
                   pl.BlockSpec((1, 1, 1, 128), lambda b, c: (b, 0, 0, 0))],
        out_shape=[jax.ShapeDtypeStruct((n_seq * t, wv), F32),
                   jax.ShapeDtypeStruct((n_seq, 1, n_heads, dqk, dv), F32),
                   jax.ShapeDtypeStruct((n_seq, 1, n_heads, 1, dqk), F32),
                   jax.ShapeDtypeStruct((n_seq, 1, 1, 128), F32)],
        scratch_shapes=[pltpu.VMEM((n_heads, dqk, dv), F32),
                        pltpu.VMEM((n_heads, 1, dqk), F32),
                        pltpu.VMEM((1, 128), F32)],
        compiler_params=_params(("arbitrary", "arbitrary")),
        name="mlstm_bw" if reverse else "mlstm_fw",
    )(proj, proj, proj, proj, b_gates_pad, c0, n0, m0)


def _moe_input(x_ref, m_ref, g_ref):
    return _rms(x_ref[...], g_ref[...]) * (1.0 + m_ref[0, 4:5, :]) + m_ref[0, 3:4, :]


def _route_kernel(x_ref, m_ref, g_ref, wr_ref, br_ref, idx_ref, gate_ref, rank_ref, cnt_ref, carry_scr,
                  *, n_experts):
    i = pl.program_id(0)
    tm = x_ref.shape[0]

    @pl.when(i == 0)
    def _():
        carry_scr[...] = jnp.zeros_like(carry_scr)

    h = _moe_input(x_ref, m_ref, g_ref)
    logits = lax.dot_general(wr_ref[...], h, (((1,), (1,)), ((), ())), preferred_element_type=F32,
                             precision=lax.Precision.HIGHEST) + br_ref[...]
    e_io = lax.broadcasted_iota(I32, (n_experts, tm), 0).astype(F32)
    work = logits
    vals, idxs = [], []
    chosen = jnp.zeros((n_experts, tm), F32)
    for _ in range(TOP_K):
        mx = work.max(axis=0, keepdims=True)
        ix = jnp.min(jnp.where(work == mx, e_io, float(n_experts)), axis=0, keepdims=True)
        hit = e_io == ix
        vals.append(mx)
        idxs.append(ix)
        chosen = jnp.where(hit, 1.0, chosen)
        work = jnp.where(hit, -jnp.inf, work)
    es = [jnp.exp(v - vals[0]) for v in vals]
    den = es[0] + es[1] + es[2] + es[3]
    srow = lax.broadcasted_iota(I32, (tm, tm), 0)
    scol = lax.broadcasted_iota(I32, (tm, tm), 1)
    before = jnp.where(srow < scol, 1.0, 0.0).astype(BF16)
    pos = jnp.dot(chosen.astype(BF16), before, preferred_element_type=F32) + carry_scr[...]
    ranks = [jnp.sum(jnp.where(e_io == ix, pos, 0.0), axis=0, keepdims=True) for ix in idxs]
    carry_scr[...] = carry_scr[...] + chosen.sum(axis=1, keepdims=True)
    idx_ref[...] = jnp.concatenate(idxs, axis=0).astype(I32)
    gate_ref[...] = jnp.concatenate([e / den for e in es], axis=0)
    rank_ref[...] = jnp.concatenate(ranks, axis=0).astype(I32)
    cnt_ref[...] = jnp.broadcast_to(carry_scr[...], cnt_ref.shape).astype(I32)


def _route_call(x, modsel, g, w_router, b_router):
    n, d = x.shape
    n_experts = w_router.shape[1]
    tm = TOKEN_TILE
    body = functools.partial(_route_kernel, n_experts=n_experts)
    return pl.pallas_call(
        body,
        grid=(n // tm,),
        in_specs=[pl.BlockSpec((tm, d), lambda i: (i, 0)),
                  pl.BlockSpec((1, 6, d), lambda i: (i, 0, 0)),
                  pl.BlockSpec((1, d), lambda i: (0, 0)),
                  pl.BlockSpec((n_experts, d), lambda i: (0, 0)),
                  pl.BlockSpec((n_experts, 1), lambda i: (0, 0))],
        out_specs=[pl.BlockSpec((TOP_K, tm), lambda i: (0, i)),
                   pl.BlockSpec((TOP_K, tm), lambda i: (0, i)),
                   pl.BlockSpec((TOP_K, tm), lambda i: (0, i)),
                   pl.BlockSpec((n_experts, 128), lambda i: (0, 0))],
        out_shape=[jax.ShapeDtypeStruct((TOP_K, n), I32),
                   jax.ShapeDtypeStruct((TOP_K, n), F32),
                   jax.ShapeDtypeStruct((TOP_K, n), I32),
                   jax.ShapeDtypeStruct((n_experts, 128), I32)],
        scratch_shapes=[pltpu.VMEM((n_experts, 1), F32)],
        compiler_params=_params(("arbitrary",)),
        name="moe_route",
    )(x, modsel, g.reshape(1, d), w_router.T, b_router.reshape(n_experts, 1))


def _slot_kernel(cnt_ref, idx_ref, rank_ref, dest_ref, bexp_ref, nused_ref, *, n_experts, n_blocks):
    cnt = cnt_ref[:, 0:1].astype(F32)
    padded = jnp.ceil(cnt * (1.0 / MOE_BLOCK)) * MOE_BLOCK
    er = lax.broadcasted_iota(I32, (n_experts, n_experts), 0)
    ec = lax.broadcasted_iota(I32, (n_experts, n_experts), 1)
    start_row = jnp.sum(jnp.where(er < ec, padded, 0.0), axis=0, keepdims=True)
    start_col = jnp.sum(jnp.where(er == ec, start_row, 0.0), axis=1, keepdims=True)
    end_col = start_col + padded
    idx = idx_ref[...]
    e_io = lax.broadcasted_iota(I32, (n_experts,) + idx.shape[1:], 0)
    rows = []
    for k in range(TOP_K):
        hit = e_io == idx[k:k + 1, :]
        rows.append(jnp.sum(jnp.where(hit, start_col, 0.0), axis=0, keepdims=True))
    dest_ref[...] = jnp.concatenate(rows, axis=0).astype(I32) + rank_ref[...]
    blk_start = (lax.broadcasted_iota(I32, (n_experts, n_blocks), 1) * MOE_BLOCK).astype(F32)
    n_done = jnp.sum(jnp.where(end_col <= blk_start, 1.0, 0.0), axis=0, keepdims=True)
    bexp_ref[...] = jnp.minimum(n_done, n_experts - 1.0).astype(I32)
    nused_ref[...] = (jnp.sum(padded, axis=0, keepdims=True) * (1.0 / MOE_BLOCK)).astype(I32)


def _slot_call(counts, idx_t, rank_t, n_blocks):
    n_experts = counts.shape[0]
    n = idx_t.shape[1]
    tn = min(2048, n)
    body = functools.partial(_slot_kernel, n_experts=n_experts, n_blocks=n_blocks)
    return pl.pallas_call(
        body,
        grid=(n // tn,),
        in_specs=[pl.BlockSpec((n_experts, 128), lambda i: (0, 0)),
                  pl.BlockSpec((TOP_K, tn), lambda i: (0, i)),
                  pl.BlockSpec((TOP_K, tn), lambda i: (0, i))],
        out_specs=[pl.BlockSpec((TOP_K, tn), lambda i: (0, i)),
                   pl.BlockSpec((1, n_blocks), lambda i: (0, 0)),
                   pl.BlockSpec((1, 1), lambda i: (0, 0))],
        out_shape=[jax.ShapeDtypeStruct((TOP_K, n), I32),
                   jax.ShapeDtypeStruct((1, n_blocks), I32),
                   jax.ShapeDtypeStruct((1, 1), I32)],
        compiler_params=_params(("arbitrary",)),
        name="moe_slots",
    )(counts, idx_t, rank_t)


LANES = 128
ROW_SUBLANES = 8
DMA_ISSUE_UNROLL = 8


def _to_row_tiles(ref, base, x):
    rows = x.shape[0]
    for c in range(ROW_SUBLANES):
        ref[pl.ds(base * ROW_SUBLANES + c, rows, stride=ROW_SUBLANES), :] = x[:, c * LANES:(c + 1) * LANES]


def _from_row_tiles(ref, base, rows, c):
    return ref[pl.ds(base * ROW_SUBLANES + c, rows, stride=ROW_SUBLANES), :]


def _row_tile(ref, r):
    return ref.at[pl.ds(pl.multiple_of(r * ROW_SUBLANES, ROW_SUBLANES), ROW_SUBLANES)]


def _dispatch_kernel(dest_ref, x_ref, m_ref, g_ref, xs_in_ref, xs_ref, h_scr, sem):
    del xs_in_ref
    tm = x_ref.shape[0]
    _to_row_tiles(h_scr, 0, _moe_input(x_ref, m_ref, g_ref))

    def start(j, carry):
        r = lax.rem(j, tm)
        pltpu.make_async_copy(_row_tile(h_scr, r), _row_tile(xs_ref, dest_ref[0, 0, j]), sem.at[0]).start()
        return carry

    lax.fori_loop(0, TOP_K * tm, start, 0, unroll=DMA_ISSUE_UNROLL)
    for _ in range(TOP_K):
        pltpu.make_async_copy(h_scr, xs_ref.at[pl.ds(0, tm * ROW_SUBLANES)], sem.at[0]).wait()


def _dispatch_call(dest_tiles, x, modsel, g, xs_zero):
    n, d = x.shape
    tm = TOKEN_TILE
    assert d == ROW_SUBLANES * LANES
    return pl.pallas_call(
        _dispatch_kernel,
        grid=(n // tm,),
        in_specs=[pl.BlockSpec((1, 1, TOP_K * tm), lambda i: (i, 0, 0), memory_space=pltpu.SMEM),
                  pl.BlockSpec((tm, d), lambda i: (i, 0)),
                  pl.BlockSpec((1, 6, d), lambda i: (i, 0, 0)),
                  pl.BlockSpec((1, d), lambda i: (0, 0)),
                  pl.BlockSpec(memory_space=pl.ANY)],
        out_specs=pl.BlockSpec(memory_space=pl.ANY),
        out_shape=jax.ShapeDtypeStruct(xs_zero.shape, F32),
        scratch_shapes=[pltpu.VMEM((tm * ROW_SUBLANES, LANES), F32), pltpu.SemaphoreType.DMA((1,))],
        input_output_aliases={4: 0},
        compiler_params=_params(("arbitrary",)),
        name="moe_dispatch",
    )(dest_tiles, x, modsel, g.reshape(1, d), xs_zero)


def _ffn_kernel(bexp_ref, nused_ref, xs_ref, wgu_ref, bgu_ref, wdn_ref, bdn_ref, ys_ref, wgu_scr, wdn_scr):
    b = pl.program_id(0)
    d_ff = wdn_ref.shape[2]
    rows = xs_ref.shape[0] // ROW_SUBLANES

    @pl.when(b < nused_ref[0])
    def _():
        prev = bexp_ref[jnp.maximum(b - 1, 0)]

        @pl.when((b == 0) | (bexp_ref[b] != prev))
        def _():
            wgu_scr[...] = wgu_ref[0, 0].astype(BF16)
            wdn_scr[...] = wdn_ref[0, 0].astype(BF16)

        x = jnp.concatenate([_from_row_tiles(xs_ref, 0, rows, c).astype(BF16) for c in range(ROW_SUBLANES)],
                            axis=-1)
        gu = jnp.dot(x, wgu_scr[...], preferred_element_type=F32) + bgu_ref[0, 0]
        x_glu = jnp.minimum(gu[:, :d_ff], SWIGLU_LIMIT)
        x_lin = jnp.clip(gu[:, d_ff:], -SWIGLU_LIMIT, SWIGLU_LIMIT)
        hid = x_glu * _sigmoid(SWIGLU_ALPHA * x_glu) * (x_lin + 1.0)
        _to_row_tiles(ys_ref, 0, jnp.dot(hid.astype(BF16), wdn_scr[...], preferred_element_type=F32)
                      + bdn_ref[0, 0])

    @pl.when(b >= nused_ref[0])
    def _():
        ys_ref[...] = jnp.zeros_like(ys_ref)


def _ffn_call(layer, block_expert, n_used, xs, w_gu, b_gu, w_dn, b_dn):
    depth, n_experts, d, d_ff2 = w_gu.shape
    d_ff = d_ff2 // 2
    blk_rows = MOE_BLOCK * ROW_SUBLANES
    n_blocks = xs.shape[0] // blk_rows

    def blk(b, be, nu):
        return jnp.minimum(b, nu[0] - 1)

    grid_spec = pltpu.PrefetchScalarGridSpec(
        num_scalar_prefetch=2,
        grid=(n_blocks,),
        in_specs=[pl.BlockSpec((blk_rows, LANES), lambda b, be, nu: (blk(b, be, nu), 0)),
                  pl.BlockSpec((1, 1, d, d_ff2), lambda b, be, nu: (layer, be[blk(b, be, nu)], 0, 0)),
                  pl.BlockSpec((1, 1, 1, d_ff2), lambda b, be, nu: (layer, be[blk(b, be, nu)], 0, 0)),
                  pl.BlockSpec((1, 1, d_ff, d), lambda b, be, nu: (layer, be[blk(b, be, nu)], 0, 0)),
                  pl.BlockSpec((1, 1, 1, d), lambda b, be, nu: (layer, be[blk(b, be, nu)], 0, 0))],
        out_specs=pl.BlockSpec((blk_rows, LANES), lambda b, be, nu: (b, 0)),
        scratch_shapes=[pltpu.VMEM((d, d_ff2), BF16), pltpu.VMEM((d_ff, d), BF16)],
    )
    return pl.pallas_call(
        _ffn_kernel,
        grid_spec=grid_spec,
        out_shape=jax.ShapeDtypeStruct(xs.shape, F32),
        compiler_params=_params(("arbitrary",)),
        name="moe_ffn",
    )(block_expert, n_used, xs, w_gu, b_gu.reshape(depth, n_experts, 1, d_ff2), w_dn,
      b_dn.reshape(depth, n_experts, 1, d))


def _combine_kernel(dest_ref, ys_ref, gate_ref, x_ref, m_ref, fg_ref, y_ref, buf, sem, *, final_norm):
    tm = x_ref.shape[0]

    def start(j, carry):
        pltpu.make_async_copy(_row_tile(ys_ref, dest_ref[0, 0, j]), _row_tile(buf, j), sem.at[0]).start()
        return carry

    lax.fori_loop(0, TOP_K * tm, start, 0, unroll=DMA_ISSUE_UNROLL)
    pltpu.make_async_copy(ys_ref.at[pl.ds(0, TOP_K * tm * ROW_SUBLANES)], buf, sem.at[0]).wait()
    chunks = []
    for c in range(ROW_SUBLANES):
        y = gate_ref[:, 0:1] * _from_row_tiles(buf, 0, tm, c)
        for k in range(1, TOP_K):
            y = y + gate_ref[:, k:k + 1] * _from_row_tiles(buf, k * tm, tm, c)
        chunks.append(y)
    out = x_ref[...] + m_ref[0, 5:6, :] * jnp.concatenate(chunks, axis=-1)
    if final_norm:
        out = _rms(out, fg_ref[...])
    y_ref[...] = out


def _combine_call(dest_tiles, ys, gates_nk, x, modsel, final_g, final_norm):
    n, d = x.shape
    tm = TOKEN_TILE
    body = functools.partial(_combine_kernel, final_norm=final_norm)
    return pl.pallas_call(
        body,
        grid=(n // tm,),
        in_specs=[pl.BlockSpec((1, 1, TOP_K * tm), lambda i: (i, 0, 0), memory_space=pltpu.SMEM),
                  pl.BlockSpec(memory_space=pl.ANY),
                  pl.BlockSpec((tm, TOP_K), lambda i: (i, 0)),
                  pl.BlockSpec((tm, d), lambda i: (i, 0)),
                  pl.BlockSpec((1, 6, d), lambda i: (i, 0, 0)),
                  pl.BlockSpec((1, d), lambda i: (0, 0))],
        out_specs=pl.BlockSpec((tm, d), lambda i: (i, 0)),
        out_shape=jax.ShapeDtypeStruct((n, d), F32),
        scratch_shapes=[pltpu.VMEM((TOP_K * tm * ROW_SUBLANES, LANES), F32), pltpu.SemaphoreType.DMA((1,))],
        compiler_params=_params(("arbitrary",)),
        name="moe_combine",
    )(dest_tiles, ys, gates_nk, x, modsel, final_g.reshape(1, d))


def _moe_layer(layer, x, modsel, g2, w_router, b_router, w_gu, b_gu, w_dn, b_dn, final_g, final_norm):
    n, d = x.shape
    n_experts = w_router.shape[1]
    tm = TOKEN_TILE
    n_blocks = (n * TOP_K) // MOE_BLOCK + n_experts
    idx_t, gate_t, rank_t, counts = _route_call(x, modsel, g2, w_router, b_router)
    dest_t, block_expert, n_used = _slot_call(counts, idx_t, rank_t, n_blocks)
    dest_tiles = dest_t.reshape(TOP_K, n // tm, tm).transpose(1, 0, 2).reshape(n // tm, 1, TOP_K * tm)
    xs = _dispatch_call(dest_tiles, x, modsel, g2,
                        jnp.zeros((n_blocks * MOE_BLOCK * ROW_SUBLANES, LANES), F32))
    ys = _ffn_call(layer, block_expert.reshape(n_blocks), n_used.reshape(1), xs, w_gu, b_gu, w_dn, b_dn)
    return _combine_call(dest_tiles, ys, gate_t.T, x, modsel, final_g, final_norm)


def _rope_tables(t, hd):
    pos = np.arange(t)
    n_freq = hd // 4
    inv_freq = ROPE_THETA ** (-np.arange(n_freq, dtype=np.float32) / n_freq)
    ang = np.concatenate([(pos // GRID_W).astype(np.float32)[:, None] * inv_freq,
                          (pos % GRID_W).astype(np.float32)[:, None] * inv_freq], axis=-1)
    ang = jnp.asarray(ang, F32)
    cos, sin = jnp.cos(ang), jnp.sin(ang)
    return jnp.concatenate([cos, cos], axis=-1), jnp.concatenate([-sin, sin], axis=-1)


def kernel(x_prompt, x_sample, cache_k_a, cache_v_a, state_b, state_c_C, state_c_n, state_c_m, cache_k_d, cache_v_d, c, c_ctx, norm1_g, norm2_g, w_mod, b_mod, w_in_a, qnorm_a, knorm_a, w_out_a, w_in_b, b_f_b, lower_bounds_b, onorm_b, w_out_b, w_in_c, b_gates_c, onorm_c, w_out_c, w_in_d, rpb_d, w_out_d, w_router, b_router, w_gu, b_gu, w_dn, b_dn, final_g):
    n_ctx_seq, t_ctx, d = x_prompt.shape
    n_lat_seq, t_lat, _ = x_sample.shape
    depth = w_mod.shape[0]
    n_ctx = n_ctx_seq * t_ctx
    n_lat = n_lat_seq * t_lat
    n = n_ctx + n_lat
    tm = TOKEN_TILE
    assert t_ctx % tm == 0 and t_lat % tm == 0 and n_lat_seq + 1 <= 8

    lb_cum = jnp.cumsum(jax.nn.softmax(lower_bounds_b.astype(F32), axis=0), axis=0)
    lb_all = lb_cum - lb_cum[0]

    cond8 = jnp.zeros((8, d), F32).at[0].set(c_ctx).at[1:1 + n_lat_seq].set(c)
    mod = _mod_call(cond8, w_mod, b_mod)
    tile_row = np.concatenate([np.zeros(n_ctx // tm, np.int32),
                               1 + np.repeat(np.arange(n_lat_seq, dtype=np.int32), t_lat // tm)])

    x = jnp.concatenate([x_prompt.reshape(n_ctx, d), x_sample.reshape(n_lat, d)], axis=0)
    outs = {}
    for i in range(depth):
        kind = i % 4
        j = i // 4
        modsel = mod[i].reshape(8, 6, d)[tile_row]
        if kind == 0:
            proj = _inproj_call(x, modsel, norm1_g[i], w_in_a[j].astype(BF16))
            o_ctx, k_new = _gqa_ctx_call(proj, qnorm_a[j], knorm_a[j], n_ctx_seq, t_ctx)
            cosd, sind = _rope_tables(t_lat, 128)
            o_lat = _gqa_lat_call(proj, n_ctx, n_lat_seq, t_lat,
                                  cache_k_a[:, j].reshape(n_lat_seq, -1, 256),
                                  cache_v_a[:, j].reshape(n_lat_seq, -1, 256), cosd, sind,
                                  qnorm_a[j], knorm_a[j])
            outs["k_a"] = k_new.reshape(n_ctx_seq, 1, t_ctx, 2, 128)
            outs["v_a"] = proj[:n_ctx, 1280:1536].reshape(n_ctx_seq, 1, t_ctx, 2, 128)
            o = jnp.concatenate([o_ctx, o_lat], axis=0)
            x = _outproj_call("plain", [(o, d, 0)], w_out_a[j].astype(BF16), x, modsel)
        elif kind == 1:
            proj = _inproj_call(x, modsel, norm1_g[i], w_in_b[j].astype(BF16))
            o_dirs, s_dirs = [], []
            for reverse in (False, True):
                oc, sc = _hgrn_call(proj, 0, n_ctx_seq, t_ctx, b_f_b[j], lb_all[i], None, reverse)
                ol, _ = _hgrn_call(proj, n_ctx, n_lat_seq, t_lat, b_f_b[j], lb_all[i], state_b[:, j], reverse)
                o_dirs.append(jnp.concatenate([oc, ol], axis=0))
                s_dirs.append(sc)
            outs["s_b"] = jnp.concatenate(s_dirs, axis=1)[:, None]
            x = _outproj_call("hgrn", [(o_dirs[0], d, 0), (o_dirs[1], d, 0), (proj, d, 4),
                                       (onorm_b[j].reshape(1, 128), None, None)],
                              w_out_b[j].astype(BF16), x, modsel)
        elif kind == 2:
            w_c = jnp.pad(w_in_c[j], ((0, 0), (0, 128 - 32))).astype(BF16)
            bg = jnp.pad(b_gates_c[j].reshape(1, 32), ((0, 0), (0, 128 - 32)))
            proj = _inproj_call(x, modsel, norm1_g[i], w_c)
            state = (state_c_C[:, j], state_c_n[:, j][:, :, :, None, :],
                     jnp.pad(state_c_m[:, j], ((0, 0), (0, 0), (0, 120)))[:, :, None, :])
            o_dirs, st = [], []
            for reverse in (False, True):
                oc, cc, nc, mc = _mlstm_call(proj, 0, n_ctx_seq, t_ctx, bg, None, reverse)
                ol, _, _, _ = _mlstm_call(proj, n_ctx, n_lat_seq, t_lat, bg, state, reverse)
                o_dirs.append(jnp.concatenate([oc, ol], axis=0))
                st.append((cc, nc, mc))
            outs["c_C"] = jnp.concatenate([st[0][0], st[1][0]], axis=1)[:, None]
            outs["c_n"] = jnp.concatenate([st[0][1], st[1][1]], axis=1)[:, None, :, :, 0, :]
            outs["c_m"] = jnp.concatenate([st[0][2], st[1][2]], axis=1)[:, None, :, 0, :8]
            x = _outproj_call("mlstm", [(o_dirs[0], d, 0), (o_dirs[1], d, 0), (proj, d, 2),
                                        (onorm_c[j].reshape(1, 128), None, None)],
                              w_out_c[j].astype(BF16), x, modsel)
        else:
            proj = _inproj_call(x, modsel, norm1_g[i], w_in_d[j].astype(BF16))
            o_ctx = _mha_ctx_call(proj, n_ctx_seq, t_ctx)
            tz = _na_bias_call(rpb_d[j])
            o_lat = _na_call(proj, n_ctx, n_lat_seq, t_lat,
                             cache_k_d[:, j].reshape(n_lat_seq, -1, d),
                             cache_v_d[:, j].reshape(n_lat_seq, -1, d), tz)
            outs["k_d"] = proj[:n_ctx, d:2 * d].reshape(n_ctx_seq, 1, t_ctx, 16, 64)
            outs["v_d"] = proj[:n_ctx, 2 * d:3 * d].reshape(n_ctx_seq, 1, t_ctx, 16, 64)
            o = jnp.concatenate([o_ctx, o_lat], axis=0)
            x = _outproj_call("plain", [(o, d, 0)], w_out_d[j].astype(BF16), x, modsel)
        x = _moe_layer(i, x, modsel, norm2_g[i], w_router[i], b_router[i], w_gu, b_gu, w_dn, b_dn,
                       final_g, final_norm=(i == depth - 1))

    y_prompt = x[:n_ctx].reshape(n_ctx_seq, t_ctx, d)
    y_sample = x[n_ctx:].reshape(n_lat_seq, t_lat, d)
    return (y_prompt, y_sample, outs["k_a"], outs["v_a"], outs["s_b"], outs["c_C"], outs["c_n"], outs["c_m"],
            outs["k_d"], outs["v_d"])
```

```python
import functools

import numpy as np
import jax
import jax.numpy as jnp
from jax import lax
from jax.experimental import pallas as pl
from jax.experimental.pallas import tpu as pltpu

F32 = jnp.float32
BF16 = jnp.bfloat16
I32 = jnp.int32

NORM_EPS = 1e-6
GRID_W = 64
ROPE_THETA = 10000.0
TOP_K = 4
GLA_CHUNK = 32
GATE_SOFTCAP = 15.0
NA_ROWS = 8
NA_COLS = 16
SWIGLU_ALPHA = 1.702
SWIGLU_LIMIT = 7.0
NEG_BIG = -1e30

TOKEN_TILE = 256
MOE_BLOCK = 256
V7X_VMEM_LIMIT = 52 * 1024 * 1024


def _params(sem, vmem=V7X_VMEM_LIMIT):
    return pltpu.CompilerParams(dimension_semantics=sem, vmem_limit_bytes=vmem)


def _bdot(a, b):
    return jnp.dot(a.astype(BF16), b.astype(BF16), preferred_element_type=F32)


def _bdot_nt(a, b):
    return lax.dot_general(a.astype(BF16), b.astype(BF16), (((1,), (1,)), ((), ())),
                           preferred_element_type=F32)


def _bdot_tn(a, b):
    return lax.dot_general(a.astype(BF16), b.astype(BF16), (((0,), (0,)), ((), ())),
                           preferred_element_type=F32)


def _fdot(a, b):
    return jnp.dot(a, b, preferred_element_type=F32, precision=lax.Precision.HIGHEST)


def _rms(x, g):
    return x * lax.rsqrt(jnp.mean(x * x, axis=-1, keepdims=True) + NORM_EPS) * g


def _rms_heads(x, g, n_heads, hd):
    return jnp.concatenate([_rms(x[:, h * hd:(h + 1) * hd], g) for h in range(n_heads)], axis=-1)


def _sigmoid(x):
    return 1.0 / (1.0 + jnp.exp(-x))


def _silu(x):
    return x * _sigmoid(x)


def _softmax_rows(parts):
    m = parts[0].max(axis=-1, keepdims=True)
    for p in parts[1:]:
        m = jnp.maximum(m, p.max(axis=-1, keepdims=True))
    es = [jnp.exp(p - m) for p in parts]
    den = es[0].sum(axis=-1, keepdims=True)
    for e in es[1:]:
        den = den + e.sum(axis=-1, keepdims=True)
    return es, den


def _mod_kernel(c_ref, w_ref, b_ref, o_ref):
    o_ref[0] = _bdot(_silu(c_ref[...]), w_ref[0]) + b_ref[0]


def _mod_call(cond8, w_mod, b_mod):
    depth, d, d6 = w_mod.shape
    tn = 1024
    return pl.pallas_call(
        _mod_kernel,
        grid=(depth, d6 // tn),
        in_specs=[pl.BlockSpec((8, d), lambda i, j: (0, 0)),
                  pl.BlockSpec((1, d, tn), lambda i, j: (i, 0, j)),
                  pl.BlockSpec((1, 1, tn), lambda i, j: (i, 0, j))],
        out_specs=pl.BlockSpec((1, 8, tn), lambda i, j: (i, 0, j)),
        out_shape=jax.ShapeDtypeStruct((depth, 8, d6), F32),
        compiler_params=_params(("arbitrary", "arbitrary")),
        name="adaln_mod",
    )(cond8, w_mod, b_mod.reshape(depth, 1, d6))


def _inproj_kernel(x_ref, m_ref, g_ref, w_ref, o_ref):
    h = _rms(x_ref[...], g_ref[...]) * (1.0 + m_ref[0, 1:2, :]) + m_ref[0, 0:1, :]
    o_ref[...] = _bdot(h, w_ref[...])


def _inproj_call(x, modsel, g, w):
    n, d = x.shape
    wout = w.shape[1]
    tm = TOKEN_TILE
    return pl.pallas_call(
        _inproj_kernel,
        grid=(n // tm,),
        in_specs=[pl.BlockSpec((tm, d), lambda i: (i, 0)),
                  pl.BlockSpec((1, 6, d), lambda i: (i, 0, 0)),
                  pl.BlockSpec((1, d), lambda i: (0, 0)),
                  pl.BlockSpec((d, wout), lambda i: (0, 0))],
        out_specs=pl.BlockSpec((tm, wout), lambda i: (i, 0)),
        out_shape=jax.ShapeDtypeStruct((n, wout), F32),
        compiler_params=_params(("arbitrary",)),
        name="inproj",
    )(x, modsel, g.reshape(1, d), w)


def _outproj_kernel(*refs, kind, n_pairs, n_ctx_tiles, n_heads, hd):
    in_ctx = pl.program_id(0) < n_ctx_tiles
    mix = [jnp.where(in_ctx, refs[2 * p][...], refs[2 * p + 1][...]) for p in range(n_pairs)]
    rest = refs[2 * n_pairs:]
    if kind == "plain":
        w_ref, x_ref, m_ref, y_ref = rest
        o = mix[0]
    else:
        gate_ref, on_ref, w_ref, x_ref, m_ref, y_ref = rest
        o = _rms_heads(mix[0] + mix[1], on_ref[...], n_heads, hd)
        o = o * _silu(gate_ref[...]) if kind == "hgrn" else _sigmoid(gate_ref[...]) * o
    y_ref[...] = x_ref[...] + m_ref[0, 2:3, :] * _bdot(o, w_ref[...])


def _outproj_call(kind, pairs, gate, w, x, modsel):
    n, d = x.shape
    tm = TOKEN_TILE
    n_ctx_tiles = pairs[0][0].shape[0] // tm
    specs, args = [], []
    for a_ctx, a_lat in pairs:
        specs += [pl.BlockSpec((tm, d), lambda i: (jnp.minimum(i, n_ctx_tiles - 1), 0)),
                  pl.BlockSpec((tm, d), lambda i: (jnp.maximum(i - n_ctx_tiles, 0), 0))]
        args += [a_ctx, a_lat]
    if gate is not None:
        proj, cb, on = gate
        specs += [pl.BlockSpec((tm, d), lambda i: (i, cb)), pl.BlockSpec(on.shape, lambda i: (0, 0))]
        args += [proj, on]
    specs += [pl.BlockSpec(w.shape, lambda i: (0, 0)),
              pl.BlockSpec((tm, d), lambda i: (i, 0)),
              pl.BlockSpec((1, 6, d), lambda i: (i, 0, 0))]
    args += [w, x, modsel]
    body = functools.partial(_outproj_kernel, kind=kind, n_pairs=len(pairs), n_ctx_tiles=n_ctx_tiles,
                             n_heads=8, hd=128)
    return pl.pallas_call(
        body,
        grid=(n // tm,),
        in_specs=specs,
        out_specs=pl.BlockSpec((tm, d), lambda i: (i, 0)),
        out_shape=jax.ShapeDtypeStruct((n, d), F32),
        compiler_params=_params(("arbitrary",)),
        name="outproj_" + kind,
    )(*args)


def _gqa_ctx_kernel(p_ref, qn_ref, kn_ref, o_ref, k_ref, *, n_heads, n_kv, hd):
    rep = n_heads // n_kv
    scale = hd ** -0.5
    koff = n_heads * hd
    voff = koff + n_kv * hd
    ks = [_rms(p_ref[:, koff + g * hd: koff + (g + 1) * hd], kn_ref[...]) for g in range(n_kv)]
    k_ref[...] = jnp.concatenate(ks, axis=-1)
    for h in range(n_heads):
        g = h // rep
        q = _rms(p_ref[:, h * hd:(h + 1) * hd], qn_ref[...])
        s = _bdot_nt(q, ks[g]) * scale
        (e,), den = _softmax_rows([s])
        o_ref[:, h * hd:(h + 1) * hd] = _bdot(e / den, p_ref[:, voff + g * hd: voff + (g + 1) * hd])


def _gqa_ctx_call(proj, qn, kn, n_seq, t):
    n_heads, n_kv, hd = 8, 2, 128
    win = proj.shape[1]
    body = functools.partial(_gqa_ctx_kernel, n_heads=n_heads, n_kv=n_kv, hd=hd)
    return pl.pallas_call(
        body,
        grid=(n_seq,),
        in_specs=[pl.BlockSpec((t, win), lambda b: (b, 0)),
                  pl.BlockSpec((1, hd), lambda b: (0, 0)),
                  pl.BlockSpec((1, hd), lambda b: (0, 0))],
        out_specs=[pl.BlockSpec((t, n_heads * hd), lambda b: (b, 0)),
                   pl.BlockSpec((t, n_kv * hd), lambda b: (b, 0))],
        out_shape=[jax.ShapeDtypeStruct((n_seq * t, n_heads * hd), F32),
                   jax.ShapeDtypeStruct((n_seq * t, n_kv * hd), F32)],
        compiler_params=_params(("arbitrary",)),
        name="gqa_ctx",
    )(proj, qn.reshape(1, hd), kn.reshape(1, hd))


def _rope(x, cosd, sind):
    return x * cosd + pltpu.roll(x, x.shape[-1] // 2, 1) * sind


def _gqa_lat_kernel(pq_ref, pkv_ref, kc_ref, vc_ref, cq_ref, sq_ref, ck_ref, sk_ref, qn_ref, kn_ref,
                    o_ref, k_scr, v_scr, *, n_heads, n_kv, hd, t_ctx):
    rep = n_heads // n_kv
    scale = hd ** -0.5

    @pl.when(pl.program_id(1) == 0)
    def _():
        k_scr[0:t_ctx, :] = kc_ref[0].astype(BF16)
        v_scr[0:t_ctx, :] = vc_ref[0].astype(BF16)
        for g in range(n_kv):
            k = _rms(pkv_ref[:, g * hd:(g + 1) * hd], kn_ref[...])
            k_scr[t_ctx:, g * hd:(g + 1) * hd] = _rope(k, ck_ref[...], sk_ref[...]).astype(BF16)
        v_scr[t_ctx:, :] = pkv_ref[:, n_kv * hd:].astype(BF16)

    for h in range(n_heads):
        g = h // rep
        q = _rope(_rms(pq_ref[:, h * hd:(h + 1) * hd], qn_ref[...]), cq_ref[...], sq_ref[...])
        s = _bdot_nt(q, k_scr[:, g * hd:(g + 1) * hd]) * scale
        (e,), den = _softmax_rows([s])
        o_ref[:, h * hd:(h + 1) * hd] = _bdot(e / den, v_scr[:, g * hd:(g + 1) * hd])


def _gqa_lat_call(proj, row0, n_seq, t, cache_k, cache_v, cosd, sind, qn, kn):
    n_heads, n_kv, hd = 8, 2, 128
    tq = 256
    t_ctx = cache_k.shape[1]
    nq = t // tq
    kvw = 2 * n_kv * hd
    qblk0 = row0 // tq
    sblk0 = row0 // t
    body = functools.partial(_gqa_lat_kernel, n_heads=n_heads, n_kv=n_kv, hd=hd, t_ctx=t_ctx)
    return pl.pallas_call(
        body,
        grid=(n_seq, nq),
        in_specs=[pl.BlockSpec((tq, n_heads * hd), lambda b, i: (qblk0 + b * nq + i, 0)),
                  pl.BlockSpec((t, kvw), lambda b, i: (sblk0 + b, (n_heads * hd) // kvw)),
                  pl.BlockSpec((1, t_ctx, n_kv * hd), lambda b, i: (b, 0, 0)),
                  pl.BlockSpec((1, t_ctx, n_kv * hd), lambda b, i: (b, 0, 0)),
                  pl.BlockSpec((tq, hd), lambda b, i: (i, 0)),
                  pl.BlockSpec((tq, hd), lambda b, i: (i, 0)),
                  pl.BlockSpec((t, hd), lambda b, i: (0, 0)),
                  pl.BlockSpec((t, hd), lambda b, i: (0, 0)),
                  pl.BlockSpec((1, hd), lambda b, i: (0, 0)),
                  pl.BlockSpec((1, hd), lambda b, i: (0, 0))],
        out_specs=pl.BlockSpec((tq, n_heads * hd), lambda b, i: (b * nq + i, 0)),
        out_shape=jax.ShapeDtypeStruct((n_seq * t, n_heads * hd), F32),
        scratch_shapes=[pltpu.VMEM((t_ctx + t, n_kv * hd), BF16),
                        pltpu.VMEM((t_ctx + t, n_kv * hd), BF16)],
        compiler_params=_params(("arbitrary", "arbitrary")),
        name="gqa_latent",
    )(proj, proj, cache_k, cache_v, cosd, sind, cosd, sind, qn.reshape(1, hd), kn.reshape(1, hd))


def _mha_ctx_kernel(q_ref, k_ref, v_ref, o_ref, *, hd):
    scale = hd ** -0.5
    for h in range(q_ref.shape[1] // hd):
        sl = slice(h * hd, (h + 1) * hd)
        s = _bdot_nt(q_ref[:, sl], k_ref[:, sl]) * scale
        (e,), den = _softmax_rows([s])
        o_ref[:, sl] = _bdot(e / den, v_ref[:, sl])


def _mha_ctx_call(proj, n_seq, t):
    hd = 64
    d = proj.shape[1] // 3
    body = functools.partial(_mha_ctx_kernel, hd=hd)
    return pl.pallas_call(
        body,
        grid=(n_seq,),
        in_specs=[pl.BlockSpec((t, d), lambda b: (b, 0)),
                  pl.BlockSpec((t, d), lambda b: (b, 1)),
                  pl.BlockSpec((t, d), lambda b: (b, 2))],
        out_specs=pl.BlockSpec((t, d), lambda b: (b, 0)),
        out_shape=jax.ShapeDtypeStruct((n_seq * t, d), F32),
        compiler_params=_params(("arbitrary",)),
        name="mha_ctx",
    )(proj, proj, proj)


def _na_bias_kernel(rpb_ref, o_ref, *, n_rel_rows, n_rel_cols):
    h = pl.program_id(0)
    w_io = lax.broadcasted_iota(I32, (GRID_W, 2 * GRID_W), 0)
    lane = lax.broadcasted_iota(I32, (GRID_W, 2 * GRID_W), 1)
    ck = jnp.where(lane < GRID_W, lane, lane - GRID_W)
    c_start = jnp.clip(w_io - NA_COLS // 2, 0, GRID_W - NA_COLS)
    in_win = (ck >= c_start) & (ck < c_start + NA_COLS)
    rel = ck - w_io + (NA_COLS - 1)
    base = h * (n_rel_rows * n_rel_cols)
    tiles = []
    for j in range(n_rel_rows):
        acc = jnp.zeros((GRID_W, 2 * GRID_W), F32)
        for jj in range(n_rel_cols):
            acc = jnp.where(rel == jj, rpb_ref[base + j * n_rel_cols + jj], acc)
        tiles.append(jnp.where(in_win, acc, NEG_BIG))
    for j in range(n_rel_rows):
        hi = tiles[j + 1] if j + 1 < n_rel_rows else jnp.full((GRID_W, 2 * GRID_W), NEG_BIG, F32)
        o_ref[0, j] = jnp.where(lane < GRID_W, tiles[j], hi)


def _na_bias_call(rpb):
    n_heads, nrr, nrc = rpb.shape
    body = functools.partial(_na_bias_kernel, n_rel_rows=nrr, n_rel_cols=nrc)
    return pl.pallas_call(
        body,
        grid=(n_heads,),
        in_specs=[pl.BlockSpec(memory_space=pltpu.SMEM)],
        out_specs=pl.BlockSpec((1, nrr, GRID_W, 2 * GRID_W), lambda h: (h, 0, 0, 0)),
        out_shape=jax.ShapeDtypeStruct((n_heads, nrr, GRID_W, 2 * GRID_W), F32),
        compiler_params=_params(("arbitrary",)),
        name="na_bias",
    )(rpb.reshape(-1))


NA_QROWS = 4
NA_KROWS = 12


def _na_kernel(q_ref, k0_ref, k1_ref, k2_ref, v0_ref, v1_ref, v2_ref, kc_ref, vc_ref, tz_ref, o_ref,
               *, hd, n_grid_rows):
    scale = hd ** -0.5
    blk = pl.program_id(1)
    kstart = jnp.clip(blk * NA_QROWS - NA_ROWS // 2, 0, n_grid_rows - NA_KROWS)
    lane = lax.broadcasted_iota(I32, (GRID_W, 2 * GRID_W), 1)
    n_rel = tz_ref.shape[1]
    for hh in range(q_ref.shape[1] // hd):
        sl = slice(hh * hd, (hh + 1) * hd)
        rows = []
        for rq_l in range(NA_QROWS):
            rq = blk * NA_QROWS + rq_l
            r_start = jnp.clip(rq - NA_ROWS // 2, 0, n_grid_rows - NA_ROWS)
            tiles = []
            for m in range(NA_KROWS // 2):
                rk = kstart + 2 * m
                j = jnp.clip(rk - rq + (NA_ROWS - 1), 0, n_rel - 1)
                ok0 = (rk >= r_start) & (rk < r_start + NA_ROWS)
                ok1 = (rk + 1 >= r_start) & (rk + 1 < r_start + NA_ROWS)
                pen = jnp.where(lane < GRID_W, jnp.where(ok0, 0.0, NEG_BIG), jnp.where(ok1, 0.0, NEG_BIG))
                tiles.append(tz_ref[hh, pl.ds(j, 1)][0] + pen)
            rows.append(jnp.concatenate(tiles, axis=-1))
        bias = jnp.concatenate(rows, axis=0)
        q = q_ref[:, sl]
        k_loc = jnp.concatenate([k0_ref[:, sl], k1_ref[:, sl], k2_ref[:, sl]], axis=0)
        v_loc = jnp.concatenate([v0_ref[:, sl], v1_ref[:, sl], v2_ref[:, sl]], axis=0)
        s_loc = _bdot_nt(q, k_loc) * scale + bias
        s_ctx = _bdot_nt(q, kc_ref[0, :, sl]) * scale
        (e_loc, e_ctx), den = _softmax_rows([s_loc, s_ctx])
        o_ref[:, sl] = (_bdot(e_loc, v_loc) + _bdot(e_ctx, vc_ref[0, :, sl])) / den


def _na_call(proj, row0, n_seq, t, cache_k, cache_v, tz):
    hd, cw = 64, 128
    d = proj.shape[1] // 3
    ncb = d // cw
    tq = NA_QROWS * GRID_W
    nq = t // tq
    n_grid_rows = t // GRID_W
    t_ctx = cache_k.shape[1]
    qblk0 = row0 // tq

    def kv_map(which, j):
        def index_map(b, i, c):
            ks = jnp.clip(i * NA_QROWS - NA_ROWS // 2, 0, n_grid_rows - NA_KROWS) // NA_QROWS
            return (qblk0 + b * nq + ks + j, which * ncb + c)
        return index_map

    body = functools.partial(_na_kernel, hd=hd, n_grid_rows=n_grid_rows)
    return pl.pallas_call(
        body,
        grid=(n_seq, nq, ncb),
        in_specs=[pl.BlockSpec((tq, cw), lambda b, i, c: (qblk0 + b * nq + i, c))]
                 + [pl.BlockSpec((tq, cw), kv_map(1, j)) for j in range(3)]
                 + [pl.BlockSpec((tq, cw), kv_map(2, j)) for j in range(3)]
                 + [pl.BlockSpec((1, t_ctx, cw), lambda b, i, c: (b, 0, c)),
                    pl.BlockSpec((1, t_ctx, cw), lambda b, i, c: (b, 0, c)),
                    pl.BlockSpec((cw // hd,) + tz.shape[1:], lambda b, i, c: (c, 0, 0, 0))],
        out_specs=pl.BlockSpec((tq, cw), lambda b, i, c: (b * nq + i, c)),
        out_shape=jax.ShapeDtypeStruct((n_seq * t, d), F32),
        compiler_params=_params(("arbitrary", "arbitrary", "arbitrary")),
        name="nbr_attn",
    )(proj, proj, proj, proj, proj, proj, proj, cache_k, cache_v, tz)


def _hgrn_kernel(q_ref, v_ref, f_ref, bf_ref, lb_ref, s0_ref, o_ref, s_ref, st_scr,
                 *, reverse, n_heads, dk, has_s0, n_blk):
    c = pl.program_id(1)
    L = GLA_CHUNK
    tb = q_ref.shape[0]

    @pl.when(c == 0)
    def _():
        for h in range(n_heads):
            if has_s0:
                st_scr[h] = s0_ref[0, 0, h].T
            else:
                st_scr[h] = jnp.zeros_like(st_scr[h])

    row = lax.broadcasted_iota(I32, (L, L), 0)
    col = lax.broadcasted_iota(I32, (L, L), 1)
    tri = jnp.where((col >= row) if reverse else (col <= row), 1.0, 0.0).astype(F32)
    lb = lb_ref[0]
    bf = bf_ref[0]
    n_chunks = tb // L

    def chunk_step(jj, carry):
        jc = (n_chunks - 1 - jj) if reverse else jj
        rs = pl.ds(pl.multiple_of(jc * L, L), L)
        q = _silu(q_ref[rs, :]) * (dk ** -0.5)
        v = v_ref[rs, :]
        f = lb + (1.0 - lb) * _sigmoid(f_ref[rs, :] + bf)
        logf = jnp.log(f)
        kk = 1.0 - f
        cum = _fdot(tri, logf)
        end = cum[0:1, :] if reverse else cum[L - 1:L, :]
        qd = q * jnp.exp(cum)
        kd = kk * jnp.exp(end - cum)
        e_end = jnp.exp(end)
        acc = []
        for h in range(n_heads):
            hs = slice(h * dk, (h + 1) * dk)
            acc.append(_bdot_nt(qd[:, hs], st_scr[h]))
        tio = lax.broadcasted_iota(I32, (L, 1), 0)
        for s in range(L):
            g8 = (s // 8) * 8
            r0, r1 = (0, g8 + 8) if reverse else (g8, L)
            keep = (tio[r0:r1] <= s) if reverse else (tio[r0:r1] >= s)
            arg = jnp.where(keep, cum[r0:r1, :] - cum[s:s + 1, :], -jnp.inf)
            w = jnp.exp(arg) * q[r0:r1, :] * kk[s:s + 1, :]
            for h in range(n_heads):
                hs = slice(h * dk, (h + 1) * dk)
                a = jnp.sum(w[:, hs], axis=-1, keepdims=True)
                upd = a * v[s:s + 1, hs]
                if r0 == 0 and r1 == L:
                    acc[h] = acc[h] + upd
                elif r0 == 0:
                    acc[h] = jnp.concatenate([acc[h][:r1] + upd, acc[h][r1:]], axis=0)
                else:
                    acc[h] = jnp.concatenate([acc[h][:r0], acc[h][r0:] + upd], axis=0)
        o_ref[rs, :] = jnp.concatenate(acc, axis=-1)
        for h in range(n_heads):
            hs = slice(h * dk, (h + 1) * dk)
            st_scr[h] = st_scr[h] * e_end[:, hs] + _bdot_tn(v[:, hs], kd[:, hs])
        return carry

    lax.fori_loop(0, n_chunks, chunk_step, 0)

    @pl.when(c == n_blk - 1)
    def _():
        for h in range(n_heads):
            s_ref[0, 0, h] = st_scr[h].T


def _hgrn_call(proj, row0, n_seq, t, b_f, lb, s0, reverse):
    n_heads, dk = 8, 128
    d = n_heads * dk
    tb = 256
    n_blk = t // tb
    blk0 = row0 // tb
    di = 1 if reverse else 0
    has_s0 = s0 is not None
    if not has_s0:
        s0 = jnp.zeros((1, 2, n_heads, dk, dk), F32)

    def tok(b, c):
        return blk0 + b * n_blk + ((n_blk - 1 - c) if reverse else c)

    body = functools.partial(_hgrn_kernel, reverse=reverse, n_heads=n_heads, dk=dk,
                             has_s0=has_s0, n_blk=n_blk)
    return pl.pallas_call(
        body,
        grid=(n_seq, n_blk),
        in_specs=[pl.BlockSpec((tb, d), lambda b, c: (tok(b, c), 0)),
                  pl.BlockSpec((tb, d), lambda b, c: (tok(b, c), 1)),
                  pl.BlockSpec((tb, d), lambda b, c: (tok(b, c), 2 + di)),
                  pl.BlockSpec((1, 1, d), lambda b, c: (di, 0, 0)),
                  pl.BlockSpec((1, 1, d), lambda b, c: (di, 0, 0)),
                  pl.BlockSpec((1, 1, n_heads, dk, dk),
                               (lambda b, c: (b, di, 0, 0, 0)) if has_s0 else (lambda b, c: (0, 0, 0, 0, 0)))],
        out_specs=[pl.BlockSpec((tb, d), lambda b, c: (tok(b, c) - blk0, 0)),
                   pl.BlockSpec((1, 1, n_heads, dk, dk), lambda b, c: (b, 0, 0, 0, 0))],
        out_shape=[jax.ShapeDtypeStruct((n_seq * t, d), F32),
                   jax.ShapeDtypeStruct((n_seq, 1, n_heads, dk, dk), F32)],
        scratch_shapes=[pltpu.VMEM((n_heads, dk, dk), F32)],
        compiler_params=_params(("arbitrary", "arbitrary")),
        name="hgrn2_bw" if reverse else "hgrn2_fw",
    )(proj, proj, proj, b_f.reshape(2, 1, d), lb.reshape(2, 1, d), s0)


def _log_sigmoid(x):
    return jnp.minimum(x, 0.0) - jnp.log(1.0 + jnp.exp(-jnp.abs(x)))


def _mlstm_kernel(q_ref, k_ref, v_ref, g_ref, bg_ref, c0_ref, n0_ref, m0_ref,
                  o_ref, c_out, n_out, m_out, c_scr, n_scr, m_scr,
                  *, reverse, n_heads, dqk, dv, has_state, n_blk):
    c = pl.program_id(1)
    L = q_ref.shape[0]
    i_off = 2 * n_heads if reverse else 0
    f_off = i_off + n_heads

    @pl.when(c == 0)
    def _():
        if has_state:
            c_scr[...] = c0_ref[0, 0]
            n_scr[...] = n0_ref[0, 0]
            m_scr[...] = m0_ref[0, 0]
        else:
            c_scr[...] = jnp.zeros_like(c_scr)
            n_scr[...] = jnp.zeros_like(n_scr)
            m_scr[...] = jnp.zeros_like(m_scr)

    gates = GATE_SOFTCAP * jnp.tanh((g_ref[...] + bg_ref[...]) / GATE_SOFTCAP)
    logf = _log_sigmoid(gates)
    row = lax.broadcasted_iota(I32, (L, L), 0)
    col = lax.broadcasted_iota(I32, (L, L), 1)
    causal = (col >= row) if reverse else (col <= row)
    tri = jnp.where(causal, 1.0, 0.0).astype(F32)
    cum = _fdot(tri, logf)
    cum_t = cum.T
    gates_t = gates.T
    e_row = 0 if reverse else L - 1
    outs = []
    for h in range(n_heads):
        qh = q_ref[:, h * dqk:(h + 1) * dqk] * (dqk ** -0.5)
        kh = k_ref[:, h * dqk:(h + 1) * dqk]
        vh = v_ref[:, h * dv:(h + 1) * dv]
        cum_c = cum[:, f_off + h:f_off + h + 1]
        cum_r = cum_t[f_off + h:f_off + h + 1, :]
        i_c = gates[:, i_off + h:i_off + h + 1]
        i_r = gates_t[i_off + h:i_off + h + 1, :]
        m_prev = m_scr[0:1, h:h + 1]
        d = jnp.where(causal, cum_c - cum_r + i_r, -jnp.inf)
        m_inter = cum_c + m_prev
        m_t = jnp.maximum(m_inter, d.max(axis=-1, keepdims=True))
        w_inter = jnp.exp(m_inter - m_t)
        w_intra = jnp.exp(d - m_t)
        qk = _bdot_nt(qh, kh) * w_intra
        num = w_inter * _bdot(qh, c_scr[h]) + _bdot(qk, vh)
        den = w_inter * jnp.sum(qh * n_scr[h], axis=-1, keepdims=True) + qk.sum(axis=-1, keepdims=True)
        outs.append(num / jnp.maximum(jnp.abs(den), jnp.exp(-m_t)))
        end = cum_c[e_row:e_row + 1, :]
        g_end_r = end - cum_r + i_r
        g_end_c = end - cum_c + i_c
        m_new = jnp.maximum(end + m_prev, g_end_r.max(axis=-1, keepdims=True))
        w_old = jnp.exp(end + m_prev - m_new)
        ks = kh * jnp.exp(g_end_c - m_new)
        c_scr[h] = w_old * c_scr[h] + _bdot_tn(ks, vh)
        n_scr[h] = w_old * n_scr[h] + ks.sum(axis=0, keepdims=True)
        m_scr[0:1, h:h + 1] = m_new
    o_ref[...] = jnp.concatenate(outs, axis=-1)

    @pl.when(c == n_blk - 1)
    def _():
        c_out[0, 0] = c_scr[...]
        n_out[0, 0] = n_scr[...]
        m_out[0, 0] = m_scr[...]


def _mlstm_call(proj, row0, n_seq, t, b_gates_pad, state, reverse):
    n_heads, dqk, dv = 8, 64, 128
    wq, wv = n_heads * dqk, n_heads * dv
    L = 256
    n_blk = t // L
    blk0 = row0 // L
    di = 1 if reverse else 0
    has_state = state is not None
    if has_state:
        c0, n0, m0 = state
        smap = lambda b, c: (b, di, 0, 0, 0)
        mmap = lambda b, c: (b, di, 0, 0)
    else:
        c0 = jnp.zeros((1, 1, n_heads, dqk, dv), F32)
        n0 = jnp.zeros((1, 1, n_heads, 1, dqk), F32)
        m0 = jnp.zeros((1, 1, 1, 128), F32)
        smap = lambda b, c: (0, 0, 0, 0, 0)
        mmap = lambda b, c: (0, 0, 0, 0)

    def tok(b, c):
        return blk0 + b * n_blk + ((n_blk - 1 - c) if reverse else c)

    body = functools.partial(_mlstm_kernel, reverse=reverse, n_heads=n_heads, dqk=dqk, dv=dv,
                             has_state=has_state, n_blk=n_blk)
    gate_cb = (2 * wq + 2 * wv) // 128
    return pl.pallas_call(
        body,
        grid=(n_seq, n_blk),
        in_specs=[pl.BlockSpec((L, wq), lambda b, c: (tok(b, c), 0)),
                  pl.BlockSpec((L, wq), lambda b, c: (tok(b, c), 1)),
                  pl.BlockSpec((L, wv), lambda b, c: (tok(b, c), (2 * wq) // wv)),
                  pl.BlockSpec((L, 128), lambda b, c: (tok(b, c), gate_cb)),
                  pl.BlockSpec((1, 128), lambda b, c: (0, 0)),
                  pl.BlockSpec((1, 1, n_heads, dqk, dv), smap),
                  pl.BlockSpec((1, 1, n_heads, 1, dqk), smap),
                  pl.BlockSpec((1, 1, 1, 128), mmap)],
        out_specs=[pl.BlockSpec((L, wv), lambda b, c: (tok(b, c) - blk0, 0)),
                   pl.BlockSpec((1, 1, n_heads, dqk, dv), lambda b, c: (b, 0, 0, 0, 0)),
                   pl.BlockSpec((1, 1, n_heads, 1, dqk), lambda b, c: (b, 0, 0, 0, 0)),
                   pl.BlockSpec((1, 1, 1, 128), lambda b, c: (b, 0, 0, 0))],
        out_shape=[jax.ShapeDtypeStruct((n_seq * t, wv), F32),
                   jax.ShapeDtypeStruct((n_seq, 1, n_heads, dqk, dv), F32),
                   jax.ShapeDtypeStruct((n_seq, 1, n_heads, 1, dqk), F32),
                   jax.ShapeDtypeStruct((n_seq, 1, 1, 128), F32)],
        scratch_shapes=[pltpu.VMEM((n_heads, dqk, dv), F32),
                        pltpu.VMEM((n_heads, 1, dqk), F32),
                        pltpu.VMEM((1, 128), F32)],
        compiler_params=_params(("arbitrary", "arbitrary")),
        name="mlstm_bw" if reverse else "mlstm_fw",
    )(proj, proj, proj, proj, b_gates_pad, c0, n0, m0)


def _moe_input(x_ref, m_ref, g_ref):
    return _rms(x_ref[...], g_ref[...]) * (1.0 + m_ref[0, 4:5, :]) + m_ref[0, 3:4, :]


def _route_kernel(x_ref, m_ref, g_ref, wr_ref, br_ref, idx_ref, gate_ref, rank_ref, cnt_ref, carry_scr,
                  *, n_experts):
    i = pl.program_id(0)
    tm = x_ref.shape[0]

    @pl.when(i == 0)
    def _():
        carry_scr[...] = jnp.zeros_like(carry_scr)

    h = _moe_input(x_ref, m_ref, g_ref)
    logits = lax.dot_general(wr_ref[...], h, (((1,), (1,)), ((), ())), preferred_element_type=F32,
                             precision=lax.Precision.HIGHEST) + br_ref[...]
    e_io = lax.broadcasted_iota(I32, (n_experts, tm), 0).astype(F32)
    work = logits
    vals, idxs = [], []
    chosen = jnp.zeros((n_experts, tm), F32)
    for _ in range(TOP_K):
        mx = work.max(axis=0, keepdims=True)
        ix = jnp.min(jnp.where(work == mx, e_io, float(n_experts)), axis=0, keepdims=True)
        hit = e_io == ix
        vals.append(mx)
        idxs.append(ix)
        chosen = jnp.where(hit, 1.0, chosen)
        work = jnp.where(hit, -jnp.inf, work)
    es = [jnp.exp(v - vals[0]) for v in vals]
    den = es[0] + es[1] + es[2] + es[3]
    srow = lax.broadcasted_iota(I32, (tm, tm), 0)
    scol = lax.broadcasted_iota(I32, (tm, tm), 1)
    before = jnp.where(srow < scol, 1.0, 0.0).astype(BF16)
    pos = jnp.dot(chosen.astype(BF16), before, preferred_element_type=F32) + carry_scr[...]
    ranks = [jnp.sum(jnp.where(e_io == ix, pos, 0.0), axis=0, keepdims=True) for ix in idxs]
    carry_scr[...] = carry_scr[...] + chosen.sum(axis=1, keepdims=True)
    idx_ref[...] = jnp.concatenate(idxs, axis=0).astype(I32)
    gate_ref[...] = jnp.concatenate([e / den for e in es], axis=0)
    rank_ref[...] = jnp.concatenate(ranks, axis=0).astype(I32)
    cnt_ref[...] = jnp.broadcast_to(carry_scr[...], cnt_ref.shape).astype(I32)


def _route_call(x, modsel, g, w_router, b_router):
    n, d = x.shape
    n_experts = w_router.shape[1]
    tm = TOKEN_TILE
    body = functools.partial(_route_kernel, n_experts=n_experts)
    return pl.pallas_call(
        body,
        grid=(n // tm,),
        in_specs=[pl.BlockSpec((tm, d), lambda i: (i, 0)),
                  pl.BlockSpec((1, 6, d), lambda i: (i, 0, 0)),
                  pl.BlockSpec((1, d), lambda i: (0, 0)),
                  pl.BlockSpec((n_experts, d), lambda i: (0, 0)),
                  pl.BlockSpec((n_experts, 1), lambda i: (0, 0))],
        out_specs=[pl.BlockSpec((TOP_K, tm), lambda i: (0, i)),
                   pl.BlockSpec((TOP_K, tm), lambda i: (0, i)),
                   pl.BlockSpec((TOP_K, tm), lambda i: (0, i)),
                   pl.BlockSpec((n_experts, 128), lambda i: (0, 0))],
        out_shape=[jax.ShapeDtypeStruct((TOP_K, n), I32),
                   jax.ShapeDtypeStruct((TOP_K, n), F32),
                   jax.ShapeDtypeStruct((TOP_K, n), I32),
                   jax.ShapeDtypeStruct((n_experts, 128), I32)],
        scratch_shapes=[pltpu.VMEM((n_experts, 1), F32)],
        compiler_params=_params(("arbitrary",)),
        name="moe_route",
    )(x, modsel, g.reshape(1, d), w_router.T, b_router.reshape(n_experts, 1))


def _slot_kernel(cnt_ref, idx_ref, rank_ref, dest_ref, bexp_ref, pad_ref, nused_ref, *, n_experts, n_blocks):
    cnt = cnt_ref[:, 0:1].astype(F32)
    padded = jnp.ceil(cnt * (1.0 / MOE_BLOCK)) * MOE_BLOCK
    er = lax.broadcasted_iota(I32, (n_experts, n_experts), 0)
    ec = lax.broadcasted_iota(I32, (n_experts, n_experts), 1)
    start_row = jnp.sum(jnp.where(er < ec, padded, 0.0), axis=0, keepdims=True)
    start_col = jnp.sum(jnp.where(er == ec, start_row, 0.0), axis=1, keepdims=True)
    end_col = start_col + padded
    idx = idx_ref[...]
    e_io = lax.broadcasted_iota(I32, (n_experts,) + idx.shape[1:], 0)
    rows = []
    for k in range(TOP_K):
        hit = e_io == idx[k:k + 1, :]
        rows.append(jnp.sum(jnp.where(hit, start_col, 0.0), axis=0, keepdims=True))
    dest_ref[...] = jnp.concatenate(rows, axis=0).astype(I32) + rank_ref[...]
    blk_start = (lax.broadcasted_iota(I32, (n_experts, n_blocks), 1) * MOE_BLOCK).astype(F32)
    n_done = jnp.sum(jnp.where(end_col <= blk_start, 1.0, 0.0), axis=0, keepdims=True)
    bexp_ref[...] = jnp.minimum(n_done, n_experts - 1.0).astype(I32)
    n_used = jnp.sum(padded, axis=0, keepdims=True) * (1.0 / MOE_BLOCK)
    nused_ref[...] = n_used.astype(I32)
    cnt_row = jnp.sum(jnp.where(er == ec, cnt, 0.0), axis=0, keepdims=True)
    padded_row = jnp.sum(jnp.where(er == ec, padded, 0.0), axis=0, keepdims=True)
    pad_ref[...] = jnp.concatenate([start_row + cnt_row, padded_row - cnt_row,
                                    jnp.broadcast_to(n_used, cnt_row.shape)], axis=0).astype(I32)


def _slot_call(counts, idx_t, rank_t, n_blocks):
    n_experts = counts.shape[0]
    n = idx_t.shape[1]
    tn = min(2048, n)
    body = functools.partial(_slot_kernel, n_experts=n_experts, n_blocks=n_blocks)
    return pl.pallas_call(
        body,
        grid=(n // tn,),
        in_specs=[pl.BlockSpec((n_experts, 128), lambda i: (0, 0)),
                  pl.BlockSpec((TOP_K, tn), lambda i: (0, i)),
                  pl.BlockSpec((TOP_K, tn), lambda i: (0, i))],
        out_specs=[pl.BlockSpec((TOP_K, tn), lambda i: (0, i)),
                   pl.BlockSpec((1, n_blocks), lambda i: (0, 0)),
                   pl.BlockSpec((3, n_experts), lambda i: (0, 0)),
                   pl.BlockSpec((1, 1), lambda i: (0, 0))],
        out_shape=[jax.ShapeDtypeStruct((TOP_K, n), I32),
                   jax.ShapeDtypeStruct((1, n_blocks), I32),
                   jax.ShapeDtypeStruct((3, n_experts), I32),
                   jax.ShapeDtypeStruct((1, 1), I32)],
        compiler_params=_params(("arbitrary",)),
        name="moe_slots",
    )(counts, idx_t, rank_t)


LANES = 128
ROW_SUBLANES = 8
DMA_ISSUE_UNROLL = 8


def _to_row_tiles(ref, base, x):
    rows = x.shape[0]
    for c in range(ROW_SUBLANES):
        ref[pl.ds(base * ROW_SUBLANES + c, rows, stride=ROW_SUBLANES), :] = x[:, c * LANES:(c + 1) * LANES]


def _from_row_tiles(ref, base, rows, c):
    return ref[pl.ds(base * ROW_SUBLANES + c, rows, stride=ROW_SUBLANES), :]


def _row_tile(ref, r):
    return ref.at[pl.ds(pl.multiple_of(r * ROW_SUBLANES, ROW_SUBLANES), ROW_SUBLANES)]


def _zero_fill_padding(pad_ref, xs_ref, z_scr, sem):
    z_scr[...] = jnp.zeros_like(z_scr)
    n_experts = pad_ref.shape[1]
    bits = range(MOE_BLOCK.bit_length() - 2, -1, -1)

    def pieces(e):
        off, length = pad_ref[0, e], pad_ref[1, e]
        for bit in bits:
            size = 1 << bit
            done = (length >> (bit + 1)) << (bit + 1)
            copy = pltpu.make_async_copy(z_scr.at[pl.ds(0, size * ROW_SUBLANES)],
                                         xs_ref.at[pl.ds(pl.multiple_of((off + done) * ROW_SUBLANES, ROW_SUBLANES),
                                                         size * ROW_SUBLANES)], sem)
            yield (length & size) != 0, copy

    def tail_blocks():
        n_blocks = xs_ref.shape[0] // (MOE_BLOCK * ROW_SUBLANES)
        for b in range(n_blocks - n_experts, n_blocks):
            copy = pltpu.make_async_copy(z_scr, xs_ref.at[pl.ds(b * MOE_BLOCK * ROW_SUBLANES,
                                                                MOE_BLOCK * ROW_SUBLANES)], sem)
            yield b >= pad_ref[2, 0], copy

    def all_copies():
        for e in range(n_experts):
            yield from pieces(e)
        yield from tail_blocks()

    for needed, copy in all_copies():
        pl.when(needed)(copy.start)
    for needed, copy in all_copies():
        pl.when(needed)(copy.wait)


def _dispatch_kernel(pad_ref, dest_ref, x_ref, m_ref, g_ref, xs_ref, h_scr, z_scr, sem):
    tm = x_ref.shape[0]

    @pl.when(pl.program_id(0) == 0)
    def _():
        _zero_fill_padding(pad_ref, xs_ref, z_scr, sem.at[1])

    _to_row_tiles(h_scr, 0, _moe_input(x_ref, m_ref, g_ref))

    def start_row(r, carry):
        for k in range(TOP_K):
            pltpu.make_async_copy(_row_tile(h_scr, r), _row_tile(xs_ref, dest_ref[0, 0, k * tm + r]),
                                  sem.at[0]).start(priority=k % 2)
        return carry

    lax.fori_loop(0, tm, start_row, 0, unroll=DMA_ISSUE_UNROLL // TOP_K)
    for _ in range(TOP_K):
        pltpu.make_async_copy(h_scr, xs_ref.at[pl.ds(0, tm * ROW_SUBLANES)], sem.at[0]).wait()


def _dispatch_call(pad_info, dest_tiles, x, modsel, g, n_slots):
    n, d = x.shape
    tm = TOKEN_TILE
    assert d == ROW_SUBLANES * LANES
    return pl.pallas_call(
        _dispatch_kernel,
        grid=(n // tm,),
        in_specs=[pl.BlockSpec(memory_space=pltpu.SMEM),
                  pl.BlockSpec((1, 1, TOP_K * tm), lambda i: (i, 0, 0), memory_space=pltpu.SMEM),
                  pl.BlockSpec((tm, d), lambda i: (i, 0)),
                  pl.BlockSpec((1, 6, d), lambda i: (i, 0, 0)),
                  pl.BlockSpec((1, d), lambda i: (0, 0))],
        out_specs=pl.BlockSpec(memory_space=pl.ANY),
        out_shape=jax.ShapeDtypeStruct((n_slots * ROW_SUBLANES, LANES), F32),
        scratch_shapes=[pltpu.VMEM((tm * ROW_SUBLANES, LANES), F32),
                        pltpu.VMEM((MOE_BLOCK * ROW_SUBLANES, LANES), F32),
                        pltpu.SemaphoreType.DMA((2,))],
        compiler_params=_params(("arbitrary",)),
        name="moe_dispatch",
    )(pad_info, dest_tiles, x, modsel, g.reshape(1, d))


def _ffn_kernel(bexp_ref, nused_ref, xs_ref, wgu_ref, bgu_ref, wdn_ref, bdn_ref, ys_ref, wgu_scr, wdn_scr):
    b = pl.program_id(0)
    d_ff = wdn_ref.shape[2]
    rows = xs_ref.shape[0] // ROW_SUBLANES

    @pl.when(b < nused_ref[0])
    def _():
        prev = bexp_ref[jnp.maximum(b - 1, 0)]

        @pl.when((b == 0) | (bexp_ref[b] != prev))
        def _():
            wgu_scr[...] = wgu_ref[0, 0].astype(BF16)
            wdn_scr[...] = wdn_ref[0, 0].astype(BF16)

        x = jnp.concatenate([_from_row_tiles(xs_ref, 0, rows, c).astype(BF16) for c in range(ROW_SUBLANES)],
                            axis=-1)
        gu = jnp.dot(x, wgu_scr[...], preferred_element_type=F32) + bgu_ref[0, 0]
        x_glu = jnp.minimum(gu[:, :d_ff], SWIGLU_LIMIT)
        x_lin = jnp.clip(gu[:, d_ff:], -SWIGLU_LIMIT, SWIGLU_LIMIT)
        hid = x_glu * _sigmoid(SWIGLU_ALPHA * x_glu) * (x_lin + 1.0)
        _to_row_tiles(ys_ref, 0, jnp.dot(hid.astype(BF16), wdn_scr[...], preferred_element_type=F32)
                      + bdn_ref[0, 0])

    @pl.when(b >= nused_ref[0])
    def _():
        ys_ref[...] = jnp.zeros_like(ys_ref)


def _ffn_call(layer, block_expert, n_used, xs, w_gu, b_gu, w_dn, b_dn):
    depth, n_experts, d, d_ff2 = w_gu.shape
    d_ff = d_ff2 // 2
    blk_rows = MOE_BLOCK * ROW_SUBLANES
    n_blocks = xs.shape[0] // blk_rows

    def blk(b, nu):
        return jnp.maximum(jnp.minimum(b, nu[0] - 1), 0)

    grid_spec = pltpu.PrefetchScalarGridSpec(
        num_scalar_prefetch=2,
        grid=(n_blocks,),
        in_specs=[pl.BlockSpec((blk_rows, LANES), lambda b, be, nu: (blk(b, nu), 0)),
                  pl.BlockSpec((1, 1, d, d_ff2), lambda b, be, nu: (layer, be[blk(b, nu)], 0, 0)),
                  pl.BlockSpec((1, 1, 1, d_ff2), lambda b, be, nu: (layer, be[blk(b, nu)], 0, 0)),
                  pl.BlockSpec((1, 1, d_ff, d), lambda b, be, nu: (layer, be[blk(b, nu)], 0, 0)),
                  pl.BlockSpec((1, 1, 1, d), lambda b, be, nu: (layer, be[blk(b, nu)], 0, 0))],
        out_specs=pl.BlockSpec((blk_rows, LANES), lambda b, be, nu: (b, 0)),
        scratch_shapes=[pltpu.VMEM((d, d_ff2), BF16), pltpu.VMEM((d_ff, d), BF16)],
    )
    return pl.pallas_call(
        _ffn_kernel,
        grid_spec=grid_spec,
        out_shape=jax.ShapeDtypeStruct(xs.shape, F32),
        compiler_params=_params(("arbitrary",)),
        name="moe_ffn",
    )(block_expert, n_used, xs, w_gu, b_gu.reshape(depth, n_experts, 1, d_ff2), w_dn,
      b_dn.reshape(depth, n_experts, 1, d))


def _combine_kernel(dest_ref, ys_ref, gate_ref, x_ref, m_ref, fg_ref, y_ref, buf, sem, *, final_norm):
    tm = x_ref.shape[0]

    def start_pair(p, carry):
        for u in range(2):
            j = 2 * p + u
            pltpu.make_async_copy(_row_tile(ys_ref, dest_ref[0, 0, j]), _row_tile(buf, j),
                                  sem.at[0]).start(priority=u)
        return carry

    lax.fori_loop(0, TOP_K * tm // 2, start_pair, 0, unroll=DMA_ISSUE_UNROLL // 2)
    pltpu.make_async_copy(ys_ref.at[pl.ds(0, TOP_K * tm * ROW_SUBLANES)], buf, sem.at[0]).wait()
    chunks = []
    for c in range(ROW_SUBLANES):
        y = gate_ref[:, 0:1] * _from_row_tiles(buf, 0, tm, c)
        for k in range(1, TOP_K):
            y = y + gate_ref[:, k:k + 1] * _from_row_tiles(buf, k * tm, tm, c)
        chunks.append(y)
    out = x_ref[...] + m_ref[0, 5:6, :] * jnp.concatenate(chunks, axis=-1)
    if final_norm:
        out = _rms(out, fg_ref[...])
    y_ref[...] = out


def _combine_call(dest_tiles, ys, gates_nk, x, modsel, final_g, final_norm):
    n, d = x.shape
    tm = TOKEN_TILE
    body = functools.partial(_combine_kernel, final_norm=final_norm)
    return pl.pallas_call(
        body,
        grid=(n // tm,),
        in_specs=[pl.BlockSpec((1, 1, TOP_K * tm), lambda i: (i, 0, 0), memory_space=pltpu.SMEM),
                  pl.BlockSpec(memory_space=pl.ANY),
                  pl.BlockSpec((tm, TOP_K), lambda i: (i, 0)),
                  pl.BlockSpec((tm, d), lambda i: (i, 0)),
                  pl.BlockSpec((1, 6, d), lambda i: (i, 0, 0)),
                  pl.BlockSpec((1, d), lambda i: (0, 0))],
        out_specs=pl.BlockSpec((tm, d), lambda i: (i, 0)),
        out_shape=jax.ShapeDtypeStruct((n, d), F32),
        scratch_shapes=[pltpu.VMEM((TOP_K * tm * ROW_SUBLANES, LANES), F32), pltpu.SemaphoreType.DMA((1,))],
        compiler_params=_params(("arbitrary",)),
        name="moe_combine",
    )(dest_tiles, ys, gates_nk, x, modsel, final_g.reshape(1, d))


def _moe_layer(layer, x, modsel, g2, w_router, b_router, w_gu, b_gu, w_dn, b_dn, final_g, final_norm):
    n, d = x.shape
    n_experts = w_router.shape[1]
    tm = TOKEN_TILE
    n_blocks = (n * TOP_K) // MOE_BLOCK + n_experts
    idx_t, gate_t, rank_t, counts = _route_call(x, modsel, g2, w_router, b_router)
    dest_t, block_expert, pad_info, n_used = _slot_call(counts, idx_t, rank_t, n_blocks)
    dest_tiles = dest_t.reshape(TOP_K, n // tm, tm).transpose(1, 0, 2).reshape(n // tm, 1, TOP_K * tm)
    xs = _dispatch_call(pad_info, dest_tiles, x, modsel, g2, n_blocks * MOE_BLOCK)
    ys = _ffn_call(layer, block_expert.reshape(n_blocks), n_used.reshape(1), xs, w_gu, b_gu, w_dn, b_dn)
    return _combine_call(dest_tiles, ys, gate_t.T, x, modsel, final_g, final_norm)


def _rope_tables(t, hd):
    pos = np.arange(t)
    n_freq = hd // 4
    inv_freq = ROPE_THETA ** (-np.arange(n_freq, dtype=np.float32) / n_freq)
    ang = np.concatenate([(pos // GRID_W).astype(np.float32)[:, None] * inv_freq,
                          (pos % GRID_W).astype(np.float32)[:, None] * inv_freq], axis=-1)
    ang = jnp.asarray(ang, F32)
    cos, sin = jnp.cos(ang), jnp.sin(ang)
    return jnp.concatenate([cos, cos], axis=-1), jnp.concatenate([-sin, sin], axis=-1)


def kernel(x_prompt, x_sample, cache_k_a, cache_v_a, state_b, state_c_C, state_c_n, state_c_m, cache_k_d, cache_v_d, c, c_ctx, norm1_g, norm2_g, w_mod, b_mod, w_in_a, qnorm_a, knorm_a, w_out_a, w_in_b, b_f_b, lower_bounds_b, onorm_b, w_out_b, w_in_c, b_gates_c, onorm_c, w_out_c, w_in_d, rpb_d, w_out_d, w_router, b_router, w_gu, b_gu, w_dn, b_dn, final_g):
    n_ctx_seq, t_ctx, d = x_prompt.shape
    n_lat_seq, t_lat, _ = x_sample.shape
    depth = w_mod.shape[0]
    n_ctx = n_ctx_seq * t_ctx
    n_lat = n_lat_seq * t_lat
    n = n_ctx + n_lat
    tm = TOKEN_TILE
    assert t_ctx % tm == 0 and t_lat % tm == 0 and n_lat_seq + 1 <= 8

    lb_cum = jnp.cumsum(jax.nn.softmax(lower_bounds_b.astype(F32), axis=0), axis=0)
    lb_all = lb_cum - lb_cum[0]

    cond8 = jnp.zeros((8, d), F32).at[0].set(c_ctx).at[1:1 + n_lat_seq].set(c)
    mod = _mod_call(cond8, w_mod, b_mod)
    tile_row = np.concatenate([np.zeros(n_ctx // tm, np.int32),
                               1 + np.repeat(np.arange(n_lat_seq, dtype=np.int32), t_lat // tm)])

    x = jnp.concatenate([x_prompt.reshape(n_ctx, d), x_sample.reshape(n_lat, d)], axis=0)
    outs = {}
    for i in range(depth):
        kind = i % 4
        j = i // 4
        modsel = mod[i].reshape(8, 6, d)[tile_row]
        if kind == 0:
            proj = _inproj_call(x, modsel, norm1_g[i], w_in_a[j].astype(BF16))
            o_ctx, k_new = _gqa_ctx_call(proj, qnorm_a[j], knorm_a[j], n_ctx_seq, t_ctx)
            cosd, sind = _rope_tables(t_lat, 128)
            o_lat = _gqa_lat_call(proj, n_ctx, n_lat_seq, t_lat,
                                  cache_k_a[:, j].reshape(n_lat_seq, -1, 256),
                                  cache_v_a[:, j].reshape(n_lat_seq, -1, 256), cosd, sind,
                                  qnorm_a[j], knorm_a[j])
            outs["k_a"] = k_new.reshape(n_ctx_seq, 1, t_ctx, 2, 128)
            outs["v_a"] = proj[:n_ctx, 1280:1536].reshape(n_ctx_seq, 1, t_ctx, 2, 128)
            x = _outproj_call("plain", [(o_ctx, o_lat)], None, w_out_a[j].astype(BF16), x, modsel)
        elif kind == 1:
            proj = _inproj_call(x, modsel, norm1_g[i], w_in_b[j].astype(BF16))
            o_dirs, s_dirs = [], []
            for reverse in (False, True):
                oc, sc = _hgrn_call(proj, 0, n_ctx_seq, t_ctx, b_f_b[j], lb_all[i], None, reverse)
                ol, _ = _hgrn_call(proj, n_ctx, n_lat_seq, t_lat, b_f_b[j], lb_all[i], state_b[:, j], reverse)
                o_dirs.append((oc, ol))
                s_dirs.append(sc)
            outs["s_b"] = jnp.concatenate(s_dirs, axis=1)[:, None]
            x = _outproj_call("hgrn", o_dirs, (proj, 4, onorm_b[j].reshape(1, 128)),
                              w_out_b[j].astype(BF16), x, modsel)
        elif kind == 2:
            w_c = jnp.pad(w_in_c[j], ((0, 0), (0, 128 - 32))).astype(BF16)
            bg = jnp.pad(b_gates_c[j].reshape(1, 32), ((0, 0), (0, 128 - 32)))
            proj = _inproj_call(x, modsel, norm1_g[i], w_c)
            state = (state_c_C[:, j], state_c_n[:, j][:, :, :, None, :],
                     jnp.pad(state_c_m[:, j], ((0, 0), (0, 0), (0, 120)))[:, :, None, :])
            o_dirs, st = [], []
            for reverse in (False, True):
                oc, cc, nc, mc = _mlstm_call(proj, 0, n_ctx_seq, t_ctx, bg, None, reverse)
                ol, _, _, _ = _mlstm_call(proj, n_ctx, n_lat_seq, t_lat, bg, state, reverse)
                o_dirs.append((oc, ol))
                st.append((cc, nc, mc))
            outs["c_C"] = jnp.concatenate([st[0][0], st[1][0]], axis=1)[:, None]
            outs["c_n"] = jnp.concatenate([st[0][1], st[1][1]], axis=1)[:, None, :, :, 0, :]
            outs["c_m"] = jnp.concatenate([st[0][2], st[1][2]], axis=1)[:, None, :, 0, :8]
            x = _outproj_call("mlstm", o_dirs, (proj, 2, onorm_c[j].reshape(1, 128)),
                              w_out_c[j].astype(BF16), x, modsel)
        else:
            proj = _inproj_call(x, modsel, norm1_g[i], w_in_d[j].astype(BF16))
            o_ctx = _mha_ctx_call(proj, n_ctx_seq, t_ctx)
            tz = _na_bias_call(rpb_d[j])
            o_lat = _na_call(proj, n_ctx, n_lat_seq, t_lat,
                             cache_k_d[:, j].reshape(n_lat_seq, -1, d),
                             cache_v_d[:, j].reshape(n_lat_seq, -1, d), tz)
            outs["k_d"] = proj[:n_ctx, d:2 * d].reshape(n_ctx_seq, 1, t_ctx, 16, 64)
            outs["v_d"] = proj[:n_ctx, 2 * d:3 * d].reshape(n_ctx_seq, 1, t_ctx, 16, 64)
            x = _outproj_call("plain", [(o_ctx, o_lat)], None, w_out_d[j].astype(BF16), x, modsel)
        x = _moe_layer(i, x, modsel, norm2_g[i], w_router[i], b_router[i], w_gu, b_gu, w_dn, b_dn,
                       final_g, final_norm=(i == depth - 1))

    y_prompt = x[:n_ctx].reshape(n_ctx_seq, t_ctx, d)
    y_sample = x[n_ctx:].reshape(n_lat_seq, t_lat, d)
    return (y_prompt, y_sample, outs["k_a"], outs["v_a"], outs["s_b"], outs["c_C"], outs["c_n"], outs["c_m"],
            outs["k_d"], outs["v_d"])
```

```python
import functools

import numpy as np
import jax
import jax.numpy as jnp
from jax import lax
from jax.experimental import pallas as pl
from jax.experimental.pallas import tpu as pltpu

F32 = jnp.float32
BF16 = jnp.bfloat16
I32 = jnp.int32

NORM_EPS = 1e-6
GRID_W = 64
ROPE_THETA = 10000.0
TOP_K = 4
HGRN_CHUNK = 128
GATE_SOFTCAP = 15.0
NA_ROWS = 8
NA_COLS = 16
SWIGLU_ALPHA = 1.702
SWIGLU_LIMIT = 7.0
NEG_BIG = -1e30

LANES = 128
ROW_SUBLANES = 8
TOKEN_TILE = 256
MOE_BLOCK = 256
V7X_VMEM_LIMIT = 52 * 1024 * 1024


def _params(sem, vmem=V7X_VMEM_LIMIT):
    return pltpu.CompilerParams(dimension_semantics=sem, vmem_limit_bytes=vmem)


def _bdot(a, b):
    return jnp.dot(a.astype(BF16), b.astype(BF16), preferred_element_type=F32)


def _bdot_nt(a, b):
    return lax.dot_general(a.astype(BF16), b.astype(BF16), (((1,), (1,)), ((), ())),
                           preferred_element_type=F32)


def _bdot_tn(a, b):
    return lax.dot_general(a.astype(BF16), b.astype(BF16), (((0,), (0,)), ((), ())),
                           preferred_element_type=F32)


def _fdot(a, b):
    return jnp.dot(a, b, preferred_element_type=F32, precision=lax.Precision.HIGHEST)


def _rms(x, g):
    return x * lax.rsqrt(jnp.mean(x * x, axis=-1, keepdims=True) + NORM_EPS) * g


def _rms_heads(x, g, n_heads, hd):
    return jnp.concatenate([_rms(x[:, h * hd:(h + 1) * hd], g) for h in range(n_heads)], axis=-1)


def _sigmoid(x):
    return 1.0 / (1.0 + jnp.exp(-x))


def _silu(x):
    return x * _sigmoid(x)


def _softmax_rows(parts):
    m = parts[0].max(axis=-1, keepdims=True)
    for p in parts[1:]:
        m = jnp.maximum(m, p.max(axis=-1, keepdims=True))
    es = [jnp.exp(p - m) for p in parts]
    den = es[0].sum(axis=-1, keepdims=True)
    for e in es[1:]:
        den = den + e.sum(axis=-1, keepdims=True)
    return es, den


def _mod_kernel(c_ref, w_ref, b_ref, o_ref):
    o_ref[0] = _bdot(_silu(c_ref[...]), w_ref[0]) + b_ref[0]


def _mod_call(cond8, w_mod, b_mod):
    depth, d, d6 = w_mod.shape
    tn = 1024
    return pl.pallas_call(
        _mod_kernel,
        grid=(depth, d6 // tn),
        in_specs=[pl.BlockSpec((8, d), lambda i, j: (0, 0)),
                  pl.BlockSpec((1, d, tn), lambda i, j: (i, 0, j)),
                  pl.BlockSpec((1, 1, tn), lambda i, j: (i, 0, j))],
        out_specs=pl.BlockSpec((1, 8, tn), lambda i, j: (i, 0, j)),
        out_shape=jax.ShapeDtypeStruct((depth, 8, d6), F32),
        compiler_params=_params(("arbitrary", "arbitrary")),
        name="adaln_mod",
    )(cond8, w_mod, b_mod.reshape(depth, 1, d6))


def _inproj_kernel(x_ref, m_ref, g_ref, w_ref, o_ref):
    h = _rms(x_ref[...], g_ref[...]) * (1.0 + m_ref[0, 1:2, :]) + m_ref[0, 0:1, :]
    o_ref[...] = _bdot(h, w_ref[...])


def _inproj_call(x, modsel, g, w):
    n, d = x.shape
    wout = w.shape[1]
    tm = TOKEN_TILE
    return pl.pallas_call(
        _inproj_kernel,
        grid=(n // tm,),
        in_specs=[pl.BlockSpec((tm, d), lambda i: (i, 0)),
                  pl.BlockSpec((1, 6, d), lambda i: (i, 0, 0)),
                  pl.BlockSpec((1, d), lambda i: (0, 0)),
                  pl.BlockSpec((d, wout), lambda i: (0, 0))],
        out_specs=pl.BlockSpec((tm, wout), lambda i: (i, 0)),
        out_shape=jax.ShapeDtypeStruct((n, wout), F32),
        compiler_params=_params(("arbitrary",)),
        name="inproj",
    )(x, modsel, g.reshape(1, d), w)


def _outproj_kernel(*refs, kind, n_pairs, n_ctx_tiles, n_heads, hd):
    in_ctx = pl.program_id(0) < n_ctx_tiles
    mix = [jnp.where(in_ctx, refs[2 * p][...], refs[2 * p + 1][...]) for p in range(n_pairs)]
    rest = refs[2 * n_pairs:]
    if kind == "plain":
        w_ref, x_ref, m_ref, y_ref = rest
        o = mix[0]
    else:
        gate_ref, on_ref, w_ref, x_ref, m_ref, y_ref = rest
        o = _rms_heads(mix[0] + mix[1], on_ref[...], n_heads, hd)
        o = o * _silu(gate_ref[...]) if kind == "hgrn" else _sigmoid(gate_ref[...]) * o
    y_ref[...] = x_ref[...] + m_ref[0, 2:3, :] * _bdot(o, w_ref[...])


def _outproj_call(kind, pairs, gate, w, x, modsel):
    n, d = x.shape
    tm = TOKEN_TILE
    n_ctx_tiles = pairs[0][0].shape[0] // tm
    specs, args = [], []
    for a_ctx, a_lat in pairs:
        specs += [pl.BlockSpec((tm, d), lambda i: (jnp.minimum(i, n_ctx_tiles - 1), 0)),
                  pl.BlockSpec((tm, d), lambda i: (jnp.maximum(i - n_ctx_tiles, 0), 0))]
        args += [a_ctx, a_lat]
    if gate is not None:
        proj, cb, on = gate
        specs += [pl.BlockSpec((tm, d), lambda i: (i, cb)), pl.BlockSpec(on.shape, lambda i: (0, 0))]
        args += [proj, on]
    specs += [pl.BlockSpec(w.shape, lambda i: (0, 0)),
              pl.BlockSpec((tm, d), lambda i: (i, 0)),
              pl.BlockSpec((1, 6, d), lambda i: (i, 0, 0))]
    args += [w, x, modsel]
    body = functools.partial(_outproj_kernel, kind=kind, n_pairs=len(pairs), n_ctx_tiles=n_ctx_tiles,
                             n_heads=8, hd=128)
    return pl.pallas_call(
        body,
        grid=(n // tm,),
        in_specs=specs,
        out_specs=pl.BlockSpec((tm, d), lambda i: (i, 0)),
        out_shape=jax.ShapeDtypeStruct((n, d), F32),
        compiler_params=_params(("arbitrary",)),
        name="outproj_" + kind,
    )(*args)


def _gqa_ctx_kernel(p_ref, qn_ref, kn_ref, o_ref, k_ref, *, n_heads, n_kv, hd):
    rep = n_heads // n_kv
    scale = hd ** -0.5
    koff = n_heads * hd
    voff = koff + n_kv * hd
    ks = [_rms(p_ref[:, koff + g * hd: koff + (g + 1) * hd], kn_ref[...]) for g in range(n_kv)]
    k_ref[...] = jnp.concatenate(ks, axis=-1)
    for h in range(n_heads):
        g = h // rep
        q = _rms(p_ref[:, h * hd:(h + 1) * hd], qn_ref[...])
        s = _bdot_nt(q, ks[g]) * scale
        (e,), den = _softmax_rows([s])
        o_ref[:, h * hd:(h + 1) * hd] = _bdot(e / den, p_ref[:, voff + g * hd: voff + (g + 1) * hd])


def _gqa_ctx_call(proj, qn, kn, n_seq, t):
    n_heads, n_kv, hd = 8, 2, 128
    win = proj.shape[1]
    body = functools.partial(_gqa_ctx_kernel, n_heads=n_heads, n_kv=n_kv, hd=hd)
    return pl.pallas_call(
        body,
        grid=(n_seq,),
        in_specs=[pl.BlockSpec((t, win), lambda b: (b, 0)),
                  pl.BlockSpec((1, hd), lambda b: (0, 0)),
                  pl.BlockSpec((1, hd), lambda b: (0, 0))],
        out_specs=[pl.BlockSpec((t, n_heads * hd), lambda b: (b, 0)),
                   pl.BlockSpec((t, n_kv * hd), lambda b: (b, 0))],
        out_shape=[jax.ShapeDtypeStruct((n_seq * t, n_heads * hd), F32),
                   jax.ShapeDtypeStruct((n_seq * t, n_kv * hd), F32)],
        compiler_params=_params(("arbitrary",)),
        name="gqa_ctx",
    )(proj, qn.reshape(1, hd), kn.reshape(1, hd))


def _rope(x, cosd, sind):
    return x * cosd + pltpu.roll(x, x.shape[-1] // 2, 1) * sind


def _gqa_lat_kernel(pq_ref, pkv_ref, kc_ref, vc_ref, cq_ref, sq_ref, ck_ref, sk_ref, qn_ref, kn_ref,
                    o_ref, k_scr, v_scr, *, n_heads, n_kv, hd, t_ctx):
    rep = n_heads // n_kv
    scale = hd ** -0.5

    @pl.when(pl.program_id(1) == 0)
    def _():
        k_scr[0:t_ctx, :] = kc_ref[0].astype(BF16)
        v_scr[0:t_ctx, :] = vc_ref[0].astype(BF16)
        for g in range(n_kv):
            k = _rms(pkv_ref[:, g * hd:(g + 1) * hd], kn_ref[...])
            k_scr[t_ctx:, g * hd:(g + 1) * hd] = _rope(k, ck_ref[...], sk_ref[...]).astype(BF16)
        v_scr[t_ctx:, :] = pkv_ref[:, n_kv * hd:].astype(BF16)

    for h in range(n_heads):
        g = h // rep
        q = _rope(_rms(pq_ref[:, h * hd:(h + 1) * hd], qn_ref[...]), cq_ref[...], sq_ref[...])
        s = _bdot_nt(q, k_scr[:, g * hd:(g + 1) * hd]) * scale
        (e,), den = _softmax_rows([s])
        o_ref[:, h * hd:(h + 1) * hd] = _bdot(e / den, v_scr[:, g * hd:(g + 1) * hd])


def _gqa_lat_call(proj, row0, n_seq, t, cache_k, cache_v, cosd, sind, qn, kn):
    n_heads, n_kv, hd = 8, 2, 128
    tq = 256
    t_ctx = cache_k.shape[1]
    nq = t // tq
    kvw = 2 * n_kv * hd
    qblk0 = row0 // tq
    sblk0 = row0 // t
    body = functools.partial(_gqa_lat_kernel, n_heads=n_heads, n_kv=n_kv, hd=hd, t_ctx=t_ctx)
    return pl.pallas_call(
        body,
        grid=(n_seq, nq),
        in_specs=[pl.BlockSpec((tq, n_heads * hd), lambda b, i: (qblk0 + b * nq + i, 0)),
                  pl.BlockSpec((t, kvw), lambda b, i: (sblk0 + b, (n_heads * hd) // kvw)),
                  pl.BlockSpec((1, t_ctx, n_kv * hd), lambda b, i: (b, 0, 0)),
                  pl.BlockSpec((1, t_ctx, n_kv * hd), lambda b, i: (b, 0, 0)),
                  pl.BlockSpec((tq, hd), lambda b, i: (i, 0)),
                  pl.BlockSpec((tq, hd), lambda b, i: (i, 0)),
                  pl.BlockSpec((t, hd), lambda b, i: (0, 0)),
                  pl.BlockSpec((t, hd), lambda b, i: (0, 0)),
                  pl.BlockSpec((1, hd), lambda b, i: (0, 0)),
                  pl.BlockSpec((1, hd), lambda b, i: (0, 0))],
        out_specs=pl.BlockSpec((tq, n_heads * hd), lambda b, i: (b * nq + i, 0)),
        out_shape=jax.ShapeDtypeStruct((n_seq * t, n_heads * hd), F32),
        scratch_shapes=[pltpu.VMEM((t_ctx + t, n_kv * hd), BF16),
                        pltpu.VMEM((t_ctx + t, n_kv * hd), BF16)],
        compiler_params=_params(("arbitrary", "arbitrary")),
        name="gqa_latent",
    )(proj, proj, cache_k, cache_v, cosd, sind, cosd, sind, qn.reshape(1, hd), kn.reshape(1, hd))


def _mha_ctx_kernel(q_ref, k_ref, v_ref, o_ref, *, hd):
    scale = hd ** -0.5
    for h in range(q_ref.shape[1] // hd):
        sl = slice(h * hd, (h + 1) * hd)
        s = _bdot_nt(q_ref[:, sl], k_ref[:, sl]) * scale
        (e,), den = _softmax_rows([s])
        o_ref[:, sl] = _bdot(e / den, v_ref[:, sl])


def _mha_ctx_call(proj, n_seq, t):
    hd = 64
    d = proj.shape[1] // 3
    body = functools.partial(_mha_ctx_kernel, hd=hd)
    return pl.pallas_call(
        body,
        grid=(n_seq,),
        in_specs=[pl.BlockSpec((t, d), lambda b: (b, 0)),
                  pl.BlockSpec((t, d), lambda b: (b, 1)),
                  pl.BlockSpec((t, d), lambda b: (b, 2))],
        out_specs=pl.BlockSpec((t, d), lambda b: (b, 0)),
        out_shape=jax.ShapeDtypeStruct((n_seq * t, d), F32),
        compiler_params=_params(("arbitrary",)),
        name="mha_ctx",
    )(proj, proj, proj)


def _na_bias_kernel(rpb_ref, o_ref, *, n_rel_rows, n_rel_cols):
    h = pl.program_id(0)
    w_io = lax.broadcasted_iota(I32, (GRID_W, 2 * GRID_W), 0)
    lane = lax.broadcasted_iota(I32, (GRID_W, 2 * GRID_W), 1)
    ck = jnp.where(lane < GRID_W, lane, lane - GRID_W)
    c_start = jnp.clip(w_io - NA_COLS // 2, 0, GRID_W - NA_COLS)
    in_win = (ck >= c_start) & (ck < c_start + NA_COLS)
    rel = ck - w_io + (NA_COLS - 1)
    base = h * (n_rel_rows * n_rel_cols)
    tiles = []
    for j in range(n_rel_rows):
        acc = jnp.zeros((GRID_W, 2 * GRID_W), F32)
        for jj in range(n_rel_cols):
            acc = jnp.where(rel == jj, rpb_ref[base + j * n_rel_cols + jj], acc)
        tiles.append(jnp.where(in_win, acc, NEG_BIG))
    for j in range(n_rel_rows):
        hi = tiles[j + 1] if j + 1 < n_rel_rows else jnp.full((GRID_W, 2 * GRID_W), NEG_BIG, F32)
        o_ref[0, j] = jnp.where(lane < GRID_W, tiles[j], hi)


def _na_bias_call(rpb):
    n_heads, nrr, nrc = rpb.shape
    body = functools.partial(_na_bias_kernel, n_rel_rows=nrr, n_rel_cols=nrc)
    return pl.pallas_call(
        body,
        grid=(n_heads,),
        in_specs=[pl.BlockSpec(memory_space=pltpu.SMEM)],
        out_specs=pl.BlockSpec((1, nrr, GRID_W, 2 * GRID_W), lambda h: (h, 0, 0, 0)),
        out_shape=jax.ShapeDtypeStruct((n_heads, nrr, GRID_W, 2 * GRID_W), F32),
        compiler_params=_params(("arbitrary",)),
        name="na_bias",
    )(rpb.reshape(-1))


NA_QROWS = 4
NA_KROWS = 12


def _na_kernel(q_ref, k0_ref, k1_ref, k2_ref, v0_ref, v1_ref, v2_ref, kc_ref, vc_ref, tz_ref, o_ref,
               *, hd, n_grid_rows):
    scale = hd ** -0.5
    blk = pl.program_id(1)
    kstart = jnp.clip(blk * NA_QROWS - NA_ROWS // 2, 0, n_grid_rows - NA_KROWS)
    lane = lax.broadcasted_iota(I32, (GRID_W, 2 * GRID_W), 1)
    n_rel = tz_ref.shape[1]
    for hh in range(q_ref.shape[1] // hd):
        sl = slice(hh * hd, (hh + 1) * hd)
        rows = []
        for rq_l in range(NA_QROWS):
            rq = blk * NA_QROWS + rq_l
            r_start = jnp.clip(rq - NA_ROWS // 2, 0, n_grid_rows - NA_ROWS)
            tiles = []
            for m in range(NA_KROWS // 2):
                rk = kstart + 2 * m
                j = jnp.clip(rk - rq + (NA_ROWS - 1), 0, n_rel - 1)
                ok0 = (rk >= r_start) & (rk < r_start + NA_ROWS)
                ok1 = (rk + 1 >= r_start) & (rk + 1 < r_start + NA_ROWS)
                pen = jnp.where(lane < GRID_W, jnp.where(ok0, 0.0, NEG_BIG), jnp.where(ok1, 0.0, NEG_BIG))
                tiles.append(tz_ref[hh, pl.ds(j, 1)][0] + pen)
            rows.append(jnp.concatenate(tiles, axis=-1))
        bias = jnp.concatenate(rows, axis=0)
        q = q_ref[:, sl]
        k_loc = jnp.concatenate([k0_ref[:, sl], k1_ref[:, sl], k2_ref[:, sl]], axis=0)
        v_loc = jnp.concatenate([v0_ref[:, sl], v1_ref[:, sl], v2_ref[:, sl]], axis=0)
        s_loc = _bdot_nt(q, k_loc) * scale + bias
        s_ctx = _bdot_nt(q, kc_ref[0, :, sl]) * scale
        (e_loc, e_ctx), den = _softmax_rows([s_loc, s_ctx])
        o_ref[:, sl] = (_bdot(e_loc, v_loc) + _bdot(e_ctx, vc_ref[0, :, sl])) / den


def _na_call(proj, row0, n_seq, t, cache_k, cache_v, tz):
    hd, cw = 64, 128
    d = proj.shape[1] // 3
    ncb = d // cw
    tq = NA_QROWS * GRID_W
    nq = t // tq
    n_grid_rows = t // GRID_W
    t_ctx = cache_k.shape[1]
    qblk0 = row0 // tq

    def kv_map(which, j):
        def index_map(b, i, c):
            ks = jnp.clip(i * NA_QROWS - NA_ROWS // 2, 0, n_grid_rows - NA_KROWS) // NA_QROWS
            return (qblk0 + b * nq + ks + j, which * ncb + c)
        return index_map

    body = functools.partial(_na_kernel, hd=hd, n_grid_rows=n_grid_rows)
    return pl.pallas_call(
        body,
        grid=(n_seq, nq, ncb),
        in_specs=[pl.BlockSpec((tq, cw), lambda b, i, c: (qblk0 + b * nq + i, c))]
                 + [pl.BlockSpec((tq, cw), kv_map(1, j)) for j in range(3)]
                 + [pl.BlockSpec((tq, cw), kv_map(2, j)) for j in range(3)]
                 + [pl.BlockSpec((1, t_ctx, cw), lambda b, i, c: (b, 0, c)),
                    pl.BlockSpec((1, t_ctx, cw), lambda b, i, c: (b, 0, c)),
                    pl.BlockSpec((cw // hd,) + tz.shape[1:], lambda b, i, c: (c, 0, 0, 0))],
        out_specs=pl.BlockSpec((tq, cw), lambda b, i, c: (b * nq + i, c)),
        out_shape=jax.ShapeDtypeStruct((n_seq * t, d), F32),
        compiler_params=_params(("arbitrary", "arbitrary", "arbitrary")),
        name="nbr_attn",
    )(proj, proj, proj, proj, proj, proj, proj, cache_k, cache_v, tz)


def _hgrn_kernel(q_ref, v_ref, f_ref, bf_ref, lb_ref, s0_ref, o_ref, s_ref, st_scr,
                 *, reverse, n_heads, dk, has_s0, n_blk):
    c = pl.program_id(1)
    L = HGRN_CHUNK
    tb = q_ref.shape[0]

    @pl.when(c == 0)
    def _():
        for h in range(n_heads):
            if has_s0:
                st_scr[h] = s0_ref[0, 0, h].T
            else:
                st_scr[h] = jnp.zeros_like(st_scr[h])

    row = lax.broadcasted_iota(I32, (L, L), 0)
    col = lax.broadcasted_iota(I32, (L, L), 1)
    tri = jnp.where((col >= row) if reverse else (col <= row), 1.0, 0.0).astype(F32)
    eye = row == col
    halves = [L >> (i + 1) for i in range(L.bit_length() - 1)]
    same_pair = [(row // (2 * hf)) == (col // (2 * hf)) for hf in halves]
    rio = lax.broadcasted_iota(I32, (L, 1), 0)
    r8 = lax.broadcasted_iota(I32, (ROW_SUBLANES, 1), 0)
    lb = lb_ref[0]
    bf = bf_ref[0]
    n_chunks = tb // L

    def boundary_rows(cum, hf):
        blk = 2 * hf
        off = hf if reverse else hf - 1
        width = cum.shape[1]
        if blk >= ROW_SUBLANES:
            return jnp.concatenate([jnp.broadcast_to(cum[a + off:a + off + 1, :], (blk, width))
                                    for a in range(0, L, blk)], axis=0)
        groups = []
        for g in range(0, L, ROW_SUBLANES):
            ref = jnp.broadcast_to(cum[g + off:g + off + 1, :], (ROW_SUBLANES, width))
            for a in range(blk, ROW_SUBLANES, blk):
                ref = jnp.where(r8 >= a, jnp.broadcast_to(cum[g + a + off:g + a + off + 1, :],
                                                          (ROW_SUBLANES, width)), ref)
            groups.append(ref)
        return jnp.concatenate(groups, axis=0)

    def chunk_step(jj, carry):
        jc = (n_chunks - 1 - jj) if reverse else jj
        rs = pl.ds(pl.multiple_of(jc * L, L), L)
        q = _silu(q_ref[rs, :]) * (dk ** -0.5)
        v = v_ref[rs, :]
        f = lb + (1.0 - lb) * _sigmoid(f_ref[rs, :] + bf)
        logf = jnp.log(f)
        kk = 1.0 - f
        cum = _fdot(tri, logf)
        end = cum[0:1, :] if reverse else cum[L - 1:L, :]
        qd = q * jnp.exp(cum)
        kd = kk * jnp.exp(end - cum)
        e_end = jnp.exp(end)
        q_lv, k_lv = [], []
        for hf in halves:
            ref = boundary_rows(cum, hf)
            upper = (rio & hf) != 0
            is_query = jnp.logical_not(upper) if reverse else upper
            decay = jnp.exp(jnp.where(is_query, cum - ref, ref - cum))
            q_lv.append(jnp.where(is_query, q * decay, 0.0).astype(BF16))
            k_lv.append(jnp.where(is_query, 0.0, kk * decay).astype(BF16))
        qk_diag = q * kk
        outs = []
        for h in range(n_heads):
            hs = slice(h * dk, (h + 1) * dk)
            attn = jnp.where(eye, jnp.sum(qk_diag[:, hs], axis=-1, keepdims=True), 0.0)
            for lv in range(len(halves)):
                attn = attn + jnp.where(same_pair[lv], _bdot_nt(q_lv[lv][:, hs], k_lv[lv][:, hs]), 0.0)
            outs.append(_bdot_nt(qd[:, hs], st_scr[h]) + _bdot(attn, v[:, hs]))
        o_ref[rs, :] = jnp.concatenate(outs, axis=-1)
        for h in range(n_heads):
            hs = slice(h * dk, (h + 1) * dk)
            st_scr[h] = st_scr[h] * e_end[:, hs] + _bdot_tn(v[:, hs], kd[:, hs])
        return carry

    lax.fori_loop(0, n_chunks, chunk_step, 0)

    @pl.when(c == n_blk - 1)
    def _():
        for h in range(n_heads):
            s_ref[0, 0, h] = st_scr[h].T


def _hgrn_call(proj, row0, n_seq, t, b_f, lb, s0, reverse):
    n_heads, dk = 8, 128
    d = n_heads * dk
    tb = 256
    n_blk = t // tb
    blk0 = row0 // tb
    di = 1 if reverse else 0
    has_s0 = s0 is not None
    if not has_s0:
        s0 = jnp.zeros((1, 2, n_heads, dk, dk), F32)

    def tok(b, c):
        return blk0 + b * n_blk + ((n_blk - 1 - c) if reverse else c)

    body = functools.partial(_hgrn_kernel, reverse=reverse, n_heads=n_heads, dk=dk,
                             has_s0=has_s0, n_blk=n_blk)
    return pl.pallas_call(
        body,
        grid=(n_seq, n_blk),
        in_specs=[pl.BlockSpec((tb, d), lambda b, c: (tok(b, c), 0)),
                  pl.BlockSpec((tb, d), lambda b, c: (tok(b, c), 1)),
                  pl.BlockSpec((tb, d), lambda b, c: (tok(b, c), 2 + di)),
                  pl.BlockSpec((1, 1, d), lambda b, c: (di, 0, 0)),
                  pl.BlockSpec((1, 1, d), lambda b, c: (di, 0, 0)),
                  pl.BlockSpec((1, 1, n_heads, dk, dk),
                               (lambda b, c: (b, di, 0, 0, 0)) if has_s0 else (lambda b, c: (0, 0, 0, 0, 0)))],
        out_specs=[pl.BlockSpec((tb, d), lambda b, c: (tok(b, c) - blk0, 0)),
                   pl.BlockSpec((1, 1, n_heads, dk, dk), lambda b, c: (b, 0, 0, 0, 0))],
        out_shape=[jax.ShapeDtypeStruct((n_seq * t, d), F32),
                   jax.ShapeDtypeStruct((n_seq, 1, n_heads, dk, dk), F32)],
        scratch_shapes=[pltpu.VMEM((n_heads, dk, dk), F32)],
        compiler_params=_params(("arbitrary", "arbitrary")),
        name="hgrn2_bw" if reverse else "hgrn2_fw",
    )(proj, proj, proj, b_f.reshape(2, 1, d), lb.reshape(2, 1, d), s0)


def _log_sigmoid(x):
    return jnp.minimum(x, 0.0) - jnp.log(1.0 + jnp.exp(-jnp.abs(x)))


def _mlstm_kernel(q_ref, k_ref, v_ref, g_ref, bg_ref, c0_ref, n0_ref, m0_ref,
                  o_ref, c_out, n_out, m_out, c_scr, n_scr, m_scr,
                  *, reverse, n_heads, dqk, dv, has_state, n_blk):
    c = pl.program_id(1)
    L = q_ref.shape[0]
    i_off = 2 * n_heads if reverse else 0
    f_off = i_off + n_heads

    @pl.when(c == 0)
    def _():
        if has_state:
            c_scr[...] = c0_ref[0, 0]
            n_scr[...] = n0_ref[0, 0]
            m_scr[...] = m0_ref[0, 0]
        else:
            c_scr[...] = jnp.zeros_like(c_scr)
            n_scr[...] = jnp.zeros_like(n_scr)
            m_scr[...] = jnp.zeros_like(m_scr)

    gates = GATE_SOFTCAP * jnp.tanh((g_ref[...] + bg_ref[...]) / GATE_SOFTCAP)
    logf = _log_sigmoid(gates)
    row = lax.broadcasted_iota(I32, (L, L), 0)
    col = lax.broadcasted_iota(I32, (L, L), 1)
    causal = (col >= row) if reverse else (col <= row)
    tri = jnp.where(causal, 1.0, 0.0).astype(F32)
    cum = _fdot(tri, logf)
    cum_t = cum.T
    gates_t = gates.T
    e_row = 0 if reverse else L - 1
    outs = []
    for h in range(n_heads):
        qh = q_ref[:, h * dqk:(h + 1) * dqk] * (dqk ** -0.5)
        kh = k_ref[:, h * dqk:(h + 1) * dqk]
        vh = v_ref[:, h * dv:(h + 1) * dv]
        cum_c = cum[:, f_off + h:f_off + h + 1]
        cum_r = cum_t[f_off + h:f_off + h + 1, :]
        i_c = gates[:, i_off + h:i_off + h + 1]
        i_r = gates_t[i_off + h:i_off + h + 1, :]
        m_prev = m_scr[0:1, h:h + 1]
        d = jnp.where(causal, cum_c - cum_r + i_r, -jnp.inf)
        m_inter = cum_c + m_prev
        m_t = jnp.maximum(m_inter, d.max(axis=-1, keepdims=True))
        w_inter = jnp.exp(m_inter - m_t)
        w_intra = jnp.exp(d - m_t)
        qk = _bdot_nt(qh, kh) * w_intra
        num = w_inter * _bdot(qh, c_scr[h]) + _bdot(qk, vh)
        den = w_inter * jnp.sum(qh * n_scr[h], axis=-1, keepdims=True) + qk.sum(axis=-1, keepdims=True)
        outs.append(num / jnp.maximum(jnp.abs(den), jnp.exp(-m_t)))
        end = cum_c[e_row:e_row + 1, :]
        g_end_r = end - cum_r + i_r
        g_end_c = end - cum_c + i_c
        m_new = jnp.maximum(end + m_prev, g_end_r.max(axis=-1, keepdims=True))
        w_old = jnp.exp(end + m_prev - m_new)
        ks = kh * jnp.exp(g_end_c - m_new)
        c_scr[h] = w_old * c_scr[h] + _bdot_tn(ks, vh)
        n_scr[h] = w_old * n_scr[h] + ks.sum(axis=0, keepdims=True)
        m_scr[0:1, h:h + 1] = m_new
    o_ref[...] = jnp.concatenate(outs, axis=-1)

    @pl.when(c == n_blk - 1)
    def _():
        c_out[0, 0] = c_scr[...]
        n_out[0, 0] = n_scr[...]
        m_out[0, 0] = m_scr[...]


def _mlstm_call(proj, row0, n_seq, t, b_gates_pad, state, reverse):
    n_heads, dqk, dv = 8, 64, 128
    wq, wv = n_heads * dqk, n_heads * dv
    L = 256
    n_blk = t // L
    blk0 = row0 // L
    di = 1 if reverse else 0
    has_state = state is not None
    if has_state:
        c0, n0, m0 = state
        smap = lambda b, c: (b, di, 0, 0, 0)
        mmap = lambda b, c: (b, di, 0, 0)
    else:
        c0 = jnp.zeros((1, 1, n_heads, dqk, dv), F32)
        n0 = jnp.zeros((1, 1, n_heads, 1, dqk), F32)
        m0 = jnp.zeros((1, 1, 1, 128), F32)
        smap = lambda b, c: (0, 0, 0, 0, 0)
        mmap = lambda b, c: (0, 0, 0, 0)

    def tok(b, c):
        return blk0 + b * n_blk + ((n_blk - 1 - c) if reverse else c)

    body = functools.partial(_mlstm_kernel, reverse=reverse, n_heads=n_heads, dqk=dqk, dv=dv,
                             has_state=has_state, n_blk=n_blk)
    gate_cb = (2 * wq + 2 * wv) // 128
    return pl.pallas_call(
        body,
        grid=(n_seq, n_blk),
        in_specs=[pl.BlockSpec((L, wq), lambda b, c: (tok(b, c), 0)),
                  pl.BlockSpec((L, wq), lambda b, c: (tok(b, c), 1)),
                  pl.BlockSpec((L, wv), lambda b, c: (tok(b, c), (2 * wq) // wv)),
                  pl.BlockSpec((L, 128), lambda b, c: (tok(b, c), gate_cb)),
                  pl.BlockSpec((1, 128), lambda b, c: (0, 0)),
                  pl.BlockSpec((1, 1, n_heads, dqk, dv), smap),
                  pl.BlockSpec((1, 1, n_heads, 1, dqk), smap),
                  pl.BlockSpec((1, 1, 1, 128), mmap)],
        out_specs=[pl.BlockSpec((L, wv), lambda b, c: (tok(b, c) - blk0, 0)),
                   pl.BlockSpec((1, 1, n_heads, dqk, dv), lambda b, c: (b, 0, 0, 0, 0)),
                   pl.BlockSpec((1, 1, n_heads, 1, dqk), lambda b, c: (b, 0, 0, 0, 0)),
                   pl.BlockSpec((1, 1, 1, 128), lambda b, c: (b, 0, 0, 0))],
        out_shape=[jax.ShapeDtypeStruct((n_seq * t, wv), F32),
                   jax.ShapeDtypeStruct((n_seq, 1, n_heads, dqk, dv), F32),
                   jax.ShapeDtypeStruct((n_seq, 1, n_heads, 1, dqk), F32),
                   jax.ShapeDtypeStruct((n_seq, 1, 1, 128), F32)],
        scratch_shapes=[pltpu.VMEM((n_heads, dqk, dv), F32),
                        pltpu.VMEM((n_heads, 1, dqk), F32),
                        pltpu.VMEM((1, 128), F32)],
        compiler_params=_params(("arbitrary", "arbitrary")),
        name="mlstm_bw" if reverse else "mlstm_fw",
    )(proj, proj, proj, proj, b_gates_pad, c0, n0, m0)


def _moe_input(x_ref, m_ref, g_ref):
    return _rms(x_ref[...], g_ref[...]) * (1.0 + m_ref[0, 4:5, :]) + m_ref[0, 3:4, :]


def _route_kernel(x_ref, m_ref, g_ref, wr_ref, br_ref, idx_ref, gate_ref, rank_ref, cnt_ref, carry_scr,
                  *, n_experts):
    i = pl.program_id(0)
    tm = x_ref.shape[0]

    @pl.when(i == 0)
    def _():
        carry_scr[...] = jnp.zeros_like(carry_scr)

    h = _moe_input(x_ref, m_ref, g_ref)
    logits = lax.dot_general(wr_ref[...], h, (((1,), (1,)), ((), ())), preferred_element_type=F32,
                             precision=lax.Precision.HIGHEST) + br_ref[...]
    e_io = lax.broadcasted_iota(I32, (n_experts, tm), 0).astype(F32)
    work = logits
    vals, idxs = [], []
    chosen = jnp.zeros((n_experts, tm), F32)
    for _ in range(TOP_K):
        mx = work.max(axis=0, keepdims=True)
        ix = jnp.min(jnp.where(work == mx, e_io, float(n_experts)), axis=0, keepdims=True)
        hit = e_io == ix
        vals.append(mx)
        idxs.append(ix)
        chosen = jnp.where(hit, 1.0, chosen)
        work = jnp.where(hit, -jnp.inf, work)
    es = [jnp.exp(v - vals[0]) for v in vals]
    den = es[0] + es[1] + es[2] + es[3]
    srow = lax.broadcasted_iota(I32, (tm, tm), 0)
    scol = lax.broadcasted_iota(I32, (tm, tm), 1)
    before = jnp.where(srow < scol, 1.0, 0.0).astype(BF16)
    pos = jnp.dot(chosen.astype(BF16), before, preferred_element_type=F32) + carry_scr[...]
    ranks = [jnp.sum(jnp.where(e_io == ix, pos, 0.0), axis=0, keepdims=True) for ix in idxs]
    carry_scr[...] = carry_scr[...] + chosen.sum(axis=1, keepdims=True)
    idx_ref[...] = jnp.concatenate(idxs, axis=0).astype(I32)
    gate_ref[...] = jnp.concatenate([e / den for e in es], axis=0)
    rank_ref[...] = jnp.concatenate(ranks, axis=0).astype(I32)
    cnt_ref[...] = jnp.broadcast_to(carry_scr[...], cnt_ref.shape).astype(I32)


def _route_call(x, modsel, g, w_router, b_router):
    n, d = x.shape
    n_experts = w_router.shape[1]
    tm = TOKEN_TILE
    body = functools.partial(_route_kernel, n_experts=n_experts)
    return pl.pallas_call(
        body,
        grid=(n // tm,),
        in_specs=[pl.BlockSpec((tm, d), lambda i: (i, 0)),
                  pl.BlockSpec((1, 6, d), lambda i: (i, 0, 0)),
                  pl.BlockSpec((1, d), lambda i: (0, 0)),
                  pl.BlockSpec((n_experts, d), lambda i: (0, 0)),
                  pl.BlockSpec((n_experts, 1), lambda i: (0, 0))],
        out_specs=[pl.BlockSpec((TOP_K, tm), lambda i: (0, i)),
                   pl.BlockSpec((TOP_K, tm), lambda i: (0, i)),
                   pl.BlockSpec((TOP_K, tm), lambda i: (0, i)),
                   pl.BlockSpec((n_experts, 128), lambda i: (0, 0))],
        out_shape=[jax.ShapeDtypeStruct((TOP_K, n), I32),
                   jax.ShapeDtypeStruct((TOP_K, n), F32),
                   jax.ShapeDtypeStruct((TOP_K, n), I32),
                   jax.ShapeDtypeStruct((n_experts, 128), I32)],
        scratch_shapes=[pltpu.VMEM((n_experts, 1), F32)],
        compiler_params=_params(("arbitrary",)),
        name="moe_route",
    )(x, modsel, g.reshape(1, d), w_router.T, b_router.reshape(n_experts, 1))


def _slot_kernel(cnt_ref, idx_ref, rank_ref, dest_ref, binfo_ref, pad_ref, nused_ref, *, n_experts, n_blocks):
    cnt = cnt_ref[:, 0:1].astype(F32)
    padded = jnp.ceil(cnt * (1.0 / MOE_BLOCK)) * MOE_BLOCK
    er = lax.broadcasted_iota(I32, (n_experts, n_experts), 0)
    ec = lax.broadcasted_iota(I32, (n_experts, n_experts), 1)
    start_row = jnp.sum(jnp.where(er < ec, padded, 0.0), axis=0, keepdims=True)
    start_col = jnp.sum(jnp.where(er == ec, start_row, 0.0), axis=1, keepdims=True)
    end_col = start_col + padded
    idx = idx_ref[...]
    e_io = lax.broadcasted_iota(I32, (n_experts,) + idx.shape[1:], 0)
    rows = []
    for k in range(TOP_K):
        hit = e_io == idx[k:k + 1, :]
        rows.append(jnp.sum(jnp.where(hit, start_col, 0.0), axis=0, keepdims=True))
    dest_ref[...] = jnp.concatenate(rows, axis=0).astype(I32) + rank_ref[...]
    blk_start = (lax.broadcasted_iota(I32, (n_experts, n_blocks), 1) * MOE_BLOCK).astype(F32)
    n_done = jnp.sum(jnp.where(end_col <= blk_start, 1.0, 0.0), axis=0, keepdims=True)
    bexp = jnp.minimum(n_done, n_experts - 1.0)
    used_row = jnp.sum(jnp.where(er == ec, jnp.where(cnt > 0.0, 1.0, 0.0), 0.0), axis=0, keepdims=True)
    ecf = ec.astype(F32)
    next_col = jnp.min(jnp.where((ec > er) & (used_row > 0.0), ecf, float(n_experts)), axis=1, keepdims=True)
    ord_col = jnp.sum(jnp.where(ec < er, used_row, 0.0), axis=1, keepdims=True)
    par_col = ord_col - 2.0 * jnp.floor(ord_col * 0.5)
    mine = lax.broadcasted_iota(I32, (n_experts, n_blocks), 0).astype(F32) == bexp
    bnext = jnp.sum(jnp.where(mine, next_col, 0.0), axis=0, keepdims=True)
    bslot = jnp.sum(jnp.where(mine, par_col, 0.0), axis=0, keepdims=True)
    binfo_ref[...] = jnp.concatenate([bexp, bnext, bslot], axis=0).astype(I32)
    n_used = jnp.sum(padded, axis=0, keepdims=True) * (1.0 / MOE_BLOCK)
    nused_ref[...] = n_used.astype(I32)
    cnt_row = jnp.sum(jnp.where(er == ec, cnt, 0.0), axis=0, keepdims=True)
    padded_row = jnp.sum(jnp.where(er == ec, padded, 0.0), axis=0, keepdims=True)
    pad_ref[...] = jnp.concatenate([start_row + cnt_row, padded_row - cnt_row,
                                    jnp.broadcast_to(n_used, cnt_row.shape)], axis=0).astype(I32)


def _slot_call(counts, idx_t, rank_t, n_blocks):
    n_experts = counts.shape[0]
    n = idx_t.shape[1]
    tn = min(2048, n)
    body = functools.partial(_slot_kernel, n_experts=n_experts, n_blocks=n_blocks)
    return pl.pallas_call(
        body,
        grid=(n // tn,),
        in_specs=[pl.BlockSpec((n_experts, 128), lambda i: (0, 0)),
                  pl.BlockSpec((TOP_K, tn), lambda i: (0, i)),
                  pl.BlockSpec((TOP_K, tn), lambda i: (0, i))],
        out_specs=[pl.BlockSpec((TOP_K, tn), lambda i: (0, i)),
                   pl.BlockSpec((3, n_blocks), lambda i: (0, 0)),
                   pl.BlockSpec((3, n_experts), lambda i: (0, 0)),
                   pl.BlockSpec((1, 1), lambda i: (0, 0))],
        out_shape=[jax.ShapeDtypeStruct((TOP_K, n), I32),
                   jax.ShapeDtypeStruct((3, n_blocks), I32),
                   jax.ShapeDtypeStruct((3, n_experts), I32),
                   jax.ShapeDtypeStruct((1, 1), I32)],
        compiler_params=_params(("arbitrary",)),
        name="moe_slots",
    )(counts, idx_t, rank_t)


DMA_ISSUE_UNROLL = 8


def _to_row_tiles(ref, base, x):
    rows = x.shape[0]
    for c in range(ROW_SUBLANES):
        ref[pl.ds(base * ROW_SUBLANES + c, rows, stride=ROW_SUBLANES), :] = x[:, c * LANES:(c + 1) * LANES]


def _from_row_tiles(ref, base, rows, c):
    return ref[pl.ds(base * ROW_SUBLANES + c, rows, stride=ROW_SUBLANES), :]


def _row_tile(ref, r):
    return ref.at[pl.ds(pl.multiple_of(r * ROW_SUBLANES, ROW_SUBLANES), ROW_SUBLANES)]


def _zero_fill_padding(pad_ref, xs_ref, z_scr, sem):
    z_scr[...] = jnp.zeros_like(z_scr)
    n_experts = pad_ref.shape[1]
    bits = range(MOE_BLOCK.bit_length() - 2, -1, -1)

    def pieces(e):
        off, length = pad_ref[0, e], pad_ref[1, e]
        for bit in bits:
            size = 1 << bit
            done = (length >> (bit + 1)) << (bit + 1)
            copy = pltpu.make_async_copy(z_scr.at[pl.ds(0, size * ROW_SUBLANES)],
                                         xs_ref.at[pl.ds(pl.multiple_of((off + done) * ROW_SUBLANES, ROW_SUBLANES),
                                                         size * ROW_SUBLANES)], sem)
            yield (length & size) != 0, copy

    def tail_blocks():
        n_blocks = xs_ref.shape[0] // (MOE_BLOCK * ROW_SUBLANES)
        for b in range(n_blocks - n_experts, n_blocks):
            copy = pltpu.make_async_copy(z_scr, xs_ref.at[pl.ds(b * MOE_BLOCK * ROW_SUBLANES,
                                                                MOE_BLOCK * ROW_SUBLANES)], sem)
            yield b >= pad_ref[2, 0], copy

    def all_copies():
        for e in range(n_experts):
            yield from pieces(e)
        yield from tail_blocks()

    for needed, copy in all_copies():
        pl.when(needed)(copy.start)
    for needed, copy in all_copies():
        pl.when(needed)(copy.wait)


def _dispatch_kernel(pad_ref, dest_ref, x_ref, m_ref, g_ref, xs_ref, h_scr, z_scr, sem):
    tm = x_ref.shape[0]

    @pl.when(pl.program_id(0) == 0)
    def _():
        _zero_fill_padding(pad_ref, xs_ref, z_scr, sem.at[1])

    _to_row_tiles(h_scr, 0, _moe_input(x_ref, m_ref, g_ref))

    def start_row(r, carry):
        for k in range(TOP_K):
            pltpu.make_async_copy(_row_tile(h_scr, r), _row_tile(xs_ref, dest_ref[0, 0, k * tm + r]),
                                  sem.at[0]).start(priority=k % 2)
        return carry

    lax.fori_loop(0, tm, start_row, 0, unroll=DMA_ISSUE_UNROLL // TOP_K)
    for _ in range(TOP_K):
        pltpu.make_async_copy(h_scr, xs_ref.at[pl.ds(0, tm * ROW_SUBLANES)], sem.at[0]).wait()


def _dispatch_call(pad_info, dest_tiles, x, modsel, g, n_slots):
    n, d = x.shape
    tm = TOKEN_TILE
    assert d == ROW_SUBLANES * LANES
    return pl.pallas_call(
        _dispatch_kernel,
        grid=(n // tm,),
        in_specs=[pl.BlockSpec(memory_space=pltpu.SMEM),
                  pl.BlockSpec((1, 1, TOP_K * tm), lambda i: (i, 0, 0), memory_space=pltpu.SMEM),
                  pl.BlockSpec((tm, d), lambda i: (i, 0)),
                  pl.BlockSpec((1, 6, d), lambda i: (i, 0, 0)),
                  pl.BlockSpec((1, d), lambda i: (0, 0))],
        out_specs=pl.BlockSpec(memory_space=pl.ANY),
        out_shape=jax.ShapeDtypeStruct((n_slots * ROW_SUBLANES, LANES), F32),
        scratch_shapes=[pltpu.VMEM((tm * ROW_SUBLANES, LANES), F32),
                        pltpu.VMEM((MOE_BLOCK * ROW_SUBLANES, LANES), F32),
                        pltpu.SemaphoreType.DMA((2,))],
        compiler_params=_params(("arbitrary",)),
        name="moe_dispatch",
    )(pad_info, dest_tiles, x, modsel, g.reshape(1, d))


def _ffn_kernel(binfo_ref, nused_ref, xs_ref, wgu_hbm, bgu_ref, wdn_hbm, bdn_ref, ys_ref,
                wgu_f32, wdn_f32, wgu_scr, wdn_scr, sem, *, layer, n_experts, n_blocks):
    b = pl.program_id(0)
    d_ff = wdn_scr.shape[0]
    rows = xs_ref.shape[0] // ROW_SUBLANES

    def weight_copies(e, slot):
        return (pltpu.make_async_copy(wgu_hbm.at[layer, e], wgu_f32.at[slot], sem.at[slot]),
                pltpu.make_async_copy(wdn_hbm.at[layer, e], wdn_f32.at[slot], sem.at[slot]))

    @pl.when(b < nused_ref[0])
    def _():
        e = binfo_ref[b]
        prev = binfo_ref[jnp.maximum(b - 1, 0)]
        slot = binfo_ref[2 * n_blocks + b]

        @pl.when((b == 0) | (e != prev))
        def _():
            @pl.when(b == 0)
            def _():
                for cp in weight_copies(e, slot):
                    cp.start()

            nxt = binfo_ref[n_blocks + b]

            @pl.when(nxt < n_experts)
            def _():
                for cp in weight_copies(nxt, 1 - slot):
                    cp.start()

            for cp in weight_copies(e, slot):
                cp.wait()
            wgu_scr[...] = wgu_f32[slot].astype(BF16)
            wdn_scr[...] = wdn_f32[slot].astype(BF16)

        x = jnp.concatenate([_from_row_tiles(xs_ref, 0, rows, c).astype(BF16) for c in range(ROW_SUBLANES)],
                            axis=-1)
        gu = jnp.dot(x, wgu_scr[...], preferred_element_type=F32) + bgu_ref[0, 0]
        x_glu = jnp.minimum(gu[:, :d_ff], SWIGLU_LIMIT)
        x_lin = jnp.clip(gu[:, d_ff:], -SWIGLU_LIMIT, SWIGLU_LIMIT)
        hid = x_glu * _sigmoid(SWIGLU_ALPHA * x_glu) * (x_lin + 1.0)
        _to_row_tiles(ys_ref, 0, jnp.dot(hid.astype(BF16), wdn_scr[...], preferred_element_type=F32)
                      + bdn_ref[0, 0])

    @pl.when(b >= nused_ref[0])
    def _():
        ys_ref[...] = jnp.zeros_like(ys_ref)


def _ffn_call(layer, block_info, n_used, xs, w_gu, b_gu, w_dn, b_dn):
    depth, n_experts, d, d_ff2 = w_gu.shape
    d_ff = d_ff2 // 2
    blk_rows = MOE_BLOCK * ROW_SUBLANES
    n_blocks = xs.shape[0] // blk_rows

    def blk(b, nu):
        return jnp.maximum(jnp.minimum(b, nu[0] - 1), 0)

    grid_spec = pltpu.PrefetchScalarGridSpec(
        num_scalar_prefetch=2,
        grid=(n_blocks,),
        in_specs=[pl.BlockSpec((blk_rows, LANES), lambda b, bi, nu: (blk(b, nu), 0)),
                  pl.BlockSpec(memory_space=pl.ANY),
                  pl.BlockSpec((1, 1, 1, d_ff2), lambda b, bi, nu: (layer, bi[blk(b, nu)], 0, 0)),
                  pl.BlockSpec(memory_space=pl.ANY),
                  pl.BlockSpec((1, 1, 1, d), lambda b, bi, nu: (layer, bi[blk(b, nu)], 0, 0))],
        out_specs=pl.BlockSpec((blk_rows, LANES), lambda b, bi, nu: (b, 0)),
        scratch_shapes=[pltpu.VMEM((2, d, d_ff2), F32), pltpu.VMEM((2, d_ff, d), F32),
                        pltpu.VMEM((d, d_ff2), BF16), pltpu.VMEM((d_ff, d), BF16),
                        pltpu.SemaphoreType.DMA((2,))],
    )
    body = functools.partial(_ffn_kernel, layer=layer, n_experts=n_experts, n_blocks=n_blocks)
    return pl.pallas_call(
        body,
        grid_spec=grid_spec,
        out_shape=jax.ShapeDtypeStruct(xs.shape, F32),
        compiler_params=_params(("arbitrary",)),
        name="moe_ffn",
    )(block_info, n_used, xs, w_gu, b_gu.reshape(depth, n_experts, 1, d_ff2), w_dn,
      b_dn.reshape(depth, n_experts, 1, d))


def _combine_kernel(dest_ref, ys_ref, gate_ref, x_ref, m_ref, fg_ref, y_ref, buf, sem, *, final_norm):
    tm = x_ref.shape[0]

    def start_pair(p, carry):
        for u in range(2):
            j = 2 * p + u
            pltpu.make_async_copy(_row_tile(ys_ref, dest_ref[0, 0, j]), _row_tile(buf, j),
                                  sem.at[0]).start(priority=u)
        return carry

    lax.fori_loop(0, TOP_K * tm // 2, start_pair, 0, unroll=DMA_ISSUE_UNROLL // 2)
    pltpu.make_async_copy(ys_ref.at[pl.ds(0, TOP_K * tm * ROW_SUBLANES)], buf, sem.at[0]).wait()
    chunks = []
    for c in range(ROW_SUBLANES):
        y = gate_ref[:, 0:1] * _from_row_tiles(buf, 0, tm, c)
        for k in range(1, TOP_K):
            y = y + gate_ref[:, k:k + 1] * _from_row_tiles(buf, k * tm, tm, c)
        chunks.append(y)
    out = x_ref[...] + m_ref[0, 5:6, :] * jnp.concatenate(chunks, axis=-1)
    if final_norm:
        out = _rms(out, fg_ref[...])
    y_ref[...] = out


def _combine_call(dest_tiles, ys, gates_nk, x, modsel, final_g, final_norm):
    n, d = x.shape
    tm = TOKEN_TILE
    body = functools.partial(_combine_kernel, final_norm=final_norm)
    return pl.pallas_call(
        body,
        grid=(n // tm,),
        in_specs=[pl.BlockSpec((1, 1, TOP_K * tm), lambda i: (i, 0, 0), memory_space=pltpu.SMEM),
                  pl.BlockSpec(memory_space=pl.ANY),
                  pl.BlockSpec((tm, TOP_K), lambda i: (i, 0)),
                  pl.BlockSpec((tm, d), lambda i: (i, 0)),
                  pl.BlockSpec((1, 6, d), lambda i: (i, 0, 0)),
                  pl.BlockSpec((1, d), lambda i: (0, 0))],
        out_specs=pl.BlockSpec((tm, d), lambda i: (i, 0)),
        out_shape=jax.ShapeDtypeStruct((n, d), F32),
        scratch_shapes=[pltpu.VMEM((TOP_K * tm * ROW_SUBLANES, LANES), F32), pltpu.SemaphoreType.DMA((1,))],
        compiler_params=_params(("arbitrary",)),
        name="moe_combine",
    )(dest_tiles, ys, gates_nk, x, modsel, final_g.reshape(1, d))


def _moe_layer(layer, x, modsel, g2, w_router, b_router, w_gu, b_gu, w_dn, b_dn, final_g, final_norm):
    n, d = x.shape
    n_experts = w_router.shape[1]
    tm = TOKEN_TILE
    n_blocks = (n * TOP_K) // MOE_BLOCK + n_experts
    idx_t, gate_t, rank_t, counts = _route_call(x, modsel, g2, w_router, b_router)
    dest_t, block_info, pad_info, n_used = _slot_call(counts, idx_t, rank_t, n_blocks)
    dest_tiles = dest_t.reshape(TOP_K, n // tm, tm).transpose(1, 0, 2).reshape(n // tm, 1, TOP_K * tm)
    xs = _dispatch_call(pad_info, dest_tiles, x, modsel, g2, n_blocks * MOE_BLOCK)
    ys = _ffn_call(layer, block_info.reshape(3 * n_blocks), n_used.reshape(1), xs, w_gu, b_gu, w_dn, b_dn)
    return _combine_call(dest_tiles, ys, gate_t.T, x, modsel, final_g, final_norm)


def _rope_tables(t, hd):
    pos = np.arange(t)
    n_freq = hd // 4
    inv_freq = ROPE_THETA ** (-np.arange(n_freq, dtype=np.float32) / n_freq)
    ang = np.concatenate([(pos // GRID_W).astype(np.float32)[:, None] * inv_freq,
                          (pos % GRID_W).astype(np.float32)[:, None] * inv_freq], axis=-1)
    ang = jnp.asarray(ang, F32)
    cos, sin = jnp.cos(ang), jnp.sin(ang)
    return jnp.concatenate([cos, cos], axis=-1), jnp.concatenate([-sin, sin], axis=-1)


def kernel(x_prompt, x_sample, cache_k_a, cache_v_a, state_b, state_c_C, state_c_n, state_c_m, cache_k_d, cache_v_d, c, c_ctx, norm1_g, norm2_g, w_mod, b_mod, w_in_a, qnorm_a, knorm_a, w_out_a, w_in_b, b_f_b, lower_bounds_b, onorm_b, w_out_b, w_in_c, b_gates_c, onorm_c, w_out_c, w_in_d, rpb_d, w_out_d, w_router, b_router, w_gu, b_gu, w_dn, b_dn, final_g):
    n_ctx_seq, t_ctx, d = x_prompt.shape
    n_lat_seq, t_lat, _ = x_sample.shape
    depth = w_mod.shape[0]
    n_ctx = n_ctx_seq * t_ctx
    n_lat = n_lat_seq * t_lat
    n = n_ctx + n_lat
    tm = TOKEN_TILE
    assert t_ctx % tm == 0 and t_lat % tm == 0 and n_lat_seq + 1 <= 8

    lb_cum = jnp.cumsum(jax.nn.softmax(lower_bounds_b.astype(F32), axis=0), axis=0)
    lb_all = lb_cum - lb_cum[0]

    cond8 = jnp.zeros((8, d), F32).at[0].set(c_ctx).at[1:1 + n_lat_seq].set(c)
    mod = _mod_call(cond8, w_mod, b_mod)
    tile_row = np.concatenate([np.zeros(n_ctx // tm, np.int32),
                               1 + np.repeat(np.arange(n_lat_seq, dtype=np.int32), t_lat // tm)])

    x = jnp.concatenate([x_prompt.reshape(n_ctx, d), x_sample.reshape(n_lat, d)], axis=0)
    outs = {}
    for i in range(depth):
        kind = i % 4
        j = i // 4
        modsel = mod[i].reshape(8, 6, d)[tile_row]
        if kind == 0:
            proj = _inproj_call(x, modsel, norm1_g[i], w_in_a[j].astype(BF16))
            o_ctx, k_new = _gqa_ctx_call(proj, qnorm_a[j], knorm_a[j], n_ctx_seq, t_ctx)
            cosd, sind = _rope_tables(t_lat, 128)
            o_lat = _gqa_lat_call(proj, n_ctx, n_lat_seq, t_lat,
                                  cache_k_a[:, j].reshape(n_lat_seq, -1, 256),
                                  cache_v_a[:, j].reshape(n_lat_seq, -1, 256), cosd, sind,
                                  qnorm_a[j], knorm_a[j])
            outs["k_a"] = k_new.reshape(n_ctx_seq, 1, t_ctx, 2, 128)
            outs["v_a"] = proj[:n_ctx, 1280:1536].reshape(n_ctx_seq, 1, t_ctx, 2, 128)
            x = _outproj_call("plain", [(o_ctx, o_lat)], None, w_out_a[j].astype(BF16), x, modsel)
        elif kind == 1:
            proj = _inproj_call(x, modsel, norm1_g[i], w_in_b[j].astype(BF16))
            o_dirs, s_dirs = [], []
            for reverse in (False, True):
                oc, sc = _hgrn_call(proj, 0, n_ctx_seq, t_ctx, b_f_b[j], lb_all[i], None, reverse)
                ol, _ = _hgrn_call(proj, n_ctx, n_lat_seq, t_lat, b_f_b[j], lb_all[i], state_b[:, j], reverse)
                o_dirs.append((oc, ol))
                s_dirs.append(sc)
            outs["s_b"] = jnp.concatenate(s_dirs, axis=1)[:, None]
            x = _outproj_call("hgrn", o_dirs, (proj, 4, onorm_b[j].reshape(1, 128)),
                              w_out_b[j].astype(BF16), x, modsel)
        elif kind == 2:
            w_c = jnp.pad(w_in_c[j], ((0, 0), (0, 128 - 32))).astype(BF16)
            bg = jnp.pad(b_gates_c[j].reshape(1, 32), ((0, 0), (0, 128 - 32)))
            proj = _inproj_call(x, modsel, norm1_g[i], w_c)
            state = (state_c_C[:, j], state_c_n[:, j][:, :, :, None, :],
                     jnp.pad(state_c_m[:, j], ((0, 0), (0, 0), (0, 120)))[:, :, None, :])
            o_dirs, st = [], []
            for reverse in (False, True):
                oc, cc, nc, mc = _mlstm_call(proj, 0, n_ctx_seq, t_ctx, bg, None, reverse)
                ol, _, _, _ = _mlstm_call(proj, n_ctx, n_lat_seq, t_lat, bg, state, reverse)
                o_dirs.append((oc, ol))
                st.append((cc, nc, mc))
            outs["c_C"] = jnp.concatenate([st[0][0], st[1][0]], axis=1)[:, None]
            outs["c_n"] = jnp.concatenate([st[0][1], st[1][1]], axis=1)[:, None, :, :, 0, :]
            outs["c_m"] = jnp.concatenate([st[0][2], st[1][2]], axis=1)[:, None, :, 0, :8]
            x = _outproj_call("mlstm", o_dirs, (proj, 2, onorm_c[j].reshape(1, 128)),
                              w_out_c[j].astype(BF16), x, modsel)
        else:
            proj = _inproj_call(x, modsel, norm1_g[i], w_in_d[j].astype(BF16))
            o_ctx = _mha_ctx_call(proj, n_ctx_seq, t_ctx)
            tz = _na_bias_call(rpb_d[j])
            o_lat = _na_call(proj, n_ctx, n_lat_seq, t_lat,
                             cache_k_d[:, j].reshape(n_lat_seq, -1, d),
                             cache_v_d[:, j].reshape(n_lat_seq, -1, d), tz)
            outs["k_d"] = proj[:n_ctx, d:2 * d].reshape(n_ctx_seq, 1, t_ctx, 16, 64)
            outs["v_d"] = proj[:n_ctx, 2 * d:3 * d].reshape(n_ctx_seq, 1, t_ctx, 16, 64)
            x = _outproj_call("plain", [(o_ctx, o_lat)], None, w_out_d[j].astype(BF16), x, modsel)
        x = _moe_layer(i, x, modsel, norm2_g[i], w_router[i], b_router[i], w_gu, b_gu, w_dn, b_dn,
                       final_g, final_norm=(i == depth - 1))

    y_prompt = x[:n_ctx].reshape(n_ctx_seq, t_ctx, d)
    y_sample = x[n_ctx:].reshape(n_lat_seq, t_lat, d)
    return (y_prompt, y_sample, outs["k_a"], outs["v_a"], outs["s_b"], outs["c_C"], outs["c_n"], outs["c_m"],
            outs["k_d"], outs["v_d"])
```

```python
import functools

import numpy as np
import jax
import jax.numpy as jnp
from jax import lax
from jax.experimental import pallas as pl
from jax.experimental.pallas import tpu as pltpu

F32 = jnp.float32
BF16 = jnp.bfloat16
I32 = jnp.int32

NORM_EPS = 1e-6
GRID_W = 64
ROPE_THETA = 10000.0
TOP_K = 4
HGRN_CHUNK = 128
GATE_SOFTCAP = 15.0
NA_ROWS = 8
NA_COLS = 16
SWIGLU_ALPHA = 1.702
SWIGLU_LIMIT = 7.0
NEG_BIG = -1e30

LANES = 128
ROW_SUBLANES = 8
TOKEN_TILE = 256
MOE_BLOCK = 256
ATTN_HEAD_GROUP = 2
V7X_VMEM_LIMIT = 52 * 1024 * 1024


def _params(sem, vmem=V7X_VMEM_LIMIT):
    return pltpu.CompilerParams(dimension_semantics=sem, vmem_limit_bytes=vmem)


def _bdot(a, b):
    return jnp.dot(a.astype(BF16), b.astype(BF16), preferred_element_type=F32)


def _bdot_nt(a, b):
    return lax.dot_general(a.astype(BF16), b.astype(BF16), (((1,), (1,)), ((), ())),
                           preferred_element_type=F32)


def _bdot_tn(a, b):
    return lax.dot_general(a.astype(BF16), b.astype(BF16), (((0,), (0,)), ((), ())),
                           preferred_element_type=F32)


def _fdot(a, b):
    return jnp.dot(a, b, preferred_element_type=F32, precision=lax.Precision.HIGHEST)


def _rms(x, g):
    return x * lax.rsqrt(jnp.mean(x * x, axis=-1, keepdims=True) + NORM_EPS) * g


def _rms_heads(x, g, n_heads, hd):
    return jnp.concatenate([_rms(x[:, h * hd:(h + 1) * hd], g) for h in range(n_heads)], axis=-1)


def _sigmoid(x):
    return 1.0 / (1.0 + jnp.exp(-x))


def _silu(x):
    return x * _sigmoid(x)


def _softmax_rows(parts):
    m = parts[0].max(axis=-1, keepdims=True)
    for p in parts[1:]:
        m = jnp.maximum(m, p.max(axis=-1, keepdims=True))
    es = [jnp.exp(p - m) for p in parts]
    den = es[0].sum(axis=-1, keepdims=True)
    for e in es[1:]:
        den = den + e.sum(axis=-1, keepdims=True)
    return es, den


def _mod_kernel(c_ref, w_ref, b_ref, o_ref):
    o_ref[0] = _bdot(_silu(c_ref[...]), w_ref[0]) + b_ref[0]


def _mod_call(cond8, w_mod, b_mod):
    depth, d, d6 = w_mod.shape
    tn = 1024
    return pl.pallas_call(
        _mod_kernel,
        grid=(depth, d6 // tn),
        in_specs=[pl.BlockSpec((8, d), lambda i, j: (0, 0)),
                  pl.BlockSpec((1, d, tn), lambda i, j: (i, 0, j)),
                  pl.BlockSpec((1, 1, tn), lambda i, j: (i, 0, j))],
        out_specs=pl.BlockSpec((1, 8, tn), lambda i, j: (i, 0, j)),
        out_shape=jax.ShapeDtypeStruct((depth, 8, d6), F32),
        compiler_params=_params(("arbitrary", "arbitrary")),
        name="adaln_mod",
    )(cond8, w_mod, b_mod.reshape(depth, 1, d6))


def _inproj_kernel(x_ref, m_ref, g_ref, w_ref, o_ref):
    h = _rms(x_ref[...], g_ref[...]) * (1.0 + m_ref[0, 1:2, :]) + m_ref[0, 0:1, :]
    o_ref[...] = _bdot(h, w_ref[...])


def _inproj_call(x, modsel, g, w):
    n, d = x.shape
    wout = w.shape[1]
    tm = TOKEN_TILE
    return pl.pallas_call(
        _inproj_kernel,
        grid=(n // tm,),
        in_specs=[pl.BlockSpec((tm, d), lambda i: (i, 0)),
                  pl.BlockSpec((1, 6, d), lambda i: (i, 0, 0)),
                  pl.BlockSpec((1, d), lambda i: (0, 0)),
                  pl.BlockSpec((d, wout), lambda i: (0, 0))],
        out_specs=pl.BlockSpec((tm, wout), lambda i: (i, 0)),
        out_shape=jax.ShapeDtypeStruct((n, wout), F32),
        compiler_params=_params(("arbitrary",)),
        name="inproj",
    )(x, modsel, g.reshape(1, d), w)


def _outproj_kernel(*refs, kind, n_pairs, n_ctx_tiles, n_heads, hd):
    in_ctx = pl.program_id(0) < n_ctx_tiles
    mix = [jnp.where(in_ctx, refs[2 * p][...], refs[2 * p + 1][...]) for p in range(n_pairs)]
    rest = refs[2 * n_pairs:]
    if kind == "plain":
        w_ref, x_ref, m_ref, y_ref = rest
        o = mix[0]
    else:
        gate_ref, on_ref, w_ref, x_ref, m_ref, y_ref = rest
        o = _rms_heads(mix[0] + mix[1], on_ref[...], n_heads, hd)
        o = o * _silu(gate_ref[...]) if kind == "hgrn" else _sigmoid(gate_ref[...]) * o
    y_ref[...] = x_ref[...] + m_ref[0, 2:3, :] * _bdot(o, w_ref[...])


def _outproj_call(kind, pairs, gate, w, x, modsel):
    n, d = x.shape
    tm = TOKEN_TILE
    n_ctx_tiles = pairs[0][0].shape[0] // tm
    specs, args = [], []
    for a_ctx, a_lat in pairs:
        specs += [pl.BlockSpec((tm, d), lambda i: (jnp.minimum(i, n_ctx_tiles - 1), 0)),
                  pl.BlockSpec((tm, d), lambda i: (jnp.maximum(i - n_ctx_tiles, 0), 0))]
        args += [a_ctx, a_lat]
    if gate is not None:
        proj, cb, on = gate
        specs += [pl.BlockSpec((tm, d), lambda i: (i, cb)), pl.BlockSpec(on.shape, lambda i: (0, 0))]
        args += [proj, on]
    specs += [pl.BlockSpec(w.shape, lambda i: (0, 0)),
              pl.BlockSpec((tm, d), lambda i: (i, 0)),
              pl.BlockSpec((1, 6, d), lambda i: (i, 0, 0))]
    args += [w, x, modsel]
    body = functools.partial(_outproj_kernel, kind=kind, n_pairs=len(pairs), n_ctx_tiles=n_ctx_tiles,
                             n_heads=8, hd=128)
    return pl.pallas_call(
        body,
        grid=(n // tm,),
        in_specs=specs,
        out_specs=pl.BlockSpec((tm, d), lambda i: (i, 0)),
        out_shape=jax.ShapeDtypeStruct((n, d), F32),
        compiler_params=_params(("arbitrary",)),
        name="outproj_" + kind,
    )(*args)


def _gqa_ctx_kernel(p_ref, qn_ref, kn_ref, o_ref, k_ref, *, n_heads, n_kv, hd):
    rep = n_heads // n_kv
    scale = hd ** -0.5
    koff = n_heads * hd
    voff = koff + n_kv * hd
    ks = [_rms(p_ref[:, koff + g * hd: koff + (g + 1) * hd], kn_ref[...]) for g in range(n_kv)]
    k_ref[...] = jnp.concatenate(ks, axis=-1)
    for h in range(n_heads):
        g = h // rep
        q = _rms(p_ref[:, h * hd:(h + 1) * hd], qn_ref[...])
        s = _bdot_nt(q, ks[g]) * scale
        (e,), den = _softmax_rows([s])
        o_ref[:, h * hd:(h + 1) * hd] = _bdot(e / den, p_ref[:, voff + g * hd: voff + (g + 1) * hd])


def _gqa_ctx_call(proj, qn, kn, n_seq, t):
    n_heads, n_kv, hd = 8, 2, 128
    win = proj.shape[1]
    body = functools.partial(_gqa_ctx_kernel, n_heads=n_heads, n_kv=n_kv, hd=hd)
    return pl.pallas_call(
        body,
        grid=(n_seq,),
        in_specs=[pl.BlockSpec((t, win), lambda b: (b, 0)),
                  pl.BlockSpec((1, hd), lambda b: (0, 0)),
                  pl.BlockSpec((1, hd), lambda b: (0, 0))],
        out_specs=[pl.BlockSpec((t, n_heads * hd), lambda b: (b, 0)),
                   pl.BlockSpec((t, n_kv * hd), lambda b: (b, 0))],
        out_shape=[jax.ShapeDtypeStruct((n_seq * t, n_heads * hd), F32),
                   jax.ShapeDtypeStruct((n_seq * t, n_kv * hd), F32)],
        compiler_params=_params(("arbitrary",)),
        name="gqa_ctx",
    )(proj, qn.reshape(1, hd), kn.reshape(1, hd))


def _rope(x, cosd, sind):
    return x * cosd + pltpu.roll(x, x.shape[-1] // 2, 1) * sind


def _gqa_lat_kernel(pq_ref, pkv_ref, kc_ref, vc_ref, cq_ref, sq_ref, ck_ref, sk_ref, qn_ref, kn_ref,
                    o_ref, k_scr, v_scr, *, n_heads, n_kv, hd, t_ctx):
    rep = n_heads // n_kv
    scale = hd ** -0.5

    @pl.when(pl.program_id(1) == 0)
    def _():
        k_scr[0:t_ctx, :] = kc_ref[0].astype(BF16)
        v_scr[...] = jnp.ones_like(v_scr)
        for g in range(n_kv):
            k = _rms(pkv_ref[:, g * hd:(g + 1) * hd], kn_ref[...])
            k_scr[t_ctx:, g * hd:(g + 1) * hd] = _rope(k, ck_ref[...], sk_ref[...]).astype(BF16)
            v_scr[0:t_ctx, 2 * g * hd:(2 * g + 1) * hd] = vc_ref[0, :, g * hd:(g + 1) * hd].astype(BF16)
            v_scr[t_ctx:, 2 * g * hd:(2 * g + 1) * hd] = pkv_ref[:, (n_kv + g) * hd:(n_kv + g + 1) * hd].astype(BF16)

    qs = [(_rope(_rms(pq_ref[:, h * hd:(h + 1) * hd], qn_ref[...]), cq_ref[...], sq_ref[...]) * scale).astype(BF16)
          for h in range(n_heads)]
    for h0 in range(0, n_heads, ATTN_HEAD_GROUP):
        hs = range(h0, min(h0 + ATTN_HEAD_GROUP, n_heads))
        s = [_bdot_nt(qs[h], k_scr[:, (h // rep) * hd:(h // rep + 1) * hd]) for h in hs]
        e = [jnp.exp(x - x.max(axis=-1, keepdims=True)).astype(BF16) for x in s]
        pv = [jnp.dot(e[i], v_scr[:, 2 * (h // rep) * hd:2 * (h // rep + 1) * hd], preferred_element_type=F32)
              for i, h in enumerate(hs)]
        for i, h in enumerate(hs):
            o_ref[:, h * hd:(h + 1) * hd] = pv[i][:, :hd] / pv[i][:, hd:]


def _gqa_lat_call(proj, row0, n_seq, t, cache_k, cache_v, cosd, sind, qn, kn):
    n_heads, n_kv, hd = 8, 2, 128
    tq = 256
    t_ctx = cache_k.shape[1]
    nq = t // tq
    kvw = 2 * n_kv * hd
    qblk0 = row0 // tq
    sblk0 = row0 // t
    body = functools.partial(_gqa_lat_kernel, n_heads=n_heads, n_kv=n_kv, hd=hd, t_ctx=t_ctx)
    return pl.pallas_call(
        body,
        grid=(n_seq, nq),
        in_specs=[pl.BlockSpec((tq, n_heads * hd), lambda b, i: (qblk0 + b * nq + i, 0)),
                  pl.BlockSpec((t, kvw), lambda b, i: (sblk0 + b, (n_heads * hd) // kvw)),
                  pl.BlockSpec((1, t_ctx, n_kv * hd), lambda b, i: (b, 0, 0)),
                  pl.BlockSpec((1, t_ctx, n_kv * hd), lambda b, i: (b, 0, 0)),
                  pl.BlockSpec((tq, hd), lambda b, i: (i, 0)),
                  pl.BlockSpec((tq, hd), lambda b, i: (i, 0)),
                  pl.BlockSpec((t, hd), lambda b, i: (0, 0)),
                  pl.BlockSpec((t, hd), lambda b, i: (0, 0)),
                  pl.BlockSpec((1, hd), lambda b, i: (0, 0)),
                  pl.BlockSpec((1, hd), lambda b, i: (0, 0))],
        out_specs=pl.BlockSpec((tq, n_heads * hd), lambda b, i: (b * nq + i, 0)),
        out_shape=jax.ShapeDtypeStruct((n_seq * t, n_heads * hd), F32),
        scratch_shapes=[pltpu.VMEM((t_ctx + t, n_kv * hd), BF16),
                        pltpu.VMEM((t_ctx + t, 2 * n_kv * hd), BF16)],
        compiler_params=_params(("arbitrary", "arbitrary")),
        name="gqa_latent",
    )(proj, proj, cache_k, cache_v, cosd, sind, cosd, sind, qn.reshape(1, hd), kn.reshape(1, hd))


def _mha_ctx_kernel(q_ref, k_ref, v_ref, o_ref, *, hd):
    scale = hd ** -0.5
    n_heads = q_ref.shape[1] // hd
    ones = jnp.ones((v_ref.shape[0], hd), BF16)
    for h0 in range(0, n_heads, ATTN_HEAD_GROUP):
        sls = [slice(h * hd, (h + 1) * hd) for h in range(h0, min(h0 + ATTN_HEAD_GROUP, n_heads))]
        s = [_bdot_nt(q_ref[:, sl] * scale, k_ref[:, sl]) for sl in sls]
        e = [jnp.exp(x - x.max(axis=-1, keepdims=True)).astype(BF16) for x in s]
        pv = [jnp.dot(e[i], jnp.concatenate([v_ref[:, sl].astype(BF16), ones], axis=-1),
                      preferred_element_type=F32) for i, sl in enumerate(sls)]
        for i, sl in enumerate(sls):
            o_ref[:, sl] = pv[i][:, :hd] / pv[i][:, hd:]


def _mha_ctx_call(proj, n_seq, t):
    hd = 64
    d = proj.shape[1] // 3
    body = functools.partial(_mha_ctx_kernel, hd=hd)
    return pl.pallas_call(
        body,
        grid=(n_seq,),
        in_specs=[pl.BlockSpec((t, d), lambda b: (b, 0)),
                  pl.BlockSpec((t, d), lambda b: (b, 1)),
                  pl.BlockSpec((t, d), lambda b: (b, 2))],
        out_specs=pl.BlockSpec((t, d), lambda b: (b, 0)),
        out_shape=jax.ShapeDtypeStruct((n_seq * t, d), F32),
        compiler_params=_params(("arbitrary",)),
        name="mha_ctx",
    )(proj, proj, proj)


def _na_bias_kernel(rpb_ref, o_ref, *, n_rel_rows, n_rel_cols):
    h = pl.program_id(0)
    w_io = lax.broadcasted_iota(I32, (GRID_W, 2 * GRID_W), 0)
    lane = lax.broadcasted_iota(I32, (GRID_W, 2 * GRID_W), 1)
    ck = jnp.where(lane < GRID_W, lane, lane - GRID_W)
    c_start = jnp.clip(w_io - NA_COLS // 2, 0, GRID_W - NA_COLS)
    in_win = (ck >= c_start) & (ck < c_start + NA_COLS)
    rel = ck - w_io + (NA_COLS - 1)
    base = h * (n_rel_rows * n_rel_cols)
    tiles = []
    for j in range(n_rel_rows):
        acc = jnp.zeros((GRID_W, 2 * GRID_W), F32)
        for jj in range(n_rel_cols):
            acc = jnp.where(rel == jj, rpb_ref[base + j * n_rel_cols + jj], acc)
        tiles.append(jnp.where(in_win, acc, NEG_BIG))
    for j in range(n_rel_rows):
        hi = tiles[j + 1] if j + 1 < n_rel_rows else jnp.full((GRID_W, 2 * GRID_W), NEG_BIG, F32)
        o_ref[0, j] = jnp.where(lane < GRID_W, tiles[j], hi)


def _na_bias_call(rpb):
    n_heads, nrr, nrc = rpb.shape
    body = functools.partial(_na_bias_kernel, n_rel_rows=nrr, n_rel_cols=nrc)
    return pl.pallas_call(
        body,
        grid=(n_heads,),
        in_specs=[pl.BlockSpec(memory_space=pltpu.SMEM)],
        out_specs=pl.BlockSpec((1, nrr, GRID_W, 2 * GRID_W), lambda h: (h, 0, 0, 0)),
        out_shape=jax.ShapeDtypeStruct((n_heads, nrr, GRID_W, 2 * GRID_W), F32),
        compiler_params=_params(("arbitrary",)),
        name="na_bias",
    )(rpb.reshape(-1))


NA_QROWS = 4
NA_KROWS = 12


def _na_kernel(q_ref, k0_ref, k1_ref, k2_ref, v0_ref, v1_ref, v2_ref, kc_ref, vc_ref, tz_ref, o_ref,
               *, hd, n_grid_rows):
    scale = hd ** -0.5
    blk = pl.program_id(1)
    kstart = jnp.clip(blk * NA_QROWS - NA_ROWS // 2, 0, n_grid_rows - NA_KROWS)
    lane = lax.broadcasted_iota(I32, (GRID_W, 2 * GRID_W), 1)
    n_rel = tz_ref.shape[1]
    heads = range(q_ref.shape[1] // hd)
    sls = [slice(hh * hd, (hh + 1) * hd) for hh in heads]
    rel, pen = [], []
    for rq_l in range(NA_QROWS):
        rq = blk * NA_QROWS + rq_l
        r_start = jnp.clip(rq - NA_ROWS // 2, 0, n_grid_rows - NA_ROWS)
        rel.append([])
        pen.append([])
        for m in range(NA_KROWS // 2):
            rk = kstart + 2 * m
            rel[-1].append(jnp.clip(rk - rq + (NA_ROWS - 1), 0, n_rel - 1))
            ok0 = (rk >= r_start) & (rk < r_start + NA_ROWS)
            ok1 = (rk + 1 >= r_start) & (rk + 1 < r_start + NA_ROWS)
            pen[-1].append(jnp.where(lane < GRID_W, jnp.where(ok0, 0.0, NEG_BIG), jnp.where(ok1, 0.0, NEG_BIG)))
    bias = [jnp.concatenate([jnp.concatenate([tz_ref[hh, pl.ds(rel[r][m], 1)][0] + pen[r][m]
                                              for m in range(NA_KROWS // 2)], axis=-1)
                             for r in range(NA_QROWS)], axis=0) for hh in heads]
    q = [q_ref[:, sl] * scale for sl in sls]
    k_loc = [jnp.concatenate([k0_ref[:, sl], k1_ref[:, sl], k2_ref[:, sl]], axis=0) for sl in sls]
    v_loc = [jnp.concatenate([v0_ref[:, sl], v1_ref[:, sl], v2_ref[:, sl]], axis=0).astype(BF16) for sl in sls]
    v_ctx = [vc_ref[0, :, sl].astype(BF16) for sl in sls]
    s_loc = [_bdot_nt(q[h], k_loc[h]) + bias[h] for h in heads]
    s_ctx = [_bdot_nt(q[h], kc_ref[0, :, sls[h]]) for h in heads]
    m = [jnp.maximum(s_loc[h].max(axis=-1, keepdims=True), s_ctx[h].max(axis=-1, keepdims=True)) for h in heads]
    e_loc = [jnp.exp(s_loc[h] - m[h]).astype(BF16) for h in heads]
    e_ctx = [jnp.exp(s_ctx[h] - m[h]).astype(BF16) for h in heads]
    pv = [jnp.dot(e_loc[h], jnp.concatenate([v_loc[h], jnp.ones_like(v_loc[h])], axis=-1),
                  preferred_element_type=F32)
          + jnp.dot(e_ctx[h], jnp.concatenate([v_ctx[h], jnp.ones_like(v_ctx[h])], axis=-1),
                    preferred_element_type=F32) for h in heads]
    for h in heads:
        o_ref[:, sls[h]] = pv[h][:, :hd] / pv[h][:, hd:]


def _na_call(proj, row0, n_seq, t, cache_k, cache_v, tz):
    hd, cw = 64, 128
    d = proj.shape[1] // 3
    ncb = d // cw
    tq = NA_QROWS * GRID_W
    nq = t // tq
    n_grid_rows = t // GRID_W
    t_ctx = cache_k.shape[1]
    qblk0 = row0 // tq

    def kv_map(which, j):
        def index_map(b, i, c):
            ks = jnp.clip(i * NA_QROWS - NA_ROWS // 2, 0, n_grid_rows - NA_KROWS) // NA_QROWS
            return (qblk0 + b * nq + ks + j, which * ncb + c)
        return index_map

    body = functools.partial(_na_kernel, hd=hd, n_grid_rows=n_grid_rows)
    return pl.pallas_call(
        body,
        grid=(n_seq, nq, ncb),
        in_specs=[pl.BlockSpec((tq, cw), lambda b, i, c: (qblk0 + b * nq + i, c))]
                 + [pl.BlockSpec((tq, cw), kv_map(1, j)) for j in range(3)]
                 + [pl.BlockSpec((tq, cw), kv_map(2, j)) for j in range(3)]
                 + [pl.BlockSpec((1, t_ctx, cw), lambda b, i, c: (b, 0, c)),
                    pl.BlockSpec((1, t_ctx, cw), lambda b, i, c: (b, 0, c)),
                    pl.BlockSpec((cw // hd,) + tz.shape[1:], lambda b, i, c: (c, 0, 0, 0))],
        out_specs=pl.BlockSpec((tq, cw), lambda b, i, c: (b * nq + i, c)),
        out_shape=jax.ShapeDtypeStruct((n_seq * t, d), F32),
        compiler_params=_params(("arbitrary", "arbitrary", "arbitrary")),
        name="nbr_attn",
    )(proj, proj, proj, proj, proj, proj, proj, cache_k, cache_v, tz)


def _hgrn_kernel(q_ref, v_ref, f_ref, bf_ref, lb_ref, s0_ref, o_ref, s_ref, st_scr,
                 *, reverse, n_heads, dk, has_s0, n_blk):
    c = pl.program_id(1)
    L = HGRN_CHUNK
    tb = q_ref.shape[0]

    @pl.when(c == 0)
    def _():
        for h in range(n_heads):
            if has_s0:
                st_scr[h] = s0_ref[0, 0, h].T
            else:
                st_scr[h] = jnp.zeros_like(st_scr[h])

    row = lax.broadcasted_iota(I32, (L, L), 0)
    col = lax.broadcasted_iota(I32, (L, L), 1)
    tri = jnp.where((col >= row) if reverse else (col <= row), 1.0, 0.0).astype(F32)
    eye = row == col
    halves = [L >> (i + 1) for i in range(L.bit_length() - 1)]
    same_pair = [(row // (2 * hf)) == (col // (2 * hf)) for hf in halves]
    rio = lax.broadcasted_iota(I32, (L, 1), 0)
    r8 = lax.broadcasted_iota(I32, (ROW_SUBLANES, 1), 0)
    lb = lb_ref[0]
    bf = bf_ref[0]
    n_chunks = tb // L

    def boundary_rows(cum, hf):
        blk = 2 * hf
        off = hf if reverse else hf - 1
        width = cum.shape[1]
        if blk >= ROW_SUBLANES:
            return jnp.concatenate([jnp.broadcast_to(cum[a + off:a + off + 1, :], (blk, width))
                                    for a in range(0, L, blk)], axis=0)
        groups = []
        for g in range(0, L, ROW_SUBLANES):
            ref = jnp.broadcast_to(cum[g + off:g + off + 1, :], (ROW_SUBLANES, width))
            for a in range(blk, ROW_SUBLANES, blk):
                ref = jnp.where(r8 >= a, jnp.broadcast_to(cum[g + a + off:g + a + off + 1, :],
                                                          (ROW_SUBLANES, width)), ref)
            groups.append(ref)
        return jnp.concatenate(groups, axis=0)

    def chunk_step(jj, carry):
        jc = (n_chunks - 1 - jj) if reverse else jj
        rs = pl.ds(pl.multiple_of(jc * L, L), L)
        q = _silu(q_ref[rs, :]) * (dk ** -0.5)
        v = v_ref[rs, :]
        f = lb + (1.0 - lb) * _sigmoid(f_ref[rs, :] + bf)
        logf = jnp.log(f)
        kk = 1.0 - f
        cum = _fdot(tri, logf)
        end = cum[0:1, :] if reverse else cum[L - 1:L, :]
        qd = q * jnp.exp(cum)
        kd = kk * jnp.exp(end - cum)
        e_end = jnp.exp(end)
        q_lv, k_lv = [], []
        for hf in halves:
            ref = boundary_rows(cum, hf)
            upper = (rio & hf) != 0
            is_query = jnp.logical_not(upper) if reverse else upper
            decay = jnp.exp(jnp.where(is_query, cum - ref, ref - cum))
            q_lv.append(jnp.where(is_query, q * decay, 0.0).astype(BF16))
            k_lv.append(jnp.where(is_query, 0.0, kk * decay).astype(BF16))
        qk_diag = q * kk
        outs = []
        for h in range(n_heads):
            hs = slice(h * dk, (h + 1) * dk)
            attn = jnp.where(eye, jnp.sum(qk_diag[:, hs], axis=-1, keepdims=True), 0.0)
            for lv in range(len(halves)):
                attn = attn + jnp.where(same_pair[lv], _bdot_nt(q_lv[lv][:, hs], k_lv[lv][:, hs]), 0.0)
            outs.append(_bdot_nt(qd[:, hs], st_scr[h]) + _bdot(attn, v[:, hs]))
        o_ref[rs, :] = jnp.concatenate(outs, axis=-1)
        for h in range(n_heads):
            hs = slice(h * dk, (h + 1) * dk)
            st_scr[h] = st_scr[h] * e_end[:, hs] + _bdot_tn(v[:, hs], kd[:, hs])
        return carry

    lax.fori_loop(0, n_chunks, chunk_step, 0)

    @pl.when(c == n_blk - 1)
    def _():
        for h in range(n_heads):
            s_ref[0, 0, h] = st_scr[h].T


def _hgrn_call(proj, row0, n_seq, t, b_f, lb, s0, reverse):
    n_heads, dk = 8, 128
    d = n_heads * dk
    tb = 256
    n_blk = t // tb
    blk0 = row0 // tb
    di = 1 if reverse else 0
    has_s0 = s0 is not None
    if not has_s0:
        s0 = jnp.zeros((1, 2, n_heads, dk, dk), F32)

    def tok(b, c):
        return blk0 + b * n_blk + ((n_blk - 1 - c) if reverse else c)

    body = functools.partial(_hgrn_kernel, reverse=reverse, n_heads=n_heads, dk=dk,
                             has_s0=has_s0, n_blk=n_blk)
    return pl.pallas_call(
        body,
        grid=(n_seq, n_blk),
        in_specs=[pl.BlockSpec((tb, d), lambda b, c: (tok(b, c), 0)),
                  pl.BlockSpec((tb, d), lambda b, c: (tok(b, c), 1)),
                  pl.BlockSpec((tb, d), lambda b, c: (tok(b, c), 2 + di)),
                  pl.BlockSpec((1, 1, d), lambda b, c: (di, 0, 0)),
                  pl.BlockSpec((1, 1, d), lambda b, c: (di, 0, 0)),
                  pl.BlockSpec((1, 1, n_heads, dk, dk),
                               (lambda b, c: (b, di, 0, 0, 0)) if has_s0 else (lambda b, c: (0, 0, 0, 0, 0)))],
        out_specs=[pl.BlockSpec((tb, d), lambda b, c: (tok(b, c) - blk0, 0)),
                   pl.BlockSpec((1, 1, n_heads, dk, dk), lambda b, c: (b, 0, 0, 0, 0))],
        out_shape=[jax.ShapeDtypeStruct((n_seq * t, d), F32),
                   jax.ShapeDtypeStruct((n_seq, 1, n_heads, dk, dk), F32)],
        scratch_shapes=[pltpu.VMEM((n_heads, dk, dk), F32)],
        compiler_params=_params(("arbitrary", "arbitrary")),
        name="hgrn2_bw" if reverse else "hgrn2_fw",
    )(proj, proj, proj, b_f.reshape(2, 1, d), lb.reshape(2, 1, d), s0)


def _log_sigmoid(x):
    return jnp.minimum(x, 0.0) - jnp.log(1.0 + jnp.exp(-jnp.abs(x)))


def _mlstm_kernel(q_ref, k_ref, v_ref, g_ref, bg_ref, c0_ref, n0_ref, m0_ref,
                  o_ref, c_out, n_out, m_out, c_scr, n_scr, m_scr,
                  *, reverse, n_heads, dqk, dv, has_state, n_blk):
    c = pl.program_id(1)
    L = q_ref.shape[0]
    i_off = 2 * n_heads if reverse else 0
    f_off = i_off + n_heads

    @pl.when(c == 0)
    def _():
        if has_state:
            c_scr[...] = c0_ref[0, 0]
            n_scr[...] = n0_ref[0, 0]
            m_scr[...] = m0_ref[0, 0]
        else:
            c_scr[...] = jnp.zeros_like(c_scr)
            n_scr[...] = jnp.zeros_like(n_scr)
            m_scr[...] = jnp.zeros_like(m_scr)

    gates = GATE_SOFTCAP * jnp.tanh((g_ref[...] + bg_ref[...]) / GATE_SOFTCAP)
    logf = _log_sigmoid(gates)
    row = lax.broadcasted_iota(I32, (L, L), 0)
    col = lax.broadcasted_iota(I32, (L, L), 1)
    causal = (col >= row) if reverse else (col <= row)
    tri = jnp.where(causal, 1.0, 0.0).astype(F32)
    cum = _fdot(tri, logf)
    cum_t = cum.T
    gates_t = gates.T
    e_row = 0 if reverse else L - 1
    m_all = m_scr[...]
    c_prev = [c_scr[h] for h in range(n_heads)]
    n_prev = [n_scr[h] for h in range(n_heads)]
    lane = lax.broadcasted_iota(I32, m_all.shape, 1)
    heads = range(n_heads)
    qs = [q_ref[:, h * dqk:(h + 1) * dqk] * (dqk ** -0.5) for h in heads]
    ks_ = [k_ref[:, h * dqk:(h + 1) * dqk] for h in heads]
    vs = [v_ref[:, h * dv:(h + 1) * dv] for h in heads]
    cum_c = [cum[:, f_off + h:f_off + h + 1] for h in heads]
    cum_r = [cum_t[f_off + h:f_off + h + 1, :] for h in heads]
    i_c = [gates[:, i_off + h:i_off + h + 1] for h in heads]
    i_r = [gates_t[i_off + h:i_off + h + 1, :] for h in heads]
    m_prev = [m_all[0:1, h:h + 1] for h in heads]
    d = [jnp.where(causal, cum_c[h] - cum_r[h] + i_r[h], -jnp.inf) for h in heads]
    m_inter = [cum_c[h] + m_prev[h] for h in heads]
    m_t = [jnp.maximum(m_inter[h], d[h].max(axis=-1, keepdims=True)) for h in heads]
    scores = [_bdot_nt(qs[h], ks_[h]) for h in heads]
    inter = [_bdot(qs[h], c_prev[h]) for h in heads]
    qn = [jnp.sum(qs[h] * n_prev[h], axis=-1, keepdims=True) for h in heads]
    w_inter = [jnp.exp(m_inter[h] - m_t[h]) for h in heads]
    qk = [scores[h] * jnp.exp(d[h] - m_t[h]) for h in heads]
    num = [w_inter[h] * inter[h] + _bdot(qk[h], vs[h]) for h in heads]
    den = [w_inter[h] * qn[h] + qk[h].sum(axis=-1, keepdims=True) for h in heads]
    outs = [num[h] / jnp.maximum(jnp.abs(den[h]), jnp.exp(-m_t[h])) for h in heads]
    end = [cum_c[h][e_row:e_row + 1, :] for h in heads]
    g_end_r = [end[h] - cum_r[h] + i_r[h] for h in heads]
    g_end_c = [end[h] - cum_c[h] + i_c[h] for h in heads]
    m_new = [jnp.maximum(end[h] + m_prev[h], g_end_r[h].max(axis=-1, keepdims=True)) for h in heads]
    w_old = [jnp.exp(end[h] + m_prev[h] - m_new[h]) for h in heads]
    kd = [ks_[h] * jnp.exp(g_end_c[h] - m_new[h]) for h in heads]
    c_new = [w_old[h] * c_prev[h] + _bdot_tn(kd[h], vs[h]) for h in heads]
    n_new = [w_old[h] * n_prev[h] + kd[h].sum(axis=0, keepdims=True) for h in heads]
    m_next = m_all
    for h in heads:
        m_next = jnp.where(lane == h, m_new[h], m_next)
    o_ref[...] = jnp.concatenate(outs, axis=-1)
    for h in range(n_heads):
        c_scr[h] = c_new[h]
        n_scr[h] = n_new[h]
    m_scr[...] = m_next

    @pl.when(c == n_blk - 1)
    def _():
        c_out[0, 0] = c_scr[...]
        n_out[0, 0] = n_scr[...]
        m_out[0, 0] = m_scr[...]


def _mlstm_call(proj, row0, n_seq, t, b_gates_pad, state, reverse):
    n_heads, dqk, dv = 8, 64, 128
    wq, wv = n_heads * dqk, n_heads * dv
    L = 256
    n_blk = t // L
    blk0 = row0 // L
    di = 1 if reverse else 0
    has_state = state is not None
    if has_state:
        c0, n0, m0 = state
        smap = lambda b, c: (b, di, 0, 0, 0)
        mmap = lambda b, c: (b, di, 0, 0)
    else:
        c0 = jnp.zeros((1, 1, n_heads, dqk, dv), F32)
        n0 = jnp.zeros((1, 1, n_heads, 1, dqk), F32)
        m0 = jnp.zeros((1, 1, 1, 128), F32)
        smap = lambda b, c: (0, 0, 0, 0, 0)
        mmap = lambda b, c: (0, 0, 0, 0)

    def tok(b, c):
        return blk0 + b * n_blk + ((n_blk - 1 - c) if reverse else c)

    body = functools.partial(_mlstm_kernel, reverse=reverse, n_heads=n_heads, dqk=dqk, dv=dv,
                             has_state=has_state, n_blk=n_blk)
    gate_cb = (2 * wq + 2 * wv) // 128
    return pl.pallas_call(
        body,
        grid=(n_seq, n_blk),
        in_specs=[pl.BlockSpec((L, wq), lambda b, c: (tok(b, c), 0)),
                  pl.BlockSpec((L, wq), lambda b, c: (tok(b, c), 1)),
                  pl.BlockSpec((L, wv), lambda b, c: (tok(b, c), (2 * wq) // wv)),
                  pl.BlockSpec((L, 128), lambda b, c: (tok(b, c), gate_cb)),
                  pl.BlockSpec((1, 128), lambda b, c: (0, 0)),
                  pl.BlockSpec((1, 1, n_heads, dqk, dv), smap),
                  pl.BlockSpec((1, 1, n_heads, 1, dqk), smap),
                  pl.BlockSpec((1, 1, 1, 128), mmap)],
        out_specs=[pl.BlockSpec((L, wv), lambda b, c: (tok(b, c) - blk0, 0)),
                   pl.BlockSpec((1, 1, n_heads, dqk, dv), lambda b, c: (b, 0, 0, 0, 0)),
                   pl.BlockSpec((1, 1, n_heads, 1, dqk), lambda b, c: (b, 0, 0, 0, 0)),
                   pl.BlockSpec((1, 1, 1, 128), lambda b, c: (b, 0, 0, 0))],
        out_shape=[jax.ShapeDtypeStruct((n_seq * t, wv), F32),
                   jax.ShapeDtypeStruct((n_seq, 1, n_heads, dqk, dv), F32),
                   jax.ShapeDtypeStruct((n_seq, 1, n_heads, 1, dqk), F32),
                   jax.ShapeDtypeStruct((n_seq, 1, 1, 128), F32)],
        scratch_shapes=[pltpu.VMEM((n_heads, dqk, dv), F32),
                        pltpu.VMEM((n_heads, 1, dqk), F32),
                        pltpu.VMEM((1, 128), F32)],
        compiler_params=_params(("arbitrary", "arbitrary")),
        name="mlstm_bw" if reverse else "mlstm_fw",
    )(proj, proj, proj, proj, b_gates_pad, c0, n0, m0)


def _moe_input(x_ref, m_ref, g_ref):
    return _rms(x_ref[...], g_ref[...]) * (1.0 + m_ref[0, 4:5, :]) + m_ref[0, 3:4, :]


def _route_kernel(x_ref, m_ref, g_ref, wr_ref, br_ref, idx_ref, gate_ref, rank_ref, cnt_ref, carry_scr,
                  *, n_experts):
    i = pl.program_id(0)
    tm = x_ref.shape[0]

    @pl.when(i == 0)
    def _():
        carry_scr[...] = jnp.zeros_like(carry_scr)

    h = _moe_input(x_ref, m_ref, g_ref)
    logits = lax.dot_general(wr_ref[...], h, (((1,), (1,)), ((), ())), preferred_element_type=F32,
                             precision=lax.Precision.HIGHEST) + br_ref[...]
    e_io = lax.broadcasted_iota(I32, (n_experts, tm), 0).astype(F32)
    work = logits
    vals, idxs = [], []
    chosen = jnp.zeros((n_experts, tm), F32)
    for _ in range(TOP_K):
        mx = work.max(axis=0, keepdims=True)
        ix = jnp.min(jnp.where(work == mx, e_io, float(n_experts)), axis=0, keepdims=True)
        hit = e_io == ix
        vals.append(mx)
        idxs.append(ix)
        chosen = jnp.where(hit, 1.0, chosen)
        work = jnp.where(hit, -jnp.inf, work)
    es = [jnp.exp(v - vals[0]) for v in vals]
    den = es[0] + es[1] + es[2] + es[3]
    srow = lax.broadcasted_iota(I32, (tm, tm), 0)
    scol = lax.broadcasted_iota(I32, (tm, tm), 1)
    before = jnp.where(srow < scol, 1.0, 0.0).astype(BF16)
    pos = jnp.dot(chosen.astype(BF16), before, preferred_element_type=F32) + carry_scr[...]
    ranks = [jnp.sum(jnp.where(e_io == ix, pos, 0.0), axis=0, keepdims=True) for ix in idxs]
    carry_scr[...] = carry_scr[...] + chosen.sum(axis=1, keepdims=True)
    idx_ref[...] = jnp.concatenate(idxs, axis=0).astype(I32)
    gate_ref[...] = jnp.concatenate([e / den for e in es], axis=0)
    rank_ref[...] = jnp.concatenate(ranks, axis=0).astype(I32)
    cnt_ref[...] = jnp.broadcast_to(carry_scr[...], cnt_ref.shape).astype(I32)


def _route_call(x, modsel, g, w_router, b_router):
    n, d = x.shape
    n_experts = w_router.shape[1]
    tm = TOKEN_TILE
    body = functools.partial(_route_kernel, n_experts=n_experts)
    return pl.pallas_call(
        body,
        grid=(n // tm,),
        in_specs=[pl.BlockSpec((tm, d), lambda i: (i, 0)),
                  pl.BlockSpec((1, 6, d), lambda i: (i, 0, 0)),
                  pl.BlockSpec((1, d), lambda i: (0, 0)),
                  pl.BlockSpec((n_experts, d), lambda i: (0, 0)),
                  pl.BlockSpec((n_experts, 1), lambda i: (0, 0))],
        out_specs=[pl.BlockSpec((TOP_K, tm), lambda i: (0, i)),
                   pl.BlockSpec((TOP_K, tm), lambda i: (0, i)),
                   pl.BlockSpec((TOP_K, tm), lambda i: (0, i)),
                   pl.BlockSpec((n_experts, 128), lambda i: (0, 0))],
        out_shape=[jax.ShapeDtypeStruct((TOP_K, n), I32),
                   jax.ShapeDtypeStruct((TOP_K, n), F32),
                   jax.ShapeDtypeStruct((TOP_K, n), I32),
                   jax.ShapeDtypeStruct((n_experts, 128), I32)],
        scratch_shapes=[pltpu.VMEM((n_experts, 1), F32)],
        compiler_params=_params(("arbitrary",)),
        name="moe_route",
    )(x, modsel, g.reshape(1, d), w_router.T, b_router.reshape(n_experts, 1))


def _slot_kernel(cnt_ref, idx_ref, rank_ref, dest_ref, binfo_ref, pad_ref, nused_ref, *, n_experts, n_blocks):
    cnt = cnt_ref[:, 0:1].astype(F32)
    padded = jnp.ceil(cnt * (1.0 / MOE_BLOCK)) * MOE_BLOCK
    er = lax.broadcasted_iota(I32, (n_experts, n_experts), 0)
    ec = lax.broadcasted_iota(I32, (n_experts, n_experts), 1)
    start_row = jnp.sum(jnp.where(er < ec, padded, 0.0), axis=0, keepdims=True)
    start_col = jnp.sum(jnp.where(er == ec, start_row, 0.0), axis=1, keepdims=True)
    end_col = start_col + padded
    idx = idx_ref[...]
    e_io = lax.broadcasted_iota(I32, (n_experts,) + idx.shape[1:], 0)
    rows = []
    for k in range(TOP_K):
        hit = e_io == idx[k:k + 1, :]
        rows.append(jnp.sum(jnp.where(hit, start_col, 0.0), axis=0, keepdims=True))
    dest_ref[...] = jnp.concatenate(rows, axis=0).astype(I32) + rank_ref[...]
    blk_start = (lax.broadcasted_iota(I32, (n_experts, n_blocks), 1) * MOE_BLOCK).astype(F32)
    n_done = jnp.sum(jnp.where(end_col <= blk_start, 1.0, 0.0), axis=0, keepdims=True)
    bexp = jnp.minimum(n_done, n_experts - 1.0)
    used_row = jnp.sum(jnp.where(er == ec, jnp.where(cnt > 0.0, 1.0, 0.0), 0.0), axis=0, keepdims=True)
    ecf = ec.astype(F32)
    next_col = jnp.min(jnp.where((ec > er) & (used_row > 0.0), ecf, float(n_experts)), axis=1, keepdims=True)
    ord_col = jnp.sum(jnp.where(ec < er, used_row, 0.0), axis=1, keepdims=True)
    par_col = ord_col - 2.0 * jnp.floor(ord_col * 0.5)
    mine = lax.broadcasted_iota(I32, (n_experts, n_blocks), 0).astype(F32) == bexp
    bnext = jnp.sum(jnp.where(mine, next_col, 0.0), axis=0, keepdims=True)
    bslot = jnp.sum(jnp.where(mine, par_col, 0.0), axis=0, keepdims=True)
    binfo_ref[...] = jnp.concatenate([bexp, bnext, bslot], axis=0).astype(I32)
    n_used = jnp.sum(padded, axis=0, keepdims=True) * (1.0 / MOE_BLOCK)
    nused_ref[...] = n_used.astype(I32)
    cnt_row = jnp.sum(jnp.where(er == ec, cnt, 0.0), axis=0, keepdims=True)
    padded_row = jnp.sum(jnp.where(er == ec, padded, 0.0), axis=0, keepdims=True)
    pad_ref[...] = jnp.concatenate([start_row + cnt_row, padded_row - cnt_row,
                                    jnp.broadcast_to(n_used, cnt_row.shape)], axis=0).astype(I32)


def _slot_call(counts, idx_t, rank_t, n_blocks):
    n_experts = counts.shape[0]
    n = idx_t.shape[1]
    tn = min(2048, n)
    body = functools.partial(_slot_kernel, n_experts=n_experts, n_blocks=n_blocks)
    return pl.pallas_call(
        body,
        grid=(n // tn,),
        in_specs=[pl.BlockSpec((n_experts, 128), lambda i: (0, 0)),
                  pl.BlockSpec((TOP_K, tn), lambda i: (0, i)),
                  pl.BlockSpec((TOP_K, tn), lambda i: (0, i))],
        out_specs=[pl.BlockSpec((TOP_K, tn), lambda i: (0, i)),
                   pl.BlockSpec((3, n_blocks), lambda i: (0, 0)),
                   pl.BlockSpec((3, n_experts), lambda i: (0, 0)),
                   pl.BlockSpec((1, 1), lambda i: (0, 0))],
        out_shape=[jax.ShapeDtypeStruct((TOP_K, n), I32),
                   jax.ShapeDtypeStruct((3, n_blocks), I32),
                   jax.ShapeDtypeStruct((3, n_experts), I32),
                   jax.ShapeDtypeStruct((1, 1), I32)],
        compiler_params=_params(("arbitrary",)),
        name="moe_slots",
    )(counts, idx_t, rank_t)


DMA_ISSUE_UNROLL = 8


def _to_row_tiles(ref, base, x):
    rows = x.shape[0]
    for c in range(ROW_SUBLANES):
        ref[pl.ds(base * ROW_SUBLANES + c, rows, stride=ROW_SUBLANES), :] = x[:, c * LANES:(c + 1) * LANES]


def _from_row_tiles(ref, base, rows, c):
    return ref[pl.ds(base * ROW_SUBLANES + c, rows, stride=ROW_SUBLANES), :]


def _row_tile(ref, r):
    return ref.at[pl.ds(pl.multiple_of(r * ROW_SUBLANES, ROW_SUBLANES), ROW_SUBLANES)]


def _zero_fill_padding(pad_ref, xs_ref, z_scr, sem):
    z_scr[...] = jnp.zeros_like(z_scr)
    n_experts = pad_ref.shape[1]
    bits = range(MOE_BLOCK.bit_length() - 2, -1, -1)

    def pieces(e):
        off, length = pad_ref[0, e], pad_ref[1, e]
        for bit in bits:
            size = 1 << bit
            done = (length >> (bit + 1)) << (bit + 1)
            copy = pltpu.make_async_copy(z_scr.at[pl.ds(0, size * ROW_SUBLANES)],
                                         xs_ref.at[pl.ds(pl.multiple_of((off + done) * ROW_SUBLANES, ROW_SUBLANES),
                                                         size * ROW_SUBLANES)], sem)
            yield (length & size) != 0, copy

    def tail_blocks():
        n_blocks = xs_ref.shape[0] // (MOE_BLOCK * ROW_SUBLANES)
        for b in range(n_blocks - n_experts, n_blocks):
            copy = pltpu.make_async_copy(z_scr, xs_ref.at[pl.ds(b * MOE_BLOCK * ROW_SUBLANES,
                                                                MOE_BLOCK * ROW_SUBLANES)], sem)
            yield b >= pad_ref[2, 0], copy

    def all_copies():
        for e in range(n_experts):
            yield from pieces(e)
        yield from tail_blocks()

    for needed, copy in all_copies():
        pl.when(needed)(copy.start)
    for needed, copy in all_copies():
        pl.when(needed)(copy.wait)


def _dispatch_kernel(pad_ref, dest_ref, x_ref, m_ref, g_ref, xs_ref, h_scr, z_scr, sem):
    tm = x_ref.shape[0]

    @pl.when(pl.program_id(0) == 0)
    def _():
        _zero_fill_padding(pad_ref, xs_ref, z_scr, sem.at[1])

    _to_row_tiles(h_scr, 0, _moe_input(x_ref, m_ref, g_ref))

    def start_row(r, carry):
        for k in range(TOP_K):
            pltpu.make_async_copy(_row_tile(h_scr, r), _row_tile(xs_ref, dest_ref[0, 0, k * tm + r]),
                                  sem.at[0]).start(priority=k % 2)
        return carry

    lax.fori_loop(0, tm, start_row, 0, unroll=DMA_ISSUE_UNROLL // TOP_K)
    for _ in range(TOP_K):
        pltpu.make_async_copy(h_scr, xs_ref.at[pl.ds(0, tm * ROW_SUBLANES)], sem.at[0]).wait()


def _dispatch_call(pad_info, dest_tiles, x, modsel, g, n_slots):
    n, d = x.shape
    tm = TOKEN_TILE
    assert d == ROW_SUBLANES * LANES
    return pl.pallas_call(
        _dispatch_kernel,
        grid=(n // tm,),
        in_specs=[pl.BlockSpec(memory_space=pltpu.SMEM),
                  pl.BlockSpec((1, 1, TOP_K * tm), lambda i: (i, 0, 0), memory_space=pltpu.SMEM),
                  pl.BlockSpec((tm, d), lambda i: (i, 0)),
                  pl.BlockSpec((1, 6, d), lambda i: (i, 0, 0)),
                  pl.BlockSpec((1, d), lambda i: (0, 0))],
        out_specs=pl.BlockSpec(memory_space=pl.ANY),
        out_shape=jax.ShapeDtypeStruct((n_slots * ROW_SUBLANES, LANES), F32),
        scratch_shapes=[pltpu.VMEM((tm * ROW_SUBLANES, LANES), F32),
                        pltpu.VMEM((MOE_BLOCK * ROW_SUBLANES, LANES), F32),
                        pltpu.SemaphoreType.DMA((2,))],
        compiler_params=_params(("arbitrary",)),
        name="moe_dispatch",
    )(pad_info, dest_tiles, x, modsel, g.reshape(1, d))


def _ffn_kernel(binfo_ref, nused_ref, xs_ref, wgu_hbm, bgu_ref, wdn_hbm, bdn_ref, ys_ref,
                wgu_f32, wdn_f32, wgu_scr, wdn_scr, sem, *, layer, n_experts, n_blocks):
    b = pl.program_id(0)
    d_ff = wdn_scr.shape[0]
    rows = xs_ref.shape[0] // ROW_SUBLANES

    def weight_copies(e, slot):
        return (pltpu.make_async_copy(wgu_hbm.at[layer, e], wgu_f32.at[slot], sem.at[slot]),
                pltpu.make_async_copy(wdn_hbm.at[layer, e], wdn_f32.at[slot], sem.at[slot]))

    @pl.when(b < nused_ref[0])
    def _():
        e = binfo_ref[b]
        prev = binfo_ref[jnp.maximum(b - 1, 0)]
        slot = binfo_ref[2 * n_blocks + b]

        @pl.when((b == 0) | (e != prev))
        def _():
            @pl.when(b == 0)
            def _():
                for cp in weight_copies(e, slot):
                    cp.start()

            nxt = binfo_ref[n_blocks + b]

            @pl.when(nxt < n_experts)
            def _():
                for cp in weight_copies(nxt, 1 - slot):
                    cp.start()

            for cp in weight_copies(e, slot):
                cp.wait()
            wgu_scr[...] = wgu_f32[slot].astype(BF16)
            wdn_scr[...] = wdn_f32[slot].astype(BF16)

        x = jnp.concatenate([_from_row_tiles(xs_ref, 0, rows, c).astype(BF16) for c in range(ROW_SUBLANES)],
                            axis=-1)
        gu = jnp.dot(x, wgu_scr[...], preferred_element_type=F32) + bgu_ref[0, 0]
        x_glu = jnp.minimum(gu[:, :d_ff], SWIGLU_LIMIT)
        x_lin = jnp.clip(gu[:, d_ff:], -SWIGLU_LIMIT, SWIGLU_LIMIT)
        hid = x_glu * _sigmoid(SWIGLU_ALPHA * x_glu) * (x_lin + 1.0)
        _to_row_tiles(ys_ref, 0, jnp.dot(hid.astype(BF16), wdn_scr[...], preferred_element_type=F32)
                      + bdn_ref[0, 0])

    @pl.when(b >= nused_ref[0])
    def _():
        ys_ref[...] = jnp.zeros_like(ys_ref)


def _ffn_call(layer, block_info, n_used, xs, w_gu, b_gu, w_dn, b_dn):
    depth, n_experts, d, d_ff2 = w_gu.shape
    d_ff = d_ff2 // 2
    blk_rows = MOE_BLOCK * ROW_SUBLANES
    n_blocks = xs.shape[0] // blk_rows

    def blk(b, nu):
        return jnp.maximum(jnp.minimum(b, nu[0] - 1), 0)

    grid_spec = pltpu.PrefetchScalarGridSpec(
        num_scalar_prefetch=2,
        grid=(n_blocks,),
        in_specs=[pl.BlockSpec((blk_rows, LANES), lambda b, bi, nu: (blk(b, nu), 0)),
                  pl.BlockSpec(memory_space=pl.ANY),
                  pl.BlockSpec((1, 1, 1, d_ff2), lambda b, bi, nu: (layer, bi[blk(b, nu)], 0, 0)),
                  pl.BlockSpec(memory_space=pl.ANY),
                  pl.BlockSpec((1, 1, 1, d), lambda b, bi, nu: (layer, bi[blk(b, nu)], 0, 0))],
        out_specs=pl.BlockSpec((blk_rows, LANES), lambda b, bi, nu: (b, 0)),
        scratch_shapes=[pltpu.VMEM((2, d, d_ff2), F32), pltpu.VMEM((2, d_ff, d), F32),
                        pltpu.VMEM((d, d_ff2), BF16), pltpu.VMEM((d_ff, d), BF16),
                        pltpu.SemaphoreType.DMA((2,))],
    )
    body = functools.partial(_ffn_kernel, layer=layer, n_experts=n_experts, n_blocks=n_blocks)
    return pl.pallas_call(
        body,
        grid_spec=grid_spec,
        out_shape=jax.ShapeDtypeStruct(xs.shape, F32),
        compiler_params=_params(("arbitrary",)),
        name="moe_ffn",
    )(block_info, n_used, xs, w_gu, b_gu.reshape(depth, n_experts, 1, d_ff2), w_dn,
      b_dn.reshape(depth, n_experts, 1, d))


def _combine_kernel(dest_ref, ys_ref, gate_ref, x_ref, m_ref, fg_ref, y_ref, buf, sem, *, final_norm):
    tm = x_ref.shape[0]

    def start_pair(p, carry):
        for u in range(2):
            j = 2 * p + u
            pltpu.make_async_copy(_row_tile(ys_ref, dest_ref[0, 0, j]), _row_tile(buf, j),
                                  sem.at[0]).start(priority=u)
        return carry

    lax.fori_loop(0, TOP_K * tm // 2, start_pair, 0, unroll=DMA_ISSUE_UNROLL // 2)
    pltpu.make_async_copy(ys_ref.at[pl.ds(0, TOP_K * tm * ROW_SUBLANES)], buf, sem.at[0]).wait()
    chunks = []
    for c in range(ROW_SUBLANES):
        y = gate_ref[:, 0:1] * _from_row_tiles(buf, 0, tm, c)
        for k in range(1, TOP_K):
            y = y + gate_ref[:, k:k + 1] * _from_row_tiles(buf, k * tm, tm, c)
        chunks.append(y)
    out = x_ref[...] + m_ref[0, 5:6, :] * jnp.concatenate(chunks, axis=-1)
    if final_norm:
        out = _rms(out, fg_ref[...])
    y_ref[...] = out


def _combine_call(dest_tiles, ys, gates_nk, x, modsel, final_g, final_norm):
    n, d = x.shape
    tm = TOKEN_TILE
    body = functools.partial(_combine_kernel, final_norm=final_norm)
    return pl.pallas_call(
        body,
        grid=(n // tm,),
        in_specs=[pl.BlockSpec((1, 1, TOP_K * tm), lambda i: (i, 0, 0), memory_space=pltpu.SMEM),
                  pl.BlockSpec(memory_space=pl.ANY),
                  pl.BlockSpec((tm, TOP_K), lambda i: (i, 0)),
                  pl.BlockSpec((tm, d), lambda i: (i, 0)),
                  pl.BlockSpec((1, 6, d), lambda i: (i, 0, 0)),
                  pl.BlockSpec((1, d), lambda i: (0, 0))],
        out_specs=pl.BlockSpec((tm, d), lambda i: (i, 0)),
        out_shape=jax.ShapeDtypeStruct((n, d), F32),
        scratch_shapes=[pltpu.VMEM((TOP_K * tm * ROW_SUBLANES, LANES), F32), pltpu.SemaphoreType.DMA((1,))],
        compiler_params=_params(("arbitrary",)),
        name="moe_combine",
    )(dest_tiles, ys, gates_nk, x, modsel, final_g.reshape(1, d))


def _moe_layer(layer, x, modsel, g2, w_router, b_router, w_gu, b_gu, w_dn, b_dn, final_g, final_norm):
    n, d = x.shape
    n_experts = w_router.shape[1]
    tm = TOKEN_TILE
    n_blocks = (n * TOP_K) // MOE_BLOCK + n_experts
    idx_t, gate_t, rank_t, counts = _route_call(x, modsel, g2, w_router, b_router)
    dest_t, block_info, pad_info, n_used = _slot_call(counts, idx_t, rank_t, n_blocks)
    dest_tiles = dest_t.reshape(TOP_K, n // tm, tm).transpose(1, 0, 2).reshape(n // tm, 1, TOP_K * tm)
    xs = _dispatch_call(pad_info, dest_tiles, x, modsel, g2, n_blocks * MOE_BLOCK)
    ys = _ffn_call(layer, block_info.reshape(3 * n_blocks), n_used.reshape(1), xs, w_gu, b_gu, w_dn, b_dn)
    return _combine_call(dest_tiles, ys, gate_t.T, x, modsel, final_g, final_norm)


def _rope_tables(t, hd):
    pos = np.arange(t)
    n_freq = hd // 4
    inv_freq = ROPE_THETA ** (-np.arange(n_freq, dtype=np.float32) / n_freq)
    ang = np.concatenate([(pos // GRID_W).astype(np.float32)[:, None] * inv_freq,
                          (pos % GRID_W).astype(np.float32)[:, None] * inv_freq], axis=-1)
    ang = jnp.asarray(ang, F32)
    cos, sin = jnp.cos(ang), jnp.sin(ang)
    return jnp.concatenate([cos, cos], axis=-1), jnp.concatenate([-sin, sin], axis=-1)


def kernel(x_prompt, x_sample, cache_k_a, cache_v_a, state_b, state_c_C, state_c_n, state_c_m, cache_k_d, cache_v_d, c, c_ctx, norm1_g, norm2_g, w_mod, b_mod, w_in_a, qnorm_a, knorm_a, w_out_a, w_in_b, b_f_b, lower_bounds_b, onorm_b, w_out_b, w_in_c, b_gates_c, onorm_c, w_out_c, w_in_d, rpb_d, w_out_d, w_router, b_router, w_gu, b_gu, w_dn, b_dn, final_g):
    n_ctx_seq, t_ctx, d = x_prompt.shape
    n_lat_seq, t_lat, _ = x_sample.shape
    depth = w_mod.shape[0]
    n_ctx = n_ctx_seq * t_ctx
    n_lat = n_lat_seq * t_lat
    n = n_ctx + n_lat
    tm = TOKEN_TILE
    assert t_ctx % tm == 0 and t_lat % tm == 0 and n_lat_seq + 1 <= 8

    lb_cum = jnp.cumsum(jax.nn.softmax(lower_bounds_b.astype(F32), axis=0), axis=0)
    lb_all = lb_cum - lb_cum[0]

    cond8 = jnp.zeros((8, d), F32).at[0].set(c_ctx).at[1:1 + n_lat_seq].set(c)
    mod = _mod_call(cond8, w_mod, b_mod)
    tile_row = np.concatenate([np.zeros(n_ctx // tm, np.int32),
                               1 + np.repeat(np.arange(n_lat_seq, dtype=np.int32), t_lat // tm)])

    x = jnp.concatenate([x_prompt.reshape(n_ctx, d), x_sample.reshape(n_lat, d)], axis=0)
    outs = {}
    for i in range(depth):
        kind = i % 4
        j = i // 4
        modsel = mod[i].reshape(8, 6, d)[tile_row]
        if kind == 0:
            proj = _inproj_call(x, modsel, norm1_g[i], w_in_a[j].astype(BF16))
            o_ctx, k_new = _gqa_ctx_call(proj, qnorm_a[j], knorm_a[j], n_ctx_seq, t_ctx)
            cosd, sind = _rope_tables(t_lat, 128)
            o_lat = _gqa_lat_call(proj, n_ctx, n_lat_seq, t_lat,
                                  cache_k_a[:, j].reshape(n_lat_seq, -1, 256),
                                  cache_v_a[:, j].reshape(n_lat_seq, -1, 256), cosd, sind,
                                  qnorm_a[j], knorm_a[j])
            outs["k_a"] = k_new.reshape(n_ctx_seq, 1, t_ctx, 2, 128)
            outs["v_a"] = proj[:n_ctx, 1280:1536].reshape(n_ctx_seq, 1, t_ctx, 2, 128)
            x = _outproj_call("plain", [(o_ctx, o_lat)], None, w_out_a[j].astype(BF16), x, modsel)
        elif kind == 1:
            proj = _inproj_call(x, modsel, norm1_g[i], w_in_b[j].astype(BF16))
            o_dirs, s_dirs = [], []
            for reverse in (False, True):
                oc, sc = _hgrn_call(proj, 0, n_ctx_seq, t_ctx, b_f_b[j], lb_all[i], None, reverse)
                ol, _ = _hgrn_call(proj, n_ctx, n_lat_seq, t_lat, b_f_b[j], lb_all[i], state_b[:, j], reverse)
                o_dirs.append((oc, ol))
                s_dirs.append(sc)
            outs["s_b"] = jnp.concatenate(s_dirs, axis=1)[:, None]
            x = _outproj_call("hgrn", o_dirs, (proj, 4, onorm_b[j].reshape(1, 128)),
                              w_out_b[j].astype(BF16), x, modsel)
        elif kind == 2:
            w_c = jnp.pad(w_in_c[j], ((0, 0), (0, 128 - 32))).astype(BF16)
            bg = jnp.pad(b_gates_c[j].reshape(1, 32), ((0, 0), (0, 128 - 32)))
            proj = _inproj_call(x, modsel, norm1_g[i], w_c)
            state = (state_c_C[:, j], state_c_n[:, j][:, :, :, None, :],
                     jnp.pad(state_c_m[:, j], ((0, 0), (0, 0), (0, 120)))[:, :, None, :])
            o_dirs, st = [], []
            for reverse in (False, True):
                oc, cc, nc, mc = _mlstm_call(proj, 0, n_ctx_seq, t_ctx, bg, None, reverse)
                ol, _, _, _ = _mlstm_call(proj, n_ctx, n_lat_seq, t_lat, bg, state, reverse)
                o_dirs.append((oc, ol))
                st.append((cc, nc, mc))
            outs["c_C"] = jnp.concatenate([st[0][0], st[1][0]], axis=1)[:, None]
            outs["c_n"] = jnp.concatenate([st[0][1], st[1][1]], axis=1)[:, None, :, :, 0, :]
            outs["c_m"] = jnp.concatenate([st[0][2], st[1][2]], axis=1)[:, None, :, 0, :8]
            x = _outproj_call("mlstm", o_dirs, (proj, 2, onorm_c[j].reshape(1, 128)),
                              w_out_c[j].astype(BF16), x, modsel)
        else:
            proj = _inproj_call(x, modsel, norm1_g[i], w_in_d[j].astype(BF16))
            o_ctx = _mha_ctx_call(proj, n_ctx_seq, t_ctx)
            tz = _na_bias_call(rpb_d[j])
            o_lat = _na_call(proj, n_ctx, n_lat_seq, t_lat,
                             cache_k_d[:, j].reshape(n_lat_seq, -1, d),
                             cache_v_d[:, j].reshape(n_lat_seq, -1, d), tz)
            outs["k_d"] = proj[:n_ctx, d:2 * d].reshape(n_ctx_seq, 1, t_ctx, 16, 64)
            outs["v_d"] = proj[:n_ctx, 2 * d:3 * d].reshape(n_ctx_seq, 1, t_ctx, 16, 64)
            x = _outproj_call("plain", [(o_ctx, o_lat)], None, w_out_d[j].astype(BF16), x, modsel)
        x = _moe_layer(i, x, modsel, norm2_g[i], w_router[i], b_router[i], w_gu, b_gu, w_dn, b_dn,
                       final_g, final_norm=(i == depth - 1))

    y_prompt = x[:n_ctx].reshape(n_ctx_seq, t_ctx, d)
    y_sample = x[n_ctx:].reshape(n_lat_seq, t_lat, d)
    return (y_prompt, y_sample, outs["k_a"], outs["v_a"], outs["s_b"], outs["c_C"], outs["c_n"], outs["c_m"],
            outs["k_d"], outs["v_d"])
```

```python
import functools

import numpy as np
import jax
import jax.numpy as jnp
from jax import lax
from jax.experimental import pallas as pl
from jax.experimental.pallas import tpu as pltpu

F32 = jnp.float32
BF16 = jnp.bfloat16
I32 = jnp.int32

NORM_EPS = 1e-6
GRID_W = 64
ROPE_THETA = 10000.0
TOP_K = 4
HGRN_CHUNK = 128
GATE_SOFTCAP = 15.0
NA_ROWS = 8
NA_COLS = 16
SWIGLU_ALPHA = 1.702
SWIGLU_LIMIT = 7.0
NEG_BIG = -1e30

LANES = 128
ROW_SUBLANES = 8
TOKEN_TILE = 256
MOE_BLOCK = 256
ATTN_HEAD_GROUP = 2
V7X_VMEM_LIMIT = 52 * 1024 * 1024


def _params(sem, vmem=V7X_VMEM_LIMIT):
    return pltpu.CompilerParams(dimension_semantics=sem, vmem_limit_bytes=vmem)


def _bdot(a, b):
    return jnp.dot(a.astype(BF16), b.astype(BF16), preferred_element_type=F32)


def _bdot_nt(a, b):
    return lax.dot_general(a.astype(BF16), b.astype(BF16), (((1,), (1,)), ((), ())),
                           preferred_element_type=F32)


def _bdot_tn(a, b):
    return lax.dot_general(a.astype(BF16), b.astype(BF16), (((0,), (0,)), ((), ())),
                           preferred_element_type=F32)


def _fdot(a, b):
    return jnp.dot(a, b, preferred_element_type=F32, precision=lax.Precision.HIGHEST)


def _rms(x, g):
    return x * lax.rsqrt(jnp.mean(x * x, axis=-1, keepdims=True) + NORM_EPS) * g


def _rms_heads(x, g, n_heads, hd):
    return jnp.concatenate([_rms(x[:, h * hd:(h + 1) * hd], g) for h in range(n_heads)], axis=-1)


def _sigmoid(x):
    return 1.0 / (1.0 + jnp.exp(-x))


def _silu(x):
    return x * _sigmoid(x)


def _softmax_rows(parts):
    m = parts[0].max(axis=-1, keepdims=True)
    for p in parts[1:]:
        m = jnp.maximum(m, p.max(axis=-1, keepdims=True))
    es = [jnp.exp(p - m) for p in parts]
    den = es[0].sum(axis=-1, keepdims=True)
    for e in es[1:]:
        den = den + e.sum(axis=-1, keepdims=True)
    return es, den


def _mod_kernel(c_ref, w_ref, b_ref, o_ref):
    o_ref[0] = _bdot(_silu(c_ref[...]), w_ref[0]) + b_ref[0]


def _mod_call(cond8, w_mod, b_mod):
    depth, d, d6 = w_mod.shape
    tn = 1024
    return pl.pallas_call(
        _mod_kernel,
        grid=(depth, d6 // tn),
        in_specs=[pl.BlockSpec((8, d), lambda i, j: (0, 0)),
                  pl.BlockSpec((1, d, tn), lambda i, j: (i, 0, j)),
                  pl.BlockSpec((1, 1, tn), lambda i, j: (i, 0, j))],
        out_specs=pl.BlockSpec((1, 8, tn), lambda i, j: (i, 0, j)),
        out_shape=jax.ShapeDtypeStruct((depth, 8, d6), F32),
        compiler_params=_params(("arbitrary", "arbitrary")),
        name="adaln_mod",
    )(cond8, w_mod, b_mod.reshape(depth, 1, d6))


def _inproj_kernel(x_ref, m_ref, g_ref, w_ref, o_ref):
    h = _rms(x_ref[...], g_ref[...]) * (1.0 + m_ref[0, 1:2, :]) + m_ref[0, 0:1, :]
    o_ref[...] = _bdot(h, w_ref[...])


def _inproj_call(x, modsel, g, w):
    n, d = x.shape
    wout = w.shape[1]
    tm = TOKEN_TILE
    return pl.pallas_call(
        _inproj_kernel,
        grid=(n // tm,),
        in_specs=[pl.BlockSpec((tm, d), lambda i: (i, 0)),
                  pl.BlockSpec((1, 6, d), lambda i: (i, 0, 0)),
                  pl.BlockSpec((1, d), lambda i: (0, 0)),
                  pl.BlockSpec((d, wout), lambda i: (0, 0))],
        out_specs=pl.BlockSpec((tm, wout), lambda i: (i, 0)),
        out_shape=jax.ShapeDtypeStruct((n, wout), F32),
        compiler_params=_params(("arbitrary",)),
        name="inproj",
    )(x, modsel, g.reshape(1, d), w)


def _outproj_kernel(*refs, kind, n_pairs, n_ctx_tiles, n_heads, hd):
    in_ctx = pl.program_id(0) < n_ctx_tiles
    mix = [jnp.where(in_ctx, refs[2 * p][...], refs[2 * p + 1][...]) for p in range(n_pairs)]
    rest = refs[2 * n_pairs:]
    if kind == "plain":
        w_ref, x_ref, m_ref, y_ref = rest
        o = mix[0]
    else:
        gate_ref, on_ref, w_ref, x_ref, m_ref, y_ref = rest
        o = _rms_heads(mix[0] + mix[1], on_ref[...], n_heads, hd)
        o = o * _silu(gate_ref[...]) if kind == "hgrn" else _sigmoid(gate_ref[...]) * o
    y_ref[...] = x_ref[...] + m_ref[0, 2:3, :] * _bdot(o, w_ref[...])


def _outproj_call(kind, pairs, gate, w, x, modsel):
    n, d = x.shape
    tm = TOKEN_TILE
    n_ctx_tiles = pairs[0][0].shape[0] // tm
    specs, args = [], []
    for a_ctx, a_lat in pairs:
        specs += [pl.BlockSpec((tm, d), lambda i: (jnp.minimum(i, n_ctx_tiles - 1), 0)),
                  pl.BlockSpec((tm, d), lambda i: (jnp.maximum(i - n_ctx_tiles, 0), 0))]
        args += [a_ctx, a_lat]
    if gate is not None:
        proj, cb, on = gate
        specs += [pl.BlockSpec((tm, d), lambda i: (i, cb)), pl.BlockSpec(on.shape, lambda i: (0, 0))]
        args += [proj, on]
    specs += [pl.BlockSpec(w.shape, lambda i: (0, 0)),
              pl.BlockSpec((tm, d), lambda i: (i, 0)),
              pl.BlockSpec((1, 6, d), lambda i: (i, 0, 0))]
    args += [w, x, modsel]
    body = functools.partial(_outproj_kernel, kind=kind, n_pairs=len(pairs), n_ctx_tiles=n_ctx_tiles,
                             n_heads=8, hd=128)
    return pl.pallas_call(
        body,
        grid=(n // tm,),
        in_specs=specs,
        out_specs=pl.BlockSpec((tm, d), lambda i: (i, 0)),
        out_shape=jax.ShapeDtypeStruct((n, d), F32),
        compiler_params=_params(("arbitrary",)),
        name="outproj_" + kind,
    )(*args)


def _gqa_ctx_kernel(p_ref, qn_ref, kn_ref, o_ref, k_ref, *, n_heads, n_kv, hd):
    rep = n_heads // n_kv
    scale = hd ** -0.5
    koff = n_heads * hd
    voff = koff + n_kv * hd
    ks = [_rms(p_ref[:, koff + g * hd: koff + (g + 1) * hd], kn_ref[...]) for g in range(n_kv)]
    k_ref[...] = jnp.concatenate(ks, axis=-1)
    for h in range(n_heads):
        g = h // rep
        q = _rms(p_ref[:, h * hd:(h + 1) * hd], qn_ref[...])
        s = _bdot_nt(q, ks[g]) * scale
        (e,), den = _softmax_rows([s])
        o_ref[:, h * hd:(h + 1) * hd] = _bdot(e / den, p_ref[:, voff + g * hd: voff + (g + 1) * hd])


def _gqa_ctx_call(proj, qn, kn, n_seq, t):
    n_heads, n_kv, hd = 8, 2, 128
    win = proj.shape[1]
    body = functools.partial(_gqa_ctx_kernel, n_heads=n_heads, n_kv=n_kv, hd=hd)
    return pl.pallas_call(
        body,
        grid=(n_seq,),
        in_specs=[pl.BlockSpec((t, win), lambda b: (b, 0)),
                  pl.BlockSpec((1, hd), lambda b: (0, 0)),
                  pl.BlockSpec((1, hd), lambda b: (0, 0))],
        out_specs=[pl.BlockSpec((t, n_heads * hd), lambda b: (b, 0)),
                   pl.BlockSpec((t, n_kv * hd), lambda b: (b, 0))],
        out_shape=[jax.ShapeDtypeStruct((n_seq * t, n_heads * hd), F32),
                   jax.ShapeDtypeStruct((n_seq * t, n_kv * hd), F32)],
        compiler_params=_params(("arbitrary",)),
        name="gqa_ctx",
    )(proj, qn.reshape(1, hd), kn.reshape(1, hd))


def _rope(x, cosd, sind):
    return x * cosd + pltpu.roll(x, x.shape[-1] // 2, 1) * sind


def _gqa_lat_kernel(pq_ref, pkv_ref, kc_ref, vc_ref, cq_ref, sq_ref, ck_ref, sk_ref, qn_ref, kn_ref,
                    o_ref, k_scr, v_scr, *, n_heads, n_kv, hd, t_ctx):
    rep = n_heads // n_kv
    scale = hd ** -0.5

    @pl.when(pl.program_id(1) == 0)
    def _():
        k_scr[0:t_ctx, :] = kc_ref[0].astype(BF16)
        v_scr[...] = jnp.ones_like(v_scr)
        for g in range(n_kv):
            k = _rms(pkv_ref[:, g * hd:(g + 1) * hd], kn_ref[...])
            k_scr[t_ctx:, g * hd:(g + 1) * hd] = _rope(k, ck_ref[...], sk_ref[...]).astype(BF16)
            v_scr[0:t_ctx, 2 * g * hd:(2 * g + 1) * hd] = vc_ref[0, :, g * hd:(g + 1) * hd].astype(BF16)
            v_scr[t_ctx:, 2 * g * hd:(2 * g + 1) * hd] = pkv_ref[:, (n_kv + g) * hd:(n_kv + g + 1) * hd].astype(BF16)

    qs = [(_rope(_rms(pq_ref[:, h * hd:(h + 1) * hd], qn_ref[...]), cq_ref[...], sq_ref[...]) * scale).astype(BF16)
          for h in range(n_heads)]
    for h0 in range(0, n_heads, ATTN_HEAD_GROUP):
        hs = range(h0, min(h0 + ATTN_HEAD_GROUP, n_heads))
        s = [_bdot_nt(qs[h], k_scr[:, (h // rep) * hd:(h // rep + 1) * hd]) for h in hs]
        e = [jnp.exp(x - x.max(axis=-1, keepdims=True)).astype(BF16) for x in s]
        pv = [jnp.dot(e[i], v_scr[:, 2 * (h // rep) * hd:2 * (h // rep + 1) * hd], preferred_element_type=F32)
              for i, h in enumerate(hs)]
        for i, h in enumerate(hs):
            o_ref[:, h * hd:(h + 1) * hd] = pv[i][:, :hd] / pv[i][:, hd:]


def _gqa_lat_call(proj, row0, n_seq, t, cache_k, cache_v, cosd, sind, qn, kn):
    n_heads, n_kv, hd = 8, 2, 128
    tq = 256
    t_ctx = cache_k.shape[1]
    nq = t // tq
    kvw = 2 * n_kv * hd
    qblk0 = row0 // tq
    sblk0 = row0 // t
    body = functools.partial(_gqa_lat_kernel, n_heads=n_heads, n_kv=n_kv, hd=hd, t_ctx=t_ctx)
    return pl.pallas_call(
        body,
        grid=(n_seq, nq),
        in_specs=[pl.BlockSpec((tq, n_heads * hd), lambda b, i: (qblk0 + b * nq + i, 0)),
                  pl.BlockSpec((t, kvw), lambda b, i: (sblk0 + b, (n_heads * hd) // kvw)),
                  pl.BlockSpec((1, t_ctx, n_kv * hd), lambda b, i: (b, 0, 0)),
                  pl.BlockSpec((1, t_ctx, n_kv * hd), lambda b, i: (b, 0, 0)),
                  pl.BlockSpec((tq, hd), lambda b, i: (i, 0)),
                  pl.BlockSpec((tq, hd), lambda b, i: (i, 0)),
                  pl.BlockSpec((t, hd), lambda b, i: (0, 0)),
                  pl.BlockSpec((t, hd), lambda b, i: (0, 0)),
                  pl.BlockSpec((1, hd), lambda b, i: (0, 0)),
                  pl.BlockSpec((1, hd), lambda b, i: (0, 0))],
        out_specs=pl.BlockSpec((tq, n_heads * hd), lambda b, i: (b * nq + i, 0)),
        out_shape=jax.ShapeDtypeStruct((n_seq * t, n_heads * hd), F32),
        scratch_shapes=[pltpu.VMEM((t_ctx + t, n_kv * hd), BF16),
                        pltpu.VMEM((t_ctx + t, 2 * n_kv * hd), BF16)],
        compiler_params=_params(("arbitrary", "arbitrary")),
        name="gqa_latent",
    )(proj, proj, cache_k, cache_v, cosd, sind, cosd, sind, qn.reshape(1, hd), kn.reshape(1, hd))


def _mha_ctx_kernel(q_ref, k_ref, v_ref, o_ref, *, hd):
    scale = hd ** -0.5
    n_heads = q_ref.shape[1] // hd
    ones = jnp.ones((v_ref.shape[0], hd), BF16)
    for h0 in range(0, n_heads, ATTN_HEAD_GROUP):
        sls = [slice(h * hd, (h + 1) * hd) for h in range(h0, min(h0 + ATTN_HEAD_GROUP, n_heads))]
        s = [_bdot_nt(q_ref[:, sl] * scale, k_ref[:, sl]) for sl in sls]
        e = [jnp.exp(x - x.max(axis=-1, keepdims=True)).astype(BF16) for x in s]
        pv = [jnp.dot(e[i], jnp.concatenate([v_ref[:, sl].astype(BF16), ones], axis=-1),
                      preferred_element_type=F32) for i, sl in enumerate(sls)]
        for i, sl in enumerate(sls):
            o_ref[:, sl] = pv[i][:, :hd] / pv[i][:, hd:]


def _mha_ctx_call(proj, n_seq, t):
    hd = 64
    d = proj.shape[1] // 3
    body = functools.partial(_mha_ctx_kernel, hd=hd)
    return pl.pallas_call(
        body,
        grid=(n_seq,),
        in_specs=[pl.BlockSpec((t, d), lambda b: (b, 0)),
                  pl.BlockSpec((t, d), lambda b: (b, 1)),
                  pl.BlockSpec((t, d), lambda b: (b, 2))],
        out_specs=pl.BlockSpec((t, d), lambda b: (b, 0)),
        out_shape=jax.ShapeDtypeStruct((n_seq * t, d), F32),
        compiler_params=_params(("arbitrary",)),
        name="mha_ctx",
    )(proj, proj, proj)


def _na_bias_kernel(rpb_ref, o_ref, *, n_rel_rows, n_rel_cols):
    h = pl.program_id(0)
    w_io = lax.broadcasted_iota(I32, (GRID_W, 2 * GRID_W), 0)
    lane = lax.broadcasted_iota(I32, (GRID_W, 2 * GRID_W), 1)
    ck = jnp.where(lane < GRID_W, lane, lane - GRID_W)
    c_start = jnp.clip(w_io - NA_COLS // 2, 0, GRID_W - NA_COLS)
    in_win = (ck >= c_start) & (ck < c_start + NA_COLS)
    rel = ck - w_io + (NA_COLS - 1)
    base = h * (n_rel_rows * n_rel_cols)
    tiles = []
    for j in range(n_rel_rows):
        acc = jnp.zeros((GRID_W, 2 * GRID_W), F32)
        for jj in range(n_rel_cols):
            acc = jnp.where(rel == jj, rpb_ref[base + j * n_rel_cols + jj], acc)
        tiles.append(jnp.where(in_win, acc, NEG_BIG))
    for j in range(n_rel_rows):
        hi = tiles[j + 1] if j + 1 < n_rel_rows else jnp.full((GRID_W, 2 * GRID_W), NEG_BIG, F32)
        o_ref[0, j] = jnp.where(lane < GRID_W, tiles[j], hi)


def _na_bias_call(rpb):
    n_heads, nrr, nrc = rpb.shape
    body = functools.partial(_na_bias_kernel, n_rel_rows=nrr, n_rel_cols=nrc)
    return pl.pallas_call(
        body,
        grid=(n_heads,),
        in_specs=[pl.BlockSpec(memory_space=pltpu.SMEM)],
        out_specs=pl.BlockSpec((1, nrr, GRID_W, 2 * GRID_W), lambda h: (h, 0, 0, 0)),
        out_shape=jax.ShapeDtypeStruct((n_heads, nrr, GRID_W, 2 * GRID_W), F32),
        compiler_params=_params(("arbitrary",)),
        name="na_bias",
    )(rpb.reshape(-1))


NA_QROWS = 4
NA_KROWS = 12


def _na_kernel(q_ref, k0_ref, k1_ref, k2_ref, v0_ref, v1_ref, v2_ref, kc_ref, vc_ref, tz_ref, o_ref,
               *, hd, n_grid_rows):
    scale = hd ** -0.5
    blk = pl.program_id(1)
    kstart = jnp.clip(blk * NA_QROWS - NA_ROWS // 2, 0, n_grid_rows - NA_KROWS)
    lane = lax.broadcasted_iota(I32, (GRID_W, 2 * GRID_W), 1)
    n_rel = tz_ref.shape[1]
    heads = range(q_ref.shape[1] // hd)
    sls = [slice(hh * hd, (hh + 1) * hd) for hh in heads]
    rel, pen = [], []
    for rq_l in range(NA_QROWS):
        rq = blk * NA_QROWS + rq_l
        r_start = jnp.clip(rq - NA_ROWS // 2, 0, n_grid_rows - NA_ROWS)
        rel.append([])
        pen.append([])
        for m in range(NA_KROWS // 2):
            rk = kstart + 2 * m
            rel[-1].append(jnp.clip(rk - rq + (NA_ROWS - 1), 0, n_rel - 1))
            ok0 = (rk >= r_start) & (rk < r_start + NA_ROWS)
            ok1 = (rk + 1 >= r_start) & (rk + 1 < r_start + NA_ROWS)
            pen[-1].append(jnp.where(lane < GRID_W, jnp.where(ok0, 0.0, NEG_BIG), jnp.where(ok1, 0.0, NEG_BIG)))
    bias = [jnp.concatenate([jnp.concatenate([tz_ref[hh, pl.ds(rel[r][m], 1)][0] + pen[r][m]
                                              for m in range(NA_KROWS // 2)], axis=-1)
                             for r in range(NA_QROWS)], axis=0) for hh in heads]
    q = [q_ref[:, sl] * scale for sl in sls]
    k_loc = [jnp.concatenate([k0_ref[:, sl], k1_ref[:, sl], k2_ref[:, sl]], axis=0) for sl in sls]
    v_loc = [jnp.concatenate([v0_ref[:, sl], v1_ref[:, sl], v2_ref[:, sl]], axis=0).astype(BF16) for sl in sls]
    v_ctx = [vc_ref[0, :, sl].astype(BF16) for sl in sls]
    s_loc = [_bdot_nt(q[h], k_loc[h]) + bias[h] for h in heads]
    s_ctx = [_bdot_nt(q[h], kc_ref[0, :, sls[h]]) for h in heads]
    m = [jnp.maximum(s_loc[h].max(axis=-1, keepdims=True), s_ctx[h].max(axis=-1, keepdims=True)) for h in heads]
    e_loc = [jnp.exp(s_loc[h] - m[h]).astype(BF16) for h in heads]
    e_ctx = [jnp.exp(s_ctx[h] - m[h]).astype(BF16) for h in heads]
    pv = [jnp.dot(e_loc[h], jnp.concatenate([v_loc[h], jnp.ones_like(v_loc[h])], axis=-1),
                  preferred_element_type=F32)
          + jnp.dot(e_ctx[h], jnp.concatenate([v_ctx[h], jnp.ones_like(v_ctx[h])], axis=-1),
                    preferred_element_type=F32) for h in heads]
    for h in heads:
        o_ref[:, sls[h]] = pv[h][:, :hd] / pv[h][:, hd:]


def _na_call(proj, row0, n_seq, t, cache_k, cache_v, tz):
    hd, cw = 64, 128
    d = proj.shape[1] // 3
    ncb = d // cw
    tq = NA_QROWS * GRID_W
    nq = t // tq
    n_grid_rows = t // GRID_W
    t_ctx = cache_k.shape[1]
    qblk0 = row0 // tq

    def kv_map(which, j):
        def index_map(b, i, c):
            ks = jnp.clip(i * NA_QROWS - NA_ROWS // 2, 0, n_grid_rows - NA_KROWS) // NA_QROWS
            return (qblk0 + b * nq + ks + j, which * ncb + c)
        return index_map

    body = functools.partial(_na_kernel, hd=hd, n_grid_rows=n_grid_rows)
    return pl.pallas_call(
        body,
        grid=(n_seq, nq, ncb),
        in_specs=[pl.BlockSpec((tq, cw), lambda b, i, c: (qblk0 + b * nq + i, c))]
                 + [pl.BlockSpec((tq, cw), kv_map(1, j)) for j in range(3)]
                 + [pl.BlockSpec((tq, cw), kv_map(2, j)) for j in range(3)]
                 + [pl.BlockSpec((1, t_ctx, cw), lambda b, i, c: (b, 0, c)),
                    pl.BlockSpec((1, t_ctx, cw), lambda b, i, c: (b, 0, c)),
                    pl.BlockSpec((cw // hd,) + tz.shape[1:], lambda b, i, c: (c, 0, 0, 0))],
        out_specs=pl.BlockSpec((tq, cw), lambda b, i, c: (b * nq + i, c)),
        out_shape=jax.ShapeDtypeStruct((n_seq * t, d), F32),
        compiler_params=_params(("arbitrary", "arbitrary", "arbitrary")),
        name="nbr_attn",
    )(proj, proj, proj, proj, proj, proj, proj, cache_k, cache_v, tz)


def _hgrn_kernel(q_ref, v_ref, f_ref, bf_ref, lb_ref, s0_ref, o_ref, s_ref, st_scr,
                 *, reverse, n_heads, dk, has_s0, n_blk):
    c = pl.program_id(1)
    L = HGRN_CHUNK
    tb = q_ref.shape[0]

    @pl.when(c == 0)
    def _():
        for h in range(n_heads):
            if has_s0:
                st_scr[h] = s0_ref[0, 0, h].T
            else:
                st_scr[h] = jnp.zeros_like(st_scr[h])

    row = lax.broadcasted_iota(I32, (L, L), 0)
    col = lax.broadcasted_iota(I32, (L, L), 1)
    tri = jnp.where((col >= row) if reverse else (col <= row), 1.0, 0.0).astype(F32)
    eye = jnp.where(row == col, 1.0, 0.0).astype(F32)
    halves = [L >> (i + 1) for i in range(L.bit_length() - 1)]
    same_pair = [jnp.where((row // (2 * hf)) == (col // (2 * hf)), 1.0, 0.0).astype(F32) for hf in halves]
    rio = lax.broadcasted_iota(I32, (L, q_ref.shape[1]), 0)
    is_query = [jnp.where(((rio & hf) == 0) if reverse else ((rio & hf) != 0), 1.0, 0.0).astype(F32)
                for hf in halves]
    r8 = lax.broadcasted_iota(I32, (ROW_SUBLANES, 1), 0)
    lb = lb_ref[0]
    bf = bf_ref[0]
    n_chunks = tb // L

    def boundary_rows(cum, hf):
        blk = 2 * hf
        off = hf if reverse else hf - 1
        width = cum.shape[1]
        if blk >= ROW_SUBLANES:
            return jnp.concatenate([jnp.broadcast_to(cum[a + off:a + off + 1, :], (blk, width))
                                    for a in range(0, L, blk)], axis=0)
        groups = []
        for g in range(0, L, ROW_SUBLANES):
            ref = jnp.broadcast_to(cum[g + off:g + off + 1, :], (ROW_SUBLANES, width))
            for a in range(blk, ROW_SUBLANES, blk):
                ref = jnp.where(r8 >= a, jnp.broadcast_to(cum[g + a + off:g + a + off + 1, :],
                                                          (ROW_SUBLANES, width)), ref)
            groups.append(ref)
        return jnp.concatenate(groups, axis=0)

    def chunk_step(jj, carry):
        jc = (n_chunks - 1 - jj) if reverse else jj
        rs = pl.ds(pl.multiple_of(jc * L, L), L)
        q = _silu(q_ref[rs, :]) * (dk ** -0.5)
        v = v_ref[rs, :]
        f = lb + (1.0 - lb) * _sigmoid(f_ref[rs, :] + bf)
        logf = jnp.log(f)
        kk = 1.0 - f
        cum = _fdot(tri, logf)
        end = cum[0:1, :] if reverse else cum[L - 1:L, :]
        qd = q * jnp.exp(cum)
        kd = kk * jnp.exp(end - cum)
        e_end = jnp.exp(end)
        q_lv, k_lv = [], []
        for lv, hf in enumerate(halves):
            isq = is_query[lv]
            decay = jnp.exp((2.0 * isq - 1.0) * (cum - boundary_rows(cum, hf)))
            dq = decay * isq
            q_lv.append((q * dq).astype(BF16))
            k_lv.append((kk * (decay - dq)).astype(BF16))
        qk_diag = q * kk
        outs = []
        for h in range(n_heads):
            hs = slice(h * dk, (h + 1) * dk)
            attn = eye * jnp.sum(qk_diag[:, hs], axis=-1, keepdims=True)
            for lv in range(len(halves)):
                attn = attn + same_pair[lv] * _bdot_nt(q_lv[lv][:, hs], k_lv[lv][:, hs])
            outs.append(_bdot_nt(qd[:, hs], st_scr[h]) + _bdot(attn, v[:, hs]))
        o_ref[rs, :] = jnp.concatenate(outs, axis=-1)
        for h in range(n_heads):
            hs = slice(h * dk, (h + 1) * dk)
            st_scr[h] = st_scr[h] * e_end[:, hs] + _bdot_tn(v[:, hs], kd[:, hs])
        return carry

    lax.fori_loop(0, n_chunks, chunk_step, 0)

    @pl.when(c == n_blk - 1)
    def _():
        for h in range(n_heads):
            s_ref[0, 0, h] = st_scr[h].T


def _hgrn_call(proj, row0, n_seq, t, b_f, lb, s0, reverse):
    n_heads, dk = 8, 128
    d = n_heads * dk
    tb = 256
    n_blk = t // tb
    blk0 = row0 // tb
    di = 1 if reverse else 0
    has_s0 = s0 is not None
    if not has_s0:
        s0 = jnp.zeros((1, 2, n_heads, dk, dk), F32)

    def tok(b, c):
        return blk0 + b * n_blk + ((n_blk - 1 - c) if reverse else c)

    body = functools.partial(_hgrn_kernel, reverse=reverse, n_heads=n_heads, dk=dk,
                             has_s0=has_s0, n_blk=n_blk)
    return pl.pallas_call(
        body,
        grid=(n_seq, n_blk),
        in_specs=[pl.BlockSpec((tb, d), lambda b, c: (tok(b, c), 0)),
                  pl.BlockSpec((tb, d), lambda b, c: (tok(b, c), 1)),
                  pl.BlockSpec((tb, d), lambda b, c: (tok(b, c), 2 + di)),
                  pl.BlockSpec((1, 1, d), lambda b, c: (di, 0, 0)),
                  pl.BlockSpec((1, 1, d), lambda b, c: (di, 0, 0)),
                  pl.BlockSpec((1, 1, n_heads, dk, dk),
                               (lambda b, c: (b, di, 0, 0, 0)) if has_s0 else (lambda b, c: (0, 0, 0, 0, 0)))],
        out_specs=[pl.BlockSpec((tb, d), lambda b, c: (tok(b, c) - blk0, 0)),
                   pl.BlockSpec((1, 1, n_heads, dk, dk), lambda b, c: (b, 0, 0, 0, 0))],
        out_shape=[jax.ShapeDtypeStruct((n_seq * t, d), F32),
                   jax.ShapeDtypeStruct((n_seq, 1, n_heads, dk, dk), F32)],
        scratch_shapes=[pltpu.VMEM((n_heads, dk, dk), F32)],
        compiler_params=_params(("arbitrary", "arbitrary")),
        name="hgrn2_bw" if reverse else "hgrn2_fw",
    )(proj, proj, proj, b_f.reshape(2, 1, d), lb.reshape(2, 1, d), s0)


def _log_sigmoid(x):
    return jnp.minimum(x, 0.0) - jnp.log(1.0 + jnp.exp(-jnp.abs(x)))


def _mlstm_kernel(q_ref, k_ref, v_ref, g_ref, bg_ref, c0_ref, n0_ref, m0_ref,
                  o_ref, c_out, n_out, m_out, c_scr, n_scr, m_scr,
                  *, reverse, n_heads, dqk, dv, has_state, n_blk):
    c = pl.program_id(1)
    L = q_ref.shape[0]
    i_off = 2 * n_heads if reverse else 0
    f_off = i_off + n_heads

    @pl.when(c == 0)
    def _():
        if has_state:
            c_scr[...] = c0_ref[0, 0]
            n_scr[...] = n0_ref[0, 0]
            m_scr[...] = m0_ref[0, 0]
        else:
            c_scr[...] = jnp.zeros_like(c_scr)
            n_scr[...] = jnp.zeros_like(n_scr)
            m_scr[...] = jnp.zeros_like(m_scr)

    gates = GATE_SOFTCAP * jnp.tanh((g_ref[...] + bg_ref[...]) / GATE_SOFTCAP)
    logf = _log_sigmoid(gates)
    row = lax.broadcasted_iota(I32, (L, L), 0)
    col = lax.broadcasted_iota(I32, (L, L), 1)
    causal = (col >= row) if reverse else (col <= row)
    tri = jnp.where(causal, 1.0, 0.0).astype(F32)
    cum = _fdot(tri, logf)
    cum_t = cum.T
    gates_t = gates.T
    e_row = 0 if reverse else L - 1
    m_all = m_scr[...]
    c_prev = [c_scr[h] for h in range(n_heads)]
    n_prev = [n_scr[h] for h in range(n_heads)]
    lane = lax.broadcasted_iota(I32, m_all.shape, 1)
    heads = range(n_heads)
    qs = [q_ref[:, h * dqk:(h + 1) * dqk] * (dqk ** -0.5) for h in heads]
    ks_ = [k_ref[:, h * dqk:(h + 1) * dqk] for h in heads]
    vs = [v_ref[:, h * dv:(h + 1) * dv] for h in heads]
    cum_c = [cum[:, f_off + h:f_off + h + 1] for h in heads]
    cum_r = [cum_t[f_off + h:f_off + h + 1, :] for h in heads]
    i_c = [gates[:, i_off + h:i_off + h + 1] for h in heads]
    i_r = [gates_t[i_off + h:i_off + h + 1, :] for h in heads]
    m_prev = [m_all[0:1, h:h + 1] for h in heads]
    d = [jnp.where(causal, cum_c[h] - cum_r[h] + i_r[h], -jnp.inf) for h in heads]
    m_inter = [cum_c[h] + m_prev[h] for h in heads]
    m_t = [jnp.maximum(m_inter[h], d[h].max(axis=-1, keepdims=True)) for h in heads]
    scores = [_bdot_nt(qs[h], ks_[h]) for h in heads]
    inter = [_bdot(qs[h], c_prev[h]) for h in heads]
    qn = [jnp.sum(qs[h] * n_prev[h], axis=-1, keepdims=True) for h in heads]
    w_inter = [jnp.exp(m_inter[h] - m_t[h]) for h in heads]
    qk = [scores[h] * jnp.exp(d[h] - m_t[h]) for h in heads]
    num = [w_inter[h] * inter[h] + _bdot(qk[h], vs[h]) for h in heads]
    den = [w_inter[h] * qn[h] + qk[h].sum(axis=-1, keepdims=True) for h in heads]
    outs = [num[h] / jnp.maximum(jnp.abs(den[h]), jnp.exp(-m_t[h])) for h in heads]
    end = [cum_c[h][e_row:e_row + 1, :] for h in heads]
    g_end_r = [end[h] - cum_r[h] + i_r[h] for h in heads]
    g_end_c = [end[h] - cum_c[h] + i_c[h] for h in heads]
    m_new = [jnp.maximum(end[h] + m_prev[h], g_end_r[h].max(axis=-1, keepdims=True)) for h in heads]
    w_old = [jnp.exp(end[h] + m_prev[h] - m_new[h]) for h in heads]
    kd = [ks_[h] * jnp.exp(g_end_c[h] - m_new[h]) for h in heads]
    c_new = [w_old[h] * c_prev[h] + _bdot_tn(kd[h], vs[h]) for h in heads]
    n_new = [w_old[h] * n_prev[h] + kd[h].sum(axis=0, keepdims=True) for h in heads]
    m_next = m_all
    for h in heads:
        m_next = jnp.where(lane == h, m_new[h], m_next)
    o_ref[...] = jnp.concatenate(outs, axis=-1)
    for h in range(n_heads):
        c_scr[h] = c_new[h]
        n_scr[h] = n_new[h]
    m_scr[...] = m_next

    @pl.when(c == n_blk - 1)
    def _():
        c_out[0, 0] = c_scr[...]
        n_out[0, 0] = n_scr[...]
        m_out[0, 0] = m_scr[...]


def _mlstm_call(proj, row0, n_seq, t, b_gates_pad, state, reverse):
    n_heads, dqk, dv = 8, 64, 128
    wq, wv = n_heads * dqk, n_heads * dv
    L = 256
    n_blk = t // L
    blk0 = row0 // L
    di = 1 if reverse else 0
    has_state = state is not None
    if has_state:
        c0, n0, m0 = state
        smap = lambda b, c: (b, di, 0, 0, 0)
        mmap = lambda b, c: (b, di, 0, 0)
    else:
        c0 = jnp.zeros((1, 1, n_heads, dqk, dv), F32)
        n0 = jnp.zeros((1, 1, n_heads, 1, dqk), F32)
        m0 = jnp.zeros((1, 1, 1, 128), F32)
        smap = lambda b, c: (0, 0, 0, 0, 0)
        mmap = lambda b, c: (0, 0, 0, 0)

    def tok(b, c):
        return blk0 + b * n_blk + ((n_blk - 1 - c) if reverse else c)

    body = functools.partial(_mlstm_kernel, reverse=reverse, n_heads=n_heads, dqk=dqk, dv=dv,
                             has_state=has_state, n_blk=n_blk)
    gate_cb = (2 * wq + 2 * wv) // 128
    return pl.pallas_call(
        body,
        grid=(n_seq, n_blk),
        in_specs=[pl.BlockSpec((L, wq), lambda b, c: (tok(b, c), 0)),
                  pl.BlockSpec((L, wq), lambda b, c: (tok(b, c), 1)),
                  pl.BlockSpec((L, wv), lambda b, c: (tok(b, c), (2 * wq) // wv)),
                  pl.BlockSpec((L, 128), lambda b, c: (tok(b, c), gate_cb)),
                  pl.BlockSpec((1, 128), lambda b, c: (0, 0)),
                  pl.BlockSpec((1, 1, n_heads, dqk, dv), smap),
                  pl.BlockSpec((1, 1, n_heads, 1, dqk), smap),
                  pl.BlockSpec((1, 1, 1, 128), mmap)],
        out_specs=[pl.BlockSpec((L, wv), lambda b, c: (tok(b, c) - blk0, 0)),
                   pl.BlockSpec((1, 1, n_heads, dqk, dv), lambda b, c: (b, 0, 0, 0, 0)),
                   pl.BlockSpec((1, 1, n_heads, 1, dqk), lambda b, c: (b, 0, 0, 0, 0)),
                   pl.BlockSpec((1, 1, 1, 128), lambda b, c: (b, 0, 0, 0))],
        out_shape=[jax.ShapeDtypeStruct((n_seq * t, wv), F32),
                   jax.ShapeDtypeStruct((n_seq, 1, n_heads, dqk, dv), F32),
                   jax.ShapeDtypeStruct((n_seq, 1, n_heads, 1, dqk), F32),
                   jax.ShapeDtypeStruct((n_seq, 1, 1, 128), F32)],
        scratch_shapes=[pltpu.VMEM((n_heads, dqk, dv), F32),
                        pltpu.VMEM((n_heads, 1, dqk), F32),
                        pltpu.VMEM((1, 128), F32)],
        compiler_params=_params(("arbitrary", "arbitrary")),
        name="mlstm_bw" if reverse else "mlstm_fw",
    )(proj, proj, proj, proj, b_gates_pad, c0, n0, m0)


def _moe_input(x_ref, m_ref, g_ref):
    return _rms(x_ref[...], g_ref[...]) * (1.0 + m_ref[0, 4:5, :]) + m_ref[0, 3:4, :]


def _route_kernel(x_ref, m_ref, g_ref, wr_ref, br_ref, idx_ref, gate_ref, rank_ref, cnt_ref, carry_scr,
                  *, n_experts):
    i = pl.program_id(0)
    tm = x_ref.shape[0]

    @pl.when(i == 0)
    def _():
        carry_scr[...] = jnp.zeros_like(carry_scr)

    h = _moe_input(x_ref, m_ref, g_ref)
    logits = lax.dot_general(wr_ref[...], h, (((1,), (1,)), ((), ())), preferred_element_type=F32,
                             precision=lax.Precision.HIGHEST) + br_ref[...]
    e_io = lax.broadcasted_iota(I32, (n_experts, tm), 0).astype(F32)
    work = logits
    vals, idxs = [], []
    chosen = jnp.zeros((n_experts, tm), F32)
    for _ in range(TOP_K):
        mx = work.max(axis=0, keepdims=True)
        ix = jnp.min(jnp.where(work == mx, e_io, float(n_experts)), axis=0, keepdims=True)
        hit = e_io == ix
        vals.append(mx)
        idxs.append(ix)
        chosen = jnp.where(hit, 1.0, chosen)
        work = jnp.where(hit, -jnp.inf, work)
    es = [jnp.exp(v - vals[0]) for v in vals]
    den = es[0] + es[1] + es[2] + es[3]
    srow = lax.broadcasted_iota(I32, (tm, tm), 0)
    scol = lax.broadcasted_iota(I32, (tm, tm), 1)
    before = jnp.where(srow < scol, 1.0, 0.0).astype(BF16)
    pos = jnp.dot(chosen.astype(BF16), before, preferred_element_type=F32) + carry_scr[...]
    ranks = [jnp.sum(jnp.where(e_io == ix, pos, 0.0), axis=0, keepdims=True) for ix in idxs]
    carry_scr[...] = carry_scr[...] + chosen.sum(axis=1, keepdims=True)
    idx_ref[...] = jnp.concatenate(idxs, axis=0).astype(I32)
    gate_ref[...] = jnp.concatenate([e / den for e in es], axis=0)
    rank_ref[...] = jnp.concatenate(ranks, axis=0).astype(I32)
    cnt_ref[...] = jnp.broadcast_to(carry_scr[...], cnt_ref.shape).astype(I32)


def _route_call(x, modsel, g, w_router, b_router):
    n, d = x.shape
    n_experts = w_router.shape[1]
    tm = TOKEN_TILE
    body = functools.partial(_route_kernel, n_experts=n_experts)
    return pl.pallas_call(
        body,
        grid=(n // tm,),
        in_specs=[pl.BlockSpec((tm, d), lambda i: (i, 0)),
                  pl.BlockSpec((1, 6, d), lambda i: (i, 0, 0)),
                  pl.BlockSpec((1, d), lambda i: (0, 0)),
                  pl.BlockSpec((n_experts, d), lambda i: (0, 0)),
                  pl.BlockSpec((n_experts, 1), lambda i: (0, 0))],
        out_specs=[pl.BlockSpec((TOP_K, tm), lambda i: (0, i)),
                   pl.BlockSpec((TOP_K, tm), lambda i: (0, i)),
                   pl.BlockSpec((TOP_K, tm), lambda i: (0, i)),
                   pl.BlockSpec((n_experts, 128), lambda i: (0, 0))],
        out_shape=[jax.ShapeDtypeStruct((TOP_K, n), I32),
                   jax.ShapeDtypeStruct((TOP_K, n), F32),
                   jax.ShapeDtypeStruct((TOP_K, n), I32),
                   jax.ShapeDtypeStruct((n_experts, 128), I32)],
        scratch_shapes=[pltpu.VMEM((n_experts, 1), F32)],
        compiler_params=_params(("arbitrary",)),
        name="moe_route",
    )(x, modsel, g.reshape(1, d), w_router.T, b_router.reshape(n_experts, 1))


def _slot_kernel(cnt_ref, idx_ref, rank_ref, dest_ref, binfo_ref, pad_ref, nused_ref, *, n_experts, n_blocks):
    cnt = cnt_ref[:, 0:1].astype(F32)
    padded = jnp.ceil(cnt * (1.0 / MOE_BLOCK)) * MOE_BLOCK
    er = lax.broadcasted_iota(I32, (n_experts, n_experts), 0)
    ec = lax.broadcasted_iota(I32, (n_experts, n_experts), 1)
    start_row = jnp.sum(jnp.where(er < ec, padded, 0.0), axis=0, keepdims=True)
    start_col = jnp.sum(jnp.where(er == ec, start_row, 0.0), axis=1, keepdims=True)
    end_col = start_col + padded
    idx = idx_ref[...]
    e_io = lax.broadcasted_iota(I32, (n_experts,) + idx.shape[1:], 0)
    rows = []
    for k in range(TOP_K):
        hit = e_io == idx[k:k + 1, :]
        rows.append(jnp.sum(jnp.where(hit, start_col, 0.0), axis=0, keepdims=True))
    dest_ref[...] = jnp.concatenate(rows, axis=0).astype(I32) + rank_ref[...]
    blk_start = (lax.broadcasted_iota(I32, (n_experts, n_blocks), 1) * MOE_BLOCK).astype(F32)
    n_done = jnp.sum(jnp.where(end_col <= blk_start, 1.0, 0.0), axis=0, keepdims=True)
    bexp = jnp.minimum(n_done, n_experts - 1.0)
    used_row = jnp.sum(jnp.where(er == ec, jnp.where(cnt > 0.0, 1.0, 0.0), 0.0), axis=0, keepdims=True)
    ecf = ec.astype(F32)
    next_col = jnp.min(jnp.where((ec > er) & (used_row > 0.0), ecf, float(n_experts)), axis=1, keepdims=True)
    ord_col = jnp.sum(jnp.where(ec < er, used_row, 0.0), axis=1, keepdims=True)
    par_col = ord_col - 2.0 * jnp.floor(ord_col * 0.5)
    mine = lax.broadcasted_iota(I32, (n_experts, n_blocks), 0).astype(F32) == bexp
    bnext = jnp.sum(jnp.where(mine, next_col, 0.0), axis=0, keepdims=True)
    bslot = jnp.sum(jnp.where(mine, par_col, 0.0), axis=0, keepdims=True)
    binfo_ref[...] = jnp.concatenate([bexp, bnext, bslot], axis=0).astype(I32)
    n_used = jnp.sum(padded, axis=0, keepdims=True) * (1.0 / MOE_BLOCK)
    nused_ref[...] = n_used.astype(I32)
    cnt_row = jnp.sum(jnp.where(er == ec, cnt, 0.0), axis=0, keepdims=True)
    padded_row = jnp.sum(jnp.where(er == ec, padded, 0.0), axis=0, keepdims=True)
    pad_ref[...] = jnp.concatenate([start_row + cnt_row, padded_row - cnt_row,
                                    jnp.broadcast_to(n_used, cnt_row.shape)], axis=0).astype(I32)


def _slot_call(counts, idx_t, rank_t, n_blocks):
    n_experts = counts.shape[0]
    n = idx_t.shape[1]
    tn = min(2048, n)
    body = functools.partial(_slot_kernel, n_experts=n_experts, n_blocks=n_blocks)
    return pl.pallas_call(
        body,
        grid=(n // tn,),
        in_specs=[pl.BlockSpec((n_experts, 128), lambda i: (0, 0)),
                  pl.BlockSpec((TOP_K, tn), lambda i: (0, i)),
                  pl.BlockSpec((TOP_K, tn), lambda i: (0, i))],
        out_specs=[pl.BlockSpec((TOP_K, tn), lambda i: (0, i)),
                   pl.BlockSpec((3, n_blocks), lambda i: (0, 0)),
                   pl.BlockSpec((3, n_experts), lambda i: (0, 0)),
                   pl.BlockSpec((1, 1), lambda i: (0, 0))],
        out_shape=[jax.ShapeDtypeStruct((TOP_K, n), I32),
                   jax.ShapeDtypeStruct((3, n_blocks), I32),
                   jax.ShapeDtypeStruct((3, n_experts), I32),
                   jax.ShapeDtypeStruct((1, 1), I32)],
        compiler_params=_params(("arbitrary",)),
        name="moe_slots",
    )(counts, idx_t, rank_t)


DMA_ISSUE_UNROLL = 8


def _to_row_tiles(ref, base, x):
    rows = x.shape[0]
    for c in range(ROW_SUBLANES):
        ref[pl.ds(base * ROW_SUBLANES + c, rows, stride=ROW_SUBLANES), :] = x[:, c * LANES:(c + 1) * LANES]


def _from_row_tiles(ref, base, rows, c):
    return ref[pl.ds(base * ROW_SUBLANES + c, rows, stride=ROW_SUBLANES), :]


def _row_tile(ref, r):
    return ref.at[pl.ds(pl.multiple_of(r * ROW_SUBLANES, ROW_SUBLANES), ROW_SUBLANES)]


def _zero_fill_padding(pad_ref, xs_ref, z_scr, sem):
    z_scr[...] = jnp.zeros_like(z_scr)
    n_experts = pad_ref.shape[1]
    bits = range(MOE_BLOCK.bit_length() - 2, -1, -1)

    def pieces(e):
        off, length = pad_ref[0, e], pad_ref[1, e]
        for bit in bits:
            size = 1 << bit
            done = (length >> (bit + 1)) << (bit + 1)
            copy = pltpu.make_async_copy(z_scr.at[pl.ds(0, size * ROW_SUBLANES)],
                                         xs_ref.at[pl.ds(pl.multiple_of((off + done) * ROW_SUBLANES, ROW_SUBLANES),
                                                         size * ROW_SUBLANES)], sem)
            yield (length & size) != 0, copy

    def tail_blocks():
        n_blocks = xs_ref.shape[0] // (MOE_BLOCK * ROW_SUBLANES)
        for b in range(n_blocks - n_experts, n_blocks):
            copy = pltpu.make_async_copy(z_scr, xs_ref.at[pl.ds(b * MOE_BLOCK * ROW_SUBLANES,
                                                                MOE_BLOCK * ROW_SUBLANES)], sem)
            yield b >= pad_ref[2, 0], copy

    def all_copies():
        for e in range(n_experts):
            yield from pieces(e)
        yield from tail_blocks()

    for needed, copy in all_copies():
        pl.when(needed)(copy.start)
    for needed, copy in all_copies():
        pl.when(needed)(copy.wait)


def _dispatch_kernel(pad_ref, dest_ref, x_ref, m_ref, g_ref, xs_ref, h_scr, z_scr, sem, *, n_tiles):
    i = pl.program_id(0)
    tm = x_ref.shape[0]

    @pl.when(i == 0)
    def _():
        _zero_fill_padding(pad_ref, xs_ref, z_scr, sem.at[2])

    slot = lax.rem(i, 2)
    src = h_scr.at[slot]
    _to_row_tiles(src, 0, _moe_input(x_ref, m_ref, g_ref))

    def start_row(r, carry):
        for k in range(TOP_K):
            pltpu.make_async_copy(_row_tile(src, r), _row_tile(xs_ref, dest_ref[0, 0, k * tm + r]),
                                  sem.at[slot]).start(priority=k % 2)
        return carry

    lax.fori_loop(0, tm, start_row, 0, unroll=DMA_ISSUE_UNROLL // TOP_K)

    def wait_tile(s):
        for _ in range(TOP_K):
            pltpu.make_async_copy(h_scr.at[s], xs_ref.at[pl.ds(0, tm * ROW_SUBLANES)], sem.at[s]).wait()

    @pl.when(i >= 1)
    def _():
        wait_tile(1 - slot)

    @pl.when(i == n_tiles - 1)
    def _():
        wait_tile(slot)


def _dispatch_call(pad_info, dest_tiles, x, modsel, g, n_slots):
    n, d = x.shape
    tm = TOKEN_TILE
    assert d == ROW_SUBLANES * LANES
    return pl.pallas_call(
        functools.partial(_dispatch_kernel, n_tiles=n // tm),
        grid=(n // tm,),
        in_specs=[pl.BlockSpec(memory_space=pltpu.SMEM),
                  pl.BlockSpec((1, 1, TOP_K * tm), lambda i: (i, 0, 0), memory_space=pltpu.SMEM),
                  pl.BlockSpec((tm, d), lambda i: (i, 0)),
                  pl.BlockSpec((1, 6, d), lambda i: (i, 0, 0)),
                  pl.BlockSpec((1, d), lambda i: (0, 0))],
        out_specs=pl.BlockSpec(memory_space=pl.ANY),
        out_shape=jax.ShapeDtypeStruct((n_slots * ROW_SUBLANES, LANES), F32),
        scratch_shapes=[pltpu.VMEM((2, tm * ROW_SUBLANES, LANES), F32),
                        pltpu.VMEM((MOE_BLOCK * ROW_SUBLANES, LANES), F32),
                        pltpu.SemaphoreType.DMA((3,))],
        compiler_params=_params(("arbitrary",)),
        name="moe_dispatch",
    )(pad_info, dest_tiles, x, modsel, g.reshape(1, d))


def _ffn_kernel(binfo_ref, nused_ref, xs_ref, wgu_hbm, bgu_ref, wdn_hbm, bdn_ref, ys_ref,
                wgu_f32, wdn_f32, wgu_scr, wdn_scr, sem, *, layer, n_experts, n_blocks):
    b = pl.program_id(0)
    d_ff = wdn_scr.shape[0]
    rows = xs_ref.shape[0] // ROW_SUBLANES

    def weight_copies(e, slot):
        return (pltpu.make_async_copy(wgu_hbm.at[layer, e], wgu_f32.at[slot], sem.at[slot]),
                pltpu.make_async_copy(wdn_hbm.at[layer, e], wdn_f32.at[slot], sem.at[slot]))

    @pl.when(b < nused_ref[0])
    def _():
        e = binfo_ref[b]
        prev = binfo_ref[jnp.maximum(b - 1, 0)]
        slot = binfo_ref[2 * n_blocks + b]

        @pl.when((b == 0) | (e != prev))
        def _():
            @pl.when(b == 0)
            def _():
                for cp in weight_copies(e, slot):
                    cp.start()

            nxt = binfo_ref[n_blocks + b]

            @pl.when(nxt < n_experts)
            def _():
                for cp in weight_copies(nxt, 1 - slot):
                    cp.start()

            for cp in weight_copies(e, slot):
                cp.wait()
            wgu_scr[...] = wgu_f32[slot].astype(BF16)
            wdn_scr[...] = wdn_f32[slot].astype(BF16)

        x = jnp.concatenate([_from_row_tiles(xs_ref, 0, rows, c).astype(BF16) for c in range(ROW_SUBLANES)],
                            axis=-1)
        gu = jnp.dot(x, wgu_scr[...], preferred_element_type=F32) + bgu_ref[0, 0]
        x_glu = jnp.minimum(gu[:, :d_ff], SWIGLU_LIMIT)
        x_lin = jnp.clip(gu[:, d_ff:], -SWIGLU_LIMIT, SWIGLU_LIMIT)
        hid = x_glu * _sigmoid(SWIGLU_ALPHA * x_glu) * (x_lin + 1.0)
        _to_row_tiles(ys_ref, 0, jnp.dot(hid.astype(BF16), wdn_scr[...], preferred_element_type=F32)
                      + bdn_ref[0, 0])

    @pl.when(b >= nused_ref[0])
    def _():
        ys_ref[...] = jnp.zeros_like(ys_ref)


def _ffn_call(layer, block_info, n_used, xs, w_gu, b_gu, w_dn, b_dn):
    depth, n_experts, d, d_ff2 = w_gu.shape
    d_ff = d_ff2 // 2
    blk_rows = MOE_BLOCK * ROW_SUBLANES
    n_blocks = xs.shape[0] // blk_rows

    def blk(b, nu):
        return jnp.maximum(jnp.minimum(b, nu[0] - 1), 0)

    grid_spec = pltpu.PrefetchScalarGridSpec(
        num_scalar_prefetch=2,
        grid=(n_blocks,),
        in_specs=[pl.BlockSpec((blk_rows, LANES), lambda b, bi, nu: (blk(b, nu), 0)),
                  pl.BlockSpec(memory_space=pl.ANY),
                  pl.BlockSpec((1, 1, 1, d_ff2), lambda b, bi, nu: (layer, bi[blk(b, nu)], 0, 0)),
                  pl.BlockSpec(memory_space=pl.ANY),
                  pl.BlockSpec((1, 1, 1, d), lambda b, bi, nu: (layer, bi[blk(b, nu)], 0, 0))],
        out_specs=pl.BlockSpec((blk_rows, LANES), lambda b, bi, nu: (b, 0)),
        scratch_shapes=[pltpu.VMEM((2, d, d_ff2), F32), pltpu.VMEM((2, d_ff, d), F32),
                        pltpu.VMEM((d, d_ff2), BF16), pltpu.VMEM((d_ff, d), BF16),
                        pltpu.SemaphoreType.DMA((2,))],
    )
    body = functools.partial(_ffn_kernel, layer=layer, n_experts=n_experts, n_blocks=n_blocks)
    return pl.pallas_call(
        body,
        grid_spec=grid_spec,
        out_shape=jax.ShapeDtypeStruct(xs.shape, F32),
        compiler_params=_params(("arbitrary",)),
        name="moe_ffn",
    )(block_info, n_used, xs, w_gu, b_gu.reshape(depth, n_experts, 1, d_ff2), w_dn,
      b_dn.reshape(depth, n_experts, 1, d))


def _combine_kernel(dest_ref, dest_next_ref, ys_ref, gate_ref, x_ref, m_ref, fg_ref, y_ref, buf, sem,
                    *, final_norm, n_tiles):
    i = pl.program_id(0)
    tm = x_ref.shape[0]
    rows = TOP_K * tm

    def start_gathers(d_ref, slot):
        def start_pair(p, carry):
            for u in range(2):
                j = 2 * p + u
                pltpu.make_async_copy(_row_tile(ys_ref, d_ref[0, 0, j]), _row_tile(buf.at[slot], j),
                                      sem.at[slot]).start(priority=u)
            return carry

        lax.fori_loop(0, rows // 2, start_pair, 0, unroll=DMA_ISSUE_UNROLL // 2)

    slot = lax.rem(i, 2)

    @pl.when(i == 0)
    def _():
        start_gathers(dest_ref, 0)

    @pl.when(i + 1 < n_tiles)
    def _():
        start_gathers(dest_next_ref, 1 - slot)

    pltpu.make_async_copy(ys_ref.at[pl.ds(0, rows * ROW_SUBLANES)], buf.at[slot], sem.at[slot]).wait()
    cur = buf.at[slot]
    chunks = []
    for c in range(ROW_SUBLANES):
        y = gate_ref[:, 0:1] * _from_row_tiles(cur, 0, tm, c)
        for k in range(1, TOP_K):
            y = y + gate_ref[:, k:k + 1] * _from_row_tiles(cur, k * tm, tm, c)
        chunks.append(y)
    out = x_ref[...] + m_ref[0, 5:6, :] * jnp.concatenate(chunks, axis=-1)
    if final_norm:
        out = _rms(out, fg_ref[...])
    y_ref[...] = out


def _combine_call(dest_tiles, ys, gates_nk, x, modsel, final_g, final_norm):
    n, d = x.shape
    tm = TOKEN_TILE
    n_tiles = n // tm
    body = functools.partial(_combine_kernel, final_norm=final_norm, n_tiles=n_tiles)
    return pl.pallas_call(
        body,
        grid=(n_tiles,),
        in_specs=[pl.BlockSpec((1, 1, TOP_K * tm), lambda i: (i, 0, 0), memory_space=pltpu.SMEM),
                  pl.BlockSpec((1, 1, TOP_K * tm), lambda i: (jnp.minimum(i + 1, n_tiles - 1), 0, 0),
                               memory_space=pltpu.SMEM),
                  pl.BlockSpec(memory_space=pl.ANY),
                  pl.BlockSpec((tm, TOP_K), lambda i: (i, 0)),
                  pl.BlockSpec((tm, d), lambda i: (i, 0)),
                  pl.BlockSpec((1, 6, d), lambda i: (i, 0, 0)),
                  pl.BlockSpec((1, d), lambda i: (0, 0))],
        out_specs=pl.BlockSpec((tm, d), lambda i: (i, 0)),
        out_shape=jax.ShapeDtypeStruct((n, d), F32),
        scratch_shapes=[pltpu.VMEM((2, TOP_K * tm * ROW_SUBLANES, LANES), F32), pltpu.SemaphoreType.DMA((2,))],
        compiler_params=_params(("arbitrary",)),
        name="moe_combine",
    )(dest_tiles, dest_tiles, ys, gates_nk, x, modsel, final_g.reshape(1, d))


def _moe_layer(layer, x, modsel, g2, w_router, b_router, w_gu, b_gu, w_dn, b_dn, final_g, final_norm):
    n, d = x.shape
    n_experts = w_router.shape[1]
    tm = TOKEN_TILE
    n_blocks = (n * TOP_K) // MOE_BLOCK + n_experts
    idx_t, gate_t, rank_t, counts = _route_call(x, modsel, g2, w_router, b_router)
    dest_t, block_info, pad_info, n_used = _slot_call(counts, idx_t, rank_t, n_blocks)
    dest_tiles = dest_t.reshape(TOP_K, n // tm, tm).transpose(1, 0, 2).reshape(n // tm, 1, TOP_K * tm)
    xs = _dispatch_call(pad_info, dest_tiles, x, modsel, g2, n_blocks * MOE_BLOCK)
    ys = _ffn_call(layer, block_info.reshape(3 * n_blocks), n_used.reshape(1), xs, w_gu, b_gu, w_dn, b_dn)
    return _combine_call(dest_tiles, ys, gate_t.T, x, modsel, final_g, final_norm)


def _rope_tables(t, hd):
    pos = np.arange(t)
    n_freq = hd // 4
    inv_freq = ROPE_THETA ** (-np.arange(n_freq, dtype=np.float32) / n_freq)
    ang = np.concatenate([(pos // GRID_W).astype(np.float32)[:, None] * inv_freq,
                          (pos % GRID_W).astype(np.float32)[:, None] * inv_freq], axis=-1)
    ang = jnp.asarray(ang, F32)
    cos, sin = jnp.cos(ang), jnp.sin(ang)
    return jnp.concatenate([cos, cos], axis=-1), jnp.concatenate([-sin, sin], axis=-1)


def kernel(x_prompt, x_sample, cache_k_a, cache_v_a, state_b, state_c_C, state_c_n, state_c_m, cache_k_d, cache_v_d, c, c_ctx, norm1_g, norm2_g, w_mod, b_mod, w_in_a, qnorm_a, knorm_a, w_out_a, w_in_b, b_f_b, lower_bounds_b, onorm_b, w_out_b, w_in_c, b_gates_c, onorm_c, w_out_c, w_in_d, rpb_d, w_out_d, w_router, b_router, w_gu, b_gu, w_dn, b_dn, final_g):
    n_ctx_seq, t_ctx, d = x_prompt.shape
    n_lat_seq, t_lat, _ = x_sample.shape
    depth = w_mod.shape[0]
    n_ctx = n_ctx_seq * t_ctx
    n_lat = n_lat_seq * t_lat
    n = n_ctx + n_lat
    tm = TOKEN_TILE
    assert t_ctx % tm == 0 and t_lat % tm == 0 and n_lat_seq + 1 <= 8

    lb_cum = jnp.cumsum(jax.nn.softmax(lower_bounds_b.astype(F32), axis=0), axis=0)
    lb_all = lb_cum - lb_cum[0]

    cond8 = jnp.zeros((8, d), F32).at[0].set(c_ctx).at[1:1 + n_lat_seq].set(c)
    mod = _mod_call(cond8, w_mod, b_mod)
    tile_row = np.concatenate([np.zeros(n_ctx // tm, np.int32),
                               1 + np.repeat(np.arange(n_lat_seq, dtype=np.int32), t_lat // tm)])

    x = jnp.concatenate([x_prompt.reshape(n_ctx, d), x_sample.reshape(n_lat, d)], axis=0)
    outs = {}
    for i in range(depth):
        kind = i % 4
        j = i // 4
        modsel = mod[i].reshape(8, 6, d)[tile_row]
        if kind == 0:
            proj = _inproj_call(x, modsel, norm1_g[i], w_in_a[j].astype(BF16))
            o_ctx, k_new = _gqa_ctx_call(proj, qnorm_a[j], knorm_a[j], n_ctx_seq, t_ctx)
            cosd, sind = _rope_tables(t_lat, 128)
            o_lat = _gqa_lat_call(proj, n_ctx, n_lat_seq, t_lat,
                                  cache_k_a[:, j].reshape(n_lat_seq, -1, 256),
                                  cache_v_a[:, j].reshape(n_lat_seq, -1, 256), cosd, sind,
                                  qnorm_a[j], knorm_a[j])
            outs["k_a"] = k_new.reshape(n_ctx_seq, 1, t_ctx, 2, 128)
            outs["v_a"] = proj[:n_ctx, 1280:1536].reshape(n_ctx_seq, 1, t_ctx, 2, 128)
            x = _outproj_call("plain", [(o_ctx, o_lat)], None, w_out_a[j].astype(BF16), x, modsel)
        elif kind == 1:
            proj = _inproj_call(x, modsel, norm1_g[i], w_in_b[j].astype(BF16))
            o_dirs, s_dirs = [], []
            for reverse in (False, True):
                oc, sc = _hgrn_call(proj, 0, n_ctx_seq, t_ctx, b_f_b[j], lb_all[i], None, reverse)
                ol, _ = _hgrn_call(proj, n_ctx, n_lat_seq, t_lat, b_f_b[j], lb_all[i], state_b[:, j], reverse)
                o_dirs.append((oc, ol))
                s_dirs.append(sc)
            outs["s_b"] = jnp.concatenate(s_dirs, axis=1)[:, None]
            x = _outproj_call("hgrn", o_dirs, (proj, 4, onorm_b[j].reshape(1, 128)),
                              w_out_b[j].astype(BF16), x, modsel)
        elif kind == 2:
            w_c = jnp.pad(w_in_c[j], ((0, 0), (0, 128 - 32))).astype(BF16)
            bg = jnp.pad(b_gates_c[j].reshape(1, 32), ((0, 0), (0, 128 - 32)))
            proj = _inproj_call(x, modsel, norm1_g[i], w_c)
            state = (state_c_C[:, j], state_c_n[:, j][:, :, :, None, :],
                     jnp.pad(state_c_m[:, j], ((0, 0), (0, 0), (0, 120)))[:, :, None, :])
            o_dirs, st = [], []
            for reverse in (False, True):
                oc, cc, nc, mc = _mlstm_call(proj, 0, n_ctx_seq, t_ctx, bg, None, reverse)
                ol, _, _, _ = _mlstm_call(proj, n_ctx, n_lat_seq, t_lat, bg, state, reverse)
                o_dirs.append((oc, ol))
                st.append((cc, nc, mc))
            outs["c_C"] = jnp.concatenate([st[0][0], st[1][0]], axis=1)[:, None]
            outs["c_n"] = jnp.concatenate([st[0][1], st[1][1]], axis=1)[:, None, :, :, 0, :]
            outs["c_m"] = jnp.concatenate([st[0][2], st[1][2]], axis=1)[:, None, :, 0, :8]
            x = _outproj_call("mlstm", o_dirs, (proj, 2, onorm_c[j].reshape(1, 128)),
                              w_out_c[j].astype(BF16), x, modsel)
        else:
            proj = _inproj_call(x, modsel, norm1_g[i], w_in_d[j].astype(BF16))
            o_ctx = _mha_ctx_call(proj, n_ctx_seq, t_ctx)
            tz = _na_bias_call(rpb_d[j])
            o_lat = _na_call(proj, n_ctx, n_lat_seq, t_lat,
                             cache_k_d[:, j].reshape(n_lat_seq, -1, d),
                             cache_v_d[:, j].reshape(n_lat_seq, -1, d), tz)
            outs["k_d"] = proj[:n_ctx, d:2 * d].reshape(n_ctx_seq, 1, t_ctx, 16, 64)
            outs["v_d"] = proj[:n_ctx, 2 * d:3 * d].reshape(n_ctx_seq, 1, t_ctx, 16, 64)
            x = _outproj_call("plain", [(o_ctx, o_lat)], None, w_out_d[j].astype(BF16), x, modsel)
        x = _moe_layer(i, x, modsel, norm2_g[i], w_router[i], b_router[i], w_gu, b_gu, w_dn, b_dn,
                       final_g, final_norm=(i == depth - 1))

    y_prompt = x[:n_ctx].reshape(n_ctx_seq, t_ctx, d)
    y_sample = x[n_ctx:].reshape(n_lat_seq, t_lat, d)
    return (y_prompt, y_sample, outs["k_a"], outs["v_a"], outs["s_b"], outs["c_C"], outs["c_n"], outs["c_m"],
            outs["k_d"], outs["v_d"])
```

```python
import functools

import numpy as np
import jax
import jax.numpy as jnp
from jax import lax
from jax.experimental import pallas as pl
from jax.experimental.pallas import tpu as pltpu

F32 = jnp.float32
BF16 = jnp.bfloat16
I32 = jnp.int32

NORM_EPS = 1e-6
GRID_W = 64
ROPE_THETA = 10000.0
TOP_K = 4
HGRN_CHUNK = 128
GATE_SOFTCAP = 15.0
NA_ROWS = 8
NA_COLS = 16
SWIGLU_ALPHA = 1.702
SWIGLU_LIMIT = 7.0
NEG_BIG = -1e30

LANES = 128
ROW_SUBLANES = 8
TOKEN_TILE = 256
MOE_BLOCK = 256
ATTN_HEAD_GROUP = 2
V7X_VMEM_LIMIT = 52 * 1024 * 1024


def _params(sem, vmem=V7X_VMEM_LIMIT):
    return pltpu.CompilerParams(dimension_semantics=sem, vmem_limit_bytes=vmem)


def _bdot(a, b):
    return jnp.dot(a.astype(BF16), b.astype(BF16), preferred_element_type=F32)


def _bdot_nt(a, b):
    return lax.dot_general(a.astype(BF16), b.astype(BF16), (((1,), (1,)), ((), ())),
                           preferred_element_type=F32)


def _bdot_tn(a, b):
    return lax.dot_general(a.astype(BF16), b.astype(BF16), (((0,), (0,)), ((), ())),
                           preferred_element_type=F32)


def _fdot(a, b):
    return jnp.dot(a, b, preferred_element_type=F32, precision=lax.Precision.HIGHEST)


def _rms(x, g):
    return x * lax.rsqrt(jnp.mean(x * x, axis=-1, keepdims=True) + NORM_EPS) * g


def _rms_heads(x, g, n_heads, hd):
    return jnp.concatenate([_rms(x[:, h * hd:(h + 1) * hd], g) for h in range(n_heads)], axis=-1)


def _sigmoid(x):
    return 1.0 / (1.0 + jnp.exp(-x))


def _silu(x):
    return x * _sigmoid(x)


def _softmax_rows(parts):
    m = parts[0].max(axis=-1, keepdims=True)
    for p in parts[1:]:
        m = jnp.maximum(m, p.max(axis=-1, keepdims=True))
    es = [jnp.exp(p - m) for p in parts]
    den = es[0].sum(axis=-1, keepdims=True)
    for e in es[1:]:
        den = den + e.sum(axis=-1, keepdims=True)
    return es, den


def _mod_kernel(c_ref, w_ref, b_ref, o_ref):
    o_ref[0] = _bdot(_silu(c_ref[...]), w_ref[0]) + b_ref[0]


def _mod_call(cond8, w_mod, b_mod):
    depth, d, d6 = w_mod.shape
    tn = 1024
    return pl.pallas_call(
        _mod_kernel,
        grid=(depth, d6 // tn),
        in_specs=[pl.BlockSpec((8, d), lambda i, j: (0, 0)),
                  pl.BlockSpec((1, d, tn), lambda i, j: (i, 0, j)),
                  pl.BlockSpec((1, 1, tn), lambda i, j: (i, 0, j))],
        out_specs=pl.BlockSpec((1, 8, tn), lambda i, j: (i, 0, j)),
        out_shape=jax.ShapeDtypeStruct((depth, 8, d6), F32),
        compiler_params=_params(("arbitrary", "arbitrary")),
        name="adaln_mod",
    )(cond8, w_mod, b_mod.reshape(depth, 1, d6))


def _inproj_kernel(x_ref, m_ref, g_ref, w_ref, o_ref):
    h = _rms(x_ref[...], g_ref[...]) * (1.0 + m_ref[0, 1:2, :]) + m_ref[0, 0:1, :]
    o_ref[...] = _bdot(h, w_ref[...])


def _inproj_call(x, modsel, g, w):
    n, d = x.shape
    wout = w.shape[1]
    tm = TOKEN_TILE
    return pl.pallas_call(
        _inproj_kernel,
        grid=(n // tm,),
        in_specs=[pl.BlockSpec((tm, d), lambda i: (i, 0)),
                  pl.BlockSpec((1, 6, d), lambda i: (i, 0, 0)),
                  pl.BlockSpec((1, d), lambda i: (0, 0)),
                  pl.BlockSpec((d, wout), lambda i: (0, 0))],
        out_specs=pl.BlockSpec((tm, wout), lambda i: (i, 0)),
        out_shape=jax.ShapeDtypeStruct((n, wout), F32),
        compiler_params=_params(("arbitrary",)),
        name="inproj",
    )(x, modsel, g.reshape(1, d), w)


def _outproj_kernel(*refs, kind, n_pairs, n_ctx_tiles, n_heads, hd):
    in_ctx = pl.program_id(0) < n_ctx_tiles
    mix = [jnp.where(in_ctx, refs[2 * p][...], refs[2 * p + 1][...]) for p in range(n_pairs)]
    rest = refs[2 * n_pairs:]
    if kind == "plain":
        w_ref, x_ref, m_ref, y_ref = rest
        o = mix[0]
    else:
        gate_ref, on_ref, w_ref, x_ref, m_ref, y_ref = rest
        o = _rms_heads(mix[0] + mix[1], on_ref[...], n_heads, hd)
        o = o * _silu(gate_ref[...]) if kind == "hgrn" else _sigmoid(gate_ref[...]) * o
    y_ref[...] = x_ref[...] + m_ref[0, 2:3, :] * _bdot(o, w_ref[...])


def _outproj_call(kind, pairs, gate, w, x, modsel):
    n, d = x.shape
    tm = TOKEN_TILE
    n_ctx_tiles = pairs[0][0].shape[0] // tm
    specs, args = [], []
    for a_ctx, a_lat in pairs:
        specs += [pl.BlockSpec((tm, d), lambda i: (jnp.minimum(i, n_ctx_tiles - 1), 0)),
                  pl.BlockSpec((tm, d), lambda i: (jnp.maximum(i - n_ctx_tiles, 0), 0))]
        args += [a_ctx, a_lat]
    if gate is not None:
        proj, cb, on = gate
        specs += [pl.BlockSpec((tm, d), lambda i: (i, cb)), pl.BlockSpec(on.shape, lambda i: (0, 0))]
        args += [proj, on]
    specs += [pl.BlockSpec(w.shape, lambda i: (0, 0)),
              pl.BlockSpec((tm, d), lambda i: (i, 0)),
              pl.BlockSpec((1, 6, d), lambda i: (i, 0, 0))]
    args += [w, x, modsel]
    body = functools.partial(_outproj_kernel, kind=kind, n_pairs=len(pairs), n_ctx_tiles=n_ctx_tiles,
                             n_heads=8, hd=128)
    return pl.pallas_call(
        body,
        grid=(n // tm,),
        in_specs=specs,
        out_specs=pl.BlockSpec((tm, d), lambda i: (i, 0)),
        out_shape=jax.ShapeDtypeStruct((n, d), F32),
        compiler_params=_params(("arbitrary",)),
        name="outproj_" + kind,
    )(*args)


def _gqa_ctx_kernel(p_ref, qn_ref, kn_ref, o_ref, k_ref, *, n_heads, n_kv, hd):
    rep = n_heads // n_kv
    scale = hd ** -0.5
    koff = n_heads * hd
    voff = koff + n_kv * hd
    ks = [_rms(p_ref[:, koff + g * hd: koff + (g + 1) * hd], kn_ref[...]) for g in range(n_kv)]
    k_ref[...] = jnp.concatenate(ks, axis=-1)
    for h in range(n_heads):
        g = h // rep
        q = _rms(p_ref[:, h * hd:(h + 1) * hd], qn_ref[...])
        s = _bdot_nt(q, ks[g]) * scale
        (e,), den = _softmax_rows([s])
        o_ref[:, h * hd:(h + 1) * hd] = _bdot(e / den, p_ref[:, voff + g * hd: voff + (g + 1) * hd])


def _gqa_ctx_call(proj, qn, kn, n_seq, t):
    n_heads, n_kv, hd = 8, 2, 128
    win = proj.shape[1]
    body = functools.partial(_gqa_ctx_kernel, n_heads=n_heads, n_kv=n_kv, hd=hd)
    return pl.pallas_call(
        body,
        grid=(n_seq,),
        in_specs=[pl.BlockSpec((t, win), lambda b: (b, 0)),
                  pl.BlockSpec((1, hd), lambda b: (0, 0)),
                  pl.BlockSpec((1, hd), lambda b: (0, 0))],
        out_specs=[pl.BlockSpec((t, n_heads * hd), lambda b: (b, 0)),
                   pl.BlockSpec((t, n_kv * hd), lambda b: (b, 0))],
        out_shape=[jax.ShapeDtypeStruct((n_seq * t, n_heads * hd), F32),
                   jax.ShapeDtypeStruct((n_seq * t, n_kv * hd), F32)],
        compiler_params=_params(("arbitrary",)),
        name="gqa_ctx",
    )(proj, qn.reshape(1, hd), kn.reshape(1, hd))


def _rope(x, cosd, sind):
    return x * cosd + pltpu.roll(x, x.shape[-1] // 2, 1) * sind


def _gqa_lat_kernel(pq_ref, pkv_ref, kc_ref, vc_ref, cq_ref, sq_ref, ck_ref, sk_ref, qn_ref, kn_ref,
                    o_ref, k_scr, v_scr, *, n_heads, n_kv, hd, t_ctx):
    rep = n_heads // n_kv
    scale = hd ** -0.5

    @pl.when(pl.program_id(1) == 0)
    def _():
        k_scr[0:t_ctx, :] = kc_ref[0].astype(BF16)
        v_scr[...] = jnp.ones_like(v_scr)
        for g in range(n_kv):
            k = _rms(pkv_ref[:, g * hd:(g + 1) * hd], kn_ref[...])
            k_scr[t_ctx:, g * hd:(g + 1) * hd] = _rope(k, ck_ref[...], sk_ref[...]).astype(BF16)
            v_scr[0:t_ctx, 2 * g * hd:(2 * g + 1) * hd] = vc_ref[0, :, g * hd:(g + 1) * hd].astype(BF16)
            v_scr[t_ctx:, 2 * g * hd:(2 * g + 1) * hd] = pkv_ref[:, (n_kv + g) * hd:(n_kv + g + 1) * hd].astype(BF16)

    qs = [(_rope(_rms(pq_ref[:, h * hd:(h + 1) * hd], qn_ref[...]), cq_ref[...], sq_ref[...]) * scale).astype(BF16)
          for h in range(n_heads)]
    for h0 in range(0, n_heads, ATTN_HEAD_GROUP):
        hs = range(h0, min(h0 + ATTN_HEAD_GROUP, n_heads))
        s = [_bdot_nt(qs[h], k_scr[:, (h // rep) * hd:(h // rep + 1) * hd]) for h in hs]
        e = [jnp.exp(x - x.max(axis=-1, keepdims=True)).astype(BF16) for x in s]
        pv = [jnp.dot(e[i], v_scr[:, 2 * (h // rep) * hd:2 * (h // rep + 1) * hd], preferred_element_type=F32)
              for i, h in enumerate(hs)]
        for i, h in enumerate(hs):
            o_ref[:, h * hd:(h + 1) * hd] = pv[i][:, :hd] / pv[i][:, hd:]


def _gqa_lat_call(proj, row0, n_seq, t, cache_k, cache_v, cosd, sind, qn, kn):
    n_heads, n_kv, hd = 8, 2, 128
    tq = 256
    t_ctx = cache_k.shape[1]
    nq = t // tq
    kvw = 2 * n_kv * hd
    qblk0 = row0 // tq
    sblk0 = row0 // t
    body = functools.partial(_gqa_lat_kernel, n_heads=n_heads, n_kv=n_kv, hd=hd, t_ctx=t_ctx)
    return pl.pallas_call(
        body,
        grid=(n_seq, nq),
        in_specs=[pl.BlockSpec((tq, n_heads * hd), lambda b, i: (qblk0 + b * nq + i, 0)),
                  pl.BlockSpec((t, kvw), lambda b, i: (sblk0 + b, (n_heads * hd) // kvw)),
                  pl.BlockSpec((1, t_ctx, n_kv * hd), lambda b, i: (b, 0, 0)),
                  pl.BlockSpec((1, t_ctx, n_kv * hd), lambda b, i: (b, 0, 0)),
                  pl.BlockSpec((tq, hd), lambda b, i: (i, 0)),
                  pl.BlockSpec((tq, hd), lambda b, i: (i, 0)),
                  pl.BlockSpec((t, hd), lambda b, i: (0, 0)),
                  pl.BlockSpec((t, hd), lambda b, i: (0, 0)),
                  pl.BlockSpec((1, hd), lambda b, i: (0, 0)),
                  pl.BlockSpec((1, hd), lambda b, i: (0, 0))],
        out_specs=pl.BlockSpec((tq, n_heads * hd), lambda b, i: (b * nq + i, 0)),
        out_shape=jax.ShapeDtypeStruct((n_seq * t, n_heads * hd), F32),
        scratch_shapes=[pltpu.VMEM((t_ctx + t, n_kv * hd), BF16),
                        pltpu.VMEM((t_ctx + t, 2 * n_kv * hd), BF16)],
        compiler_params=_params(("arbitrary", "arbitrary")),
        name="gqa_latent",
    )(proj, proj, cache_k, cache_v, cosd, sind, cosd, sind, qn.reshape(1, hd), kn.reshape(1, hd))


def _mha_ctx_kernel(q_ref, k_ref, v_ref, o_ref, *, hd):
    scale = hd ** -0.5
    n_heads = q_ref.shape[1] // hd
    ones = jnp.ones((v_ref.shape[0], hd), BF16)
    for h0 in range(0, n_heads, ATTN_HEAD_GROUP):
        sls = [slice(h * hd, (h + 1) * hd) for h in range(h0, min(h0 + ATTN_HEAD_GROUP, n_heads))]
        s = [_bdot_nt(q_ref[:, sl] * scale, k_ref[:, sl]) for sl in sls]
        e = [jnp.exp(x - x.max(axis=-1, keepdims=True)).astype(BF16) for x in s]
        pv = [jnp.dot(e[i], jnp.concatenate([v_ref[:, sl].astype(BF16), ones], axis=-1),
                      preferred_element_type=F32) for i, sl in enumerate(sls)]
        for i, sl in enumerate(sls):
            o_ref[:, sl] = pv[i][:, :hd] / pv[i][:, hd:]


def _mha_ctx_call(proj, n_seq, t):
    hd = 64
    d = proj.shape[1] // 3
    body = functools.partial(_mha_ctx_kernel, hd=hd)
    return pl.pallas_call(
        body,
        grid=(n_seq,),
        in_specs=[pl.BlockSpec((t, d), lambda b: (b, 0)),
                  pl.BlockSpec((t, d), lambda b: (b, 1)),
                  pl.BlockSpec((t, d), lambda b: (b, 2))],
        out_specs=pl.BlockSpec((t, d), lambda b: (b, 0)),
        out_shape=jax.ShapeDtypeStruct((n_seq * t, d), F32),
        compiler_params=_params(("arbitrary",)),
        name="mha_ctx",
    )(proj, proj, proj)


def _na_bias_kernel(rpb_ref, o_ref, *, n_rel_rows, n_rel_cols):
    h = pl.program_id(0)
    w_io = lax.broadcasted_iota(I32, (GRID_W, 2 * GRID_W), 0)
    lane = lax.broadcasted_iota(I32, (GRID_W, 2 * GRID_W), 1)
    ck = jnp.where(lane < GRID_W, lane, lane - GRID_W)
    c_start = jnp.clip(w_io - NA_COLS // 2, 0, GRID_W - NA_COLS)
    in_win = (ck >= c_start) & (ck < c_start + NA_COLS)
    rel = ck - w_io + (NA_COLS - 1)
    base = h * (n_rel_rows * n_rel_cols)
    tiles = []
    for j in range(n_rel_rows):
        acc = jnp.zeros((GRID_W, 2 * GRID_W), F32)
        for jj in range(n_rel_cols):
            acc = jnp.where(rel == jj, rpb_ref[base + j * n_rel_cols + jj], acc)
        tiles.append(jnp.where(in_win, acc, NEG_BIG))
    for j in range(n_rel_rows):
        hi = tiles[j + 1] if j + 1 < n_rel_rows else jnp.full((GRID_W, 2 * GRID_W), NEG_BIG, F32)
        o_ref[0, j] = jnp.where(lane < GRID_W, tiles[j], hi)


def _na_bias_call(rpb):
    n_heads, nrr, nrc = rpb.shape
    body = functools.partial(_na_bias_kernel, n_rel_rows=nrr, n_rel_cols=nrc)
    return pl.pallas_call(
        body,
        grid=(n_heads,),
        in_specs=[pl.BlockSpec(memory_space=pltpu.SMEM)],
        out_specs=pl.BlockSpec((1, nrr, GRID_W, 2 * GRID_W), lambda h: (h, 0, 0, 0)),
        out_shape=jax.ShapeDtypeStruct((n_heads, nrr, GRID_W, 2 * GRID_W), F32),
        compiler_params=_params(("arbitrary",)),
        name="na_bias",
    )(rpb.reshape(-1))


NA_QROWS = 4
NA_KROWS = 12


def _na_kernel(q_ref, k0_ref, k1_ref, k2_ref, v0_ref, v1_ref, v2_ref, kc_ref, vc_ref, tz_ref, o_ref,
               *, hd, n_grid_rows):
    scale = hd ** -0.5
    blk = pl.program_id(1)
    kstart = jnp.clip(blk * NA_QROWS - NA_ROWS // 2, 0, n_grid_rows - NA_KROWS)
    lane = lax.broadcasted_iota(I32, (GRID_W, 2 * GRID_W), 1)
    n_rel = tz_ref.shape[1]
    heads = range(q_ref.shape[1] // hd)
    sls = [slice(hh * hd, (hh + 1) * hd) for hh in heads]
    rel, pen = [], []
    for rq_l in range(NA_QROWS):
        rq = blk * NA_QROWS + rq_l
        r_start = jnp.clip(rq - NA_ROWS // 2, 0, n_grid_rows - NA_ROWS)
        rel.append([])
        pen.append([])
        for m in range(NA_KROWS // 2):
            rk = kstart + 2 * m
            rel[-1].append(jnp.clip(rk - rq + (NA_ROWS - 1), 0, n_rel - 1))
            ok0 = (rk >= r_start) & (rk < r_start + NA_ROWS)
            ok1 = (rk + 1 >= r_start) & (rk + 1 < r_start + NA_ROWS)
            pen[-1].append(jnp.where(lane < GRID_W, jnp.where(ok0, 0.0, NEG_BIG), jnp.where(ok1, 0.0, NEG_BIG)))
    bias = [jnp.concatenate([jnp.concatenate([tz_ref[hh, pl.ds(rel[r][m], 1)][0] + pen[r][m]
                                              for m in range(NA_KROWS // 2)], axis=-1)
                             for r in range(NA_QROWS)], axis=0) for hh in heads]
    q = [q_ref[:, sl] * scale for sl in sls]
    k_loc = [jnp.concatenate([k0_ref[:, sl], k1_ref[:, sl], k2_ref[:, sl]], axis=0) for sl in sls]
    v_loc = [jnp.concatenate([v0_ref[:, sl], v1_ref[:, sl], v2_ref[:, sl]], axis=0).astype(BF16) for sl in sls]
    v_ctx = [vc_ref[0, :, sl].astype(BF16) for sl in sls]
    s_loc = [_bdot_nt(q[h], k_loc[h]) + bias[h] for h in heads]
    s_ctx = [_bdot_nt(q[h], kc_ref[0, :, sls[h]]) for h in heads]
    m = [jnp.maximum(s_loc[h].max(axis=-1, keepdims=True), s_ctx[h].max(axis=-1, keepdims=True)) for h in heads]
    e_loc = [jnp.exp(s_loc[h] - m[h]).astype(BF16) for h in heads]
    e_ctx = [jnp.exp(s_ctx[h] - m[h]).astype(BF16) for h in heads]
    pv = [jnp.dot(e_loc[h], jnp.concatenate([v_loc[h], jnp.ones_like(v_loc[h])], axis=-1),
                  preferred_element_type=F32)
          + jnp.dot(e_ctx[h], jnp.concatenate([v_ctx[h], jnp.ones_like(v_ctx[h])], axis=-1),
                    preferred_element_type=F32) for h in heads]
    for h in heads:
        o_ref[:, sls[h]] = pv[h][:, :hd] / pv[h][:, hd:]


def _na_call(proj, row0, n_seq, t, cache_k, cache_v, tz):
    hd, cw = 64, 128
    d = proj.shape[1] // 3
    ncb = d // cw
    tq = NA_QROWS * GRID_W
    nq = t // tq
    n_grid_rows = t // GRID_W
    t_ctx = cache_k.shape[1]
    qblk0 = row0 // tq

    def kv_map(which, j):
        def index_map(b, i, c):
            ks = jnp.clip(i * NA_QROWS - NA_ROWS // 2, 0, n_grid_rows - NA_KROWS) // NA_QROWS
            return (qblk0 + b * nq + ks + j, which * ncb + c)
        return index_map

    body = functools.partial(_na_kernel, hd=hd, n_grid_rows=n_grid_rows)
    return pl.pallas_call(
        body,
        grid=(n_seq, nq, ncb),
        in_specs=[pl.BlockSpec((tq, cw), lambda b, i, c: (qblk0 + b * nq + i, c))]
                 + [pl.BlockSpec((tq, cw), kv_map(1, j)) for j in range(3)]
                 + [pl.BlockSpec((tq, cw), kv_map(2, j)) for j in range(3)]
                 + [pl.BlockSpec((1, t_ctx, cw), lambda b, i, c: (b, 0, c)),
                    pl.BlockSpec((1, t_ctx, cw), lambda b, i, c: (b, 0, c)),
                    pl.BlockSpec((cw // hd,) + tz.shape[1:], lambda b, i, c: (c, 0, 0, 0))],
        out_specs=pl.BlockSpec((tq, cw), lambda b, i, c: (b * nq + i, c)),
        out_shape=jax.ShapeDtypeStruct((n_seq * t, d), F32),
        compiler_params=_params(("arbitrary", "arbitrary", "arbitrary")),
        name="nbr_attn",
    )(proj, proj, proj, proj, proj, proj, proj, cache_k, cache_v, tz)


def _hgrn_kernel(q_ref, v_ref, f_ref, bf_ref, lb_ref, s0_ref, o_ref, s_ref, st_scr,
                 *, reverse, n_heads, dk, has_s0, n_blk):
    c = pl.program_id(1)
    L = HGRN_CHUNK
    tb = q_ref.shape[0]

    @pl.when(c == 0)
    def _():
        for h in range(n_heads):
            if has_s0:
                st_scr[h] = s0_ref[0, 0, h].T
            else:
                st_scr[h] = jnp.zeros_like(st_scr[h])

    row = lax.broadcasted_iota(I32, (L, L), 0)
    col = lax.broadcasted_iota(I32, (L, L), 1)
    tri = jnp.where((col >= row) if reverse else (col <= row), 1.0, 0.0).astype(F32)
    eye = jnp.where(row == col, 1.0, 0.0).astype(F32)
    halves = [L >> (i + 1) for i in range(L.bit_length() - 1)]
    same_pair = [jnp.where((row // (2 * hf)) == (col // (2 * hf)), 1.0, 0.0).astype(F32) for hf in halves]
    rio = lax.broadcasted_iota(I32, (L, q_ref.shape[1]), 0)
    is_query = [jnp.where(((rio & hf) == 0) if reverse else ((rio & hf) != 0), 1.0, 0.0).astype(F32)
                for hf in halves]
    r8 = lax.broadcasted_iota(I32, (ROW_SUBLANES, 1), 0)
    lb = lb_ref[0]
    bf = bf_ref[0]
    n_chunks = tb // L

    def boundary_rows(cum, hf):
        blk = 2 * hf
        off = hf if reverse else hf - 1
        width = cum.shape[1]
        if blk >= ROW_SUBLANES:
            return jnp.concatenate([jnp.broadcast_to(cum[a + off:a + off + 1, :], (blk, width))
                                    for a in range(0, L, blk)], axis=0)
        groups = []
        for g in range(0, L, ROW_SUBLANES):
            ref = jnp.broadcast_to(cum[g + off:g + off + 1, :], (ROW_SUBLANES, width))
            for a in range(blk, ROW_SUBLANES, blk):
                ref = jnp.where(r8 >= a, jnp.broadcast_to(cum[g + a + off:g + a + off + 1, :],
                                                          (ROW_SUBLANES, width)), ref)
            groups.append(ref)
        return jnp.concatenate(groups, axis=0)

    def chunk_step(jj, carry):
        jc = (n_chunks - 1 - jj) if reverse else jj
        rs = pl.ds(pl.multiple_of(jc * L, L), L)
        q = _silu(q_ref[rs, :]) * (dk ** -0.5)
        v = v_ref[rs, :]
        f = lb + (1.0 - lb) * _sigmoid(f_ref[rs, :] + bf)
        logf = jnp.log(f)
        kk = 1.0 - f
        cum = _fdot(tri, logf)
        end = cum[0:1, :] if reverse else cum[L - 1:L, :]
        qd = q * jnp.exp(cum)
        kd = kk * jnp.exp(end - cum)
        e_end = jnp.exp(end)
        q_lv, k_lv = [], []
        for lv, hf in enumerate(halves):
            isq = is_query[lv]
            decay = jnp.exp((2.0 * isq - 1.0) * (cum - boundary_rows(cum, hf)))
            dq = decay * isq
            q_lv.append((q * dq).astype(BF16))
            k_lv.append((kk * (decay - dq)).astype(BF16))
        qk_diag = q * kk
        outs = []
        for h in range(n_heads):
            hs = slice(h * dk, (h + 1) * dk)
            attn = eye * jnp.sum(qk_diag[:, hs], axis=-1, keepdims=True)
            for lv in range(len(halves)):
                attn = attn + same_pair[lv] * _bdot_nt(q_lv[lv][:, hs], k_lv[lv][:, hs])
            outs.append(_bdot_nt(qd[:, hs], st_scr[h]) + _bdot(attn, v[:, hs]))
        o_ref[rs, :] = jnp.concatenate(outs, axis=-1)
        for h in range(n_heads):
            hs = slice(h * dk, (h + 1) * dk)
            st_scr[h] = st_scr[h] * e_end[:, hs] + _bdot_tn(v[:, hs], kd[:, hs])
        return carry

    lax.fori_loop(0, n_chunks, chunk_step, 0)

    @pl.when(c == n_blk - 1)
    def _():
        for h in range(n_heads):
            s_ref[0, 0, h] = st_scr[h].T


def _hgrn_call(proj, row0, n_seq, t, b_f, lb, s0, reverse):
    n_heads, dk = 8, 128
    d = n_heads * dk
    tb = 256
    n_blk = t // tb
    blk0 = row0 // tb
    di = 1 if reverse else 0
    has_s0 = s0 is not None
    if not has_s0:
        s0 = jnp.zeros((1, 2, n_heads, dk, dk), F32)

    def tok(b, c):
        return blk0 + b * n_blk + ((n_blk - 1 - c) if reverse else c)

    body = functools.partial(_hgrn_kernel, reverse=reverse, n_heads=n_heads, dk=dk,
                             has_s0=has_s0, n_blk=n_blk)
    return pl.pallas_call(
        body,
        grid=(n_seq, n_blk),
        in_specs=[pl.BlockSpec((tb, d), lambda b, c: (tok(b, c), 0)),
                  pl.BlockSpec((tb, d), lambda b, c: (tok(b, c), 1)),
                  pl.BlockSpec((tb, d), lambda b, c: (tok(b, c), 2 + di)),
                  pl.BlockSpec((1, 1, d), lambda b, c: (di, 0, 0)),
                  pl.BlockSpec((1, 1, d), lambda b, c: (di, 0, 0)),
                  pl.BlockSpec((1, 1, n_heads, dk, dk),
                               (lambda b, c: (b, di, 0, 0, 0)) if has_s0 else (lambda b, c: (0, 0, 0, 0, 0)))],
        out_specs=[pl.BlockSpec((tb, d), lambda b, c: (tok(b, c) - blk0, 0)),
                   pl.BlockSpec((1, 1, n_heads, dk, dk), lambda b, c: (b, 0, 0, 0, 0))],
        out_shape=[jax.ShapeDtypeStruct((n_seq * t, d), F32),
                   jax.ShapeDtypeStruct((n_seq, 1, n_heads, dk, dk), F32)],
        scratch_shapes=[pltpu.VMEM((n_heads, dk, dk), F32)],
        compiler_params=_params(("arbitrary", "arbitrary")),
        name="hgrn2_bw" if reverse else "hgrn2_fw",
    )(proj, proj, proj, b_f.reshape(2, 1, d), lb.reshape(2, 1, d), s0)


def _log_sigmoid(x):
    return jnp.minimum(x, 0.0) - jnp.log(1.0 + jnp.exp(-jnp.abs(x)))


def _mlstm_kernel(q_ref, k_ref, v_ref, g_ref, bg_ref, c0_ref, n0_ref, m0_ref,
                  o_ref, c_out, n_out, m_out, c_scr, n_scr, m_scr,
                  *, reverse, n_heads, dqk, dv, has_state, n_blk):
    c = pl.program_id(1)
    L = q_ref.shape[0]
    i_off = 2 * n_heads if reverse else 0
    f_off = i_off + n_heads

    @pl.when(c == 0)
    def _():
        if has_state:
            c_scr[...] = c0_ref[0, 0]
            n_scr[...] = n0_ref[0, 0]
            m_scr[...] = m0_ref[0, 0]
        else:
            c_scr[...] = jnp.zeros_like(c_scr)
            n_scr[...] = jnp.zeros_like(n_scr)
            m_scr[...] = jnp.zeros_like(m_scr)

    gates = GATE_SOFTCAP * jnp.tanh((g_ref[...] + bg_ref[...]) / GATE_SOFTCAP)
    logf = _log_sigmoid(gates)
    row = lax.broadcasted_iota(I32, (L, L), 0)
    col = lax.broadcasted_iota(I32, (L, L), 1)
    causal = (col >= row) if reverse else (col <= row)
    tri = jnp.where(causal, 1.0, 0.0).astype(F32)
    cum = _fdot(tri, logf)
    cum_t = cum.T
    gates_t = gates.T
    e_row = 0 if reverse else L - 1
    m_all = m_scr[...]
    c_prev = [c_scr[h] for h in range(n_heads)]
    n_prev = [n_scr[h] for h in range(n_heads)]
    lane = lax.broadcasted_iota(I32, m_all.shape, 1)
    heads = range(n_heads)
    qs = [q_ref[:, h * dqk:(h + 1) * dqk] * (dqk ** -0.5) for h in heads]
    ks_ = [k_ref[:, h * dqk:(h + 1) * dqk] for h in heads]
    vs = [v_ref[:, h * dv:(h + 1) * dv] for h in heads]
    cum_c = [cum[:, f_off + h:f_off + h + 1] for h in heads]
    cum_r = [cum_t[f_off + h:f_off + h + 1, :] for h in heads]
    i_c = [gates[:, i_off + h:i_off + h + 1] for h in heads]
    i_r = [gates_t[i_off + h:i_off + h + 1, :] for h in heads]
    m_prev = [m_all[0:1, h:h + 1] for h in heads]
    d = [jnp.where(causal, cum_c[h] - cum_r[h] + i_r[h], -jnp.inf) for h in heads]
    m_inter = [cum_c[h] + m_prev[h] for h in heads]
    m_t = [jnp.maximum(m_inter[h], d[h].max(axis=-1, keepdims=True)) for h in heads]
    scores = [_bdot_nt(qs[h], ks_[h]) for h in heads]
    inter = [_bdot(qs[h], c_prev[h]) for h in heads]
    qn = [jnp.sum(qs[h] * n_prev[h], axis=-1, keepdims=True) for h in heads]
    w_inter = [jnp.exp(m_inter[h] - m_t[h]) for h in heads]
    qk = [(scores[h] * jnp.exp(d[h] - m_t[h])).astype(BF16) for h in heads]
    pv = [jnp.dot(qk[h], jnp.concatenate([vs[h].astype(BF16), jnp.ones((L, dv), BF16)], axis=-1),
                  preferred_element_type=F32) for h in heads]
    num = [w_inter[h] * inter[h] + pv[h][:, :dv] for h in heads]
    den = [w_inter[h] * qn[h] + pv[h][:, dv:] for h in heads]
    outs = [num[h] / jnp.maximum(jnp.abs(den[h]), jnp.exp(-m_t[h])) for h in heads]
    end = [cum_c[h][e_row:e_row + 1, :] for h in heads]
    g_end_r = [end[h] - cum_r[h] + i_r[h] for h in heads]
    g_end_c = [end[h] - cum_c[h] + i_c[h] for h in heads]
    m_new = [jnp.maximum(end[h] + m_prev[h], g_end_r[h].max(axis=-1, keepdims=True)) for h in heads]
    w_old = [jnp.exp(end[h] + m_prev[h] - m_new[h]) for h in heads]
    kd = [ks_[h] * jnp.exp(g_end_c[h] - m_new[h]) for h in heads]
    c_new = [w_old[h] * c_prev[h] + _bdot_tn(kd[h], vs[h]) for h in heads]
    n_new = [w_old[h] * n_prev[h] + kd[h].sum(axis=0, keepdims=True) for h in heads]
    m_next = m_all
    for h in heads:
        m_next = jnp.where(lane == h, m_new[h], m_next)
    o_ref[...] = jnp.concatenate(outs, axis=-1)
    for h in range(n_heads):
        c_scr[h] = c_new[h]
        n_scr[h] = n_new[h]
    m_scr[...] = m_next

    @pl.when(c == n_blk - 1)
    def _():
        c_out[0, 0] = c_scr[...]
        n_out[0, 0] = n_scr[...]
        m_out[0, 0] = m_scr[...]


def _mlstm_call(proj, row0, n_seq, t, b_gates_pad, state, reverse):
    n_heads, dqk, dv = 8, 64, 128
    wq, wv = n_heads * dqk, n_heads * dv
    L = 256
    n_blk = t // L
    blk0 = row0 // L
    di = 1 if reverse else 0
    has_state = state is not None
    if has_state:
        c0, n0, m0 = state
        smap = lambda b, c: (b, di, 0, 0, 0)
        mmap = lambda b, c: (b, di, 0, 0)
    else:
        c0 = jnp.zeros((1, 1, n_heads, dqk, dv), F32)
        n0 = jnp.zeros((1, 1, n_heads, 1, dqk), F32)
        m0 = jnp.zeros((1, 1, 1, 128), F32)
        smap = lambda b, c: (0, 0, 0, 0, 0)
        mmap = lambda b, c: (0, 0, 0, 0)

    def tok(b, c):
        return blk0 + b * n_blk + ((n_blk - 1 - c) if reverse else c)

    body = functools.partial(_mlstm_kernel, reverse=reverse, n_heads=n_heads, dqk=dqk, dv=dv,
                             has_state=has_state, n_blk=n_blk)
    gate_cb = (2 * wq + 2 * wv) // 128
    return pl.pallas_call(
        body,
        grid=(n_seq, n_blk),
        in_specs=[pl.BlockSpec((L, wq), lambda b, c: (tok(b, c), 0)),
                  pl.BlockSpec((L, wq), lambda b, c: (tok(b, c), 1)),
                  pl.BlockSpec((L, wv), lambda b, c: (tok(b, c), (2 * wq) // wv)),
                  pl.BlockSpec((L, 128), lambda b, c: (tok(b, c), gate_cb)),
                  pl.BlockSpec((1, 128), lambda b, c: (0, 0)),
                  pl.BlockSpec((1, 1, n_heads, dqk, dv), smap),
                  pl.BlockSpec((1, 1, n_heads, 1, dqk), smap),
                  pl.BlockSpec((1, 1, 1, 128), mmap)],
        out_specs=[pl.BlockSpec((L, wv), lambda b, c: (tok(b, c) - blk0, 0)),
                   pl.BlockSpec((1, 1, n_heads, dqk, dv), lambda b, c: (b, 0, 0, 0, 0)),
                   pl.BlockSpec((1, 1, n_heads, 1, dqk), lambda b, c: (b, 0, 0, 0, 0)),
                   pl.BlockSpec((1, 1, 1, 128), lambda b, c: (b, 0, 0, 0))],
        out_shape=[jax.ShapeDtypeStruct((n_seq * t, wv), F32),
                   jax.ShapeDtypeStruct((n_seq, 1, n_heads, dqk, dv), F32),
                   jax.ShapeDtypeStruct((n_seq, 1, n_heads, 1, dqk), F32),
                   jax.ShapeDtypeStruct((n_seq, 1, 1, 128), F32)],
        scratch_shapes=[pltpu.VMEM((n_heads, dqk, dv), F32),
                        pltpu.VMEM((n_heads, 1, dqk), F32),
                        pltpu.VMEM((1, 128), F32)],
        compiler_params=_params(("arbitrary", "arbitrary")),
        name="mlstm_bw" if reverse else "mlstm_fw",
    )(proj, proj, proj, proj, b_gates_pad, c0, n0, m0)


def _moe_input(x_ref, m_ref, g_ref):
    return _rms(x_ref[...], g_ref[...]) * (1.0 + m_ref[0, 4:5, :]) + m_ref[0, 3:4, :]


def _route_kernel(x_ref, m_ref, g_ref, wr_ref, br_ref, idx_ref, gate_ref, rank_ref, cnt_ref, carry_scr,
                  *, n_experts):
    i = pl.program_id(0)
    tm = x_ref.shape[0]

    @pl.when(i == 0)
    def _():
        carry_scr[...] = jnp.zeros_like(carry_scr)

    h = _moe_input(x_ref, m_ref, g_ref)
    logits = lax.dot_general(wr_ref[...], h, (((1,), (1,)), ((), ())), preferred_element_type=F32,
                             precision=lax.Precision.HIGHEST) + br_ref[...]
    e_io = lax.broadcasted_iota(I32, (n_experts, tm), 0).astype(F32)
    work = logits
    vals, idxs = [], []
    chosen = jnp.zeros((n_experts, tm), F32)
    for _ in range(TOP_K):
        mx = work.max(axis=0, keepdims=True)
        ix = jnp.min(jnp.where(work == mx, e_io, float(n_experts)), axis=0, keepdims=True)
        hit = e_io == ix
        vals.append(mx)
        idxs.append(ix)
        chosen = jnp.where(hit, 1.0, chosen)
        work = jnp.where(hit, -jnp.inf, work)
    es = [jnp.exp(v - vals[0]) for v in vals]
    den = es[0] + es[1] + es[2] + es[3]
    srow = lax.broadcasted_iota(I32, (tm, tm), 0)
    scol = lax.broadcasted_iota(I32, (tm, tm), 1)
    before = jnp.where(srow < scol, 1.0, 0.0).astype(BF16)
    pos = jnp.dot(chosen.astype(BF16), before, preferred_element_type=F32) + carry_scr[...]
    ranks = [jnp.sum(jnp.where(e_io == ix, pos, 0.0), axis=0, keepdims=True) for ix in idxs]
    carry_scr[...] = carry_scr[...] + chosen.sum(axis=1, keepdims=True)
    idx_ref[...] = jnp.concatenate(idxs, axis=0).astype(I32)
    gate_ref[...] = jnp.concatenate([e / den for e in es], axis=0)
    rank_ref[...] = jnp.concatenate(ranks, axis=0).astype(I32)
    cnt_ref[...] = jnp.broadcast_to(carry_scr[...], cnt_ref.shape).astype(I32)


def _route_call(x, modsel, g, w_router, b_router):
    n, d = x.shape
    n_experts = w_router.shape[1]
    tm = TOKEN_TILE
    body = functools.partial(_route_kernel, n_experts=n_experts)
    return pl.pallas_call(
        body,
        grid=(n // tm,),
        in_specs=[pl.BlockSpec((tm, d), lambda i: (i, 0)),
                  pl.BlockSpec((1, 6, d), lambda i: (i, 0, 0)),
                  pl.BlockSpec((1, d), lambda i: (0, 0)),
                  pl.BlockSpec((n_experts, d), lambda i: (0, 0)),
                  pl.BlockSpec((n_experts, 1), lambda i: (0, 0))],
        out_specs=[pl.BlockSpec((TOP_K, tm), lambda i: (0, i)),
                   pl.BlockSpec((TOP_K, tm), lambda i: (0, i)),
                   pl.BlockSpec((TOP_K, tm), lambda i: (0, i)),
                   pl.BlockSpec((n_experts, 128), lambda i: (0, 0))],
        out_shape=[jax.ShapeDtypeStruct((TOP_K, n), I32),
                   jax.ShapeDtypeStruct((TOP_K, n), F32),
                   jax.ShapeDtypeStruct((TOP_K, n), I32),
                   jax.ShapeDtypeStruct((n_experts, 128), I32)],
        scratch_shapes=[pltpu.VMEM((n_experts, 1), F32)],
        compiler_params=_params(("arbitrary",)),
        name="moe_route",
    )(x, modsel, g.reshape(1, d), w_router.T, b_router.reshape(n_experts, 1))


def _slot_kernel(cnt_ref, idx_ref, rank_ref, dest_ref, binfo_ref, pad_ref, nused_ref, *, n_experts, n_blocks):
    cnt = cnt_ref[:, 0:1].astype(F32)
    padded = jnp.ceil(cnt * (1.0 / MOE_BLOCK)) * MOE_BLOCK
    er = lax.broadcasted_iota(I32, (n_experts, n_experts), 0)
    ec = lax.broadcasted_iota(I32, (n_experts, n_experts), 1)
    start_row = jnp.sum(jnp.where(er < ec, padded, 0.0), axis=0, keepdims=True)
    start_col = jnp.sum(jnp.where(er == ec, start_row, 0.0), axis=1, keepdims=True)
    end_col = start_col + padded
    idx = idx_ref[...]
    e_io = lax.broadcasted_iota(I32, (n_experts,) + idx.shape[1:], 0)
    rows = []
    for k in range(TOP_K):
        hit = e_io == idx[k:k + 1, :]
        rows.append(jnp.sum(jnp.where(hit, start_col, 0.0), axis=0, keepdims=True))
    dest_ref[...] = jnp.concatenate(rows, axis=0).astype(I32) + rank_ref[...]
    blk_start = (lax.broadcasted_iota(I32, (n_experts, n_blocks), 1) * MOE_BLOCK).astype(F32)
    n_done = jnp.sum(jnp.where(end_col <= blk_start, 1.0, 0.0), axis=0, keepdims=True)
    bexp = jnp.minimum(n_done, n_experts - 1.0)
    used_row = jnp.sum(jnp.where(er == ec, jnp.where(cnt > 0.0, 1.0, 0.0), 0.0), axis=0, keepdims=True)
    ecf = ec.astype(F32)
    next_col = jnp.min(jnp.where((ec > er) & (used_row > 0.0), ecf, float(n_experts)), axis=1, keepdims=True)
    ord_col = jnp.sum(jnp.where(ec < er, used_row, 0.0), axis=1, keepdims=True)
    par_col = ord_col - 2.0 * jnp.floor(ord_col * 0.5)
    mine = lax.broadcasted_iota(I32, (n_experts, n_blocks), 0).astype(F32) == bexp
    bnext = jnp.sum(jnp.where(mine, next_col, 0.0), axis=0, keepdims=True)
    bslot = jnp.sum(jnp.where(mine, par_col, 0.0), axis=0, keepdims=True)
    binfo_ref[...] = jnp.concatenate([bexp, bnext, bslot], axis=0).astype(I32)
    n_used = jnp.sum(padded, axis=0, keepdims=True) * (1.0 / MOE_BLOCK)
    nused_ref[...] = n_used.astype(I32)
    cnt_row = jnp.sum(jnp.where(er == ec, cnt, 0.0), axis=0, keepdims=True)
    padded_row = jnp.sum(jnp.where(er == ec, padded, 0.0), axis=0, keepdims=True)
    pad_ref[...] = jnp.concatenate([start_row + cnt_row, padded_row - cnt_row,
                                    jnp.broadcast_to(n_used, cnt_row.shape)], axis=0).astype(I32)


def _slot_call(counts, idx_t, rank_t, n_blocks):
    n_experts = counts.shape[0]
    n = idx_t.shape[1]
    tn = min(2048, n)
    body = functools.partial(_slot_kernel, n_experts=n_experts, n_blocks=n_blocks)
    return pl.pallas_call(
        body,
        grid=(n // tn,),
        in_specs=[pl.BlockSpec((n_experts, 128), lambda i: (0, 0)),
                  pl.BlockSpec((TOP_K, tn), lambda i: (0, i)),
                  pl.BlockSpec((TOP_K, tn), lambda i: (0, i))],
        out_specs=[pl.BlockSpec((TOP_K, tn), lambda i: (0, i)),
                   pl.BlockSpec((3, n_blocks), lambda i: (0, 0)),
                   pl.BlockSpec((3, n_experts), lambda i: (0, 0)),
                   pl.BlockSpec((1, 1), lambda i: (0, 0))],
        out_shape=[jax.ShapeDtypeStruct((TOP_K, n), I32),
                   jax.ShapeDtypeStruct((3, n_blocks), I32),
                   jax.ShapeDtypeStruct((3, n_experts), I32),
                   jax.ShapeDtypeStruct((1, 1), I32)],
        compiler_params=_params(("arbitrary",)),
        name="moe_slots",
    )(counts, idx_t, rank_t)


DMA_ISSUE_UNROLL = 8


def _to_row_tiles(ref, base, x):
    rows = x.shape[0]
    for c in range(ROW_SUBLANES):
        ref[pl.ds(base * ROW_SUBLANES + c, rows, stride=ROW_SUBLANES), :] = x[:, c * LANES:(c + 1) * LANES]


def _from_row_tiles(ref, base, rows, c):
    return ref[pl.ds(base * ROW_SUBLANES + c, rows, stride=ROW_SUBLANES), :]


def _row_tile(ref, r):
    return ref.at[pl.ds(pl.multiple_of(r * ROW_SUBLANES, ROW_SUBLANES), ROW_SUBLANES)]


def _zero_fill_padding(pad_ref, xs_ref, z_scr, sem):
    z_scr[...] = jnp.zeros_like(z_scr)
    n_experts = pad_ref.shape[1]
    bits = range(MOE_BLOCK.bit_length() - 2, -1, -1)

    def pieces(e):
        off, length = pad_ref[0, e], pad_ref[1, e]
        for bit in bits:
            size = 1 << bit
            done = (length >> (bit + 1)) << (bit + 1)
            copy = pltpu.make_async_copy(z_scr.at[pl.ds(0, size * ROW_SUBLANES)],
                                         xs_ref.at[pl.ds(pl.multiple_of((off + done) * ROW_SUBLANES, ROW_SUBLANES),
                                                         size * ROW_SUBLANES)], sem)
            yield (length & size) != 0, copy

    def tail_blocks():
        n_blocks = xs_ref.shape[0] // (MOE_BLOCK * ROW_SUBLANES)
        for b in range(n_blocks - n_experts, n_blocks):
            copy = pltpu.make_async_copy(z_scr, xs_ref.at[pl.ds(b * MOE_BLOCK * ROW_SUBLANES,
                                                                MOE_BLOCK * ROW_SUBLANES)], sem)
            yield b >= pad_ref[2, 0], copy

    def all_copies():
        for e in range(n_experts):
            yield from pieces(e)
        yield from tail_blocks()

    for needed, copy in all_copies():
        pl.when(needed)(copy.start)
    for needed, copy in all_copies():
        pl.when(needed)(copy.wait)


def _dispatch_kernel(pad_ref, dest_ref, x_ref, m_ref, g_ref, xs_ref, h_scr, z_scr, sem, *, n_tiles):
    i = pl.program_id(0)
    tm = x_ref.shape[0]

    @pl.when(i == 0)
    def _():
        _zero_fill_padding(pad_ref, xs_ref, z_scr, sem.at[2])

    slot = lax.rem(i, 2)
    src = h_scr.at[slot]
    _to_row_tiles(src, 0, _moe_input(x_ref, m_ref, g_ref))

    def start_row(r, carry):
        for k in range(TOP_K):
            pltpu.make_async_copy(_row_tile(src, r), _row_tile(xs_ref, dest_ref[0, 0, k * tm + r]),
                                  sem.at[slot]).start(priority=k % 2)
        return carry

    lax.fori_loop(0, tm, start_row, 0, unroll=DMA_ISSUE_UNROLL // TOP_K)

    def wait_tile(s):
        for _ in range(TOP_K):
            pltpu.make_async_copy(h_scr.at[s], xs_ref.at[pl.ds(0, tm * ROW_SUBLANES)], sem.at[s]).wait()

    @pl.when(i >= 1)
    def _():
        wait_tile(1 - slot)

    @pl.when(i == n_tiles - 1)
    def _():
        wait_tile(slot)


def _dispatch_call(pad_info, dest_tiles, x, modsel, g, n_slots):
    n, d = x.shape
    tm = TOKEN_TILE
    assert d == ROW_SUBLANES * LANES
    return pl.pallas_call(
        functools.partial(_dispatch_kernel, n_tiles=n // tm),
        grid=(n // tm,),
        in_specs=[pl.BlockSpec(memory_space=pltpu.SMEM),
                  pl.BlockSpec((1, 1, TOP_K * tm), lambda i: (i, 0, 0), memory_space=pltpu.SMEM),
                  pl.BlockSpec((tm, d), lambda i: (i, 0)),
                  pl.BlockSpec((1, 6, d), lambda i: (i, 0, 0)),
                  pl.BlockSpec((1, d), lambda i: (0, 0))],
        out_specs=pl.BlockSpec(memory_space=pl.ANY),
        out_shape=jax.ShapeDtypeStruct((n_slots * ROW_SUBLANES, LANES), F32),
        scratch_shapes=[pltpu.VMEM((2, tm * ROW_SUBLANES, LANES), F32),
                        pltpu.VMEM((MOE_BLOCK * ROW_SUBLANES, LANES), F32),
                        pltpu.SemaphoreType.DMA((3,))],
        compiler_params=_params(("arbitrary",)),
        name="moe_dispatch",
    )(pad_info, dest_tiles, x, modsel, g.reshape(1, d))


def _ffn_kernel(binfo_ref, nused_ref, xs_ref, wgu_hbm, bgu_ref, wdn_hbm, bdn_ref, ys_ref,
                wgu_f32, wdn_f32, wgu_scr, wdn_scr, sem, *, layer, n_experts, n_blocks):
    b = pl.program_id(0)
    d_ff = wdn_scr.shape[0]
    rows = xs_ref.shape[0] // ROW_SUBLANES

    def weight_copies(e, slot):
        return (pltpu.make_async_copy(wgu_hbm.at[layer, e], wgu_f32.at[slot], sem.at[slot]),
                pltpu.make_async_copy(wdn_hbm.at[layer, e], wdn_f32.at[slot], sem.at[slot]))

    @pl.when(b < nused_ref[0])
    def _():
        e = binfo_ref[b]
        prev = binfo_ref[jnp.maximum(b - 1, 0)]
        slot = binfo_ref[2 * n_blocks + b]

        @pl.when((b == 0) | (e != prev))
        def _():
            @pl.when(b == 0)
            def _():
                for cp in weight_copies(e, slot):
                    cp.start()

            nxt = binfo_ref[n_blocks + b]

            @pl.when(nxt < n_experts)
            def _():
                for cp in weight_copies(nxt, 1 - slot):
                    cp.start()

            for cp in weight_copies(e, slot):
                cp.wait()
            wgu_scr[...] = wgu_f32[slot].astype(BF16)
            wdn_scr[...] = wdn_f32[slot].astype(BF16)

        x = jnp.concatenate([_from_row_tiles(xs_ref, 0, rows, c).astype(BF16) for c in range(ROW_SUBLANES)],
                            axis=-1)
        gu = jnp.dot(x, wgu_scr[...], preferred_element_type=F32) + bgu_ref[0, 0]
        x_glu = jnp.minimum(gu[:, :d_ff], SWIGLU_LIMIT)
        x_lin = jnp.clip(gu[:, d_ff:], -SWIGLU_LIMIT, SWIGLU_LIMIT)
        hid = x_glu * _sigmoid(SWIGLU_ALPHA * x_glu) * (x_lin + 1.0)
        _to_row_tiles(ys_ref, 0, jnp.dot(hid.astype(BF16), wdn_scr[...], preferred_element_type=F32)
                      + bdn_ref[0, 0])

    @pl.when(b >= nused_ref[0])
    def _():
        ys_ref[...] = jnp.zeros_like(ys_ref)


def _ffn_call(layer, block_info, n_used, xs, w_gu, b_gu, w_dn, b_dn):
    depth, n_experts, d, d_ff2 = w_gu.shape
    d_ff = d_ff2 // 2
    blk_rows = MOE_BLOCK * ROW_SUBLANES
    n_blocks = xs.shape[0] // blk_rows

    def blk(b, nu):
        return jnp.maximum(jnp.minimum(b, nu[0] - 1), 0)

    grid_spec = pltpu.PrefetchScalarGridSpec(
        num_scalar_prefetch=2,
        grid=(n_blocks,),
        in_specs=[pl.BlockSpec((blk_rows, LANES), lambda b, bi, nu: (blk(b, nu), 0)),
                  pl.BlockSpec(memory_space=pl.ANY),
                  pl.BlockSpec((1, 1, 1, d_ff2), lambda b, bi, nu: (layer, bi[blk(b, nu)], 0, 0)),
                  pl.BlockSpec(memory_space=pl.ANY),
                  pl.BlockSpec((1, 1, 1, d), lambda b, bi, nu: (layer, bi[blk(b, nu)], 0, 0))],
        out_specs=pl.BlockSpec((blk_rows, LANES), lambda b, bi, nu: (b, 0)),
        scratch_shapes=[pltpu.VMEM((2, d, d_ff2), F32), pltpu.VMEM((2, d_ff, d), F32),
                        pltpu.VMEM((d, d_ff2), BF16), pltpu.VMEM((d_ff, d), BF16),
                        pltpu.SemaphoreType.DMA((2,))],
    )
    body = functools.partial(_ffn_kernel, layer=layer, n_experts=n_experts, n_blocks=n_blocks)
    return pl.pallas_call(
        body,
        grid_spec=grid_spec,
        out_shape=jax.ShapeDtypeStruct(xs.shape, F32),
        compiler_params=_params(("arbitrary",)),
        name="moe_ffn",
    )(block_info, n_used, xs, w_gu, b_gu.reshape(depth, n_experts, 1, d_ff2), w_dn,
      b_dn.reshape(depth, n_experts, 1, d))


def _gather_expert_rows(dest_ref, dest_next_ref, ys_ref, gate_ref, buf, sem, n_tiles):
    i = pl.program_id(0)
    tm = gate_ref.shape[0]
    rows = TOP_K * tm

    def start_gathers(d_ref, slot):
        def start_pair(p, carry):
            for u in range(2):
                j = 2 * p + u
                pltpu.make_async_copy(_row_tile(ys_ref, d_ref[0, 0, j]), _row_tile(buf.at[slot], j),
                                      sem.at[slot]).start(priority=u)
            return carry

        lax.fori_loop(0, rows // 2, start_pair, 0, unroll=DMA_ISSUE_UNROLL // 2)

    slot = lax.rem(i, 2)

    @pl.when(i == 0)
    def _():
        start_gathers(dest_ref, 0)

    @pl.when(i + 1 < n_tiles)
    def _():
        start_gathers(dest_next_ref, 1 - slot)

    pltpu.make_async_copy(ys_ref.at[pl.ds(0, rows * ROW_SUBLANES)], buf.at[slot], sem.at[slot]).wait()
    cur = buf.at[slot]
    chunks = []
    for c in range(ROW_SUBLANES):
        y = gate_ref[:, 0:1] * _from_row_tiles(cur, 0, tm, c)
        for k in range(1, TOP_K):
            y = y + gate_ref[:, k:k + 1] * _from_row_tiles(cur, k * tm, tm, c)
        chunks.append(y)
    return jnp.concatenate(chunks, axis=-1)


def _combine_kernel(dest_ref, dest_next_ref, ys_ref, gate_ref, x_ref, m_ref, fg_ref, y_ref, buf, sem,
                    *, n_tiles):
    y = _gather_expert_rows(dest_ref, dest_next_ref, ys_ref, gate_ref, buf, sem, n_tiles)
    y_ref[...] = _rms(x_ref[...] + m_ref[0, 5:6, :] * y, fg_ref[...])


def _combine_inproj_kernel(dest_ref, dest_next_ref, ys_ref, gate_ref, x_ref, mp_ref, mc_ref, g_ref, w_ref,
                           x_out_ref, p_ref, buf, sem, *, n_tiles):
    y = _gather_expert_rows(dest_ref, dest_next_ref, ys_ref, gate_ref, buf, sem, n_tiles)
    x = x_ref[...] + mp_ref[0, 5:6, :] * y
    x_out_ref[...] = x
    h = _rms(x, g_ref[...]) * (1.0 + mc_ref[0, 1:2, :]) + mc_ref[0, 0:1, :]
    p_ref[...] = _bdot(h, w_ref[...])


def _moe_gather_specs(tm, n_tiles, d):
    return [pl.BlockSpec((1, 1, TOP_K * tm), lambda i: (i, 0, 0), memory_space=pltpu.SMEM),
            pl.BlockSpec((1, 1, TOP_K * tm), lambda i: (jnp.minimum(i + 1, n_tiles - 1), 0, 0),
                         memory_space=pltpu.SMEM),
            pl.BlockSpec(memory_space=pl.ANY),
            pl.BlockSpec((tm, TOP_K), lambda i: (i, 0)),
            pl.BlockSpec((tm, d), lambda i: (i, 0))]


def _moe_gather_scratch(tm):
    return [pltpu.VMEM((2, TOP_K * tm * ROW_SUBLANES, LANES), F32), pltpu.SemaphoreType.DMA((2,))]


def _combine_inproj_call(pending, x, modsel_prev, modsel, g, w):
    dest_tiles, ys, gates_nk = pending
    n, d = x.shape
    wout = w.shape[1]
    tm = TOKEN_TILE
    n_tiles = n // tm
    body = functools.partial(_combine_inproj_kernel, n_tiles=n_tiles)
    return pl.pallas_call(
        body,
        grid=(n_tiles,),
        in_specs=_moe_gather_specs(tm, n_tiles, d)
                 + [pl.BlockSpec((1, 6, d), lambda i: (i, 0, 0)),
                    pl.BlockSpec((1, 6, d), lambda i: (i, 0, 0)),
                    pl.BlockSpec((1, d), lambda i: (0, 0)),
                    pl.BlockSpec((d, wout), lambda i: (0, 0))],
        out_specs=[pl.BlockSpec((tm, d), lambda i: (i, 0)),
                   pl.BlockSpec((tm, wout), lambda i: (i, 0))],
        out_shape=[jax.ShapeDtypeStruct((n, d), F32),
                   jax.ShapeDtypeStruct((n, wout), F32)],
        scratch_shapes=_moe_gather_scratch(tm),
        compiler_params=_params(("arbitrary",)),
        name="moe_combine_inproj",
    )(dest_tiles, dest_tiles, ys, gates_nk, x, modsel_prev, modsel, g.reshape(1, d), w)


def _combine_call(pending, x, modsel, final_g):
    dest_tiles, ys, gates_nk = pending
    n, d = x.shape
    tm = TOKEN_TILE
    n_tiles = n // tm
    body = functools.partial(_combine_kernel, n_tiles=n_tiles)
    return pl.pallas_call(
        body,
        grid=(n_tiles,),
        in_specs=_moe_gather_specs(tm, n_tiles, d)
                 + [pl.BlockSpec((1, 6, d), lambda i: (i, 0, 0)),
                    pl.BlockSpec((1, d), lambda i: (0, 0))],
        out_specs=pl.BlockSpec((tm, d), lambda i: (i, 0)),
        out_shape=jax.ShapeDtypeStruct((n, d), F32),
        scratch_shapes=_moe_gather_scratch(tm),
        compiler_params=_params(("arbitrary",)),
        name="moe_combine",
    )(dest_tiles, dest_tiles, ys, gates_nk, x, modsel, final_g.reshape(1, d))


def _moe_experts(layer, x, modsel, g2, w_router, b_router, w_gu, b_gu, w_dn, b_dn):
    n, d = x.shape
    n_experts = w_router.shape[1]
    tm = TOKEN_TILE
    n_blocks = (n * TOP_K) // MOE_BLOCK + n_experts
    idx_t, gate_t, rank_t, counts = _route_call(x, modsel, g2, w_router, b_router)
    dest_t, block_info, pad_info, n_used = _slot_call(counts, idx_t, rank_t, n_blocks)
    dest_tiles = dest_t.reshape(TOP_K, n // tm, tm).transpose(1, 0, 2).reshape(n // tm, 1, TOP_K * tm)
    xs = _dispatch_call(pad_info, dest_tiles, x, modsel, g2, n_blocks * MOE_BLOCK)
    ys = _ffn_call(layer, block_info.reshape(3 * n_blocks), n_used.reshape(1), xs, w_gu, b_gu, w_dn, b_dn)
    return dest_tiles, ys, gate_t.T


def _rope_tables(t, hd):
    pos = np.arange(t)
    n_freq = hd // 4
    inv_freq = ROPE_THETA ** (-np.arange(n_freq, dtype=np.float32) / n_freq)
    ang = np.concatenate([(pos // GRID_W).astype(np.float32)[:, None] * inv_freq,
                          (pos % GRID_W).astype(np.float32)[:, None] * inv_freq], axis=-1)
    ang = jnp.asarray(ang, F32)
    cos, sin = jnp.cos(ang), jnp.sin(ang)
    return jnp.concatenate([cos, cos], axis=-1), jnp.concatenate([-sin, sin], axis=-1)


def kernel(x_prompt, x_sample, cache_k_a, cache_v_a, state_b, state_c_C, state_c_n, state_c_m, cache_k_d, cache_v_d, c, c_ctx, norm1_g, norm2_g, w_mod, b_mod, w_in_a, qnorm_a, knorm_a, w_out_a, w_in_b, b_f_b, lower_bounds_b, onorm_b, w_out_b, w_in_c, b_gates_c, onorm_c, w_out_c, w_in_d, rpb_d, w_out_d, w_router, b_router, w_gu, b_gu, w_dn, b_dn, final_g):
    n_ctx_seq, t_ctx, d = x_prompt.shape
    n_lat_seq, t_lat, _ = x_sample.shape
    depth = w_mod.shape[0]
    n_ctx = n_ctx_seq * t_ctx
    n_lat = n_lat_seq * t_lat
    n = n_ctx + n_lat
    tm = TOKEN_TILE
    assert t_ctx % tm == 0 and t_lat % tm == 0 and n_lat_seq + 1 <= 8

    lb_cum = jnp.cumsum(jax.nn.softmax(lower_bounds_b.astype(F32), axis=0), axis=0)
    lb_all = lb_cum - lb_cum[0]

    cond8 = jnp.zeros((8, d), F32).at[0].set(c_ctx).at[1:1 + n_lat_seq].set(c)
    mod = _mod_call(cond8, w_mod, b_mod)
    tile_row = np.concatenate([np.zeros(n_ctx // tm, np.int32),
                               1 + np.repeat(np.arange(n_lat_seq, dtype=np.int32), t_lat // tm)])

    x = jnp.concatenate([x_prompt.reshape(n_ctx, d), x_sample.reshape(n_lat, d)], axis=0)
    outs = {}
    pending = None
    modsel = None

    def inproj(x, modsel, g, w):
        if pending is None:
            return x, _inproj_call(x, modsel, g, w)
        return _combine_inproj_call(pending, x, modsel_prev, modsel, g, w)

    for i in range(depth):
        kind = i % 4
        j = i // 4
        modsel_prev = modsel
        modsel = mod[i].reshape(8, 6, d)[tile_row]
        if kind == 0:
            x, proj = inproj(x, modsel, norm1_g[i],w_in_a[j].astype(BF16))
            o_ctx, k_new = _gqa_ctx_call(proj, qnorm_a[j], knorm_a[j], n_ctx_seq, t_ctx)
            cosd, sind = _rope_tables(t_lat, 128)
            o_lat = _gqa_lat_call(proj, n_ctx, n_lat_seq, t_lat,
                                  cache_k_a[:, j].reshape(n_lat_seq, -1, 256),
                                  cache_v_a[:, j].reshape(n_lat_seq, -1, 256), cosd, sind,
                                  qnorm_a[j], knorm_a[j])
            outs["k_a"] = k_new.reshape(n_ctx_seq, 1, t_ctx, 2, 128)
            outs["v_a"] = proj[:n_ctx, 1280:1536].reshape(n_ctx_seq, 1, t_ctx, 2, 128)
            x = _outproj_call("plain", [(o_ctx, o_lat)], None, w_out_a[j].astype(BF16), x, modsel)
        elif kind == 1:
            x, proj = inproj(x, modsel, norm1_g[i],w_in_b[j].astype(BF16))
            o_dirs, s_dirs = [], []
            for reverse in (False, True):
                oc, sc = _hgrn_call(proj, 0, n_ctx_seq, t_ctx, b_f_b[j], lb_all[i], None, reverse)
                ol, _ = _hgrn_call(proj, n_ctx, n_lat_seq, t_lat, b_f_b[j], lb_all[i], state_b[:, j], reverse)
                o_dirs.append((oc, ol))
                s_dirs.append(sc)
            outs["s_b"] = jnp.concatenate(s_dirs, axis=1)[:, None]
            x = _outproj_call("hgrn", o_dirs, (proj, 4, onorm_b[j].reshape(1, 128)),
                              w_out_b[j].astype(BF16), x, modsel)
        elif kind == 2:
            w_c = jnp.pad(w_in_c[j], ((0, 0), (0, 128 - 32))).astype(BF16)
            bg = jnp.pad(b_gates_c[j].reshape(1, 32), ((0, 0), (0, 128 - 32)))
            x, proj = inproj(x, modsel, norm1_g[i],w_c)
            state = (state_c_C[:, j], state_c_n[:, j][:, :, :, None, :],
                     jnp.pad(state_c_m[:, j], ((0, 0), (0, 0), (0, 120)))[:, :, None, :])
            o_dirs, st = [], []
            for reverse in (False, True):
                oc, cc, nc, mc = _mlstm_call(proj, 0, n_ctx_seq, t_ctx, bg, None, reverse)
                ol, _, _, _ = _mlstm_call(proj, n_ctx, n_lat_seq, t_lat, bg, state, reverse)
                o_dirs.append((oc, ol))
                st.append((cc, nc, mc))
            outs["c_C"] = jnp.concatenate([st[0][0], st[1][0]], axis=1)[:, None]
            outs["c_n"] = jnp.concatenate([st[0][1], st[1][1]], axis=1)[:, None, :, :, 0, :]
            outs["c_m"] = jnp.concatenate([st[0][2], st[1][2]], axis=1)[:, None, :, 0, :8]
            x = _outproj_call("mlstm", o_dirs, (proj, 2, onorm_c[j].reshape(1, 128)),
                              w_out_c[j].astype(BF16), x, modsel)
        else:
            x, proj = inproj(x, modsel, norm1_g[i],w_in_d[j].astype(BF16))
            o_ctx = _mha_ctx_call(proj, n_ctx_seq, t_ctx)
            tz = _na_bias_call(rpb_d[j])
            o_lat = _na_call(proj, n_ctx, n_lat_seq, t_lat,
                             cache_k_d[:, j].reshape(n_lat_seq, -1, d),
                             cache_v_d[:, j].reshape(n_lat_seq, -1, d), tz)
            outs["k_d"] = proj[:n_ctx, d:2 * d].reshape(n_ctx_seq, 1, t_ctx, 16, 64)
            outs["v_d"] = proj[:n_ctx, 2 * d:3 * d].reshape(n_ctx_seq, 1, t_ctx, 16, 64)
            x = _outproj_call("plain", [(o_ctx, o_lat)], None, w_out_d[j].astype(BF16), x, modsel)
        pending = _moe_experts(i, x, modsel, norm2_g[i], w_router[i], b_router[i], w_gu, b_gu, w_dn, b_dn)
    x = _combine_call(pending, x, modsel, final_g)

    y_prompt = x[:n_ctx].reshape(n_ctx_seq, t_ctx, d)
    y_sample = x[n_ctx:].reshape(n_lat_seq, t_lat, d)
    return (y_prompt, y_sample, outs["k_a"], outs["v_a"], outs["s_b"], outs["c_C"], outs["c_n"], outs["c_m"],
            outs["k_d"], outs["v_d"])
```

```python
import functools

import numpy as np
import jax
import jax.numpy as jnp
from jax import lax
from jax.experimental import pallas as pl
from jax.experimental.pallas import tpu as pltpu

F32 = jnp.float32
BF16 = jnp.bfloat16
I32 = jnp.int32

NORM_EPS = 1e-6
GRID_W = 64
ROPE_THETA = 10000.0
TOP_K = 4
HGRN_CHUNK = 128
GATE_SOFTCAP = 15.0
NA_ROWS = 8
NA_COLS = 16
SWIGLU_ALPHA = 1.702
SWIGLU_LIMIT = 7.0
NEG_BIG = -1e30

LANES = 128
ROW_SUBLANES = 8
TOKEN_TILE = 256
MOE_BLOCK = 256
ATTN_HEAD_GROUP = 8
V7X_VMEM_LIMIT = 52 * 1024 * 1024


def _params(sem, vmem=V7X_VMEM_LIMIT):
    return pltpu.CompilerParams(dimension_semantics=sem, vmem_limit_bytes=vmem)


def _bdot(a, b):
    return jnp.dot(a.astype(BF16), b.astype(BF16), preferred_element_type=F32)


def _bdot_nt(a, b):
    return lax.dot_general(a.astype(BF16), b.astype(BF16), (((1,), (1,)), ((), ())),
                           preferred_element_type=F32)


def _bdot_tn(a, b):
    return lax.dot_general(a.astype(BF16), b.astype(BF16), (((0,), (0,)), ((), ())),
                           preferred_element_type=F32)


def _fdot(a, b):
    return jnp.dot(a, b, preferred_element_type=F32, precision=lax.Precision.HIGHEST)


def _rms(x, g):
    return x * lax.rsqrt(jnp.mean(x * x, axis=-1, keepdims=True) + NORM_EPS) * g


def _rms_heads(x, g, n_heads, hd):
    return jnp.concatenate([_rms(x[:, h * hd:(h + 1) * hd], g) for h in range(n_heads)], axis=-1)


def _sigmoid(x):
    return 1.0 / (1.0 + jnp.exp(-x))


def _silu(x):
    return x * _sigmoid(x)


def _softmax_rows(parts):
    m = parts[0].max(axis=-1, keepdims=True)
    for p in parts[1:]:
        m = jnp.maximum(m, p.max(axis=-1, keepdims=True))
    es = [jnp.exp(p - m) for p in parts]
    den = es[0].sum(axis=-1, keepdims=True)
    for e in es[1:]:
        den = den + e.sum(axis=-1, keepdims=True)
    return es, den


def _mod_kernel(c_ref, w_ref, b_ref, o_ref):
    o_ref[0] = _bdot(_silu(c_ref[...]), w_ref[0]) + b_ref[0]


def _mod_call(cond8, w_mod, b_mod):
    depth, d, d6 = w_mod.shape
    tn = 1024
    return pl.pallas_call(
        _mod_kernel,
        grid=(depth, d6 // tn),
        in_specs=[pl.BlockSpec((8, d), lambda i, j: (0, 0)),
                  pl.BlockSpec((1, d, tn), lambda i, j: (i, 0, j)),
                  pl.BlockSpec((1, 1, tn), lambda i, j: (i, 0, j))],
        out_specs=pl.BlockSpec((1, 8, tn), lambda i, j: (i, 0, j)),
        out_shape=jax.ShapeDtypeStruct((depth, 8, d6), F32),
        compiler_params=_params(("arbitrary", "arbitrary")),
        name="adaln_mod",
    )(cond8, w_mod, b_mod.reshape(depth, 1, d6))


def _inproj_kernel(x_ref, m_ref, g_ref, w_ref, o_ref):
    h = _rms(x_ref[...], g_ref[...]) * (1.0 + m_ref[0, 1:2, :]) + m_ref[0, 0:1, :]
    o_ref[...] = _bdot(h, w_ref[...])


def _inproj_call(x, modsel, g, w):
    n, d = x.shape
    wout = w.shape[1]
    tm = TOKEN_TILE
    return pl.pallas_call(
        _inproj_kernel,
        grid=(n // tm,),
        in_specs=[pl.BlockSpec((tm, d), lambda i: (i, 0)),
                  pl.BlockSpec((1, 6, d), lambda i: (i, 0, 0)),
                  pl.BlockSpec((1, d), lambda i: (0, 0)),
                  pl.BlockSpec((d, wout), lambda i: (0, 0))],
        out_specs=pl.BlockSpec((tm, wout), lambda i: (i, 0)),
        out_shape=jax.ShapeDtypeStruct((n, wout), F32),
        compiler_params=_params(("arbitrary",)),
        name="inproj",
    )(x, modsel, g.reshape(1, d), w)


def _outproj_kernel(*refs, kind, n_pairs, n_ctx_tiles, n_heads, hd):
    in_ctx = pl.program_id(0) < n_ctx_tiles
    mix = [jnp.where(in_ctx, refs[2 * p][...], refs[2 * p + 1][...]) for p in range(n_pairs)]
    rest = refs[2 * n_pairs:]
    if kind == "plain":
        w_ref, x_ref, m_ref, y_ref = rest
        o = mix[0]
    else:
        gate_ref, on_ref, w_ref, x_ref, m_ref, y_ref = rest
        o = _rms_heads(mix[0] + mix[1], on_ref[...], n_heads, hd)
        o = o * _silu(gate_ref[...]) if kind == "hgrn" else _sigmoid(gate_ref[...]) * o
    y_ref[...] = x_ref[...] + m_ref[0, 2:3, :] * _bdot(o, w_ref[...])


def _outproj_call(kind, pairs, gate, w, x, modsel):
    n, d = x.shape
    tm = TOKEN_TILE
    n_ctx_tiles = pairs[0][0].shape[0] // tm
    specs, args = [], []
    for a_ctx, a_lat in pairs:
        specs += [pl.BlockSpec((tm, d), lambda i: (jnp.minimum(i, n_ctx_tiles - 1), 0)),
                  pl.BlockSpec((tm, d), lambda i: (jnp.maximum(i - n_ctx_tiles, 0), 0))]
        args += [a_ctx, a_lat]
    if gate is not None:
        proj, cb, on = gate
        specs += [pl.BlockSpec((tm, d), lambda i: (i, cb)), pl.BlockSpec(on.shape, lambda i: (0, 0))]
        args += [proj, on]
    specs += [pl.BlockSpec(w.shape, lambda i: (0, 0)),
              pl.BlockSpec((tm, d), lambda i: (i, 0)),
              pl.BlockSpec((1, 6, d), lambda i: (i, 0, 0))]
    args += [w, x, modsel]
    body = functools.partial(_outproj_kernel, kind=kind, n_pairs=len(pairs), n_ctx_tiles=n_ctx_tiles,
                             n_heads=8, hd=128)
    return pl.pallas_call(
        body,
        grid=(n // tm,),
        in_specs=specs,
        out_specs=pl.BlockSpec((tm, d), lambda i: (i, 0)),
        out_shape=jax.ShapeDtypeStruct((n, d), F32),
        compiler_params=_params(("arbitrary",)),
        name="outproj_" + kind,
    )(*args)


def _gqa_ctx_kernel(p_ref, qn_ref, kn_ref, o_ref, k_ref, *, n_heads, n_kv, hd):
    rep = n_heads // n_kv
    scale = hd ** -0.5
    koff = n_heads * hd
    voff = koff + n_kv * hd
    ks = [_rms(p_ref[:, koff + g * hd: koff + (g + 1) * hd], kn_ref[...]) for g in range(n_kv)]
    k_ref[...] = jnp.concatenate(ks, axis=-1)
    for h in range(n_heads):
        g = h // rep
        q = _rms(p_ref[:, h * hd:(h + 1) * hd], qn_ref[...])
        s = _bdot_nt(q, ks[g]) * scale
        (e,), den = _softmax_rows([s])
        o_ref[:, h * hd:(h + 1) * hd] = _bdot(e / den, p_ref[:, voff + g * hd: voff + (g + 1) * hd])


def _gqa_ctx_call(proj, qn, kn, n_seq, t):
    n_heads, n_kv, hd = 8, 2, 128
    win = proj.shape[1]
    body = functools.partial(_gqa_ctx_kernel, n_heads=n_heads, n_kv=n_kv, hd=hd)
    return pl.pallas_call(
        body,
        grid=(n_seq,),
        in_specs=[pl.BlockSpec((t, win), lambda b: (b, 0)),
                  pl.BlockSpec((1, hd), lambda b: (0, 0)),
                  pl.BlockSpec((1, hd), lambda b: (0, 0))],
        out_specs=[pl.BlockSpec((t, n_heads * hd), lambda b: (b, 0)),
                   pl.BlockSpec((t, n_kv * hd), lambda b: (b, 0))],
        out_shape=[jax.ShapeDtypeStruct((n_seq * t, n_heads * hd), F32),
                   jax.ShapeDtypeStruct((n_seq * t, n_kv * hd), F32)],
        compiler_params=_params(("arbitrary",)),
        name="gqa_ctx",
    )(proj, qn.reshape(1, hd), kn.reshape(1, hd))


def _rope(x, cosd, sind):
    return x * cosd + pltpu.roll(x, x.shape[-1] // 2, 1) * sind


def _gqa_lat_kernel(pq_ref, pkv_ref, kc_ref, vc_ref, cq_ref, sq_ref, ck_ref, sk_ref, qn_ref, kn_ref,
                    o_ref, k_scr, v_scr, *, n_heads, n_kv, hd, t_ctx):
    rep = n_heads // n_kv
    scale = hd ** -0.5

    @pl.when(pl.program_id(1) == 0)
    def _():
        k_scr[0:t_ctx, :] = kc_ref[0].astype(BF16)
        v_scr[...] = jnp.ones_like(v_scr)
        for g in range(n_kv):
            k = _rms(pkv_ref[:, g * hd:(g + 1) * hd], kn_ref[...])
            k_scr[t_ctx:, g * hd:(g + 1) * hd] = _rope(k, ck_ref[...], sk_ref[...]).astype(BF16)
            v_scr[0:t_ctx, 2 * g * hd:(2 * g + 1) * hd] = vc_ref[0, :, g * hd:(g + 1) * hd].astype(BF16)
            v_scr[t_ctx:, 2 * g * hd:(2 * g + 1) * hd] = pkv_ref[:, (n_kv + g) * hd:(n_kv + g + 1) * hd].astype(BF16)

    qs = [(_rope(_rms(pq_ref[:, h * hd:(h + 1) * hd], qn_ref[...]), cq_ref[...], sq_ref[...]) * scale).astype(BF16)
          for h in range(n_heads)]
    for h0 in range(0, n_heads, ATTN_HEAD_GROUP):
        hs = range(h0, min(h0 + ATTN_HEAD_GROUP, n_heads))
        s = [_bdot_nt(qs[h], k_scr[:, (h // rep) * hd:(h // rep + 1) * hd]) for h in hs]
        e = [jnp.exp(x - x.max(axis=-1, keepdims=True)).astype(BF16) for x in s]
        pv = [jnp.dot(e[i], v_scr[:, 2 * (h // rep) * hd:2 * (h // rep + 1) * hd], preferred_element_type=F32)
              for i, h in enumerate(hs)]
        for i, h in enumerate(hs):
            o_ref[:, h * hd:(h + 1) * hd] = pv[i][:, :hd] / pv[i][:, hd:]


def _gqa_lat_call(proj, row0, n_seq, t, cache_k, cache_v, cosd, sind, qn, kn):
    n_heads, n_kv, hd = 8, 2, 128
    tq = 256
    t_ctx = cache_k.shape[1]
    nq = t // tq
    kvw = 2 * n_kv * hd
    qblk0 = row0 // tq
    sblk0 = row0 // t
    body = functools.partial(_gqa_lat_kernel, n_heads=n_heads, n_kv=n_kv, hd=hd, t_ctx=t_ctx)
    return pl.pallas_call(
        body,
        grid=(n_seq, nq),
        in_specs=[pl.BlockSpec((tq, n_heads * hd), lambda b, i: (qblk0 + b * nq + i, 0)),
                  pl.BlockSpec((t, kvw), lambda b, i: (sblk0 + b, (n_heads * hd) // kvw)),
                  pl.BlockSpec((1, t_ctx, n_kv * hd), lambda b, i: (b, 0, 0)),
                  pl.BlockSpec((1, t_ctx, n_kv * hd), lambda b, i: (b, 0, 0)),
                  pl.BlockSpec((tq, hd), lambda b, i: (i, 0)),
                  pl.BlockSpec((tq, hd), lambda b, i: (i, 0)),
                  pl.BlockSpec((t, hd), lambda b, i: (0, 0)),
                  pl.BlockSpec((t, hd), lambda b, i: (0, 0)),
                  pl.BlockSpec((1, hd), lambda b, i: (0, 0)),
                  pl.BlockSpec((1, hd), lambda b, i: (0, 0))],
        out_specs=pl.BlockSpec((tq, n_heads * hd), lambda b, i: (b * nq + i, 0)),
        out_shape=jax.ShapeDtypeStruct((n_seq * t, n_heads * hd), F32),
        scratch_shapes=[pltpu.VMEM((t_ctx + t, n_kv * hd), BF16),
                        pltpu.VMEM((t_ctx + t, 2 * n_kv * hd), BF16)],
        compiler_params=_params(("arbitrary", "arbitrary")),
        name="gqa_latent",
    )(proj, proj, cache_k, cache_v, cosd, sind, cosd, sind, qn.reshape(1, hd), kn.reshape(1, hd))


def _mha_ctx_kernel(q_ref, k_ref, v_ref, o_ref, *, hd):
    scale = hd ** -0.5
    n_heads = q_ref.shape[1] // hd
    ones = jnp.ones((v_ref.shape[0], hd), BF16)
    for h0 in range(0, n_heads, ATTN_HEAD_GROUP):
        sls = [slice(h * hd, (h + 1) * hd) for h in range(h0, min(h0 + ATTN_HEAD_GROUP, n_heads))]
        s = [_bdot_nt(q_ref[:, sl] * scale, k_ref[:, sl]) for sl in sls]
        e = [jnp.exp(x - x.max(axis=-1, keepdims=True)).astype(BF16) for x in s]
        pv = [jnp.dot(e[i], jnp.concatenate([v_ref[:, sl].astype(BF16), ones], axis=-1),
                      preferred_element_type=F32) for i, sl in enumerate(sls)]
        for i, sl in enumerate(sls):
            o_ref[:, sl] = pv[i][:, :hd] / pv[i][:, hd:]


def _mha_ctx_call(proj, n_seq, t):
    hd = 64
    d = proj.shape[1] // 3
    body = functools.partial(_mha_ctx_kernel, hd=hd)
    return pl.pallas_call(
        body,
        grid=(n_seq,),
        in_specs=[pl.BlockSpec((t, d), lambda b: (b, 0)),
                  pl.BlockSpec((t, d), lambda b: (b, 1)),
                  pl.BlockSpec((t, d), lambda b: (b, 2))],
        out_specs=pl.BlockSpec((t, d), lambda b: (b, 0)),
        out_shape=jax.ShapeDtypeStruct((n_seq * t, d), F32),
        compiler_params=_params(("arbitrary",)),
        name="mha_ctx",
    )(proj, proj, proj)


def _na_bias_kernel(rpb_ref, o_ref, *, n_rel_rows, n_rel_cols):
    h = pl.program_id(0)
    w_io = lax.broadcasted_iota(I32, (GRID_W, 2 * GRID_W), 0)
    lane = lax.broadcasted_iota(I32, (GRID_W, 2 * GRID_W), 1)
    ck = jnp.where(lane < GRID_W, lane, lane - GRID_W)
    c_start = jnp.clip(w_io - NA_COLS // 2, 0, GRID_W - NA_COLS)
    in_win = (ck >= c_start) & (ck < c_start + NA_COLS)
    rel = ck - w_io + (NA_COLS - 1)
    base = h * (n_rel_rows * n_rel_cols)
    tiles = []
    for j in range(n_rel_rows):
        acc = jnp.zeros((GRID_W, 2 * GRID_W), F32)
        for jj in range(n_rel_cols):
            acc = jnp.where(rel == jj, rpb_ref[base + j * n_rel_cols + jj], acc)
        tiles.append(jnp.where(in_win, acc, NEG_BIG))
    for j in range(n_rel_rows):
        hi = tiles[j + 1] if j + 1 < n_rel_rows else jnp.full((GRID_W, 2 * GRID_W), NEG_BIG, F32)
        o_ref[0, j] = jnp.where(lane < GRID_W, tiles[j], hi)


def _na_bias_call(rpb):
    n_heads, nrr, nrc = rpb.shape
    body = functools.partial(_na_bias_kernel, n_rel_rows=nrr, n_rel_cols=nrc)
    return pl.pallas_call(
        body,
        grid=(n_heads,),
        in_specs=[pl.BlockSpec(memory_space=pltpu.SMEM)],
        out_specs=pl.BlockSpec((1, nrr, GRID_W, 2 * GRID_W), lambda h: (h, 0, 0, 0)),
        out_shape=jax.ShapeDtypeStruct((n_heads, nrr, GRID_W, 2 * GRID_W), F32),
        compiler_params=_params(("arbitrary",)),
        name="na_bias",
    )(rpb.reshape(-1))


NA_QROWS = 4
NA_KROWS = 12


def _na_kernel(q_ref, k0_ref, k1_ref, k2_ref, v0_ref, v1_ref, v2_ref, kc_ref, vc_ref, tz_ref, o_ref,
               *, hd, n_grid_rows):
    scale = hd ** -0.5
    blk = pl.program_id(1)
    kstart = jnp.clip(blk * NA_QROWS - NA_ROWS // 2, 0, n_grid_rows - NA_KROWS)
    lane = lax.broadcasted_iota(I32, (GRID_W, 2 * GRID_W), 1)
    n_rel = tz_ref.shape[1]
    heads = range(q_ref.shape[1] // hd)
    sls = [slice(hh * hd, (hh + 1) * hd) for hh in heads]
    rel, pen = [], []
    for rq_l in range(NA_QROWS):
        rq = blk * NA_QROWS + rq_l
        r_start = jnp.clip(rq - NA_ROWS // 2, 0, n_grid_rows - NA_ROWS)
        rel.append([])
        pen.append([])
        for m in range(NA_KROWS // 2):
            rk = kstart + 2 * m
            rel[-1].append(jnp.clip(rk - rq + (NA_ROWS - 1), 0, n_rel - 1))
            ok0 = (rk >= r_start) & (rk < r_start + NA_ROWS)
            ok1 = (rk + 1 >= r_start) & (rk + 1 < r_start + NA_ROWS)
            pen[-1].append(jnp.where(lane < GRID_W, jnp.where(ok0, 0.0, NEG_BIG), jnp.where(ok1, 0.0, NEG_BIG)))
    bias = [jnp.concatenate([jnp.concatenate([tz_ref[hh, pl.ds(rel[r][m], 1)][0] + pen[r][m]
                                              for m in range(NA_KROWS // 2)], axis=-1)
                             for r in range(NA_QROWS)], axis=0) for hh in heads]
    q = [q_ref[:, sl] * scale for sl in sls]
    k_loc = [jnp.concatenate([k0_ref[:, sl], k1_ref[:, sl], k2_ref[:, sl]], axis=0) for sl in sls]
    v_loc = [jnp.concatenate([v0_ref[:, sl], v1_ref[:, sl], v2_ref[:, sl]], axis=0).astype(BF16) for sl in sls]
    v_ctx = [vc_ref[0, :, sl].astype(BF16) for sl in sls]
    s_loc = [_bdot_nt(q[h], k_loc[h]) + bias[h] for h in heads]
    s_ctx = [_bdot_nt(q[h], kc_ref[0, :, sls[h]]) for h in heads]
    m = [jnp.maximum(s_loc[h].max(axis=-1, keepdims=True), s_ctx[h].max(axis=-1, keepdims=True)) for h in heads]
    e_loc = [jnp.exp(s_loc[h] - m[h]).astype(BF16) for h in heads]
    e_ctx = [jnp.exp(s_ctx[h] - m[h]).astype(BF16) for h in heads]
    pv = [jnp.dot(e_loc[h], jnp.concatenate([v_loc[h], jnp.ones_like(v_loc[h])], axis=-1),
                  preferred_element_type=F32)
          + jnp.dot(e_ctx[h], jnp.concatenate([v_ctx[h], jnp.ones_like(v_ctx[h])], axis=-1),
                    preferred_element_type=F32) for h in heads]
    for h in heads:
        o_ref[:, sls[h]] = pv[h][:, :hd] / pv[h][:, hd:]


def _na_call(proj, row0, n_seq, t, cache_k, cache_v, tz):
    hd, cw = 64, 256
    d = proj.shape[1] // 3
    ncb = d // cw
    tq = NA_QROWS * GRID_W
    nq = t // tq
    n_grid_rows = t // GRID_W
    t_ctx = cache_k.shape[1]
    qblk0 = row0 // tq

    def kv_map(which, j):
        def index_map(b, i, c):
            ks = jnp.clip(i * NA_QROWS - NA_ROWS // 2, 0, n_grid_rows - NA_KROWS) // NA_QROWS
            return (qblk0 + b * nq + ks + j, which * ncb + c)
        return index_map

    body = functools.partial(_na_kernel, hd=hd, n_grid_rows=n_grid_rows)
    return pl.pallas_call(
        body,
        grid=(n_seq, nq, ncb),
        in_specs=[pl.BlockSpec((tq, cw), lambda b, i, c: (qblk0 + b * nq + i, c))]
                 + [pl.BlockSpec((tq, cw), kv_map(1, j)) for j in range(3)]
                 + [pl.BlockSpec((tq, cw), kv_map(2, j)) for j in range(3)]
                 + [pl.BlockSpec((1, t_ctx, cw), lambda b, i, c: (b, 0, c)),
                    pl.BlockSpec((1, t_ctx, cw), lambda b, i, c: (b, 0, c)),
                    pl.BlockSpec((cw // hd,) + tz.shape[1:], lambda b, i, c: (c, 0, 0, 0))],
        out_specs=pl.BlockSpec((tq, cw), lambda b, i, c: (b * nq + i, c)),
        out_shape=jax.ShapeDtypeStruct((n_seq * t, d), F32),
        compiler_params=_params(("arbitrary", "arbitrary", "arbitrary")),
        name="nbr_attn",
    )(proj, proj, proj, proj, proj, proj, proj, cache_k, cache_v, tz)


def _hgrn_kernel(q_ref, v_ref, f_ref, bf_ref, lb_ref, s0_ref, o_ref, s_ref, st_scr,
                 *, reverse, n_heads, dk, has_s0, n_blk):
    c = pl.program_id(1)
    L = HGRN_CHUNK
    tb = q_ref.shape[0]

    @pl.when(c == 0)
    def _():
        for h in range(n_heads):
            if has_s0:
                st_scr[h] = s0_ref[0, 0, h].T
            else:
                st_scr[h] = jnp.zeros_like(st_scr[h])

    row = lax.broadcasted_iota(I32, (L, L), 0)
    col = lax.broadcasted_iota(I32, (L, L), 1)
    tri = jnp.where((col >= row) if reverse else (col <= row), 1.0, 0.0).astype(F32)
    eye = jnp.where(row == col, 1.0, 0.0).astype(F32)
    halves = [L >> (i + 1) for i in range(L.bit_length() - 1)]
    same_pair = [jnp.where((row // (2 * hf)) == (col // (2 * hf)), 1.0, 0.0).astype(F32) for hf in halves]
    rio = lax.broadcasted_iota(I32, (L, q_ref.shape[1]), 0)
    is_query = [jnp.where(((rio & hf) == 0) if reverse else ((rio & hf) != 0), 1.0, 0.0).astype(F32)
                for hf in halves]
    r8 = lax.broadcasted_iota(I32, (ROW_SUBLANES, 1), 0)
    lb = lb_ref[0]
    bf = bf_ref[0]
    n_chunks = tb // L

    def boundary_rows(cum, hf):
        blk = 2 * hf
        off = hf if reverse else hf - 1
        width = cum.shape[1]
        if blk >= ROW_SUBLANES:
            return jnp.concatenate([jnp.broadcast_to(cum[a + off:a + off + 1, :], (blk, width))
                                    for a in range(0, L, blk)], axis=0)
        groups = []
        for g in range(0, L, ROW_SUBLANES):
            ref = jnp.broadcast_to(cum[g + off:g + off + 1, :], (ROW_SUBLANES, width))
            for a in range(blk, ROW_SUBLANES, blk):
                ref = jnp.where(r8 >= a, jnp.broadcast_to(cum[g + a + off:g + a + off + 1, :],
                                                          (ROW_SUBLANES, width)), ref)
            groups.append(ref)
        return jnp.concatenate(groups, axis=0)

    def chunk_step(jj, carry):
        jc = (n_chunks - 1 - jj) if reverse else jj
        rs = pl.ds(pl.multiple_of(jc * L, L), L)
        q = _silu(q_ref[rs, :]) * (dk ** -0.5)
        v = v_ref[rs, :]
        f = lb + (1.0 - lb) * _sigmoid(f_ref[rs, :] + bf)
        logf = jnp.log(f)
        kk = 1.0 - f
        cum = _fdot(tri, logf)
        end = cum[0:1, :] if reverse else cum[L - 1:L, :]
        qd = q * jnp.exp(cum)
        kd = kk * jnp.exp(end - cum)
        e_end = jnp.exp(end)
        q_lv, k_lv = [], []
        for lv, hf in enumerate(halves):
            isq = is_query[lv]
            decay = jnp.exp((2.0 * isq - 1.0) * (cum - boundary_rows(cum, hf)))
            dq = decay * isq
            q_lv.append((q * dq).astype(BF16))
            k_lv.append((kk * (decay - dq)).astype(BF16))
        qk_diag = q * kk
        hsl = [slice(h * dk, (h + 1) * dk) for h in range(n_heads)]
        inter = [_bdot_nt(qd[:, hs], st_scr[h]) for h, hs in enumerate(hsl)]
        attn = [eye * jnp.sum(qk_diag[:, hs], axis=-1, keepdims=True) for hs in hsl]
        for lv in range(len(halves)):
            prods = [_bdot_nt(q_lv[lv][:, hs], k_lv[lv][:, hs]) for hs in hsl]
            attn = [attn[h] + same_pair[lv] * prods[h] for h in range(n_heads)]
        outs = [inter[h] + _bdot(attn[h], v[:, hs]) for h, hs in enumerate(hsl)]
        o_ref[rs, :] = jnp.concatenate(outs, axis=-1)
        for h in range(n_heads):
            hs = slice(h * dk, (h + 1) * dk)
            st_scr[h] = st_scr[h] * e_end[:, hs] + _bdot_tn(v[:, hs], kd[:, hs])
        return carry

    lax.fori_loop(0, n_chunks, chunk_step, 0)

    @pl.when(c == n_blk - 1)
    def _():
        for h in range(n_heads):
            s_ref[0, 0, h] = st_scr[h].T


def _hgrn_call(proj, row0, n_seq, t, b_f, lb, s0, reverse):
    n_heads, dk = 8, 128
    d = n_heads * dk
    tb = 256
    n_blk = t // tb
    blk0 = row0 // tb
    di = 1 if reverse else 0
    has_s0 = s0 is not None
    if not has_s0:
        s0 = jnp.zeros((1, 2, n_heads, dk, dk), F32)

    def tok(b, c):
        return blk0 + b * n_blk + ((n_blk - 1 - c) if reverse else c)

    body = functools.partial(_hgrn_kernel, reverse=reverse, n_heads=n_heads, dk=dk,
                             has_s0=has_s0, n_blk=n_blk)
    return pl.pallas_call(
        body,
        grid=(n_seq, n_blk),
        in_specs=[pl.BlockSpec((tb, d), lambda b, c: (tok(b, c), 0)),
                  pl.BlockSpec((tb, d), lambda b, c: (tok(b, c), 1)),
                  pl.BlockSpec((tb, d), lambda b, c: (tok(b, c), 2 + di)),
                  pl.BlockSpec((1, 1, d), lambda b, c: (di, 0, 0)),
                  pl.BlockSpec((1, 1, d), lambda b, c: (di, 0, 0)),
                  pl.BlockSpec((1, 1, n_heads, dk, dk),
                               (lambda b, c: (b, di, 0, 0, 0)) if has_s0 else (lambda b, c: (0, 0, 0, 0, 0)))],
        out_specs=[pl.BlockSpec((tb, d), lambda b, c: (tok(b, c) - blk0, 0)),
                   pl.BlockSpec((1, 1, n_heads, dk, dk), lambda b, c: (b, 0, 0, 0, 0))],
        out_shape=[jax.ShapeDtypeStruct((n_seq * t, d), F32),
                   jax.ShapeDtypeStruct((n_seq, 1, n_heads, dk, dk), F32)],
        scratch_shapes=[pltpu.VMEM((n_heads, dk, dk), F32)],
        compiler_params=_params(("arbitrary", "arbitrary")),
        name="hgrn2_bw" if reverse else "hgrn2_fw",
    )(proj, proj, proj, b_f.reshape(2, 1, d), lb.reshape(2, 1, d), s0)


def _log_sigmoid(x):
    return jnp.minimum(x, 0.0) - jnp.log(1.0 + jnp.exp(-jnp.abs(x)))


def _mlstm_kernel(q_ref, k_ref, v_ref, g_ref, bg_ref, c0_ref, n0_ref, m0_ref,
                  o_ref, c_out, n_out, m_out, c_scr, n_scr, m_scr,
                  *, reverse, n_heads, dqk, dv, has_state, n_blk):
    c = pl.program_id(1)
    L = q_ref.shape[0]
    i_off = 2 * n_heads if reverse else 0
    f_off = i_off + n_heads

    @pl.when(c == 0)
    def _():
        if has_state:
            c_scr[...] = c0_ref[0, 0]
            n_scr[...] = n0_ref[0, 0]
            m_scr[...] = m0_ref[0, 0]
        else:
            c_scr[...] = jnp.zeros_like(c_scr)
            n_scr[...] = jnp.zeros_like(n_scr)
            m_scr[...] = jnp.zeros_like(m_scr)

    gates = GATE_SOFTCAP * jnp.tanh((g_ref[...] + bg_ref[...]) / GATE_SOFTCAP)
    logf = _log_sigmoid(gates)
    row = lax.broadcasted_iota(I32, (L, L), 0)
    col = lax.broadcasted_iota(I32, (L, L), 1)
    causal = (col >= row) if reverse else (col <= row)
    tri = jnp.where(causal, 1.0, 0.0).astype(F32)
    cum = _fdot(tri, logf)
    cum_t = cum.T
    gates_t = gates.T
    e_row = 0 if reverse else L - 1
    m_all = m_scr[...]
    c_prev = [c_scr[h] for h in range(n_heads)]
    n_prev = [n_scr[h] for h in range(n_heads)]
    lane = lax.broadcasted_iota(I32, m_all.shape, 1)
    heads = range(n_heads)
    qs = [q_ref[:, h * dqk:(h + 1) * dqk] * (dqk ** -0.5) for h in heads]
    ks_ = [k_ref[:, h * dqk:(h + 1) * dqk] for h in heads]
    vs = [v_ref[:, h * dv:(h + 1) * dv] for h in heads]
    cum_c = [cum[:, f_off + h:f_off + h + 1] for h in heads]
    cum_r = [cum_t[f_off + h:f_off + h + 1, :] for h in heads]
    i_c = [gates[:, i_off + h:i_off + h + 1] for h in heads]
    i_r = [gates_t[i_off + h:i_off + h + 1, :] for h in heads]
    m_prev = [m_all[0:1, h:h + 1] for h in heads]
    d = [jnp.where(causal, cum_c[h] - cum_r[h] + i_r[h], -jnp.inf) for h in heads]
    m_inter = [cum_c[h] + m_prev[h] for h in heads]
    m_t = [jnp.maximum(m_inter[h], d[h].max(axis=-1, keepdims=True)) for h in heads]
    scores = [_bdot_nt(qs[h], ks_[h]) for h in heads]
    inter = [_bdot(qs[h], c_prev[h]) for h in heads]
    qn = [jnp.sum(qs[h] * n_prev[h], axis=-1, keepdims=True) for h in heads]
    w_inter = [jnp.exp(m_inter[h] - m_t[h]) for h in heads]
    qk = [scores[h] * jnp.exp(d[h] - m_t[h]) for h in heads]
    num = [w_inter[h] * inter[h] + _bdot(qk[h], vs[h]) for h in heads]
    den = [w_inter[h] * qn[h] + qk[h].sum(axis=-1, keepdims=True) for h in heads]
    outs = [num[h] / jnp.maximum(jnp.abs(den[h]), jnp.exp(-m_t[h])) for h in heads]
    end = [cum_c[h][e_row:e_row + 1, :] for h in heads]
    g_end_r = [end[h] - cum_r[h] + i_r[h] for h in heads]
    g_end_c = [end[h] - cum_c[h] + i_c[h] for h in heads]
    m_new = [jnp.maximum(end[h] + m_prev[h], g_end_r[h].max(axis=-1, keepdims=True)) for h in heads]
    w_old = [jnp.exp(end[h] + m_prev[h] - m_new[h]) for h in heads]
    kd = [ks_[h] * jnp.exp(g_end_c[h] - m_new[h]) for h in heads]
    c_new = [w_old[h] * c_prev[h] + _bdot_tn(kd[h], vs[h]) for h in heads]
    n_new = [w_old[h] * n_prev[h] + kd[h].sum(axis=0, keepdims=True) for h in heads]
    m_next = m_all
    for h in heads:
        m_next = jnp.where(lane == h, m_new[h], m_next)
    o_ref[...] = jnp.concatenate(outs, axis=-1)
    for h in range(n_heads):
        c_scr[h] = c_new[h]
        n_scr[h] = n_new[h]
    m_scr[...] = m_next

    @pl.when(c == n_blk - 1)
    def _():
        c_out[0, 0] = c_scr[...]
        n_out[0, 0] = n_scr[...]
        m_out[0, 0] = m_scr[...]


def _mlstm_call(proj, row0, n_seq, t, b_gates_pad, state, reverse):
    n_heads, dqk, dv = 8, 64, 128
    wq, wv = n_heads * dqk, n_heads * dv
    L = 256
    n_blk = t // L
    blk0 = row0 // L
    di = 1 if reverse else 0
    has_state = state is not None
    if has_state:
        c0, n0, m0 = state
        smap = lambda b, c: (b, di, 0, 0, 0)
        mmap = lambda b, c: (b, di, 0, 0)
    else:
        c0 = jnp.zeros((1, 1, n_heads, dqk, dv), F32)
        n0 = jnp.zeros((1, 1, n_heads, 1, dqk), F32)
        m0 = jnp.zeros((1, 1, 1, 128), F32)
        smap = lambda b, c: (0, 0, 0, 0, 0)
        mmap = lambda b, c: (0, 0, 0, 0)

    def tok(b, c):
        return blk0 + b * n_blk + ((n_blk - 1 - c) if reverse else c)

    body = functools.partial(_mlstm_kernel, reverse=reverse, n_heads=n_heads, dqk=dqk, dv=dv,
                             has_state=has_state, n_blk=n_blk)
    gate_cb = (2 * wq + 2 * wv) // 128
    return pl.pallas_call(
        body,
        grid=(n_seq, n_blk),
        in_specs=[pl.BlockSpec((L, wq), lambda b, c: (tok(b, c), 0)),
                  pl.BlockSpec((L, wq), lambda b, c: (tok(b, c), 1)),
                  pl.BlockSpec((L, wv), lambda b, c: (tok(b, c), (2 * wq) // wv)),
                  pl.BlockSpec((L, 128), lambda b, c: (tok(b, c), gate_cb)),
                  pl.BlockSpec((1, 128), lambda b, c: (0, 0)),
                  pl.BlockSpec((1, 1, n_heads, dqk, dv), smap),
                  pl.BlockSpec((1, 1, n_heads, 1, dqk), smap),
                  pl.BlockSpec((1, 1, 1, 128), mmap)],
        out_specs=[pl.BlockSpec((L, wv), lambda b, c: (tok(b, c) - blk0, 0)),
                   pl.BlockSpec((1, 1, n_heads, dqk, dv), lambda b, c: (b, 0, 0, 0, 0)),
                   pl.BlockSpec((1, 1, n_heads, 1, dqk), lambda b, c: (b, 0, 0, 0, 0)),
                   pl.BlockSpec((1, 1, 1, 128), lambda b, c: (b, 0, 0, 0))],
        out_shape=[jax.ShapeDtypeStruct((n_seq * t, wv), F32),
                   jax.ShapeDtypeStruct((n_seq, 1, n_heads, dqk, dv), F32),
                   jax.ShapeDtypeStruct((n_seq, 1, n_heads, 1, dqk), F32),
                   jax.ShapeDtypeStruct((n_seq, 1, 1, 128), F32)],
        scratch_shapes=[pltpu.VMEM((n_heads, dqk, dv), F32),
                        pltpu.VMEM((n_heads, 1, dqk), F32),
                        pltpu.VMEM((1, 128), F32)],
        compiler_params=_params(("arbitrary", "arbitrary")),
        name="mlstm_bw" if reverse else "mlstm_fw",
    )(proj, proj, proj, proj, b_gates_pad, c0, n0, m0)


def _moe_input(x_ref, m_ref, g_ref):
    return _rms(x_ref[...], g_ref[...]) * (1.0 + m_ref[0, 4:5, :]) + m_ref[0, 3:4, :]


def _route_kernel(x_ref, m_ref, g_ref, wr_ref, br_ref, idx_ref, gate_ref, rank_ref, cnt_ref, carry_scr,
                  *, n_experts):
    i = pl.program_id(0)
    tm = x_ref.shape[0]

    @pl.when(i == 0)
    def _():
        carry_scr[...] = jnp.zeros_like(carry_scr)

    h = _moe_input(x_ref, m_ref, g_ref)
    logits = lax.dot_general(wr_ref[...], h, (((1,), (1,)), ((), ())), preferred_element_type=F32,
                             precision=lax.Precision.HIGHEST) + br_ref[...]
    e_io = lax.broadcasted_iota(I32, (n_experts, tm), 0).astype(F32)
    work = logits
    vals, idxs = [], []
    chosen = jnp.zeros((n_experts, tm), F32)
    for _ in range(TOP_K):
        mx = work.max(axis=0, keepdims=True)
        ix = jnp.min(jnp.where(work == mx, e_io, float(n_experts)), axis=0, keepdims=True)
        hit = e_io == ix
        vals.append(mx)
        idxs.append(ix)
        chosen = jnp.where(hit, 1.0, chosen)
        work = jnp.where(hit, -jnp.inf, work)
    es = [jnp.exp(v - vals[0]) for v in vals]
    den = es[0] + es[1] + es[2] + es[3]
    srow = lax.broadcasted_iota(I32, (tm, tm), 0)
    scol = lax.broadcasted_iota(I32, (tm, tm), 1)
    before = jnp.where(srow < scol, 1.0, 0.0).astype(BF16)
    pos = jnp.dot(chosen.astype(BF16), before, preferred_element_type=F32) + carry_scr[...]
    ranks = [jnp.sum(jnp.where(e_io == ix, pos, 0.0), axis=0, keepdims=True) for ix in idxs]
    carry_scr[...] = carry_scr[...] + chosen.sum(axis=1, keepdims=True)
    idx_ref[...] = jnp.concatenate(idxs, axis=0).astype(I32)
    gate_ref[...] = jnp.concatenate([e / den for e in es], axis=0)
    rank_ref[...] = jnp.concatenate(ranks, axis=0).astype(I32)
    cnt_ref[...] = jnp.broadcast_to(carry_scr[...], cnt_ref.shape).astype(I32)


def _route_call(x, modsel, g, w_router, b_router):
    n, d = x.shape
    n_experts = w_router.shape[1]
    tm = TOKEN_TILE
    body = functools.partial(_route_kernel, n_experts=n_experts)
    return pl.pallas_call(
        body,
        grid=(n // tm,),
        in_specs=[pl.BlockSpec((tm, d), lambda i: (i, 0)),
                  pl.BlockSpec((1, 6, d), lambda i: (i, 0, 0)),
                  pl.BlockSpec((1, d), lambda i: (0, 0)),
                  pl.BlockSpec((n_experts, d), lambda i: (0, 0)),
                  pl.BlockSpec((n_experts, 1), lambda i: (0, 0))],
        out_specs=[pl.BlockSpec((TOP_K, tm), lambda i: (0, i)),
                   pl.BlockSpec((TOP_K, tm), lambda i: (0, i)),
                   pl.BlockSpec((TOP_K, tm), lambda i: (0, i)),
                   pl.BlockSpec((n_experts, 128), lambda i: (0, 0))],
        out_shape=[jax.ShapeDtypeStruct((TOP_K, n), I32),
                   jax.ShapeDtypeStruct((TOP_K, n), F32),
                   jax.ShapeDtypeStruct((TOP_K, n), I32),
                   jax.ShapeDtypeStruct((n_experts, 128), I32)],
        scratch_shapes=[pltpu.VMEM((n_experts, 1), F32)],
        compiler_params=_params(("arbitrary",)),
        name="moe_route",
    )(x, modsel, g.reshape(1, d), w_router.T, b_router.reshape(n_experts, 1))


def _slot_kernel(cnt_ref, idx_ref, rank_ref, dest_ref, binfo_ref, pad_ref, nused_ref, *, n_experts, n_blocks):
    cnt = cnt_ref[:, 0:1].astype(F32)
    padded = jnp.ceil(cnt * (1.0 / MOE_BLOCK)) * MOE_BLOCK
    er = lax.broadcasted_iota(I32, (n_experts, n_experts), 0)
    ec = lax.broadcasted_iota(I32, (n_experts, n_experts), 1)
    start_row = jnp.sum(jnp.where(er < ec, padded, 0.0), axis=0, keepdims=True)
    start_col = jnp.sum(jnp.where(er == ec, start_row, 0.0), axis=1, keepdims=True)
    end_col = start_col + padded
    idx = idx_ref[...]
    e_io = lax.broadcasted_iota(I32, (n_experts,) + idx.shape[1:], 0)
    rows = []
    for k in range(TOP_K):
        hit = e_io == idx[k:k + 1, :]
        rows.append(jnp.sum(jnp.where(hit, start_col, 0.0), axis=0, keepdims=True))
    dest_ref[...] = jnp.concatenate(rows, axis=0).astype(I32) + rank_ref[...]
    blk_start = (lax.broadcasted_iota(I32, (n_experts, n_blocks), 1) * MOE_BLOCK).astype(F32)
    n_done = jnp.sum(jnp.where(end_col <= blk_start, 1.0, 0.0), axis=0, keepdims=True)
    bexp = jnp.minimum(n_done, n_experts - 1.0)
    used_row = jnp.sum(jnp.where(er == ec, jnp.where(cnt > 0.0, 1.0, 0.0), 0.0), axis=0, keepdims=True)
    ecf = ec.astype(F32)
    next_col = jnp.min(jnp.where((ec > er) & (used_row > 0.0), ecf, float(n_experts)), axis=1, keepdims=True)
    ord_col = jnp.sum(jnp.where(ec < er, used_row, 0.0), axis=1, keepdims=True)
    par_col = ord_col - 2.0 * jnp.floor(ord_col * 0.5)
    mine = lax.broadcasted_iota(I32, (n_experts, n_blocks), 0).astype(F32) == bexp
    bnext = jnp.sum(jnp.where(mine, next_col, 0.0), axis=0, keepdims=True)
    bslot = jnp.sum(jnp.where(mine, par_col, 0.0), axis=0, keepdims=True)
    binfo_ref[...] = jnp.concatenate([bexp, bnext, bslot], axis=0).astype(I32)
    n_used = jnp.sum(padded, axis=0, keepdims=True) * (1.0 / MOE_BLOCK)
    nused_ref[...] = n_used.astype(I32)
    cnt_row = jnp.sum(jnp.where(er == ec, cnt, 0.0), axis=0, keepdims=True)
    padded_row = jnp.sum(jnp.where(er == ec, padded, 0.0), axis=0, keepdims=True)
    pad_ref[...] = jnp.concatenate([start_row + cnt_row, padded_row - cnt_row,
                                    jnp.broadcast_to(n_used, cnt_row.shape)], axis=0).astype(I32)


def _slot_call(counts, idx_t, rank_t, n_blocks):
    n_experts = counts.shape[0]
    n = idx_t.shape[1]
    tn = min(2048, n)
    body = functools.partial(_slot_kernel, n_experts=n_experts, n_blocks=n_blocks)
    return pl.pallas_call(
        body,
        grid=(n // tn,),
        in_specs=[pl.BlockSpec((n_experts, 128), lambda i: (0, 0)),
                  pl.BlockSpec((TOP_K, tn), lambda i: (0, i)),
                  pl.BlockSpec((TOP_K, tn), lambda i: (0, i))],
        out_specs=[pl.BlockSpec((TOP_K, tn), lambda i: (0, i)),
                   pl.BlockSpec((3, n_blocks), lambda i: (0, 0)),
                   pl.BlockSpec((3, n_experts), lambda i: (0, 0)),
                   pl.BlockSpec((1, 1), lambda i: (0, 0))],
        out_shape=[jax.ShapeDtypeStruct((TOP_K, n), I32),
                   jax.ShapeDtypeStruct((3, n_blocks), I32),
                   jax.ShapeDtypeStruct((3, n_experts), I32),
                   jax.ShapeDtypeStruct((1, 1), I32)],
        compiler_params=_params(("arbitrary",)),
        name="moe_slots",
    )(counts, idx_t, rank_t)


DMA_ISSUE_UNROLL = 8


def _to_row_tiles(ref, base, x):
    rows = x.shape[0]
    for c in range(ROW_SUBLANES):
        ref[pl.ds(base * ROW_SUBLANES + c, rows, stride=ROW_SUBLANES), :] = x[:, c * LANES:(c + 1) * LANES]


def _from_row_tiles(ref, base, rows, c):
    return ref[pl.ds(base * ROW_SUBLANES + c, rows, stride=ROW_SUBLANES), :]


def _row_tile(ref, r):
    return ref.at[pl.ds(pl.multiple_of(r * ROW_SUBLANES, ROW_SUBLANES), ROW_SUBLANES)]


def _zero_fill_padding(pad_ref, xs_ref, z_scr, sem):
    z_scr[...] = jnp.zeros_like(z_scr)
    n_experts = pad_ref.shape[1]
    bits = range(MOE_BLOCK.bit_length() - 2, -1, -1)

    def pieces(e):
        off, length = pad_ref[0, e], pad_ref[1, e]
        for bit in bits:
            size = 1 << bit
            done = (length >> (bit + 1)) << (bit + 1)
            copy = pltpu.make_async_copy(z_scr.at[pl.ds(0, size * ROW_SUBLANES)],
                                         xs_ref.at[pl.ds(pl.multiple_of((off + done) * ROW_SUBLANES, ROW_SUBLANES),
                                                         size * ROW_SUBLANES)], sem)
            yield (length & size) != 0, copy

    def tail_blocks():
        n_blocks = xs_ref.shape[0] // (MOE_BLOCK * ROW_SUBLANES)
        for b in range(n_blocks - n_experts, n_blocks):
            copy = pltpu.make_async_copy(z_scr, xs_ref.at[pl.ds(b * MOE_BLOCK * ROW_SUBLANES,
                                                                MOE_BLOCK * ROW_SUBLANES)], sem)
            yield b >= pad_ref[2, 0], copy

    def all_copies():
        for e in range(n_experts):
            yield from pieces(e)
        yield from tail_blocks()

    for needed, copy in all_copies():
        pl.when(needed)(copy.start)
    for needed, copy in all_copies():
        pl.when(needed)(copy.wait)


def _dispatch_kernel(pad_ref, dest_ref, x_ref, m_ref, g_ref, xs_ref, h_scr, z_scr, sem, *, n_tiles):
    i = pl.program_id(0)
    tm = x_ref.shape[0]

    @pl.when(i == 0)
    def _():
        _zero_fill_padding(pad_ref, xs_ref, z_scr, sem.at[2])

    slot = lax.rem(i, 2)
    src = h_scr.at[slot]
    _to_row_tiles(src, 0, _moe_input(x_ref, m_ref, g_ref))

    def start_row(r, carry):
        for k in range(TOP_K):
            pltpu.make_async_copy(_row_tile(src, r), _row_tile(xs_ref, dest_ref[0, 0, k * tm + r]),
                                  sem.at[slot]).start(priority=k % 2)
        return carry

    lax.fori_loop(0, tm, start_row, 0, unroll=DMA_ISSUE_UNROLL // TOP_K)

    def wait_tile(s):
        for _ in range(TOP_K):
            pltpu.make_async_copy(h_scr.at[s], xs_ref.at[pl.ds(0, tm * ROW_SUBLANES)], sem.at[s]).wait()

    @pl.when(i >= 1)
    def _():
        wait_tile(1 - slot)

    @pl.when(i == n_tiles - 1)
    def _():
        wait_tile(slot)


def _dispatch_call(pad_info, dest_tiles, x, modsel, g, n_slots):
    n, d = x.shape
    tm = TOKEN_TILE
    assert d == ROW_SUBLANES * LANES
    return pl.pallas_call(
        functools.partial(_dispatch_kernel, n_tiles=n // tm),
        grid=(n // tm,),
        in_specs=[pl.BlockSpec(memory_space=pltpu.SMEM),
                  pl.BlockSpec((1, 1, TOP_K * tm), lambda i: (i, 0, 0), memory_space=pltpu.SMEM),
                  pl.BlockSpec((tm, d), lambda i: (i, 0)),
                  pl.BlockSpec((1, 6, d), lambda i: (i, 0, 0)),
                  pl.BlockSpec((1, d), lambda i: (0, 0))],
        out_specs=pl.BlockSpec(memory_space=pl.ANY),
        out_shape=jax.ShapeDtypeStruct((n_slots * ROW_SUBLANES, LANES), F32),
        scratch_shapes=[pltpu.VMEM((2, tm * ROW_SUBLANES, LANES), F32),
                        pltpu.VMEM((MOE_BLOCK * ROW_SUBLANES, LANES), F32),
                        pltpu.SemaphoreType.DMA((3,))],
        compiler_params=_params(("arbitrary",)),
        name="moe_dispatch",
    )(pad_info, dest_tiles, x, modsel, g.reshape(1, d))


def _ffn_kernel(binfo_ref, nused_ref, xs_ref, wgu_hbm, bgu_ref, wdn_hbm, bdn_ref, ys_ref,
                wgu_f32, wdn_f32, wgu_scr, wdn_scr, sem, *, layer, n_experts, n_blocks):
    b = pl.program_id(0)
    d_ff = wdn_scr.shape[0]
    rows = xs_ref.shape[0] // ROW_SUBLANES

    def weight_copies(e, slot):
        return (pltpu.make_async_copy(wgu_hbm.at[layer, e], wgu_f32.at[slot], sem.at[slot]),
                pltpu.make_async_copy(wdn_hbm.at[layer, e], wdn_f32.at[slot], sem.at[slot]))

    @pl.when(b < nused_ref[0])
    def _():
        e = binfo_ref[b]
        prev = binfo_ref[jnp.maximum(b - 1, 0)]
        slot = binfo_ref[2 * n_blocks + b]

        @pl.when((b == 0) | (e != prev))
        def _():
            @pl.when(b == 0)
            def _():
                for cp in weight_copies(e, slot):
                    cp.start()

            nxt = binfo_ref[n_blocks + b]

            @pl.when(nxt < n_experts)
            def _():
                for cp in weight_copies(nxt, 1 - slot):
                    cp.start()

            for cp in weight_copies(e, slot):
                cp.wait()
            wgu_scr[...] = wgu_f32[slot].astype(BF16)
            wdn_scr[...] = wdn_f32[slot].astype(BF16)

        x = jnp.concatenate([_from_row_tiles(xs_ref, 0, rows, c).astype(BF16) for c in range(ROW_SUBLANES)],
                            axis=-1)
        gu = jnp.dot(x, wgu_scr[...], preferred_element_type=F32) + bgu_ref[0, 0]
        x_glu = jnp.minimum(gu[:, :d_ff], SWIGLU_LIMIT)
        x_lin = jnp.clip(gu[:, d_ff:], -SWIGLU_LIMIT, SWIGLU_LIMIT)
        hid = x_glu * _sigmoid(SWIGLU_ALPHA * x_glu) * (x_lin + 1.0)
        _to_row_tiles(ys_ref, 0, jnp.dot(hid.astype(BF16), wdn_scr[...], preferred_element_type=F32)
                      + bdn_ref[0, 0])

    @pl.when(b >= nused_ref[0])
    def _():
        ys_ref[...] = jnp.zeros_like(ys_ref)


def _ffn_call(layer, block_info, n_used, xs, w_gu, b_gu, w_dn, b_dn):
    depth, n_experts, d, d_ff2 = w_gu.shape
    d_ff = d_ff2 // 2
    blk_rows = MOE_BLOCK * ROW_SUBLANES
    n_blocks = xs.shape[0] // blk_rows

    def blk(b, nu):
        return jnp.maximum(jnp.minimum(b, nu[0] - 1), 0)

    grid_spec = pltpu.PrefetchScalarGridSpec(
        num_scalar_prefetch=2,
        grid=(n_blocks,),
        in_specs=[pl.BlockSpec((blk_rows, LANES), lambda b, bi, nu: (blk(b, nu), 0)),
                  pl.BlockSpec(memory_space=pl.ANY),
                  pl.BlockSpec((1, 1, 1, d_ff2), lambda b, bi, nu: (layer, bi[blk(b, nu)], 0, 0)),
                  pl.BlockSpec(memory_space=pl.ANY),
                  pl.BlockSpec((1, 1, 1, d), lambda b, bi, nu: (layer, bi[blk(b, nu)], 0, 0))],
        out_specs=pl.BlockSpec((blk_rows, LANES), lambda b, bi, nu: (b, 0)),
        scratch_shapes=[pltpu.VMEM((2, d, d_ff2), F32), pltpu.VMEM((2, d_ff, d), F32),
                        pltpu.VMEM((d, d_ff2), BF16), pltpu.VMEM((d_ff, d), BF16),
                        pltpu.SemaphoreType.DMA((2,))],
    )
    body = functools.partial(_ffn_kernel, layer=layer, n_experts=n_experts, n_blocks=n_blocks)
    return pl.pallas_call(
        body,
        grid_spec=grid_spec,
        out_shape=jax.ShapeDtypeStruct(xs.shape, F32),
        compiler_params=_params(("arbitrary",)),
        name="moe_ffn",
    )(block_info, n_used, xs, w_gu, b_gu.reshape(depth, n_experts, 1, d_ff2), w_dn,
      b_dn.reshape(depth, n_experts, 1, d))


def _gather_expert_rows(dest_ref, dest_next_ref, ys_ref, gate_ref, buf, sem, n_tiles):
    i = pl.program_id(0)
    tm = gate_ref.shape[0]
    rows = TOP_K * tm

    def start_gathers(d_ref, slot):
        def start_pair(p, carry):
            for u in range(2):
                j = 2 * p + u
                pltpu.make_async_copy(_row_tile(ys_ref, d_ref[0, 0, j]), _row_tile(buf.at[slot], j),
                                      sem.at[slot]).start(priority=u)
            return carry

        lax.fori_loop(0, rows // 2, start_pair, 0, unroll=DMA_ISSUE_UNROLL // 2)

    slot = lax.rem(i, 2)

    @pl.when(i == 0)
    def _():
        start_gathers(dest_ref, 0)

    def wait_tile(s):
        pltpu.make_async_copy(ys_ref.at[pl.ds(0, rows * ROW_SUBLANES)], buf.at[s], sem.at[s]).wait()

    @pl.when(i + 1 < n_tiles)
    def _():
        start_gathers(dest_next_ref, 1 - slot)

    wait_tile(slot)
    cur = buf.at[slot]
    chunks = []
    for c in range(ROW_SUBLANES):
        y = gate_ref[:, 0:1] * _from_row_tiles(cur, 0, tm, c)
        for k in range(1, TOP_K):
            y = y + gate_ref[:, k:k + 1] * _from_row_tiles(cur, k * tm, tm, c)
        chunks.append(y)
    return jnp.concatenate(chunks, axis=-1)


def _combine_kernel(dest_ref, dest_next_ref, ys_ref, gate_ref, x_ref, m_ref, fg_ref, y_ref, buf, sem,
                    *, n_tiles):
    y = _gather_expert_rows(dest_ref, dest_next_ref, ys_ref, gate_ref, buf, sem, n_tiles)
    y_ref[...] = _rms(x_ref[...] + m_ref[0, 5:6, :] * y, fg_ref[...])


def _combine_inproj_kernel(dest_ref, dest_next_ref, ys_ref, gate_ref, x_ref, mp_ref, mc_ref, g_ref, w_ref,
                           x_out_ref, p_ref, buf, sem, *, n_tiles):
    y = _gather_expert_rows(dest_ref, dest_next_ref, ys_ref, gate_ref, buf, sem, n_tiles)
    x = x_ref[...] + mp_ref[0, 5:6, :] * y
    x_out_ref[...] = x
    h = _rms(x, g_ref[...]) * (1.0 + mc_ref[0, 1:2, :]) + mc_ref[0, 0:1, :]
    p_ref[...] = _bdot(h, w_ref[...])


def _moe_gather_specs(tm, n_tiles, d):
    return [pl.BlockSpec((1, 1, TOP_K * tm), lambda i: (i, 0, 0), memory_space=pltpu.SMEM),
            pl.BlockSpec((1, 1, TOP_K * tm), lambda i: (jnp.minimum(i + 1, n_tiles - 1), 0, 0),
                         memory_space=pltpu.SMEM),
            pl.BlockSpec(memory_space=pl.ANY),
            pl.BlockSpec((tm, TOP_K), lambda i: (i, 0)),
            pl.BlockSpec((tm, d), lambda i: (i, 0))]


def _moe_gather_scratch(tm):
    return [pltpu.VMEM((2, TOP_K * tm * ROW_SUBLANES, LANES), F32), pltpu.SemaphoreType.DMA((2,))]


def _combine_inproj_call(pending, x, modsel_prev, modsel, g, w):
    dest_tiles, ys, gates_nk = pending
    n, d = x.shape
    wout = w.shape[1]
    tm = TOKEN_TILE
    n_tiles = n // tm
    body = functools.partial(_combine_inproj_kernel, n_tiles=n_tiles)
    return pl.pallas_call(
        body,
        grid=(n_tiles,),
        in_specs=_moe_gather_specs(tm, n_tiles, d)
                 + [pl.BlockSpec((1, 6, d), lambda i: (i, 0, 0)),
                    pl.BlockSpec((1, 6, d), lambda i: (i, 0, 0)),
                    pl.BlockSpec((1, d), lambda i: (0, 0)),
                    pl.BlockSpec((d, wout), lambda i: (0, 0))],
        out_specs=[pl.BlockSpec((tm, d), lambda i: (i, 0)),
                   pl.BlockSpec((tm, wout), lambda i: (i, 0))],
        out_shape=[jax.ShapeDtypeStruct((n, d), F32),
                   jax.ShapeDtypeStruct((n, wout), F32)],
        scratch_shapes=_moe_gather_scratch(tm),
        compiler_params=_params(("arbitrary",)),
        name="moe_combine_inproj",
    )(dest_tiles, dest_tiles, ys, gates_nk, x, modsel_prev, modsel, g.reshape(1, d), w)


def _combine_call(pending, x, modsel, final_g):
    dest_tiles, ys, gates_nk = pending
    n, d = x.shape
    tm = TOKEN_TILE
    n_tiles = n // tm
    body = functools.partial(_combine_kernel, n_tiles=n_tiles)
    return pl.pallas_call(
        body,
        grid=(n_tiles,),
        in_specs=_moe_gather_specs(tm, n_tiles, d)
                 + [pl.BlockSpec((1, 6, d), lambda i: (i, 0, 0)),
                    pl.BlockSpec((1, d), lambda i: (0, 0))],
        out_specs=pl.BlockSpec((tm, d), lambda i: (i, 0)),
        out_shape=jax.ShapeDtypeStruct((n, d), F32),
        scratch_shapes=_moe_gather_scratch(tm),
        compiler_params=_params(("arbitrary",)),
        name="moe_combine",
    )(dest_tiles, dest_tiles, ys, gates_nk, x, modsel, final_g.reshape(1, d))


def _moe_experts(layer, x, modsel, g2, w_router, b_router, w_gu, b_gu, w_dn, b_dn):
    n, d = x.shape
    n_experts = w_router.shape[1]
    tm = TOKEN_TILE
    n_blocks = (n * TOP_K) // MOE_BLOCK + n_experts
    idx_t, gate_t, rank_t, counts = _route_call(x, modsel, g2, w_router, b_router)
    dest_t, block_info, pad_info, n_used = _slot_call(counts, idx_t, rank_t, n_blocks)
    dest_tiles = dest_t.reshape(TOP_K, n // tm, tm).transpose(1, 0, 2).reshape(n // tm, 1, TOP_K * tm)
    xs = _dispatch_call(pad_info, dest_tiles, x, modsel, g2, n_blocks * MOE_BLOCK)
    ys = _ffn_call(layer, block_info.reshape(3 * n_blocks), n_used.reshape(1), xs, w_gu, b_gu, w_dn, b_dn)
    return dest_tiles, ys, gate_t.T


def _rope_tables(t, hd):
    pos = np.arange(t)
    n_freq = hd // 4
    inv_freq = ROPE_THETA ** (-np.arange(n_freq, dtype=np.float32) / n_freq)
    ang = np.concatenate([(pos // GRID_W).astype(np.float32)[:, None] * inv_freq,
                          (pos % GRID_W).astype(np.float32)[:, None] * inv_freq], axis=-1)
    ang = jnp.asarray(ang, F32)
    cos, sin = jnp.cos(ang), jnp.sin(ang)
    return jnp.concatenate([cos, cos], axis=-1), jnp.concatenate([-sin, sin], axis=-1)


def kernel(x_prompt, x_sample, cache_k_a, cache_v_a, state_b, state_c_C, state_c_n, state_c_m, cache_k_d, cache_v_d, c, c_ctx, norm1_g, norm2_g, w_mod, b_mod, w_in_a, qnorm_a, knorm_a, w_out_a, w_in_b, b_f_b, lower_bounds_b, onorm_b, w_out_b, w_in_c, b_gates_c, onorm_c, w_out_c, w_in_d, rpb_d, w_out_d, w_router, b_router, w_gu, b_gu, w_dn, b_dn, final_g):
    n_ctx_seq, t_ctx, d = x_prompt.shape
    n_lat_seq, t_lat, _ = x_sample.shape
    depth = w_mod.shape[0]
    n_ctx = n_ctx_seq * t_ctx
    n_lat = n_lat_seq * t_lat
    n = n_ctx + n_lat
    tm = TOKEN_TILE
    assert t_ctx % tm == 0 and t_lat % tm == 0 and n_lat_seq + 1 <= 8

    lb_cum = jnp.cumsum(jax.nn.softmax(lower_bounds_b.astype(F32), axis=0), axis=0)
    lb_all = lb_cum - lb_cum[0]

    cond8 = jnp.zeros((8, d), F32).at[0].set(c_ctx).at[1:1 + n_lat_seq].set(c)
    mod = _mod_call(cond8, w_mod, b_mod)
    tile_row = np.concatenate([np.zeros(n_ctx // tm, np.int32),
                               1 + np.repeat(np.arange(n_lat_seq, dtype=np.int32), t_lat // tm)])

    x = jnp.concatenate([x_prompt.reshape(n_ctx, d), x_sample.reshape(n_lat, d)], axis=0)
    outs = {}
    pending = None
    modsel = None

    def inproj(x, modsel, g, w):
        if pending is None:
            return x, _inproj_call(x, modsel, g, w)
        return _combine_inproj_call(pending, x, modsel_prev, modsel, g, w)

    for i in range(depth):
        kind = i % 4
        j = i // 4
        modsel_prev = modsel
        modsel = mod[i].reshape(8, 6, d)[tile_row]
        if kind == 0:
            x, proj = inproj(x, modsel, norm1_g[i],w_in_a[j].astype(BF16))
            o_ctx, k_new = _gqa_ctx_call(proj, qnorm_a[j], knorm_a[j], n_ctx_seq, t_ctx)
            cosd, sind = _rope_tables(t_lat, 128)
            o_lat = _gqa_lat_call(proj, n_ctx, n_lat_seq, t_lat,
                                  cache_k_a[:, j].reshape(n_lat_seq, -1, 256),
                                  cache_v_a[:, j].reshape(n_lat_seq, -1, 256), cosd, sind,
                                  qnorm_a[j], knorm_a[j])
            outs["k_a"] = k_new.reshape(n_ctx_seq, 1, t_ctx, 2, 128)
            outs["v_a"] = proj[:n_ctx, 1280:1536].reshape(n_ctx_seq, 1, t_ctx, 2, 128)
            x = _outproj_call("plain", [(o_ctx, o_lat)], None, w_out_a[j].astype(BF16), x, modsel)
        elif kind == 1:
            x, proj = inproj(x, modsel, norm1_g[i],w_in_b[j].astype(BF16))
            o_dirs, s_dirs = [], []
            for reverse in (False, True):
                oc, sc = _hgrn_call(proj, 0, n_ctx_seq, t_ctx, b_f_b[j], lb_all[i], None, reverse)
                ol, _ = _hgrn_call(proj, n_ctx, n_lat_seq, t_lat, b_f_b[j], lb_all[i], state_b[:, j], reverse)
                o_dirs.append((oc, ol))
                s_dirs.append(sc)
            outs["s_b"] = jnp.concatenate(s_dirs, axis=1)[:, None]
            x = _outproj_call("hgrn", o_dirs, (proj, 4, onorm_b[j].reshape(1, 128)),
                              w_out_b[j].astype(BF16), x, modsel)
        elif kind == 2:
            w_c = jnp.pad(w_in_c[j], ((0, 0), (0, 128 - 32))).astype(BF16)
            bg = jnp.pad(b_gates_c[j].reshape(1, 32), ((0, 0), (0, 128 - 32)))
            x, proj = inproj(x, modsel, norm1_g[i],w_c)
            state = (state_c_C[:, j], state_c_n[:, j][:, :, :, None, :],
                     jnp.pad(state_c_m[:, j], ((0, 0), (0, 0), (0, 120)))[:, :, None, :])
            o_dirs, st = [], []
            for reverse in (False, True):
                oc, cc, nc, mc = _mlstm_call(proj, 0, n_ctx_seq, t_ctx, bg, None, reverse)
                ol, _, _, _ = _mlstm_call(proj, n_ctx, n_lat_seq, t_lat, bg, state, reverse)
                o_dirs.append((oc, ol))
                st.append((cc, nc, mc))
            outs["c_C"] = jnp.concatenate([st[0][0], st[1][0]], axis=1)[:, None]
            outs["c_n"] = jnp.concatenate([st[0][1], st[1][1]], axis=1)[:, None, :, :, 0, :]
            outs["c_m"] = jnp.concatenate([st[0][2], st[1][2]], axis=1)[:, None, :, 0, :8]
            x = _outproj_call("mlstm", o_dirs, (proj, 2, onorm_c[j].reshape(1, 128)),
                              w_out_c[j].astype(BF16), x, modsel)
        else:
            x, proj = inproj(x, modsel, norm1_g[i],w_in_d[j].astype(BF16))
            o_ctx = _mha_ctx_call(proj, n_ctx_seq, t_ctx)
            tz = _na_bias_call(rpb_d[j])
            o_lat = _na_call(proj, n_ctx, n_lat_seq, t_lat,
                             cache_k_d[:, j].reshape(n_lat_seq, -1, d),
                             cache_v_d[:, j].reshape(n_lat_seq, -1, d), tz)
            outs["k_d"] = proj[:n_ctx, d:2 * d].reshape(n_ctx_seq, 1, t_ctx, 16, 64)
            outs["v_d"] = proj[:n_ctx, 2 * d:3 * d].reshape(n_ctx_seq, 1, t_ctx, 16, 64)
            x = _outproj_call("plain", [(o_ctx, o_lat)], None, w_out_d[j].astype(BF16), x, modsel)
        pending = _moe_experts(i, x, modsel, norm2_g[i], w_router[i], b_router[i], w_gu, b_gu, w_dn, b_dn)
    x = _combine_call(pending, x, modsel, final_g)

    y_prompt = x[:n_ctx].reshape(n_ctx_seq, t_ctx, d)
    y_sample = x[n_ctx:].reshape(n_lat_seq, t_lat, d)
    return (y_prompt, y_sample, outs["k_a"], outs["v_a"], outs["s_b"], outs["c_C"], outs["c_n"], outs["c_m"],
            outs["k_d"], outs["v_d"])
```

```python
import functools

import numpy as np
import jax
import jax.numpy as jnp
from jax import lax
from jax.experimental import pallas as pl
from jax.experimental.pallas import tpu as pltpu

F32 = jnp.float32
BF16 = jnp.bfloat16
I32 = jnp.int32

NORM_EPS = 1e-6
GRID_W = 64
ROPE_THETA = 10000.0
TOP_K = 4
HGRN_CHUNK = 128
GATE_SOFTCAP = 15.0
NA_ROWS = 8
NA_COLS = 16
SWIGLU_ALPHA = 1.702
SWIGLU_LIMIT = 7.0
NEG_BIG = -1e30

LANES = 128
ROW_SUBLANES = 8
TOKEN_TILE = 256
MOE_BLOCK = 256
ATTN_HEAD_GROUP = 8
V7X_VMEM_LIMIT = 52 * 1024 * 1024


def _params(sem, vmem=V7X_VMEM_LIMIT):
    return pltpu.CompilerParams(dimension_semantics=sem, vmem_limit_bytes=vmem)


def _bdot(a, b):
    return jnp.dot(a.astype(BF16), b.astype(BF16), preferred_element_type=F32)


def _bdot_nt(a, b):
    return lax.dot_general(a.astype(BF16), b.astype(BF16), (((1,), (1,)), ((), ())),
                           preferred_element_type=F32)


def _bdot_tn(a, b):
    return lax.dot_general(a.astype(BF16), b.astype(BF16), (((0,), (0,)), ((), ())),
                           preferred_element_type=F32)


def _fdot(a, b):
    return jnp.dot(a, b, preferred_element_type=F32, precision=lax.Precision.HIGHEST)


def _rms(x, g):
    return x * lax.rsqrt(jnp.mean(x * x, axis=-1, keepdims=True) + NORM_EPS) * g


def _rms_heads(x, g, n_heads, hd):
    return jnp.concatenate([_rms(x[:, h * hd:(h + 1) * hd], g) for h in range(n_heads)], axis=-1)


def _sigmoid(x):
    return 1.0 / (1.0 + jnp.exp(-x))


def _silu(x):
    return x * _sigmoid(x)


def _softmax_rows(parts):
    m = parts[0].max(axis=-1, keepdims=True)
    for p in parts[1:]:
        m = jnp.maximum(m, p.max(axis=-1, keepdims=True))
    es = [jnp.exp(p - m) for p in parts]
    den = es[0].sum(axis=-1, keepdims=True)
    for e in es[1:]:
        den = den + e.sum(axis=-1, keepdims=True)
    return es, den


def _mod_kernel(c_ref, w_ref, b_ref, o_ref):
    o_ref[0] = _bdot(_silu(c_ref[...]), w_ref[0]) + b_ref[0]


def _mod_call(cond8, w_mod, b_mod):
    depth, d, d6 = w_mod.shape
    tn = 1024
    return pl.pallas_call(
        _mod_kernel,
        grid=(depth, d6 // tn),
        in_specs=[pl.BlockSpec((8, d), lambda i, j: (0, 0)),
                  pl.BlockSpec((1, d, tn), lambda i, j: (i, 0, j)),
                  pl.BlockSpec((1, 1, tn), lambda i, j: (i, 0, j))],
        out_specs=pl.BlockSpec((1, 8, tn), lambda i, j: (i, 0, j)),
        out_shape=jax.ShapeDtypeStruct((depth, 8, d6), F32),
        compiler_params=_params(("arbitrary", "arbitrary")),
        name="adaln_mod",
    )(cond8, w_mod, b_mod.reshape(depth, 1, d6))


def _inproj_kernel(x_ref, m_ref, g_ref, w_ref, o_ref):
    h = _rms(x_ref[...], g_ref[...]) * (1.0 + m_ref[0, 1:2, :]) + m_ref[0, 0:1, :]
    o_ref[...] = _bdot(h, w_ref[...])


def _inproj_call(x, modsel, g, w):
    n, d = x.shape
    wout = w.shape[1]
    tm = TOKEN_TILE
    return pl.pallas_call(
        _inproj_kernel,
        grid=(n // tm,),
        in_specs=[pl.BlockSpec((tm, d), lambda i: (i, 0)),
                  pl.BlockSpec((1, 6, d), lambda i: (i, 0, 0)),
                  pl.BlockSpec((1, d), lambda i: (0, 0)),
                  pl.BlockSpec((d, wout), lambda i: (0, 0))],
        out_specs=pl.BlockSpec((tm, wout), lambda i: (i, 0)),
        out_shape=jax.ShapeDtypeStruct((n, wout), F32),
        compiler_params=_params(("arbitrary",)),
        name="inproj",
    )(x, modsel, g.reshape(1, d), w)


def _outproj_kernel(*refs, kind, n_pairs, n_ctx_tiles, n_heads, hd):
    in_ctx = pl.program_id(0) < n_ctx_tiles
    mix = [jnp.where(in_ctx, refs[2 * p][...], refs[2 * p + 1][...]) for p in range(n_pairs)]
    rest = refs[2 * n_pairs:]
    if kind == "plain":
        w_ref, x_ref, m_ref, y_ref = rest
        o = mix[0]
    else:
        gate_ref, on_ref, w_ref, x_ref, m_ref, y_ref = rest
        o = _rms_heads(mix[0] + mix[1], on_ref[...], n_heads, hd)
        o = o * _silu(gate_ref[...]) if kind == "hgrn" else _sigmoid(gate_ref[...]) * o
    y_ref[...] = x_ref[...] + m_ref[0, 2:3, :] * _bdot(o, w_ref[...])


def _outproj_call(kind, pairs, gate, w, x, modsel):
    n, d = x.shape
    tm = TOKEN_TILE
    n_ctx_tiles = pairs[0][0].shape[0] // tm
    specs, args = [], []
    for a_ctx, a_lat in pairs:
        specs += [pl.BlockSpec((tm, d), lambda i: (jnp.minimum(i, n_ctx_tiles - 1), 0)),
                  pl.BlockSpec((tm, d), lambda i: (jnp.maximum(i - n_ctx_tiles, 0), 0))]
        args += [a_ctx, a_lat]
    if gate is not None:
        proj, cb, on = gate
        specs += [pl.BlockSpec((tm, d), lambda i: (i, cb)), pl.BlockSpec(on.shape, lambda i: (0, 0))]
        args += [proj, on]
    specs += [pl.BlockSpec(w.shape, lambda i: (0, 0)),
              pl.BlockSpec((tm, d), lambda i: (i, 0)),
              pl.BlockSpec((1, 6, d), lambda i: (i, 0, 0))]
    args += [w, x, modsel]
    body = functools.partial(_outproj_kernel, kind=kind, n_pairs=len(pairs), n_ctx_tiles=n_ctx_tiles,
                             n_heads=8, hd=128)
    return pl.pallas_call(
        body,
        grid=(n // tm,),
        in_specs=specs,
        out_specs=pl.BlockSpec((tm, d), lambda i: (i, 0)),
        out_shape=jax.ShapeDtypeStruct((n, d), F32),
        compiler_params=_params(("arbitrary",)),
        name="outproj_" + kind,
    )(*args)


def _gqa_ctx_kernel(p_ref, qn_ref, kn_ref, o_ref, k_ref, *, n_heads, n_kv, hd):
    rep = n_heads // n_kv
    scale = hd ** -0.5
    koff = n_heads * hd
    voff = koff + n_kv * hd
    ks = [_rms(p_ref[:, koff + g * hd: koff + (g + 1) * hd], kn_ref[...]) for g in range(n_kv)]
    k_ref[...] = jnp.concatenate(ks, axis=-1)
    for h in range(n_heads):
        g = h // rep
        q = _rms(p_ref[:, h * hd:(h + 1) * hd], qn_ref[...])
        s = _bdot_nt(q, ks[g]) * scale
        (e,), den = _softmax_rows([s])
        o_ref[:, h * hd:(h + 1) * hd] = _bdot(e / den, p_ref[:, voff + g * hd: voff + (g + 1) * hd])


def _gqa_ctx_call(proj, qn, kn, n_seq, t):
    n_heads, n_kv, hd = 8, 2, 128
    win = proj.shape[1]
    body = functools.partial(_gqa_ctx_kernel, n_heads=n_heads, n_kv=n_kv, hd=hd)
    return pl.pallas_call(
        body,
        grid=(n_seq,),
        in_specs=[pl.BlockSpec((t, win), lambda b: (b, 0)),
                  pl.BlockSpec((1, hd), lambda b: (0, 0)),
                  pl.BlockSpec((1, hd), lambda b: (0, 0))],
        out_specs=[pl.BlockSpec((t, n_heads * hd), lambda b: (b, 0)),
                   pl.BlockSpec((t, n_kv * hd), lambda b: (b, 0))],
        out_shape=[jax.ShapeDtypeStruct((n_seq * t, n_heads * hd), F32),
                   jax.ShapeDtypeStruct((n_seq * t, n_kv * hd), F32)],
        compiler_params=_params(("arbitrary",)),
        name="gqa_ctx",
    )(proj, qn.reshape(1, hd), kn.reshape(1, hd))


def _rope(x, cosd, sind):
    return x * cosd + pltpu.roll(x, x.shape[-1] // 2, 1) * sind


def _gqa_lat_kernel(pq_ref, pkv_ref, kc_ref, vc_ref, cq_ref, sq_ref, ck_ref, sk_ref, qn_ref, kn_ref,
                    o_ref, k_scr, v_scr, *, n_heads, n_kv, hd, t_ctx):
    rep = n_heads // n_kv
    scale = hd ** -0.5

    @pl.when(pl.program_id(1) == 0)
    def _():
        k_scr[0:t_ctx, :] = kc_ref[0].astype(BF16)
        v_scr[...] = jnp.ones_like(v_scr)
        for g in range(n_kv):
            k = _rms(pkv_ref[:, g * hd:(g + 1) * hd], kn_ref[...])
            k_scr[t_ctx:, g * hd:(g + 1) * hd] = _rope(k, ck_ref[...], sk_ref[...]).astype(BF16)
            v_scr[0:t_ctx, 2 * g * hd:(2 * g + 1) * hd] = vc_ref[0, :, g * hd:(g + 1) * hd].astype(BF16)
            v_scr[t_ctx:, 2 * g * hd:(2 * g + 1) * hd] = pkv_ref[:, (n_kv + g) * hd:(n_kv + g + 1) * hd].astype(BF16)

    qs = [(_rope(_rms(pq_ref[:, h * hd:(h + 1) * hd], qn_ref[...]), cq_ref[...], sq_ref[...]) * scale).astype(BF16)
          for h in range(n_heads)]
    for h0 in range(0, n_heads, ATTN_HEAD_GROUP):
        hs = range(h0, min(h0 + ATTN_HEAD_GROUP, n_heads))
        s = [_bdot_nt(qs[h], k_scr[:, (h // rep) * hd:(h // rep + 1) * hd]) for h in hs]
        e = [jnp.exp(x - x.max(axis=-1, keepdims=True)).astype(BF16) for x in s]
        pv = [jnp.dot(e[i], v_scr[:, 2 * (h // rep) * hd:2 * (h // rep + 1) * hd], preferred_element_type=F32)
              for i, h in enumerate(hs)]
        for i, h in enumerate(hs):
            o_ref[:, h * hd:(h + 1) * hd] = pv[i][:, :hd] / pv[i][:, hd:]


def _gqa_lat_call(proj, row0, n_seq, t, cache_k, cache_v, cosd, sind, qn, kn):
    n_heads, n_kv, hd = 8, 2, 128
    tq = 256
    t_ctx = cache_k.shape[1]
    nq = t // tq
    kvw = 2 * n_kv * hd
    qblk0 = row0 // tq
    sblk0 = row0 // t
    body = functools.partial(_gqa_lat_kernel, n_heads=n_heads, n_kv=n_kv, hd=hd, t_ctx=t_ctx)
    return pl.pallas_call(
        body,
        grid=(n_seq, nq),
        in_specs=[pl.BlockSpec((tq, n_heads * hd), lambda b, i: (qblk0 + b * nq + i, 0)),
                  pl.BlockSpec((t, kvw), lambda b, i: (sblk0 + b, (n_heads * hd) // kvw)),
                  pl.BlockSpec((1, t_ctx, n_kv * hd), lambda b, i: (b, 0, 0)),
                  pl.BlockSpec((1, t_ctx, n_kv * hd), lambda b, i: (b, 0, 0)),
                  pl.BlockSpec((tq, hd), lambda b, i: (i, 0)),
                  pl.BlockSpec((tq, hd), lambda b, i: (i, 0)),
                  pl.BlockSpec((t, hd), lambda b, i: (0, 0)),
                  pl.BlockSpec((t, hd), lambda b, i: (0, 0)),
                  pl.BlockSpec((1, hd), lambda b, i: (0, 0)),
                  pl.BlockSpec((1, hd), lambda b, i: (0, 0))],
        out_specs=pl.BlockSpec((tq, n_heads * hd), lambda b, i: (b * nq + i, 0)),
        out_shape=jax.ShapeDtypeStruct((n_seq * t, n_heads * hd), F32),
        scratch_shapes=[pltpu.VMEM((t_ctx + t, n_kv * hd), BF16),
                        pltpu.VMEM((t_ctx + t, 2 * n_kv * hd), BF16)],
        compiler_params=_params(("arbitrary", "arbitrary")),
        name="gqa_latent",
    )(proj, proj, cache_k, cache_v, cosd, sind, cosd, sind, qn.reshape(1, hd), kn.reshape(1, hd))


def _mha_ctx_kernel(q_ref, k_ref, v_ref, o_ref, *, hd):
    scale = hd ** -0.5
    n_heads = q_ref.shape[1] // hd
    ones = jnp.ones((v_ref.shape[0], hd), BF16)
    for h0 in range(0, n_heads, ATTN_HEAD_GROUP):
        sls = [slice(h * hd, (h + 1) * hd) for h in range(h0, min(h0 + ATTN_HEAD_GROUP, n_heads))]
        s = [_bdot_nt(q_ref[:, sl] * scale, k_ref[:, sl]) for sl in sls]
        e = [jnp.exp(x - x.max(axis=-1, keepdims=True)).astype(BF16) for x in s]
        pv = [jnp.dot(e[i], jnp.concatenate([v_ref[:, sl].astype(BF16), ones], axis=-1),
                      preferred_element_type=F32) for i, sl in enumerate(sls)]
        for i, sl in enumerate(sls):
            o_ref[:, sl] = pv[i][:, :hd] / pv[i][:, hd:]


def _mha_ctx_call(proj, n_seq, t):
    hd = 64
    d = proj.shape[1] // 3
    body = functools.partial(_mha_ctx_kernel, hd=hd)
    return pl.pallas_call(
        body,
        grid=(n_seq,),
        in_specs=[pl.BlockSpec((t, d), lambda b: (b, 0)),
                  pl.BlockSpec((t, d), lambda b: (b, 1)),
                  pl.BlockSpec((t, d), lambda b: (b, 2))],
        out_specs=pl.BlockSpec((t, d), lambda b: (b, 0)),
        out_shape=jax.ShapeDtypeStruct((n_seq * t, d), F32),
        compiler_params=_params(("arbitrary",)),
        name="mha_ctx",
    )(proj, proj, proj)


def _na_bias_kernel(rpb_ref, o_ref, *, n_rel_rows, n_rel_cols):
    h = pl.program_id(0)
    w_io = lax.broadcasted_iota(I32, (GRID_W, 2 * GRID_W), 0)
    lane = lax.broadcasted_iota(I32, (GRID_W, 2 * GRID_W), 1)
    ck = jnp.where(lane < GRID_W, lane, lane - GRID_W)
    c_start = jnp.clip(w_io - NA_COLS // 2, 0, GRID_W - NA_COLS)
    in_win = (ck >= c_start) & (ck < c_start + NA_COLS)
    rel = ck - w_io + (NA_COLS - 1)
    base = h * (n_rel_rows * n_rel_cols)
    tiles = []
    for j in range(n_rel_rows):
        acc = jnp.zeros((GRID_W, 2 * GRID_W), F32)
        for jj in range(n_rel_cols):
            acc = jnp.where(rel == jj, rpb_ref[base + j * n_rel_cols + jj], acc)
        tiles.append(jnp.where(in_win, acc, NEG_BIG))
    for j in range(n_rel_rows):
        hi = tiles[j + 1] if j + 1 < n_rel_rows else jnp.full((GRID_W, 2 * GRID_W), NEG_BIG, F32)
        o_ref[0, j] = jnp.where(lane < GRID_W, tiles[j], hi)


def _na_bias_call(rpb):
    n_heads, nrr, nrc = rpb.shape
    body = functools.partial(_na_bias_kernel, n_rel_rows=nrr, n_rel_cols=nrc)
    return pl.pallas_call(
        body,
        grid=(n_heads,),
        in_specs=[pl.BlockSpec(memory_space=pltpu.SMEM)],
        out_specs=pl.BlockSpec((1, nrr, GRID_W, 2 * GRID_W), lambda h: (h, 0, 0, 0)),
        out_shape=jax.ShapeDtypeStruct((n_heads, nrr, GRID_W, 2 * GRID_W), F32),
        compiler_params=_params(("arbitrary",)),
        name="na_bias",
    )(rpb.reshape(-1))


NA_QROWS = 4
NA_KROWS = 12


def _na_kernel(q_ref, k0_ref, k1_ref, k2_ref, v0_ref, v1_ref, v2_ref, kc_ref, vc_ref, tz_ref, o_ref,
               *, hd, n_grid_rows):
    scale = hd ** -0.5
    blk = pl.program_id(1)
    kstart = jnp.clip(blk * NA_QROWS - NA_ROWS // 2, 0, n_grid_rows - NA_KROWS)
    lane = lax.broadcasted_iota(I32, (GRID_W, 2 * GRID_W), 1)
    n_rel = tz_ref.shape[1]
    heads = range(q_ref.shape[1] // hd)
    sls = [slice(hh * hd, (hh + 1) * hd) for hh in heads]
    rel, pen = [], []
    for rq_l in range(NA_QROWS):
        rq = blk * NA_QROWS + rq_l
        r_start = jnp.clip(rq - NA_ROWS // 2, 0, n_grid_rows - NA_ROWS)
        rel.append([])
        pen.append([])
        for m in range(NA_KROWS // 2):
            rk = kstart + 2 * m
            rel[-1].append(jnp.clip(rk - rq + (NA_ROWS - 1), 0, n_rel - 1))
            ok0 = (rk >= r_start) & (rk < r_start + NA_ROWS)
            ok1 = (rk + 1 >= r_start) & (rk + 1 < r_start + NA_ROWS)
            pen[-1].append(jnp.where(lane < GRID_W, jnp.where(ok0, 0.0, NEG_BIG), jnp.where(ok1, 0.0, NEG_BIG)))
    bias = [jnp.concatenate([jnp.concatenate([tz_ref[hh, pl.ds(rel[r][m], 1)][0] + pen[r][m]
                                              for m in range(NA_KROWS // 2)], axis=-1)
                             for r in range(NA_QROWS)], axis=0) for hh in heads]
    q = [q_ref[:, sl] * scale for sl in sls]
    k_loc = [jnp.concatenate([k0_ref[:, sl], k1_ref[:, sl], k2_ref[:, sl]], axis=0) for sl in sls]
    v_loc = [jnp.concatenate([v0_ref[:, sl], v1_ref[:, sl], v2_ref[:, sl]], axis=0).astype(BF16) for sl in sls]
    v_ctx = [vc_ref[0, :, sl].astype(BF16) for sl in sls]
    s_loc = [_bdot_nt(q[h], k_loc[h]) + bias[h] for h in heads]
    s_ctx = [_bdot_nt(q[h], kc_ref[0, :, sls[h]]) for h in heads]
    m = [jnp.maximum(s_loc[h].max(axis=-1, keepdims=True), s_ctx[h].max(axis=-1, keepdims=True)) for h in heads]
    e_loc = [jnp.exp(s_loc[h] - m[h]).astype(BF16) for h in heads]
    e_ctx = [jnp.exp(s_ctx[h] - m[h]).astype(BF16) for h in heads]
    pv = [jnp.dot(e_loc[h], jnp.concatenate([v_loc[h], jnp.ones_like(v_loc[h])], axis=-1),
                  preferred_element_type=F32)
          + jnp.dot(e_ctx[h], jnp.concatenate([v_ctx[h], jnp.ones_like(v_ctx[h])], axis=-1),
                    preferred_element_type=F32) for h in heads]
    for h in heads:
        o_ref[:, sls[h]] = pv[h][:, :hd] / pv[h][:, hd:]


def _na_call(proj, row0, n_seq, t, cache_k, cache_v, tz):
    hd, cw = 64, 256
    d = proj.shape[1] // 3
    ncb = d // cw
    tq = NA_QROWS * GRID_W
    nq = t // tq
    n_grid_rows = t // GRID_W
    t_ctx = cache_k.shape[1]
    qblk0 = row0 // tq

    def kv_map(which, j):
        def index_map(b, i, c):
            ks = jnp.clip(i * NA_QROWS - NA_ROWS // 2, 0, n_grid_rows - NA_KROWS) // NA_QROWS
            return (qblk0 + b * nq + ks + j, which * ncb + c)
        return index_map

    body = functools.partial(_na_kernel, hd=hd, n_grid_rows=n_grid_rows)
    return pl.pallas_call(
        body,
        grid=(n_seq, nq, ncb),
        in_specs=[pl.BlockSpec((tq, cw), lambda b, i, c: (qblk0 + b * nq + i, c))]
                 + [pl.BlockSpec((tq, cw), kv_map(1, j)) for j in range(3)]
                 + [pl.BlockSpec((tq, cw), kv_map(2, j)) for j in range(3)]
                 + [pl.BlockSpec((1, t_ctx, cw), lambda b, i, c: (b, 0, c)),
                    pl.BlockSpec((1, t_ctx, cw), lambda b, i, c: (b, 0, c)),
                    pl.BlockSpec((cw // hd,) + tz.shape[1:], lambda b, i, c: (c, 0, 0, 0))],
        out_specs=pl.BlockSpec((tq, cw), lambda b, i, c: (b * nq + i, c)),
        out_shape=jax.ShapeDtypeStruct((n_seq * t, d), F32),
        compiler_params=_params(("arbitrary", "arbitrary", "arbitrary")),
        name="nbr_attn",
    )(proj, proj, proj, proj, proj, proj, proj, cache_k, cache_v, tz)


def _hgrn_kernel(q_ref, v_ref, f_ref, bf_ref, lb_ref, s0_ref, o_ref, s_ref, st_scr,
                 *, reverse, n_heads, dk, has_s0, n_blk):
    c = pl.program_id(1)
    L = HGRN_CHUNK
    tb = q_ref.shape[0]

    @pl.when(c == 0)
    def _():
        for h in range(n_heads):
            if has_s0:
                st_scr[h] = s0_ref[0, 0, h].T
            else:
                st_scr[h] = jnp.zeros_like(st_scr[h])

    row = lax.broadcasted_iota(I32, (L, L), 0)
    col = lax.broadcasted_iota(I32, (L, L), 1)
    tri = jnp.where((col >= row) if reverse else (col <= row), 1.0, 0.0).astype(F32)
    eye = jnp.where(row == col, 1.0, 0.0).astype(F32)
    halves = [L >> (i + 1) for i in range(L.bit_length() - 1)]
    same_pair = [jnp.where((row // (2 * hf)) == (col // (2 * hf)), 1.0, 0.0).astype(F32) for hf in halves]
    rio = lax.broadcasted_iota(I32, (L, q_ref.shape[1]), 0)
    is_query = [jnp.where(((rio & hf) == 0) if reverse else ((rio & hf) != 0), 1.0, 0.0).astype(F32)
                for hf in halves]
    r8 = lax.broadcasted_iota(I32, (ROW_SUBLANES, 1), 0)
    lb = lb_ref[0]
    bf = bf_ref[0]
    n_chunks = tb // L

    def boundary_rows(cum, hf):
        blk = 2 * hf
        off = hf if reverse else hf - 1
        width = cum.shape[1]
        if blk >= ROW_SUBLANES:
            return jnp.concatenate([jnp.broadcast_to(cum[a + off:a + off + 1, :], (blk, width))
                                    for a in range(0, L, blk)], axis=0)
        groups = []
        for g in range(0, L, ROW_SUBLANES):
            ref = jnp.broadcast_to(cum[g + off:g + off + 1, :], (ROW_SUBLANES, width))
            for a in range(blk, ROW_SUBLANES, blk):
                ref = jnp.where(r8 >= a, jnp.broadcast_to(cum[g + a + off:g + a + off + 1, :],
                                                          (ROW_SUBLANES, width)), ref)
            groups.append(ref)
        return jnp.concatenate(groups, axis=0)

    def chunk_step(jj, carry):
        jc = (n_chunks - 1 - jj) if reverse else jj
        rs = pl.ds(pl.multiple_of(jc * L, L), L)
        q = _silu(q_ref[rs, :]) * (dk ** -0.5)
        v = v_ref[rs, :]
        f = lb + (1.0 - lb) * _sigmoid(f_ref[rs, :] + bf)
        logf = jnp.log(f)
        kk = 1.0 - f
        cum = _fdot(tri, logf)
        end = cum[0:1, :] if reverse else cum[L - 1:L, :]
        qd = q * jnp.exp(cum)
        kd = kk * jnp.exp(end - cum)
        e_end = jnp.exp(end)
        q_lv, k_lv = [], []
        for lv, hf in enumerate(halves):
            isq = is_query[lv]
            decay = jnp.exp((2.0 * isq - 1.0) * (cum - boundary_rows(cum, hf)))
            dq = decay * isq
            q_lv.append((q * dq).astype(BF16))
            k_lv.append((kk * (decay - dq)).astype(BF16))
        qk_diag = q * kk
        hsl = [slice(h * dk, (h + 1) * dk) for h in range(n_heads)]
        inter = [_bdot_nt(qd[:, hs], st_scr[h]) for h, hs in enumerate(hsl)]
        attn = [eye * jnp.sum(qk_diag[:, hs], axis=-1, keepdims=True) for hs in hsl]
        for lv in range(len(halves)):
            prods = [_bdot_nt(q_lv[lv][:, hs], k_lv[lv][:, hs]) for hs in hsl]
            attn = [attn[h] + same_pair[lv] * prods[h] for h in range(n_heads)]
        outs = [inter[h] + _bdot(attn[h], v[:, hs]) for h, hs in enumerate(hsl)]
        o_ref[rs, :] = jnp.concatenate(outs, axis=-1)
        for h in range(n_heads):
            hs = slice(h * dk, (h + 1) * dk)
            st_scr[h] = st_scr[h] * e_end[:, hs] + _bdot_tn(v[:, hs], kd[:, hs])
        return carry

    lax.fori_loop(0, n_chunks, chunk_step, 0)

    @pl.when(c == n_blk - 1)
    def _():
        for h in range(n_heads):
            s_ref[0, 0, h] = st_scr[h].T


def _hgrn_call(proj, row0, n_seq, t, b_f, lb, s0, reverse):
    n_heads, dk = 8, 128
    d = n_heads * dk
    tb = 256
    n_blk = t // tb
    blk0 = row0 // tb
    di = 1 if reverse else 0
    has_s0 = s0 is not None
    if not has_s0:
        s0 = jnp.zeros((1, 2, n_heads, dk, dk), F32)

    def tok(b, c):
        return blk0 + b * n_blk + ((n_blk - 1 - c) if reverse else c)

    body = functools.partial(_hgrn_kernel, reverse=reverse, n_heads=n_heads, dk=dk,
                             has_s0=has_s0, n_blk=n_blk)
    return pl.pallas_call(
        body,
        grid=(n_seq, n_blk),
        in_specs=[pl.BlockSpec((tb, d), lambda b, c: (tok(b, c), 0)),
                  pl.BlockSpec((tb, d), lambda b, c: (tok(b, c), 1)),
                  pl.BlockSpec((tb, d), lambda b, c: (tok(b, c), 2 + di)),
                  pl.BlockSpec((1, 1, d), lambda b, c: (di, 0, 0)),
                  pl.BlockSpec((1, 1, d), lambda b, c: (di, 0, 0)),
                  pl.BlockSpec((1, 1, n_heads, dk, dk),
                               (lambda b, c: (b, di, 0, 0, 0)) if has_s0 else (lambda b, c: (0, 0, 0, 0, 0)))],
        out_specs=[pl.BlockSpec((tb, d), lambda b, c: (tok(b, c) - blk0, 0)),
                   pl.BlockSpec((1, 1, n_heads, dk, dk), lambda b, c: (b, 0, 0, 0, 0))],
        out_shape=[jax.ShapeDtypeStruct((n_seq * t, d), F32),
                   jax.ShapeDtypeStruct((n_seq, 1, n_heads, dk, dk), F32)],
        scratch_shapes=[pltpu.VMEM((n_heads, dk, dk), F32)],
        compiler_params=_params(("arbitrary", "arbitrary")),
        name="hgrn2_bw" if reverse else "hgrn2_fw",
    )(proj, proj, proj, b_f.reshape(2, 1, d), lb.reshape(2, 1, d), s0)


def _log_sigmoid(x):
    return jnp.minimum(x, 0.0) - jnp.log(1.0 + jnp.exp(-jnp.abs(x)))


def _mlstm_kernel(qf_ref, kf_ref, vf_ref, gf_ref, qb_ref, kb_ref, vb_ref, gb_ref, bg_ref, c0_ref, n0_ref, m0_ref,
                  of_ref, ob_ref, c_out, n_out, m_out, c_scr, n_scr, m_scr,
                  *, n_heads, dqk, dv, has_state, n_blk):
    c = pl.program_id(1)
    L = qf_ref.shape[0]

    @pl.when(c == 0)
    def _():
        if has_state:
            c_scr[...] = c0_ref[0]
            n_scr[...] = n0_ref[0]
            m_scr[...] = m0_ref[0]
        else:
            c_scr[...] = jnp.zeros_like(c_scr)
            n_scr[...] = jnp.zeros_like(n_scr)
            m_scr[...] = jnp.zeros_like(m_scr)

    row = lax.broadcasted_iota(I32, (L, L), 0)
    col = lax.broadcasted_iota(I32, (L, L), 1)
    lane = lax.broadcasted_iota(I32, (1, LANES), 1)
    heads = [(s, h) for s in range(2) for h in range(n_heads)]
    refs = [(qf_ref, kf_ref, vf_ref, gf_ref), (qb_ref, kb_ref, vb_ref, gb_ref)]
    causal_s, cum_s, cumt_s, gates_s, gatest_s, m_all = [], [], [], [], [], []
    for s in range(2):
        gates = GATE_SOFTCAP * jnp.tanh((refs[s][3][...] + bg_ref[...]) / GATE_SOFTCAP)
        causal = (col >= row) if s == 1 else (col <= row)
        cum = _fdot(jnp.where(causal, 1.0, 0.0).astype(F32), _log_sigmoid(gates))
        causal_s.append(causal)
        cum_s.append(cum)
        cumt_s.append(cum.T)
        gates_s.append(gates)
        gatest_s.append(gates.T)
        m_all.append(m_scr[s])
    i_off = [0, 2 * n_heads]
    f_off = [n_heads, 3 * n_heads]
    e_row = [L - 1, 0]
    c_prev = {it: c_scr[it[0], it[1]] for it in heads}
    n_prev = {it: n_scr[it[0], it[1]] for it in heads}
    qs = {(s, h): refs[s][0][:, h * dqk:(h + 1) * dqk] * (dqk ** -0.5) for s, h in heads}
    ks_ = {(s, h): refs[s][1][:, h * dqk:(h + 1) * dqk] for s, h in heads}
    vs = {(s, h): refs[s][2][:, h * dv:(h + 1) * dv] for s, h in heads}
    cum_c = {(s, h): cum_s[s][:, f_off[s] + h:f_off[s] + h + 1] for s, h in heads}
    cum_r = {(s, h): cumt_s[s][f_off[s] + h:f_off[s] + h + 1, :] for s, h in heads}
    i_c = {(s, h): gates_s[s][:, i_off[s] + h:i_off[s] + h + 1] for s, h in heads}
    i_r = {(s, h): gatest_s[s][i_off[s] + h:i_off[s] + h + 1, :] for s, h in heads}
    m_prev = {(s, h): m_all[s][0:1, h:h + 1] for s, h in heads}
    d = {h: jnp.where(causal_s[h[0]], cum_c[h] - cum_r[h] + i_r[h], -jnp.inf) for h in heads}
    m_inter = {h: cum_c[h] + m_prev[h] for h in heads}
    m_t = {h: jnp.maximum(m_inter[h], d[h].max(axis=-1, keepdims=True)) for h in heads}
    scores = {h: _bdot_nt(qs[h], ks_[h]) for h in heads}
    inter = {h: _bdot(qs[h], c_prev[h]) for h in heads}
    qn = {h: jnp.sum(qs[h] * n_prev[h], axis=-1, keepdims=True) for h in heads}
    w_inter = {h: jnp.exp(m_inter[h] - m_t[h]) for h in heads}
    qk = {h: scores[h] * jnp.exp(d[h] - m_t[h]) for h in heads}
    num = {h: w_inter[h] * inter[h] + _bdot(qk[h], vs[h]) for h in heads}
    den = {h: w_inter[h] * qn[h] + qk[h].sum(axis=-1, keepdims=True) for h in heads}
    outs = {h: num[h] / jnp.maximum(jnp.abs(den[h]), jnp.exp(-m_t[h])) for h in heads}
    end = {h: cum_c[h][e_row[h[0]]:e_row[h[0]] + 1, :] for h in heads}
    g_end_r = {h: end[h] - cum_r[h] + i_r[h] for h in heads}
    g_end_c = {h: end[h] - cum_c[h] + i_c[h] for h in heads}
    m_new = {h: jnp.maximum(end[h] + m_prev[h], g_end_r[h].max(axis=-1, keepdims=True)) for h in heads}
    w_old = {h: jnp.exp(end[h] + m_prev[h] - m_new[h]) for h in heads}
    kd = {h: ks_[h] * jnp.exp(g_end_c[h] - m_new[h]) for h in heads}
    c_new = {h: w_old[h] * c_prev[h] + _bdot_tn(kd[h], vs[h]) for h in heads}
    n_new = {h: w_old[h] * n_prev[h] + kd[h].sum(axis=0, keepdims=True) for h in heads}
    of_ref[...] = jnp.concatenate([outs[(0, h)] for h in range(n_heads)], axis=-1)
    ob_ref[...] = jnp.concatenate([outs[(1, h)] for h in range(n_heads)], axis=-1)
    for s in range(2):
        m_next = m_all[s]
        for h in range(n_heads):
            c_scr[s, h] = c_new[(s, h)]
            n_scr[s, h] = n_new[(s, h)]
            m_next = jnp.where(lane == h, m_new[(s, h)], m_next)
        m_scr[s] = m_next

    @pl.when(c == n_blk - 1)
    def _():
        c_out[0] = c_scr[...]
        n_out[0] = n_scr[...]
        m_out[0] = m_scr[...]


def _mlstm_call(proj, row0, n_seq, t, b_gates_pad, state):
    n_heads, dqk, dv = 8, 64, 128
    wq, wv = n_heads * dqk, n_heads * dv
    L = 256
    n_blk = t // L
    blk0 = row0 // L
    has_state = state is not None
    if has_state:
        c0, n0, m0 = state
        smap = lambda b, c: (b, 0, 0, 0, 0)
        mmap = lambda b, c: (b, 0, 0, 0)
    else:
        c0 = jnp.zeros((1, 2, n_heads, dqk, dv), F32)
        n0 = jnp.zeros((1, 2, n_heads, 1, dqk), F32)
        m0 = jnp.zeros((1, 2, 1, 128), F32)
        smap = lambda b, c: (0, 0, 0, 0, 0)
        mmap = lambda b, c: (0, 0, 0, 0)

    def tok(b, c, reverse):
        return blk0 + b * n_blk + ((n_blk - 1 - c) if reverse else c)

    body = functools.partial(_mlstm_kernel, n_heads=n_heads, dqk=dqk, dv=dv, has_state=has_state, n_blk=n_blk)
    gate_cb = (2 * wq + 2 * wv) // 128

    def token_specs(reverse):
        return [pl.BlockSpec((L, wq), lambda b, c: (tok(b, c, reverse), 0)),
                pl.BlockSpec((L, wq), lambda b, c: (tok(b, c, reverse), 1)),
                pl.BlockSpec((L, wv), lambda b, c: (tok(b, c, reverse), (2 * wq) // wv)),
                pl.BlockSpec((L, 128), lambda b, c: (tok(b, c, reverse), gate_cb))]

    return pl.pallas_call(
        body,
        grid=(n_seq, n_blk),
        in_specs=token_specs(False) + token_specs(True)
                 + [pl.BlockSpec((1, 128), lambda b, c: (0, 0)),
                    pl.BlockSpec((1, 2, n_heads, dqk, dv), smap),
                    pl.BlockSpec((1, 2, n_heads, 1, dqk), smap),
                    pl.BlockSpec((1, 2, 1, 128), mmap)],
        out_specs=[pl.BlockSpec((L, wv), lambda b, c: (tok(b, c, False) - blk0, 0)),
                   pl.BlockSpec((L, wv), lambda b, c: (tok(b, c, True) - blk0, 0)),
                   pl.BlockSpec((1, 2, n_heads, dqk, dv), lambda b, c: (b, 0, 0, 0, 0)),
                   pl.BlockSpec((1, 2, n_heads, 1, dqk), lambda b, c: (b, 0, 0, 0, 0)),
                   pl.BlockSpec((1, 2, 1, 128), lambda b, c: (b, 0, 0, 0))],
        out_shape=[jax.ShapeDtypeStruct((n_seq * t, wv), F32),
                   jax.ShapeDtypeStruct((n_seq * t, wv), F32),
                   jax.ShapeDtypeStruct((n_seq, 2, n_heads, dqk, dv), F32),
                   jax.ShapeDtypeStruct((n_seq, 2, n_heads, 1, dqk), F32),
                   jax.ShapeDtypeStruct((n_seq, 2, 1, 128), F32)],
        scratch_shapes=[pltpu.VMEM((2, n_heads, dqk, dv), F32),
                        pltpu.VMEM((2, n_heads, 1, dqk), F32),
                        pltpu.VMEM((2, 1, 128), F32)],
        compiler_params=_params(("arbitrary", "arbitrary")),
        name="mlstm",
    )(proj, proj, proj, proj, proj, proj, proj, proj, b_gates_pad, c0, n0, m0)


def _moe_input(x_ref, m_ref, g_ref):
    return _rms(x_ref[...], g_ref[...]) * (1.0 + m_ref[0, 4:5, :]) + m_ref[0, 3:4, :]


def _route_kernel(x_ref, m_ref, g_ref, wr_ref, br_ref, idx_ref, gate_ref, rank_ref, cnt_ref, carry_scr,
                  *, n_experts):
    i = pl.program_id(0)
    tm = x_ref.shape[0]

    @pl.when(i == 0)
    def _():
        carry_scr[...] = jnp.zeros_like(carry_scr)

    h = _moe_input(x_ref, m_ref, g_ref)
    w = wr_ref[...]
    h_hi, w_hi = h.astype(BF16), w.astype(BF16)
    h_lo = (h - h_hi.astype(F32)).astype(BF16)
    w_lo = (w - w_hi.astype(F32)).astype(BF16)
    logits = _bdot_nt(w_hi, h_hi) + _bdot_nt(w_lo, h_hi) + _bdot_nt(w_hi, h_lo) + br_ref[...]
    e_io = lax.broadcasted_iota(I32, (n_experts, tm), 0).astype(F32)
    work = logits
    vals, idxs = [], []
    chosen = jnp.zeros((n_experts, tm), F32)
    for _ in range(TOP_K):
        mx = work.max(axis=0, keepdims=True)
        ix = jnp.min(jnp.where(work == mx, e_io, float(n_experts)), axis=0, keepdims=True)
        hit = e_io == ix
        vals.append(mx)
        idxs.append(ix)
        chosen = jnp.where(hit, 1.0, chosen)
        work = jnp.where(hit, -jnp.inf, work)
    es = [jnp.exp(v - vals[0]) for v in vals]
    den = es[0] + es[1] + es[2] + es[3]
    srow = lax.broadcasted_iota(I32, (tm, tm), 0)
    scol = lax.broadcasted_iota(I32, (tm, tm), 1)
    before = jnp.where(srow < scol, 1.0, 0.0).astype(BF16)
    pos = jnp.dot(chosen.astype(BF16), before, preferred_element_type=F32) + carry_scr[...]
    ranks = [jnp.sum(jnp.where(e_io == ix, pos, 0.0), axis=0, keepdims=True) for ix in idxs]
    carry_scr[...] = carry_scr[...] + chosen.sum(axis=1, keepdims=True)
    idx_ref[...] = jnp.concatenate(idxs, axis=0).astype(I32)
    gate_ref[...] = jnp.concatenate([e / den for e in es], axis=0)
    rank_ref[...] = jnp.concatenate(ranks, axis=0).astype(I32)
    cnt_ref[...] = jnp.broadcast_to(carry_scr[...], cnt_ref.shape).astype(I32)


def _route_call(x, modsel, g, w_router, b_router):
    n, d = x.shape
    n_experts = w_router.shape[1]
    tm = TOKEN_TILE
    body = functools.partial(_route_kernel, n_experts=n_experts)
    return pl.pallas_call(
        body,
        grid=(n // tm,),
        in_specs=[pl.BlockSpec((tm, d), lambda i: (i, 0)),
                  pl.BlockSpec((1, 6, d), lambda i: (i, 0, 0)),
                  pl.BlockSpec((1, d), lambda i: (0, 0)),
                  pl.BlockSpec((n_experts, d), lambda i: (0, 0)),
                  pl.BlockSpec((n_experts, 1), lambda i: (0, 0))],
        out_specs=[pl.BlockSpec((TOP_K, tm), lambda i: (0, i)),
                   pl.BlockSpec((TOP_K, tm), lambda i: (0, i)),
                   pl.BlockSpec((TOP_K, tm), lambda i: (0, i)),
                   pl.BlockSpec((n_experts, 128), lambda i: (0, 0))],
        out_shape=[jax.ShapeDtypeStruct((TOP_K, n), I32),
                   jax.ShapeDtypeStruct((TOP_K, n), F32),
                   jax.ShapeDtypeStruct((TOP_K, n), I32),
                   jax.ShapeDtypeStruct((n_experts, 128), I32)],
        scratch_shapes=[pltpu.VMEM((n_experts, 1), F32)],
        compiler_params=_params(("arbitrary",)),
        name="moe_route",
    )(x, modsel, g.reshape(1, d), w_router.T, b_router.reshape(n_experts, 1))


def _slot_kernel(cnt_ref, idx_ref, rank_ref, dest_ref, binfo_ref, pad_ref, nused_ref, *, n_experts, n_blocks):
    cnt = cnt_ref[:, 0:1].astype(F32)
    padded = jnp.ceil(cnt * (1.0 / MOE_BLOCK)) * MOE_BLOCK
    er = lax.broadcasted_iota(I32, (n_experts, n_experts), 0)
    ec = lax.broadcasted_iota(I32, (n_experts, n_experts), 1)
    start_row = jnp.sum(jnp.where(er < ec, padded, 0.0), axis=0, keepdims=True)
    start_col = jnp.sum(jnp.where(er == ec, start_row, 0.0), axis=1, keepdims=True)
    end_col = start_col + padded
    idx = idx_ref[...]
    e_io = lax.broadcasted_iota(I32, (n_experts,) + idx.shape[1:], 0)
    rows = []
    for k in range(TOP_K):
        hit = e_io == idx[k:k + 1, :]
        rows.append(jnp.sum(jnp.where(hit, start_col, 0.0), axis=0, keepdims=True))
    dest_ref[...] = jnp.concatenate(rows, axis=0).astype(I32) + rank_ref[...]
    blk_start = (lax.broadcasted_iota(I32, (n_experts, n_blocks), 1) * MOE_BLOCK).astype(F32)
    n_done = jnp.sum(jnp.where(end_col <= blk_start, 1.0, 0.0), axis=0, keepdims=True)
    bexp = jnp.minimum(n_done, n_experts - 1.0)
    used_row = jnp.sum(jnp.where(er == ec, jnp.where(cnt > 0.0, 1.0, 0.0), 0.0), axis=0, keepdims=True)
    ecf = ec.astype(F32)
    next_col = jnp.min(jnp.where((ec > er) & (used_row > 0.0), ecf, float(n_experts)), axis=1, keepdims=True)
    ord_col = jnp.sum(jnp.where(ec < er, used_row, 0.0), axis=1, keepdims=True)
    par_col = ord_col - 2.0 * jnp.floor(ord_col * 0.5)
    mine = lax.broadcasted_iota(I32, (n_experts, n_blocks), 0).astype(F32) == bexp
    bnext = jnp.sum(jnp.where(mine, next_col, 0.0), axis=0, keepdims=True)
    bslot = jnp.sum(jnp.where(mine, par_col, 0.0), axis=0, keepdims=True)
    binfo_ref[...] = jnp.concatenate([bexp, bnext, bslot], axis=0).astype(I32)
    n_used = jnp.sum(padded, axis=0, keepdims=True) * (1.0 / MOE_BLOCK)
    nused_ref[...] = n_used.astype(I32)
    cnt_row = jnp.sum(jnp.where(er == ec, cnt, 0.0), axis=0, keepdims=True)
    padded_row = jnp.sum(jnp.where(er == ec, padded, 0.0), axis=0, keepdims=True)
    pad_ref[...] = jnp.concatenate([start_row + cnt_row, padded_row - cnt_row,
                                    jnp.broadcast_to(n_used, cnt_row.shape)], axis=0).astype(I32)


def _slot_call(counts, idx_t, rank_t, n_blocks):
    n_experts = counts.shape[0]
    n = idx_t.shape[1]
    tn = min(2048, n)
    body = functools.partial(_slot_kernel, n_experts=n_experts, n_blocks=n_blocks)
    return pl.pallas_call(
        body,
        grid=(n // tn,),
        in_specs=[pl.BlockSpec((n_experts, 128), lambda i: (0, 0)),
                  pl.BlockSpec((TOP_K, tn), lambda i: (0, i)),
                  pl.BlockSpec((TOP_K, tn), lambda i: (0, i))],
        out_specs=[pl.BlockSpec((TOP_K, tn), lambda i: (0, i)),
                   pl.BlockSpec((3, n_blocks), lambda i: (0, 0)),
                   pl.BlockSpec((3, n_experts), lambda i: (0, 0)),
                   pl.BlockSpec((1, 1), lambda i: (0, 0))],
        out_shape=[jax.ShapeDtypeStruct((TOP_K, n), I32),
                   jax.ShapeDtypeStruct((3, n_blocks), I32),
                   jax.ShapeDtypeStruct((3, n_experts), I32),
                   jax.ShapeDtypeStruct((1, 1), I32)],
        compiler_params=_params(("arbitrary",)),
        name="moe_slots",
    )(counts, idx_t, rank_t)


DMA_ISSUE_UNROLL = 8


def _to_row_tiles(ref, base, x):
    rows = x.shape[0]
    for c in range(ROW_SUBLANES):
        ref[pl.ds(base * ROW_SUBLANES + c, rows, stride=ROW_SUBLANES), :] = x[:, c * LANES:(c + 1) * LANES]


def _from_row_tiles(ref, base, rows, c):
    return ref[pl.ds(base * ROW_SUBLANES + c, rows, stride=ROW_SUBLANES), :]


def _row_tile(ref, r):
    return ref.at[pl.ds(pl.multiple_of(r * ROW_SUBLANES, ROW_SUBLANES), ROW_SUBLANES)]


def _zero_fill_padding(pad_ref, xs_ref, z_scr, sem):
    z_scr[...] = jnp.zeros_like(z_scr)
    n_experts = pad_ref.shape[1]
    bits = range(MOE_BLOCK.bit_length() - 2, -1, -1)

    def pieces(e):
        off, length = pad_ref[0, e], pad_ref[1, e]
        for bit in bits:
            size = 1 << bit
            done = (length >> (bit + 1)) << (bit + 1)
            copy = pltpu.make_async_copy(z_scr.at[pl.ds(0, size * ROW_SUBLANES)],
                                         xs_ref.at[pl.ds(pl.multiple_of((off + done) * ROW_SUBLANES, ROW_SUBLANES),
                                                         size * ROW_SUBLANES)], sem)
            yield (length & size) != 0, copy

    def tail_blocks():
        n_blocks = xs_ref.shape[0] // (MOE_BLOCK * ROW_SUBLANES)
        for b in range(n_blocks - n_experts, n_blocks):
            copy = pltpu.make_async_copy(z_scr, xs_ref.at[pl.ds(b * MOE_BLOCK * ROW_SUBLANES,
                                                                MOE_BLOCK * ROW_SUBLANES)], sem)
            yield b >= pad_ref[2, 0], copy

    def all_copies():
        for e in range(n_experts):
            yield from pieces(e)
        yield from tail_blocks()

    for needed, copy in all_copies():
        pl.when(needed)(copy.start)
    for needed, copy in all_copies():
        pl.when(needed)(copy.wait)


def _dispatch_kernel(pad_ref, dest_ref, x_ref, m_ref, g_ref, xs_ref, h_scr, z_scr, sem, *, n_tiles):
    i = pl.program_id(0)
    tm = x_ref.shape[0]

    @pl.when(i == 0)
    def _():
        _zero_fill_padding(pad_ref, xs_ref, z_scr, sem.at[2])

    slot = lax.rem(i, 2)
    src = h_scr.at[slot]
    _to_row_tiles(src, 0, _moe_input(x_ref, m_ref, g_ref))

    def start_row(r, carry):
        for k in range(TOP_K):
            pltpu.make_async_copy(_row_tile(src, r), _row_tile(xs_ref, dest_ref[0, 0, k * tm + r]),
                                  sem.at[slot]).start(priority=k % 2)
        return carry

    lax.fori_loop(0, tm, start_row, 0, unroll=DMA_ISSUE_UNROLL // TOP_K)

    def wait_tile(s):
        for _ in range(TOP_K):
            pltpu.make_async_copy(h_scr.at[s], xs_ref.at[pl.ds(0, tm * ROW_SUBLANES)], sem.at[s]).wait()

    @pl.when(i >= 1)
    def _():
        wait_tile(1 - slot)

    @pl.when(i == n_tiles - 1)
    def _():
        wait_tile(slot)


def _dispatch_call(pad_info, dest_tiles, x, modsel, g, n_slots):
    n, d = x.shape
    tm = TOKEN_TILE
    assert d == ROW_SUBLANES * LANES
    return pl.pallas_call(
        functools.partial(_dispatch_kernel, n_tiles=n // tm),
        grid=(n // tm,),
        in_specs=[pl.BlockSpec(memory_space=pltpu.SMEM),
                  pl.BlockSpec((1, 1, TOP_K * tm), lambda i: (i, 0, 0), memory_space=pltpu.SMEM),
                  pl.BlockSpec((tm, d), lambda i: (i, 0)),
                  pl.BlockSpec((1, 6, d), lambda i: (i, 0, 0)),
                  pl.BlockSpec((1, d), lambda i: (0, 0))],
        out_specs=pl.BlockSpec(memory_space=pl.ANY),
        out_shape=jax.ShapeDtypeStruct((n_slots * ROW_SUBLANES, LANES), F32),
        scratch_shapes=[pltpu.VMEM((2, tm * ROW_SUBLANES, LANES), F32),
                        pltpu.VMEM((MOE_BLOCK * ROW_SUBLANES, LANES), F32),
                        pltpu.SemaphoreType.DMA((3,))],
        compiler_params=_params(("arbitrary",)),
        name="moe_dispatch",
    )(pad_info, dest_tiles, x, modsel, g.reshape(1, d))


def _ffn_kernel(binfo_ref, nused_ref, xs_ref, wgu_hbm, bgu_ref, wdn_hbm, bdn_ref, ys_ref,
                wgu_f32, wdn_f32, wgu_scr, wdn_scr, sem, *, layer, n_experts, n_blocks):
    b = pl.program_id(0)
    d_ff = wdn_scr.shape[0]
    rows = xs_ref.shape[0] // ROW_SUBLANES

    def weight_copies(e, slot):
        return (pltpu.make_async_copy(wgu_hbm.at[layer, e], wgu_f32.at[slot], sem.at[slot]),
                pltpu.make_async_copy(wdn_hbm.at[layer, e], wdn_f32.at[slot], sem.at[slot]))

    @pl.when(b < nused_ref[0])
    def _():
        e = binfo_ref[b]
        prev = binfo_ref[jnp.maximum(b - 1, 0)]
        slot = binfo_ref[2 * n_blocks + b]

        @pl.when((b == 0) | (e != prev))
        def _():
            @pl.when(b == 0)
            def _():
                for cp in weight_copies(e, slot):
                    cp.start()

            nxt = binfo_ref[n_blocks + b]

            @pl.when(nxt < n_experts)
            def _():
                for cp in weight_copies(nxt, 1 - slot):
                    cp.start()

            for cp in weight_copies(e, slot):
                cp.wait()
            wgu_scr[...] = wgu_f32[slot].astype(BF16)
            wdn_scr[...] = wdn_f32[slot].astype(BF16)

        x = jnp.concatenate([_from_row_tiles(xs_ref, 0, rows, c).astype(BF16) for c in range(ROW_SUBLANES)],
                            axis=-1)
        gu = jnp.dot(x, wgu_scr[...], preferred_element_type=F32) + bgu_ref[0, 0]
        x_glu = jnp.minimum(gu[:, :d_ff], SWIGLU_LIMIT)
        x_lin = jnp.clip(gu[:, d_ff:], -SWIGLU_LIMIT, SWIGLU_LIMIT)
        hid = x_glu * _sigmoid(SWIGLU_ALPHA * x_glu) * (x_lin + 1.0)
        _to_row_tiles(ys_ref, 0, jnp.dot(hid.astype(BF16), wdn_scr[...], preferred_element_type=F32)
                      + bdn_ref[0, 0])

    @pl.when(b >= nused_ref[0])
    def _():
        ys_ref[...] = jnp.zeros_like(ys_ref)


def _ffn_call(layer, block_info, n_used, xs, w_gu, b_gu, w_dn, b_dn):
    depth, n_experts, d, d_ff2 = w_gu.shape
    d_ff = d_ff2 // 2
    blk_rows = MOE_BLOCK * ROW_SUBLANES
    n_blocks = xs.shape[0] // blk_rows

    def blk(b, nu):
        return jnp.maximum(jnp.minimum(b, nu[0] - 1), 0)

    grid_spec = pltpu.PrefetchScalarGridSpec(
        num_scalar_prefetch=2,
        grid=(n_blocks,),
        in_specs=[pl.BlockSpec((blk_rows, LANES), lambda b, bi, nu: (blk(b, nu), 0)),
                  pl.BlockSpec(memory_space=pl.ANY),
                  pl.BlockSpec((1, 1, 1, d_ff2), lambda b, bi, nu: (layer, bi[blk(b, nu)], 0, 0)),
                  pl.BlockSpec(memory_space=pl.ANY),
                  pl.BlockSpec((1, 1, 1, d), lambda b, bi, nu: (layer, bi[blk(b, nu)], 0, 0))],
        out_specs=pl.BlockSpec((blk_rows, LANES), lambda b, bi, nu: (b, 0)),
        scratch_shapes=[pltpu.VMEM((2, d, d_ff2), F32), pltpu.VMEM((2, d_ff, d), F32),
                        pltpu.VMEM((d, d_ff2), BF16), pltpu.VMEM((d_ff, d), BF16),
                        pltpu.SemaphoreType.DMA((2,))],
    )
    body = functools.partial(_ffn_kernel, layer=layer, n_experts=n_experts, n_blocks=n_blocks)
    return pl.pallas_call(
        body,
        grid_spec=grid_spec,
        out_shape=jax.ShapeDtypeStruct(xs.shape, F32),
        compiler_params=_params(("arbitrary",)),
        name="moe_ffn",
    )(block_info, n_used, xs, w_gu, b_gu.reshape(depth, n_experts, 1, d_ff2), w_dn,
      b_dn.reshape(depth, n_experts, 1, d))


def _gather_expert_rows(dest_ref, dest_next_ref, ys_ref, gate_ref, buf, sem, n_tiles):
    i = pl.program_id(0)
    tm = gate_ref.shape[0]
    rows = TOP_K * tm

    def start_gathers(d_ref, slot):
        def start_pair(p, carry):
            for u in range(2):
                j = 2 * p + u
                pltpu.make_async_copy(_row_tile(ys_ref, d_ref[0, 0, j]), _row_tile(buf.at[slot], j),
                                      sem.at[slot]).start(priority=u)
            return carry

        lax.fori_loop(0, rows // 2, start_pair, 0, unroll=DMA_ISSUE_UNROLL // 2)

    slot = lax.rem(i, 2)

    @pl.when(i == 0)
    def _():
        start_gathers(dest_ref, 0)

    def wait_tile(s):
        pltpu.make_async_copy(ys_ref.at[pl.ds(0, rows * ROW_SUBLANES)], buf.at[s], sem.at[s]).wait()

    @pl.when(i + 1 < n_tiles)
    def _():
        start_gathers(dest_next_ref, 1 - slot)

    wait_tile(slot)
    cur = buf.at[slot]
    chunks = []
    for c in range(ROW_SUBLANES):
        y = gate_ref[:, 0:1] * _from_row_tiles(cur, 0, tm, c)
        for k in range(1, TOP_K):
            y = y + gate_ref[:, k:k + 1] * _from_row_tiles(cur, k * tm, tm, c)
        chunks.append(y)
    return jnp.concatenate(chunks, axis=-1)


def _combine_kernel(dest_ref, dest_next_ref, ys_ref, gate_ref, x_ref, m_ref, fg_ref, y_ref, buf, sem,
                    *, n_tiles):
    y = _gather_expert_rows(dest_ref, dest_next_ref, ys_ref, gate_ref, buf, sem, n_tiles)
    y_ref[...] = _rms(x_ref[...] + m_ref[0, 5:6, :] * y, fg_ref[...])


def _combine_inproj_kernel(dest_ref, dest_next_ref, ys_ref, gate_ref, x_ref, mp_ref, mc_ref, g_ref, w_ref,
                           x_out_ref, p_ref, buf, sem, *, n_tiles):
    y = _gather_expert_rows(dest_ref, dest_next_ref, ys_ref, gate_ref, buf, sem, n_tiles)
    x = x_ref[...] + mp_ref[0, 5:6, :] * y
    x_out_ref[...] = x
    h = _rms(x, g_ref[...]) * (1.0 + mc_ref[0, 1:2, :]) + mc_ref[0, 0:1, :]
    p_ref[...] = _bdot(h, w_ref[...])


def _moe_gather_specs(tm, n_tiles, d):
    return [pl.BlockSpec((1, 1, TOP_K * tm), lambda i: (i, 0, 0), memory_space=pltpu.SMEM),
            pl.BlockSpec((1, 1, TOP_K * tm), lambda i: (jnp.minimum(i + 1, n_tiles - 1), 0, 0),
                         memory_space=pltpu.SMEM),
            pl.BlockSpec(memory_space=pl.ANY),
            pl.BlockSpec((tm, TOP_K), lambda i: (i, 0)),
            pl.BlockSpec((tm, d), lambda i: (i, 0))]


def _moe_gather_scratch(tm):
    return [pltpu.VMEM((2, TOP_K * tm * ROW_SUBLANES, LANES), F32), pltpu.SemaphoreType.DMA((2,))]


def _combine_inproj_call(pending, x, modsel_prev, modsel, g, w):
    dest_tiles, ys, gates_nk = pending
    n, d = x.shape
    wout = w.shape[1]
    tm = TOKEN_TILE
    n_tiles = n // tm
    body = functools.partial(_combine_inproj_kernel, n_tiles=n_tiles)
    return pl.pallas_call(
        body,
        grid=(n_tiles,),
        in_specs=_moe_gather_specs(tm, n_tiles, d)
                 + [pl.BlockSpec((1, 6, d), lambda i: (i, 0, 0)),
                    pl.BlockSpec((1, 6, d), lambda i: (i, 0, 0)),
                    pl.BlockSpec((1, d), lambda i: (0, 0)),
                    pl.BlockSpec((d, wout), lambda i: (0, 0))],
        out_specs=[pl.BlockSpec((tm, d), lambda i: (i, 0)),
                   pl.BlockSpec((tm, wout), lambda i: (i, 0))],
        out_shape=[jax.ShapeDtypeStruct((n, d), F32),
                   jax.ShapeDtypeStruct((n, wout), F32)],
        scratch_shapes=_moe_gather_scratch(tm),
        compiler_params=_params(("arbitrary",)),
        name="moe_combine_inproj",
    )(dest_tiles, dest_tiles, ys, gates_nk, x, modsel_prev, modsel, g.reshape(1, d), w)


def _combine_call(pending, x, modsel, final_g):
    dest_tiles, ys, gates_nk = pending
    n, d = x.shape
    tm = TOKEN_TILE
    n_tiles = n // tm
    body = functools.partial(_combine_kernel, n_tiles=n_tiles)
    return pl.pallas_call(
        body,
        grid=(n_tiles,),
        in_specs=_moe_gather_specs(tm, n_tiles, d)
                 + [pl.BlockSpec((1, 6, d), lambda i: (i, 0, 0)),
                    pl.BlockSpec((1, d), lambda i: (0, 0))],
        out_specs=pl.BlockSpec((tm, d), lambda i: (i, 0)),
        out_shape=jax.ShapeDtypeStruct((n, d), F32),
        scratch_shapes=_moe_gather_scratch(tm),
        compiler_params=_params(("arbitrary",)),
        name="moe_combine",
    )(dest_tiles, dest_tiles, ys, gates_nk, x, modsel, final_g.reshape(1, d))


def _moe_experts(layer, x, modsel, g2, w_router, b_router, w_gu, b_gu, w_dn, b_dn):
    n, d = x.shape
    n_experts = w_router.shape[1]
    tm = TOKEN_TILE
    n_blocks = (n * TOP_K) // MOE_BLOCK + n_experts
    idx_t, gate_t, rank_t, counts = _route_call(x, modsel, g2, w_router, b_router)
    dest_t, block_info, pad_info, n_used = _slot_call(counts, idx_t, rank_t, n_blocks)
    dest_tiles = dest_t.reshape(TOP_K, n // tm, tm).transpose(1, 0, 2).reshape(n // tm, 1, TOP_K * tm)
    xs = _dispatch_call(pad_info, dest_tiles, x, modsel, g2, n_blocks * MOE_BLOCK)
    ys = _ffn_call(layer, block_info.reshape(3 * n_blocks), n_used.reshape(1), xs, w_gu, b_gu, w_dn, b_dn)
    return dest_tiles, ys, gate_t.T


def _rope_tables(t, hd):
    pos = np.arange(t)
    n_freq = hd // 4
    inv_freq = ROPE_THETA ** (-np.arange(n_freq, dtype=np.float32) / n_freq)
    ang = np.concatenate([(pos // GRID_W).astype(np.float32)[:, None] * inv_freq,
                          (pos % GRID_W).astype(np.float32)[:, None] * inv_freq], axis=-1)
    ang = jnp.asarray(ang, F32)
    cos, sin = jnp.cos(ang), jnp.sin(ang)
    return jnp.concatenate([cos, cos], axis=-1), jnp.concatenate([-sin, sin], axis=-1)


def kernel(x_prompt, x_sample, cache_k_a, cache_v_a, state_b, state_c_C, state_c_n, state_c_m, cache_k_d, cache_v_d, c, c_ctx, norm1_g, norm2_g, w_mod, b_mod, w_in_a, qnorm_a, knorm_a, w_out_a, w_in_b, b_f_b, lower_bounds_b, onorm_b, w_out_b, w_in_c, b_gates_c, onorm_c, w_out_c, w_in_d, rpb_d, w_out_d, w_router, b_router, w_gu, b_gu, w_dn, b_dn, final_g):
    n_ctx_seq, t_ctx, d = x_prompt.shape
    n_lat_seq, t_lat, _ = x_sample.shape
    depth = w_mod.shape[0]
    n_ctx = n_ctx_seq * t_ctx
    n_lat = n_lat_seq * t_lat
    n = n_ctx + n_lat
    tm = TOKEN_TILE
    assert t_ctx % tm == 0 and t_lat % tm == 0 and n_lat_seq + 1 <= 8

    lb_cum = jnp.cumsum(jax.nn.softmax(lower_bounds_b.astype(F32), axis=0), axis=0)
    lb_all = lb_cum - lb_cum[0]

    cond8 = jnp.zeros((8, d), F32).at[0].set(c_ctx).at[1:1 + n_lat_seq].set(c)
    mod = _mod_call(cond8, w_mod, b_mod)
    tile_row = np.concatenate([np.zeros(n_ctx // tm, np.int32),
                               1 + np.repeat(np.arange(n_lat_seq, dtype=np.int32), t_lat // tm)])

    x = jnp.concatenate([x_prompt.reshape(n_ctx, d), x_sample.reshape(n_lat, d)], axis=0)
    outs = {}
    pending = None
    modsel = None

    def inproj(x, modsel, g, w):
        if pending is None:
            return x, _inproj_call(x, modsel, g, w)
        return _combine_inproj_call(pending, x, modsel_prev, modsel, g, w)

    for i in range(depth):
        kind = i % 4
        j = i // 4
        modsel_prev = modsel
        modsel = mod[i].reshape(8, 6, d)[tile_row]
        if kind == 0:
            x, proj = inproj(x, modsel, norm1_g[i],w_in_a[j].astype(BF16))
            o_ctx, k_new = _gqa_ctx_call(proj, qnorm_a[j], knorm_a[j], n_ctx_seq, t_ctx)
            cosd, sind = _rope_tables(t_lat, 128)
            o_lat = _gqa_lat_call(proj, n_ctx, n_lat_seq, t_lat,
                                  cache_k_a[:, j].reshape(n_lat_seq, -1, 256),
                                  cache_v_a[:, j].reshape(n_lat_seq, -1, 256), cosd, sind,
                                  qnorm_a[j], knorm_a[j])
            outs["k_a"] = k_new.reshape(n_ctx_seq, 1, t_ctx, 2, 128)
            outs["v_a"] = proj[:n_ctx, 1280:1536].reshape(n_ctx_seq, 1, t_ctx, 2, 128)
            x = _outproj_call("plain", [(o_ctx, o_lat)], None, w_out_a[j].astype(BF16), x, modsel)
        elif kind == 1:
            x, proj = inproj(x, modsel, norm1_g[i],w_in_b[j].astype(BF16))
            o_dirs, s_dirs = [], []
            for reverse in (False, True):
                oc, sc = _hgrn_call(proj, 0, n_ctx_seq, t_ctx, b_f_b[j], lb_all[i], None, reverse)
                ol, _ = _hgrn_call(proj, n_ctx, n_lat_seq, t_lat, b_f_b[j], lb_all[i], state_b[:, j], reverse)
                o_dirs.append((oc, ol))
                s_dirs.append(sc)
            outs["s_b"] = jnp.concatenate(s_dirs, axis=1)[:, None]
            x = _outproj_call("hgrn", o_dirs, (proj, 4, onorm_b[j].reshape(1, 128)),
                              w_out_b[j].astype(BF16), x, modsel)
        elif kind == 2:
            w_c = jnp.pad(w_in_c[j], ((0, 0), (0, 128 - 32))).astype(BF16)
            bg = jnp.pad(b_gates_c[j].reshape(1, 32), ((0, 0), (0, 128 - 32)))
            x, proj = inproj(x, modsel, norm1_g[i],w_c)
            state = (state_c_C[:, j], state_c_n[:, j][:, :, :, None, :],
                     jnp.pad(state_c_m[:, j], ((0, 0), (0, 0), (0, 120)))[:, :, None, :])
            ocf, ocb, cc, nc, mc = _mlstm_call(proj, 0, n_ctx_seq, t_ctx, bg, None)
            olf, olb, _, _, _ = _mlstm_call(proj, n_ctx, n_lat_seq, t_lat, bg, state)
            o_dirs = [(ocf, olf), (ocb, olb)]
            outs["c_C"] = cc[:, None]
            outs["c_n"] = nc[:, None, :, :, 0, :]
            outs["c_m"] = mc[:, None, :, 0, :8]
            x = _outproj_call("mlstm", o_dirs, (proj, 2, onorm_c[j].reshape(1, 128)),
                              w_out_c[j].astype(BF16), x, modsel)
        else:
            x, proj = inproj(x, modsel, norm1_g[i],w_in_d[j].astype(BF16))
            o_ctx = _mha_ctx_call(proj, n_ctx_seq, t_ctx)
            tz = _na_bias_call(rpb_d[j])
            o_lat = _na_call(proj, n_ctx, n_lat_seq, t_lat,
                             cache_k_d[:, j].reshape(n_lat_seq, -1, d),
                             cache_v_d[:, j].reshape(n_lat_seq, -1, d), tz)
            outs["k_d"] = proj[:n_ctx, d:2 * d].reshape(n_ctx_seq, 1, t_ctx, 16, 64)
            outs["v_d"] = proj[:n_ctx, 2 * d:3 * d].reshape(n_ctx_seq, 1, t_ctx, 16, 64)
            x = _outproj_call("plain", [(o_ctx, o_lat)], None, w_out_d[j].astype(BF16), x, modsel)
        pending = _moe_experts(i, x, modsel, norm2_g[i], w_router[i], b_router[i], w_gu, b_gu, w_dn, b_dn)
    x = _combine_call(pending, x, modsel, final_g)

    y_prompt = x[:n_ctx].reshape(n_ctx_seq, t_ctx, d)
    y_sample = x[n_ctx:].reshape(n_lat_seq, t_lat, d)
    return (y_prompt, y_sample, outs["k_a"], outs["v_a"], outs["s_b"], outs["c_C"], outs["c_n"], outs["c_m"],
            outs["k_d"], outs["v_d"])
```

```python
import functools

import numpy as np
import jax
import jax.numpy as jnp
from jax import lax
from jax.experimental import pallas as pl
from jax.experimental.pallas import tpu as pltpu

F32 = jnp.float32
BF16 = jnp.bfloat16
I32 = jnp.int32

NORM_EPS = 1e-6
GRID_W = 64
ROPE_THETA = 10000.0
TOP_K = 4
HGRN_CHUNK = 128
GATE_SOFTCAP = 15.0
NA_ROWS = 8
NA_COLS = 16
SWIGLU_ALPHA = 1.702
SWIGLU_LIMIT = 7.0
NEG_BIG = -1e30

LANES = 128
ROW_SUBLANES = 8
TOKEN_TILE = 256
MOE_BLOCK = 256
ATTN_HEAD_GROUP = 8
V7X_VMEM_LIMIT = 52 * 1024 * 1024


def _params(sem, vmem=V7X_VMEM_LIMIT):
    return pltpu.CompilerParams(dimension_semantics=sem, vmem_limit_bytes=vmem)


def _bdot(a, b):
    return jnp.dot(a.astype(BF16), b.astype(BF16), preferred_element_type=F32)


def _bdot_nt(a, b):
    return lax.dot_general(a.astype(BF16), b.astype(BF16), (((1,), (1,)), ((), ())),
                           preferred_element_type=F32)


def _bdot_tn(a, b):
    return lax.dot_general(a.astype(BF16), b.astype(BF16), (((0,), (0,)), ((), ())),
                           preferred_element_type=F32)


def _fdot(a, b):
    return jnp.dot(a, b, preferred_element_type=F32, precision=lax.Precision.HIGHEST)


def _rms(x, g):
    return x * lax.rsqrt(jnp.mean(x * x, axis=-1, keepdims=True) + NORM_EPS) * g


def _rms_heads(x, g, n_heads, hd):
    return jnp.concatenate([_rms(x[:, h * hd:(h + 1) * hd], g) for h in range(n_heads)], axis=-1)


def _sigmoid(x):
    return 1.0 / (1.0 + jnp.exp(-x))


def _silu(x):
    return x * _sigmoid(x)


def _softmax_rows(parts):
    m = parts[0].max(axis=-1, keepdims=True)
    for p in parts[1:]:
        m = jnp.maximum(m, p.max(axis=-1, keepdims=True))
    es = [jnp.exp(p - m) for p in parts]
    den = es[0].sum(axis=-1, keepdims=True)
    for e in es[1:]:
        den = den + e.sum(axis=-1, keepdims=True)
    return es, den


def _mod_kernel(c_ref, w_ref, b_ref, o_ref):
    o_ref[0] = _bdot(_silu(c_ref[...]), w_ref[0]) + b_ref[0]


def _mod_call(cond8, w_mod, b_mod):
    depth, d, d6 = w_mod.shape
    tn = 1024
    return pl.pallas_call(
        _mod_kernel,
        grid=(depth, d6 // tn),
        in_specs=[pl.BlockSpec((8, d), lambda i, j: (0, 0)),
                  pl.BlockSpec((1, d, tn), lambda i, j: (i, 0, j)),
                  pl.BlockSpec((1, 1, tn), lambda i, j: (i, 0, j))],
        out_specs=pl.BlockSpec((1, 8, tn), lambda i, j: (i, 0, j)),
        out_shape=jax.ShapeDtypeStruct((depth, 8, d6), F32),
        compiler_params=_params(("arbitrary", "arbitrary")),
        name="adaln_mod",
    )(cond8, w_mod, b_mod.reshape(depth, 1, d6))


def _inproj_kernel(x_ref, m_ref, g_ref, w_ref, o_ref):
    h = _rms(x_ref[...], g_ref[...]) * (1.0 + m_ref[0, 1:2, :]) + m_ref[0, 0:1, :]
    o_ref[...] = _bdot(h, w_ref[...])


def _inproj_call(x, modsel, g, w):
    n, d = x.shape
    wout = w.shape[1]
    tm = TOKEN_TILE
    return pl.pallas_call(
        _inproj_kernel,
        grid=(n // tm,),
        in_specs=[pl.BlockSpec((tm, d), lambda i: (i, 0)),
                  pl.BlockSpec((1, 6, d), lambda i: (i, 0, 0)),
                  pl.BlockSpec((1, d), lambda i: (0, 0)),
                  pl.BlockSpec((d, wout), lambda i: (0, 0))],
        out_specs=pl.BlockSpec((tm, wout), lambda i: (i, 0)),
        out_shape=jax.ShapeDtypeStruct((n, wout), F32),
        compiler_params=_params(("arbitrary",)),
        name="inproj",
    )(x, modsel, g.reshape(1, d), w)


def _outproj_kernel(*refs, kind, n_pairs, n_ctx_tiles, n_heads, hd):
    in_ctx = pl.program_id(0) < n_ctx_tiles
    mix = [jnp.where(in_ctx, refs[2 * p][...], refs[2 * p + 1][...]) for p in range(n_pairs)]
    rest = refs[2 * n_pairs:]
    if kind == "plain":
        w_ref, x_ref, m_ref, y_ref = rest
        o = mix[0]
    else:
        gate_ref, on_ref, w_ref, x_ref, m_ref, y_ref = rest
        o = _rms_heads(mix[0] + mix[1], on_ref[...], n_heads, hd)
        o = o * _silu(gate_ref[...]) if kind == "hgrn" else _sigmoid(gate_ref[...]) * o
    y_ref[...] = x_ref[...] + m_ref[0, 2:3, :] * _bdot(o, w_ref[...])


def _outproj_call(kind, pairs, gate, w, x, modsel):
    n, d = x.shape
    tm = TOKEN_TILE
    n_ctx_tiles = pairs[0][0].shape[0] // tm
    specs, args = [], []
    for a_ctx, a_lat in pairs:
        specs += [pl.BlockSpec((tm, d), lambda i: (jnp.minimum(i, n_ctx_tiles - 1), 0)),
                  pl.BlockSpec((tm, d), lambda i: (jnp.maximum(i - n_ctx_tiles, 0), 0))]
        args += [a_ctx, a_lat]
    if gate is not None:
        proj, cb, on = gate
        specs += [pl.BlockSpec((tm, d), lambda i: (i, cb)), pl.BlockSpec(on.shape, lambda i: (0, 0))]
        args += [proj, on]
    specs += [pl.BlockSpec(w.shape, lambda i: (0, 0)),
              pl.BlockSpec((tm, d), lambda i: (i, 0)),
              pl.BlockSpec((1, 6, d), lambda i: (i, 0, 0))]
    args += [w, x, modsel]
    body = functools.partial(_outproj_kernel, kind=kind, n_pairs=len(pairs), n_ctx_tiles=n_ctx_tiles,
                             n_heads=8, hd=128)
    return pl.pallas_call(
        body,
        grid=(n // tm,),
        in_specs=specs,
        out_specs=pl.BlockSpec((tm, d), lambda i: (i, 0)),
        out_shape=jax.ShapeDtypeStruct((n, d), F32),
        compiler_params=_params(("arbitrary",)),
        name="outproj_" + kind,
    )(*args)


def _gqa_ctx_kernel(p_ref, qn_ref, kn_ref, o_ref, k_ref, *, n_heads, n_kv, hd):
    rep = n_heads // n_kv
    scale = hd ** -0.5
    koff = n_heads * hd
    voff = koff + n_kv * hd
    ks = [_rms(p_ref[:, koff + g * hd: koff + (g + 1) * hd], kn_ref[...]) for g in range(n_kv)]
    k_ref[...] = jnp.concatenate(ks, axis=-1)
    for h in range(n_heads):
        g = h // rep
        q = _rms(p_ref[:, h * hd:(h + 1) * hd], qn_ref[...])
        s = _bdot_nt(q, ks[g]) * scale
        (e,), den = _softmax_rows([s])
        o_ref[:, h * hd:(h + 1) * hd] = _bdot(e / den, p_ref[:, voff + g * hd: voff + (g + 1) * hd])


def _gqa_ctx_call(proj, qn, kn, n_seq, t):
    n_heads, n_kv, hd = 8, 2, 128
    win = proj.shape[1]
    body = functools.partial(_gqa_ctx_kernel, n_heads=n_heads, n_kv=n_kv, hd=hd)
    return pl.pallas_call(
        body,
        grid=(n_seq,),
        in_specs=[pl.BlockSpec((t, win), lambda b: (b, 0)),
                  pl.BlockSpec((1, hd), lambda b: (0, 0)),
                  pl.BlockSpec((1, hd), lambda b: (0, 0))],
        out_specs=[pl.BlockSpec((t, n_heads * hd), lambda b: (b, 0)),
                   pl.BlockSpec((t, n_kv * hd), lambda b: (b, 0))],
        out_shape=[jax.ShapeDtypeStruct((n_seq * t, n_heads * hd), F32),
                   jax.ShapeDtypeStruct((n_seq * t, n_kv * hd), F32)],
        compiler_params=_params(("arbitrary",)),
        name="gqa_ctx",
    )(proj, qn.reshape(1, hd), kn.reshape(1, hd))


def _rope(x, cosd, sind):
    return x * cosd + pltpu.roll(x, x.shape[-1] // 2, 1) * sind


def _gqa_lat_kernel(pq_ref, pkv_ref, kc_ref, vc_ref, cq_ref, sq_ref, ck_ref, sk_ref, qn_ref, kn_ref,
                    o_ref, k_scr, v_scr, *, n_heads, n_kv, hd, t_ctx):
    rep = n_heads // n_kv
    scale = hd ** -0.5

    @pl.when(pl.program_id(1) == 0)
    def _():
        k_scr[0:t_ctx, :] = kc_ref[0].astype(BF16)
        v_scr[...] = jnp.ones_like(v_scr)
        for g in range(n_kv):
            k = _rms(pkv_ref[:, g * hd:(g + 1) * hd], kn_ref[...])
            k_scr[t_ctx:, g * hd:(g + 1) * hd] = _rope(k, ck_ref[...], sk_ref[...]).astype(BF16)
            v_scr[0:t_ctx, 2 * g * hd:(2 * g + 1) * hd] = vc_ref[0, :, g * hd:(g + 1) * hd].astype(BF16)
            v_scr[t_ctx:, 2 * g * hd:(2 * g + 1) * hd] = pkv_ref[:, (n_kv + g) * hd:(n_kv + g + 1) * hd].astype(BF16)

    qs = [(_rope(_rms(pq_ref[:, h * hd:(h + 1) * hd], qn_ref[...]), cq_ref[...], sq_ref[...]) * scale).astype(BF16)
          for h in range(n_heads)]
    for h0 in range(0, n_heads, ATTN_HEAD_GROUP):
        hs = range(h0, min(h0 + ATTN_HEAD_GROUP, n_heads))
        s = [_bdot_nt(qs[h], k_scr[:, (h // rep) * hd:(h // rep + 1) * hd]) for h in hs]
        e = [jnp.exp(x - x.max(axis=-1, keepdims=True)).astype(BF16) for x in s]
        pv = [jnp.dot(e[i], v_scr[:, 2 * (h // rep) * hd:2 * (h // rep + 1) * hd], preferred_element_type=F32)
              for i, h in enumerate(hs)]
        for i, h in enumerate(hs):
            o_ref[:, h * hd:(h + 1) * hd] = pv[i][:, :hd] / pv[i][:, hd:]


def _gqa_lat_call(proj, row0, n_seq, t, cache_k, cache_v, cosd, sind, qn, kn):
    n_heads, n_kv, hd = 8, 2, 128
    tq = 256
    t_ctx = cache_k.shape[1]
    nq = t // tq
    kvw = 2 * n_kv * hd
    qblk0 = row0 // tq
    sblk0 = row0 // t
    body = functools.partial(_gqa_lat_kernel, n_heads=n_heads, n_kv=n_kv, hd=hd, t_ctx=t_ctx)
    return pl.pallas_call(
        body,
        grid=(n_seq, nq),
        in_specs=[pl.BlockSpec((tq, n_heads * hd), lambda b, i: (qblk0 + b * nq + i, 0)),
                  pl.BlockSpec((t, kvw), lambda b, i: (sblk0 + b, (n_heads * hd) // kvw)),
                  pl.BlockSpec((1, t_ctx, n_kv * hd), lambda b, i: (b, 0, 0)),
                  pl.BlockSpec((1, t_ctx, n_kv * hd), lambda b, i: (b, 0, 0)),
                  pl.BlockSpec((tq, hd), lambda b, i: (i, 0)),
                  pl.BlockSpec((tq, hd), lambda b, i: (i, 0)),
                  pl.BlockSpec((t, hd), lambda b, i: (0, 0)),
                  pl.BlockSpec((t, hd), lambda b, i: (0, 0)),
                  pl.BlockSpec((1, hd), lambda b, i: (0, 0)),
                  pl.BlockSpec((1, hd), lambda b, i: (0, 0))],
        out_specs=pl.BlockSpec((tq, n_heads * hd), lambda b, i: (b * nq + i, 0)),
        out_shape=jax.ShapeDtypeStruct((n_seq * t, n_heads * hd), F32),
        scratch_shapes=[pltpu.VMEM((t_ctx + t, n_kv * hd), BF16),
                        pltpu.VMEM((t_ctx + t, 2 * n_kv * hd), BF16)],
        compiler_params=_params(("arbitrary", "arbitrary")),
        name="gqa_latent",
    )(proj, proj, cache_k, cache_v, cosd, sind, cosd, sind, qn.reshape(1, hd), kn.reshape(1, hd))


def _mha_ctx_kernel(q_ref, k_ref, v_ref, o_ref, *, hd):
    scale = hd ** -0.5
    n_heads = q_ref.shape[1] // hd
    ones = jnp.ones((v_ref.shape[0], hd), BF16)
    for h0 in range(0, n_heads, ATTN_HEAD_GROUP):
        sls = [slice(h * hd, (h + 1) * hd) for h in range(h0, min(h0 + ATTN_HEAD_GROUP, n_heads))]
        s = [_bdot_nt(q_ref[:, sl] * scale, k_ref[:, sl]) for sl in sls]
        e = [jnp.exp(x - x.max(axis=-1, keepdims=True)).astype(BF16) for x in s]
        pv = [jnp.dot(e[i], jnp.concatenate([v_ref[:, sl].astype(BF16), ones], axis=-1),
                      preferred_element_type=F32) for i, sl in enumerate(sls)]
        for i, sl in enumerate(sls):
            o_ref[:, sl] = pv[i][:, :hd] / pv[i][:, hd:]


def _mha_ctx_call(proj, n_seq, t):
    hd = 64
    d = proj.shape[1] // 3
    body = functools.partial(_mha_ctx_kernel, hd=hd)
    return pl.pallas_call(
        body,
        grid=(n_seq,),
        in_specs=[pl.BlockSpec((t, d), lambda b: (b, 0)),
                  pl.BlockSpec((t, d), lambda b: (b, 1)),
                  pl.BlockSpec((t, d), lambda b: (b, 2))],
        out_specs=pl.BlockSpec((t, d), lambda b: (b, 0)),
        out_shape=jax.ShapeDtypeStruct((n_seq * t, d), F32),
        compiler_params=_params(("arbitrary",)),
        name="mha_ctx",
    )(proj, proj, proj)


def _na_bias_kernel(rpb_ref, o_ref, *, n_rel_rows, n_rel_cols):
    h = pl.program_id(0)
    w_io = lax.broadcasted_iota(I32, (GRID_W, 2 * GRID_W), 0)
    lane = lax.broadcasted_iota(I32, (GRID_W, 2 * GRID_W), 1)
    ck = jnp.where(lane < GRID_W, lane, lane - GRID_W)
    c_start = jnp.clip(w_io - NA_COLS // 2, 0, GRID_W - NA_COLS)
    in_win = (ck >= c_start) & (ck < c_start + NA_COLS)
    rel = ck - w_io + (NA_COLS - 1)
    base = h * (n_rel_rows * n_rel_cols)
    tiles = []
    for j in range(n_rel_rows):
        acc = jnp.zeros((GRID_W, 2 * GRID_W), F32)
        for jj in range(n_rel_cols):
            acc = jnp.where(rel == jj, rpb_ref[base + j * n_rel_cols + jj], acc)
        tiles.append(jnp.where(in_win, acc, NEG_BIG))
    for j in range(n_rel_rows):
        hi = tiles[j + 1] if j + 1 < n_rel_rows else jnp.full((GRID_W, 2 * GRID_W), NEG_BIG, F32)
        o_ref[0, j] = jnp.where(lane < GRID_W, tiles[j], hi)


def _na_bias_call(rpb):
    n_heads, nrr, nrc = rpb.shape
    body = functools.partial(_na_bias_kernel, n_rel_rows=nrr, n_rel_cols=nrc)
    return pl.pallas_call(
        body,
        grid=(n_heads,),
        in_specs=[pl.BlockSpec(memory_space=pltpu.SMEM)],
        out_specs=pl.BlockSpec((1, nrr, GRID_W, 2 * GRID_W), lambda h: (h, 0, 0, 0)),
        out_shape=jax.ShapeDtypeStruct((n_heads, nrr, GRID_W, 2 * GRID_W), F32),
        compiler_params=_params(("arbitrary",)),
        name="na_bias",
    )(rpb.reshape(-1))


NA_QROWS = 4
NA_KROWS = 12


def _na_kernel(q_ref, k0_ref, k1_ref, k2_ref, v0_ref, v1_ref, v2_ref, kc_ref, vc_ref, tz_ref, o_ref,
               *, hd, n_grid_rows):
    scale = hd ** -0.5
    blk = pl.program_id(1)
    kstart = jnp.clip(blk * NA_QROWS - NA_ROWS // 2, 0, n_grid_rows - NA_KROWS)
    lane = lax.broadcasted_iota(I32, (GRID_W, 2 * GRID_W), 1)
    n_rel = tz_ref.shape[1]
    heads = range(q_ref.shape[1] // hd)
    sls = [slice(hh * hd, (hh + 1) * hd) for hh in heads]
    rel, pen = [], []
    for rq_l in range(NA_QROWS):
        rq = blk * NA_QROWS + rq_l
        r_start = jnp.clip(rq - NA_ROWS // 2, 0, n_grid_rows - NA_ROWS)
        rel.append([])
        pen.append([])
        for m in range(NA_KROWS // 2):
            rk = kstart + 2 * m
            rel[-1].append(jnp.clip(rk - rq + (NA_ROWS - 1), 0, n_rel - 1))
            ok0 = (rk >= r_start) & (rk < r_start + NA_ROWS)
            ok1 = (rk + 1 >= r_start) & (rk + 1 < r_start + NA_ROWS)
            pen[-1].append(jnp.where(lane < GRID_W, jnp.where(ok0, 0.0, NEG_BIG), jnp.where(ok1, 0.0, NEG_BIG)))
    bias = [jnp.concatenate([jnp.concatenate([tz_ref[hh, pl.ds(rel[r][m], 1)][0] + pen[r][m]
                                              for m in range(NA_KROWS // 2)], axis=-1)
                             for r in range(NA_QROWS)], axis=0) for hh in heads]
    q = [q_ref[:, sl] * scale for sl in sls]
    k_loc = [jnp.concatenate([k0_ref[:, sl], k1_ref[:, sl], k2_ref[:, sl]], axis=0) for sl in sls]
    v_loc = [jnp.concatenate([v0_ref[:, sl], v1_ref[:, sl], v2_ref[:, sl]], axis=0).astype(BF16) for sl in sls]
    v_ctx = [vc_ref[0, :, sl].astype(BF16) for sl in sls]
    s_loc = [_bdot_nt(q[h], k_loc[h]) + bias[h] for h in heads]
    s_ctx = [_bdot_nt(q[h], kc_ref[0, :, sls[h]]) for h in heads]
    m = [jnp.maximum(s_loc[h].max(axis=-1, keepdims=True), s_ctx[h].max(axis=-1, keepdims=True)) for h in heads]
    e_loc = [jnp.exp(s_loc[h] - m[h]).astype(BF16) for h in heads]
    e_ctx = [jnp.exp(s_ctx[h] - m[h]).astype(BF16) for h in heads]
    pv = [jnp.dot(e_loc[h], jnp.concatenate([v_loc[h], jnp.ones_like(v_loc[h])], axis=-1),
                  preferred_element_type=F32)
          + jnp.dot(e_ctx[h], jnp.concatenate([v_ctx[h], jnp.ones_like(v_ctx[h])], axis=-1),
                    preferred_element_type=F32) for h in heads]
    for h in heads:
        o_ref[:, sls[h]] = pv[h][:, :hd] / pv[h][:, hd:]


def _na_call(proj, row0, n_seq, t, cache_k, cache_v, tz):
    hd, cw = 64, 256
    d = proj.shape[1] // 3
    ncb = d // cw
    tq = NA_QROWS * GRID_W
    nq = t // tq
    n_grid_rows = t // GRID_W
    t_ctx = cache_k.shape[1]
    qblk0 = row0 // tq

    def kv_map(which, j):
        def index_map(b, i, c):
            ks = jnp.clip(i * NA_QROWS - NA_ROWS // 2, 0, n_grid_rows - NA_KROWS) // NA_QROWS
            return (qblk0 + b * nq + ks + j, which * ncb + c)
        return index_map

    body = functools.partial(_na_kernel, hd=hd, n_grid_rows=n_grid_rows)
    return pl.pallas_call(
        body,
        grid=(n_seq, nq, ncb),
        in_specs=[pl.BlockSpec((tq, cw), lambda b, i, c: (qblk0 + b * nq + i, c))]
                 + [pl.BlockSpec((tq, cw), kv_map(1, j)) for j in range(3)]
                 + [pl.BlockSpec((tq, cw), kv_map(2, j)) for j in range(3)]
                 + [pl.BlockSpec((1, t_ctx, cw), lambda b, i, c: (b, 0, c)),
                    pl.BlockSpec((1, t_ctx, cw), lambda b, i, c: (b, 0, c)),
                    pl.BlockSpec((cw // hd,) + tz.shape[1:], lambda b, i, c: (c, 0, 0, 0))],
        out_specs=pl.BlockSpec((tq, cw), lambda b, i, c: (b * nq + i, c)),
        out_shape=jax.ShapeDtypeStruct((n_seq * t, d), F32),
        compiler_params=_params(("arbitrary", "arbitrary", "arbitrary")),
        name="nbr_attn",
    )(proj, proj, proj, proj, proj, proj, proj, cache_k, cache_v, tz)


def _hgrn_kernel(q_ref, v_ref, f_ref, bf_ref, lb_ref, s0_ref, o_ref, s_ref, st_scr,
                 *, reverse, n_heads, dk, has_s0, n_blk):
    c = pl.program_id(1)
    L = HGRN_CHUNK
    tb = q_ref.shape[0]

    @pl.when(c == 0)
    def _():
        for h in range(n_heads):
            if has_s0:
                st_scr[h] = s0_ref[0, 0, h].T
            else:
                st_scr[h] = jnp.zeros_like(st_scr[h])

    row = lax.broadcasted_iota(I32, (L, L), 0)
    col = lax.broadcasted_iota(I32, (L, L), 1)
    tri = jnp.where((col >= row) if reverse else (col <= row), 1.0, 0.0).astype(F32)
    eye = jnp.where(row == col, 1.0, 0.0).astype(F32)
    halves = [L >> (i + 1) for i in range(L.bit_length() - 1)]
    same_pair = [jnp.where((row // (2 * hf)) == (col // (2 * hf)), 1.0, 0.0).astype(F32) for hf in halves]
    rio = lax.broadcasted_iota(I32, (L, q_ref.shape[1]), 0)
    is_query = [jnp.where(((rio & hf) == 0) if reverse else ((rio & hf) != 0), 1.0, 0.0).astype(F32)
                for hf in halves]
    r8 = lax.broadcasted_iota(I32, (ROW_SUBLANES, 1), 0)
    lb = lb_ref[0]
    bf = bf_ref[0]
    n_chunks = tb // L

    def boundary_rows(cum, hf):
        blk = 2 * hf
        off = hf if reverse else hf - 1
        width = cum.shape[1]
        if blk >= ROW_SUBLANES:
            return jnp.concatenate([jnp.broadcast_to(cum[a + off:a + off + 1, :], (blk, width))
                                    for a in range(0, L, blk)], axis=0)
        groups = []
        for g in range(0, L, ROW_SUBLANES):
            ref = jnp.broadcast_to(cum[g + off:g + off + 1, :], (ROW_SUBLANES, width))
            for a in range(blk, ROW_SUBLANES, blk):
                ref = jnp.where(r8 >= a, jnp.broadcast_to(cum[g + a + off:g + a + off + 1, :],
                                                          (ROW_SUBLANES, width)), ref)
            groups.append(ref)
        return jnp.concatenate(groups, axis=0)

    def chunk_step(jj, carry):
        jc = (n_chunks - 1 - jj) if reverse else jj
        rs = pl.ds(pl.multiple_of(jc * L, L), L)
        q = _silu(q_ref[rs, :]) * (dk ** -0.5)
        v = v_ref[rs, :]
        f = lb + (1.0 - lb) * _sigmoid(f_ref[rs, :] + bf)
        logf = jnp.log(f)
        kk = 1.0 - f
        cum = _fdot(tri, logf)
        end = cum[0:1, :] if reverse else cum[L - 1:L, :]
        qd = q * jnp.exp(cum)
        kd = kk * jnp.exp(end - cum)
        e_end = jnp.exp(end)
        q_lv, k_lv = [], []
        for lv, hf in enumerate(halves):
            isq = is_query[lv]
            decay = jnp.exp((2.0 * isq - 1.0) * (cum - boundary_rows(cum, hf)))
            dq = decay * isq
            q_lv.append((q * dq).astype(BF16))
            k_lv.append((kk * (decay - dq)).astype(BF16))
        qk_diag = q * kk
        hsl = [slice(h * dk, (h + 1) * dk) for h in range(n_heads)]
        inter = [_bdot_nt(qd[:, hs], st_scr[h]) for h, hs in enumerate(hsl)]
        attn = [eye * jnp.sum(qk_diag[:, hs], axis=-1, keepdims=True) for hs in hsl]
        for lv in range(len(halves)):
            prods = [_bdot_nt(q_lv[lv][:, hs], k_lv[lv][:, hs]) for hs in hsl]
            attn = [attn[h] + same_pair[lv] * prods[h] for h in range(n_heads)]
        outs = [inter[h] + _bdot(attn[h], v[:, hs]) for h, hs in enumerate(hsl)]
        o_ref[rs, :] = jnp.concatenate(outs, axis=-1)
        for h in range(n_heads):
            hs = slice(h * dk, (h + 1) * dk)
            st_scr[h] = st_scr[h] * e_end[:, hs] + _bdot_tn(v[:, hs], kd[:, hs])
        return carry

    lax.fori_loop(0, n_chunks, chunk_step, 0)

    @pl.when(c == n_blk - 1)
    def _():
        for h in range(n_heads):
            s_ref[0, 0, h] = st_scr[h].T


def _hgrn_call(proj, row0, n_seq, t, b_f, lb, s0, reverse):
    n_heads, dk = 8, 128
    d = n_heads * dk
    tb = 256
    n_blk = t // tb
    blk0 = row0 // tb
    di = 1 if reverse else 0
    has_s0 = s0 is not None
    if not has_s0:
        s0 = jnp.zeros((1, 2, n_heads, dk, dk), F32)

    def tok(b, c):
        return blk0 + b * n_blk + ((n_blk - 1 - c) if reverse else c)

    body = functools.partial(_hgrn_kernel, reverse=reverse, n_heads=n_heads, dk=dk,
                             has_s0=has_s0, n_blk=n_blk)
    return pl.pallas_call(
        body,
        grid=(n_seq, n_blk),
        in_specs=[pl.BlockSpec((tb, d), lambda b, c: (tok(b, c), 0)),
                  pl.BlockSpec((tb, d), lambda b, c: (tok(b, c), 1)),
                  pl.BlockSpec((tb, d), lambda b, c: (tok(b, c), 2 + di)),
                  pl.BlockSpec((1, 1, d), lambda b, c: (di, 0, 0)),
                  pl.BlockSpec((1, 1, d), lambda b, c: (di, 0, 0)),
                  pl.BlockSpec((1, 1, n_heads, dk, dk),
                               (lambda b, c: (b, di, 0, 0, 0)) if has_s0 else (lambda b, c: (0, 0, 0, 0, 0)))],
        out_specs=[pl.BlockSpec((tb, d), lambda b, c: (tok(b, c) - blk0, 0)),
                   pl.BlockSpec((1, 1, n_heads, dk, dk), lambda b, c: (b, 0, 0, 0, 0))],
        out_shape=[jax.ShapeDtypeStruct((n_seq * t, d), F32),
                   jax.ShapeDtypeStruct((n_seq, 1, n_heads, dk, dk), F32)],
        scratch_shapes=[pltpu.VMEM((n_heads, dk, dk), F32)],
        compiler_params=_params(("arbitrary", "arbitrary")),
        name="hgrn2_bw" if reverse else "hgrn2_fw",
    )(proj, proj, proj, b_f.reshape(2, 1, d), lb.reshape(2, 1, d), s0)


def _log_sigmoid(x):
    return jnp.minimum(x, 0.0) - jnp.log(1.0 + jnp.exp(-jnp.abs(x)))


def _mlstm_kernel(qf_ref, kf_ref, vf_ref, gf_ref, qb_ref, kb_ref, vb_ref, gb_ref, bg_ref, c0_ref, n0_ref, m0_ref,
                  of_ref, ob_ref, c_out, n_out, m_out, c_scr, n_scr, m_scr,
                  *, n_heads, dqk, dv, has_state, n_blk):
    c = pl.program_id(1)
    L = qf_ref.shape[0]

    @pl.when(c == 0)
    def _():
        if has_state:
            c_scr[...] = c0_ref[0]
            n_scr[...] = n0_ref[0]
            m_scr[...] = m0_ref[0]
        else:
            c_scr[...] = jnp.zeros_like(c_scr)
            n_scr[...] = jnp.zeros_like(n_scr)
            m_scr[...] = jnp.zeros_like(m_scr)

    row = lax.broadcasted_iota(I32, (L, L), 0)
    col = lax.broadcasted_iota(I32, (L, L), 1)
    lane = lax.broadcasted_iota(I32, (1, LANES), 1)
    heads = [(s, h) for s in range(2) for h in range(n_heads)]
    refs = [(qf_ref, kf_ref, vf_ref, gf_ref), (qb_ref, kb_ref, vb_ref, gb_ref)]
    causal_s, cum_s, cumt_s, gates_s, gatest_s, m_all = [], [], [], [], [], []
    for s in range(2):
        gates = GATE_SOFTCAP * jnp.tanh((refs[s][3][...] + bg_ref[...]) / GATE_SOFTCAP)
        causal = (col >= row) if s == 1 else (col <= row)
        cum = _fdot(jnp.where(causal, 1.0, 0.0).astype(F32), _log_sigmoid(gates))
        causal_s.append(causal)
        cum_s.append(cum)
        cumt_s.append(cum.T)
        gates_s.append(gates)
        gatest_s.append(gates.T)
        m_all.append(m_scr[s])
    i_off = [0, 2 * n_heads]
    f_off = [n_heads, 3 * n_heads]
    e_row = [L - 1, 0]
    c_prev = {it: c_scr[it[0], it[1]] for it in heads}
    n_prev = {it: n_scr[it[0], it[1]] for it in heads}
    qs = {(s, h): refs[s][0][:, h * dqk:(h + 1) * dqk] * (dqk ** -0.5) for s, h in heads}
    ks_ = {(s, h): refs[s][1][:, h * dqk:(h + 1) * dqk] for s, h in heads}
    vs = {(s, h): refs[s][2][:, h * dv:(h + 1) * dv] for s, h in heads}
    cum_c = {(s, h): cum_s[s][:, f_off[s] + h:f_off[s] + h + 1] for s, h in heads}
    cum_r = {(s, h): cumt_s[s][f_off[s] + h:f_off[s] + h + 1, :] for s, h in heads}
    i_c = {(s, h): gates_s[s][:, i_off[s] + h:i_off[s] + h + 1] for s, h in heads}
    i_r = {(s, h): gatest_s[s][i_off[s] + h:i_off[s] + h + 1, :] for s, h in heads}
    m_prev = {(s, h): m_all[s][0:1, h:h + 1] for s, h in heads}
    d = {h: jnp.where(causal_s[h[0]], cum_c[h] - cum_r[h] + i_r[h], -jnp.inf) for h in heads}
    m_inter = {h: cum_c[h] + m_prev[h] for h in heads}
    m_t = {h: jnp.maximum(m_inter[h], d[h].max(axis=-1, keepdims=True)) for h in heads}
    scores = {h: _bdot_nt(qs[h], ks_[h]) for h in heads}
    inter = {h: _bdot(qs[h], c_prev[h]) for h in heads}
    qn = {h: jnp.sum(qs[h] * n_prev[h], axis=-1, keepdims=True) for h in heads}
    w_inter = {h: jnp.exp(m_inter[h] - m_t[h]) for h in heads}
    qk = {h: scores[h] * jnp.exp(d[h] - m_t[h]) for h in heads}
    num = {h: w_inter[h] * inter[h] + _bdot(qk[h], vs[h]) for h in heads}
    den = {h: w_inter[h] * qn[h] + qk[h].sum(axis=-1, keepdims=True) for h in heads}
    outs = {h: num[h] / jnp.maximum(jnp.abs(den[h]), jnp.exp(-m_t[h])) for h in heads}
    end = {h: cum_c[h][e_row[h[0]]:e_row[h[0]] + 1, :] for h in heads}
    g_end_r = {h: end[h] - cum_r[h] + i_r[h] for h in heads}
    g_end_c = {h: end[h] - cum_c[h] + i_c[h] for h in heads}
    m_new = {h: jnp.maximum(end[h] + m_prev[h], g_end_r[h].max(axis=-1, keepdims=True)) for h in heads}
    w_old = {h: jnp.exp(end[h] + m_prev[h] - m_new[h]) for h in heads}
    kd = {h: ks_[h] * jnp.exp(g_end_c[h] - m_new[h]) for h in heads}
    c_new = {h: w_old[h] * c_prev[h] + _bdot_tn(kd[h], vs[h]) for h in heads}
    n_new = {h: w_old[h] * n_prev[h] + kd[h].sum(axis=0, keepdims=True) for h in heads}
    of_ref[...] = jnp.concatenate([outs[(0, h)] for h in range(n_heads)], axis=-1)
    ob_ref[...] = jnp.concatenate([outs[(1, h)] for h in range(n_heads)], axis=-1)
    for s in range(2):
        m_next = m_all[s]
        for h in range(n_heads):
            c_scr[s, h] = c_new[(s, h)]
            n_scr[s, h] = n_new[(s, h)]
            m_next = jnp.where(lane == h, m_new[(s, h)], m_next)
        m_scr[s] = m_next

    @pl.when(c == n_blk - 1)
    def _():
        c_out[0] = c_scr[...]
        n_out[0] = n_scr[...]
        m_out[0] = m_scr[...]


def _mlstm_call(proj, row0, n_seq, t, b_gates_pad, state):
    n_heads, dqk, dv = 8, 64, 128
    wq, wv = n_heads * dqk, n_heads * dv
    L = 256
    n_blk = t // L
    blk0 = row0 // L
    has_state = state is not None
    if has_state:
        c0, n0, m0 = state
        smap = lambda b, c: (b, 0, 0, 0, 0)
        mmap = lambda b, c: (b, 0, 0, 0)
    else:
        c0 = jnp.zeros((1, 2, n_heads, dqk, dv), F32)
        n0 = jnp.zeros((1, 2, n_heads, 1, dqk), F32)
        m0 = jnp.zeros((1, 2, 1, 128), F32)
        smap = lambda b, c: (0, 0, 0, 0, 0)
        mmap = lambda b, c: (0, 0, 0, 0)

    def tok(b, c, reverse):
        return blk0 + b * n_blk + ((n_blk - 1 - c) if reverse else c)

    body = functools.partial(_mlstm_kernel, n_heads=n_heads, dqk=dqk, dv=dv, has_state=has_state, n_blk=n_blk)
    gate_cb = (2 * wq + 2 * wv) // 128

    def token_specs(reverse):
        return [pl.BlockSpec((L, wq), lambda b, c: (tok(b, c, reverse), 0)),
                pl.BlockSpec((L, wq), lambda b, c: (tok(b, c, reverse), 1)),
                pl.BlockSpec((L, wv), lambda b, c: (tok(b, c, reverse), (2 * wq) // wv)),
                pl.BlockSpec((L, 128), lambda b, c: (tok(b, c, reverse), gate_cb))]

    return pl.pallas_call(
        body,
        grid=(n_seq, n_blk),
        in_specs=token_specs(False) + token_specs(True)
                 + [pl.BlockSpec((1, 128), lambda b, c: (0, 0)),
                    pl.BlockSpec((1, 2, n_heads, dqk, dv), smap),
                    pl.BlockSpec((1, 2, n_heads, 1, dqk), smap),
                    pl.BlockSpec((1, 2, 1, 128), mmap)],
        out_specs=[pl.BlockSpec((L, wv), lambda b, c: (tok(b, c, False) - blk0, 0)),
                   pl.BlockSpec((L, wv), lambda b, c: (tok(b, c, True) - blk0, 0)),
                   pl.BlockSpec((1, 2, n_heads, dqk, dv), lambda b, c: (b, 0, 0, 0, 0)),
                   pl.BlockSpec((1, 2, n_heads, 1, dqk), lambda b, c: (b, 0, 0, 0, 0)),
                   pl.BlockSpec((1, 2, 1, 128), lambda b, c: (b, 0, 0, 0))],
        out_shape=[jax.ShapeDtypeStruct((n_seq * t, wv), F32),
                   jax.ShapeDtypeStruct((n_seq * t, wv), F32),
                   jax.ShapeDtypeStruct((n_seq, 2, n_heads, dqk, dv), F32),
                   jax.ShapeDtypeStruct((n_seq, 2, n_heads, 1, dqk), F32),
                   jax.ShapeDtypeStruct((n_seq, 2, 1, 128), F32)],
        scratch_shapes=[pltpu.VMEM((2, n_heads, dqk, dv), F32),
                        pltpu.VMEM((2, n_heads, 1, dqk), F32),
                        pltpu.VMEM((2, 1, 128), F32)],
        compiler_params=_params(("arbitrary", "arbitrary")),
        name="mlstm",
    )(proj, proj, proj, proj, proj, proj, proj, proj, b_gates_pad, c0, n0, m0)


def _moe_input(x_ref, m_ref, g_ref):
    return _rms(x_ref[...], g_ref[...]) * (1.0 + m_ref[0, 4:5, :]) + m_ref[0, 3:4, :]


def _route_kernel(x_ref, m_ref, g_ref, wr_ref, br_ref, idx_ref, gate_ref, rank_ref, cnt_ref, carry_scr,
                  *, n_experts):
    i = pl.program_id(0)
    tm = x_ref.shape[0]

    @pl.when(i == 0)
    def _():
        carry_scr[...] = jnp.zeros_like(carry_scr)

    h = _moe_input(x_ref, m_ref, g_ref)
    w = wr_ref[...]
    h_hi, w_hi = h.astype(BF16), w.astype(BF16)
    h_lo = (h - h_hi.astype(F32)).astype(BF16)
    w_lo = (w - w_hi.astype(F32)).astype(BF16)
    logits = _bdot_nt(w_hi, h_hi) + _bdot_nt(w_lo, h_hi) + _bdot_nt(w_hi, h_lo) + br_ref[...]
    e_io = lax.broadcasted_iota(I32, (n_experts, tm), 0).astype(F32)
    work = logits
    vals, idxs = [], []
    chosen = jnp.zeros((n_experts, tm), F32)
    for _ in range(TOP_K):
        mx = work.max(axis=0, keepdims=True)
        ix = jnp.min(jnp.where(work == mx, e_io, float(n_experts)), axis=0, keepdims=True)
        hit = e_io == ix
        vals.append(mx)
        idxs.append(ix)
        chosen = jnp.where(hit, 1.0, chosen)
        work = jnp.where(hit, -jnp.inf, work)
    es = [jnp.exp(v - vals[0]) for v in vals]
    den = es[0] + es[1] + es[2] + es[3]
    srow = lax.broadcasted_iota(I32, (tm, tm), 0)
    scol = lax.broadcasted_iota(I32, (tm, tm), 1)
    before = jnp.where(srow < scol, 1.0, 0.0).astype(BF16)
    pos = jnp.dot(chosen.astype(BF16), before, preferred_element_type=F32) + carry_scr[...]
    ranks = [jnp.sum(jnp.where(e_io == ix, pos, 0.0), axis=0, keepdims=True) for ix in idxs]
    carry_scr[...] = carry_scr[...] + chosen.sum(axis=1, keepdims=True)
    idx_ref[...] = jnp.concatenate(idxs, axis=0).astype(I32)
    gate_ref[...] = jnp.concatenate([e / den for e in es], axis=0)
    rank_ref[...] = jnp.concatenate(ranks, axis=0).astype(I32)
    cnt_ref[...] = jnp.broadcast_to(carry_scr[...], cnt_ref.shape).astype(I32)


def _route_call(x, modsel, g, w_router, b_router):
    n, d = x.shape
    n_experts = w_router.shape[1]
    tm = TOKEN_TILE
    body = functools.partial(_route_kernel, n_experts=n_experts)
    return pl.pallas_call(
        body,
        grid=(n // tm,),
        in_specs=[pl.BlockSpec((tm, d), lambda i: (i, 0)),
                  pl.BlockSpec((1, 6, d), lambda i: (i, 0, 0)),
                  pl.BlockSpec((1, d), lambda i: (0, 0)),
                  pl.BlockSpec((n_experts, d), lambda i: (0, 0)),
                  pl.BlockSpec((n_experts, 1), lambda i: (0, 0))],
        out_specs=[pl.BlockSpec((TOP_K, tm), lambda i: (0, i)),
                   pl.BlockSpec((TOP_K, tm), lambda i: (0, i)),
                   pl.BlockSpec((TOP_K, tm), lambda i: (0, i)),
                   pl.BlockSpec((n_experts, 128), lambda i: (0, 0))],
        out_shape=[jax.ShapeDtypeStruct((TOP_K, n), I32),
                   jax.ShapeDtypeStruct((TOP_K, n), F32),
                   jax.ShapeDtypeStruct((TOP_K, n), I32),
                   jax.ShapeDtypeStruct((n_experts, 128), I32)],
        scratch_shapes=[pltpu.VMEM((n_experts, 1), F32)],
        compiler_params=_params(("arbitrary",)),
        name="moe_route",
    )(x, modsel, g.reshape(1, d), w_router.T, b_router.reshape(n_experts, 1))


def _slot_kernel(cnt_ref, idx_ref, rank_ref, dest_ref, binfo_ref, pad_ref, nused_ref, *, n_experts, n_blocks):
    cnt = cnt_ref[:, 0:1].astype(F32)
    padded = jnp.ceil(cnt * (1.0 / MOE_BLOCK)) * MOE_BLOCK
    er = lax.broadcasted_iota(I32, (n_experts, n_experts), 0)
    ec = lax.broadcasted_iota(I32, (n_experts, n_experts), 1)
    start_row = jnp.sum(jnp.where(er < ec, padded, 0.0), axis=0, keepdims=True)
    start_col = jnp.sum(jnp.where(er == ec, start_row, 0.0), axis=1, keepdims=True)
    end_col = start_col + padded
    idx = idx_ref[...]
    e_io = lax.broadcasted_iota(I32, (n_experts,) + idx.shape[1:], 0)
    rows = []
    for k in range(TOP_K):
        hit = e_io == idx[k:k + 1, :]
        rows.append(jnp.sum(jnp.where(hit, start_col, 0.0), axis=0, keepdims=True))
    dest_ref[...] = jnp.concatenate(rows, axis=0).astype(I32) + rank_ref[...]
    blk_start = (lax.broadcasted_iota(I32, (n_experts, n_blocks), 1) * MOE_BLOCK).astype(F32)
    n_done = jnp.sum(jnp.where(end_col <= blk_start, 1.0, 0.0), axis=0, keepdims=True)
    bexp = jnp.minimum(n_done, n_experts - 1.0)
    used_row = jnp.sum(jnp.where(er == ec, jnp.where(cnt > 0.0, 1.0, 0.0), 0.0), axis=0, keepdims=True)
    ecf = ec.astype(F32)
    next_col = jnp.min(jnp.where((ec > er) & (used_row > 0.0), ecf, float(n_experts)), axis=1, keepdims=True)
    ord_col = jnp.sum(jnp.where(ec < er, used_row, 0.0), axis=1, keepdims=True)
    par_col = ord_col - 2.0 * jnp.floor(ord_col * 0.5)
    mine = lax.broadcasted_iota(I32, (n_experts, n_blocks), 0).astype(F32) == bexp
    bnext = jnp.sum(jnp.where(mine, next_col, 0.0), axis=0, keepdims=True)
    bslot = jnp.sum(jnp.where(mine, par_col, 0.0), axis=0, keepdims=True)
    binfo_ref[...] = jnp.concatenate([bexp, bnext, bslot], axis=0).astype(I32)
    n_used = jnp.sum(padded, axis=0, keepdims=True) * (1.0 / MOE_BLOCK)
    nused_ref[...] = n_used.astype(I32)
    cnt_row = jnp.sum(jnp.where(er == ec, cnt, 0.0), axis=0, keepdims=True)
    padded_row = jnp.sum(jnp.where(er == ec, padded, 0.0), axis=0, keepdims=True)
    pad_ref[...] = jnp.concatenate([start_row + cnt_row, padded_row - cnt_row,
                                    jnp.broadcast_to(n_used, cnt_row.shape)], axis=0).astype(I32)


def _slot_call(counts, idx_t, rank_t, n_blocks):
    n_experts = counts.shape[0]
    n = idx_t.shape[1]
    tn = min(2048, n)
    body = functools.partial(_slot_kernel, n_experts=n_experts, n_blocks=n_blocks)
    return pl.pallas_call(
        body,
        grid=(n // tn,),
        in_specs=[pl.BlockSpec((n_experts, 128), lambda i: (0, 0)),
                  pl.BlockSpec((TOP_K, tn), lambda i: (0, i)),
                  pl.BlockSpec((TOP_K, tn), lambda i: (0, i))],
        out_specs=[pl.BlockSpec((TOP_K, tn), lambda i: (0, i)),
                   pl.BlockSpec((3, n_blocks), lambda i: (0, 0)),
                   pl.BlockSpec((3, n_experts), lambda i: (0, 0)),
                   pl.BlockSpec((1, 1), lambda i: (0, 0))],
        out_shape=[jax.ShapeDtypeStruct((TOP_K, n), I32),
                   jax.ShapeDtypeStruct((3, n_blocks), I32),
                   jax.ShapeDtypeStruct((3, n_experts), I32),
                   jax.ShapeDtypeStruct((1, 1), I32)],
        compiler_params=_params(("arbitrary",)),
        name="moe_slots",
    )(counts, idx_t, rank_t)


DMA_ISSUE_UNROLL = 8


def _to_row_tiles(ref, base, x):
    rows = x.shape[0]
    for c in range(ROW_SUBLANES):
        ref[pl.ds(base * ROW_SUBLANES + c, rows, stride=ROW_SUBLANES), :] = x[:, c * LANES:(c + 1) * LANES]


def _from_row_tiles(ref, base, rows, c):
    return ref[pl.ds(base * ROW_SUBLANES + c, rows, stride=ROW_SUBLANES), :]


def _row_tile(ref, r):
    return ref.at[pl.ds(pl.multiple_of(r * ROW_SUBLANES, ROW_SUBLANES), ROW_SUBLANES)]


def _zero_fill_padding(pad_ref, xs_ref, z_scr, sem):
    z_scr[...] = jnp.zeros_like(z_scr)
    n_experts = pad_ref.shape[1]
    bits = range(MOE_BLOCK.bit_length() - 2, -1, -1)

    def pieces(e):
        off, length = pad_ref[0, e], pad_ref[1, e]
        for bit in bits:
            size = 1 << bit
            done = (length >> (bit + 1)) << (bit + 1)
            copy = pltpu.make_async_copy(z_scr.at[pl.ds(0, size * ROW_SUBLANES)],
                                         xs_ref.at[pl.ds(pl.multiple_of((off + done) * ROW_SUBLANES, ROW_SUBLANES),
                                                         size * ROW_SUBLANES)], sem)
            yield (length & size) != 0, copy

    def tail_blocks():
        n_blocks = xs_ref.shape[0] // (MOE_BLOCK * ROW_SUBLANES)
        for b in range(n_blocks - n_experts, n_blocks):
            copy = pltpu.make_async_copy(z_scr, xs_ref.at[pl.ds(b * MOE_BLOCK * ROW_SUBLANES,
                                                                MOE_BLOCK * ROW_SUBLANES)], sem)
            yield b >= pad_ref[2, 0], copy

    def all_copies():
        for e in range(n_experts):
            yield from pieces(e)
        yield from tail_blocks()

    for needed, copy in all_copies():
        pl.when(needed)(copy.start)
    for needed, copy in all_copies():
        pl.when(needed)(copy.wait)


def _dispatch_kernel(pad_ref, dest_ref, x_ref, m_ref, g_ref, xs_ref, h_scr, z_scr, sem, *, n_tiles):
    i = pl.program_id(0)
    tm = x_ref.shape[0]

    @pl.when(i == 0)
    def _():
        _zero_fill_padding(pad_ref, xs_ref, z_scr, sem.at[2])

    slot = lax.rem(i, 2)
    src = h_scr.at[slot]
    _to_row_tiles(src, 0, _moe_input(x_ref, m_ref, g_ref))

    def start_row(r, carry):
        for k in range(TOP_K):
            pltpu.make_async_copy(_row_tile(src, r), _row_tile(xs_ref, dest_ref[0, 0, k * tm + r]),
                                  sem.at[slot]).start(priority=k % 2)
        return carry

    lax.fori_loop(0, tm, start_row, 0, unroll=DMA_ISSUE_UNROLL // TOP_K)

    def wait_tile(s):
        for _ in range(TOP_K):
            pltpu.make_async_copy(h_scr.at[s], xs_ref.at[pl.ds(0, tm * ROW_SUBLANES)], sem.at[s]).wait()

    @pl.when(i >= 1)
    def _():
        wait_tile(1 - slot)

    @pl.when(i == n_tiles - 1)
    def _():
        wait_tile(slot)


def _dispatch_call(pad_info, dest_tiles, x, modsel, g, n_slots):
    n, d = x.shape
    tm = TOKEN_TILE
    assert d == ROW_SUBLANES * LANES
    return pl.pallas_call(
        functools.partial(_dispatch_kernel, n_tiles=n // tm),
        grid=(n // tm,),
        in_specs=[pl.BlockSpec(memory_space=pltpu.SMEM),
                  pl.BlockSpec((1, 1, TOP_K * tm), lambda i: (i, 0, 0), memory_space=pltpu.SMEM),
                  pl.BlockSpec((tm, d), lambda i: (i, 0)),
                  pl.BlockSpec((1, 6, d), lambda i: (i, 0, 0)),
                  pl.BlockSpec((1, d), lambda i: (0, 0))],
        out_specs=pl.BlockSpec(memory_space=pl.ANY),
        out_shape=jax.ShapeDtypeStruct((n_slots * ROW_SUBLANES, LANES), F32),
        scratch_shapes=[pltpu.VMEM((2, tm * ROW_SUBLANES, LANES), F32),
                        pltpu.VMEM((MOE_BLOCK * ROW_SUBLANES, LANES), F32),
                        pltpu.SemaphoreType.DMA((3,))],
        compiler_params=_params(("arbitrary",)),
        name="moe_dispatch",
    )(pad_info, dest_tiles, x, modsel, g.reshape(1, d))


FFN_BLOCKS_PER_STEP = 2


def _ffn_kernel(binfo_ref, nused_ref, xs_ref, wgu_hbm, wdn_hbm, *rest, layer, n_experts, n_blocks):
    bias_refs = rest[:2 * FFN_BLOCKS_PER_STEP]
    ys_ref, wgu_f32, wdn_f32, wgu_scr, wdn_scr, sem = rest[2 * FFN_BLOCKS_PER_STEP:]
    d_ff = wdn_scr.shape[0]
    rows = MOE_BLOCK

    def weight_copies(e, slot):
        return (pltpu.make_async_copy(wgu_hbm.at[layer, e], wgu_f32.at[slot], sem.at[slot]),
                pltpu.make_async_copy(wdn_hbm.at[layer, e], wdn_f32.at[slot], sem.at[slot]))

    def one_block(u):
        b = pl.program_id(0) * FFN_BLOCKS_PER_STEP + u
        base = u * MOE_BLOCK
        bgu_ref, bdn_ref = bias_refs[2 * u], bias_refs[2 * u + 1]

        @pl.when(b < nused_ref[0])
        def _():
            e = binfo_ref[b]
            prev = binfo_ref[jnp.maximum(b - 1, 0)]
            slot = binfo_ref[2 * n_blocks + b]

            @pl.when((b == 0) | (e != prev))
            def _():
                @pl.when(b == 0)
                def _():
                    for cp in weight_copies(e, slot):
                        cp.start()

                nxt = binfo_ref[n_blocks + b]

                @pl.when(nxt < n_experts)
                def _():
                    for cp in weight_copies(nxt, 1 - slot):
                        cp.start()

                for cp in weight_copies(e, slot):
                    cp.wait()
                wgu_scr[...] = wgu_f32[slot].astype(BF16)
                wdn_scr[...] = wdn_f32[slot].astype(BF16)

            x = jnp.concatenate([_from_row_tiles(xs_ref, base, rows, c).astype(BF16)
                                 for c in range(ROW_SUBLANES)], axis=-1)
            gu = jnp.dot(x, wgu_scr[...], preferred_element_type=F32) + bgu_ref[0, 0]
            x_glu = jnp.minimum(gu[:, :d_ff], SWIGLU_LIMIT)
            x_lin = jnp.clip(gu[:, d_ff:], -SWIGLU_LIMIT, SWIGLU_LIMIT)
            hid = x_glu * _sigmoid(SWIGLU_ALPHA * x_glu) * (x_lin + 1.0)
            _to_row_tiles(ys_ref, base, jnp.dot(hid.astype(BF16), wdn_scr[...], preferred_element_type=F32)
                          + bdn_ref[0, 0])

        @pl.when(b >= nused_ref[0])
        def _():
            ys_ref[pl.ds(base * ROW_SUBLANES, rows * ROW_SUBLANES), :] = jnp.zeros(
                (rows * ROW_SUBLANES, LANES), F32)

    for u in range(FFN_BLOCKS_PER_STEP):
        one_block(u)


def _ffn_call(layer, block_info, n_used, xs, w_gu, b_gu, w_dn, b_dn):
    depth, n_experts, d, d_ff2 = w_gu.shape
    d_ff = d_ff2 // 2
    g = FFN_BLOCKS_PER_STEP
    step_rows = g * MOE_BLOCK * ROW_SUBLANES
    n_blocks = xs.shape[0] // (MOE_BLOCK * ROW_SUBLANES)
    assert n_blocks % g == 0

    def blk(b, nu):
        return jnp.maximum(jnp.minimum(b, nu[0] - 1), 0)

    def bias_specs(u):
        return [pl.BlockSpec((1, 1, 1, d_ff2), lambda s, bi, nu: (layer, bi[blk(g * s + u, nu)], 0, 0)),
                pl.BlockSpec((1, 1, 1, d), lambda s, bi, nu: (layer, bi[blk(g * s + u, nu)], 0, 0))]

    bias_args = [b_gu.reshape(depth, n_experts, 1, d_ff2), b_dn.reshape(depth, n_experts, 1, d)] * g
    grid_spec = pltpu.PrefetchScalarGridSpec(
        num_scalar_prefetch=2,
        grid=(n_blocks // g,),
        in_specs=[pl.BlockSpec((step_rows, LANES), lambda s, bi, nu: (blk(g * s, nu) // g, 0)),
                  pl.BlockSpec(memory_space=pl.ANY),
                  pl.BlockSpec(memory_space=pl.ANY)]
                 + [spec for u in range(g) for spec in bias_specs(u)],
        out_specs=pl.BlockSpec((step_rows, LANES), lambda s, bi, nu: (s, 0)),
        scratch_shapes=[pltpu.VMEM((2, d, d_ff2), F32), pltpu.VMEM((2, d_ff, d), F32),
                        pltpu.VMEM((d, d_ff2), BF16), pltpu.VMEM((d_ff, d), BF16),
                        pltpu.SemaphoreType.DMA((2,))],
    )
    body = functools.partial(_ffn_kernel, layer=layer, n_experts=n_experts, n_blocks=n_blocks)
    return pl.pallas_call(
        body,
        grid_spec=grid_spec,
        out_shape=jax.ShapeDtypeStruct(xs.shape, F32),
        compiler_params=_params(("arbitrary",)),
        name="moe_ffn",
    )(block_info, n_used, xs, w_gu, w_dn, *bias_args)


def _gather_expert_rows(dest_ref, dest_next_ref, ys_ref, gate_ref, buf, sem, n_tiles):
    i = pl.program_id(0)
    tm = gate_ref.shape[0]
    rows = TOP_K * tm

    def start_gathers(d_ref, slot):
        def start_pair(p, carry):
            for u in range(2):
                j = 2 * p + u
                pltpu.make_async_copy(_row_tile(ys_ref, d_ref[0, 0, j]), _row_tile(buf.at[slot], j),
                                      sem.at[slot]).start(priority=u)
            return carry

        lax.fori_loop(0, rows // 2, start_pair, 0, unroll=DMA_ISSUE_UNROLL // 2)

    slot = lax.rem(i, 2)

    @pl.when(i == 0)
    def _():
        start_gathers(dest_ref, 0)

    def wait_tile(s):
        pltpu.make_async_copy(ys_ref.at[pl.ds(0, rows * ROW_SUBLANES)], buf.at[s], sem.at[s]).wait()

    @pl.when(i + 1 < n_tiles)
    def _():
        start_gathers(dest_next_ref, 1 - slot)

    wait_tile(slot)
    cur = buf.at[slot]
    chunks = []
    for c in range(ROW_SUBLANES):
        y = gate_ref[:, 0:1] * _from_row_tiles(cur, 0, tm, c)
        for k in range(1, TOP_K):
            y = y + gate_ref[:, k:k + 1] * _from_row_tiles(cur, k * tm, tm, c)
        chunks.append(y)
    return jnp.concatenate(chunks, axis=-1)


def _combine_kernel(dest_ref, dest_next_ref, ys_ref, gate_ref, x_ref, m_ref, fg_ref, y_ref, buf, sem,
                    *, n_tiles):
    y = _gather_expert_rows(dest_ref, dest_next_ref, ys_ref, gate_ref, buf, sem, n_tiles)
    y_ref[...] = _rms(x_ref[...] + m_ref[0, 5:6, :] * y, fg_ref[...])


def _combine_inproj_kernel(dest_ref, dest_next_ref, ys_ref, gate_ref, x_ref, mp_ref, mc_ref, g_ref, w_ref,
                           x_out_ref, p_ref, buf, sem, *, n_tiles):
    y = _gather_expert_rows(dest_ref, dest_next_ref, ys_ref, gate_ref, buf, sem, n_tiles)
    x = x_ref[...] + mp_ref[0, 5:6, :] * y
    x_out_ref[...] = x
    h = _rms(x, g_ref[...]) * (1.0 + mc_ref[0, 1:2, :]) + mc_ref[0, 0:1, :]
    p_ref[...] = _bdot(h, w_ref[...])


def _moe_gather_specs(tm, n_tiles, d):
    return [pl.BlockSpec((1, 1, TOP_K * tm), lambda i: (i, 0, 0), memory_space=pltpu.SMEM),
            pl.BlockSpec((1, 1, TOP_K * tm), lambda i: (jnp.minimum(i + 1, n_tiles - 1), 0, 0),
                         memory_space=pltpu.SMEM),
            pl.BlockSpec(memory_space=pl.ANY),
            pl.BlockSpec((tm, TOP_K), lambda i: (i, 0)),
            pl.BlockSpec((tm, d), lambda i: (i, 0))]


def _moe_gather_scratch(tm):
    return [pltpu.VMEM((2, TOP_K * tm * ROW_SUBLANES, LANES), F32), pltpu.SemaphoreType.DMA((2,))]


def _combine_inproj_call(pending, x, modsel_prev, modsel, g, w):
    dest_tiles, ys, gates_nk = pending
    n, d = x.shape
    wout = w.shape[1]
    tm = TOKEN_TILE
    n_tiles = n // tm
    body = functools.partial(_combine_inproj_kernel, n_tiles=n_tiles)
    return pl.pallas_call(
        body,
        grid=(n_tiles,),
        in_specs=_moe_gather_specs(tm, n_tiles, d)
                 + [pl.BlockSpec((1, 6, d), lambda i: (i, 0, 0)),
                    pl.BlockSpec((1, 6, d), lambda i: (i, 0, 0)),
                    pl.BlockSpec((1, d), lambda i: (0, 0)),
                    pl.BlockSpec((d, wout), lambda i: (0, 0))],
        out_specs=[pl.BlockSpec((tm, d), lambda i: (i, 0)),
                   pl.BlockSpec((tm, wout), lambda i: (i, 0))],
        out_shape=[jax.ShapeDtypeStruct((n, d), F32),
                   jax.ShapeDtypeStruct((n, wout), F32)],
        scratch_shapes=_moe_gather_scratch(tm),
        compiler_params=_params(("arbitrary",)),
        name="moe_combine_inproj",
    )(dest_tiles, dest_tiles, ys, gates_nk, x, modsel_prev, modsel, g.reshape(1, d), w)


def _combine_call(pending, x, modsel, final_g):
    dest_tiles, ys, gates_nk = pending
    n, d = x.shape
    tm = TOKEN_TILE
    n_tiles = n // tm
    body = functools.partial(_combine_kernel, n_tiles=n_tiles)
    return pl.pallas_call(
        body,
        grid=(n_tiles,),
        in_specs=_moe_gather_specs(tm, n_tiles, d)
                 + [pl.BlockSpec((1, 6, d), lambda i: (i, 0, 0)),
                    pl.BlockSpec((1, d), lambda i: (0, 0))],
        out_specs=pl.BlockSpec((tm, d), lambda i: (i, 0)),
        out_shape=jax.ShapeDtypeStruct((n, d), F32),
        scratch_shapes=_moe_gather_scratch(tm),
        compiler_params=_params(("arbitrary",)),
        name="moe_combine",
    )(dest_tiles, dest_tiles, ys, gates_nk, x, modsel, final_g.reshape(1, d))


def _moe_experts(layer, x, modsel, g2, w_router, b_router, w_gu, b_gu, w_dn, b_dn):
    n, d = x.shape
    n_experts = w_router.shape[1]
    tm = TOKEN_TILE
    n_blocks = (n * TOP_K) // MOE_BLOCK + n_experts
    idx_t, gate_t, rank_t, counts = _route_call(x, modsel, g2, w_router, b_router)
    dest_t, block_info, pad_info, n_used = _slot_call(counts, idx_t, rank_t, n_blocks)
    dest_tiles = dest_t.reshape(TOP_K, n // tm, tm).transpose(1, 0, 2).reshape(n // tm, 1, TOP_K * tm)
    xs = _dispatch_call(pad_info, dest_tiles, x, modsel, g2, n_blocks * MOE_BLOCK)
    ys = _ffn_call(layer, block_info.reshape(3 * n_blocks), n_used.reshape(1), xs, w_gu, b_gu, w_dn, b_dn)
    return dest_tiles, ys, gate_t.T


def _rope_tables(t, hd):
    pos = np.arange(t)
    n_freq = hd // 4
    inv_freq = ROPE_THETA ** (-np.arange(n_freq, dtype=np.float32) / n_freq)
    ang = np.concatenate([(pos // GRID_W).astype(np.float32)[:, None] * inv_freq,
                          (pos % GRID_W).astype(np.float32)[:, None] * inv_freq], axis=-1)
    ang = jnp.asarray(ang, F32)
    cos, sin = jnp.cos(ang), jnp.sin(ang)
    return jnp.concatenate([cos, cos], axis=-1), jnp.concatenate([-sin, sin], axis=-1)


def kernel(x_prompt, x_sample, cache_k_a, cache_v_a, state_b, state_c_C, state_c_n, state_c_m, cache_k_d, cache_v_d, c, c_ctx, norm1_g, norm2_g, w_mod, b_mod, w_in_a, qnorm_a, knorm_a, w_out_a, w_in_b, b_f_b, lower_bounds_b, onorm_b, w_out_b, w_in_c, b_gates_c, onorm_c, w_out_c, w_in_d, rpb_d, w_out_d, w_router, b_router, w_gu, b_gu, w_dn, b_dn, final_g):
    n_ctx_seq, t_ctx, d = x_prompt.shape
    n_lat_seq, t_lat, _ = x_sample.shape
    depth = w_mod.shape[0]
    n_ctx = n_ctx_seq * t_ctx
    n_lat = n_lat_seq * t_lat
    n = n_ctx + n_lat
    tm = TOKEN_TILE
    assert t_ctx % tm == 0 and t_lat % tm == 0 and n_lat_seq + 1 <= 8

    lb_cum = jnp.cumsum(jax.nn.softmax(lower_bounds_b.astype(F32), axis=0), axis=0)
    lb_all = lb_cum - lb_cum[0]

    cond8 = jnp.zeros((8, d), F32).at[0].set(c_ctx).at[1:1 + n_lat_seq].set(c)
    mod = _mod_call(cond8, w_mod, b_mod)
    tile_row = np.concatenate([np.zeros(n_ctx // tm, np.int32),
                               1 + np.repeat(np.arange(n_lat_seq, dtype=np.int32), t_lat // tm)])

    x = jnp.concatenate([x_prompt.reshape(n_ctx, d), x_sample.reshape(n_lat, d)], axis=0)
    outs = {}
    pending = None
    modsel = None

    def inproj(x, modsel, g, w):
        if pending is None:
            return x, _inproj_call(x, modsel, g, w)
        return _combine_inproj_call(pending, x, modsel_prev, modsel, g, w)

    for i in range(depth):
        kind = i % 4
        j = i // 4
        modsel_prev = modsel
        modsel = mod[i].reshape(8, 6, d)[tile_row]
        if kind == 0:
            x, proj = inproj(x, modsel, norm1_g[i],w_in_a[j].astype(BF16))
            o_ctx, k_new = _gqa_ctx_call(proj, qnorm_a[j], knorm_a[j], n_ctx_seq, t_ctx)
            cosd, sind = _rope_tables(t_lat, 128)
            o_lat = _gqa_lat_call(proj, n_ctx, n_lat_seq, t_lat,
                                  cache_k_a[:, j].reshape(n_lat_seq, -1, 256),
                                  cache_v_a[:, j].reshape(n_lat_seq, -1, 256), cosd, sind,
                                  qnorm_a[j], knorm_a[j])
            outs["k_a"] = k_new.reshape(n_ctx_seq, 1, t_ctx, 2, 128)
            outs["v_a"] = proj[:n_ctx, 1280:1536].reshape(n_ctx_seq, 1, t_ctx, 2, 128)
            x = _outproj_call("plain", [(o_ctx, o_lat)], None, w_out_a[j].astype(BF16), x, modsel)
        elif kind == 1:
            x, proj = inproj(x, modsel, norm1_g[i],w_in_b[j].astype(BF16))
            o_dirs, s_dirs = [], []
            for reverse in (False, True):
                oc, sc = _hgrn_call(proj, 0, n_ctx_seq, t_ctx, b_f_b[j], lb_all[i], None, reverse)
                ol, _ = _hgrn_call(proj, n_ctx, n_lat_seq, t_lat, b_f_b[j], lb_all[i], state_b[:, j], reverse)
                o_dirs.append((oc, ol))
                s_dirs.append(sc)
            outs["s_b"] = jnp.concatenate(s_dirs, axis=1)[:, None]
            x = _outproj_call("hgrn", o_dirs, (proj, 4, onorm_b[j].reshape(1, 128)),
                              w_out_b[j].astype(BF16), x, modsel)
        elif kind == 2:
            w_c = jnp.pad(w_in_c[j], ((0, 0), (0, 128 - 32))).astype(BF16)
            bg = jnp.pad(b_gates_c[j].reshape(1, 32), ((0, 0), (0, 128 - 32)))
            x, proj = inproj(x, modsel, norm1_g[i],w_c)
            state = (state_c_C[:, j], state_c_n[:, j][:, :, :, None, :],
                     jnp.pad(state_c_m[:, j], ((0, 0), (0, 0), (0, 120)))[:, :, None, :])
            ocf, ocb, cc, nc, mc = _mlstm_call(proj, 0, n_ctx_seq, t_ctx, bg, None)
            olf, olb, _, _, _ = _mlstm_call(proj, n_ctx, n_lat_seq, t_lat, bg, state)
            o_dirs = [(ocf, olf), (ocb, olb)]
            outs["c_C"] = cc[:, None]
            outs["c_n"] = nc[:, None, :, :, 0, :]
            outs["c_m"] = mc[:, None, :, 0, :8]
            x = _outproj_call("mlstm", o_dirs, (proj, 2, onorm_c[j].reshape(1, 128)),
                              w_out_c[j].astype(BF16), x, modsel)
        else:
            x, proj = inproj(x, modsel, norm1_g[i],w_in_d[j].astype(BF16))
            o_ctx = _mha_ctx_call(proj, n_ctx_seq, t_ctx)
            tz = _na_bias_call(rpb_d[j])
            o_lat = _na_call(proj, n_ctx, n_lat_seq, t_lat,
                             cache_k_d[:, j].reshape(n_lat_seq, -1, d),
                             cache_v_d[:, j].reshape(n_lat_seq, -1, d), tz)
            outs["k_d"] = proj[:n_ctx, d:2 * d].reshape(n_ctx_seq, 1, t_ctx, 16, 64)
            outs["v_d"] = proj[:n_ctx, 2 * d:3 * d].reshape(n_ctx_seq, 1, t_ctx, 16, 64)
            x = _outproj_call("plain", [(o_ctx, o_lat)], None, w_out_d[j].astype(BF16), x, modsel)
        pending = _moe_experts(i, x, modsel, norm2_g[i], w_router[i], b_router[i], w_gu, b_gu, w_dn, b_dn)
    x = _combine_call(pending, x, modsel, final_g)

    y_prompt = x[:n_ctx].reshape(n_ctx_seq, t_ctx, d)
    y_sample = x[n_ctx:].reshape(n_lat_seq, t_lat, d)
    return (y_prompt, y_sample, outs["k_a"], outs["v_a"], outs["s_b"], outs["c_C"], outs["c_n"], outs["c_m"],
            outs["k_d"], outs["v_d"])
```

```python
import functools

import numpy as np
import jax
import jax.numpy as jnp
from jax import lax
from jax.experimental import pallas as pl
from jax.experimental.pallas import tpu as pltpu

F32 = jnp.float32
BF16 = jnp.bfloat16
I32 = jnp.int32

NORM_EPS = 1e-6
GRID_W = 64
ROPE_THETA = 10000.0
TOP_K = 4
HGRN_CHUNK = 128
GATE_SOFTCAP = 15.0
NA_ROWS = 8
NA_COLS = 16
SWIGLU_ALPHA = 1.702
SWIGLU_LIMIT = 7.0
NEG_BIG = -1e30

LANES = 128
ROW_SUBLANES = 8
TOKEN_TILE = 256
PROJ_TILE = 512
MOE_BLOCK = 256
ATTN_HEAD_GROUP = 8
V7X_VMEM_LIMIT = 52 * 1024 * 1024


def _params(sem, vmem=V7X_VMEM_LIMIT):
    return pltpu.CompilerParams(dimension_semantics=sem, vmem_limit_bytes=vmem)


def _bdot(a, b):
    return jnp.dot(a.astype(BF16), b.astype(BF16), preferred_element_type=F32)


def _bdot_nt(a, b):
    return lax.dot_general(a.astype(BF16), b.astype(BF16), (((1,), (1,)), ((), ())),
                           preferred_element_type=F32)


def _bdot_tn(a, b):
    return lax.dot_general(a.astype(BF16), b.astype(BF16), (((0,), (0,)), ((), ())),
                           preferred_element_type=F32)


def _fdot(a, b):
    return jnp.dot(a, b, preferred_element_type=F32, precision=lax.Precision.HIGHEST)


def _rms(x, g):
    return x * lax.rsqrt(jnp.mean(x * x, axis=-1, keepdims=True) + NORM_EPS) * g


def _rms_heads(x, g, n_heads, hd):
    return jnp.concatenate([_rms(x[:, h * hd:(h + 1) * hd], g) for h in range(n_heads)], axis=-1)


def _sigmoid(x):
    return 1.0 / (1.0 + jnp.exp(-x))


def _silu(x):
    return x * _sigmoid(x)


def _softmax_rows(parts):
    m = parts[0].max(axis=-1, keepdims=True)
    for p in parts[1:]:
        m = jnp.maximum(m, p.max(axis=-1, keepdims=True))
    es = [jnp.exp(p - m) for p in parts]
    den = es[0].sum(axis=-1, keepdims=True)
    for e in es[1:]:
        den = den + e.sum(axis=-1, keepdims=True)
    return es, den


def _mod_kernel(c_ref, w_ref, b_ref, o_ref):
    o_ref[0] = _bdot(_silu(c_ref[...]), w_ref[0]) + b_ref[0]


def _mod_call(cond8, w_mod, b_mod):
    depth, d, d6 = w_mod.shape
    tn = 1024
    return pl.pallas_call(
        _mod_kernel,
        grid=(depth, d6 // tn),
        in_specs=[pl.BlockSpec((8, d), lambda i, j: (0, 0)),
                  pl.BlockSpec((1, d, tn), lambda i, j: (i, 0, j)),
                  pl.BlockSpec((1, 1, tn), lambda i, j: (i, 0, j))],
        out_specs=pl.BlockSpec((1, 8, tn), lambda i, j: (i, 0, j)),
        out_shape=jax.ShapeDtypeStruct((depth, 8, d6), F32),
        compiler_params=_params(("arbitrary", "arbitrary")),
        name="adaln_mod",
    )(cond8, w_mod, b_mod.reshape(depth, 1, d6))


def _inproj_kernel(x_ref, m_ref, g_ref, w_ref, o_ref):
    h = _rms(x_ref[...], g_ref[...]) * (1.0 + m_ref[0, 1:2, :]) + m_ref[0, 0:1, :]
    o_ref[...] = _bdot(h, w_ref[...])


def _inproj_call(x, modsel, g, w):
    n, d = x.shape
    wout = w.shape[1]
    tm = PROJ_TILE
    return pl.pallas_call(
        _inproj_kernel,
        grid=(n // tm,),
        in_specs=[pl.BlockSpec((tm, d), lambda i: (i, 0)),
                  pl.BlockSpec((1, 6, d), lambda i: (i * (PROJ_TILE // TOKEN_TILE), 0, 0)),
                  pl.BlockSpec((1, d), lambda i: (0, 0)),
                  pl.BlockSpec((d, wout), lambda i: (0, 0))],
        out_specs=pl.BlockSpec((tm, wout), lambda i: (i, 0)),
        out_shape=jax.ShapeDtypeStruct((n, wout), F32),
        compiler_params=_params(("arbitrary",)),
        name="inproj",
    )(x, modsel, g.reshape(1, d), w)


def _outproj_kernel(*refs, kind, n_pairs, n_ctx_tiles, n_heads, hd):
    in_ctx = pl.program_id(0) < n_ctx_tiles
    mix = [jnp.where(in_ctx, refs[2 * p][...], refs[2 * p + 1][...]) for p in range(n_pairs)]
    rest = refs[2 * n_pairs:]
    if kind == "plain":
        w_ref, x_ref, m_ref, y_ref = rest
        o = mix[0]
    else:
        gate_ref, on_ref, w_ref, x_ref, m_ref, y_ref = rest
        o = _rms_heads(mix[0] + mix[1], on_ref[...], n_heads, hd)
        o = o * _silu(gate_ref[...]) if kind == "hgrn" else _sigmoid(gate_ref[...]) * o
    y_ref[...] = x_ref[...] + m_ref[0, 2:3, :] * _bdot(o, w_ref[...])


def _outproj_call(kind, pairs, gate, w, x, modsel):
    n, d = x.shape
    tm = PROJ_TILE
    n_ctx_tiles = pairs[0][0].shape[0] // tm
    specs, args = [], []
    for a_ctx, a_lat in pairs:
        specs += [pl.BlockSpec((tm, d), lambda i: (jnp.minimum(i, n_ctx_tiles - 1), 0)),
                  pl.BlockSpec((tm, d), lambda i: (jnp.maximum(i - n_ctx_tiles, 0), 0))]
        args += [a_ctx, a_lat]
    if gate is not None:
        proj, cb, on = gate
        specs += [pl.BlockSpec((tm, d), lambda i: (i, cb)), pl.BlockSpec(on.shape, lambda i: (0, 0))]
        args += [proj, on]
    specs += [pl.BlockSpec(w.shape, lambda i: (0, 0)),
              pl.BlockSpec((tm, d), lambda i: (i, 0)),
              pl.BlockSpec((1, 6, d), lambda i: (i * (PROJ_TILE // TOKEN_TILE), 0, 0))]
    args += [w, x, modsel]
    body = functools.partial(_outproj_kernel, kind=kind, n_pairs=len(pairs), n_ctx_tiles=n_ctx_tiles,
                             n_heads=8, hd=128)
    return pl.pallas_call(
        body,
        grid=(n // tm,),
        in_specs=specs,
        out_specs=pl.BlockSpec((tm, d), lambda i: (i, 0)),
        out_shape=jax.ShapeDtypeStruct((n, d), F32),
        compiler_params=_params(("arbitrary",)),
        name="outproj_" + kind,
    )(*args)


def _gqa_ctx_kernel(p_ref, qn_ref, kn_ref, o_ref, k_ref, *, n_heads, n_kv, hd):
    rep = n_heads // n_kv
    scale = hd ** -0.5
    koff = n_heads * hd
    voff = koff + n_kv * hd
    ks = [_rms(p_ref[:, koff + g * hd: koff + (g + 1) * hd], kn_ref[...]) for g in range(n_kv)]
    k_ref[...] = jnp.concatenate(ks, axis=-1)
    for h in range(n_heads):
        g = h // rep
        q = _rms(p_ref[:, h * hd:(h + 1) * hd], qn_ref[...])
        s = _bdot_nt(q, ks[g]) * scale
        (e,), den = _softmax_rows([s])
        o_ref[:, h * hd:(h + 1) * hd] = _bdot(e / den, p_ref[:, voff + g * hd: voff + (g + 1) * hd])


def _gqa_ctx_call(proj, qn, kn, n_seq, t):
    n_heads, n_kv, hd = 8, 2, 128
    win = proj.shape[1]
    body = functools.partial(_gqa_ctx_kernel, n_heads=n_heads, n_kv=n_kv, hd=hd)
    return pl.pallas_call(
        body,
        grid=(n_seq,),
        in_specs=[pl.BlockSpec((t, win), lambda b: (b, 0)),
                  pl.BlockSpec((1, hd), lambda b: (0, 0)),
                  pl.BlockSpec((1, hd), lambda b: (0, 0))],
        out_specs=[pl.BlockSpec((t, n_heads * hd), lambda b: (b, 0)),
                   pl.BlockSpec((t, n_kv * hd), lambda b: (b, 0))],
        out_shape=[jax.ShapeDtypeStruct((n_seq * t, n_heads * hd), F32),
                   jax.ShapeDtypeStruct((n_seq * t, n_kv * hd), F32)],
        compiler_params=_params(("arbitrary",)),
        name="gqa_ctx",
    )(proj, qn.reshape(1, hd), kn.reshape(1, hd))


def _rope(x, cosd, sind):
    return x * cosd + pltpu.roll(x, x.shape[-1] // 2, 1) * sind


def _gqa_lat_kernel(pq_ref, pkv_ref, kc_ref, vc_ref, cq_ref, sq_ref, ck_ref, sk_ref, qn_ref, kn_ref,
                    o_ref, k_scr, v_scr, *, n_heads, n_kv, hd, t_ctx):
    rep = n_heads // n_kv
    scale = hd ** -0.5

    @pl.when(pl.program_id(1) == 0)
    def _():
        k_scr[0:t_ctx, :] = kc_ref[0].astype(BF16)
        v_scr[...] = jnp.ones_like(v_scr)
        for g in range(n_kv):
            k = _rms(pkv_ref[:, g * hd:(g + 1) * hd], kn_ref[...])
            k_scr[t_ctx:, g * hd:(g + 1) * hd] = _rope(k, ck_ref[...], sk_ref[...]).astype(BF16)
            v_scr[0:t_ctx, 2 * g * hd:(2 * g + 1) * hd] = vc_ref[0, :, g * hd:(g + 1) * hd].astype(BF16)
            v_scr[t_ctx:, 2 * g * hd:(2 * g + 1) * hd] = pkv_ref[:, (n_kv + g) * hd:(n_kv + g + 1) * hd].astype(BF16)

    qs = [(_rope(_rms(pq_ref[:, h * hd:(h + 1) * hd], qn_ref[...]), cq_ref[...], sq_ref[...]) * scale).astype(BF16)
          for h in range(n_heads)]
    for h0 in range(0, n_heads, ATTN_HEAD_GROUP):
        hs = range(h0, min(h0 + ATTN_HEAD_GROUP, n_heads))
        s = [_bdot_nt(qs[h], k_scr[:, (h // rep) * hd:(h // rep + 1) * hd]) for h in hs]
        e = [jnp.exp(x - x.max(axis=-1, keepdims=True)).astype(BF16) for x in s]
        pv = [jnp.dot(e[i], v_scr[:, 2 * (h // rep) * hd:2 * (h // rep + 1) * hd], preferred_element_type=F32)
              for i, h in enumerate(hs)]
        for i, h in enumerate(hs):
            o_ref[:, h * hd:(h + 1) * hd] = pv[i][:, :hd] / pv[i][:, hd:]


def _gqa_lat_call(proj, row0, n_seq, t, cache_k, cache_v, cosd, sind, qn, kn):
    n_heads, n_kv, hd = 8, 2, 128
    tq = 256
    t_ctx = cache_k.shape[1]
    nq = t // tq
    kvw = 2 * n_kv * hd
    qblk0 = row0 // tq
    sblk0 = row0 // t
    body = functools.partial(_gqa_lat_kernel, n_heads=n_heads, n_kv=n_kv, hd=hd, t_ctx=t_ctx)
    return pl.pallas_call(
        body,
        grid=(n_seq, nq),
        in_specs=[pl.BlockSpec((tq, n_heads * hd), lambda b, i: (qblk0 + b * nq + i, 0)),
                  pl.BlockSpec((t, kvw), lambda b, i: (sblk0 + b, (n_heads * hd) // kvw)),
                  pl.BlockSpec((1, t_ctx, n_kv * hd), lambda b, i: (b, 0, 0)),
                  pl.BlockSpec((1, t_ctx, n_kv * hd), lambda b, i: (b, 0, 0)),
                  pl.BlockSpec((tq, hd), lambda b, i: (i, 0)),
                  pl.BlockSpec((tq, hd), lambda b, i: (i, 0)),
                  pl.BlockSpec((t, hd), lambda b, i: (0, 0)),
                  pl.BlockSpec((t, hd), lambda b, i: (0, 0)),
                  pl.BlockSpec((1, hd), lambda b, i: (0, 0)),
                  pl.BlockSpec((1, hd), lambda b, i: (0, 0))],
        out_specs=pl.BlockSpec((tq, n_heads * hd), lambda b, i: (b * nq + i, 0)),
        out_shape=jax.ShapeDtypeStruct((n_seq * t, n_heads * hd), F32),
        scratch_shapes=[pltpu.VMEM((t_ctx + t, n_kv * hd), BF16),
                        pltpu.VMEM((t_ctx + t, 2 * n_kv * hd), BF16)],
        compiler_params=_params(("arbitrary", "arbitrary")),
        name="gqa_latent",
    )(proj, proj, cache_k, cache_v, cosd, sind, cosd, sind, qn.reshape(1, hd), kn.reshape(1, hd))


def _mha_ctx_kernel(q_ref, k_ref, v_ref, o_ref, *, hd):
    scale = hd ** -0.5
    n_heads = q_ref.shape[1] // hd
    ones = jnp.ones((v_ref.shape[0], hd), BF16)
    for h0 in range(0, n_heads, ATTN_HEAD_GROUP):
        sls = [slice(h * hd, (h + 1) * hd) for h in range(h0, min(h0 + ATTN_HEAD_GROUP, n_heads))]
        s = [_bdot_nt(q_ref[:, sl] * scale, k_ref[:, sl]) for sl in sls]
        e = [jnp.exp(x - x.max(axis=-1, keepdims=True)).astype(BF16) for x in s]
        pv = [jnp.dot(e[i], jnp.concatenate([v_ref[:, sl].astype(BF16), ones], axis=-1),
                      preferred_element_type=F32) for i, sl in enumerate(sls)]
        for i, sl in enumerate(sls):
            o_ref[:, sl] = pv[i][:, :hd] / pv[i][:, hd:]


def _mha_ctx_call(proj, n_seq, t):
    hd = 64
    d = proj.shape[1] // 3
    body = functools.partial(_mha_ctx_kernel, hd=hd)
    return pl.pallas_call(
        body,
        grid=(n_seq,),
        in_specs=[pl.BlockSpec((t, d), lambda b: (b, 0)),
                  pl.BlockSpec((t, d), lambda b: (b, 1)),
                  pl.BlockSpec((t, d), lambda b: (b, 2))],
        out_specs=pl.BlockSpec((t, d), lambda b: (b, 0)),
        out_shape=jax.ShapeDtypeStruct((n_seq * t, d), F32),
        compiler_params=_params(("arbitrary",)),
        name="mha_ctx",
    )(proj, proj, proj)


def _na_bias_kernel(rpb_ref, o_ref, *, n_rel_rows, n_rel_cols):
    h = pl.program_id(0)
    w_io = lax.broadcasted_iota(I32, (GRID_W, 2 * GRID_W), 0)
    lane = lax.broadcasted_iota(I32, (GRID_W, 2 * GRID_W), 1)
    ck = jnp.where(lane < GRID_W, lane, lane - GRID_W)
    c_start = jnp.clip(w_io - NA_COLS // 2, 0, GRID_W - NA_COLS)
    in_win = (ck >= c_start) & (ck < c_start + NA_COLS)
    rel = ck - w_io + (NA_COLS - 1)
    base = h * (n_rel_rows * n_rel_cols)
    tiles = []
    for j in range(n_rel_rows):
        acc = jnp.zeros((GRID_W, 2 * GRID_W), F32)
        for jj in range(n_rel_cols):
            acc = jnp.where(rel == jj, rpb_ref[base + j * n_rel_cols + jj], acc)
        tiles.append(jnp.where(in_win, acc, NEG_BIG))
    for j in range(n_rel_rows):
        hi = tiles[j + 1] if j + 1 < n_rel_rows else jnp.full((GRID_W, 2 * GRID_W), NEG_BIG, F32)
        o_ref[0, j] = jnp.where(lane < GRID_W, tiles[j], hi)


def _na_bias_call(rpb):
    n_heads, nrr, nrc = rpb.shape
    body = functools.partial(_na_bias_kernel, n_rel_rows=nrr, n_rel_cols=nrc)
    return pl.pallas_call(
        body,
        grid=(n_heads,),
        in_specs=[pl.BlockSpec(memory_space=pltpu.SMEM)],
        out_specs=pl.BlockSpec((1, nrr, GRID_W, 2 * GRID_W), lambda h: (h, 0, 0, 0)),
        out_shape=jax.ShapeDtypeStruct((n_heads, nrr, GRID_W, 2 * GRID_W), F32),
        compiler_params=_params(("arbitrary",)),
        name="na_bias",
    )(rpb.reshape(-1))


NA_QROWS = 4
NA_KROWS = 12


def _na_kernel(q_ref, k0_ref, k1_ref, k2_ref, v0_ref, v1_ref, v2_ref, kc_ref, vc_ref, tz_ref, o_ref,
               *, hd, n_grid_rows):
    scale = hd ** -0.5
    blk = pl.program_id(1)
    kstart = jnp.clip(blk * NA_QROWS - NA_ROWS // 2, 0, n_grid_rows - NA_KROWS)
    lane = lax.broadcasted_iota(I32, (GRID_W, 2 * GRID_W), 1)
    n_rel = tz_ref.shape[1]
    heads = range(q_ref.shape[1] // hd)
    sls = [slice(hh * hd, (hh + 1) * hd) for hh in heads]
    rel, pen = [], []
    for rq_l in range(NA_QROWS):
        rq = blk * NA_QROWS + rq_l
        r_start = jnp.clip(rq - NA_ROWS // 2, 0, n_grid_rows - NA_ROWS)
        rel.append([])
        pen.append([])
        for m in range(NA_KROWS // 2):
            rk = kstart + 2 * m
            rel[-1].append(jnp.clip(rk - rq + (NA_ROWS - 1), 0, n_rel - 1))
            ok0 = (rk >= r_start) & (rk < r_start + NA_ROWS)
            ok1 = (rk + 1 >= r_start) & (rk + 1 < r_start + NA_ROWS)
            pen[-1].append(jnp.where(lane < GRID_W, jnp.where(ok0, 0.0, NEG_BIG), jnp.where(ok1, 0.0, NEG_BIG)))
    bias = [jnp.concatenate([jnp.concatenate([tz_ref[hh, pl.ds(rel[r][m], 1)][0] + pen[r][m]
                                              for m in range(NA_KROWS // 2)], axis=-1)
                             for r in range(NA_QROWS)], axis=0) for hh in heads]
    q = [q_ref[:, sl] * scale for sl in sls]
    k_loc = [jnp.concatenate([k0_ref[:, sl], k1_ref[:, sl], k2_ref[:, sl]], axis=0) for sl in sls]
    v_loc = [jnp.concatenate([v0_ref[:, sl], v1_ref[:, sl], v2_ref[:, sl]], axis=0).astype(BF16) for sl in sls]
    v_ctx = [vc_ref[0, :, sl].astype(BF16) for sl in sls]
    s_loc = [_bdot_nt(q[h], k_loc[h]) + bias[h] for h in heads]
    s_ctx = [_bdot_nt(q[h], kc_ref[0, :, sls[h]]) for h in heads]
    m = [jnp.maximum(s_loc[h].max(axis=-1, keepdims=True), s_ctx[h].max(axis=-1, keepdims=True)) for h in heads]
    e_loc = [jnp.exp(s_loc[h] - m[h]).astype(BF16) for h in heads]
    e_ctx = [jnp.exp(s_ctx[h] - m[h]).astype(BF16) for h in heads]
    pv = [jnp.dot(e_loc[h], jnp.concatenate([v_loc[h], jnp.ones_like(v_loc[h])], axis=-1),
                  preferred_element_type=F32)
          + jnp.dot(e_ctx[h], jnp.concatenate([v_ctx[h], jnp.ones_like(v_ctx[h])], axis=-1),
                    preferred_element_type=F32) for h in heads]
    for h in heads:
        o_ref[:, sls[h]] = pv[h][:, :hd] / pv[h][:, hd:]


def _na_call(proj, row0, n_seq, t, cache_k, cache_v, tz):
    hd, cw = 64, 256
    d = proj.shape[1] // 3
    ncb = d // cw
    tq = NA_QROWS * GRID_W
    nq = t // tq
    n_grid_rows = t // GRID_W
    t_ctx = cache_k.shape[1]
    qblk0 = row0 // tq

    def kv_map(which, j):
        def index_map(b, i, c):
            ks = jnp.clip(i * NA_QROWS - NA_ROWS // 2, 0, n_grid_rows - NA_KROWS) // NA_QROWS
            return (qblk0 + b * nq + ks + j, which * ncb + c)
        return index_map

    body = functools.partial(_na_kernel, hd=hd, n_grid_rows=n_grid_rows)
    return pl.pallas_call(
        body,
        grid=(n_seq, nq, ncb),
        in_specs=[pl.BlockSpec((tq, cw), lambda b, i, c: (qblk0 + b * nq + i, c))]
                 + [pl.BlockSpec((tq, cw), kv_map(1, j)) for j in range(3)]
                 + [pl.BlockSpec((tq, cw), kv_map(2, j)) for j in range(3)]
                 + [pl.BlockSpec((1, t_ctx, cw), lambda b, i, c: (b, 0, c)),
                    pl.BlockSpec((1, t_ctx, cw), lambda b, i, c: (b, 0, c)),
                    pl.BlockSpec((cw // hd,) + tz.shape[1:], lambda b, i, c: (c, 0, 0, 0))],
        out_specs=pl.BlockSpec((tq, cw), lambda b, i, c: (b * nq + i, c)),
        out_shape=jax.ShapeDtypeStruct((n_seq * t, d), F32),
        compiler_params=_params(("arbitrary", "arbitrary", "arbitrary")),
        name="nbr_attn",
    )(proj, proj, proj, proj, proj, proj, proj, cache_k, cache_v, tz)


def _hgrn_kernel(q_ref, v_ref, f_ref, bf_ref, lb_ref, s0_ref, o_ref, s_ref, st_scr,
                 *, reverse, n_heads, dk, has_s0, n_blk):
    c = pl.program_id(1)
    L = HGRN_CHUNK
    tb = q_ref.shape[0]

    @pl.when(c == 0)
    def _():
        for h in range(n_heads):
            if has_s0:
                st_scr[h] = s0_ref[0, 0, h].T
            else:
                st_scr[h] = jnp.zeros_like(st_scr[h])

    row = lax.broadcasted_iota(I32, (L, L), 0)
    col = lax.broadcasted_iota(I32, (L, L), 1)
    tri = jnp.where((col >= row) if reverse else (col <= row), 1.0, 0.0).astype(F32)
    eye = jnp.where(row == col, 1.0, 0.0).astype(F32)
    halves = [L >> (i + 1) for i in range(L.bit_length() - 1)]
    same_pair = [jnp.where((row // (2 * hf)) == (col // (2 * hf)), 1.0, 0.0).astype(F32) for hf in halves]
    rio = lax.broadcasted_iota(I32, (L, q_ref.shape[1]), 0)
    is_query = [jnp.where(((rio & hf) == 0) if reverse else ((rio & hf) != 0), 1.0, 0.0).astype(F32)
                for hf in halves]
    r8 = lax.broadcasted_iota(I32, (ROW_SUBLANES, 1), 0)
    lb = lb_ref[0]
    bf = bf_ref[0]
    n_chunks = tb // L

    def boundary_rows(cum, hf):
        blk = 2 * hf
        off = hf if reverse else hf - 1
        width = cum.shape[1]
        if blk >= ROW_SUBLANES:
            return jnp.concatenate([jnp.broadcast_to(cum[a + off:a + off + 1, :], (blk, width))
                                    for a in range(0, L, blk)], axis=0)
        groups = []
        for g in range(0, L, ROW_SUBLANES):
            ref = jnp.broadcast_to(cum[g + off:g + off + 1, :], (ROW_SUBLANES, width))
            for a in range(blk, ROW_SUBLANES, blk):
                ref = jnp.where(r8 >= a, jnp.broadcast_to(cum[g + a + off:g + a + off + 1, :],
                                                          (ROW_SUBLANES, width)), ref)
            groups.append(ref)
        return jnp.concatenate(groups, axis=0)

    def chunk_step(jj, carry):
        jc = (n_chunks - 1 - jj) if reverse else jj
        rs = pl.ds(pl.multiple_of(jc * L, L), L)
        q = _silu(q_ref[rs, :]) * (dk ** -0.5)
        v = v_ref[rs, :]
        f = lb + (1.0 - lb) * _sigmoid(f_ref[rs, :] + bf)
        logf = jnp.log(f)
        kk = 1.0 - f
        cum = _fdot(tri, logf)
        end = cum[0:1, :] if reverse else cum[L - 1:L, :]
        qd = q * jnp.exp(cum)
        kd = kk * jnp.exp(end - cum)
        e_end = jnp.exp(end)
        q_lv, k_lv = [], []
        for lv, hf in enumerate(halves):
            isq = is_query[lv]
            decay = jnp.exp((2.0 * isq - 1.0) * (cum - boundary_rows(cum, hf)))
            dq = decay * isq
            q_lv.append((q * dq).astype(BF16))
            k_lv.append((kk * (decay - dq)).astype(BF16))
        qk_diag = q * kk
        hsl = [slice(h * dk, (h + 1) * dk) for h in range(n_heads)]
        inter = [_bdot_nt(qd[:, hs], st_scr[h]) for h, hs in enumerate(hsl)]
        attn = [eye * jnp.sum(qk_diag[:, hs], axis=-1, keepdims=True) for hs in hsl]
        for lv in range(len(halves)):
            prods = [_bdot_nt(q_lv[lv][:, hs], k_lv[lv][:, hs]) for hs in hsl]
            attn = [attn[h] + same_pair[lv] * prods[h] for h in range(n_heads)]
        outs = [inter[h] + _bdot(attn[h], v[:, hs]) for h, hs in enumerate(hsl)]
        o_ref[rs, :] = jnp.concatenate(outs, axis=-1)
        for h in range(n_heads):
            hs = slice(h * dk, (h + 1) * dk)
            st_scr[h] = st_scr[h] * e_end[:, hs] + _bdot_tn(v[:, hs], kd[:, hs])
        return carry

    lax.fori_loop(0, n_chunks, chunk_step, 0)

    @pl.when(c == n_blk - 1)
    def _():
        for h in range(n_heads):
            s_ref[0, 0, h] = st_scr[h].T


def _hgrn_call(proj, row0, n_seq, t, b_f, lb, s0, reverse):
    n_heads, dk = 8, 128
    d = n_heads * dk
    tb = 256
    n_blk = t // tb
    blk0 = row0 // tb
    di = 1 if reverse else 0
    has_s0 = s0 is not None
    if not has_s0:
        s0 = jnp.zeros((1, 2, n_heads, dk, dk), F32)

    def tok(b, c):
        return blk0 + b * n_blk + ((n_blk - 1 - c) if reverse else c)

    body = functools.partial(_hgrn_kernel, reverse=reverse, n_heads=n_heads, dk=dk,
                             has_s0=has_s0, n_blk=n_blk)
    return pl.pallas_call(
        body,
        grid=(n_seq, n_blk),
        in_specs=[pl.BlockSpec((tb, d), lambda b, c: (tok(b, c), 0)),
                  pl.BlockSpec((tb, d), lambda b, c: (tok(b, c), 1)),
                  pl.BlockSpec((tb, d), lambda b, c: (tok(b, c), 2 + di)),
                  pl.BlockSpec((1, 1, d), lambda b, c: (di, 0, 0)),
                  pl.BlockSpec((1, 1, d), lambda b, c: (di, 0, 0)),
                  pl.BlockSpec((1, 1, n_heads, dk, dk),
                               (lambda b, c: (b, di, 0, 0, 0)) if has_s0 else (lambda b, c: (0, 0, 0, 0, 0)))],
        out_specs=[pl.BlockSpec((tb, d), lambda b, c: (tok(b, c) - blk0, 0)),
                   pl.BlockSpec((1, 1, n_heads, dk, dk), lambda b, c: (b, 0, 0, 0, 0))],
        out_shape=[jax.ShapeDtypeStruct((n_seq * t, d), F32),
                   jax.ShapeDtypeStruct((n_seq, 1, n_heads, dk, dk), F32)],
        scratch_shapes=[pltpu.VMEM((n_heads, dk, dk), F32)],
        compiler_params=_params(("arbitrary", "arbitrary")),
        name="hgrn2_bw" if reverse else "hgrn2_fw",
    )(proj, proj, proj, b_f.reshape(2, 1, d), lb.reshape(2, 1, d), s0)


def _log_sigmoid(x):
    return jnp.minimum(x, 0.0) - jnp.log(1.0 + jnp.exp(-jnp.abs(x)))


def _mlstm_kernel(qf_ref, kf_ref, vf_ref, gf_ref, qb_ref, kb_ref, vb_ref, gb_ref, bg_ref, c0_ref, n0_ref, m0_ref,
                  of_ref, ob_ref, c_out, n_out, m_out, c_scr, n_scr, m_scr,
                  *, n_heads, dqk, dv, has_state, n_blk):
    c = pl.program_id(1)
    L = qf_ref.shape[0]

    @pl.when(c == 0)
    def _():
        if has_state:
            c_scr[...] = c0_ref[0]
            n_scr[...] = n0_ref[0]
            m_scr[...] = m0_ref[0]
        else:
            c_scr[...] = jnp.zeros_like(c_scr)
            n_scr[...] = jnp.zeros_like(n_scr)
            m_scr[...] = jnp.zeros_like(m_scr)

    row = lax.broadcasted_iota(I32, (L, L), 0)
    col = lax.broadcasted_iota(I32, (L, L), 1)
    lane = lax.broadcasted_iota(I32, (1, LANES), 1)
    heads = [(s, h) for s in range(2) for h in range(n_heads)]
    refs = [(qf_ref, kf_ref, vf_ref, gf_ref), (qb_ref, kb_ref, vb_ref, gb_ref)]
    causal_s, cum_s, cumt_s, gates_s, gatest_s, m_all = [], [], [], [], [], []
    for s in range(2):
        gates = GATE_SOFTCAP * jnp.tanh((refs[s][3][...] + bg_ref[...]) / GATE_SOFTCAP)
        causal = (col >= row) if s == 1 else (col <= row)
        cum = _fdot(jnp.where(causal, 1.0, 0.0).astype(F32), _log_sigmoid(gates))
        causal_s.append(causal)
        cum_s.append(cum)
        cumt_s.append(cum.T)
        gates_s.append(gates)
        gatest_s.append(gates.T)
        m_all.append(m_scr[s])
    i_off = [0, 2 * n_heads]
    f_off = [n_heads, 3 * n_heads]
    e_row = [L - 1, 0]
    c_prev = {it: c_scr[it[0], it[1]] for it in heads}
    n_prev = {it: n_scr[it[0], it[1]] for it in heads}
    qs = {(s, h): refs[s][0][:, h * dqk:(h + 1) * dqk] * (dqk ** -0.5) for s, h in heads}
    ks_ = {(s, h): refs[s][1][:, h * dqk:(h + 1) * dqk] for s, h in heads}
    vs = {(s, h): refs[s][2][:, h * dv:(h + 1) * dv] for s, h in heads}
    cum_c = {(s, h): cum_s[s][:, f_off[s] + h:f_off[s] + h + 1] for s, h in heads}
    cum_r = {(s, h): cumt_s[s][f_off[s] + h:f_off[s] + h + 1, :] for s, h in heads}
    i_c = {(s, h): gates_s[s][:, i_off[s] + h:i_off[s] + h + 1] for s, h in heads}
    i_r = {(s, h): gatest_s[s][i_off[s] + h:i_off[s] + h + 1, :] for s, h in heads}
    m_prev = {(s, h): m_all[s][0:1, h:h + 1] for s, h in heads}
    d = {h: jnp.where(causal_s[h[0]], cum_c[h] - cum_r[h] + i_r[h], -jnp.inf) for h in heads}
    m_inter = {h: cum_c[h] + m_prev[h] for h in heads}
    m_t = {h: jnp.maximum(m_inter[h], d[h].max(axis=-1, keepdims=True)) for h in heads}
    scores = {h: _bdot_nt(qs[h], ks_[h]) for h in heads}
    inter = {h: _bdot(qs[h], c_prev[h]) for h in heads}
    qn = {h: jnp.sum(qs[h] * n_prev[h], axis=-1, keepdims=True) for h in heads}
    w_inter = {h: jnp.exp(m_inter[h] - m_t[h]) for h in heads}
    qk = {h: scores[h] * jnp.exp(d[h] - m_t[h]) for h in heads}
    num = {h: w_inter[h] * inter[h] + _bdot(qk[h], vs[h]) for h in heads}
    den = {h: w_inter[h] * qn[h] + qk[h].sum(axis=-1, keepdims=True) for h in heads}
    outs = {h: num[h] / jnp.maximum(jnp.abs(den[h]), jnp.exp(-m_t[h])) for h in heads}
    end = {h: cum_c[h][e_row[h[0]]:e_row[h[0]] + 1, :] for h in heads}
    g_end_r = {h: end[h] - cum_r[h] + i_r[h] for h in heads}
    g_end_c = {h: end[h] - cum_c[h] + i_c[h] for h in heads}
    m_new = {h: jnp.maximum(end[h] + m_prev[h], g_end_r[h].max(axis=-1, keepdims=True)) for h in heads}
    w_old = {h: jnp.exp(end[h] + m_prev[h] - m_new[h]) for h in heads}
    kd = {h: ks_[h] * jnp.exp(g_end_c[h] - m_new[h]) for h in heads}
    c_new = {h: w_old[h] * c_prev[h] + _bdot_tn(kd[h], vs[h]) for h in heads}
    n_new = {h: w_old[h] * n_prev[h] + kd[h].sum(axis=0, keepdims=True) for h in heads}
    of_ref[...] = jnp.concatenate([outs[(0, h)] for h in range(n_heads)], axis=-1)
    ob_ref[...] = jnp.concatenate([outs[(1, h)] for h in range(n_heads)], axis=-1)
    for s in range(2):
        m_next = m_all[s]
        for h in range(n_heads):
            c_scr[s, h] = c_new[(s, h)]
            n_scr[s, h] = n_new[(s, h)]
            m_next = jnp.where(lane == h, m_new[(s, h)], m_next)
        m_scr[s] = m_next

    @pl.when(c == n_blk - 1)
    def _():
        c_out[0] = c_scr[...]
        n_out[0] = n_scr[...]
        m_out[0] = m_scr[...]


def _mlstm_call(proj, row0, n_seq, t, b_gates_pad, state):
    n_heads, dqk, dv = 8, 64, 128
    wq, wv = n_heads * dqk, n_heads * dv
    L = 256
    n_blk = t // L
    blk0 = row0 // L
    has_state = state is not None
    if has_state:
        c0, n0, m0 = state
        smap = lambda b, c: (b, 0, 0, 0, 0)
        mmap = lambda b, c: (b, 0, 0, 0)
    else:
        c0 = jnp.zeros((1, 2, n_heads, dqk, dv), F32)
        n0 = jnp.zeros((1, 2, n_heads, 1, dqk), F32)
        m0 = jnp.zeros((1, 2, 1, 128), F32)
        smap = lambda b, c: (0, 0, 0, 0, 0)
        mmap = lambda b, c: (0, 0, 0, 0)

    def tok(b, c, reverse):
        return blk0 + b * n_blk + ((n_blk - 1 - c) if reverse else c)

    body = functools.partial(_mlstm_kernel, n_heads=n_heads, dqk=dqk, dv=dv, has_state=has_state, n_blk=n_blk)
    gate_cb = (2 * wq + 2 * wv) // 128

    def token_specs(reverse):
        return [pl.BlockSpec((L, wq), lambda b, c: (tok(b, c, reverse), 0)),
                pl.BlockSpec((L, wq), lambda b, c: (tok(b, c, reverse), 1)),
                pl.BlockSpec((L, wv), lambda b, c: (tok(b, c, reverse), (2 * wq) // wv)),
                pl.BlockSpec((L, 128), lambda b, c: (tok(b, c, reverse), gate_cb))]

    return pl.pallas_call(
        body,
        grid=(n_seq, n_blk),
        in_specs=token_specs(False) + token_specs(True)
                 + [pl.BlockSpec((1, 128), lambda b, c: (0, 0)),
                    pl.BlockSpec((1, 2, n_heads, dqk, dv), smap),
                    pl.BlockSpec((1, 2, n_heads, 1, dqk), smap),
                    pl.BlockSpec((1, 2, 1, 128), mmap)],
        out_specs=[pl.BlockSpec((L, wv), lambda b, c: (tok(b, c, False) - blk0, 0)),
                   pl.BlockSpec((L, wv), lambda b, c: (tok(b, c, True) - blk0, 0)),
                   pl.BlockSpec((1, 2, n_heads, dqk, dv), lambda b, c: (b, 0, 0, 0, 0)),
                   pl.BlockSpec((1, 2, n_heads, 1, dqk), lambda b, c: (b, 0, 0, 0, 0)),
                   pl.BlockSpec((1, 2, 1, 128), lambda b, c: (b, 0, 0, 0))],
        out_shape=[jax.ShapeDtypeStruct((n_seq * t, wv), F32),
                   jax.ShapeDtypeStruct((n_seq * t, wv), F32),
                   jax.ShapeDtypeStruct((n_seq, 2, n_heads, dqk, dv), F32),
                   jax.ShapeDtypeStruct((n_seq, 2, n_heads, 1, dqk), F32),
                   jax.ShapeDtypeStruct((n_seq, 2, 1, 128), F32)],
        scratch_shapes=[pltpu.VMEM((2, n_heads, dqk, dv), F32),
                        pltpu.VMEM((2, n_heads, 1, dqk), F32),
                        pltpu.VMEM((2, 1, 128), F32)],
        compiler_params=_params(("arbitrary", "arbitrary")),
        name="mlstm",
    )(proj, proj, proj, proj, proj, proj, proj, proj, b_gates_pad, c0, n0, m0)


def _moe_input(x_ref, m_ref, g_ref):
    return _rms(x_ref[...], g_ref[...]) * (1.0 + m_ref[0, 4:5, :]) + m_ref[0, 3:4, :]


def _route_kernel(x_ref, m_ref, g_ref, wr_ref, br_ref, idx_ref, gate_ref, rank_ref, cnt_ref, carry_scr,
                  *, n_experts):
    i = pl.program_id(0)
    tm = x_ref.shape[0]

    @pl.when(i == 0)
    def _():
        carry_scr[...] = jnp.zeros_like(carry_scr)

    h = _moe_input(x_ref, m_ref, g_ref)
    w = wr_ref[...]
    h_hi, w_hi = h.astype(BF16), w.astype(BF16)
    h_lo = (h - h_hi.astype(F32)).astype(BF16)
    w_lo = (w - w_hi.astype(F32)).astype(BF16)
    logits = _bdot_nt(w_hi, h_hi) + _bdot_nt(w_lo, h_hi) + _bdot_nt(w_hi, h_lo) + br_ref[...]
    e_io = lax.broadcasted_iota(I32, (n_experts, tm), 0).astype(F32)
    work = logits
    vals, idxs = [], []
    chosen = jnp.zeros((n_experts, tm), F32)
    for _ in range(TOP_K):
        mx = work.max(axis=0, keepdims=True)
        ix = jnp.min(jnp.where(work == mx, e_io, float(n_experts)), axis=0, keepdims=True)
        hit = e_io == ix
        vals.append(mx)
        idxs.append(ix)
        chosen = jnp.where(hit, 1.0, chosen)
        work = jnp.where(hit, -jnp.inf, work)
    es = [jnp.exp(v - vals[0]) for v in vals]
    den = es[0] + es[1] + es[2] + es[3]
    srow = lax.broadcasted_iota(I32, (tm, tm), 0)
    scol = lax.broadcasted_iota(I32, (tm, tm), 1)
    before = jnp.where(srow < scol, 1.0, 0.0).astype(BF16)
    pos = jnp.dot(chosen.astype(BF16), before, preferred_element_type=F32) + carry_scr[...]
    ranks = [jnp.sum(jnp.where(e_io == ix, pos, 0.0), axis=0, keepdims=True) for ix in idxs]
    carry_scr[...] = carry_scr[...] + chosen.sum(axis=1, keepdims=True)
    idx_ref[...] = jnp.concatenate(idxs, axis=0).astype(I32)
    gate_ref[...] = jnp.concatenate([e / den for e in es], axis=0)
    rank_ref[...] = jnp.concatenate(ranks, axis=0).astype(I32)
    cnt_ref[...] = jnp.broadcast_to(carry_scr[...], cnt_ref.shape).astype(I32)


def _route_call(x, modsel, g, w_router, b_router):
    n, d = x.shape
    n_experts = w_router.shape[1]
    tm = TOKEN_TILE
    body = functools.partial(_route_kernel, n_experts=n_experts)
    return pl.pallas_call(
        body,
        grid=(n // tm,),
        in_specs=[pl.BlockSpec((tm, d), lambda i: (i, 0)),
                  pl.BlockSpec((1, 6, d), lambda i: (i, 0, 0)),
                  pl.BlockSpec((1, d), lambda i: (0, 0)),
                  pl.BlockSpec((n_experts, d), lambda i: (0, 0)),
                  pl.BlockSpec((n_experts, 1), lambda i: (0, 0))],
        out_specs=[pl.BlockSpec((TOP_K, tm), lambda i: (0, i)),
                   pl.BlockSpec((TOP_K, tm), lambda i: (0, i)),
                   pl.BlockSpec((TOP_K, tm), lambda i: (0, i)),
                   pl.BlockSpec((n_experts, 128), lambda i: (0, 0))],
        out_shape=[jax.ShapeDtypeStruct((TOP_K, n), I32),
                   jax.ShapeDtypeStruct((TOP_K, n), F32),
                   jax.ShapeDtypeStruct((TOP_K, n), I32),
                   jax.ShapeDtypeStruct((n_experts, 128), I32)],
        scratch_shapes=[pltpu.VMEM((n_experts, 1), F32)],
        compiler_params=_params(("arbitrary",)),
        name="moe_route",
    )(x, modsel, g.reshape(1, d), w_router.T, b_router.reshape(n_experts, 1))


def _slot_kernel(cnt_ref, idx_ref, rank_ref, dest_ref, binfo_ref, pad_ref, nused_ref, *, n_experts, n_blocks):
    cnt = cnt_ref[:, 0:1].astype(F32)
    padded = jnp.ceil(cnt * (1.0 / MOE_BLOCK)) * MOE_BLOCK
    er = lax.broadcasted_iota(I32, (n_experts, n_experts), 0)
    ec = lax.broadcasted_iota(I32, (n_experts, n_experts), 1)
    start_row = jnp.sum(jnp.where(er < ec, padded, 0.0), axis=0, keepdims=True)
    start_col = jnp.sum(jnp.where(er == ec, start_row, 0.0), axis=1, keepdims=True)
    end_col = start_col + padded
    idx = idx_ref[...]
    e_io = lax.broadcasted_iota(I32, (n_experts,) + idx.shape[1:], 0)
    rows = []
    for k in range(TOP_K):
        hit = e_io == idx[k:k + 1, :]
        rows.append(jnp.sum(jnp.where(hit, start_col, 0.0), axis=0, keepdims=True))
    dest_ref[...] = jnp.concatenate(rows, axis=0).astype(I32) + rank_ref[...]
    blk_start = (lax.broadcasted_iota(I32, (n_experts, n_blocks), 1) * MOE_BLOCK).astype(F32)
    n_done = jnp.sum(jnp.where(end_col <= blk_start, 1.0, 0.0), axis=0, keepdims=True)
    bexp = jnp.minimum(n_done, n_experts - 1.0)
    used_row = jnp.sum(jnp.where(er == ec, jnp.where(cnt > 0.0, 1.0, 0.0), 0.0), axis=0, keepdims=True)
    ecf = ec.astype(F32)
    next_col = jnp.min(jnp.where((ec > er) & (used_row > 0.0), ecf, float(n_experts)), axis=1, keepdims=True)
    ord_col = jnp.sum(jnp.where(ec < er, used_row, 0.0), axis=1, keepdims=True)
    par_col = ord_col - 2.0 * jnp.floor(ord_col * 0.5)
    mine = lax.broadcasted_iota(I32, (n_experts, n_blocks), 0).astype(F32) == bexp
    bnext = jnp.sum(jnp.where(mine, next_col, 0.0), axis=0, keepdims=True)
    bslot = jnp.sum(jnp.where(mine, par_col, 0.0), axis=0, keepdims=True)
    binfo_ref[...] = jnp.concatenate([bexp, bnext, bslot], axis=0).astype(I32)
    n_used = jnp.sum(padded, axis=0, keepdims=True) * (1.0 / MOE_BLOCK)
    nused_ref[...] = n_used.astype(I32)
    cnt_row = jnp.sum(jnp.where(er == ec, cnt, 0.0), axis=0, keepdims=True)
    padded_row = jnp.sum(jnp.where(er == ec, padded, 0.0), axis=0, keepdims=True)
    pad_ref[...] = jnp.concatenate([start_row + cnt_row, padded_row - cnt_row,
                                    jnp.broadcast_to(n_used, cnt_row.shape)], axis=0).astype(I32)


def _slot_call(counts, idx_t, rank_t, n_blocks):
    n_experts = counts.shape[0]
    n = idx_t.shape[1]
    tn = min(2048, n)
    body = functools.partial(_slot_kernel, n_experts=n_experts, n_blocks=n_blocks)
    return pl.pallas_call(
        body,
        grid=(n // tn,),
        in_specs=[pl.BlockSpec((n_experts, 128), lambda i: (0, 0)),
                  pl.BlockSpec((TOP_K, tn), lambda i: (0, i)),
                  pl.BlockSpec((TOP_K, tn), lambda i: (0, i))],
        out_specs=[pl.BlockSpec((TOP_K, tn), lambda i: (0, i)),
                   pl.BlockSpec((3, n_blocks), lambda i: (0, 0)),
                   pl.BlockSpec((3, n_experts), lambda i: (0, 0)),
                   pl.BlockSpec((1, 1), lambda i: (0, 0))],
        out_shape=[jax.ShapeDtypeStruct((TOP_K, n), I32),
                   jax.ShapeDtypeStruct((3, n_blocks), I32),
                   jax.ShapeDtypeStruct((3, n_experts), I32),
                   jax.ShapeDtypeStruct((1, 1), I32)],
        compiler_params=_params(("arbitrary",)),
        name="moe_slots",
    )(counts, idx_t, rank_t)


DMA_ISSUE_UNROLL = 8


def _to_row_tiles(ref, base, x):
    rows = x.shape[0]
    for c in range(ROW_SUBLANES):
        ref[pl.ds(base * ROW_SUBLANES + c, rows, stride=ROW_SUBLANES), :] = x[:, c * LANES:(c + 1) * LANES]


def _from_row_tiles(ref, base, rows, c):
    return ref[pl.ds(base * ROW_SUBLANES + c, rows, stride=ROW_SUBLANES), :]


def _row_tile(ref, r):
    return ref.at[pl.ds(pl.multiple_of(r * ROW_SUBLANES, ROW_SUBLANES), ROW_SUBLANES)]


def _zero_fill_padding(pad_ref, xs_ref, z_scr, sem):
    z_scr[...] = jnp.zeros_like(z_scr)
    n_experts = pad_ref.shape[1]
    bits = range(MOE_BLOCK.bit_length() - 2, -1, -1)

    def pieces(e):
        off, length = pad_ref[0, e], pad_ref[1, e]
        for bit in bits:
            size = 1 << bit
            done = (length >> (bit + 1)) << (bit + 1)
            copy = pltpu.make_async_copy(z_scr.at[pl.ds(0, size * ROW_SUBLANES)],
                                         xs_ref.at[pl.ds(pl.multiple_of((off + done) * ROW_SUBLANES, ROW_SUBLANES),
                                                         size * ROW_SUBLANES)], sem)
            yield (length & size) != 0, copy

    def tail_blocks():
        n_blocks = xs_ref.shape[0] // (MOE_BLOCK * ROW_SUBLANES)
        for b in range(n_blocks - n_experts, n_blocks):
            copy = pltpu.make_async_copy(z_scr, xs_ref.at[pl.ds(b * MOE_BLOCK * ROW_SUBLANES,
                                                                MOE_BLOCK * ROW_SUBLANES)], sem)
            yield b >= pad_ref[2, 0], copy

    def all_copies():
        for e in range(n_experts):
            yield from pieces(e)
        yield from tail_blocks()

    for needed, copy in all_copies():
        pl.when(needed)(copy.start)
    for needed, copy in all_copies():
        pl.when(needed)(copy.wait)


def _dispatch_kernel(pad_ref, dest_ref, x_ref, m_ref, g_ref, xs_ref, h_scr, z_scr, sem, *, n_tiles):
    i = pl.program_id(0)
    tm = x_ref.shape[0]

    @pl.when(i == 0)
    def _():
        _zero_fill_padding(pad_ref, xs_ref, z_scr, sem.at[2])

    slot = lax.rem(i, 2)
    src = h_scr.at[slot]
    _to_row_tiles(src, 0, _moe_input(x_ref, m_ref, g_ref))

    def start_row(r, carry):
        for k in range(TOP_K):
            pltpu.make_async_copy(_row_tile(src, r), _row_tile(xs_ref, dest_ref[0, 0, k * tm + r]),
                                  sem.at[slot]).start(priority=k % 2)
        return carry

    lax.fori_loop(0, tm, start_row, 0, unroll=DMA_ISSUE_UNROLL // TOP_K)

    def wait_tile(s):
        for _ in range(TOP_K):
            pltpu.make_async_copy(h_scr.at[s], xs_ref.at[pl.ds(0, tm * ROW_SUBLANES)], sem.at[s]).wait()

    @pl.when(i >= 1)
    def _():
        wait_tile(1 - slot)

    @pl.when(i == n_tiles - 1)
    def _():
        wait_tile(slot)


def _dispatch_call(pad_info, dest_tiles, x, modsel, g, n_slots):
    n, d = x.shape
    tm = TOKEN_TILE
    assert d == ROW_SUBLANES * LANES
    return pl.pallas_call(
        functools.partial(_dispatch_kernel, n_tiles=n // tm),
        grid=(n // tm,),
        in_specs=[pl.BlockSpec(memory_space=pltpu.SMEM),
                  pl.BlockSpec((1, 1, TOP_K * tm), lambda i: (i, 0, 0), memory_space=pltpu.SMEM),
                  pl.BlockSpec((tm, d), lambda i: (i, 0)),
                  pl.BlockSpec((1, 6, d), lambda i: (i, 0, 0)),
                  pl.BlockSpec((1, d), lambda i: (0, 0))],
        out_specs=pl.BlockSpec(memory_space=pl.ANY),
        out_shape=jax.ShapeDtypeStruct((n_slots * ROW_SUBLANES, LANES), F32),
        scratch_shapes=[pltpu.VMEM((2, tm * ROW_SUBLANES, LANES), F32),
                        pltpu.VMEM((MOE_BLOCK * ROW_SUBLANES, LANES), F32),
                        pltpu.SemaphoreType.DMA((3,))],
        compiler_params=_params(("arbitrary",)),
        name="moe_dispatch",
    )(pad_info, dest_tiles, x, modsel, g.reshape(1, d))


FFN_BLOCKS_PER_STEP = 4


def _ffn_kernel(binfo_ref, nused_ref, xs_ref, wgu_hbm, wdn_hbm, *rest, layer, n_experts, n_blocks):
    bias_refs = rest[:2 * FFN_BLOCKS_PER_STEP]
    ys_ref, wgu_f32, wdn_f32, wgu_scr, wdn_scr, sem = rest[2 * FFN_BLOCKS_PER_STEP:]
    d_ff = wdn_scr.shape[0]
    rows = MOE_BLOCK

    def weight_copies(e, slot):
        return (pltpu.make_async_copy(wgu_hbm.at[layer, e], wgu_f32.at[slot], sem.at[slot]),
                pltpu.make_async_copy(wdn_hbm.at[layer, e], wdn_f32.at[slot], sem.at[slot]))

    def one_block(u):
        b = pl.program_id(0) * FFN_BLOCKS_PER_STEP + u
        base = u * MOE_BLOCK
        bgu_ref, bdn_ref = bias_refs[2 * u], bias_refs[2 * u + 1]

        @pl.when(b < nused_ref[0])
        def _():
            e = binfo_ref[b]
            prev = binfo_ref[jnp.maximum(b - 1, 0)]
            slot = binfo_ref[2 * n_blocks + b]

            @pl.when((b == 0) | (e != prev))
            def _():
                @pl.when(b == 0)
                def _():
                    for cp in weight_copies(e, slot):
                        cp.start()

                nxt = binfo_ref[n_blocks + b]

                @pl.when(nxt < n_experts)
                def _():
                    for cp in weight_copies(nxt, 1 - slot):
                        cp.start()

                for cp in weight_copies(e, slot):
                    cp.wait()
                wgu_scr[...] = wgu_f32[slot].astype(BF16)
                wdn_scr[...] = wdn_f32[slot].astype(BF16)

            x = jnp.concatenate([_from_row_tiles(xs_ref, base, rows, c).astype(BF16)
                                 for c in range(ROW_SUBLANES)], axis=-1)
            gu = jnp.dot(x, wgu_scr[...], preferred_element_type=F32) + bgu_ref[0, 0]
            x_glu = jnp.minimum(gu[:, :d_ff], SWIGLU_LIMIT)
            x_lin = jnp.clip(gu[:, d_ff:], -SWIGLU_LIMIT, SWIGLU_LIMIT)
            hid = x_glu * _sigmoid(SWIGLU_ALPHA * x_glu) * (x_lin + 1.0)
            _to_row_tiles(ys_ref, base, jnp.dot(hid.astype(BF16), wdn_scr[...], preferred_element_type=F32)
                          + bdn_ref[0, 0])

        @pl.when(b >= nused_ref[0])
        def _():
            ys_ref[pl.ds(base * ROW_SUBLANES, rows * ROW_SUBLANES), :] = jnp.zeros(
                (rows * ROW_SUBLANES, LANES), F32)

    for u in range(FFN_BLOCKS_PER_STEP):
        one_block(u)


def _ffn_call(layer, block_info, n_used, xs, w_gu, b_gu, w_dn, b_dn):
    depth, n_experts, d, d_ff2 = w_gu.shape
    d_ff = d_ff2 // 2
    g = FFN_BLOCKS_PER_STEP
    step_rows = g * MOE_BLOCK * ROW_SUBLANES
    n_blocks = xs.shape[0] // (MOE_BLOCK * ROW_SUBLANES)
    assert n_blocks % g == 0

    def blk(b, nu):
        return jnp.maximum(jnp.minimum(b, nu[0] - 1), 0)

    def bias_specs(u):
        return [pl.BlockSpec((1, 1, 1, d_ff2), lambda s, bi, nu: (layer, bi[blk(g * s + u, nu)], 0, 0)),
                pl.BlockSpec((1, 1, 1, d), lambda s, bi, nu: (layer, bi[blk(g * s + u, nu)], 0, 0))]

    bias_args = [b_gu.reshape(depth, n_experts, 1, d_ff2), b_dn.reshape(depth, n_experts, 1, d)] * g
    grid_spec = pltpu.PrefetchScalarGridSpec(
        num_scalar_prefetch=2,
        grid=(n_blocks // g,),
        in_specs=[pl.BlockSpec((step_rows, LANES), lambda s, bi, nu: (blk(g * s, nu) // g, 0)),
                  pl.BlockSpec(memory_space=pl.ANY),
                  pl.BlockSpec(memory_space=pl.ANY)]
                 + [spec for u in range(g) for spec in bias_specs(u)],
        out_specs=pl.BlockSpec((step_rows, LANES), lambda s, bi, nu: (s, 0)),
        scratch_shapes=[pltpu.VMEM((2, d, d_ff2), F32), pltpu.VMEM((2, d_ff, d), F32),
                        pltpu.VMEM((d, d_ff2), BF16), pltpu.VMEM((d_ff, d), BF16),
                        pltpu.SemaphoreType.DMA((2,))],
    )
    body = functools.partial(_ffn_kernel, layer=layer, n_experts=n_experts, n_blocks=n_blocks)
    return pl.pallas_call(
        body,
        grid_spec=grid_spec,
        out_shape=jax.ShapeDtypeStruct(xs.shape, F32),
        compiler_params=_params(("arbitrary",)),
        name="moe_ffn",
    )(block_info, n_used, xs, w_gu, w_dn, *bias_args)


def _gather_expert_rows(dest_ref, dest_next_ref, ys_ref, gate_ref, buf, sem, n_tiles):
    i = pl.program_id(0)
    tm = gate_ref.shape[0]
    rows = TOP_K * tm

    def start_gathers(d_ref, slot):
        def start_pair(p, carry):
            for u in range(2):
                j = 2 * p + u
                pltpu.make_async_copy(_row_tile(ys_ref, d_ref[0, 0, j]), _row_tile(buf.at[slot], j),
                                      sem.at[slot]).start(priority=u)
            return carry

        lax.fori_loop(0, rows // 2, start_pair, 0, unroll=DMA_ISSUE_UNROLL // 2)

    slot = lax.rem(i, 2)

    @pl.when(i == 0)
    def _():
        start_gathers(dest_ref, 0)

    def wait_tile(s):
        pltpu.make_async_copy(ys_ref.at[pl.ds(0, rows * ROW_SUBLANES)], buf.at[s], sem.at[s]).wait()

    @pl.when(i + 1 < n_tiles)
    def _():
        start_gathers(dest_next_ref, 1 - slot)

    wait_tile(slot)
    cur = buf.at[slot]
    chunks = []
    for c in range(ROW_SUBLANES):
        y = gate_ref[:, 0:1] * _from_row_tiles(cur, 0, tm, c)
        for k in range(1, TOP_K):
            y = y + gate_ref[:, k:k + 1] * _from_row_tiles(cur, k * tm, tm, c)
        chunks.append(y)
    return jnp.concatenate(chunks, axis=-1)


def _combine_kernel(dest_ref, dest_next_ref, ys_ref, gate_ref, x_ref, m_ref, fg_ref, y_ref, buf, sem,
                    *, n_tiles):
    y = _gather_expert_rows(dest_ref, dest_next_ref, ys_ref, gate_ref, buf, sem, n_tiles)
    y_ref[...] = _rms(x_ref[...] + m_ref[0, 5:6, :] * y, fg_ref[...])


def _combine_inproj_kernel(dest_ref, dest_next_ref, ys_ref, gate_ref, x_ref, mp_ref, mc_ref, g_ref, w_ref,
                           x_out_ref, p_ref, buf, sem, *, n_tiles):
    y = _gather_expert_rows(dest_ref, dest_next_ref, ys_ref, gate_ref, buf, sem, n_tiles)
    x = x_ref[...] + mp_ref[0, 5:6, :] * y
    x_out_ref[...] = x
    h = _rms(x, g_ref[...]) * (1.0 + mc_ref[0, 1:2, :]) + mc_ref[0, 0:1, :]
    p_ref[...] = _bdot(h, w_ref[...])


def _moe_gather_specs(tm, n_tiles, d):
    return [pl.BlockSpec((1, 1, TOP_K * tm), lambda i: (i, 0, 0), memory_space=pltpu.SMEM),
            pl.BlockSpec((1, 1, TOP_K * tm), lambda i: (jnp.minimum(i + 1, n_tiles - 1), 0, 0),
                         memory_space=pltpu.SMEM),
            pl.BlockSpec(memory_space=pl.ANY),
            pl.BlockSpec((tm, TOP_K), lambda i: (i, 0)),
            pl.BlockSpec((tm, d), lambda i: (i, 0))]


def _moe_gather_scratch(tm):
    return [pltpu.VMEM((2, TOP_K * tm * ROW_SUBLANES, LANES), F32), pltpu.SemaphoreType.DMA((2,))]


def _combine_inproj_call(pending, x, modsel_prev, modsel, g, w):
    dest_tiles, ys, gates_nk = pending
    n, d = x.shape
    wout = w.shape[1]
    tm = TOKEN_TILE
    n_tiles = n // tm
    body = functools.partial(_combine_inproj_kernel, n_tiles=n_tiles)
    return pl.pallas_call(
        body,
        grid=(n_tiles,),
        in_specs=_moe_gather_specs(tm, n_tiles, d)
                 + [pl.BlockSpec((1, 6, d), lambda i: (i, 0, 0)),
                    pl.BlockSpec((1, 6, d), lambda i: (i, 0, 0)),
                    pl.BlockSpec((1, d), lambda i: (0, 0)),
                    pl.BlockSpec((d, wout), lambda i: (0, 0))],
        out_specs=[pl.BlockSpec((tm, d), lambda i: (i, 0)),
                   pl.BlockSpec((tm, wout), lambda i: (i, 0))],
        out_shape=[jax.ShapeDtypeStruct((n, d), F32),
                   jax.ShapeDtypeStruct((n, wout), F32)],
        scratch_shapes=_moe_gather_scratch(tm),
        compiler_params=_params(("arbitrary",)),
        name="moe_combine_inproj",
    )(dest_tiles, dest_tiles, ys, gates_nk, x, modsel_prev, modsel, g.reshape(1, d), w)


def _combine_call(pending, x, modsel, final_g):
    dest_tiles, ys, gates_nk = pending
    n, d = x.shape
    tm = TOKEN_TILE
    n_tiles = n // tm
    body = functools.partial(_combine_kernel, n_tiles=n_tiles)
    return pl.pallas_call(
        body,
        grid=(n_tiles,),
        in_specs=_moe_gather_specs(tm, n_tiles, d)
                 + [pl.BlockSpec((1, 6, d), lambda i: (i, 0, 0)),
                    pl.BlockSpec((1, d), lambda i: (0, 0))],
        out_specs=pl.BlockSpec((tm, d), lambda i: (i, 0)),
        out_shape=jax.ShapeDtypeStruct((n, d), F32),
        scratch_shapes=_moe_gather_scratch(tm),
        compiler_params=_params(("arbitrary",)),
        name="moe_combine",
    )(dest_tiles, dest_tiles, ys, gates_nk, x, modsel, final_g.reshape(1, d))


def _moe_experts(layer, x, modsel, g2, w_router, b_router, w_gu, b_gu, w_dn, b_dn):
    n, d = x.shape
    n_experts = w_router.shape[1]
    tm = TOKEN_TILE
    n_blocks = (n * TOP_K) // MOE_BLOCK + n_experts
    idx_t, gate_t, rank_t, counts = _route_call(x, modsel, g2, w_router, b_router)
    dest_t, block_info, pad_info, n_used = _slot_call(counts, idx_t, rank_t, n_blocks)
    dest_tiles = dest_t.reshape(TOP_K, n // tm, tm).transpose(1, 0, 2).reshape(n // tm, 1, TOP_K * tm)
    xs = _dispatch_call(pad_info, dest_tiles, x, modsel, g2, n_blocks * MOE_BLOCK)
    ys = _ffn_call(layer, block_info.reshape(3 * n_blocks), n_used.reshape(1), xs, w_gu, b_gu, w_dn, b_dn)
    return dest_tiles, ys, gate_t.T


def _rope_tables(t, hd):
    pos = np.arange(t)
    n_freq = hd // 4
    inv_freq = ROPE_THETA ** (-np.arange(n_freq, dtype=np.float32) / n_freq)
    ang = np.concatenate([(pos // GRID_W).astype(np.float32)[:, None] * inv_freq,
                          (pos % GRID_W).astype(np.float32)[:, None] * inv_freq], axis=-1)
    ang = jnp.asarray(ang, F32)
    cos, sin = jnp.cos(ang), jnp.sin(ang)
    return jnp.concatenate([cos, cos], axis=-1), jnp.concatenate([-sin, sin], axis=-1)


def kernel(x_prompt, x_sample, cache_k_a, cache_v_a, state_b, state_c_C, state_c_n, state_c_m, cache_k_d, cache_v_d, c, c_ctx, norm1_g, norm2_g, w_mod, b_mod, w_in_a, qnorm_a, knorm_a, w_out_a, w_in_b, b_f_b, lower_bounds_b, onorm_b, w_out_b, w_in_c, b_gates_c, onorm_c, w_out_c, w_in_d, rpb_d, w_out_d, w_router, b_router, w_gu, b_gu, w_dn, b_dn, final_g):
    n_ctx_seq, t_ctx, d = x_prompt.shape
    n_lat_seq, t_lat, _ = x_sample.shape
    depth = w_mod.shape[0]
    n_ctx = n_ctx_seq * t_ctx
    n_lat = n_lat_seq * t_lat
    n = n_ctx + n_lat
    tm = TOKEN_TILE
    assert t_ctx % tm == 0 and t_lat % tm == 0 and n_lat_seq + 1 <= 8
    assert n_ctx % PROJ_TILE == 0 and t_lat % PROJ_TILE == 0

    lb_cum = jnp.cumsum(jax.nn.softmax(lower_bounds_b.astype(F32), axis=0), axis=0)
    lb_all = lb_cum - lb_cum[0]

    cond8 = jnp.zeros((8, d), F32).at[0].set(c_ctx).at[1:1 + n_lat_seq].set(c)
    mod = _mod_call(cond8, w_mod, b_mod)
    tile_row = np.concatenate([np.zeros(n_ctx // tm, np.int32),
                               1 + np.repeat(np.arange(n_lat_seq, dtype=np.int32), t_lat // tm)])

    x = jnp.concatenate([x_prompt.reshape(n_ctx, d), x_sample.reshape(n_lat, d)], axis=0)
    outs = {}
    pending = None
    modsel = None

    def inproj(x, modsel, g, w):
        if pending is None:
            return x, _inproj_call(x, modsel, g, w)
        return _combine_inproj_call(pending, x, modsel_prev, modsel, g, w)

    for i in range(depth):
        kind = i % 4
        j = i // 4
        modsel_prev = modsel
        modsel = mod[i].reshape(8, 6, d)[tile_row]
        if kind == 0:
            x, proj = inproj(x, modsel, norm1_g[i],w_in_a[j].astype(BF16))
            o_ctx, k_new = _gqa_ctx_call(proj, qnorm_a[j], knorm_a[j], n_ctx_seq, t_ctx)
            cosd, sind = _rope_tables(t_lat, 128)
            o_lat = _gqa_lat_call(proj, n_ctx, n_lat_seq, t_lat,
                                  cache_k_a[:, j].reshape(n_lat_seq, -1, 256),
                                  cache_v_a[:, j].reshape(n_lat_seq, -1, 256), cosd, sind,
                                  qnorm_a[j], knorm_a[j])
            outs["k_a"] = k_new.reshape(n_ctx_seq, 1, t_ctx, 2, 128)
            outs["v_a"] = proj[:n_ctx, 1280:1536].reshape(n_ctx_seq, 1, t_ctx, 2, 128)
            x = _outproj_call("plain", [(o_ctx, o_lat)], None, w_out_a[j].astype(BF16), x, modsel)
        elif kind == 1:
            x, proj = inproj(x, modsel, norm1_g[i],w_in_b[j].astype(BF16))
            o_dirs, s_dirs = [], []
            for reverse in (False, True):
                oc, sc = _hgrn_call(proj, 0, n_ctx_seq, t_ctx, b_f_b[j], lb_all[i], None, reverse)
                ol, _ = _hgrn_call(proj, n_ctx, n_lat_seq, t_lat, b_f_b[j], lb_all[i], state_b[:, j], reverse)
                o_dirs.append((oc, ol))
                s_dirs.append(sc)
            outs["s_b"] = jnp.concatenate(s_dirs, axis=1)[:, None]
            x = _outproj_call("hgrn", o_dirs, (proj, 4, onorm_b[j].reshape(1, 128)),
                              w_out_b[j].astype(BF16), x, modsel)
        elif kind == 2:
            w_c = jnp.pad(w_in_c[j], ((0, 0), (0, 128 - 32))).astype(BF16)
            bg = jnp.pad(b_gates_c[j].reshape(1, 32), ((0, 0), (0, 128 - 32)))
            x, proj = inproj(x, modsel, norm1_g[i],w_c)
            state = (state_c_C[:, j], state_c_n[:, j][:, :, :, None, :],
                     jnp.pad(state_c_m[:, j], ((0, 0), (0, 0), (0, 120)))[:, :, None, :])
            ocf, ocb, cc, nc, mc = _mlstm_call(proj, 0, n_ctx_seq, t_ctx, bg, None)
            olf, olb, _, _, _ = _mlstm_call(proj, n_ctx, n_lat_seq, t_lat, bg, state)
            o_dirs = [(ocf, olf), (ocb, olb)]
            outs["c_C"] = cc[:, None]
            outs["c_n"] = nc[:, None, :, :, 0, :]
            outs["c_m"] = mc[:, None, :, 0, :8]
            x = _outproj_call("mlstm", o_dirs, (proj, 2, onorm_c[j].reshape(1, 128)),
                              w_out_c[j].astype(BF16), x, modsel)
        else:
            x, proj = inproj(x, modsel, norm1_g[i],w_in_d[j].astype(BF16))
            o_ctx = _mha_ctx_call(proj, n_ctx_seq, t_ctx)
            tz = _na_bias_call(rpb_d[j])
            o_lat = _na_call(proj, n_ctx, n_lat_seq, t_lat,
                             cache_k_d[:, j].reshape(n_lat_seq, -1, d),
                             cache_v_d[:, j].reshape(n_lat_seq, -1, d), tz)
            outs["k_d"] = proj[:n_ctx, d:2 * d].reshape(n_ctx_seq, 1, t_ctx, 16, 64)
            outs["v_d"] = proj[:n_ctx, 2 * d:3 * d].reshape(n_ctx_seq, 1, t_ctx, 16, 64)
            x = _outproj_call("plain", [(o_ctx, o_lat)], None, w_out_d[j].astype(BF16), x, modsel)
        pending = _moe_experts(i, x, modsel, norm2_g[i], w_router[i], b_router[i], w_gu, b_gu, w_dn, b_dn)
    x = _combine_call(pending, x, modsel, final_g)

    y_prompt = x[:n_ctx].reshape(n_ctx_seq, t_ctx, d)
    y_sample = x[n_ctx:].reshape(n_lat_seq, t_lat, d)
    return (y_prompt, y_sample, outs["k_a"], outs["v_a"], outs["s_b"], outs["c_C"], outs["c_n"], outs["c_m"],
            outs["k_d"], outs["v_d"])
```

```python
import functools

import numpy as np
import jax
import jax.numpy as jnp
from jax import lax
from jax.experimental import pallas as pl
from jax.experimental.pallas import tpu as pltpu

F32 = jnp.float32
BF16 = jnp.bfloat16
I32 = jnp.int32

NORM_EPS = 1e-6
GRID_W = 64
ROPE_THETA = 10000.0
TOP_K = 4
HGRN_CHUNK = 128
GATE_SOFTCAP = 15.0
NA_ROWS = 8
NA_COLS = 16
SWIGLU_ALPHA = 1.702
SWIGLU_LIMIT = 7.0
NEG_BIG = -1e30

LANES = 128
ROW_SUBLANES = 8
TOKEN_TILE = 256
PROJ_TILE = 512
MOE_BLOCK = 256
ATTN_HEAD_GROUP = 8
V7X_VMEM_LIMIT = 52 * 1024 * 1024


def _params(sem, vmem=V7X_VMEM_LIMIT):
    return pltpu.CompilerParams(dimension_semantics=sem, vmem_limit_bytes=vmem)


def _bdot(a, b):
    return jnp.dot(a.astype(BF16), b.astype(BF16), preferred_element_type=F32)


def _bdot_nt(a, b):
    return lax.dot_general(a.astype(BF16), b.astype(BF16), (((1,), (1,)), ((), ())),
                           preferred_element_type=F32)


def _bdot_tn(a, b):
    return lax.dot_general(a.astype(BF16), b.astype(BF16), (((0,), (0,)), ((), ())),
                           preferred_element_type=F32)


def _fdot(a, b):
    return jnp.dot(a, b, preferred_element_type=F32, precision=lax.Precision.HIGHEST)


def _rms(x, g):
    return x * lax.rsqrt(jnp.mean(x * x, axis=-1, keepdims=True) + NORM_EPS) * g


def _rms_heads(x, g, n_heads, hd):
    return jnp.concatenate([_rms(x[:, h * hd:(h + 1) * hd], g) for h in range(n_heads)], axis=-1)


def _sigmoid(x):
    return 1.0 / (1.0 + jnp.exp(-x))


def _silu(x):
    return x * _sigmoid(x)


def _softmax_rows(parts):
    m = parts[0].max(axis=-1, keepdims=True)
    for p in parts[1:]:
        m = jnp.maximum(m, p.max(axis=-1, keepdims=True))
    es = [jnp.exp(p - m) for p in parts]
    den = es[0].sum(axis=-1, keepdims=True)
    for e in es[1:]:
        den = den + e.sum(axis=-1, keepdims=True)
    return es, den


def _mod_kernel(c_ref, w_ref, b_ref, o_ref):
    o_ref[0] = _bdot(_silu(c_ref[...]), w_ref[0]) + b_ref[0]


def _mod_call(cond8, w_mod, b_mod):
    depth, d, d6 = w_mod.shape
    tn = 1024
    return pl.pallas_call(
        _mod_kernel,
        grid=(depth, d6 // tn),
        in_specs=[pl.BlockSpec((8, d), lambda i, j: (0, 0)),
                  pl.BlockSpec((1, d, tn), lambda i, j: (i, 0, j)),
                  pl.BlockSpec((1, 1, tn), lambda i, j: (i, 0, j))],
        out_specs=pl.BlockSpec((1, 8, tn), lambda i, j: (i, 0, j)),
        out_shape=jax.ShapeDtypeStruct((depth, 8, d6), F32),
        compiler_params=_params(("arbitrary", "arbitrary")),
        name="adaln_mod",
    )(cond8, w_mod, b_mod.reshape(depth, 1, d6))


def _inproj_kernel(x_ref, m_ref, g_ref, w_ref, o_ref):
    h = _rms(x_ref[...], g_ref[...]) * (1.0 + m_ref[0, 1:2, :]) + m_ref[0, 0:1, :]
    o_ref[...] = _bdot(h, w_ref[...])


def _inproj_call(x, modsel, g, w):
    n, d = x.shape
    wout = w.shape[1]
    tm = PROJ_TILE
    return pl.pallas_call(
        _inproj_kernel,
        grid=(n // tm,),
        in_specs=[pl.BlockSpec((tm, d), lambda i: (i, 0)),
                  pl.BlockSpec((1, 6, d), lambda i: (i * (PROJ_TILE // TOKEN_TILE), 0, 0)),
                  pl.BlockSpec((1, d), lambda i: (0, 0)),
                  pl.BlockSpec((d, wout), lambda i: (0, 0))],
        out_specs=pl.BlockSpec((tm, wout), lambda i: (i, 0)),
        out_shape=jax.ShapeDtypeStruct((n, wout), F32),
        compiler_params=_params(("arbitrary",)),
        name="inproj",
    )(x, modsel, g.reshape(1, d), w)


def _outproj_kernel(*refs, kind, n_pairs, n_ctx_tiles, n_heads, hd):
    in_ctx = pl.program_id(0) < n_ctx_tiles
    mix = [jnp.where(in_ctx, refs[2 * p][...], refs[2 * p + 1][...]) for p in range(n_pairs)]
    rest = refs[2 * n_pairs:]
    if kind == "plain":
        w_ref, x_ref, m_ref, y_ref = rest
        o = mix[0]
    else:
        gate_ref, on_ref, w_ref, x_ref, m_ref, y_ref = rest
        o = _rms_heads(mix[0] + mix[1], on_ref[...], n_heads, hd)
        o = o * _silu(gate_ref[...]) if kind == "hgrn" else _sigmoid(gate_ref[...]) * o
    y_ref[...] = x_ref[...] + m_ref[0, 2:3, :] * _bdot(o, w_ref[...])


def _outproj_call(kind, pairs, gate, w, x, modsel):
    n, d = x.shape
    tm = PROJ_TILE
    n_ctx_tiles = pairs[0][0].shape[0] // tm
    specs, args = [], []
    for a_ctx, a_lat in pairs:
        specs += [pl.BlockSpec((tm, d), lambda i: (jnp.minimum(i, n_ctx_tiles - 1), 0)),
                  pl.BlockSpec((tm, d), lambda i: (jnp.maximum(i - n_ctx_tiles, 0), 0))]
        args += [a_ctx, a_lat]
    if gate is not None:
        proj, cb, on = gate
        specs += [pl.BlockSpec((tm, d), lambda i: (i, cb)), pl.BlockSpec(on.shape, lambda i: (0, 0))]
        args += [proj, on]
    specs += [pl.BlockSpec(w.shape, lambda i: (0, 0)),
              pl.BlockSpec((tm, d), lambda i: (i, 0)),
              pl.BlockSpec((1, 6, d), lambda i: (i * (PROJ_TILE // TOKEN_TILE), 0, 0))]
    args += [w, x, modsel]
    body = functools.partial(_outproj_kernel, kind=kind, n_pairs=len(pairs), n_ctx_tiles=n_ctx_tiles,
                             n_heads=8, hd=128)
    return pl.pallas_call(
        body,
        grid=(n // tm,),
        in_specs=specs,
        out_specs=pl.BlockSpec((tm, d), lambda i: (i, 0)),
        out_shape=jax.ShapeDtypeStruct((n, d), F32),
        compiler_params=_params(("arbitrary",)),
        name="outproj_" + kind,
    )(*args)


def _gqa_ctx_kernel(p_ref, qn_ref, kn_ref, o_ref, k_ref, *, n_heads, n_kv, hd):
    rep = n_heads // n_kv
    scale = hd ** -0.5
    koff = n_heads * hd
    voff = koff + n_kv * hd
    ks = [_rms(p_ref[:, koff + g * hd: koff + (g + 1) * hd], kn_ref[...]) for g in range(n_kv)]
    k_ref[...] = jnp.concatenate(ks, axis=-1)
    for h in range(n_heads):
        g = h // rep
        q = _rms(p_ref[:, h * hd:(h + 1) * hd], qn_ref[...])
        s = _bdot_nt(q, ks[g]) * scale
        (e,), den = _softmax_rows([s])
        o_ref[:, h * hd:(h + 1) * hd] = _bdot(e / den, p_ref[:, voff + g * hd: voff + (g + 1) * hd])


def _gqa_ctx_call(proj, qn, kn, n_seq, t):
    n_heads, n_kv, hd = 8, 2, 128
    win = proj.shape[1]
    body = functools.partial(_gqa_ctx_kernel, n_heads=n_heads, n_kv=n_kv, hd=hd)
    return pl.pallas_call(
        body,
        grid=(n_seq,),
        in_specs=[pl.BlockSpec((t, win), lambda b: (b, 0)),
                  pl.BlockSpec((1, hd), lambda b: (0, 0)),
                  pl.BlockSpec((1, hd), lambda b: (0, 0))],
        out_specs=[pl.BlockSpec((t, n_heads * hd), lambda b: (b, 0)),
                   pl.BlockSpec((t, n_kv * hd), lambda b: (b, 0))],
        out_shape=[jax.ShapeDtypeStruct((n_seq * t, n_heads * hd), F32),
                   jax.ShapeDtypeStruct((n_seq * t, n_kv * hd), F32)],
        compiler_params=_params(("arbitrary",)),
        name="gqa_ctx",
    )(proj, qn.reshape(1, hd), kn.reshape(1, hd))


def _rope(x, cosd, sind):
    return x * cosd + pltpu.roll(x, x.shape[-1] // 2, 1) * sind


def _gqa_lat_kernel(pq_ref, pkv_ref, kc_ref, vc_ref, cq_ref, sq_ref, ck_ref, sk_ref, qn_ref, kn_ref,
                    o_ref, k_scr, v_scr, *, n_heads, n_kv, hd, t_ctx):
    rep = n_heads // n_kv
    scale = hd ** -0.5

    @pl.when(pl.program_id(1) == 0)
    def _():
        k_scr[0:t_ctx, :] = kc_ref[0].astype(BF16)
        v_scr[...] = jnp.ones_like(v_scr)
        for g in range(n_kv):
            k = _rms(pkv_ref[:, g * hd:(g + 1) * hd], kn_ref[...])
            k_scr[t_ctx:, g * hd:(g + 1) * hd] = _rope(k, ck_ref[...], sk_ref[...]).astype(BF16)
            v_scr[0:t_ctx, 2 * g * hd:(2 * g + 1) * hd] = vc_ref[0, :, g * hd:(g + 1) * hd].astype(BF16)
            v_scr[t_ctx:, 2 * g * hd:(2 * g + 1) * hd] = pkv_ref[:, (n_kv + g) * hd:(n_kv + g + 1) * hd].astype(BF16)

    qs = [(_rope(_rms(pq_ref[:, h * hd:(h + 1) * hd], qn_ref[...]), cq_ref[...], sq_ref[...]) * scale).astype(BF16)
          for h in range(n_heads)]
    for h0 in range(0, n_heads, ATTN_HEAD_GROUP):
        hs = range(h0, min(h0 + ATTN_HEAD_GROUP, n_heads))
        s = [_bdot_nt(qs[h], k_scr[:, (h // rep) * hd:(h // rep + 1) * hd]) for h in hs]
        e = [jnp.exp(x - x.max(axis=-1, keepdims=True)).astype(BF16) for x in s]
        pv = [jnp.dot(e[i], v_scr[:, 2 * (h // rep) * hd:2 * (h // rep + 1) * hd], preferred_element_type=F32)
              for i, h in enumerate(hs)]
        for i, h in enumerate(hs):
            o_ref[:, h * hd:(h + 1) * hd] = pv[i][:, :hd] / pv[i][:, hd:]


def _gqa_lat_call(proj, row0, n_seq, t, cache_k, cache_v, cosd, sind, qn, kn):
    n_heads, n_kv, hd = 8, 2, 128
    tq = 256
    t_ctx = cache_k.shape[1]
    nq = t // tq
    kvw = 2 * n_kv * hd
    qblk0 = row0 // tq
    sblk0 = row0 // t
    body = functools.partial(_gqa_lat_kernel, n_heads=n_heads, n_kv=n_kv, hd=hd, t_ctx=t_ctx)
    return pl.pallas_call(
        body,
        grid=(n_seq, nq),
        in_specs=[pl.BlockSpec((tq, n_heads * hd), lambda b, i: (qblk0 + b * nq + i, 0)),
                  pl.BlockSpec((t, kvw), lambda b, i: (sblk0 + b, (n_heads * hd) // kvw)),
                  pl.BlockSpec((1, t_ctx, n_kv * hd), lambda b, i: (b, 0, 0)),
                  pl.BlockSpec((1, t_ctx, n_kv * hd), lambda b, i: (b, 0, 0)),
                  pl.BlockSpec((tq, hd), lambda b, i: (i, 0)),
                  pl.BlockSpec((tq, hd), lambda b, i: (i, 0)),
                  pl.BlockSpec((t, hd), lambda b, i: (0, 0)),
                  pl.BlockSpec((t, hd), lambda b, i: (0, 0)),
                  pl.BlockSpec((1, hd), lambda b, i: (0, 0)),
                  pl.BlockSpec((1, hd), lambda b, i: (0, 0))],
        out_specs=pl.BlockSpec((tq, n_heads * hd), lambda b, i: (b * nq + i, 0)),
        out_shape=jax.ShapeDtypeStruct((n_seq * t, n_heads * hd), F32),
        scratch_shapes=[pltpu.VMEM((t_ctx + t, n_kv * hd), BF16),
                        pltpu.VMEM((t_ctx + t, 2 * n_kv * hd), BF16)],
        compiler_params=_params(("arbitrary", "arbitrary")),
        name="gqa_latent",
    )(proj, proj, cache_k, cache_v, cosd, sind, cosd, sind, qn.reshape(1, hd), kn.reshape(1, hd))


def _mha_ctx_kernel(q_ref, k_ref, v_ref, o_ref, *, hd):
    scale = hd ** -0.5
    n_heads = q_ref.shape[1] // hd
    ones = jnp.ones((v_ref.shape[0], hd), BF16)
    for h0 in range(0, n_heads, ATTN_HEAD_GROUP):
        sls = [slice(h * hd, (h + 1) * hd) for h in range(h0, min(h0 + ATTN_HEAD_GROUP, n_heads))]
        s = [_bdot_nt(q_ref[:, sl] * scale, k_ref[:, sl]) for sl in sls]
        e = [jnp.exp(x - x.max(axis=-1, keepdims=True)).astype(BF16) for x in s]
        pv = [jnp.dot(e[i], jnp.concatenate([v_ref[:, sl].astype(BF16), ones], axis=-1),
                      preferred_element_type=F32) for i, sl in enumerate(sls)]
        for i, sl in enumerate(sls):
            o_ref[:, sl] = pv[i][:, :hd] / pv[i][:, hd:]


def _mha_ctx_call(proj, n_seq, t):
    hd = 64
    d = proj.shape[1] // 3
    body = functools.partial(_mha_ctx_kernel, hd=hd)
    return pl.pallas_call(
        body,
        grid=(n_seq,),
        in_specs=[pl.BlockSpec((t, d), lambda b: (b, 0)),
                  pl.BlockSpec((t, d), lambda b: (b, 1)),
                  pl.BlockSpec((t, d), lambda b: (b, 2))],
        out_specs=pl.BlockSpec((t, d), lambda b: (b, 0)),
        out_shape=jax.ShapeDtypeStruct((n_seq * t, d), F32),
        compiler_params=_params(("arbitrary",)),
        name="mha_ctx",
    )(proj, proj, proj)


def _na_bias_kernel(rpb_ref, o_ref, *, n_rel_rows, n_rel_cols):
    h = pl.program_id(0)
    w_io = lax.broadcasted_iota(I32, (GRID_W, 2 * GRID_W), 0)
    lane = lax.broadcasted_iota(I32, (GRID_W, 2 * GRID_W), 1)
    ck = jnp.where(lane < GRID_W, lane, lane - GRID_W)
    c_start = jnp.clip(w_io - NA_COLS // 2, 0, GRID_W - NA_COLS)
    in_win = (ck >= c_start) & (ck < c_start + NA_COLS)
    rel = ck - w_io + (NA_COLS - 1)
    base = h * (n_rel_rows * n_rel_cols)
    tiles = []
    for j in range(n_rel_rows):
        acc = jnp.zeros((GRID_W, 2 * GRID_W), F32)
        for jj in range(n_rel_cols):
            acc = jnp.where(rel == jj, rpb_ref[base + j * n_rel_cols + jj], acc)
        tiles.append(jnp.where(in_win, acc, NEG_BIG))
    for j in range(n_rel_rows):
        hi = tiles[j + 1] if j + 1 < n_rel_rows else jnp.full((GRID_W, 2 * GRID_W), NEG_BIG, F32)
        o_ref[0, j] = jnp.where(lane < GRID_W, tiles[j], hi)


def _na_bias_call(rpb):
    n_heads, nrr, nrc = rpb.shape
    body = functools.partial(_na_bias_kernel, n_rel_rows=nrr, n_rel_cols=nrc)
    return pl.pallas_call(
        body,
        grid=(n_heads,),
        in_specs=[pl.BlockSpec(memory_space=pltpu.SMEM)],
        out_specs=pl.BlockSpec((1, nrr, GRID_W, 2 * GRID_W), lambda h: (h, 0, 0, 0)),
        out_shape=jax.ShapeDtypeStruct((n_heads, nrr, GRID_W, 2 * GRID_W), F32),
        compiler_params=_params(("arbitrary",)),
        name="na_bias",
    )(rpb.reshape(-1))


NA_QROWS = 4
NA_KROWS = 12


def _na_kernel(q_ref, k0_ref, k1_ref, k2_ref, v0_ref, v1_ref, v2_ref, kc_ref, vc_ref, tz_ref, o_ref,
               *, hd, n_grid_rows):
    scale = hd ** -0.5
    blk = pl.program_id(1)
    kstart = jnp.clip(blk * NA_QROWS - NA_ROWS // 2, 0, n_grid_rows - NA_KROWS)
    lane = lax.broadcasted_iota(I32, (GRID_W, 2 * GRID_W), 1)
    n_rel = tz_ref.shape[1]
    heads = range(q_ref.shape[1] // hd)
    sls = [slice(hh * hd, (hh + 1) * hd) for hh in heads]
    rel, pen = [], []
    for rq_l in range(NA_QROWS):
        rq = blk * NA_QROWS + rq_l
        r_start = jnp.clip(rq - NA_ROWS // 2, 0, n_grid_rows - NA_ROWS)
        rel.append([])
        pen.append([])
        for m in range(NA_KROWS // 2):
            rk = kstart + 2 * m
            rel[-1].append(jnp.clip(rk - rq + (NA_ROWS - 1), 0, n_rel - 1))
            ok0 = (rk >= r_start) & (rk < r_start + NA_ROWS)
            ok1 = (rk + 1 >= r_start) & (rk + 1 < r_start + NA_ROWS)
            pen[-1].append(jnp.where(lane < GRID_W, jnp.where(ok0, 0.0, NEG_BIG), jnp.where(ok1, 0.0, NEG_BIG)))
    bias = [jnp.concatenate([jnp.concatenate([tz_ref[hh, pl.ds(rel[r][m], 1)][0] + pen[r][m]
                                              for m in range(NA_KROWS // 2)], axis=-1)
                             for r in range(NA_QROWS)], axis=0) for hh in heads]
    q = [q_ref[:, sl] * scale for sl in sls]
    k_loc = [jnp.concatenate([k0_ref[:, sl], k1_ref[:, sl], k2_ref[:, sl]], axis=0) for sl in sls]
    v_loc = [jnp.concatenate([v0_ref[:, sl], v1_ref[:, sl], v2_ref[:, sl]], axis=0).astype(BF16) for sl in sls]
    v_ctx = [vc_ref[0, :, sl].astype(BF16) for sl in sls]
    s_loc = [_bdot_nt(q[h], k_loc[h]) + bias[h] for h in heads]
    s_ctx = [_bdot_nt(q[h], kc_ref[0, :, sls[h]]) for h in heads]
    m = [jnp.maximum(s_loc[h].max(axis=-1, keepdims=True), s_ctx[h].max(axis=-1, keepdims=True)) for h in heads]
    e_loc = [jnp.exp(s_loc[h] - m[h]).astype(BF16) for h in heads]
    e_ctx = [jnp.exp(s_ctx[h] - m[h]).astype(BF16) for h in heads]
    pv = [jnp.dot(e_loc[h], jnp.concatenate([v_loc[h], jnp.ones_like(v_loc[h])], axis=-1),
                  preferred_element_type=F32)
          + jnp.dot(e_ctx[h], jnp.concatenate([v_ctx[h], jnp.ones_like(v_ctx[h])], axis=-1),
                    preferred_element_type=F32) for h in heads]
    for h in heads:
        o_ref[:, sls[h]] = pv[h][:, :hd] / pv[h][:, hd:]


def _na_call(proj, row0, n_seq, t, cache_k, cache_v, tz):
    hd, cw = 64, 512
    d = proj.shape[1] // 3
    ncb = d // cw
    tq = NA_QROWS * GRID_W
    nq = t // tq
    n_grid_rows = t // GRID_W
    t_ctx = cache_k.shape[1]
    qblk0 = row0 // tq

    def kv_map(which, j):
        def index_map(b, i, c):
            ks = jnp.clip(i * NA_QROWS - NA_ROWS // 2, 0, n_grid_rows - NA_KROWS) // NA_QROWS
            return (qblk0 + b * nq + ks + j, which * ncb + c)
        return index_map

    body = functools.partial(_na_kernel, hd=hd, n_grid_rows=n_grid_rows)
    return pl.pallas_call(
        body,
        grid=(n_seq, nq, ncb),
        in_specs=[pl.BlockSpec((tq, cw), lambda b, i, c: (qblk0 + b * nq + i, c))]
                 + [pl.BlockSpec((tq, cw), kv_map(1, j)) for j in range(3)]
                 + [pl.BlockSpec((tq, cw), kv_map(2, j)) for j in range(3)]
                 + [pl.BlockSpec((1, t_ctx, cw), lambda b, i, c: (b, 0, c)),
                    pl.BlockSpec((1, t_ctx, cw), lambda b, i, c: (b, 0, c)),
                    pl.BlockSpec((cw // hd,) + tz.shape[1:], lambda b, i, c: (c, 0, 0, 0))],
        out_specs=pl.BlockSpec((tq, cw), lambda b, i, c: (b * nq + i, c)),
        out_shape=jax.ShapeDtypeStruct((n_seq * t, d), F32),
        compiler_params=_params(("arbitrary", "arbitrary", "arbitrary")),
        name="nbr_attn",
    )(proj, proj, proj, proj, proj, proj, proj, cache_k, cache_v, tz)


def _hgrn_kernel(q_ref, v_ref, f_ref, bf_ref, lb_ref, s0_ref, o_ref, s_ref, st_scr,
                 *, reverse, n_heads, dk, has_s0, n_blk):
    c = pl.program_id(1)
    L = HGRN_CHUNK
    tb = q_ref.shape[0]

    @pl.when(c == 0)
    def _():
        for h in range(n_heads):
            if has_s0:
                st_scr[h] = s0_ref[0, 0, h].T
            else:
                st_scr[h] = jnp.zeros_like(st_scr[h])

    row = lax.broadcasted_iota(I32, (L, L), 0)
    col = lax.broadcasted_iota(I32, (L, L), 1)
    tri = jnp.where((col >= row) if reverse else (col <= row), 1.0, 0.0).astype(F32)
    eye = jnp.where(row == col, 1.0, 0.0).astype(F32)
    halves = [L >> (i + 1) for i in range(L.bit_length() - 1)]
    same_pair = [jnp.where((row // (2 * hf)) == (col // (2 * hf)), 1.0, 0.0).astype(F32) for hf in halves]
    rio = lax.broadcasted_iota(I32, (L, q_ref.shape[1]), 0)
    is_query = [jnp.where(((rio & hf) == 0) if reverse else ((rio & hf) != 0), 1.0, 0.0).astype(F32)
                for hf in halves]
    sign = [2.0 * isq - 1.0 for isq in is_query]
    r8 = lax.broadcasted_iota(I32, (ROW_SUBLANES, 1), 0)
    lb = lb_ref[0]
    bf = bf_ref[0]
    n_chunks = tb // L

    def boundary_rows(cum, hf):
        blk = 2 * hf
        off = hf if reverse else hf - 1
        width = cum.shape[1]
        if blk >= ROW_SUBLANES:
            return jnp.concatenate([jnp.broadcast_to(cum[a + off:a + off + 1, :], (blk, width))
                                    for a in range(0, L, blk)], axis=0)
        groups = []
        for g in range(0, L, ROW_SUBLANES):
            ref = jnp.broadcast_to(cum[g + off:g + off + 1, :], (ROW_SUBLANES, width))
            for a in range(blk, ROW_SUBLANES, blk):
                ref = jnp.where(r8 >= a, jnp.broadcast_to(cum[g + a + off:g + a + off + 1, :],
                                                          (ROW_SUBLANES, width)), ref)
            groups.append(ref)
        return jnp.concatenate(groups, axis=0)

    def chunk_step(jj, carry):
        jc = (n_chunks - 1 - jj) if reverse else jj
        rs = pl.ds(pl.multiple_of(jc * L, L), L)
        q = _silu(q_ref[rs, :]) * (dk ** -0.5)
        v = v_ref[rs, :]
        f = lb + (1.0 - lb) * _sigmoid(f_ref[rs, :] + bf)
        logf = jnp.log(f)
        kk = 1.0 - f
        cum = _fdot(tri, logf)
        end = cum[0:1, :] if reverse else cum[L - 1:L, :]
        qd = q * jnp.exp(cum)
        kd = kk * jnp.exp(end - cum)
        e_end = jnp.exp(end)
        q_lv, k_lv = [], []
        for lv, hf in enumerate(halves):
            isq = is_query[lv]
            decay = jnp.exp(sign[lv] * (cum - boundary_rows(cum, hf)))
            dq = decay * isq
            q_lv.append((q * dq).astype(BF16))
            k_lv.append((kk * (decay - dq)).astype(BF16))
        qk_diag = q * kk
        hsl = [slice(h * dk, (h + 1) * dk) for h in range(n_heads)]
        inter = [_bdot_nt(qd[:, hs], st_scr[h]) for h, hs in enumerate(hsl)]
        attn = [eye * jnp.sum(qk_diag[:, hs], axis=-1, keepdims=True) for hs in hsl]
        for lv in range(len(halves)):
            prods = [_bdot_nt(q_lv[lv][:, hs], k_lv[lv][:, hs]) for hs in hsl]
            attn = [attn[h] + same_pair[lv] * prods[h] for h in range(n_heads)]
        outs = [inter[h] + _bdot(attn[h], v[:, hs]) for h, hs in enumerate(hsl)]
        o_ref[rs, :] = jnp.concatenate(outs, axis=-1)
        for h in range(n_heads):
            hs = slice(h * dk, (h + 1) * dk)
            st_scr[h] = st_scr[h] * e_end[:, hs] + _bdot_tn(v[:, hs], kd[:, hs])
        return carry

    lax.fori_loop(0, n_chunks, chunk_step, 0)

    @pl.when(c == n_blk - 1)
    def _():
        for h in range(n_heads):
            s_ref[0, 0, h] = st_scr[h].T


def _hgrn_call(proj, row0, n_seq, t, b_f, lb, s0, reverse):
    n_heads, dk = 8, 128
    d = n_heads * dk
    tb = 256
    n_blk = t // tb
    blk0 = row0 // tb
    di = 1 if reverse else 0
    has_s0 = s0 is not None
    if not has_s0:
        s0 = jnp.zeros((1, 2, n_heads, dk, dk), F32)

    def tok(b, c):
        return blk0 + b * n_blk + ((n_blk - 1 - c) if reverse else c)

    body = functools.partial(_hgrn_kernel, reverse=reverse, n_heads=n_heads, dk=dk,
                             has_s0=has_s0, n_blk=n_blk)
    return pl.pallas_call(
        body,
        grid=(n_seq, n_blk),
        in_specs=[pl.BlockSpec((tb, d), lambda b, c: (tok(b, c), 0)),
                  pl.BlockSpec((tb, d), lambda b, c: (tok(b, c), 1)),
                  pl.BlockSpec((tb, d), lambda b, c: (tok(b, c), 2 + di)),
                  pl.BlockSpec((1, 1, d), lambda b, c: (di, 0, 0)),
                  pl.BlockSpec((1, 1, d), lambda b, c: (di, 0, 0)),
                  pl.BlockSpec((1, 1, n_heads, dk, dk),
                               (lambda b, c: (b, di, 0, 0, 0)) if has_s0 else (lambda b, c: (0, 0, 0, 0, 0)))],
        out_specs=[pl.BlockSpec((tb, d), lambda b, c: (tok(b, c) - blk0, 0)),
                   pl.BlockSpec((1, 1, n_heads, dk, dk), lambda b, c: (b, 0, 0, 0, 0))],
        out_shape=[jax.ShapeDtypeStruct((n_seq * t, d), F32),
                   jax.ShapeDtypeStruct((n_seq, 1, n_heads, dk, dk), F32)],
        scratch_shapes=[pltpu.VMEM((n_heads, dk, dk), F32)],
        compiler_params=_params(("arbitrary", "arbitrary")),
        name="hgrn2_bw" if reverse else "hgrn2_fw",
    )(proj, proj, proj, b_f.reshape(2, 1, d), lb.reshape(2, 1, d), s0)


def _log_sigmoid(x):
    return jnp.minimum(x, 0.0) - jnp.log(1.0 + jnp.exp(-jnp.abs(x)))


def _mlstm_kernel(qf_ref, kf_ref, vf_ref, gf_ref, qb_ref, kb_ref, vb_ref, gb_ref, bg_ref, c0_ref, n0_ref, m0_ref,
                  of_ref, ob_ref, c_out, n_out, m_out, c_scr, n_scr, m_scr,
                  *, n_heads, dqk, dv, has_state, n_blk):
    c = pl.program_id(1)
    L = qf_ref.shape[0]

    @pl.when(c == 0)
    def _():
        if has_state:
            c_scr[...] = c0_ref[0]
            n_scr[...] = n0_ref[0]
            m_scr[...] = m0_ref[0]
        else:
            c_scr[...] = jnp.zeros_like(c_scr)
            n_scr[...] = jnp.zeros_like(n_scr)
            m_scr[...] = jnp.zeros_like(m_scr)

    row = lax.broadcasted_iota(I32, (L, L), 0)
    col = lax.broadcasted_iota(I32, (L, L), 1)
    lane = lax.broadcasted_iota(I32, (1, LANES), 1)
    heads = [(s, h) for s in range(2) for h in range(n_heads)]
    refs = [(qf_ref, kf_ref, vf_ref, gf_ref), (qb_ref, kb_ref, vb_ref, gb_ref)]
    causal_s, cum_s, cumt_s, gates_s, gatest_s, m_all = [], [], [], [], [], []
    for s in range(2):
        gates = GATE_SOFTCAP * jnp.tanh((refs[s][3][...] + bg_ref[...]) / GATE_SOFTCAP)
        causal = (col >= row) if s == 1 else (col <= row)
        cum = _fdot(jnp.where(causal, 1.0, 0.0).astype(F32), _log_sigmoid(gates))
        causal_s.append(causal)
        cum_s.append(cum)
        cumt_s.append(cum.T)
        gates_s.append(gates)
        gatest_s.append(gates.T)
        m_all.append(m_scr[s])
    i_off = [0, 2 * n_heads]
    f_off = [n_heads, 3 * n_heads]
    e_row = [L - 1, 0]
    c_prev = {it: c_scr[it[0], it[1]] for it in heads}
    n_prev = {it: n_scr[it[0], it[1]] for it in heads}
    qs = {(s, h): refs[s][0][:, h * dqk:(h + 1) * dqk] * (dqk ** -0.5) for s, h in heads}
    ks_ = {(s, h): refs[s][1][:, h * dqk:(h + 1) * dqk] for s, h in heads}
    vs = {(s, h): refs[s][2][:, h * dv:(h + 1) * dv] for s, h in heads}
    cum_c = {(s, h): cum_s[s][:, f_off[s] + h:f_off[s] + h + 1] for s, h in heads}
    cum_r = {(s, h): cumt_s[s][f_off[s] + h:f_off[s] + h + 1, :] for s, h in heads}
    i_c = {(s, h): gates_s[s][:, i_off[s] + h:i_off[s] + h + 1] for s, h in heads}
    i_r = {(s, h): gatest_s[s][i_off[s] + h:i_off[s] + h + 1, :] for s, h in heads}
    m_prev = {(s, h): m_all[s][0:1, h:h + 1] for s, h in heads}
    d = {h: jnp.where(causal_s[h[0]], cum_c[h] - cum_r[h] + i_r[h], -jnp.inf) for h in heads}
    m_inter = {h: cum_c[h] + m_prev[h] for h in heads}
    m_t = {h: jnp.maximum(m_inter[h], d[h].max(axis=-1, keepdims=True)) for h in heads}
    scores = {h: _bdot_nt(qs[h], ks_[h]) for h in heads}
    inter = {h: _bdot(qs[h], c_prev[h]) for h in heads}
    qn = {h: jnp.sum(qs[h] * n_prev[h], axis=-1, keepdims=True) for h in heads}
    w_inter = {h: jnp.exp(m_inter[h] - m_t[h]) for h in heads}
    qk = {h: scores[h] * jnp.exp(d[h] - m_t[h]) for h in heads}
    num = {h: w_inter[h] * inter[h] + _bdot(qk[h], vs[h]) for h in heads}
    den = {h: w_inter[h] * qn[h] + qk[h].sum(axis=-1, keepdims=True) for h in heads}
    outs = {h: num[h] / jnp.maximum(jnp.abs(den[h]), jnp.exp(-m_t[h])) for h in heads}
    end = {h: cum_c[h][e_row[h[0]]:e_row[h[0]] + 1, :] for h in heads}
    g_end_r = {h: end[h] - cum_r[h] + i_r[h] for h in heads}
    g_end_c = {h: end[h] - cum_c[h] + i_c[h] for h in heads}
    m_new = {h: jnp.maximum(end[h] + m_prev[h], g_end_r[h].max(axis=-1, keepdims=True)) for h in heads}
    w_old = {h: jnp.exp(end[h] + m_prev[h] - m_new[h]) for h in heads}
    kd = {h: ks_[h] * jnp.exp(g_end_c[h] - m_new[h]) for h in heads}
    c_new = {h: w_old[h] * c_prev[h] + _bdot_tn(kd[h], vs[h]) for h in heads}
    n_new = {h: w_old[h] * n_prev[h] + kd[h].sum(axis=0, keepdims=True) for h in heads}
    of_ref[...] = jnp.concatenate([outs[(0, h)] for h in range(n_heads)], axis=-1)
    ob_ref[...] = jnp.concatenate([outs[(1, h)] for h in range(n_heads)], axis=-1)
    for s in range(2):
        m_next = m_all[s]
        for h in range(n_heads):
            c_scr[s, h] = c_new[(s, h)]
            n_scr[s, h] = n_new[(s, h)]
            m_next = jnp.where(lane == h, m_new[(s, h)], m_next)
        m_scr[s] = m_next

    @pl.when(c == n_blk - 1)
    def _():
        c_out[0] = c_scr[...]
        n_out[0] = n_scr[...]
        m_out[0] = m_scr[...]


def _mlstm_call(proj, row0, n_seq, t, b_gates_pad, state):
    n_heads, dqk, dv = 8, 64, 128
    wq, wv = n_heads * dqk, n_heads * dv
    L = 256
    n_blk = t // L
    blk0 = row0 // L
    has_state = state is not None
    if has_state:
        c0, n0, m0 = state
        smap = lambda b, c: (b, 0, 0, 0, 0)
        mmap = lambda b, c: (b, 0, 0, 0)
    else:
        c0 = jnp.zeros((1, 2, n_heads, dqk, dv), F32)
        n0 = jnp.zeros((1, 2, n_heads, 1, dqk), F32)
        m0 = jnp.zeros((1, 2, 1, 128), F32)
        smap = lambda b, c: (0, 0, 0, 0, 0)
        mmap = lambda b, c: (0, 0, 0, 0)

    def tok(b, c, reverse):
        return blk0 + b * n_blk + ((n_blk - 1 - c) if reverse else c)

    body = functools.partial(_mlstm_kernel, n_heads=n_heads, dqk=dqk, dv=dv, has_state=has_state, n_blk=n_blk)
    gate_cb = (2 * wq + 2 * wv) // 128

    def token_specs(reverse):
        return [pl.BlockSpec((L, wq), lambda b, c: (tok(b, c, reverse), 0)),
                pl.BlockSpec((L, wq), lambda b, c: (tok(b, c, reverse), 1)),
                pl.BlockSpec((L, wv), lambda b, c: (tok(b, c, reverse), (2 * wq) // wv)),
                pl.BlockSpec((L, 128), lambda b, c: (tok(b, c, reverse), gate_cb))]

    return pl.pallas_call(
        body,
        grid=(n_seq, n_blk),
        in_specs=token_specs(False) + token_specs(True)
                 + [pl.BlockSpec((1, 128), lambda b, c: (0, 0)),
                    pl.BlockSpec((1, 2, n_heads, dqk, dv), smap),
                    pl.BlockSpec((1, 2, n_heads, 1, dqk), smap),
                    pl.BlockSpec((1, 2, 1, 128), mmap)],
        out_specs=[pl.BlockSpec((L, wv), lambda b, c: (tok(b, c, False) - blk0, 0)),
                   pl.BlockSpec((L, wv), lambda b, c: (tok(b, c, True) - blk0, 0)),
                   pl.BlockSpec((1, 2, n_heads, dqk, dv), lambda b, c: (b, 0, 0, 0, 0)),
                   pl.BlockSpec((1, 2, n_heads, 1, dqk), lambda b, c: (b, 0, 0, 0, 0)),
                   pl.BlockSpec((1, 2, 1, 128), lambda b, c: (b, 0, 0, 0))],
        out_shape=[jax.ShapeDtypeStruct((n_seq * t, wv), F32),
                   jax.ShapeDtypeStruct((n_seq * t, wv), F32),
                   jax.ShapeDtypeStruct((n_seq, 2, n_heads, dqk, dv), F32),
                   jax.ShapeDtypeStruct((n_seq, 2, n_heads, 1, dqk), F32),
                   jax.ShapeDtypeStruct((n_seq, 2, 1, 128), F32)],
        scratch_shapes=[pltpu.VMEM((2, n_heads, dqk, dv), F32),
                        pltpu.VMEM((2, n_heads, 1, dqk), F32),
                        pltpu.VMEM((2, 1, 128), F32)],
        compiler_params=_params(("arbitrary", "arbitrary")),
        name="mlstm",
    )(proj, proj, proj, proj, proj, proj, proj, proj, b_gates_pad, c0, n0, m0)


def _moe_input(x_ref, m_ref, g_ref):
    return _rms(x_ref[...], g_ref[...]) * (1.0 + m_ref[0, 4:5, :]) + m_ref[0, 3:4, :]


def _route_kernel(x_ref, m_ref, g_ref, wr_ref, br_ref, idx_ref, gate_ref, rank_ref, cnt_ref, carry_scr,
                  *, n_experts):
    i = pl.program_id(0)
    tm = x_ref.shape[0]

    @pl.when(i == 0)
    def _():
        carry_scr[...] = jnp.zeros_like(carry_scr)

    h = _moe_input(x_ref, m_ref, g_ref)
    w = wr_ref[...]
    h_hi, w_hi = h.astype(BF16), w.astype(BF16)
    h_lo = (h - h_hi.astype(F32)).astype(BF16)
    w_lo = (w - w_hi.astype(F32)).astype(BF16)
    logits = _bdot_nt(w_hi, h_hi) + _bdot_nt(w_lo, h_hi) + _bdot_nt(w_hi, h_lo) + br_ref[...]
    e_io = lax.broadcasted_iota(I32, (n_experts, tm), 0).astype(F32)
    work = logits
    vals, idxs = [], []
    chosen = jnp.zeros((n_experts, tm), F32)
    for _ in range(TOP_K):
        mx = work.max(axis=0, keepdims=True)
        ix = jnp.min(jnp.where(work == mx, e_io, float(n_experts)), axis=0, keepdims=True)
        hit = e_io == ix
        vals.append(mx)
        idxs.append(ix)
        chosen = jnp.where(hit, 1.0, chosen)
        work = jnp.where(hit, -jnp.inf, work)
    es = [jnp.exp(v - vals[0]) for v in vals]
    den = es[0] + es[1] + es[2] + es[3]
    srow = lax.broadcasted_iota(I32, (tm, tm), 0)
    scol = lax.broadcasted_iota(I32, (tm, tm), 1)
    before = jnp.where(srow < scol, 1.0, 0.0).astype(BF16)
    pos = jnp.dot(chosen.astype(BF16), before, preferred_element_type=F32) + carry_scr[...]
    ranks = [jnp.sum(jnp.where(e_io == ix, pos, 0.0), axis=0, keepdims=True) for ix in idxs]
    carry_scr[...] = carry_scr[...] + chosen.sum(axis=1, keepdims=True)
    idx_ref[...] = jnp.concatenate(idxs, axis=0).astype(I32)
    gate_ref[...] = jnp.concatenate([e / den for e in es], axis=0)
    rank_ref[...] = jnp.concatenate(ranks, axis=0).astype(I32)
    cnt_ref[...] = jnp.broadcast_to(carry_scr[...], cnt_ref.shape).astype(I32)


def _route_call(x, modsel, g, w_router, b_router):
    n, d = x.shape
    n_experts = w_router.shape[1]
    tm = TOKEN_TILE
    body = functools.partial(_route_kernel, n_experts=n_experts)
    return pl.pallas_call(
        body,
        grid=(n // tm,),
        in_specs=[pl.BlockSpec((tm, d), lambda i: (i, 0)),
                  pl.BlockSpec((1, 6, d), lambda i: (i, 0, 0)),
                  pl.BlockSpec((1, d), lambda i: (0, 0)),
                  pl.BlockSpec((n_experts, d), lambda i: (0, 0)),
                  pl.BlockSpec((n_experts, 1), lambda i: (0, 0))],
        out_specs=[pl.BlockSpec((TOP_K, tm), lambda i: (0, i)),
                   pl.BlockSpec((TOP_K, tm), lambda i: (0, i)),
                   pl.BlockSpec((TOP_K, tm), lambda i: (0, i)),
                   pl.BlockSpec((n_experts, 128), lambda i: (0, 0))],
        out_shape=[jax.ShapeDtypeStruct((TOP_K, n), I32),
                   jax.ShapeDtypeStruct((TOP_K, n), F32),
                   jax.ShapeDtypeStruct((TOP_K, n), I32),
                   jax.ShapeDtypeStruct((n_experts, 128), I32)],
        scratch_shapes=[pltpu.VMEM((n_experts, 1), F32)],
        compiler_params=_params(("arbitrary",)),
        name="moe_route",
    )(x, modsel, g.reshape(1, d), w_router.T, b_router.reshape(n_experts, 1))


def _slot_kernel(cnt_ref, idx_ref, rank_ref, dest_ref, binfo_ref, pad_ref, nused_ref, *, n_experts, n_blocks):
    cnt = cnt_ref[:, 0:1].astype(F32)
    padded = jnp.ceil(cnt * (1.0 / MOE_BLOCK)) * MOE_BLOCK
    er = lax.broadcasted_iota(I32, (n_experts, n_experts), 0)
    ec = lax.broadcasted_iota(I32, (n_experts, n_experts), 1)
    start_row = jnp.sum(jnp.where(er < ec, padded, 0.0), axis=0, keepdims=True)
    start_col = jnp.sum(jnp.where(er == ec, start_row, 0.0), axis=1, keepdims=True)
    end_col = start_col + padded
    idx = idx_ref[...]
    e_io = lax.broadcasted_iota(I32, (n_experts,) + idx.shape[1:], 0)
    rows = []
    for k in range(TOP_K):
        hit = e_io == idx[k:k + 1, :]
        rows.append(jnp.sum(jnp.where(hit, start_col, 0.0), axis=0, keepdims=True))
    dest_ref[...] = jnp.concatenate(rows, axis=0).astype(I32) + rank_ref[...]
    blk_start = (lax.broadcasted_iota(I32, (n_experts, n_blocks), 1) * MOE_BLOCK).astype(F32)
    n_done = jnp.sum(jnp.where(end_col <= blk_start, 1.0, 0.0), axis=0, keepdims=True)
    bexp = jnp.minimum(n_done, n_experts - 1.0)
    used_row = jnp.sum(jnp.where(er == ec, jnp.where(cnt > 0.0, 1.0, 0.0), 0.0), axis=0, keepdims=True)
    ecf = ec.astype(F32)
    next_col = jnp.min(jnp.where((ec > er) & (used_row > 0.0), ecf, float(n_experts)), axis=1, keepdims=True)
    ord_col = jnp.sum(jnp.where(ec < er, used_row, 0.0), axis=1, keepdims=True)
    par_col = ord_col - 2.0 * jnp.floor(ord_col * 0.5)
    mine = lax.broadcasted_iota(I32, (n_experts, n_blocks), 0).astype(F32) == bexp
    bnext = jnp.sum(jnp.where(mine, next_col, 0.0), axis=0, keepdims=True)
    bslot = jnp.sum(jnp.where(mine, par_col, 0.0), axis=0, keepdims=True)
    binfo_ref[...] = jnp.concatenate([bexp, bnext, bslot], axis=0).astype(I32)
    n_used = jnp.sum(padded, axis=0, keepdims=True) * (1.0 / MOE_BLOCK)
    nused_ref[...] = n_used.astype(I32)
    cnt_row = jnp.sum(jnp.where(er == ec, cnt, 0.0), axis=0, keepdims=True)
    padded_row = jnp.sum(jnp.where(er == ec, padded, 0.0), axis=0, keepdims=True)
    pad_ref[...] = jnp.concatenate([start_row + cnt_row, padded_row - cnt_row,
                                    jnp.broadcast_to(n_used, cnt_row.shape)], axis=0).astype(I32)


def _slot_call(counts, idx_t, rank_t, n_blocks):
    n_experts = counts.shape[0]
    n = idx_t.shape[1]
    tn = min(2048, n)
    body = functools.partial(_slot_kernel, n_experts=n_experts, n_blocks=n_blocks)
    return pl.pallas_call(
        body,
        grid=(n // tn,),
        in_specs=[pl.BlockSpec((n_experts, 128), lambda i: (0, 0)),
                  pl.BlockSpec((TOP_K, tn), lambda i: (0, i)),
                  pl.BlockSpec((TOP_K, tn), lambda i: (0, i))],
        out_specs=[pl.BlockSpec((TOP_K, tn), lambda i: (0, i)),
                   pl.BlockSpec((3, n_blocks), lambda i: (0, 0)),
                   pl.BlockSpec((3, n_experts), lambda i: (0, 0)),
                   pl.BlockSpec((1, 1), lambda i: (0, 0))],
        out_shape=[jax.ShapeDtypeStruct((TOP_K, n), I32),
                   jax.ShapeDtypeStruct((3, n_blocks), I32),
                   jax.ShapeDtypeStruct((3, n_experts), I32),
                   jax.ShapeDtypeStruct((1, 1), I32)],
        compiler_params=_params(("arbitrary",)),
        name="moe_slots",
    )(counts, idx_t, rank_t)


DMA_ISSUE_UNROLL = 8


def _to_row_tiles(ref, base, x):
    rows = x.shape[0]
    for c in range(ROW_SUBLANES):
        ref[pl.ds(base * ROW_SUBLANES + c, rows, stride=ROW_SUBLANES), :] = x[:, c * LANES:(c + 1) * LANES]


def _from_row_tiles(ref, base, rows, c):
    return ref[pl.ds(base * ROW_SUBLANES + c, rows, stride=ROW_SUBLANES), :]


def _row_tile(ref, r):
    return ref.at[pl.ds(pl.multiple_of(r * ROW_SUBLANES, ROW_SUBLANES), ROW_SUBLANES)]


def _zero_fill_padding(pad_ref, xs_ref, z_scr, sem):
    z_scr[...] = jnp.zeros_like(z_scr)
    n_experts = pad_ref.shape[1]
    bits = range(MOE_BLOCK.bit_length() - 2, -1, -1)

    def pieces(e):
        off, length = pad_ref[0, e], pad_ref[1, e]
        for bit in bits:
            size = 1 << bit
            done = (length >> (bit + 1)) << (bit + 1)
            copy = pltpu.make_async_copy(z_scr.at[pl.ds(0, size * ROW_SUBLANES)],
                                         xs_ref.at[pl.ds(pl.multiple_of((off + done) * ROW_SUBLANES, ROW_SUBLANES),
                                                         size * ROW_SUBLANES)], sem)
            yield (length & size) != 0, copy

    def tail_blocks():
        n_blocks = xs_ref.shape[0] // (MOE_BLOCK * ROW_SUBLANES)
        for b in range(n_blocks - n_experts, n_blocks):
            copy = pltpu.make_async_copy(z_scr, xs_ref.at[pl.ds(b * MOE_BLOCK * ROW_SUBLANES,
                                                                MOE_BLOCK * ROW_SUBLANES)], sem)
            yield b >= pad_ref[2, 0], copy

    def all_copies():
        for e in range(n_experts):
            yield from pieces(e)
        yield from tail_blocks()

    for needed, copy in all_copies():
        pl.when(needed)(copy.start)
    for needed, copy in all_copies():
        pl.when(needed)(copy.wait)


def _dispatch_kernel(pad_ref, dest_ref, x_ref, m_ref, g_ref, xs_ref, h_scr, z_scr, sem, *, n_tiles):
    i = pl.program_id(0)
    tm = x_ref.shape[0]

    @pl.when(i == 0)
    def _():
        _zero_fill_padding(pad_ref, xs_ref, z_scr, sem.at[2])

    slot = lax.rem(i, 2)
    src = h_scr.at[slot]
    _to_row_tiles(src, 0, _moe_input(x_ref, m_ref, g_ref))

    def start_row(r, carry):
        for k in range(TOP_K):
            pltpu.make_async_copy(_row_tile(src, r), _row_tile(xs_ref, dest_ref[0, 0, k * tm + r]),
                                  sem.at[slot]).start(priority=k % 2)
        return carry

    lax.fori_loop(0, tm, start_row, 0, unroll=DMA_ISSUE_UNROLL // TOP_K)

    def wait_tile(s):
        for _ in range(TOP_K):
            pltpu.make_async_copy(h_scr.at[s], xs_ref.at[pl.ds(0, tm * ROW_SUBLANES)], sem.at[s]).wait()

    @pl.when(i >= 1)
    def _():
        wait_tile(1 - slot)

    @pl.when(i == n_tiles - 1)
    def _():
        wait_tile(slot)


def _dispatch_call(pad_info, dest_tiles, x, modsel, g, n_slots):
    n, d = x.shape
    tm = TOKEN_TILE
    assert d == ROW_SUBLANES * LANES
    return pl.pallas_call(
        functools.partial(_dispatch_kernel, n_tiles=n // tm),
        grid=(n // tm,),
        in_specs=[pl.BlockSpec(memory_space=pltpu.SMEM),
                  pl.BlockSpec((1, 1, TOP_K * tm), lambda i: (i, 0, 0), memory_space=pltpu.SMEM),
                  pl.BlockSpec((tm, d), lambda i: (i, 0)),
                  pl.BlockSpec((1, 6, d), lambda i: (i, 0, 0)),
                  pl.BlockSpec((1, d), lambda i: (0, 0))],
        out_specs=pl.BlockSpec(memory_space=pl.ANY),
        out_shape=jax.ShapeDtypeStruct((n_slots * ROW_SUBLANES, LANES), F32),
        scratch_shapes=[pltpu.VMEM((2, tm * ROW_SUBLANES, LANES), F32),
                        pltpu.VMEM((MOE_BLOCK * ROW_SUBLANES, LANES), F32),
                        pltpu.SemaphoreType.DMA((3,))],
        compiler_params=_params(("arbitrary",)),
        name="moe_dispatch",
    )(pad_info, dest_tiles, x, modsel, g.reshape(1, d))


FFN_BLOCKS_PER_STEP = 4


def _ffn_kernel(binfo_ref, nused_ref, xs_ref, wgu_hbm, wdn_hbm, *rest, layer, n_experts, n_blocks):
    bias_refs = rest[:2 * FFN_BLOCKS_PER_STEP]
    ys_ref, wgu_f32, wdn_f32, wgu_scr, wdn_scr, sem = rest[2 * FFN_BLOCKS_PER_STEP:]
    d_ff = wdn_scr.shape[0]
    rows = MOE_BLOCK

    def weight_copies(e, slot):
        return (pltpu.make_async_copy(wgu_hbm.at[layer, e], wgu_f32.at[slot], sem.at[slot]),
                pltpu.make_async_copy(wdn_hbm.at[layer, e], wdn_f32.at[slot], sem.at[slot]))

    def one_block(u):
        b = pl.program_id(0) * FFN_BLOCKS_PER_STEP + u
        base = u * MOE_BLOCK
        bgu_ref, bdn_ref = bias_refs[2 * u], bias_refs[2 * u + 1]

        @pl.when(b < nused_ref[0])
        def _():
            e = binfo_ref[b]
            prev = binfo_ref[jnp.maximum(b - 1, 0)]
            slot = binfo_ref[2 * n_blocks + b]

            @pl.when((b == 0) | (e != prev))
            def _():
                @pl.when(b == 0)
                def _():
                    for cp in weight_copies(e, slot):
                        cp.start()

                nxt = binfo_ref[n_blocks + b]

                @pl.when(nxt < n_experts)
                def _():
                    for cp in weight_copies(nxt, 1 - slot):
                        cp.start()

                for cp in weight_copies(e, slot):
                    cp.wait()
                wgu_scr[...] = wgu_f32[slot].astype(BF16)
                wdn_scr[...] = wdn_f32[slot].astype(BF16)

            x = jnp.concatenate([_from_row_tiles(xs_ref, base, rows, c).astype(BF16)
                                 for c in range(ROW_SUBLANES)], axis=-1)
            gu = jnp.dot(x, wgu_scr[...], preferred_element_type=F32) + bgu_ref[0, 0]
            x_glu = jnp.minimum(gu[:, :d_ff], SWIGLU_LIMIT)
            x_lin = jnp.clip(gu[:, d_ff:], -SWIGLU_LIMIT, SWIGLU_LIMIT)
            hid = x_glu * _sigmoid(SWIGLU_ALPHA * x_glu) * (x_lin + 1.0)
            _to_row_tiles(ys_ref, base, jnp.dot(hid.astype(BF16), wdn_scr[...], preferred_element_type=F32)
                          + bdn_ref[0, 0])

        @pl.when(b >= nused_ref[0])
        def _():
            ys_ref[pl.ds(base * ROW_SUBLANES, rows * ROW_SUBLANES), :] = jnp.zeros(
                (rows * ROW_SUBLANES, LANES), F32)

    for u in range(FFN_BLOCKS_PER_STEP):
        one_block(u)


def _ffn_call(layer, block_info, n_used, xs, w_gu, b_gu, w_dn, b_dn):
    depth, n_experts, d, d_ff2 = w_gu.shape
    d_ff = d_ff2 // 2
    g = FFN_BLOCKS_PER_STEP
    step_rows = g * MOE_BLOCK * ROW_SUBLANES
    n_blocks = xs.shape[0] // (MOE_BLOCK * ROW_SUBLANES)
    assert n_blocks % g == 0

    def blk(b, nu):
        return jnp.maximum(jnp.minimum(b, nu[0] - 1), 0)

    def bias_specs(u):
        return [pl.BlockSpec((1, 1, 1, d_ff2), lambda s, bi, nu: (layer, bi[blk(g * s + u, nu)], 0, 0)),
                pl.BlockSpec((1, 1, 1, d), lambda s, bi, nu: (layer, bi[blk(g * s + u, nu)], 0, 0))]

    bias_args = [b_gu.reshape(depth, n_experts, 1, d_ff2), b_dn.reshape(depth, n_experts, 1, d)] * g
    grid_spec = pltpu.PrefetchScalarGridSpec(
        num_scalar_prefetch=2,
        grid=(n_blocks // g,),
        in_specs=[pl.BlockSpec((step_rows, LANES), lambda s, bi, nu: (blk(g * s, nu) // g, 0)),
                  pl.BlockSpec(memory_space=pl.ANY),
                  pl.BlockSpec(memory_space=pl.ANY)]
                 + [spec for u in range(g) for spec in bias_specs(u)],
        out_specs=pl.BlockSpec((step_rows, LANES), lambda s, bi, nu: (s, 0)),
        scratch_shapes=[pltpu.VMEM((2, d, d_ff2), F32), pltpu.VMEM((2, d_ff, d), F32),
                        pltpu.VMEM((d, d_ff2), BF16), pltpu.VMEM((d_ff, d), BF16),
                        pltpu.SemaphoreType.DMA((2,))],
    )
    body = functools.partial(_ffn_kernel, layer=layer, n_experts=n_experts, n_blocks=n_blocks)
    return pl.pallas_call(
        body,
        grid_spec=grid_spec,
        out_shape=jax.ShapeDtypeStruct(xs.shape, F32),
        compiler_params=_params(("arbitrary",)),
        name="moe_ffn",
    )(block_info, n_used, xs, w_gu, w_dn, *bias_args)


def _gather_expert_rows(dest_ref, dest_next_ref, ys_ref, gate_ref, buf, sem, n_tiles):
    i = pl.program_id(0)
    tm = gate_ref.shape[0]
    rows = TOP_K * tm

    def start_gathers(d_ref, slot):
        def start_pair(p, carry):
            for u in range(2):
                j = 2 * p + u
                pltpu.make_async_copy(_row_tile(ys_ref, d_ref[0, 0, j]), _row_tile(buf.at[slot], j),
                                      sem.at[slot]).start(priority=u)
            return carry

        lax.fori_loop(0, rows // 2, start_pair, 0, unroll=DMA_ISSUE_UNROLL // 2)

    slot = lax.rem(i, 2)

    @pl.when(i == 0)
    def _():
        start_gathers(dest_ref, 0)

    def wait_tile(s):
        pltpu.make_async_copy(ys_ref.at[pl.ds(0, rows * ROW_SUBLANES)], buf.at[s], sem.at[s]).wait()

    @pl.when(i + 1 < n_tiles)
    def _():
        start_gathers(dest_next_ref, 1 - slot)

    wait_tile(slot)
    cur = buf.at[slot]
    chunks = []
    for c in range(ROW_SUBLANES):
        y = gate_ref[:, 0:1] * _from_row_tiles(cur, 0, tm, c)
        for k in range(1, TOP_K):
            y = y + gate_ref[:, k:k + 1] * _from_row_tiles(cur, k * tm, tm, c)
        chunks.append(y)
    return jnp.concatenate(chunks, axis=-1)


def _combine_kernel(dest_ref, dest_next_ref, ys_ref, gate_ref, x_ref, m_ref, fg_ref, y_ref, buf, sem,
                    *, n_tiles):
    y = _gather_expert_rows(dest_ref, dest_next_ref, ys_ref, gate_ref, buf, sem, n_tiles)
    y_ref[...] = _rms(x_ref[...] + m_ref[0, 5:6, :] * y, fg_ref[...])


def _combine_inproj_kernel(dest_ref, dest_next_ref, ys_ref, gate_ref, x_ref, mp_ref, mc_ref, g_ref, w_ref,
                           x_out_ref, p_ref, buf, sem, *, n_tiles):
    y = _gather_expert_rows(dest_ref, dest_next_ref, ys_ref, gate_ref, buf, sem, n_tiles)
    x = x_ref[...] + mp_ref[0, 5:6, :] * y
    x_out_ref[...] = x
    h = _rms(x, g_ref[...]) * (1.0 + mc_ref[0, 1:2, :]) + mc_ref[0, 0:1, :]
    p_ref[...] = _bdot(h, w_ref[...])


def _moe_gather_specs(tm, n_tiles, d):
    return [pl.BlockSpec((1, 1, TOP_K * tm), lambda i: (i, 0, 0), memory_space=pltpu.SMEM),
            pl.BlockSpec((1, 1, TOP_K * tm), lambda i: (jnp.minimum(i + 1, n_tiles - 1), 0, 0),
                         memory_space=pltpu.SMEM),
            pl.BlockSpec(memory_space=pl.ANY),
            pl.BlockSpec((tm, TOP_K), lambda i: (i, 0)),
            pl.BlockSpec((tm, d), lambda i: (i, 0))]


def _moe_gather_scratch(tm):
    return [pltpu.VMEM((2, TOP_K * tm * ROW_SUBLANES, LANES), F32), pltpu.SemaphoreType.DMA((2,))]


def _combine_inproj_call(pending, x, modsel_prev, modsel, g, w):
    dest_tiles, ys, gates_nk = pending
    n, d = x.shape
    wout = w.shape[1]
    tm = TOKEN_TILE
    n_tiles = n // tm
    body = functools.partial(_combine_inproj_kernel, n_tiles=n_tiles)
    return pl.pallas_call(
        body,
        grid=(n_tiles,),
        in_specs=_moe_gather_specs(tm, n_tiles, d)
                 + [pl.BlockSpec((1, 6, d), lambda i: (i, 0, 0)),
                    pl.BlockSpec((1, 6, d), lambda i: (i, 0, 0)),
                    pl.BlockSpec((1, d), lambda i: (0, 0)),
                    pl.BlockSpec((d, wout), lambda i: (0, 0))],
        out_specs=[pl.BlockSpec((tm, d), lambda i: (i, 0)),
                   pl.BlockSpec((tm, wout), lambda i: (i, 0))],
        out_shape=[jax.ShapeDtypeStruct((n, d), F32),
                   jax.ShapeDtypeStruct((n, wout), F32)],
        scratch_shapes=_moe_gather_scratch(tm),
        compiler_params=_params(("arbitrary",)),
        name="moe_combine_inproj",
    )(dest_tiles, dest_tiles, ys, gates_nk, x, modsel_prev, modsel, g.reshape(1, d), w)


def _combine_call(pending, x, modsel, final_g):
    dest_tiles, ys, gates_nk = pending
    n, d = x.shape
    tm = TOKEN_TILE
    n_tiles = n // tm
    body = functools.partial(_combine_kernel, n_tiles=n_tiles)
    return pl.pallas_call(
        body,
        grid=(n_tiles,),
        in_specs=_moe_gather_specs(tm, n_tiles, d)
                 + [pl.BlockSpec((1, 6, d), lambda i: (i, 0, 0)),
                    pl.BlockSpec((1, d), lambda i: (0, 0))],
        out_specs=pl.BlockSpec((tm, d), lambda i: (i, 0)),
        out_shape=jax.ShapeDtypeStruct((n, d), F32),
        scratch_shapes=_moe_gather_scratch(tm),
        compiler_params=_params(("arbitrary",)),
        name="moe_combine",
    )(dest_tiles, dest_tiles, ys, gates_nk, x, modsel, final_g.reshape(1, d))


def _moe_experts(layer, x, modsel, g2, w_router, b_router, w_gu, b_gu, w_dn, b_dn):
    n, d = x.shape
    n_experts = w_router.shape[1]
    tm = TOKEN_TILE
    n_blocks = (n * TOP_K) // MOE_BLOCK + n_experts
    idx_t, gate_t, rank_t, counts = _route_call(x, modsel, g2, w_router, b_router)
    dest_t, block_info, pad_info, n_used = _slot_call(counts, idx_t, rank_t, n_blocks)
    dest_tiles = dest_t.reshape(TOP_K, n // tm, tm).transpose(1, 0, 2).reshape(n // tm, 1, TOP_K * tm)
    xs = _dispatch_call(pad_info, dest_tiles, x, modsel, g2, n_blocks * MOE_BLOCK)
    ys = _ffn_call(layer, block_info.reshape(3 * n_blocks), n_used.reshape(1), xs, w_gu, b_gu, w_dn, b_dn)
    return dest_tiles, ys, gate_t.T


def _rope_tables(t, hd):
    pos = np.arange(t)
    n_freq = hd // 4
    inv_freq = ROPE_THETA ** (-np.arange(n_freq, dtype=np.float32) / n_freq)
    ang = np.concatenate([(pos // GRID_W).astype(np.float32)[:, None] * inv_freq,
                          (pos % GRID_W).astype(np.float32)[:, None] * inv_freq], axis=-1)
    ang = jnp.asarray(ang, F32)
    cos, sin = jnp.cos(ang), jnp.sin(ang)
    return jnp.concatenate([cos, cos], axis=-1), jnp.concatenate([-sin, sin], axis=-1)


def kernel(x_prompt, x_sample, cache_k_a, cache_v_a, state_b, state_c_C, state_c_n, state_c_m, cache_k_d, cache_v_d, c, c_ctx, norm1_g, norm2_g, w_mod, b_mod, w_in_a, qnorm_a, knorm_a, w_out_a, w_in_b, b_f_b, lower_bounds_b, onorm_b, w_out_b, w_in_c, b_gates_c, onorm_c, w_out_c, w_in_d, rpb_d, w_out_d, w_router, b_router, w_gu, b_gu, w_dn, b_dn, final_g):
    n_ctx_seq, t_ctx, d = x_prompt.shape
    n_lat_seq, t_lat, _ = x_sample.shape
    depth = w_mod.shape[0]
    n_ctx = n_ctx_seq * t_ctx
    n_lat = n_lat_seq * t_lat
    n = n_ctx + n_lat
    tm = TOKEN_TILE
    assert t_ctx % tm == 0 and t_lat % tm == 0 and n_lat_seq + 1 <= 8
    assert n_ctx % PROJ_TILE == 0 and t_lat % PROJ_TILE == 0

    lb_cum = jnp.cumsum(jax.nn.softmax(lower_bounds_b.astype(F32), axis=0), axis=0)
    lb_all = lb_cum - lb_cum[0]

    cond8 = jnp.zeros((8, d), F32).at[0].set(c_ctx).at[1:1 + n_lat_seq].set(c)
    mod = _mod_call(cond8, w_mod, b_mod)
    tile_row = np.concatenate([np.zeros(n_ctx // tm, np.int32),
                               1 + np.repeat(np.arange(n_lat_seq, dtype=np.int32), t_lat // tm)])

    x = jnp.concatenate([x_prompt.reshape(n_ctx, d), x_sample.reshape(n_lat, d)], axis=0)
    outs = {}
    pending = None
    modsel = None

    def inproj(x, modsel, g, w):
        if pending is None:
            return x, _inproj_call(x, modsel, g, w)
        return _combine_inproj_call(pending, x, modsel_prev, modsel, g, w)

    for i in range(depth):
        kind = i % 4
        j = i // 4
        modsel_prev = modsel
        modsel = mod[i].reshape(8, 6, d)[tile_row]
        if kind == 0:
            x, proj = inproj(x, modsel, norm1_g[i],w_in_a[j].astype(BF16))
            o_ctx, k_new = _gqa_ctx_call(proj, qnorm_a[j], knorm_a[j], n_ctx_seq, t_ctx)
            cosd, sind = _rope_tables(t_lat, 128)
            o_lat = _gqa_lat_call(proj, n_ctx, n_lat_seq, t_lat,
                                  cache_k_a[:, j].reshape(n_lat_seq, -1, 256),
                                  cache_v_a[:, j].reshape(n_lat_seq, -1, 256), cosd, sind,
                                  qnorm_a[j], knorm_a[j])
            outs["k_a"] = k_new.reshape(n_ctx_seq, 1, t_ctx, 2, 128)
            outs["v_a"] = proj[:n_ctx, 1280:1536].reshape(n_ctx_seq, 1, t_ctx, 2, 128)
            x = _outproj_call("plain", [(o_ctx, o_lat)], None, w_out_a[j].astype(BF16), x, modsel)
        elif kind == 1:
            x, proj = inproj(x, modsel, norm1_g[i],w_in_b[j].astype(BF16))
            o_dirs, s_dirs = [], []
            for reverse in (False, True):
                oc, sc = _hgrn_call(proj, 0, n_ctx_seq, t_ctx, b_f_b[j], lb_all[i], None, reverse)
                ol, _ = _hgrn_call(proj, n_ctx, n_lat_seq, t_lat, b_f_b[j], lb_all[i], state_b[:, j], reverse)
                o_dirs.append((oc, ol))
                s_dirs.append(sc)
            outs["s_b"] = jnp.concatenate(s_dirs, axis=1)[:, None]
            x = _outproj_call("hgrn", o_dirs, (proj, 4, onorm_b[j].reshape(1, 128)),
                              w_out_b[j].astype(BF16), x, modsel)
        elif kind == 2:
            w_c = jnp.pad(w_in_c[j], ((0, 0), (0, 128 - 32))).astype(BF16)
            bg = jnp.pad(b_gates_c[j].reshape(1, 32), ((0, 0), (0, 128 - 32)))
            x, proj = inproj(x, modsel, norm1_g[i],w_c)
            state = (state_c_C[:, j], state_c_n[:, j][:, :, :, None, :],
                     jnp.pad(state_c_m[:, j], ((0, 0), (0, 0), (0, 120)))[:, :, None, :])
            ocf, ocb, cc, nc, mc = _mlstm_call(proj, 0, n_ctx_seq, t_ctx, bg, None)
            olf, olb, _, _, _ = _mlstm_call(proj, n_ctx, n_lat_seq, t_lat, bg, state)
            o_dirs = [(ocf, olf), (ocb, olb)]
            outs["c_C"] = cc[:, None]
            outs["c_n"] = nc[:, None, :, :, 0, :]
            outs["c_m"] = mc[:, None, :, 0, :8]
            x = _outproj_call("mlstm", o_dirs, (proj, 2, onorm_c[j].reshape(1, 128)),
                              w_out_c[j].astype(BF16), x, modsel)
        else:
            x, proj = inproj(x, modsel, norm1_g[i],w_in_d[j].astype(BF16))
            o_ctx = _mha_ctx_call(proj, n_ctx_seq, t_ctx)
            tz = _na_bias_call(rpb_d[j])
            o_lat = _na_call(proj, n_ctx, n_lat_seq, t_lat,
                             cache_k_d[:, j].reshape(n_lat_seq, -1, d),
                             cache_v_d[:, j].reshape(n_lat_seq, -1, d), tz)
            outs["k_d"] = proj[:n_ctx, d:2 * d].reshape(n_ctx_seq, 1, t_ctx, 16, 64)
            outs["v_d"] = proj[:n_ctx, 2 * d:3 * d].reshape(n_ctx_seq, 1, t_ctx, 16, 64)
            x = _outproj_call("plain", [(o_ctx, o_lat)], None, w_out_d[j].astype(BF16), x, modsel)
        pending = _moe_experts(i, x, modsel, norm2_g[i], w_router[i], b_router[i], w_gu, b_gu, w_dn, b_dn)
    x = _combine_call(pending, x, modsel, final_g)

    y_prompt = x[:n_ctx].reshape(n_ctx_seq, t_ctx, d)
    y_sample = x[n_ctx:].reshape(n_lat_seq, t_lat, d)
    return (y_prompt, y_sample, outs["k_a"], outs["v_a"], outs["s_b"], outs["c_C"], outs["c_n"], outs["c_m"],
            outs["k_d"], outs["v_d"])
```

```python
import functools

import numpy as np
import jax
import jax.numpy as jnp
from jax import lax
from jax.experimental import pallas as pl
from jax.experimental.pallas import tpu as pltpu

F32 = jnp.float32
BF16 = jnp.bfloat16
I32 = jnp.int32

NORM_EPS = 1e-6
GRID_W = 64
ROPE_THETA = 10000.0
TOP_K = 4
HGRN_CHUNK = 128
GATE_SOFTCAP = 15.0
NA_ROWS = 8
NA_COLS = 16
SWIGLU_ALPHA = 1.702
SWIGLU_LIMIT = 7.0
NEG_BIG = -1e30

LANES = 128
ROW_SUBLANES = 8
TOKEN_TILE = 256
PROJ_TILE = 512
MOE_BLOCK = 256
ATTN_HEAD_GROUP = 8
V7X_VMEM_LIMIT = 52 * 1024 * 1024


def _params(sem, vmem=V7X_VMEM_LIMIT):
    return pltpu.CompilerParams(dimension_semantics=sem, vmem_limit_bytes=vmem)


def _bdot(a, b):
    return jnp.dot(a.astype(BF16), b.astype(BF16), preferred_element_type=F32)


def _bdot_nt(a, b):
    return lax.dot_general(a.astype(BF16), b.astype(BF16), (((1,), (1,)), ((), ())),
                           preferred_element_type=F32)


def _bdot_tn(a, b):
    return lax.dot_general(a.astype(BF16), b.astype(BF16), (((0,), (0,)), ((), ())),
                           preferred_element_type=F32)


def _fdot(a, b):
    return jnp.dot(a, b, preferred_element_type=F32, precision=lax.Precision.HIGHEST)


def _rms(x, g):
    return x * lax.rsqrt(jnp.mean(x * x, axis=-1, keepdims=True) + NORM_EPS) * g


def _rms_heads(x, g, n_heads, hd):
    return jnp.concatenate([_rms(x[:, h * hd:(h + 1) * hd], g) for h in range(n_heads)], axis=-1)


def _sigmoid(x):
    return 1.0 / (1.0 + jnp.exp(-x))


def _silu(x):
    return x * _sigmoid(x)


def _softmax_rows(parts):
    m = parts[0].max(axis=-1, keepdims=True)
    for p in parts[1:]:
        m = jnp.maximum(m, p.max(axis=-1, keepdims=True))
    es = [jnp.exp(p - m) for p in parts]
    den = es[0].sum(axis=-1, keepdims=True)
    for e in es[1:]:
        den = den + e.sum(axis=-1, keepdims=True)
    return es, den


def _mod_kernel(c_ref, w_ref, b_ref, o_ref):
    o_ref[0] = _bdot(_silu(c_ref[...]), w_ref[0]) + b_ref[0]


def _mod_call(cond8, w_mod, b_mod):
    depth, d, d6 = w_mod.shape
    tn = 1024
    return pl.pallas_call(
        _mod_kernel,
        grid=(depth, d6 // tn),
        in_specs=[pl.BlockSpec((8, d), lambda i, j: (0, 0)),
                  pl.BlockSpec((1, d, tn), lambda i, j: (i, 0, j)),
                  pl.BlockSpec((1, 1, tn), lambda i, j: (i, 0, j))],
        out_specs=pl.BlockSpec((1, 8, tn), lambda i, j: (i, 0, j)),
        out_shape=jax.ShapeDtypeStruct((depth, 8, d6), F32),
        compiler_params=_params(("arbitrary", "arbitrary")),
        name="adaln_mod",
    )(cond8, w_mod, b_mod.reshape(depth, 1, d6))


def _inproj_kernel(x_ref, m_ref, g_ref, w_ref, o_ref):
    h = _rms(x_ref[...], g_ref[...]) * (1.0 + m_ref[0, 1:2, :]) + m_ref[0, 0:1, :]
    o_ref[...] = _bdot(h, w_ref[...])


def _inproj_call(x, modsel, g, w):
    n, d = x.shape
    wout = w.shape[1]
    tm = PROJ_TILE
    return pl.pallas_call(
        _inproj_kernel,
        grid=(n // tm,),
        in_specs=[pl.BlockSpec((tm, d), lambda i: (i, 0)),
                  pl.BlockSpec((1, 6, d), lambda i: (i * (PROJ_TILE // TOKEN_TILE), 0, 0)),
                  pl.BlockSpec((1, d), lambda i: (0, 0)),
                  pl.BlockSpec((d, wout), lambda i: (0, 0))],
        out_specs=pl.BlockSpec((tm, wout), lambda i: (i, 0)),
        out_shape=jax.ShapeDtypeStruct((n, wout), F32),
        compiler_params=_params(("arbitrary",)),
        name="inproj",
    )(x, modsel, g.reshape(1, d), w)


def _outproj_kernel(*refs, kind, n_pairs, n_ctx_tiles, n_heads, hd):
    in_ctx = pl.program_id(0) < n_ctx_tiles
    mix = [jnp.where(in_ctx, refs[2 * p][...], refs[2 * p + 1][...]) for p in range(n_pairs)]
    rest = refs[2 * n_pairs:]
    if kind == "plain":
        w_ref, x_ref, m_ref, y_ref = rest
        o = mix[0]
    else:
        gate_ref, on_ref, w_ref, x_ref, m_ref, y_ref = rest
        o = _rms_heads(mix[0] + mix[1], on_ref[...], n_heads, hd)
        o = o * _silu(gate_ref[...]) if kind == "hgrn" else _sigmoid(gate_ref[...]) * o
    y_ref[...] = x_ref[...] + m_ref[0, 2:3, :] * _bdot(o, w_ref[...])


def _outproj_call(kind, pairs, gate, w, x, modsel):
    n, d = x.shape
    tm = PROJ_TILE
    n_ctx_tiles = pairs[0][0].shape[0] // tm
    specs, args = [], []
    for a_ctx, a_lat in pairs:
        specs += [pl.BlockSpec((tm, d), lambda i: (jnp.minimum(i, n_ctx_tiles - 1), 0)),
                  pl.BlockSpec((tm, d), lambda i: (jnp.maximum(i - n_ctx_tiles, 0), 0))]
        args += [a_ctx, a_lat]
    if gate is not None:
        proj, cb, on = gate
        specs += [pl.BlockSpec((tm, d), lambda i: (i, cb)), pl.BlockSpec(on.shape, lambda i: (0, 0))]
        args += [proj, on]
    specs += [pl.BlockSpec(w.shape, lambda i: (0, 0)),
              pl.BlockSpec((tm, d), lambda i: (i, 0)),
              pl.BlockSpec((1, 6, d), lambda i: (i * (PROJ_TILE // TOKEN_TILE), 0, 0))]
    args += [w, x, modsel]
    body = functools.partial(_outproj_kernel, kind=kind, n_pairs=len(pairs), n_ctx_tiles=n_ctx_tiles,
                             n_heads=8, hd=128)
    return pl.pallas_call(
        body,
        grid=(n // tm,),
        in_specs=specs,
        out_specs=pl.BlockSpec((tm, d), lambda i: (i, 0)),
        out_shape=jax.ShapeDtypeStruct((n, d), F32),
        compiler_params=_params(("arbitrary",)),
        name="outproj_" + kind,
    )(*args)


def _gqa_ctx_kernel(p_ref, qn_ref, kn_ref, o_ref, k_ref, *, n_heads, n_kv, hd):
    rep = n_heads // n_kv
    scale = hd ** -0.5
    koff = n_heads * hd
    voff = koff + n_kv * hd
    ks = [_rms(p_ref[:, koff + g * hd: koff + (g + 1) * hd], kn_ref[...]) for g in range(n_kv)]
    k_ref[...] = jnp.concatenate(ks, axis=-1)
    ones = jnp.ones((p_ref.shape[0], hd), BF16)
    v_aug = [jnp.concatenate([p_ref[:, voff + g * hd: voff + (g + 1) * hd].astype(BF16), ones], axis=-1)
             for g in range(n_kv)]
    heads = range(n_heads)
    q = [(_rms(p_ref[:, h * hd:(h + 1) * hd], qn_ref[...]) * scale).astype(BF16) for h in heads]
    s = [_bdot_nt(q[h], ks[h // rep]) for h in heads]
    e = [jnp.exp(x - x.max(axis=-1, keepdims=True)).astype(BF16) for x in s]
    pv = [jnp.dot(e[h], v_aug[h // rep], preferred_element_type=F32) for h in heads]
    for h in heads:
        o_ref[:, h * hd:(h + 1) * hd] = pv[h][:, :hd] / pv[h][:, hd:]


def _gqa_ctx_call(proj, qn, kn, n_seq, t):
    n_heads, n_kv, hd = 8, 2, 128
    win = proj.shape[1]
    body = functools.partial(_gqa_ctx_kernel, n_heads=n_heads, n_kv=n_kv, hd=hd)
    return pl.pallas_call(
        body,
        grid=(n_seq,),
        in_specs=[pl.BlockSpec((t, win), lambda b: (b, 0)),
                  pl.BlockSpec((1, hd), lambda b: (0, 0)),
                  pl.BlockSpec((1, hd), lambda b: (0, 0))],
        out_specs=[pl.BlockSpec((t, n_heads * hd), lambda b: (b, 0)),
                   pl.BlockSpec((t, n_kv * hd), lambda b: (b, 0))],
        out_shape=[jax.ShapeDtypeStruct((n_seq * t, n_heads * hd), F32),
                   jax.ShapeDtypeStruct((n_seq * t, n_kv * hd), F32)],
        compiler_params=_params(("arbitrary",)),
        name="gqa_ctx",
    )(proj, qn.reshape(1, hd), kn.reshape(1, hd))


def _rope(x, cosd, sind):
    return x * cosd + pltpu.roll(x, x.shape[-1] // 2, 1) * sind


def _gqa_lat_kernel(pq_ref, pkv_ref, kc_ref, vc_ref, cq_ref, sq_ref, ck_ref, sk_ref, qn_ref, kn_ref,
                    o_ref, k_scr, v_scr, *, n_heads, n_kv, hd, t_ctx):
    rep = n_heads // n_kv
    scale = hd ** -0.5

    @pl.when(pl.program_id(1) == 0)
    def _():
        k_scr[0:t_ctx, :] = kc_ref[0].astype(BF16)
        v_scr[...] = jnp.ones_like(v_scr)
        for g in range(n_kv):
            k = _rms(pkv_ref[:, g * hd:(g + 1) * hd], kn_ref[...])
            k_scr[t_ctx:, g * hd:(g + 1) * hd] = _rope(k, ck_ref[...], sk_ref[...]).astype(BF16)
            v_scr[0:t_ctx, 2 * g * hd:(2 * g + 1) * hd] = vc_ref[0, :, g * hd:(g + 1) * hd].astype(BF16)
            v_scr[t_ctx:, 2 * g * hd:(2 * g + 1) * hd] = pkv_ref[:, (n_kv + g) * hd:(n_kv + g + 1) * hd].astype(BF16)

    qs = [(_rope(_rms(pq_ref[:, h * hd:(h + 1) * hd], qn_ref[...]), cq_ref[...], sq_ref[...]) * scale).astype(BF16)
          for h in range(n_heads)]
    for h0 in range(0, n_heads, ATTN_HEAD_GROUP):
        hs = range(h0, min(h0 + ATTN_HEAD_GROUP, n_heads))
        s = [_bdot_nt(qs[h], k_scr[:, (h // rep) * hd:(h // rep + 1) * hd]) for h in hs]
        e = [jnp.exp(x - x.max(axis=-1, keepdims=True)).astype(BF16) for x in s]
        pv = [jnp.dot(e[i], v_scr[:, 2 * (h // rep) * hd:2 * (h // rep + 1) * hd], preferred_element_type=F32)
              for i, h in enumerate(hs)]
        for i, h in enumerate(hs):
            o_ref[:, h * hd:(h + 1) * hd] = pv[i][:, :hd] / pv[i][:, hd:]


def _gqa_lat_call(proj, row0, n_seq, t, cache_k, cache_v, cosd, sind, qn, kn):
    n_heads, n_kv, hd = 8, 2, 128
    tq = 256
    t_ctx = cache_k.shape[1]
    nq = t // tq
    kvw = 2 * n_kv * hd
    qblk0 = row0 // tq
    sblk0 = row0 // t
    body = functools.partial(_gqa_lat_kernel, n_heads=n_heads, n_kv=n_kv, hd=hd, t_ctx=t_ctx)
    return pl.pallas_call(
        body,
        grid=(n_seq, nq),
        in_specs=[pl.BlockSpec((tq, n_heads * hd), lambda b, i: (qblk0 + b * nq + i, 0)),
                  pl.BlockSpec((t, kvw), lambda b, i: (sblk0 + b, (n_heads * hd) // kvw)),
                  pl.BlockSpec((1, t_ctx, n_kv * hd), lambda b, i: (b, 0, 0)),
                  pl.BlockSpec((1, t_ctx, n_kv * hd), lambda b, i: (b, 0, 0)),
                  pl.BlockSpec((tq, hd), lambda b, i: (i, 0)),
                  pl.BlockSpec((tq, hd), lambda b, i: (i, 0)),
                  pl.BlockSpec((t, hd), lambda b, i: (0, 0)),
                  pl.BlockSpec((t, hd), lambda b, i: (0, 0)),
                  pl.BlockSpec((1, hd), lambda b, i: (0, 0)),
                  pl.BlockSpec((1, hd), lambda b, i: (0, 0))],
        out_specs=pl.BlockSpec((tq, n_heads * hd), lambda b, i: (b * nq + i, 0)),
        out_shape=jax.ShapeDtypeStruct((n_seq * t, n_heads * hd), F32),
        scratch_shapes=[pltpu.VMEM((t_ctx + t, n_kv * hd), BF16),
                        pltpu.VMEM((t_ctx + t, 2 * n_kv * hd), BF16)],
        compiler_params=_params(("arbitrary", "arbitrary")),
        name="gqa_latent",
    )(proj, proj, cache_k, cache_v, cosd, sind, cosd, sind, qn.reshape(1, hd), kn.reshape(1, hd))


def _mha_ctx_kernel(q_ref, k_ref, v_ref, o_ref, *, hd):
    scale = hd ** -0.5
    n_heads = q_ref.shape[1] // hd
    ones = jnp.ones((v_ref.shape[0], hd), BF16)
    for h0 in range(0, n_heads, ATTN_HEAD_GROUP):
        sls = [slice(h * hd, (h + 1) * hd) for h in range(h0, min(h0 + ATTN_HEAD_GROUP, n_heads))]
        s = [_bdot_nt(q_ref[:, sl] * scale, k_ref[:, sl]) for sl in sls]
        e = [jnp.exp(x - x.max(axis=-1, keepdims=True)).astype(BF16) for x in s]
        pv = [jnp.dot(e[i], jnp.concatenate([v_ref[:, sl].astype(BF16), ones], axis=-1),
                      preferred_element_type=F32) for i, sl in enumerate(sls)]
        for i, sl in enumerate(sls):
            o_ref[:, sl] = pv[i][:, :hd] / pv[i][:, hd:]


def _mha_ctx_call(proj, n_seq, t):
    hd = 64
    d = proj.shape[1] // 3
    body = functools.partial(_mha_ctx_kernel, hd=hd)
    return pl.pallas_call(
        body,
        grid=(n_seq,),
        in_specs=[pl.BlockSpec((t, d), lambda b: (b, 0)),
                  pl.BlockSpec((t, d), lambda b: (b, 1)),
                  pl.BlockSpec((t, d), lambda b: (b, 2))],
        out_specs=pl.BlockSpec((t, d), lambda b: (b, 0)),
        out_shape=jax.ShapeDtypeStruct((n_seq * t, d), F32),
        compiler_params=_params(("arbitrary",)),
        name="mha_ctx",
    )(proj, proj, proj)


def _na_bias_kernel(rpb_ref, o_ref, *, n_rel_rows, n_rel_cols):
    h = pl.program_id(0)
    w_io = lax.broadcasted_iota(I32, (GRID_W, 2 * GRID_W), 0)
    lane = lax.broadcasted_iota(I32, (GRID_W, 2 * GRID_W), 1)
    ck = jnp.where(lane < GRID_W, lane, lane - GRID_W)
    c_start = jnp.clip(w_io - NA_COLS // 2, 0, GRID_W - NA_COLS)
    in_win = (ck >= c_start) & (ck < c_start + NA_COLS)
    rel = ck - w_io + (NA_COLS - 1)
    base = h * (n_rel_rows * n_rel_cols)
    tiles = []
    for j in range(n_rel_rows):
        acc = jnp.zeros((GRID_W, 2 * GRID_W), F32)
        for jj in range(n_rel_cols):
            acc = jnp.where(rel == jj, rpb_ref[base + j * n_rel_cols + jj], acc)
        tiles.append(jnp.where(in_win, acc, NEG_BIG))
    for j in range(n_rel_rows):
        hi = tiles[j + 1] if j + 1 < n_rel_rows else jnp.full((GRID_W, 2 * GRID_W), NEG_BIG, F32)
        o_ref[0, j] = jnp.where(lane < GRID_W, tiles[j], hi)


def _na_bias_call(rpb):
    n_heads, nrr, nrc = rpb.shape
    body = functools.partial(_na_bias_kernel, n_rel_rows=nrr, n_rel_cols=nrc)
    return pl.pallas_call(
        body,
        grid=(n_heads,),
        in_specs=[pl.BlockSpec(memory_space=pltpu.SMEM)],
        out_specs=pl.BlockSpec((1, nrr, GRID_W, 2 * GRID_W), lambda h: (h, 0, 0, 0)),
        out_shape=jax.ShapeDtypeStruct((n_heads, nrr, GRID_W, 2 * GRID_W), F32),
        compiler_params=_params(("arbitrary",)),
        name="na_bias",
    )(rpb.reshape(-1))


NA_QROWS = 4
NA_KROWS = 12


def _na_kernel(q_ref, k0_ref, k1_ref, k2_ref, v0_ref, v1_ref, v2_ref, kc_ref, vc_ref, tz_ref, o_ref,
               *, hd, n_grid_rows):
    scale = hd ** -0.5
    blk = pl.program_id(1)
    kstart = jnp.clip(blk * NA_QROWS - NA_ROWS // 2, 0, n_grid_rows - NA_KROWS)
    lane = lax.broadcasted_iota(I32, (GRID_W, 2 * GRID_W), 1)
    n_rel = tz_ref.shape[1]
    heads = range(q_ref.shape[1] // hd)
    sls = [slice(hh * hd, (hh + 1) * hd) for hh in heads]
    rel, pen = [], []
    for rq_l in range(NA_QROWS):
        rq = blk * NA_QROWS + rq_l
        r_start = jnp.clip(rq - NA_ROWS // 2, 0, n_grid_rows - NA_ROWS)
        rel.append([])
        pen.append([])
        for m in range(NA_KROWS // 2):
            rk = kstart + 2 * m
            rel[-1].append(jnp.clip(rk - rq + (NA_ROWS - 1), 0, n_rel - 1))
            ok0 = (rk >= r_start) & (rk < r_start + NA_ROWS)
            ok1 = (rk + 1 >= r_start) & (rk + 1 < r_start + NA_ROWS)
            pen[-1].append(jnp.where(lane < GRID_W, jnp.where(ok0, 0.0, NEG_BIG), jnp.where(ok1, 0.0, NEG_BIG)))
    bias = [jnp.concatenate([jnp.concatenate([tz_ref[hh, pl.ds(rel[r][m], 1)][0] + pen[r][m]
                                              for m in range(NA_KROWS // 2)], axis=-1)
                             for r in range(NA_QROWS)], axis=0) for hh in heads]
    q = [q_ref[:, sl] * scale for sl in sls]
    k_loc = [jnp.concatenate([k0_ref[:, sl], k1_ref[:, sl], k2_ref[:, sl]], axis=0) for sl in sls]
    v_loc = [jnp.concatenate([v0_ref[:, sl], v1_ref[:, sl], v2_ref[:, sl]], axis=0).astype(BF16) for sl in sls]
    v_ctx = [vc_ref[0, :, sl].astype(BF16) for sl in sls]
    s_loc = [_bdot_nt(q[h], k_loc[h]) + bias[h] for h in heads]
    s_ctx = [_bdot_nt(q[h], kc_ref[0, :, sls[h]]) for h in heads]
    m = [jnp.maximum(s_loc[h].max(axis=-1, keepdims=True), s_ctx[h].max(axis=-1, keepdims=True)) for h in heads]
    e_loc = [jnp.exp(s_loc[h] - m[h]).astype(BF16) for h in heads]
    e_ctx = [jnp.exp(s_ctx[h] - m[h]).astype(BF16) for h in heads]
    pv = [jnp.dot(e_loc[h], jnp.concatenate([v_loc[h], jnp.ones_like(v_loc[h])], axis=-1),
                  preferred_element_type=F32)
          + jnp.dot(e_ctx[h], jnp.concatenate([v_ctx[h], jnp.ones_like(v_ctx[h])], axis=-1),
                    preferred_element_type=F32) for h in heads]
    for h in heads:
        o_ref[:, sls[h]] = pv[h][:, :hd] / pv[h][:, hd:]


def _na_call(proj, row0, n_seq, t, cache_k, cache_v, tz):
    hd, cw = 64, 512
    d = proj.shape[1] // 3
    ncb = d // cw
    tq = NA_QROWS * GRID_W
    nq = t // tq
    n_grid_rows = t // GRID_W
    t_ctx = cache_k.shape[1]
    qblk0 = row0 // tq

    def kv_map(which, j):
        def index_map(b, i, c):
            ks = jnp.clip(i * NA_QROWS - NA_ROWS // 2, 0, n_grid_rows - NA_KROWS) // NA_QROWS
            return (qblk0 + b * nq + ks + j, which * ncb + c)
        return index_map

    body = functools.partial(_na_kernel, hd=hd, n_grid_rows=n_grid_rows)
    return pl.pallas_call(
        body,
        grid=(n_seq, nq, ncb),
        in_specs=[pl.BlockSpec((tq, cw), lambda b, i, c: (qblk0 + b * nq + i, c))]
                 + [pl.BlockSpec((tq, cw), kv_map(1, j)) for j in range(3)]
                 + [pl.BlockSpec((tq, cw), kv_map(2, j)) for j in range(3)]
                 + [pl.BlockSpec((1, t_ctx, cw), lambda b, i, c: (b, 0, c)),
                    pl.BlockSpec((1, t_ctx, cw), lambda b, i, c: (b, 0, c)),
                    pl.BlockSpec((cw // hd,) + tz.shape[1:], lambda b, i, c: (c, 0, 0, 0))],
        out_specs=pl.BlockSpec((tq, cw), lambda b, i, c: (b * nq + i, c)),
        out_shape=jax.ShapeDtypeStruct((n_seq * t, d), F32),
        compiler_params=_params(("arbitrary", "arbitrary", "arbitrary")),
        name="nbr_attn",
    )(proj, proj, proj, proj, proj, proj, proj, cache_k, cache_v, tz)


def _hgrn_kernel(q_ref, v_ref, f_ref, bf_ref, lb_ref, s0_ref, o_ref, s_ref, st_scr,
                 *, reverse, n_heads, dk, has_s0, n_blk):
    c = pl.program_id(1)
    L = HGRN_CHUNK
    tb = q_ref.shape[0]

    @pl.when(c == 0)
    def _():
        for h in range(n_heads):
            if has_s0:
                st_scr[h] = s0_ref[0, 0, h].T
            else:
                st_scr[h] = jnp.zeros_like(st_scr[h])

    row = lax.broadcasted_iota(I32, (L, L), 0)
    col = lax.broadcasted_iota(I32, (L, L), 1)
    tri = jnp.where((col >= row) if reverse else (col <= row), 1.0, 0.0).astype(F32)
    eye = jnp.where(row == col, 1.0, 0.0).astype(F32)
    halves = [L >> (i + 1) for i in range(L.bit_length() - 1)]
    same_pair = [jnp.where((row // (2 * hf)) == (col // (2 * hf)), 1.0, 0.0).astype(F32) for hf in halves]
    rio = lax.broadcasted_iota(I32, (L, q_ref.shape[1]), 0)
    is_query = [jnp.where(((rio & hf) == 0) if reverse else ((rio & hf) != 0), 1.0, 0.0).astype(F32)
                for hf in halves]
    sign = [2.0 * isq - 1.0 for isq in is_query]
    r8 = lax.broadcasted_iota(I32, (ROW_SUBLANES, 1), 0)
    lb = lb_ref[0]
    bf = bf_ref[0]
    n_chunks = tb // L

    def boundary_rows(cum, hf):
        blk = 2 * hf
        off = hf if reverse else hf - 1
        width = cum.shape[1]
        if blk >= ROW_SUBLANES:
            return jnp.concatenate([jnp.broadcast_to(cum[a + off:a + off + 1, :], (blk, width))
                                    for a in range(0, L, blk)], axis=0)
        groups = []
        for g in range(0, L, ROW_SUBLANES):
            ref = jnp.broadcast_to(cum[g + off:g + off + 1, :], (ROW_SUBLANES, width))
            for a in range(blk, ROW_SUBLANES, blk):
                ref = jnp.where(r8 >= a, jnp.broadcast_to(cum[g + a + off:g + a + off + 1, :],
                                                          (ROW_SUBLANES, width)), ref)
            groups.append(ref)
        return jnp.concatenate(groups, axis=0)

    def chunk_step(jj, carry):
        jc = (n_chunks - 1 - jj) if reverse else jj
        rs = pl.ds(pl.multiple_of(jc * L, L), L)
        q = _silu(q_ref[rs, :]) * (dk ** -0.5)
        v = v_ref[rs, :]
        f = lb + (1.0 - lb) * _sigmoid(f_ref[rs, :] + bf)
        logf = jnp.log(f)
        kk = 1.0 - f
        cum = _fdot(tri, logf)
        end = cum[0:1, :] if reverse else cum[L - 1:L, :]
        qd = q * jnp.exp(cum)
        kd = kk * jnp.exp(end - cum)
        e_end = jnp.exp(end)
        q_lv, k_lv = [], []
        for lv, hf in enumerate(halves):
            isq = is_query[lv]
            decay = jnp.exp(sign[lv] * (cum - boundary_rows(cum, hf)))
            dq = decay * isq
            q_lv.append((q * dq).astype(BF16))
            k_lv.append((kk * (decay - dq)).astype(BF16))
        qk_diag = q * kk
        hsl = [slice(h * dk, (h + 1) * dk) for h in range(n_heads)]
        inter = [_bdot_nt(qd[:, hs], st_scr[h]) for h, hs in enumerate(hsl)]
        attn = [eye * jnp.sum(qk_diag[:, hs], axis=-1, keepdims=True) for hs in hsl]
        for lv in range(len(halves)):
            prods = [_bdot_nt(q_lv[lv][:, hs], k_lv[lv][:, hs]) for hs in hsl]
            attn = [attn[h] + same_pair[lv] * prods[h] for h in range(n_heads)]
        outs = [inter[h] + _bdot(attn[h], v[:, hs]) for h, hs in enumerate(hsl)]
        o_ref[rs, :] = jnp.concatenate(outs, axis=-1)
        for h in range(n_heads):
            hs = slice(h * dk, (h + 1) * dk)
            st_scr[h] = st_scr[h] * e_end[:, hs] + _bdot_tn(v[:, hs], kd[:, hs])
        return carry

    lax.fori_loop(0, n_chunks, chunk_step, 0)

    @pl.when(c == n_blk - 1)
    def _():
        for h in range(n_heads):
            s_ref[0, 0, h] = st_scr[h].T


def _hgrn_call(proj, row0, n_seq, t, b_f, lb, s0, reverse):
    n_heads, dk = 8, 128
    d = n_heads * dk
    tb = 256
    n_blk = t // tb
    blk0 = row0 // tb
    di = 1 if reverse else 0
    has_s0 = s0 is not None
    if not has_s0:
        s0 = jnp.zeros((1, 2, n_heads, dk, dk), F32)

    def tok(b, c):
        return blk0 + b * n_blk + ((n_blk - 1 - c) if reverse else c)

    body = functools.partial(_hgrn_kernel, reverse=reverse, n_heads=n_heads, dk=dk,
                             has_s0=has_s0, n_blk=n_blk)
    return pl.pallas_call(
        body,
        grid=(n_seq, n_blk),
        in_specs=[pl.BlockSpec((tb, d), lambda b, c: (tok(b, c), 0)),
                  pl.BlockSpec((tb, d), lambda b, c: (tok(b, c), 1)),
                  pl.BlockSpec((tb, d), lambda b, c: (tok(b, c), 2 + di)),
                  pl.BlockSpec((1, 1, d), lambda b, c: (di, 0, 0)),
                  pl.BlockSpec((1, 1, d), lambda b, c: (di, 0, 0)),
                  pl.BlockSpec((1, 1, n_heads, dk, dk),
                               (lambda b, c: (b, di, 0, 0, 0)) if has_s0 else (lambda b, c: (0, 0, 0, 0, 0)))],
        out_specs=[pl.BlockSpec((tb, d), lambda b, c: (tok(b, c) - blk0, 0)),
                   pl.BlockSpec((1, 1, n_heads, dk, dk), lambda b, c: (b, 0, 0, 0, 0))],
        out_shape=[jax.ShapeDtypeStruct((n_seq * t, d), F32),
                   jax.ShapeDtypeStruct((n_seq, 1, n_heads, dk, dk), F32)],
        scratch_shapes=[pltpu.VMEM((n_heads, dk, dk), F32)],
        compiler_params=_params(("arbitrary", "arbitrary")),
        name="hgrn2_bw" if reverse else "hgrn2_fw",
    )(proj, proj, proj, b_f.reshape(2, 1, d), lb.reshape(2, 1, d), s0)


def _log_sigmoid(x):
    return jnp.minimum(x, 0.0) - jnp.log(1.0 + jnp.exp(-jnp.abs(x)))


def _mlstm_kernel(qf_ref, kf_ref, vf_ref, gf_ref, qb_ref, kb_ref, vb_ref, gb_ref, bg_ref, c0_ref, n0_ref, m0_ref,
                  of_ref, ob_ref, c_out, n_out, m_out, c_scr, n_scr, m_scr,
                  *, n_heads, dqk, dv, has_state, n_blk):
    c = pl.program_id(1)
    L = qf_ref.shape[0]

    @pl.when(c == 0)
    def _():
        if has_state:
            c_scr[...] = c0_ref[0]
            n_scr[...] = n0_ref[0]
            m_scr[...] = m0_ref[0]
        else:
            c_scr[...] = jnp.zeros_like(c_scr)
            n_scr[...] = jnp.zeros_like(n_scr)
            m_scr[...] = jnp.zeros_like(m_scr)

    row = lax.broadcasted_iota(I32, (L, L), 0)
    col = lax.broadcasted_iota(I32, (L, L), 1)
    lane = lax.broadcasted_iota(I32, (1, LANES), 1)
    heads = [(s, h) for s in range(2) for h in range(n_heads)]
    refs = [(qf_ref, kf_ref, vf_ref, gf_ref), (qb_ref, kb_ref, vb_ref, gb_ref)]
    causal_s, cum_s, cumt_s, gates_s, gatest_s, m_all = [], [], [], [], [], []
    for s in range(2):
        gates = GATE_SOFTCAP * jnp.tanh((refs[s][3][...] + bg_ref[...]) / GATE_SOFTCAP)
        causal = (col >= row) if s == 1 else (col <= row)
        cum = _fdot(jnp.where(causal, 1.0, 0.0).astype(F32), _log_sigmoid(gates))
        causal_s.append(causal)
        cum_s.append(cum)
        cumt_s.append(cum.T)
        gates_s.append(gates)
        gatest_s.append(gates.T)
        m_all.append(m_scr[s])
    i_off = [0, 2 * n_heads]
    f_off = [n_heads, 3 * n_heads]
    e_row = [L - 1, 0]
    c_prev = {it: c_scr[it[0], it[1]] for it in heads}
    n_prev = {it: n_scr[it[0], it[1]] for it in heads}
    qs = {(s, h): refs[s][0][:, h * dqk:(h + 1) * dqk] * (dqk ** -0.5) for s, h in heads}
    ks_ = {(s, h): refs[s][1][:, h * dqk:(h + 1) * dqk] for s, h in heads}
    vs = {(s, h): refs[s][2][:, h * dv:(h + 1) * dv] for s, h in heads}
    cum_c = {(s, h): cum_s[s][:, f_off[s] + h:f_off[s] + h + 1] for s, h in heads}
    cum_r = {(s, h): cumt_s[s][f_off[s] + h:f_off[s] + h + 1, :] for s, h in heads}
    i_c = {(s, h): gates_s[s][:, i_off[s] + h:i_off[s] + h + 1] for s, h in heads}
    i_r = {(s, h): gatest_s[s][i_off[s] + h:i_off[s] + h + 1, :] for s, h in heads}
    m_prev = {(s, h): m_all[s][0:1, h:h + 1] for s, h in heads}
    d = {h: jnp.where(causal_s[h[0]], cum_c[h] - cum_r[h] + i_r[h], -jnp.inf) for h in heads}
    m_inter = {h: cum_c[h] + m_prev[h] for h in heads}
    m_t = {h: jnp.maximum(m_inter[h], d[h].max(axis=-1, keepdims=True)) for h in heads}
    scores = {h: _bdot_nt(qs[h], ks_[h]) for h in heads}
    inter = {h: _bdot(qs[h], c_prev[h]) for h in heads}
    qn = {h: jnp.sum(qs[h] * n_prev[h], axis=-1, keepdims=True) for h in heads}
    w_inter = {h: jnp.exp(m_inter[h] - m_t[h]) for h in heads}
    qk = {h: scores[h] * jnp.exp(d[h] - m_t[h]) for h in heads}
    num = {h: w_inter[h] * inter[h] + _bdot(qk[h], vs[h]) for h in heads}
    den = {h: w_inter[h] * qn[h] + qk[h].sum(axis=-1, keepdims=True) for h in heads}
    outs = {h: num[h] / jnp.maximum(jnp.abs(den[h]), jnp.exp(-m_t[h])) for h in heads}
    end = {h: cum_c[h][e_row[h[0]]:e_row[h[0]] + 1, :] for h in heads}
    g_end_r = {h: end[h] - cum_r[h] + i_r[h] for h in heads}
    g_end_c = {h: end[h] - cum_c[h] + i_c[h] for h in heads}
    m_new = {h: jnp.maximum(end[h] + m_prev[h], g_end_r[h].max(axis=-1, keepdims=True)) for h in heads}
    w_old = {h: jnp.exp(end[h] + m_prev[h] - m_new[h]) for h in heads}
    kd = {h: ks_[h] * jnp.exp(g_end_c[h] - m_new[h]) for h in heads}
    c_new = {h: w_old[h] * c_prev[h] + _bdot_tn(kd[h], vs[h]) for h in heads}
    n_new = {h: w_old[h] * n_prev[h] + kd[h].sum(axis=0, keepdims=True) for h in heads}
    of_ref[...] = jnp.concatenate([outs[(0, h)] for h in range(n_heads)], axis=-1)
    ob_ref[...] = jnp.concatenate([outs[(1, h)] for h in range(n_heads)], axis=-1)
    for s in range(2):
        m_next = m_all[s]
        for h in range(n_heads):
            c_scr[s, h] = c_new[(s, h)]
            n_scr[s, h] = n_new[(s, h)]
            m_next = jnp.where(lane == h, m_new[(s, h)], m_next)
        m_scr[s] = m_next

    @pl.when(c == n_blk - 1)
    def _():
        c_out[0] = c_scr[...]
        n_out[0] = n_scr[...]
        m_out[0] = m_scr[...]


def _mlstm_call(proj, row0, n_seq, t, b_gates_pad, state):
    n_heads, dqk, dv = 8, 64, 128
    wq, wv = n_heads * dqk, n_heads * dv
    L = 256
    n_blk = t // L
    blk0 = row0 // L
    has_state = state is not None
    if has_state:
        c0, n0, m0 = state
        smap = lambda b, c: (b, 0, 0, 0, 0)
        mmap = lambda b, c: (b, 0, 0, 0)
    else:
        c0 = jnp.zeros((1, 2, n_heads, dqk, dv), F32)
        n0 = jnp.zeros((1, 2, n_heads, 1, dqk), F32)
        m0 = jnp.zeros((1, 2, 1, 128), F32)
        smap = lambda b, c: (0, 0, 0, 0, 0)
        mmap = lambda b, c: (0, 0, 0, 0)

    def tok(b, c, reverse):
        return blk0 + b * n_blk + ((n_blk - 1 - c) if reverse else c)

    body = functools.partial(_mlstm_kernel, n_heads=n_heads, dqk=dqk, dv=dv, has_state=has_state, n_blk=n_blk)
    gate_cb = (2 * wq + 2 * wv) // 128

    def token_specs(reverse):
        return [pl.BlockSpec((L, wq), lambda b, c: (tok(b, c, reverse), 0)),
                pl.BlockSpec((L, wq), lambda b, c: (tok(b, c, reverse), 1)),
                pl.BlockSpec((L, wv), lambda b, c: (tok(b, c, reverse), (2 * wq) // wv)),
                pl.BlockSpec((L, 128), lambda b, c: (tok(b, c, reverse), gate_cb))]

    return pl.pallas_call(
        body,
        grid=(n_seq, n_blk),
        in_specs=token_specs(False) + token_specs(True)
                 + [pl.BlockSpec((1, 128), lambda b, c: (0, 0)),
                    pl.BlockSpec((1, 2, n_heads, dqk, dv), smap),
                    pl.BlockSpec((1, 2, n_heads, 1, dqk), smap),
                    pl.BlockSpec((1, 2, 1, 128), mmap)],
        out_specs=[pl.BlockSpec((L, wv), lambda b, c: (tok(b, c, False) - blk0, 0)),
                   pl.BlockSpec((L, wv), lambda b, c: (tok(b, c, True) - blk0, 0)),
                   pl.BlockSpec((1, 2, n_heads, dqk, dv), lambda b, c: (b, 0, 0, 0, 0)),
                   pl.BlockSpec((1, 2, n_heads, 1, dqk), lambda b, c: (b, 0, 0, 0, 0)),
                   pl.BlockSpec((1, 2, 1, 128), lambda b, c: (b, 0, 0, 0))],
        out_shape=[jax.ShapeDtypeStruct((n_seq * t, wv), F32),
                   jax.ShapeDtypeStruct((n_seq * t, wv), F32),
                   jax.ShapeDtypeStruct((n_seq, 2, n_heads, dqk, dv), F32),
                   jax.ShapeDtypeStruct((n_seq, 2, n_heads, 1, dqk), F32),
                   jax.ShapeDtypeStruct((n_seq, 2, 1, 128), F32)],
        scratch_shapes=[pltpu.VMEM((2, n_heads, dqk, dv), F32),
                        pltpu.VMEM((2, n_heads, 1, dqk), F32),
                        pltpu.VMEM((2, 1, 128), F32)],
        compiler_params=_params(("arbitrary", "arbitrary")),
        name="mlstm",
    )(proj, proj, proj, proj, proj, proj, proj, proj, b_gates_pad, c0, n0, m0)


def _moe_input(x_ref, m_ref, g_ref):
    return _rms(x_ref[...], g_ref[...]) * (1.0 + m_ref[0, 4:5, :]) + m_ref[0, 3:4, :]


def _route_kernel(x_ref, m_ref, g_ref, wr_ref, br_ref, idx_ref, gate_ref, rank_ref, cnt_ref, carry_scr,
                  *, n_experts):
    i = pl.program_id(0)
    tm = x_ref.shape[0]

    @pl.when(i == 0)
    def _():
        carry_scr[...] = jnp.zeros_like(carry_scr)

    h = _moe_input(x_ref, m_ref, g_ref)
    w = wr_ref[...]
    h_hi, w_hi = h.astype(BF16), w.astype(BF16)
    h_lo = (h - h_hi.astype(F32)).astype(BF16)
    w_lo = (w - w_hi.astype(F32)).astype(BF16)
    logits = _bdot_nt(w_hi, h_hi) + _bdot_nt(w_lo, h_hi) + _bdot_nt(w_hi, h_lo) + br_ref[...]
    e_io = lax.broadcasted_iota(I32, (n_experts, tm), 0).astype(F32)
    work = logits
    vals, idxs = [], []
    chosen = jnp.zeros((n_experts, tm), F32)
    for _ in range(TOP_K):
        mx = work.max(axis=0, keepdims=True)
        ix = jnp.min(jnp.where(work == mx, e_io, float(n_experts)), axis=0, keepdims=True)
        hit = e_io == ix
        vals.append(mx)
        idxs.append(ix)
        chosen = jnp.where(hit, 1.0, chosen)
        work = jnp.where(hit, -jnp.inf, work)
    es = [jnp.exp(v - vals[0]) for v in vals]
    den = es[0] + es[1] + es[2] + es[3]
    srow = lax.broadcasted_iota(I32, (tm, tm), 0)
    scol = lax.broadcasted_iota(I32, (tm, tm), 1)
    before = jnp.where(srow < scol, 1.0, 0.0).astype(BF16)
    pos = jnp.dot(chosen.astype(BF16), before, preferred_element_type=F32) + carry_scr[...]
    ranks = [jnp.sum(jnp.where(e_io == ix, pos, 0.0), axis=0, keepdims=True) for ix in idxs]
    carry_scr[...] = carry_scr[...] + chosen.sum(axis=1, keepdims=True)
    idx_ref[...] = jnp.concatenate(idxs, axis=0).astype(I32)
    gate_ref[...] = jnp.concatenate([e / den for e in es], axis=0)
    rank_ref[...] = jnp.concatenate(ranks, axis=0).astype(I32)
    cnt_ref[...] = jnp.broadcast_to(carry_scr[...], cnt_ref.shape).astype(I32)


def _route_call(x, modsel, g, w_router, b_router):
    n, d = x.shape
    n_experts = w_router.shape[1]
    tm = TOKEN_TILE
    body = functools.partial(_route_kernel, n_experts=n_experts)
    return pl.pallas_call(
        body,
        grid=(n // tm,),
        in_specs=[pl.BlockSpec((tm, d), lambda i: (i, 0)),
                  pl.BlockSpec((1, 6, d), lambda i: (i, 0, 0)),
                  pl.BlockSpec((1, d), lambda i: (0, 0)),
                  pl.BlockSpec((n_experts, d), lambda i: (0, 0)),
                  pl.BlockSpec((n_experts, 1), lambda i: (0, 0))],
        out_specs=[pl.BlockSpec((TOP_K, tm), lambda i: (0, i)),
                   pl.BlockSpec((TOP_K, tm), lambda i: (0, i)),
                   pl.BlockSpec((TOP_K, tm), lambda i: (0, i)),
                   pl.BlockSpec((n_experts, 128), lambda i: (0, 0))],
        out_shape=[jax.ShapeDtypeStruct((TOP_K, n), I32),
                   jax.ShapeDtypeStruct((TOP_K, n), F32),
                   jax.ShapeDtypeStruct((TOP_K, n), I32),
                   jax.ShapeDtypeStruct((n_experts, 128), I32)],
        scratch_shapes=[pltpu.VMEM((n_experts, 1), F32)],
        compiler_params=_params(("arbitrary",)),
        name="moe_route",
    )(x, modsel, g.reshape(1, d), w_router.T, b_router.reshape(n_experts, 1))


def _slot_kernel(cnt_ref, idx_ref, rank_ref, dest_ref, binfo_ref, pad_ref, nused_ref, *, n_experts, n_blocks):
    cnt = cnt_ref[:, 0:1].astype(F32)
    padded = jnp.ceil(cnt * (1.0 / MOE_BLOCK)) * MOE_BLOCK
    er = lax.broadcasted_iota(I32, (n_experts, n_experts), 0)
    ec = lax.broadcasted_iota(I32, (n_experts, n_experts), 1)
    start_row = jnp.sum(jnp.where(er < ec, padded, 0.0), axis=0, keepdims=True)
    start_col = jnp.sum(jnp.where(er == ec, start_row, 0.0), axis=1, keepdims=True)
    end_col = start_col + padded
    idx = idx_ref[...]
    e_io = lax.broadcasted_iota(I32, (n_experts,) + idx.shape[1:], 0)
    rows = []
    for k in range(TOP_K):
        hit = e_io == idx[k:k + 1, :]
        rows.append(jnp.sum(jnp.where(hit, start_col, 0.0), axis=0, keepdims=True))
    dest_ref[...] = jnp.concatenate(rows, axis=0).astype(I32) + rank_ref[...]
    blk_start = (lax.broadcasted_iota(I32, (n_experts, n_blocks), 1) * MOE_BLOCK).astype(F32)
    n_done = jnp.sum(jnp.where(end_col <= blk_start, 1.0, 0.0), axis=0, keepdims=True)
    bexp = jnp.minimum(n_done, n_experts - 1.0)
    used_row = jnp.sum(jnp.where(er == ec, jnp.where(cnt > 0.0, 1.0, 0.0), 0.0), axis=0, keepdims=True)
    ecf = ec.astype(F32)
    next_col = jnp.min(jnp.where((ec > er) & (used_row > 0.0), ecf, float(n_experts)), axis=1, keepdims=True)
    ord_col = jnp.sum(jnp.where(ec < er, used_row, 0.0), axis=1, keepdims=True)
    par_col = ord_col - 2.0 * jnp.floor(ord_col * 0.5)
    mine = lax.broadcasted_iota(I32, (n_experts, n_blocks), 0).astype(F32) == bexp
    bnext = jnp.sum(jnp.where(mine, next_col, 0.0), axis=0, keepdims=True)
    bslot = jnp.sum(jnp.where(mine, par_col, 0.0), axis=0, keepdims=True)
    binfo_ref[...] = jnp.concatenate([bexp, bnext, bslot], axis=0).astype(I32)
    n_used = jnp.sum(padded, axis=0, keepdims=True) * (1.0 / MOE_BLOCK)
    nused_ref[...] = n_used.astype(I32)
    cnt_row = jnp.sum(jnp.where(er == ec, cnt, 0.0), axis=0, keepdims=True)
    padded_row = jnp.sum(jnp.where(er == ec, padded, 0.0), axis=0, keepdims=True)
    pad_ref[...] = jnp.concatenate([start_row + cnt_row, padded_row - cnt_row,
                                    jnp.broadcast_to(n_used, cnt_row.shape)], axis=0).astype(I32)


def _slot_call(counts, idx_t, rank_t, n_blocks):
    n_experts = counts.shape[0]
    n = idx_t.shape[1]
    tn = min(2048, n)
    body = functools.partial(_slot_kernel, n_experts=n_experts, n_blocks=n_blocks)
    return pl.pallas_call(
        body,
        grid=(n // tn,),
        in_specs=[pl.BlockSpec((n_experts, 128), lambda i: (0, 0)),
                  pl.BlockSpec((TOP_K, tn), lambda i: (0, i)),
                  pl.BlockSpec((TOP_K, tn), lambda i: (0, i))],
        out_specs=[pl.BlockSpec((TOP_K, tn), lambda i: (0, i)),
                   pl.BlockSpec((3, n_blocks), lambda i: (0, 0)),
                   pl.BlockSpec((3, n_experts), lambda i: (0, 0)),
                   pl.BlockSpec((1, 1), lambda i: (0, 0))],
        out_shape=[jax.ShapeDtypeStruct((TOP_K, n), I32),
                   jax.ShapeDtypeStruct((3, n_blocks), I32),
                   jax.ShapeDtypeStruct((3, n_experts), I32),
                   jax.ShapeDtypeStruct((1, 1), I32)],
        compiler_params=_params(("arbitrary",)),
        name="moe_slots",
    )(counts, idx_t, rank_t)


DMA_ISSUE_UNROLL = 8


def _to_row_tiles(ref, base, x):
    rows = x.shape[0]
    for c in range(ROW_SUBLANES):
        ref[pl.ds(base * ROW_SUBLANES + c, rows, stride=ROW_SUBLANES), :] = x[:, c * LANES:(c + 1) * LANES]


def _from_row_tiles(ref, base, rows, c):
    return ref[pl.ds(base * ROW_SUBLANES + c, rows, stride=ROW_SUBLANES), :]


def _row_tile(ref, r):
    return ref.at[pl.ds(pl.multiple_of(r * ROW_SUBLANES, ROW_SUBLANES), ROW_SUBLANES)]


def _zero_fill_padding(pad_ref, xs_ref, z_scr, sem):
    z_scr[...] = jnp.zeros_like(z_scr)
    n_experts = pad_ref.shape[1]
    bits = range(MOE_BLOCK.bit_length() - 2, -1, -1)

    def pieces(e):
        off, length = pad_ref[0, e], pad_ref[1, e]
        for bit in bits:
            size = 1 << bit
            done = (length >> (bit + 1)) << (bit + 1)
            copy = pltpu.make_async_copy(z_scr.at[pl.ds(0, size * ROW_SUBLANES)],
                                         xs_ref.at[pl.ds(pl.multiple_of((off + done) * ROW_SUBLANES, ROW_SUBLANES),
                                                         size * ROW_SUBLANES)], sem)
            yield (length & size) != 0, copy

    def tail_blocks():
        n_blocks = xs_ref.shape[0] // (MOE_BLOCK * ROW_SUBLANES)
        for b in range(n_blocks - n_experts, n_blocks):
            copy = pltpu.make_async_copy(z_scr, xs_ref.at[pl.ds(b * MOE_BLOCK * ROW_SUBLANES,
                                                                MOE_BLOCK * ROW_SUBLANES)], sem)
            yield b >= pad_ref[2, 0], copy

    def all_copies():
        for e in range(n_experts):
            yield from pieces(e)
        yield from tail_blocks()

    for needed, copy in all_copies():
        pl.when(needed)(copy.start)
    for needed, copy in all_copies():
        pl.when(needed)(copy.wait)


def _dispatch_kernel(pad_ref, dest_ref, x_ref, m_ref, g_ref, xs_ref, h_scr, z_scr, sem, *, n_tiles):
    i = pl.program_id(0)
    tm = x_ref.shape[0]

    @pl.when(i == 0)
    def _():
        _zero_fill_padding(pad_ref, xs_ref, z_scr, sem.at[2])

    slot = lax.rem(i, 2)
    src = h_scr.at[slot]
    _to_row_tiles(src, 0, _moe_input(x_ref, m_ref, g_ref))

    def start_row(r, carry):
        for k in range(TOP_K):
            pltpu.make_async_copy(_row_tile(src, r), _row_tile(xs_ref, dest_ref[0, 0, k * tm + r]),
                                  sem.at[slot]).start(priority=k % 2)
        return carry

    lax.fori_loop(0, tm, start_row, 0, unroll=DMA_ISSUE_UNROLL // TOP_K)

    def wait_tile(s):
        for _ in range(TOP_K):
            pltpu.make_async_copy(h_scr.at[s], xs_ref.at[pl.ds(0, tm * ROW_SUBLANES)], sem.at[s]).wait()

    @pl.when(i >= 1)
    def _():
        wait_tile(1 - slot)

    @pl.when(i == n_tiles - 1)
    def _():
        wait_tile(slot)


def _dispatch_call(pad_info, dest_tiles, x, modsel, g, n_slots):
    n, d = x.shape
    tm = TOKEN_TILE
    assert d == ROW_SUBLANES * LANES
    return pl.pallas_call(
        functools.partial(_dispatch_kernel, n_tiles=n // tm),
        grid=(n // tm,),
        in_specs=[pl.BlockSpec(memory_space=pltpu.SMEM),
                  pl.BlockSpec((1, 1, TOP_K * tm), lambda i: (i, 0, 0), memory_space=pltpu.SMEM),
                  pl.BlockSpec((tm, d), lambda i: (i, 0)),
                  pl.BlockSpec((1, 6, d), lambda i: (i, 0, 0)),
                  pl.BlockSpec((1, d), lambda i: (0, 0))],
        out_specs=pl.BlockSpec(memory_space=pl.ANY),
        out_shape=jax.ShapeDtypeStruct((n_slots * ROW_SUBLANES, LANES), F32),
        scratch_shapes=[pltpu.VMEM((2, tm * ROW_SUBLANES, LANES), F32),
                        pltpu.VMEM((MOE_BLOCK * ROW_SUBLANES, LANES), F32),
                        pltpu.SemaphoreType.DMA((3,))],
        compiler_params=_params(("arbitrary",)),
        name="moe_dispatch",
    )(pad_info, dest_tiles, x, modsel, g.reshape(1, d))


FFN_BLOCKS_PER_STEP = 4


def _ffn_kernel(binfo_ref, nused_ref, xs_ref, wgu_hbm, wdn_hbm, *rest, layer, n_experts, n_blocks):
    bias_refs = rest[:2 * FFN_BLOCKS_PER_STEP]
    ys_ref, wgu_f32, wdn_f32, wgu_scr, wdn_scr, sem = rest[2 * FFN_BLOCKS_PER_STEP:]
    d_ff = wdn_scr.shape[0]
    rows = MOE_BLOCK

    def weight_copies(e, slot):
        return (pltpu.make_async_copy(wgu_hbm.at[layer, e], wgu_f32.at[slot], sem.at[slot]),
                pltpu.make_async_copy(wdn_hbm.at[layer, e], wdn_f32.at[slot], sem.at[slot]))

    def one_block(u):
        b = pl.program_id(0) * FFN_BLOCKS_PER_STEP + u
        base = u * MOE_BLOCK
        bgu_ref, bdn_ref = bias_refs[2 * u], bias_refs[2 * u + 1]

        @pl.when(b < nused_ref[0])
        def _():
            e = binfo_ref[b]
            prev = binfo_ref[jnp.maximum(b - 1, 0)]
            slot = binfo_ref[2 * n_blocks + b]

            @pl.when((b == 0) | (e != prev))
            def _():
                @pl.when(b == 0)
                def _():
                    for cp in weight_copies(e, slot):
                        cp.start()

                nxt = binfo_ref[n_blocks + b]

                @pl.when(nxt < n_experts)
                def _():
                    for cp in weight_copies(nxt, 1 - slot):
                        cp.start()

                for cp in weight_copies(e, slot):
                    cp.wait()
                wgu_scr[...] = wgu_f32[slot].astype(BF16)
                wdn_scr[...] = wdn_f32[slot].astype(BF16)

            x = jnp.concatenate([_from_row_tiles(xs_ref, base, rows, c).astype(BF16)
                                 for c in range(ROW_SUBLANES)], axis=-1)
            gu = jnp.dot(x, wgu_scr[...], preferred_element_type=F32) + bgu_ref[0, 0]
            x_glu = jnp.minimum(gu[:, :d_ff], SWIGLU_LIMIT)
            x_lin = jnp.clip(gu[:, d_ff:], -SWIGLU_LIMIT, SWIGLU_LIMIT)
            hid = x_glu * _sigmoid(SWIGLU_ALPHA * x_glu) * (x_lin + 1.0)
            _to_row_tiles(ys_ref, base, jnp.dot(hid.astype(BF16), wdn_scr[...], preferred_element_type=F32)
                          + bdn_ref[0, 0])

        @pl.when(b >= nused_ref[0])
        def _():
            ys_ref[pl.ds(base * ROW_SUBLANES, rows * ROW_SUBLANES), :] = jnp.zeros(
                (rows * ROW_SUBLANES, LANES), F32)

    for u in range(FFN_BLOCKS_PER_STEP):
        one_block(u)


def _ffn_call(layer, block_info, n_used, xs, w_gu, b_gu, w_dn, b_dn):
    depth, n_experts, d, d_ff2 = w_gu.shape
    d_ff = d_ff2 // 2
    g = FFN_BLOCKS_PER_STEP
    step_rows = g * MOE_BLOCK * ROW_SUBLANES
    n_blocks = xs.shape[0] // (MOE_BLOCK * ROW_SUBLANES)
    assert n_blocks % g == 0

    def blk(b, nu):
        return jnp.maximum(jnp.minimum(b, nu[0] - 1), 0)

    def bias_specs(u):
        return [pl.BlockSpec((1, 1, 1, d_ff2), lambda s, bi, nu: (layer, bi[blk(g * s + u, nu)], 0, 0)),
                pl.BlockSpec((1, 1, 1, d), lambda s, bi, nu: (layer, bi[blk(g * s + u, nu)], 0, 0))]

    bias_args = [b_gu.reshape(depth, n_experts, 1, d_ff2), b_dn.reshape(depth, n_experts, 1, d)] * g
    grid_spec = pltpu.PrefetchScalarGridSpec(
        num_scalar_prefetch=2,
        grid=(n_blocks // g,),
        in_specs=[pl.BlockSpec((step_rows, LANES), lambda s, bi, nu: (blk(g * s, nu) // g, 0)),
                  pl.BlockSpec(memory_space=pl.ANY),
                  pl.BlockSpec(memory_space=pl.ANY)]
                 + [spec for u in range(g) for spec in bias_specs(u)],
        out_specs=pl.BlockSpec((step_rows, LANES), lambda s, bi, nu: (s, 0)),
        scratch_shapes=[pltpu.VMEM((2, d, d_ff2), F32), pltpu.VMEM((2, d_ff, d), F32),
                        pltpu.VMEM((d, d_ff2), BF16), pltpu.VMEM((d_ff, d), BF16),
                        pltpu.SemaphoreType.DMA((2,))],
    )
    body = functools.partial(_ffn_kernel, layer=layer, n_experts=n_experts, n_blocks=n_blocks)
    return pl.pallas_call(
        body,
        grid_spec=grid_spec,
        out_shape=jax.ShapeDtypeStruct(xs.shape, F32),
        compiler_params=_params(("arbitrary",)),
        name="moe_ffn",
    )(block_info, n_used, xs, w_gu, w_dn, *bias_args)


def _gather_expert_rows(dest_ref, dest_next_ref, ys_ref, gate_ref, buf, sem, n_tiles):
    i = pl.program_id(0)
    tm = gate_ref.shape[0]
    rows = TOP_K * tm

    def start_gathers(d_ref, slot):
        def start_pair(p, carry):
            for u in range(2):
                j = 2 * p + u
                pltpu.make_async_copy(_row_tile(ys_ref, d_ref[0, 0, j]), _row_tile(buf.at[slot], j),
                                      sem.at[slot]).start(priority=u)
            return carry

        lax.fori_loop(0, rows // 2, start_pair, 0, unroll=DMA_ISSUE_UNROLL // 2)

    slot = lax.rem(i, 2)

    @pl.when(i == 0)
    def _():
        start_gathers(dest_ref, 0)

    def wait_tile(s):
        pltpu.make_async_copy(ys_ref.at[pl.ds(0, rows * ROW_SUBLANES)], buf.at[s], sem.at[s]).wait()

    @pl.when(i + 1 < n_tiles)
    def _():
        start_gathers(dest_next_ref, 1 - slot)

    wait_tile(slot)
    cur = buf.at[slot]
    chunks = []
    for c in range(ROW_SUBLANES):
        y = gate_ref[:, 0:1] * _from_row_tiles(cur, 0, tm, c)
        for k in range(1, TOP_K):
            y = y + gate_ref[:, k:k + 1] * _from_row_tiles(cur, k * tm, tm, c)
        chunks.append(y)
    return jnp.concatenate(chunks, axis=-1)


def _combine_kernel(dest_ref, dest_next_ref, ys_ref, gate_ref, x_ref, m_ref, fg_ref, y_ref, buf, sem,
                    *, n_tiles):
    y = _gather_expert_rows(dest_ref, dest_next_ref, ys_ref, gate_ref, buf, sem, n_tiles)
    y_ref[...] = _rms(x_ref[...] + m_ref[0, 5:6, :] * y, fg_ref[...])


def _combine_inproj_kernel(dest_ref, dest_next_ref, ys_ref, gate_ref, x_ref, mp_ref, mc_ref, g_ref, w_ref,
                           x_out_ref, p_ref, buf, sem, *, n_tiles):
    y = _gather_expert_rows(dest_ref, dest_next_ref, ys_ref, gate_ref, buf, sem, n_tiles)
    x = x_ref[...] + mp_ref[0, 5:6, :] * y
    x_out_ref[...] = x
    h = _rms(x, g_ref[...]) * (1.0 + mc_ref[0, 1:2, :]) + mc_ref[0, 0:1, :]
    p_ref[...] = _bdot(h, w_ref[...])


def _moe_gather_specs(tm, n_tiles, d):
    return [pl.BlockSpec((1, 1, TOP_K * tm), lambda i: (i, 0, 0), memory_space=pltpu.SMEM),
            pl.BlockSpec((1, 1, TOP_K * tm), lambda i: (jnp.minimum(i + 1, n_tiles - 1), 0, 0),
                         memory_space=pltpu.SMEM),
            pl.BlockSpec(memory_space=pl.ANY),
            pl.BlockSpec((tm, TOP_K), lambda i: (i, 0)),
            pl.BlockSpec((tm, d), lambda i: (i, 0))]


def _moe_gather_scratch(tm):
    return [pltpu.VMEM((2, TOP_K * tm * ROW_SUBLANES, LANES), F32), pltpu.SemaphoreType.DMA((2,))]


def _combine_inproj_call(pending, x, modsel_prev, modsel, g, w):
    dest_tiles, ys, gates_nk = pending
    n, d = x.shape
    wout = w.shape[1]
    tm = TOKEN_TILE
    n_tiles = n // tm
    body = functools.partial(_combine_inproj_kernel, n_tiles=n_tiles)
    return pl.pallas_call(
        body,
        grid=(n_tiles,),
        in_specs=_moe_gather_specs(tm, n_tiles, d)
                 + [pl.BlockSpec((1, 6, d), lambda i: (i, 0, 0)),
                    pl.BlockSpec((1, 6, d), lambda i: (i, 0, 0)),
                    pl.BlockSpec((1, d), lambda i: (0, 0)),
                    pl.BlockSpec((d, wout), lambda i: (0, 0))],
        out_specs=[pl.BlockSpec((tm, d), lambda i: (i, 0)),
                   pl.BlockSpec((tm, wout), lambda i: (i, 0))],
        out_shape=[jax.ShapeDtypeStruct((n, d), F32),
                   jax.ShapeDtypeStruct((n, wout), F32)],
        scratch_shapes=_moe_gather_scratch(tm),
        compiler_params=_params(("arbitrary",)),
        name="moe_combine_inproj",
    )(dest_tiles, dest_tiles, ys, gates_nk, x, modsel_prev, modsel, g.reshape(1, d), w)


def _combine_call(pending, x, modsel, final_g):
    dest_tiles, ys, gates_nk = pending
    n, d = x.shape
    tm = TOKEN_TILE
    n_tiles = n // tm
    body = functools.partial(_combine_kernel, n_tiles=n_tiles)
    return pl.pallas_call(
        body,
        grid=(n_tiles,),
        in_specs=_moe_gather_specs(tm, n_tiles, d)
                 + [pl.BlockSpec((1, 6, d), lambda i: (i, 0, 0)),
                    pl.BlockSpec((1, d), lambda i: (0, 0))],
        out_specs=pl.BlockSpec((tm, d), lambda i: (i, 0)),
        out_shape=jax.ShapeDtypeStruct((n, d), F32),
        scratch_shapes=_moe_gather_scratch(tm),
        compiler_params=_params(("arbitrary",)),
        name="moe_combine",
    )(dest_tiles, dest_tiles, ys, gates_nk, x, modsel, final_g.reshape(1, d))


def _moe_experts(layer, x, modsel, g2, w_router, b_router, w_gu, b_gu, w_dn, b_dn):
    n, d = x.shape
    n_experts = w_router.shape[1]
    tm = TOKEN_TILE
    n_blocks = (n * TOP_K) // MOE_BLOCK + n_experts
    idx_t, gate_t, rank_t, counts = _route_call(x, modsel, g2, w_router, b_router)
    dest_t, block_info, pad_info, n_used = _slot_call(counts, idx_t, rank_t, n_blocks)
    dest_tiles = dest_t.reshape(TOP_K, n // tm, tm).transpose(1, 0, 2).reshape(n // tm, 1, TOP_K * tm)
    xs = _dispatch_call(pad_info, dest_tiles, x, modsel, g2, n_blocks * MOE_BLOCK)
    ys = _ffn_call(layer, block_info.reshape(3 * n_blocks), n_used.reshape(1), xs, w_gu, b_gu, w_dn, b_dn)
    return dest_tiles, ys, gate_t.T


def _rope_tables(t, hd):
    pos = np.arange(t)
    n_freq = hd // 4
    inv_freq = ROPE_THETA ** (-np.arange(n_freq, dtype=np.float32) / n_freq)
    ang = np.concatenate([(pos // GRID_W).astype(np.float32)[:, None] * inv_freq,
                          (pos % GRID_W).astype(np.float32)[:, None] * inv_freq], axis=-1)
    ang = jnp.asarray(ang, F32)
    cos, sin = jnp.cos(ang), jnp.sin(ang)
    return jnp.concatenate([cos, cos], axis=-1), jnp.concatenate([-sin, sin], axis=-1)


def kernel(x_prompt, x_sample, cache_k_a, cache_v_a, state_b, state_c_C, state_c_n, state_c_m, cache_k_d, cache_v_d, c, c_ctx, norm1_g, norm2_g, w_mod, b_mod, w_in_a, qnorm_a, knorm_a, w_out_a, w_in_b, b_f_b, lower_bounds_b, onorm_b, w_out_b, w_in_c, b_gates_c, onorm_c, w_out_c, w_in_d, rpb_d, w_out_d, w_router, b_router, w_gu, b_gu, w_dn, b_dn, final_g):
    n_ctx_seq, t_ctx, d = x_prompt.shape
    n_lat_seq, t_lat, _ = x_sample.shape
    depth = w_mod.shape[0]
    n_ctx = n_ctx_seq * t_ctx
    n_lat = n_lat_seq * t_lat
    n = n_ctx + n_lat
    tm = TOKEN_TILE
    assert t_ctx % tm == 0 and t_lat % tm == 0 and n_lat_seq + 1 <= 8
    assert n_ctx % PROJ_TILE == 0 and t_lat % PROJ_TILE == 0

    lb_cum = jnp.cumsum(jax.nn.softmax(lower_bounds_b.astype(F32), axis=0), axis=0)
    lb_all = lb_cum - lb_cum[0]

    cond8 = jnp.zeros((8, d), F32).at[0].set(c_ctx).at[1:1 + n_lat_seq].set(c)
    mod = _mod_call(cond8, w_mod, b_mod)
    tile_row = np.concatenate([np.zeros(n_ctx // tm, np.int32),
                               1 + np.repeat(np.arange(n_lat_seq, dtype=np.int32), t_lat // tm)])

    x = jnp.concatenate([x_prompt.reshape(n_ctx, d), x_sample.reshape(n_lat, d)], axis=0)
    outs = {}
    pending = None
    modsel = None

    def inproj(x, modsel, g, w):
        if pending is None:
            return x, _inproj_call(x, modsel, g, w)
        return _combine_inproj_call(pending, x, modsel_prev, modsel, g, w)

    for i in range(depth):
        kind = i % 4
        j = i // 4
        modsel_prev = modsel
        modsel = mod[i].reshape(8, 6, d)[tile_row]
        if kind == 0:
            x, proj = inproj(x, modsel, norm1_g[i],w_in_a[j].astype(BF16))
            o_ctx, k_new = _gqa_ctx_call(proj, qnorm_a[j], knorm_a[j], n_ctx_seq, t_ctx)
            cosd, sind = _rope_tables(t_lat, 128)
            o_lat = _gqa_lat_call(proj, n_ctx, n_lat_seq, t_lat,
                                  cache_k_a[:, j].reshape(n_lat_seq, -1, 256),
                                  cache_v_a[:, j].reshape(n_lat_seq, -1, 256), cosd, sind,
                                  qnorm_a[j], knorm_a[j])
            outs["k_a"] = k_new.reshape(n_ctx_seq, 1, t_ctx, 2, 128)
            outs["v_a"] = proj[:n_ctx, 1280:1536].reshape(n_ctx_seq, 1, t_ctx, 2, 128)
            x = _outproj_call("plain", [(o_ctx, o_lat)], None, w_out_a[j].astype(BF16), x, modsel)
        elif kind == 1:
            x, proj = inproj(x, modsel, norm1_g[i],w_in_b[j].astype(BF16))
            o_dirs, s_dirs = [], []
            for reverse in (False, True):
                oc, sc = _hgrn_call(proj, 0, n_ctx_seq, t_ctx, b_f_b[j], lb_all[i], None, reverse)
                ol, _ = _hgrn_call(proj, n_ctx, n_lat_seq, t_lat, b_f_b[j], lb_all[i], state_b[:, j], reverse)
                o_dirs.append((oc, ol))
                s_dirs.append(sc)
            outs["s_b"] = jnp.concatenate(s_dirs, axis=1)[:, None]
            x = _outproj_call("hgrn", o_dirs, (proj, 4, onorm_b[j].reshape(1, 128)),
                              w_out_b[j].astype(BF16), x, modsel)
        elif kind == 2:
            w_c = jnp.pad(w_in_c[j], ((0, 0), (0, 128 - 32))).astype(BF16)
            bg = jnp.pad(b_gates_c[j].reshape(1, 32), ((0, 0), (0, 128 - 32)))
            x, proj = inproj(x, modsel, norm1_g[i],w_c)
            state = (state_c_C[:, j], state_c_n[:, j][:, :, :, None, :],
                     jnp.pad(state_c_m[:, j], ((0, 0), (0, 0), (0, 120)))[:, :, None, :])
            ocf, ocb, cc, nc, mc = _mlstm_call(proj, 0, n_ctx_seq, t_ctx, bg, None)
            olf, olb, _, _, _ = _mlstm_call(proj, n_ctx, n_lat_seq, t_lat, bg, state)
            o_dirs = [(ocf, olf), (ocb, olb)]
            outs["c_C"] = cc[:, None]
            outs["c_n"] = nc[:, None, :, :, 0, :]
            outs["c_m"] = mc[:, None, :, 0, :8]
            x = _outproj_call("mlstm", o_dirs, (proj, 2, onorm_c[j].reshape(1, 128)),
                              w_out_c[j].astype(BF16), x, modsel)
        else:
            x, proj = inproj(x, modsel, norm1_g[i],w_in_d[j].astype(BF16))
            o_ctx = _mha_ctx_call(proj, n_ctx_seq, t_ctx)
            tz = _na_bias_call(rpb_d[j])
            o_lat = _na_call(proj, n_ctx, n_lat_seq, t_lat,
                             cache_k_d[:, j].reshape(n_lat_seq, -1, d),
                             cache_v_d[:, j].reshape(n_lat_seq, -1, d), tz)
            outs["k_d"] = proj[:n_ctx, d:2 * d].reshape(n_ctx_seq, 1, t_ctx, 16, 64)
            outs["v_d"] = proj[:n_ctx, 2 * d:3 * d].reshape(n_ctx_seq, 1, t_ctx, 16, 64)
            x = _outproj_call("plain", [(o_ctx, o_lat)], None, w_out_d[j].astype(BF16), x, modsel)
        pending = _moe_experts(i, x, modsel, norm2_g[i], w_router[i], b_router[i], w_gu, b_gu, w_dn, b_dn)
    x = _combine_call(pending, x, modsel, final_g)

    y_prompt = x[:n_ctx].reshape(n_ctx_seq, t_ctx, d)
    y_sample = x[n_ctx:].reshape(n_lat_seq, t_lat, d)
    return (y_prompt, y_sample, outs["k_a"], outs["v_a"], outs["s_b"], outs["c_C"], outs["c_n"], outs["c_m"],
            outs["k_d"], outs["v_d"])
```

```python
import functools

import numpy as np
import jax
import jax.numpy as jnp
from jax import lax
from jax.experimental import pallas as pl
from jax.experimental.pallas import tpu as pltpu

F32 = jnp.float32
BF16 = jnp.bfloat16
I32 = jnp.int32

NORM_EPS = 1e-6
GRID_W = 64
ROPE_THETA = 10000.0
TOP_K = 4
HGRN_CHUNK = 128
GATE_SOFTCAP = 15.0
NA_ROWS = 8
NA_COLS = 16
SWIGLU_ALPHA = 1.702
SWIGLU_LIMIT = 7.0
NEG_BIG = -1e30

LANES = 128
ROW_SUBLANES = 8
TOKEN_TILE = 256
PROJ_TILE = 512
MOE_BLOCK = 256
ATTN_HEAD_GROUP = 8
V7X_VMEM_LIMIT = 52 * 1024 * 1024


def _params(sem, vmem=V7X_VMEM_LIMIT):
    return pltpu.CompilerParams(dimension_semantics=sem, vmem_limit_bytes=vmem)


def _bdot(a, b):
    return jnp.dot(a.astype(BF16), b.astype(BF16), preferred_element_type=F32)


def _bdot_nt(a, b):
    return lax.dot_general(a.astype(BF16), b.astype(BF16), (((1,), (1,)), ((), ())),
                           preferred_element_type=F32)


def _bdot_tn(a, b):
    return lax.dot_general(a.astype(BF16), b.astype(BF16), (((0,), (0,)), ((), ())),
                           preferred_element_type=F32)


def _fdot(a, b):
    return jnp.dot(a, b, preferred_element_type=F32, precision=lax.Precision.HIGHEST)


def _rms(x, g):
    return x * lax.rsqrt(jnp.mean(x * x, axis=-1, keepdims=True) + NORM_EPS) * g


def _rms_heads(x, g, n_heads, hd):
    return jnp.concatenate([_rms(x[:, h * hd:(h + 1) * hd], g) for h in range(n_heads)], axis=-1)


def _sigmoid(x):
    return 1.0 / (1.0 + jnp.exp(-x))


def _silu(x):
    return x * _sigmoid(x)


def _softmax_rows(parts):
    m = parts[0].max(axis=-1, keepdims=True)
    for p in parts[1:]:
        m = jnp.maximum(m, p.max(axis=-1, keepdims=True))
    es = [jnp.exp(p - m) for p in parts]
    den = es[0].sum(axis=-1, keepdims=True)
    for e in es[1:]:
        den = den + e.sum(axis=-1, keepdims=True)
    return es, den


def _mod_kernel(c_ref, w_ref, b_ref, o_ref):
    o_ref[0] = _bdot(_silu(c_ref[...]), w_ref[0]) + b_ref[0]


def _mod_call(cond8, w_mod, b_mod):
    depth, d, d6 = w_mod.shape
    tn = 1024
    return pl.pallas_call(
        _mod_kernel,
        grid=(depth, d6 // tn),
        in_specs=[pl.BlockSpec((8, d), lambda i, j: (0, 0)),
                  pl.BlockSpec((1, d, tn), lambda i, j: (i, 0, j)),
                  pl.BlockSpec((1, 1, tn), lambda i, j: (i, 0, j))],
        out_specs=pl.BlockSpec((1, 8, tn), lambda i, j: (i, 0, j)),
        out_shape=jax.ShapeDtypeStruct((depth, 8, d6), F32),
        compiler_params=_params(("arbitrary", "arbitrary")),
        name="adaln_mod",
    )(cond8, w_mod, b_mod.reshape(depth, 1, d6))


def _inproj_kernel(x_ref, m_ref, g_ref, w_ref, o_ref):
    h = _rms(x_ref[...], g_ref[...]) * (1.0 + m_ref[0, 1:2, :]) + m_ref[0, 0:1, :]
    o_ref[...] = _bdot(h, w_ref[...])


def _inproj_call(x, modsel, g, w):
    n, d = x.shape
    wout = w.shape[1]
    tm = PROJ_TILE
    return pl.pallas_call(
        _inproj_kernel,
        grid=(n // tm,),
        in_specs=[pl.BlockSpec((tm, d), lambda i: (i, 0)),
                  pl.BlockSpec((1, 6, d), lambda i: (i * (PROJ_TILE // TOKEN_TILE), 0, 0)),
                  pl.BlockSpec((1, d), lambda i: (0, 0)),
                  pl.BlockSpec((d, wout), lambda i: (0, 0))],
        out_specs=pl.BlockSpec((tm, wout), lambda i: (i, 0)),
        out_shape=jax.ShapeDtypeStruct((n, wout), F32),
        compiler_params=_params(("arbitrary",)),
        name="inproj",
    )(x, modsel, g.reshape(1, d), w)


def _outproj_kernel(*refs, kind, n_pairs, n_ctx_tiles, n_heads, hd):
    in_ctx = pl.program_id(0) < n_ctx_tiles
    mix = [jnp.where(in_ctx, refs[2 * p][...], refs[2 * p + 1][...]) for p in range(n_pairs)]
    rest = refs[2 * n_pairs:]
    if kind == "plain":
        w_ref, x_ref, m_ref, y_ref = rest
        o = mix[0]
    else:
        gate_ref, on_ref, w_ref, x_ref, m_ref, y_ref = rest
        o = _rms_heads(mix[0] + mix[1], on_ref[...], n_heads, hd)
        o = o * _silu(gate_ref[...]) if kind == "hgrn" else _sigmoid(gate_ref[...]) * o
    y_ref[...] = x_ref[...] + m_ref[0, 2:3, :] * _bdot(o, w_ref[...])


def _outproj_call(kind, pairs, gate, w, x, modsel):
    n, d = x.shape
    tm = PROJ_TILE
    n_ctx_tiles = pairs[0][0].shape[0] // tm
    specs, args = [], []
    for a_ctx, a_lat in pairs:
        specs += [pl.BlockSpec((tm, d), lambda i: (jnp.minimum(i, n_ctx_tiles - 1), 0)),
                  pl.BlockSpec((tm, d), lambda i: (jnp.maximum(i - n_ctx_tiles, 0), 0))]
        args += [a_ctx, a_lat]
    if gate is not None:
        proj, cb, on = gate
        specs += [pl.BlockSpec((tm, d), lambda i: (i, cb)), pl.BlockSpec(on.shape, lambda i: (0, 0))]
        args += [proj, on]
    specs += [pl.BlockSpec(w.shape, lambda i: (0, 0)),
              pl.BlockSpec((tm, d), lambda i: (i, 0)),
              pl.BlockSpec((1, 6, d), lambda i: (i * (PROJ_TILE // TOKEN_TILE), 0, 0))]
    args += [w, x, modsel]
    body = functools.partial(_outproj_kernel, kind=kind, n_pairs=len(pairs), n_ctx_tiles=n_ctx_tiles,
                             n_heads=8, hd=128)
    return pl.pallas_call(
        body,
        grid=(n // tm,),
        in_specs=specs,
        out_specs=pl.BlockSpec((tm, d), lambda i: (i, 0)),
        out_shape=jax.ShapeDtypeStruct((n, d), F32),
        compiler_params=_params(("arbitrary",)),
        name="outproj_" + kind,
    )(*args)


def _gqa_ctx_kernel(p_ref, qn_ref, kn_ref, o_ref, k_ref, *, n_heads, n_kv, hd):
    rep = n_heads // n_kv
    scale = hd ** -0.5
    koff = n_heads * hd
    voff = koff + n_kv * hd
    ks = [_rms(p_ref[:, koff + g * hd: koff + (g + 1) * hd], kn_ref[...]) for g in range(n_kv)]
    k_ref[...] = jnp.concatenate(ks, axis=-1)
    for h in range(n_heads):
        g = h // rep
        q = _rms(p_ref[:, h * hd:(h + 1) * hd], qn_ref[...])
        s = _bdot_nt(q, ks[g]) * scale
        (e,), den = _softmax_rows([s])
        o_ref[:, h * hd:(h + 1) * hd] = _bdot(e / den, p_ref[:, voff + g * hd: voff + (g + 1) * hd]).astype(BF16)


def _gqa_ctx_call(proj, qn, kn, n_seq, t):
    n_heads, n_kv, hd = 8, 2, 128
    win = proj.shape[1]
    body = functools.partial(_gqa_ctx_kernel, n_heads=n_heads, n_kv=n_kv, hd=hd)
    return pl.pallas_call(
        body,
        grid=(n_seq,),
        in_specs=[pl.BlockSpec((t, win), lambda b: (b, 0)),
                  pl.BlockSpec((1, hd), lambda b: (0, 0)),
                  pl.BlockSpec((1, hd), lambda b: (0, 0))],
        out_specs=[pl.BlockSpec((t, n_heads * hd), lambda b: (b, 0)),
                   pl.BlockSpec((t, n_kv * hd), lambda b: (b, 0))],
        out_shape=[jax.ShapeDtypeStruct((n_seq * t, n_heads * hd), BF16),
                   jax.ShapeDtypeStruct((n_seq * t, n_kv * hd), F32)],
        compiler_params=_params(("arbitrary",)),
        name="gqa_ctx",
    )(proj, qn.reshape(1, hd), kn.reshape(1, hd))


def _rope(x, cosd, sind):
    return x * cosd + pltpu.roll(x, x.shape[-1] // 2, 1) * sind


def _gqa_lat_kernel(pq_ref, pkv_ref, kc_ref, vc_ref, cq_ref, sq_ref, ck_ref, sk_ref, qn_ref, kn_ref,
                    o_ref, k_scr, v_scr, *, n_heads, n_kv, hd, t_ctx):
    rep = n_heads // n_kv
    scale = hd ** -0.5

    @pl.when(pl.program_id(1) == 0)
    def _():
        k_scr[0:t_ctx, :] = kc_ref[0].astype(BF16)
        v_scr[...] = jnp.ones_like(v_scr)
        for g in range(n_kv):
            k = _rms(pkv_ref[:, g * hd:(g + 1) * hd], kn_ref[...])
            k_scr[t_ctx:, g * hd:(g + 1) * hd] = _rope(k, ck_ref[...], sk_ref[...]).astype(BF16)
            v_scr[0:t_ctx, 2 * g * hd:(2 * g + 1) * hd] = vc_ref[0, :, g * hd:(g + 1) * hd].astype(BF16)
            v_scr[t_ctx:, 2 * g * hd:(2 * g + 1) * hd] = pkv_ref[:, (n_kv + g) * hd:(n_kv + g + 1) * hd].astype(BF16)

    qs = [(_rope(_rms(pq_ref[:, h * hd:(h + 1) * hd], qn_ref[...]), cq_ref[...], sq_ref[...]) * scale).astype(BF16)
          for h in range(n_heads)]
    for h0 in range(0, n_heads, ATTN_HEAD_GROUP):
        hs = range(h0, min(h0 + ATTN_HEAD_GROUP, n_heads))
        s = [_bdot_nt(qs[h], k_scr[:, (h // rep) * hd:(h // rep + 1) * hd]) for h in hs]
        e = [jnp.exp(x - x.max(axis=-1, keepdims=True)).astype(BF16) for x in s]
        pv = [jnp.dot(e[i], v_scr[:, 2 * (h // rep) * hd:2 * (h // rep + 1) * hd], preferred_element_type=F32)
              for i, h in enumerate(hs)]
        for i, h in enumerate(hs):
            o_ref[:, h * hd:(h + 1) * hd] = (pv[i][:, :hd] / pv[i][:, hd:]).astype(BF16)


def _gqa_lat_call(proj, row0, n_seq, t, cache_k, cache_v, cosd, sind, qn, kn):
    n_heads, n_kv, hd = 8, 2, 128
    tq = 256
    t_ctx = cache_k.shape[1]
    nq = t // tq
    kvw = 2 * n_kv * hd
    qblk0 = row0 // tq
    sblk0 = row0 // t
    body = functools.partial(_gqa_lat_kernel, n_heads=n_heads, n_kv=n_kv, hd=hd, t_ctx=t_ctx)
    return pl.pallas_call(
        body,
        grid=(n_seq, nq),
        in_specs=[pl.BlockSpec((tq, n_heads * hd), lambda b, i: (qblk0 + b * nq + i, 0)),
                  pl.BlockSpec((t, kvw), lambda b, i: (sblk0 + b, (n_heads * hd) // kvw)),
                  pl.BlockSpec((1, t_ctx, n_kv * hd), lambda b, i: (b, 0, 0)),
                  pl.BlockSpec((1, t_ctx, n_kv * hd), lambda b, i: (b, 0, 0)),
                  pl.BlockSpec((tq, hd), lambda b, i: (i, 0)),
                  pl.BlockSpec((tq, hd), lambda b, i: (i, 0)),
                  pl.BlockSpec((t, hd), lambda b, i: (0, 0)),
                  pl.BlockSpec((t, hd), lambda b, i: (0, 0)),
                  pl.BlockSpec((1, hd), lambda b, i: (0, 0)),
                  pl.BlockSpec((1, hd), lambda b, i: (0, 0))],
        out_specs=pl.BlockSpec((tq, n_heads * hd), lambda b, i: (b * nq + i, 0)),
        out_shape=jax.ShapeDtypeStruct((n_seq * t, n_heads * hd), BF16),
        scratch_shapes=[pltpu.VMEM((t_ctx + t, n_kv * hd), BF16),
                        pltpu.VMEM((t_ctx + t, 2 * n_kv * hd), BF16)],
        compiler_params=_params(("arbitrary", "arbitrary")),
        name="gqa_latent",
    )(proj, proj, cache_k, cache_v, cosd, sind, cosd, sind, qn.reshape(1, hd), kn.reshape(1, hd))


def _mha_ctx_kernel(q_ref, k_ref, v_ref, o_ref, *, hd):
    scale = hd ** -0.5
    n_heads = q_ref.shape[1] // hd
    ones = jnp.ones((v_ref.shape[0], hd), BF16)
    for h0 in range(0, n_heads, ATTN_HEAD_GROUP):
        sls = [slice(h * hd, (h + 1) * hd) for h in range(h0, min(h0 + ATTN_HEAD_GROUP, n_heads))]
        s = [_bdot_nt(q_ref[:, sl] * scale, k_ref[:, sl]) for sl in sls]
        e = [jnp.exp(x - x.max(axis=-1, keepdims=True)).astype(BF16) for x in s]
        pv = [jnp.dot(e[i], jnp.concatenate([v_ref[:, sl].astype(BF16), ones], axis=-1),
                      preferred_element_type=F32) for i, sl in enumerate(sls)]
        o_ref[:, sls[0].start:sls[-1].stop] = jnp.concatenate(
            [pv[i][:, :hd] / pv[i][:, hd:] for i in range(len(sls))], axis=-1).astype(BF16)


def _mha_ctx_call(proj, n_seq, t):
    hd = 64
    d = proj.shape[1] // 3
    body = functools.partial(_mha_ctx_kernel, hd=hd)
    return pl.pallas_call(
        body,
        grid=(n_seq,),
        in_specs=[pl.BlockSpec((t, d), lambda b: (b, 0)),
                  pl.BlockSpec((t, d), lambda b: (b, 1)),
                  pl.BlockSpec((t, d), lambda b: (b, 2))],
        out_specs=pl.BlockSpec((t, d), lambda b: (b, 0)),
        out_shape=jax.ShapeDtypeStruct((n_seq * t, d), BF16),
        compiler_params=_params(("arbitrary",)),
        name="mha_ctx",
    )(proj, proj, proj)


def _na_bias_kernel(rpb_ref, o_ref, *, n_rel_rows, n_rel_cols):
    h = pl.program_id(0)
    w_io = lax.broadcasted_iota(I32, (GRID_W, 2 * GRID_W), 0)
    lane = lax.broadcasted_iota(I32, (GRID_W, 2 * GRID_W), 1)
    ck = jnp.where(lane < GRID_W, lane, lane - GRID_W)
    c_start = jnp.clip(w_io - NA_COLS // 2, 0, GRID_W - NA_COLS)
    in_win = (ck >= c_start) & (ck < c_start + NA_COLS)
    rel = ck - w_io + (NA_COLS - 1)
    base = h * (n_rel_rows * n_rel_cols)
    tiles = []
    for j in range(n_rel_rows):
        acc = jnp.zeros((GRID_W, 2 * GRID_W), F32)
        for jj in range(n_rel_cols):
            acc = jnp.where(rel == jj, rpb_ref[base + j * n_rel_cols + jj], acc)
        tiles.append(jnp.where(in_win, acc, NEG_BIG))
    for j in range(n_rel_rows):
        hi = tiles[j + 1] if j + 1 < n_rel_rows else jnp.full((GRID_W, 2 * GRID_W), NEG_BIG, F32)
        o_ref[0, j] = jnp.where(lane < GRID_W, tiles[j], hi)


def _na_bias_call(rpb):
    n_heads, nrr, nrc = rpb.shape
    body = functools.partial(_na_bias_kernel, n_rel_rows=nrr, n_rel_cols=nrc)
    return pl.pallas_call(
        body,
        grid=(n_heads,),
        in_specs=[pl.BlockSpec(memory_space=pltpu.SMEM)],
        out_specs=pl.BlockSpec((1, nrr, GRID_W, 2 * GRID_W), lambda h: (h, 0, 0, 0)),
        out_shape=jax.ShapeDtypeStruct((n_heads, nrr, GRID_W, 2 * GRID_W), F32),
        compiler_params=_params(("arbitrary",)),
        name="na_bias",
    )(rpb.reshape(-1))


NA_QROWS = 4
NA_KROWS = 12


def _na_kernel(q_ref, k0_ref, k1_ref, k2_ref, v0_ref, v1_ref, v2_ref, kc_ref, vc_ref, tz_ref, o_ref,
               *, hd, n_grid_rows):
    scale = hd ** -0.5
    blk = pl.program_id(1)
    kstart = jnp.clip(blk * NA_QROWS - NA_ROWS // 2, 0, n_grid_rows - NA_KROWS)
    lane = lax.broadcasted_iota(I32, (GRID_W, 2 * GRID_W), 1)
    n_rel = tz_ref.shape[1]
    heads = range(q_ref.shape[1] // hd)
    sls = [slice(hh * hd, (hh + 1) * hd) for hh in heads]
    rel, pen = [], []
    for rq_l in range(NA_QROWS):
        rq = blk * NA_QROWS + rq_l
        r_start = jnp.clip(rq - NA_ROWS // 2, 0, n_grid_rows - NA_ROWS)
        rel.append([])
        pen.append([])
        for m in range(NA_KROWS // 2):
            rk = kstart + 2 * m
            rel[-1].append(jnp.clip(rk - rq + (NA_ROWS - 1), 0, n_rel - 1))
            ok0 = (rk >= r_start) & (rk < r_start + NA_ROWS)
            ok1 = (rk + 1 >= r_start) & (rk + 1 < r_start + NA_ROWS)
            pen[-1].append(jnp.where(lane < GRID_W, jnp.where(ok0, 0.0, NEG_BIG), jnp.where(ok1, 0.0, NEG_BIG)))
    bias = [jnp.concatenate([jnp.concatenate([tz_ref[hh, pl.ds(rel[r][m], 1)][0] + pen[r][m]
                                              for m in range(NA_KROWS // 2)], axis=-1)
                             for r in range(NA_QROWS)], axis=0) for hh in heads]
    q = [q_ref[:, sl] * scale for sl in sls]
    k_loc = [jnp.concatenate([k0_ref[:, sl], k1_ref[:, sl], k2_ref[:, sl]], axis=0) for sl in sls]
    v_loc = [jnp.concatenate([v0_ref[:, sl], v1_ref[:, sl], v2_ref[:, sl]], axis=0).astype(BF16) for sl in sls]
    v_ctx = [vc_ref[0, :, sl].astype(BF16) for sl in sls]
    s_loc = [_bdot_nt(q[h], k_loc[h]) + bias[h] for h in heads]
    s_ctx = [_bdot_nt(q[h], kc_ref[0, :, sls[h]]) for h in heads]
    m = [jnp.maximum(s_loc[h].max(axis=-1, keepdims=True), s_ctx[h].max(axis=-1, keepdims=True)) for h in heads]
    e_loc = [jnp.exp(s_loc[h] - m[h]).astype(BF16) for h in heads]
    e_ctx = [jnp.exp(s_ctx[h] - m[h]).astype(BF16) for h in heads]
    pv = [jnp.dot(e_loc[h], jnp.concatenate([v_loc[h], jnp.ones_like(v_loc[h])], axis=-1),
                  preferred_element_type=F32)
          + jnp.dot(e_ctx[h], jnp.concatenate([v_ctx[h], jnp.ones_like(v_ctx[h])], axis=-1),
                    preferred_element_type=F32) for h in heads]
    o_ref[...] = jnp.concatenate([pv[h][:, :hd] / pv[h][:, hd:] for h in heads], axis=-1).astype(BF16)


def _na_call(proj, row0, n_seq, t, cache_k, cache_v, tz):
    hd, cw = 64, 512
    d = proj.shape[1] // 3
    ncb = d // cw
    tq = NA_QROWS * GRID_W
    nq = t // tq
    n_grid_rows = t // GRID_W
    t_ctx = cache_k.shape[1]
    qblk0 = row0 // tq

    def kv_map(which, j):
        def index_map(b, i, c):
            ks = jnp.clip(i * NA_QROWS - NA_ROWS // 2, 0, n_grid_rows - NA_KROWS) // NA_QROWS
            return (qblk0 + b * nq + ks + j, which * ncb + c)
        return index_map

    body = functools.partial(_na_kernel, hd=hd, n_grid_rows=n_grid_rows)
    return pl.pallas_call(
        body,
        grid=(n_seq, nq, ncb),
        in_specs=[pl.BlockSpec((tq, cw), lambda b, i, c: (qblk0 + b * nq + i, c))]
                 + [pl.BlockSpec((tq, cw), kv_map(1, j)) for j in range(3)]
                 + [pl.BlockSpec((tq, cw), kv_map(2, j)) for j in range(3)]
                 + [pl.BlockSpec((1, t_ctx, cw), lambda b, i, c: (b, 0, c)),
                    pl.BlockSpec((1, t_ctx, cw), lambda b, i, c: (b, 0, c)),
                    pl.BlockSpec((cw // hd,) + tz.shape[1:], lambda b, i, c: (c, 0, 0, 0))],
        out_specs=pl.BlockSpec((tq, cw), lambda b, i, c: (b * nq + i, c)),
        out_shape=jax.ShapeDtypeStruct((n_seq * t, d), BF16),
        compiler_params=_params(("arbitrary", "arbitrary", "arbitrary")),
        name="nbr_attn",
    )(proj, proj, proj, proj, proj, proj, proj, cache_k, cache_v, tz)


def _hgrn_kernel(q_ref, v_ref, f_ref, bf_ref, lb_ref, s0_ref, o_ref, s_ref, st_scr,
                 *, reverse, n_heads, dk, has_s0, n_blk):
    c = pl.program_id(1)
    L = HGRN_CHUNK
    tb = q_ref.shape[0]

    @pl.when(c == 0)
    def _():
        for h in range(n_heads):
            if has_s0:
                st_scr[h] = s0_ref[0, 0, h].T
            else:
                st_scr[h] = jnp.zeros_like(st_scr[h])

    row = lax.broadcasted_iota(I32, (L, L), 0)
    col = lax.broadcasted_iota(I32, (L, L), 1)
    tri = jnp.where((col >= row) if reverse else (col <= row), 1.0, 0.0).astype(F32)
    eye = jnp.where(row == col, 1.0, 0.0).astype(F32)
    halves = [L >> (i + 1) for i in range(L.bit_length() - 1)]
    same_pair = [jnp.where((row // (2 * hf)) == (col // (2 * hf)), 1.0, 0.0).astype(F32) for hf in halves]
    rio = lax.broadcasted_iota(I32, (L, q_ref.shape[1]), 0)
    is_query = [jnp.where(((rio & hf) == 0) if reverse else ((rio & hf) != 0), 1.0, 0.0).astype(F32)
                for hf in halves]
    sign = [2.0 * isq - 1.0 for isq in is_query]
    r8 = lax.broadcasted_iota(I32, (ROW_SUBLANES, 1), 0)
    lb = lb_ref[0]
    bf = bf_ref[0]
    n_chunks = tb // L

    def boundary_rows(cum, hf):
        blk = 2 * hf
        off = hf if reverse else hf - 1
        width = cum.shape[1]
        if blk >= ROW_SUBLANES:
            return jnp.concatenate([jnp.broadcast_to(cum[a + off:a + off + 1, :], (blk, width))
                                    for a in range(0, L, blk)], axis=0)
        groups = []
        for g in range(0, L, ROW_SUBLANES):
            ref = jnp.broadcast_to(cum[g + off:g + off + 1, :], (ROW_SUBLANES, width))
            for a in range(blk, ROW_SUBLANES, blk):
                ref = jnp.where(r8 >= a, jnp.broadcast_to(cum[g + a + off:g + a + off + 1, :],
                                                          (ROW_SUBLANES, width)), ref)
            groups.append(ref)
        return jnp.concatenate(groups, axis=0)

    def chunk_step(jj, carry):
        jc = (n_chunks - 1 - jj) if reverse else jj
        rs = pl.ds(pl.multiple_of(jc * L, L), L)
        q = _silu(q_ref[rs, :]) * (dk ** -0.5)
        v = v_ref[rs, :]
        f = lb + (1.0 - lb) * _sigmoid(f_ref[rs, :] + bf)
        logf = jnp.log(f)
        kk = 1.0 - f
        cum = _fdot(tri, logf)
        end = cum[0:1, :] if reverse else cum[L - 1:L, :]
        qd = q * jnp.exp(cum)
        kd = kk * jnp.exp(end - cum)
        e_end = jnp.exp(end)
        q_lv, k_lv = [], []
        for lv, hf in enumerate(halves):
            isq = is_query[lv]
            decay = jnp.exp(sign[lv] * (cum - boundary_rows(cum, hf)))
            dq = decay * isq
            q_lv.append((q * dq).astype(BF16))
            k_lv.append((kk * (decay - dq)).astype(BF16))
        qk_diag = q * kk
        hsl = [slice(h * dk, (h + 1) * dk) for h in range(n_heads)]
        inter = [_bdot_nt(qd[:, hs], st_scr[h]) for h, hs in enumerate(hsl)]
        attn = [eye * jnp.sum(qk_diag[:, hs], axis=-1, keepdims=True) for hs in hsl]
        for lv in range(len(halves)):
            prods = [_bdot_nt(q_lv[lv][:, hs], k_lv[lv][:, hs]) for hs in hsl]
            attn = [attn[h] + same_pair[lv] * prods[h] for h in range(n_heads)]
        outs = [inter[h] + _bdot(attn[h], v[:, hs]) for h, hs in enumerate(hsl)]
        o_ref[rs, :] = jnp.concatenate(outs, axis=-1)
        for h in range(n_heads):
            hs = slice(h * dk, (h + 1) * dk)
            st_scr[h] = st_scr[h] * e_end[:, hs] + _bdot_tn(v[:, hs], kd[:, hs])
        return carry

    lax.fori_loop(0, n_chunks, chunk_step, 0)

    @pl.when(c == n_blk - 1)
    def _():
        for h in range(n_heads):
            s_ref[0, 0, h] = st_scr[h].T


def _hgrn_call(proj, row0, n_seq, t, b_f, lb, s0, reverse):
    n_heads, dk = 8, 128
    d = n_heads * dk
    tb = 256
    n_blk = t // tb
    blk0 = row0 // tb
    di = 1 if reverse else 0
    has_s0 = s0 is not None
    if not has_s0:
        s0 = jnp.zeros((1, 2, n_heads, dk, dk), F32)

    def tok(b, c):
        return blk0 + b * n_blk + ((n_blk - 1 - c) if reverse else c)

    body = functools.partial(_hgrn_kernel, reverse=reverse, n_heads=n_heads, dk=dk,
                             has_s0=has_s0, n_blk=n_blk)
    return pl.pallas_call(
        body,
        grid=(n_seq, n_blk),
        in_specs=[pl.BlockSpec((tb, d), lambda b, c: (tok(b, c), 0)),
                  pl.BlockSpec((tb, d), lambda b, c: (tok(b, c), 1)),
                  pl.BlockSpec((tb, d), lambda b, c: (tok(b, c), 2 + di)),
                  pl.BlockSpec((1, 1, d), lambda b, c: (di, 0, 0)),
                  pl.BlockSpec((1, 1, d), lambda b, c: (di, 0, 0)),
                  pl.BlockSpec((1, 1, n_heads, dk, dk),
                               (lambda b, c: (b, di, 0, 0, 0)) if has_s0 else (lambda b, c: (0, 0, 0, 0, 0)))],
        out_specs=[pl.BlockSpec((tb, d), lambda b, c: (tok(b, c) - blk0, 0)),
                   pl.BlockSpec((1, 1, n_heads, dk, dk), lambda b, c: (b, 0, 0, 0, 0))],
        out_shape=[jax.ShapeDtypeStruct((n_seq * t, d), F32),
                   jax.ShapeDtypeStruct((n_seq, 1, n_heads, dk, dk), F32)],
        scratch_shapes=[pltpu.VMEM((n_heads, dk, dk), F32)],
        compiler_params=_params(("arbitrary", "arbitrary")),
        name="hgrn2_bw" if reverse else "hgrn2_fw",
    )(proj, proj, proj, b_f.reshape(2, 1, d), lb.reshape(2, 1, d), s0)


def _log_sigmoid(x):
    return jnp.minimum(x, 0.0) - jnp.log(1.0 + jnp.exp(-jnp.abs(x)))


def _mlstm_kernel(qf_ref, kf_ref, vf_ref, gf_ref, qb_ref, kb_ref, vb_ref, gb_ref, bg_ref, c0_ref, n0_ref, m0_ref,
                  of_ref, ob_ref, c_out, n_out, m_out, c_scr, n_scr, m_scr,
                  *, n_heads, dqk, dv, has_state, n_blk):
    c = pl.program_id(1)
    L = qf_ref.shape[0]

    @pl.when(c == 0)
    def _():
        if has_state:
            c_scr[...] = c0_ref[0]
            n_scr[...] = n0_ref[0]
            m_scr[...] = m0_ref[0]
        else:
            c_scr[...] = jnp.zeros_like(c_scr)
            n_scr[...] = jnp.zeros_like(n_scr)
            m_scr[...] = jnp.zeros_like(m_scr)

    row = lax.broadcasted_iota(I32, (L, L), 0)
    col = lax.broadcasted_iota(I32, (L, L), 1)
    lane = lax.broadcasted_iota(I32, (1, LANES), 1)
    heads = [(s, h) for s in range(2) for h in range(n_heads)]
    refs = [(qf_ref, kf_ref, vf_ref, gf_ref), (qb_ref, kb_ref, vb_ref, gb_ref)]
    causal_s, cum_s, cumt_s, gates_s, gatest_s, m_all = [], [], [], [], [], []
    for s in range(2):
        gates = GATE_SOFTCAP * jnp.tanh((refs[s][3][...] + bg_ref[...]) / GATE_SOFTCAP)
        causal = (col >= row) if s == 1 else (col <= row)
        cum = _fdot(jnp.where(causal, 1.0, 0.0).astype(F32), _log_sigmoid(gates))
        causal_s.append(causal)
        cum_s.append(cum)
        cumt_s.append(cum.T)
        gates_s.append(gates)
        gatest_s.append(gates.T)
        m_all.append(m_scr[s])
    i_off = [0, 2 * n_heads]
    f_off = [n_heads, 3 * n_heads]
    e_row = [L - 1, 0]
    c_prev = {it: c_scr[it[0], it[1]] for it in heads}
    n_prev = {it: n_scr[it[0], it[1]] for it in heads}
    qs = {(s, h): refs[s][0][:, h * dqk:(h + 1) * dqk] * (dqk ** -0.5) for s, h in heads}
    ks_ = {(s, h): refs[s][1][:, h * dqk:(h + 1) * dqk] for s, h in heads}
    vs = {(s, h): refs[s][2][:, h * dv:(h + 1) * dv] for s, h in heads}
    cum_c = {(s, h): cum_s[s][:, f_off[s] + h:f_off[s] + h + 1] for s, h in heads}
    cum_r = {(s, h): cumt_s[s][f_off[s] + h:f_off[s] + h + 1, :] for s, h in heads}
    i_c = {(s, h): gates_s[s][:, i_off[s] + h:i_off[s] + h + 1] for s, h in heads}
    i_r = {(s, h): gatest_s[s][i_off[s] + h:i_off[s] + h + 1, :] for s, h in heads}
    m_prev = {(s, h): m_all[s][0:1, h:h + 1] for s, h in heads}
    d = {h: jnp.where(causal_s[h[0]], cum_c[h] - cum_r[h] + i_r[h], -jnp.inf) for h in heads}
    m_inter = {h: cum_c[h] + m_prev[h] for h in heads}
    m_t = {h: jnp.maximum(m_inter[h], d[h].max(axis=-1, keepdims=True)) for h in heads}
    scores = {h: _bdot_nt(qs[h], ks_[h]) for h in heads}
    inter = {h: _bdot(qs[h], c_prev[h]) for h in heads}
    qn = {h: jnp.sum(qs[h] * n_prev[h], axis=-1, keepdims=True) for h in heads}
    w_inter = {h: jnp.exp(m_inter[h] - m_t[h]) for h in heads}
    qk = {h: scores[h] * jnp.exp(d[h] - m_t[h]) for h in heads}
    num = {h: w_inter[h] * inter[h] + _bdot(qk[h], vs[h]) for h in heads}
    den = {h: w_inter[h] * qn[h] + qk[h].sum(axis=-1, keepdims=True) for h in heads}
    outs = {h: num[h] / jnp.maximum(jnp.abs(den[h]), jnp.exp(-m_t[h])) for h in heads}
    end = {h: cum_c[h][e_row[h[0]]:e_row[h[0]] + 1, :] for h in heads}
    g_end_r = {h: end[h] - cum_r[h] + i_r[h] for h in heads}
    g_end_c = {h: end[h] - cum_c[h] + i_c[h] for h in heads}
    m_new = {h: jnp.maximum(end[h] + m_prev[h], g_end_r[h].max(axis=-1, keepdims=True)) for h in heads}
    w_old = {h: jnp.exp(end[h] + m_prev[h] - m_new[h]) for h in heads}
    kd = {h: ks_[h] * jnp.exp(g_end_c[h] - m_new[h]) for h in heads}
    c_new = {h: w_old[h] * c_prev[h] + _bdot_tn(kd[h], vs[h]) for h in heads}
    n_new = {h: w_old[h] * n_prev[h] + kd[h].sum(axis=0, keepdims=True) for h in heads}
    of_ref[...] = jnp.concatenate([outs[(0, h)] for h in range(n_heads)], axis=-1)
    ob_ref[...] = jnp.concatenate([outs[(1, h)] for h in range(n_heads)], axis=-1)
    for s in range(2):
        m_next = m_all[s]
        for h in range(n_heads):
            c_scr[s, h] = c_new[(s, h)]
            n_scr[s, h] = n_new[(s, h)]
            m_next = jnp.where(lane == h, m_new[(s, h)], m_next)
        m_scr[s] = m_next

    @pl.when(c == n_blk - 1)
    def _():
        c_out[0] = c_scr[...]
        n_out[0] = n_scr[...]
        m_out[0] = m_scr[...]


def _mlstm_call(proj, row0, n_seq, t, b_gates_pad, state):
    n_heads, dqk, dv = 8, 64, 128
    wq, wv = n_heads * dqk, n_heads * dv
    L = 256
    n_blk = t // L
    blk0 = row0 // L
    has_state = state is not None
    if has_state:
        c0, n0, m0 = state
        smap = lambda b, c: (b, 0, 0, 0, 0)
        mmap = lambda b, c: (b, 0, 0, 0)
    else:
        c0 = jnp.zeros((1, 2, n_heads, dqk, dv), F32)
        n0 = jnp.zeros((1, 2, n_heads, 1, dqk), F32)
        m0 = jnp.zeros((1, 2, 1, 128), F32)
        smap = lambda b, c: (0, 0, 0, 0, 0)
        mmap = lambda b, c: (0, 0, 0, 0)

    def tok(b, c, reverse):
        return blk0 + b * n_blk + ((n_blk - 1 - c) if reverse else c)

    body = functools.partial(_mlstm_kernel, n_heads=n_heads, dqk=dqk, dv=dv, has_state=has_state, n_blk=n_blk)
    gate_cb = (2 * wq + 2 * wv) // 128

    def token_specs(reverse):
        return [pl.BlockSpec((L, wq), lambda b, c: (tok(b, c, reverse), 0)),
                pl.BlockSpec((L, wq), lambda b, c: (tok(b, c, reverse), 1)),
                pl.BlockSpec((L, wv), lambda b, c: (tok(b, c, reverse), (2 * wq) // wv)),
                pl.BlockSpec((L, 128), lambda b, c: (tok(b, c, reverse), gate_cb))]

    return pl.pallas_call(
        body,
        grid=(n_seq, n_blk),
        in_specs=token_specs(False) + token_specs(True)
                 + [pl.BlockSpec((1, 128), lambda b, c: (0, 0)),
                    pl.BlockSpec((1, 2, n_heads, dqk, dv), smap),
                    pl.BlockSpec((1, 2, n_heads, 1, dqk), smap),
                    pl.BlockSpec((1, 2, 1, 128), mmap)],
        out_specs=[pl.BlockSpec((L, wv), lambda b, c: (tok(b, c, False) - blk0, 0)),
                   pl.BlockSpec((L, wv), lambda b, c: (tok(b, c, True) - blk0, 0)),
                   pl.BlockSpec((1, 2, n_heads, dqk, dv), lambda b, c: (b, 0, 0, 0, 0)),
                   pl.BlockSpec((1, 2, n_heads, 1, dqk), lambda b, c: (b, 0, 0, 0, 0)),
                   pl.BlockSpec((1, 2, 1, 128), lambda b, c: (b, 0, 0, 0))],
        out_shape=[jax.ShapeDtypeStruct((n_seq * t, wv), F32),
                   jax.ShapeDtypeStruct((n_seq * t, wv), F32),
                   jax.ShapeDtypeStruct((n_seq, 2, n_heads, dqk, dv), F32),
                   jax.ShapeDtypeStruct((n_seq, 2, n_heads, 1, dqk), F32),
                   jax.ShapeDtypeStruct((n_seq, 2, 1, 128), F32)],
        scratch_shapes=[pltpu.VMEM((2, n_heads, dqk, dv), F32),
                        pltpu.VMEM((2, n_heads, 1, dqk), F32),
                        pltpu.VMEM((2, 1, 128), F32)],
        compiler_params=_params(("arbitrary", "arbitrary")),
        name="mlstm",
    )(proj, proj, proj, proj, proj, proj, proj, proj, b_gates_pad, c0, n0, m0)


def _moe_input(x_ref, m_ref, g_ref):
    return _rms(x_ref[...], g_ref[...]) * (1.0 + m_ref[0, 4:5, :]) + m_ref[0, 3:4, :]


def _route_kernel(x_ref, m_ref, g_ref, wr_ref, br_ref, idx_ref, gate_ref, rank_ref, cnt_ref, carry_scr,
                  *, n_experts):
    i = pl.program_id(0)
    tm = x_ref.shape[0]

    @pl.when(i == 0)
    def _():
        carry_scr[...] = jnp.zeros_like(carry_scr)

    h = _moe_input(x_ref, m_ref, g_ref)
    w = wr_ref[...]
    h_hi, w_hi = h.astype(BF16), w.astype(BF16)
    h_lo = (h - h_hi.astype(F32)).astype(BF16)
    w_lo = (w - w_hi.astype(F32)).astype(BF16)
    logits = _bdot_nt(w_hi, h_hi) + _bdot_nt(w_lo, h_hi) + _bdot_nt(w_hi, h_lo) + br_ref[...]
    e_io = lax.broadcasted_iota(I32, (n_experts, tm), 0).astype(F32)
    work = logits
    vals, idxs = [], []
    chosen = jnp.zeros((n_experts, tm), F32)
    for _ in range(TOP_K):
        mx = work.max(axis=0, keepdims=True)
        ix = jnp.min(jnp.where(work == mx, e_io, float(n_experts)), axis=0, keepdims=True)
        hit = e_io == ix
        vals.append(mx)
        idxs.append(ix)
        chosen = jnp.where(hit, 1.0, chosen)
        work = jnp.where(hit, -jnp.inf, work)
    es = [jnp.exp(v - vals[0]) for v in vals]
    den = es[0] + es[1] + es[2] + es[3]
    srow = lax.broadcasted_iota(I32, (tm, tm), 0)
    scol = lax.broadcasted_iota(I32, (tm, tm), 1)
    before = jnp.where(srow < scol, 1.0, 0.0).astype(BF16)
    pos = jnp.dot(chosen.astype(BF16), before, preferred_element_type=F32) + carry_scr[...]
    ranks = [jnp.sum(jnp.where(e_io == ix, pos, 0.0), axis=0, keepdims=True) for ix in idxs]
    carry_scr[...] = carry_scr[...] + chosen.sum(axis=1, keepdims=True)
    idx_ref[...] = jnp.concatenate(idxs, axis=0).astype(I32)
    gate_ref[...] = jnp.concatenate([e / den for e in es], axis=0)
    rank_ref[...] = jnp.concatenate(ranks, axis=0).astype(I32)
    cnt_ref[...] = jnp.broadcast_to(carry_scr[...], cnt_ref.shape).astype(I32)


def _route_call(x, modsel, g, w_router, b_router):
    n, d = x.shape
    n_experts = w_router.shape[1]
    tm = TOKEN_TILE
    body = functools.partial(_route_kernel, n_experts=n_experts)
    return pl.pallas_call(
        body,
        grid=(n // tm,),
        in_specs=[pl.BlockSpec((tm, d), lambda i: (i, 0)),
                  pl.BlockSpec((1, 6, d), lambda i: (i, 0, 0)),
                  pl.BlockSpec((1, d), lambda i: (0, 0)),
                  pl.BlockSpec((n_experts, d), lambda i: (0, 0)),
                  pl.BlockSpec((n_experts, 1), lambda i: (0, 0))],
        out_specs=[pl.BlockSpec((TOP_K, tm), lambda i: (0, i)),
                   pl.BlockSpec((TOP_K, tm), lambda i: (0, i)),
                   pl.BlockSpec((TOP_K, tm), lambda i: (0, i)),
                   pl.BlockSpec((n_experts, 128), lambda i: (0, 0))],
        out_shape=[jax.ShapeDtypeStruct((TOP_K, n), I32),
                   jax.ShapeDtypeStruct((TOP_K, n), F32),
                   jax.ShapeDtypeStruct((TOP_K, n), I32),
                   jax.ShapeDtypeStruct((n_experts, 128), I32)],
        scratch_shapes=[pltpu.VMEM((n_experts, 1), F32)],
        compiler_params=_params(("arbitrary",)),
        name="moe_route",
    )(x, modsel, g.reshape(1, d), w_router.T, b_router.reshape(n_experts, 1))


def _slot_kernel(cnt_ref, idx_ref, rank_ref, dest_ref, binfo_ref, pad_ref, nused_ref, *, n_experts, n_blocks):
    cnt = cnt_ref[:, 0:1].astype(F32)
    padded = jnp.ceil(cnt * (1.0 / MOE_BLOCK)) * MOE_BLOCK
    er = lax.broadcasted_iota(I32, (n_experts, n_experts), 0)
    ec = lax.broadcasted_iota(I32, (n_experts, n_experts), 1)
    start_row = jnp.sum(jnp.where(er < ec, padded, 0.0), axis=0, keepdims=True)
    start_col = jnp.sum(jnp.where(er == ec, start_row, 0.0), axis=1, keepdims=True)
    end_col = start_col + padded
    idx = idx_ref[...]
    e_io = lax.broadcasted_iota(I32, (n_experts,) + idx.shape[1:], 0)
    rows = []
    for k in range(TOP_K):
        hit = e_io == idx[k:k + 1, :]
        rows.append(jnp.sum(jnp.where(hit, start_col, 0.0), axis=0, keepdims=True))
    dest_ref[...] = jnp.concatenate(rows, axis=0).astype(I32) + rank_ref[...]
    blk_start = (lax.broadcasted_iota(I32, (n_experts, n_blocks), 1) * MOE_BLOCK).astype(F32)
    n_done = jnp.sum(jnp.where(end_col <= blk_start, 1.0, 0.0), axis=0, keepdims=True)
    bexp = jnp.minimum(n_done, n_experts - 1.0)
    used_row = jnp.sum(jnp.where(er == ec, jnp.where(cnt > 0.0, 1.0, 0.0), 0.0), axis=0, keepdims=True)
    ecf = ec.astype(F32)
    next_col = jnp.min(jnp.where((ec > er) & (used_row > 0.0), ecf, float(n_experts)), axis=1, keepdims=True)
    ord_col = jnp.sum(jnp.where(ec < er, used_row, 0.0), axis=1, keepdims=True)
    par_col = ord_col - 2.0 * jnp.floor(ord_col * 0.5)
    mine = lax.broadcasted_iota(I32, (n_experts, n_blocks), 0).astype(F32) == bexp
    bnext = jnp.sum(jnp.where(mine, next_col, 0.0), axis=0, keepdims=True)
    bslot = jnp.sum(jnp.where(mine, par_col, 0.0), axis=0, keepdims=True)
    binfo_ref[...] = jnp.concatenate([bexp, bnext, bslot], axis=0).astype(I32)
    n_used = jnp.sum(padded, axis=0, keepdims=True) * (1.0 / MOE_BLOCK)
    nused_ref[...] = n_used.astype(I32)
    cnt_row = jnp.sum(jnp.where(er == ec, cnt, 0.0), axis=0, keepdims=True)
    padded_row = jnp.sum(jnp.where(er == ec, padded, 0.0), axis=0, keepdims=True)
    pad_ref[...] = jnp.concatenate([start_row + cnt_row, padded_row - cnt_row,
                                    jnp.broadcast_to(n_used, cnt_row.shape)], axis=0).astype(I32)


def _slot_call(counts, idx_t, rank_t, n_blocks):
    n_experts = counts.shape[0]
    n = idx_t.shape[1]
    tn = min(2048, n)
    body = functools.partial(_slot_kernel, n_experts=n_experts, n_blocks=n_blocks)
    return pl.pallas_call(
        body,
        grid=(n // tn,),
        in_specs=[pl.BlockSpec((n_experts, 128), lambda i: (0, 0)),
                  pl.BlockSpec((TOP_K, tn), lambda i: (0, i)),
                  pl.BlockSpec((TOP_K, tn), lambda i: (0, i))],
        out_specs=[pl.BlockSpec((TOP_K, tn), lambda i: (0, i)),
                   pl.BlockSpec((3, n_blocks), lambda i: (0, 0)),
                   pl.BlockSpec((3, n_experts), lambda i: (0, 0)),
                   pl.BlockSpec((1, 1), lambda i: (0, 0))],
        out_shape=[jax.ShapeDtypeStruct((TOP_K, n), I32),
                   jax.ShapeDtypeStruct((3, n_blocks), I32),
                   jax.ShapeDtypeStruct((3, n_experts), I32),
                   jax.ShapeDtypeStruct((1, 1), I32)],
        compiler_params=_params(("arbitrary",)),
        name="moe_slots",
    )(counts, idx_t, rank_t)


DMA_ISSUE_UNROLL = 8


def _to_row_tiles(ref, base, x):
    rows = x.shape[0]
    for c in range(ROW_SUBLANES):
        ref[pl.ds(base * ROW_SUBLANES + c, rows, stride=ROW_SUBLANES), :] = x[:, c * LANES:(c + 1) * LANES]


def _from_row_tiles(ref, base, rows, c):
    return ref[pl.ds(base * ROW_SUBLANES + c, rows, stride=ROW_SUBLANES), :]


def _row_tile(ref, r):
    return ref.at[pl.ds(pl.multiple_of(r * ROW_SUBLANES, ROW_SUBLANES), ROW_SUBLANES)]


def _zero_fill_padding(pad_ref, xs_ref, z_scr, sem):
    z_scr[...] = jnp.zeros_like(z_scr)
    n_experts = pad_ref.shape[1]
    bits = range(MOE_BLOCK.bit_length() - 2, -1, -1)

    def pieces(e):
        off, length = pad_ref[0, e], pad_ref[1, e]
        for bit in bits:
            size = 1 << bit
            done = (length >> (bit + 1)) << (bit + 1)
            copy = pltpu.make_async_copy(z_scr.at[pl.ds(0, size * ROW_SUBLANES)],
                                         xs_ref.at[pl.ds(pl.multiple_of((off + done) * ROW_SUBLANES, ROW_SUBLANES),
                                                         size * ROW_SUBLANES)], sem)
            yield (length & size) != 0, copy

    def tail_blocks():
        n_blocks = xs_ref.shape[0] // (MOE_BLOCK * ROW_SUBLANES)
        for b in range(n_blocks - n_experts, n_blocks):
            copy = pltpu.make_async_copy(z_scr, xs_ref.at[pl.ds(b * MOE_BLOCK * ROW_SUBLANES,
                                                                MOE_BLOCK * ROW_SUBLANES)], sem)
            yield b >= pad_ref[2, 0], copy

    def all_copies():
        for e in range(n_experts):
            yield from pieces(e)
        yield from tail_blocks()

    for needed, copy in all_copies():
        pl.when(needed)(copy.start)
    for needed, copy in all_copies():
        pl.when(needed)(copy.wait)


def _dispatch_kernel(pad_ref, dest_ref, x_ref, m_ref, g_ref, xs_ref, h_scr, z_scr, sem, *, n_tiles):
    i = pl.program_id(0)
    tm = x_ref.shape[0]

    @pl.when(i == 0)
    def _():
        _zero_fill_padding(pad_ref, xs_ref, z_scr, sem.at[2])

    slot = lax.rem(i, 2)
    src = h_scr.at[slot]
    _to_row_tiles(src, 0, _moe_input(x_ref, m_ref, g_ref))

    def start_row(r, carry):
        for k in range(TOP_K):
            pltpu.make_async_copy(_row_tile(src, r), _row_tile(xs_ref, dest_ref[0, 0, k * tm + r]),
                                  sem.at[slot]).start(priority=k % 2)
        return carry

    lax.fori_loop(0, tm, start_row, 0, unroll=DMA_ISSUE_UNROLL // TOP_K)

    def wait_tile(s):
        for _ in range(TOP_K):
            pltpu.make_async_copy(h_scr.at[s], xs_ref.at[pl.ds(0, tm * ROW_SUBLANES)], sem.at[s]).wait()

    @pl.when(i >= 1)
    def _():
        wait_tile(1 - slot)

    @pl.when(i == n_tiles - 1)
    def _():
        wait_tile(slot)


def _dispatch_call(pad_info, dest_tiles, x, modsel, g, n_slots):
    n, d = x.shape
    tm = TOKEN_TILE
    assert d == ROW_SUBLANES * LANES
    return pl.pallas_call(
        functools.partial(_dispatch_kernel, n_tiles=n // tm),
        grid=(n // tm,),
        in_specs=[pl.BlockSpec(memory_space=pltpu.SMEM),
                  pl.BlockSpec((1, 1, TOP_K * tm), lambda i: (i, 0, 0), memory_space=pltpu.SMEM),
                  pl.BlockSpec((tm, d), lambda i: (i, 0)),
                  pl.BlockSpec((1, 6, d), lambda i: (i, 0, 0)),
                  pl.BlockSpec((1, d), lambda i: (0, 0))],
        out_specs=pl.BlockSpec(memory_space=pl.ANY),
        out_shape=jax.ShapeDtypeStruct((n_slots * ROW_SUBLANES, LANES), F32),
        scratch_shapes=[pltpu.VMEM((2, tm * ROW_SUBLANES, LANES), F32),
                        pltpu.VMEM((MOE_BLOCK * ROW_SUBLANES, LANES), F32),
                        pltpu.SemaphoreType.DMA((3,))],
        compiler_params=_params(("arbitrary",)),
        name="moe_dispatch",
    )(pad_info, dest_tiles, x, modsel, g.reshape(1, d))


FFN_BLOCKS_PER_STEP = 4


def _ffn_kernel(binfo_ref, nused_ref, xs_ref, wgu_hbm, wdn_hbm, *rest, layer, n_experts, n_blocks):
    bias_refs = rest[:2 * FFN_BLOCKS_PER_STEP]
    ys_ref, wgu_f32, wdn_f32, wgu_scr, wdn_scr, sem = rest[2 * FFN_BLOCKS_PER_STEP:]
    d_ff = wdn_scr.shape[0]
    rows = MOE_BLOCK

    def weight_copies(e, slot):
        return (pltpu.make_async_copy(wgu_hbm.at[layer, e], wgu_f32.at[slot], sem.at[slot]),
                pltpu.make_async_copy(wdn_hbm.at[layer, e], wdn_f32.at[slot], sem.at[slot]))

    def one_block(u):
        b = pl.program_id(0) * FFN_BLOCKS_PER_STEP + u
        base = u * MOE_BLOCK
        bgu_ref, bdn_ref = bias_refs[2 * u], bias_refs[2 * u + 1]

        @pl.when(b < nused_ref[0])
        def _():
            e = binfo_ref[b]
            prev = binfo_ref[jnp.maximum(b - 1, 0)]
            slot = binfo_ref[2 * n_blocks + b]

            @pl.when((b == 0) | (e != prev))
            def _():
                @pl.when(b == 0)
                def _():
                    for cp in weight_copies(e, slot):
                        cp.start()

                nxt = binfo_ref[n_blocks + b]

                @pl.when(nxt < n_experts)
                def _():
                    for cp in weight_copies(nxt, 1 - slot):
                        cp.start()

                for cp in weight_copies(e, slot):
                    cp.wait()
                wgu_scr[...] = wgu_f32[slot].astype(BF16)
                wdn_scr[...] = wdn_f32[slot].astype(BF16)

            x = jnp.concatenate([_from_row_tiles(xs_ref, base, rows, c).astype(BF16)
                                 for c in range(ROW_SUBLANES)], axis=-1)
            gu = jnp.dot(x, wgu_scr[...], preferred_element_type=F32) + bgu_ref[0, 0]
            x_glu = jnp.minimum(gu[:, :d_ff], SWIGLU_LIMIT)
            x_lin = jnp.clip(gu[:, d_ff:], -SWIGLU_LIMIT, SWIGLU_LIMIT)
            hid = x_glu * _sigmoid(SWIGLU_ALPHA * x_glu) * (x_lin + 1.0)
            _to_row_tiles(ys_ref, base, jnp.dot(hid.astype(BF16), wdn_scr[...], preferred_element_type=F32)
                          + bdn_ref[0, 0])

        @pl.when(b >= nused_ref[0])
        def _():
            ys_ref[pl.ds(base * ROW_SUBLANES, rows * ROW_SUBLANES), :] = jnp.zeros(
                (rows * ROW_SUBLANES, LANES), F32)

    for u in range(FFN_BLOCKS_PER_STEP):
        one_block(u)


def _ffn_call(layer, block_info, n_used, xs, w_gu, b_gu, w_dn, b_dn):
    depth, n_experts, d, d_ff2 = w_gu.shape
    d_ff = d_ff2 // 2
    g = FFN_BLOCKS_PER_STEP
    step_rows = g * MOE_BLOCK * ROW_SUBLANES
    n_blocks = xs.shape[0] // (MOE_BLOCK * ROW_SUBLANES)
    assert n_blocks % g == 0

    def blk(b, nu):
        return jnp.maximum(jnp.minimum(b, nu[0] - 1), 0)

    def bias_specs(u):
        return [pl.BlockSpec((1, 1, 1, d_ff2), lambda s, bi, nu: (layer, bi[blk(g * s + u, nu)], 0, 0)),
                pl.BlockSpec((1, 1, 1, d), lambda s, bi, nu: (layer, bi[blk(g * s + u, nu)], 0, 0))]

    bias_args = [b_gu.reshape(depth, n_experts, 1, d_ff2), b_dn.reshape(depth, n_experts, 1, d)] * g
    grid_spec = pltpu.PrefetchScalarGridSpec(
        num_scalar_prefetch=2,
        grid=(n_blocks // g,),
        in_specs=[pl.BlockSpec((step_rows, LANES), lambda s, bi, nu: (blk(g * s, nu) // g, 0)),
                  pl.BlockSpec(memory_space=pl.ANY),
                  pl.BlockSpec(memory_space=pl.ANY)]
                 + [spec for u in range(g) for spec in bias_specs(u)],
        out_specs=pl.BlockSpec((step_rows, LANES), lambda s, bi, nu: (s, 0)),
        scratch_shapes=[pltpu.VMEM((2, d, d_ff2), F32), pltpu.VMEM((2, d_ff, d), F32),
                        pltpu.VMEM((d, d_ff2), BF16), pltpu.VMEM((d_ff, d), BF16),
                        pltpu.SemaphoreType.DMA((2,))],
    )
    body = functools.partial(_ffn_kernel, layer=layer, n_experts=n_experts, n_blocks=n_blocks)
    return pl.pallas_call(
        body,
        grid_spec=grid_spec,
        out_shape=jax.ShapeDtypeStruct(xs.shape, F32),
        compiler_params=_params(("arbitrary",)),
        name="moe_ffn",
    )(block_info, n_used, xs, w_gu, w_dn, *bias_args)


def _gather_expert_rows(dest_ref, dest_next_ref, ys_ref, gate_ref, buf, sem, n_tiles):
    i = pl.program_id(0)
    tm = gate_ref.shape[0]
    rows = TOP_K * tm

    def start_gathers(d_ref, slot):
        def start_pair(p, carry):
            for u in range(2):
                j = 2 * p + u
                pltpu.make_async_copy(_row_tile(ys_ref, d_ref[0, 0, j]), _row_tile(buf.at[slot], j),
                                      sem.at[slot]).start(priority=u)
            return carry

        lax.fori_loop(0, rows // 2, start_pair, 0, unroll=DMA_ISSUE_UNROLL // 2)

    slot = lax.rem(i, 2)

    @pl.when(i == 0)
    def _():
        start_gathers(dest_ref, 0)

    def wait_tile(s):
        pltpu.make_async_copy(ys_ref.at[pl.ds(0, rows * ROW_SUBLANES)], buf.at[s], sem.at[s]).wait()

    @pl.when(i + 1 < n_tiles)
    def _():
        start_gathers(dest_next_ref, 1 - slot)

    wait_tile(slot)
    cur = buf.at[slot]
    chunks = []
    for c in range(ROW_SUBLANES):
        y = gate_ref[:, 0:1] * _from_row_tiles(cur, 0, tm, c)
        for k in range(1, TOP_K):
            y = y + gate_ref[:, k:k + 1] * _from_row_tiles(cur, k * tm, tm, c)
        chunks.append(y)
    return jnp.concatenate(chunks, axis=-1)


def _combine_kernel(dest_ref, dest_next_ref, ys_ref, gate_ref, x_ref, m_ref, fg_ref, y_ref, buf, sem,
                    *, n_tiles):
    y = _gather_expert_rows(dest_ref, dest_next_ref, ys_ref, gate_ref, buf, sem, n_tiles)
    y_ref[...] = _rms(x_ref[...] + m_ref[0, 5:6, :] * y, fg_ref[...])


def _combine_inproj_kernel(dest_ref, dest_next_ref, ys_ref, gate_ref, x_ref, mp_ref, mc_ref, g_ref, w_ref,
                           x_out_ref, p_ref, buf, sem, *, n_tiles):
    y = _gather_expert_rows(dest_ref, dest_next_ref, ys_ref, gate_ref, buf, sem, n_tiles)
    x = x_ref[...] + mp_ref[0, 5:6, :] * y
    x_out_ref[...] = x
    h = _rms(x, g_ref[...]) * (1.0 + mc_ref[0, 1:2, :]) + mc_ref[0, 0:1, :]
    p_ref[...] = _bdot(h, w_ref[...])


def _moe_gather_specs(tm, n_tiles, d):
    return [pl.BlockSpec((1, 1, TOP_K * tm), lambda i: (i, 0, 0), memory_space=pltpu.SMEM),
            pl.BlockSpec((1, 1, TOP_K * tm), lambda i: (jnp.minimum(i + 1, n_tiles - 1), 0, 0),
                         memory_space=pltpu.SMEM),
            pl.BlockSpec(memory_space=pl.ANY),
            pl.BlockSpec((tm, TOP_K), lambda i: (i, 0)),
            pl.BlockSpec((tm, d), lambda i: (i, 0))]


def _moe_gather_scratch(tm):
    return [pltpu.VMEM((2, TOP_K * tm * ROW_SUBLANES, LANES), F32), pltpu.SemaphoreType.DMA((2,))]


def _combine_inproj_call(pending, x, modsel_prev, modsel, g, w):
    dest_tiles, ys, gates_nk = pending
    n, d = x.shape
    wout = w.shape[1]
    tm = TOKEN_TILE
    n_tiles = n // tm
    body = functools.partial(_combine_inproj_kernel, n_tiles=n_tiles)
    return pl.pallas_call(
        body,
        grid=(n_tiles,),
        in_specs=_moe_gather_specs(tm, n_tiles, d)
                 + [pl.BlockSpec((1, 6, d), lambda i: (i, 0, 0)),
                    pl.BlockSpec((1, 6, d), lambda i: (i, 0, 0)),
                    pl.BlockSpec((1, d), lambda i: (0, 0)),
                    pl.BlockSpec((d, wout), lambda i: (0, 0))],
        out_specs=[pl.BlockSpec((tm, d), lambda i: (i, 0)),
                   pl.BlockSpec((tm, wout), lambda i: (i, 0))],
        out_shape=[jax.ShapeDtypeStruct((n, d), F32),
                   jax.ShapeDtypeStruct((n, wout), F32)],
        scratch_shapes=_moe_gather_scratch(tm),
        compiler_params=_params(("arbitrary",)),
        name="moe_combine_inproj",
    )(dest_tiles, dest_tiles, ys, gates_nk, x, modsel_prev, modsel, g.reshape(1, d), w)


def _combine_call(pending, x, modsel, final_g):
    dest_tiles, ys, gates_nk = pending
    n, d = x.shape
    tm = TOKEN_TILE
    n_tiles = n // tm
    body = functools.partial(_combine_kernel, n_tiles=n_tiles)
    return pl.pallas_call(
        body,
        grid=(n_tiles,),
        in_specs=_moe_gather_specs(tm, n_tiles, d)
                 + [pl.BlockSpec((1, 6, d), lambda i: (i, 0, 0)),
                    pl.BlockSpec((1, d), lambda i: (0, 0))],
        out_specs=pl.BlockSpec((tm, d), lambda i: (i, 0)),
        out_shape=jax.ShapeDtypeStruct((n, d), F32),
        scratch_shapes=_moe_gather_scratch(tm),
        compiler_params=_params(("arbitrary",)),
        name="moe_combine",
    )(dest_tiles, dest_tiles, ys, gates_nk, x, modsel, final_g.reshape(1, d))


def _moe_experts(layer, x, modsel, g2, w_router, b_router, w_gu, b_gu, w_dn, b_dn):
    n, d = x.shape
    n_experts = w_router.shape[1]
    tm = TOKEN_TILE
    n_blocks = (n * TOP_K) // MOE_BLOCK + n_experts
    idx_t, gate_t, rank_t, counts = _route_call(x, modsel, g2, w_router, b_router)
    dest_t, block_info, pad_info, n_used = _slot_call(counts, idx_t, rank_t, n_blocks)
    dest_tiles = dest_t.reshape(TOP_K, n // tm, tm).transpose(1, 0, 2).reshape(n // tm, 1, TOP_K * tm)
    xs = _dispatch_call(pad_info, dest_tiles, x, modsel, g2, n_blocks * MOE_BLOCK)
    ys = _ffn_call(layer, block_info.reshape(3 * n_blocks), n_used.reshape(1), xs, w_gu, b_gu, w_dn, b_dn)
    return dest_tiles, ys, gate_t.T


def _rope_tables(t, hd):
    pos = np.arange(t)
    n_freq = hd // 4
    inv_freq = ROPE_THETA ** (-np.arange(n_freq, dtype=np.float32) / n_freq)
    ang = np.concatenate([(pos // GRID_W).astype(np.float32)[:, None] * inv_freq,
                          (pos % GRID_W).astype(np.float32)[:, None] * inv_freq], axis=-1)
    ang = jnp.asarray(ang, F32)
    cos, sin = jnp.cos(ang), jnp.sin(ang)
    return jnp.concatenate([cos, cos], axis=-1), jnp.concatenate([-sin, sin], axis=-1)


def kernel(x_prompt, x_sample, cache_k_a, cache_v_a, state_b, state_c_C, state_c_n, state_c_m, cache_k_d, cache_v_d, c, c_ctx, norm1_g, norm2_g, w_mod, b_mod, w_in_a, qnorm_a, knorm_a, w_out_a, w_in_b, b_f_b, lower_bounds_b, onorm_b, w_out_b, w_in_c, b_gates_c, onorm_c, w_out_c, w_in_d, rpb_d, w_out_d, w_router, b_router, w_gu, b_gu, w_dn, b_dn, final_g):
    n_ctx_seq, t_ctx, d = x_prompt.shape
    n_lat_seq, t_lat, _ = x_sample.shape
    depth = w_mod.shape[0]
    n_ctx = n_ctx_seq * t_ctx
    n_lat = n_lat_seq * t_lat
    n = n_ctx + n_lat
    tm = TOKEN_TILE
    assert t_ctx % tm == 0 and t_lat % tm == 0 and n_lat_seq + 1 <= 8
    assert n_ctx % PROJ_TILE == 0 and t_lat % PROJ_TILE == 0

    lb_cum = jnp.cumsum(jax.nn.softmax(lower_bounds_b.astype(F32), axis=0), axis=0)
    lb_all = lb_cum - lb_cum[0]

    cond8 = jnp.zeros((8, d), F32).at[0].set(c_ctx).at[1:1 + n_lat_seq].set(c)
    mod = _mod_call(cond8, w_mod, b_mod)
    tile_row = np.concatenate([np.zeros(n_ctx // tm, np.int32),
                               1 + np.repeat(np.arange(n_lat_seq, dtype=np.int32), t_lat // tm)])

    x = jnp.concatenate([x_prompt.reshape(n_ctx, d), x_sample.reshape(n_lat, d)], axis=0)
    outs = {}
    pending = None
    modsel = None

    def inproj(x, modsel, g, w):
        if pending is None:
            return x, _inproj_call(x, modsel, g, w)
        return _combine_inproj_call(pending, x, modsel_prev, modsel, g, w)

    for i in range(depth):
        kind = i % 4
        j = i // 4
        modsel_prev = modsel
        modsel = mod[i].reshape(8, 6, d)[tile_row]
        if kind == 0:
            x, proj = inproj(x, modsel, norm1_g[i],w_in_a[j].astype(BF16))
            o_ctx, k_new = _gqa_ctx_call(proj, qnorm_a[j], knorm_a[j], n_ctx_seq, t_ctx)
            cosd, sind = _rope_tables(t_lat, 128)
            o_lat = _gqa_lat_call(proj, n_ctx, n_lat_seq, t_lat,
                                  cache_k_a[:, j].reshape(n_lat_seq, -1, 256),
                                  cache_v_a[:, j].reshape(n_lat_seq, -1, 256), cosd, sind,
                                  qnorm_a[j], knorm_a[j])
            outs["k_a"] = k_new.reshape(n_ctx_seq, 1, t_ctx, 2, 128)
            outs["v_a"] = proj[:n_ctx, 1280:1536].reshape(n_ctx_seq, 1, t_ctx, 2, 128)
            x = _outproj_call("plain", [(o_ctx, o_lat)], None, w_out_a[j].astype(BF16), x, modsel)
        elif kind == 1:
            x, proj = inproj(x, modsel, norm1_g[i],w_in_b[j].astype(BF16))
            o_dirs, s_dirs = [], []
            for reverse in (False, True):
                oc, sc = _hgrn_call(proj, 0, n_ctx_seq, t_ctx, b_f_b[j], lb_all[i], None, reverse)
                ol, _ = _hgrn_call(proj, n_ctx, n_lat_seq, t_lat, b_f_b[j], lb_all[i], state_b[:, j], reverse)
                o_dirs.append((oc, ol))
                s_dirs.append(sc)
            outs["s_b"] = jnp.concatenate(s_dirs, axis=1)[:, None]
            x = _outproj_call("hgrn", o_dirs, (proj, 4, onorm_b[j].reshape(1, 128)),
                              w_out_b[j].astype(BF16), x, modsel)
        elif kind == 2:
            w_c = jnp.pad(w_in_c[j], ((0, 0), (0, 128 - 32))).astype(BF16)
            bg = jnp.pad(b_gates_c[j].reshape(1, 32), ((0, 0), (0, 128 - 32)))
            x, proj = inproj(x, modsel, norm1_g[i],w_c)
            state = (state_c_C[:, j], state_c_n[:, j][:, :, :, None, :],
                     jnp.pad(state_c_m[:, j], ((0, 0), (0, 0), (0, 120)))[:, :, None, :])
            ocf, ocb, cc, nc, mc = _mlstm_call(proj, 0, n_ctx_seq, t_ctx, bg, None)
            olf, olb, _, _, _ = _mlstm_call(proj, n_ctx, n_lat_seq, t_lat, bg, state)
            o_dirs = [(ocf, olf), (ocb, olb)]
            outs["c_C"] = cc[:, None]
            outs["c_n"] = nc[:, None, :, :, 0, :]
            outs["c_m"] = mc[:, None, :, 0, :8]
            x = _outproj_call("mlstm", o_dirs, (proj, 2, onorm_c[j].reshape(1, 128)),
                              w_out_c[j].astype(BF16), x, modsel)
        else:
            x, proj = inproj(x, modsel, norm1_g[i],w_in_d[j].astype(BF16))
            o_ctx = _mha_ctx_call(proj, n_ctx_seq, t_ctx)
            tz = _na_bias_call(rpb_d[j])
            o_lat = _na_call(proj, n_ctx, n_lat_seq, t_lat,
                             cache_k_d[:, j].reshape(n_lat_seq, -1, d),
                             cache_v_d[:, j].reshape(n_lat_seq, -1, d), tz)
            outs["k_d"] = proj[:n_ctx, d:2 * d].reshape(n_ctx_seq, 1, t_ctx, 16, 64)
            outs["v_d"] = proj[:n_ctx, 2 * d:3 * d].reshape(n_ctx_seq, 1, t_ctx, 16, 64)
            x = _outproj_call("plain", [(o_ctx, o_lat)], None, w_out_d[j].astype(BF16), x, modsel)
        pending = _moe_experts(i, x, modsel, norm2_g[i], w_router[i], b_router[i], w_gu, b_gu, w_dn, b_dn)
    x = _combine_call(pending, x, modsel, final_g)

    y_prompt = x[:n_ctx].reshape(n_ctx_seq, t_ctx, d)
    y_sample = x[n_ctx:].reshape(n_lat_seq, t_lat, d)
    return (y_prompt, y_sample, outs["k_a"], outs["v_a"], outs["s_b"], outs["c_C"], outs["c_n"], outs["c_m"],
            outs["k_d"], outs["v_d"])
```
